```python
import jax, jax.numpy as jnp
from jax import lax
import numpy as np

D_MODEL = 1024
BATCH = 32
SEQ = 2048
DEPTH = 4

CHUNK = 64
Q_BLOCK = 128
PLE_DIM = 256
DEEPNORM_ALPHA = (2 * DEPTH) ** 0.25
DEEPNORM_BETA = (8 * DEPTH) ** -0.25
LN_EPS = 1e-5
RMS_EPS = 1e-6

MIX_WIDTH = D_MODEL
HEAD_DIM = 64
POOL_WIDTH = MIX_WIDTH // 4
POOL_GROUPS = 4
POOL_GROUP_DIM = POOL_WIDTH // POOL_GROUPS
POOL_WINDOWS = (2, 4, 8, 16)
SSD_WIDTH = 3 * MIX_WIDTH // 8
SSD_HEADS = SSD_WIDTH // HEAD_DIM
SSD_GROUPS = 2
SSD_STATE = 128
SSD_CONV = 4
SSD_XBC_WIDTH = SSD_WIDTH + 2 * SSD_GROUPS * SSD_STATE
SB_WIDTH = MIX_WIDTH - POOL_WIDTH - SSD_WIDTH
SB_HEADS = SB_WIDTH // HEAD_DIM

COL_POOL = 0
COL_Z = COL_POOL + POOL_WIDTH
COL_XBC = COL_Z + SSD_WIDTH
COL_DT = COL_XBC + SSD_XBC_WIDTH
COL_Q = COL_DT + SSD_HEADS
COL_K = COL_Q + SB_WIDTH
COL_V = COL_K + SB_WIDTH
IN_COLS = COL_V + SB_WIDTH

D_FF = ((8 * D_MODEL // 3 + 255) // 256) * 256
FFN_CONV = 3

kernel_name = 'hybrid_pool_ssd_stickbreak_deepnorm_trunk'


def layer_norm(x, g, b):
    xf = x.astype(jnp.float32)
    mu = jnp.mean(xf, axis=-1, keepdims=True)
    var = jnp.mean(jnp.square(xf - mu), axis=-1, keepdims=True)
    y = (xf - mu) * lax.rsqrt(var + LN_EPS)
    return (y * g + b).astype(x.dtype)


def causal_depthwise_conv(x, w, b):
    k = w.shape[0]
    s_ = x.shape[1]
    xp = jnp.pad(x, ((0, 0), (k - 1, 0), (0, 0)))
    out = b
    for j in range(k):
        out = out + xp[:, j:j + s_] * w[j]
    return out


def multiscale_pool(u, w_pool, scale):
    b_, s_, _ = u.shape
    ug = u.reshape(b_, s_, POOL_GROUPS, POOL_GROUP_DIM).astype(jnp.float32)
    cs = jnp.concatenate([jnp.zeros_like(ug[:, :1]), jnp.cumsum(ug, axis=1)], axis=1)
    t = jnp.arange(s_)[:, None]
    win = jnp.array(POOL_WINDOWS, dtype=jnp.int32)[None, :]
    lo = jnp.maximum(t + 1 - win, 0)
    lower = cs[:, lo, jnp.arange(POOL_GROUPS)[None, :]]
    count = (t + 1 - lo).astype(jnp.float32)
    pooled = (cs[:, 1:] - lower) / count[..., None] - ug
    mixed = jnp.einsum('bsgc,gcd->bsgd', pooled.astype(u.dtype), w_pool)
    return mixed.reshape(b_, s_, POOL_WIDTH) * scale


def ssd_chunked_scan(xs, dt, a, bm, cm):
    b_, s_, h_, p_ = xs.shape
    c_ = s_ // CHUNK
    r_ = h_ // SSD_GROUPS
    xdt = (xs.astype(jnp.float32) * dt[..., None]).reshape(b_, c_, CHUNK, SSD_GROUPS, r_, p_)
    adt = (dt * a).reshape(b_, c_, CHUNK, SSD_GROUPS, r_)
    a_cum = jnp.cumsum(jnp.moveaxis(adt, 2, -1), axis=-1)
    bc = bm.astype(jnp.float32).reshape(b_, c_, CHUNK, SSD_GROUPS, SSD_STATE)
    cc = cm.astype(jnp.float32).reshape(b_, c_, CHUNK, SSD_GROUPS, SSD_STATE)
    causal = jnp.tril(jnp.ones((CHUNK, CHUNK), dtype=bool))
    seg = a_cum[..., :, None] - a_cum[..., None, :]
    decay = jnp.where(causal, jnp.exp(jnp.where(causal, seg, 0.0)), 0.0)
    cb = jnp.einsum('bclgn,bcsgn->bcgls', cc, bc)
    y_diag = jnp.einsum('bcgrls,bcsgrp->bclgrp', cb[:, :, :, None] * decay, xdt)
    decay_to_end = jnp.exp(a_cum[..., -1:] - a_cum)
    states = jnp.einsum('bclgn,bcgrl,bclgrp->bcgrpn', bc, decay_to_end, xdt)
    chunk_decay = jnp.exp(a_cum[..., -1])

    def carry_state(h, inp):
        st, dec = inp
        return h * dec[..., None, None] + st, h

    h0 = jnp.zeros_like(states[:, 0])
    _, h_prev = lax.scan(carry_state, h0, (jnp.moveaxis(states, 1, 0), jnp.moveaxis(chunk_decay, 1, 0)))
    h_prev = jnp.moveaxis(h_prev, 0, 1)
    y_off = jnp.einsum('bclgn,bcgrpn,bcgrl->bclgrp', cc, h_prev, jnp.exp(a_cum))
    return (y_diag + y_off).reshape(b_, s_, h_, p_)


def ssd_mixer(z, xbc, dt_raw, conv_w, conv_b, dt_bias, a_log, d_skip, norm_w):
    b_, s_, _ = xbc.shape
    xbc = jax.nn.silu(causal_depthwise_conv(xbc, conv_w, conv_b))
    xs = xbc[..., :SSD_WIDTH].reshape(b_, s_, SSD_HEADS, HEAD_DIM)
    bm = xbc[..., SSD_WIDTH:SSD_WIDTH + SSD_GROUPS * SSD_STATE].reshape(b_, s_, SSD_GROUPS, SSD_STATE)
    cm = xbc[..., SSD_WIDTH + SSD_GROUPS * SSD_STATE:].reshape(b_, s_, SSD_GROUPS, SSD_STATE)
    dt = jax.nn.softplus((dt_raw + dt_bias).astype(jnp.float32))
    a = -jnp.exp(a_log.astype(jnp.float32))
    y = ssd_chunked_scan(xs, dt, a, bm, cm) + xs.astype(jnp.float32) * d_skip[:, None]
    hg = (y.reshape(b_, s_, SSD_WIDTH) * jax.nn.silu(z.astype(jnp.float32)))
    hg = hg.reshape(b_, s_, SSD_GROUPS, SSD_WIDTH // SSD_GROUPS)
    hg = hg * lax.rsqrt(jnp.mean(jnp.square(hg), axis=-1, keepdims=True) + RMS_EPS)
    return (hg.reshape(b_, s_, SSD_WIDTH) * norm_w).astype(xbc.dtype)


def stick_breaking_attention(q, k, v):
    b_, s_, h_, d_ = q.shape
    scale = d_ ** -0.5
    outs = []
    for start in range(0, s_, Q_BLOCK):
        end = start + Q_BLOCK
        z = jnp.einsum('bqhd,bkhd->bhqk', q[:, start:end], k[:, :end]).astype(jnp.float32) * scale
        t_idx = start + jnp.arange(Q_BLOCK)[:, None]
        s_idx = jnp.arange(end)[None, :]
        strict = s_idx < t_idx
        log_not = jnp.where(strict, jax.nn.log_sigmoid(-z), 0.0)
        tail = lax.cumsum(log_not, axis=3, reverse=True) - log_not
        weights = jnp.where(strict, jnp.exp(jax.nn.log_sigmoid(z) + tail), 0.0)
        outs.append(jnp.einsum('bhqk,bkhd->bqhd', weights.astype(v.dtype), v[:, :end]))
    return jnp.concatenate(outs, axis=1)


def conv_glu_ffn(x, w_up, conv_w, conv_b, w_down):
    up = causal_depthwise_conv(x @ w_up, conv_w, conv_b)
    gate, val = up[..., :D_FF], up[..., D_FF:]
    return (jax.nn.silu(gate) * val) @ w_down


def _fwd_setup_inputs(seed: int = 0) -> dict:
    key = jax.random.key(seed)
    ks = jax.random.split(key, 26)

    def nrm(k, shape, scale):
        return jax.random.normal(k, shape, jnp.float32) * scale

    dt0 = jnp.exp(jax.random.uniform(ks[6], (DEPTH, SSD_HEADS), jnp.float32, np.log(1e-3), np.log(1e-1)))
    return {
        'x': nrm(ks[0], (BATCH, SEQ, D_MODEL), 1.0),
        'p': nrm(ks[1], (DEPTH, BATCH, SEQ, PLE_DIM), 1.0),
        'w_in': nrm(ks[2], (DEPTH, D_MODEL, IN_COLS), D_MODEL ** -0.5),
        'pool_w': nrm(ks[3], (DEPTH, POOL_GROUPS, POOL_GROUP_DIM, POOL_GROUP_DIM), POOL_GROUP_DIM ** -0.5),
        'pool_scale': 1.0 + nrm(ks[4], (DEPTH, POOL_WIDTH), 0.1),
        'ssd_conv_w': nrm(ks[5], (DEPTH, SSD_CONV, SSD_XBC_WIDTH), SSD_CONV ** -0.5),
        'ssd_conv_b': nrm(ks[7], (DEPTH, SSD_XBC_WIDTH), 0.01),
        'ssd_dt_bias': dt0 + jnp.log(-jnp.expm1(-dt0)),
        'ssd_a_log': jnp.log(jax.random.uniform(ks[8], (DEPTH, SSD_HEADS), jnp.float32, 1.0, 16.0)),
        'ssd_d': 1.0 + nrm(ks[9], (DEPTH, SSD_HEADS), 0.1),
        'ssd_norm_w': 1.0 + nrm(ks[10], (DEPTH, SSD_WIDTH), 0.05),
        'w_out': nrm(ks[11], (DEPTH, MIX_WIDTH, D_MODEL), MIX_WIDTH ** -0.5 * DEEPNORM_BETA),
        'ln1_g': 1.0 + nrm(ks[12], (DEPTH, D_MODEL), 0.05),
        'ln1_b': nrm(ks[13], (DEPTH, D_MODEL), 0.01),
        'ffn_w_up': nrm(ks[14], (DEPTH, D_MODEL, 2 * D_FF), D_MODEL ** -0.5),
        'ffn_conv_w': nrm(ks[15], (DEPTH, FFN_CONV, 2 * D_FF), FFN_CONV ** -0.5),
        'ffn_conv_b': nrm(ks[16], (DEPTH, 2 * D_FF), 0.01),
        'ffn_w_down': nrm(ks[17], (DEPTH, D_FF, D_MODEL), D_FF ** -0.5 * DEEPNORM_BETA),
        'ln2_g': 1.0 + nrm(ks[18], (DEPTH, D_MODEL), 0.05),
        'ln2_b': nrm(ks[19], (DEPTH, D_MODEL), 0.01),
        'ple_w_gate': nrm(ks[20], (DEPTH, D_MODEL, D_MODEL), D_MODEL ** -0.5),
        'ple_w_proj': nrm(ks[21], (DEPTH, PLE_DIM, D_MODEL), PLE_DIM ** -0.5 * DEEPNORM_BETA),
    }


def _fwd_reference(x, p, w_in, pool_w, pool_scale, ssd_conv_w, ssd_conv_b, ssd_dt_bias, ssd_a_log, ssd_d,
              ssd_norm_w, w_out, ln1_g, ln1_b, ffn_w_up, ffn_conv_w, ffn_conv_b, ffn_w_down, ln2_g, ln2_b,
              ple_w_gate, ple_w_proj):
    b_, s_, _ = x.shape
    for i in range(DEPTH):
        h = x @ w_in[i]
        pool_out = multiscale_pool(h[..., COL_POOL:COL_Z], pool_w[i], pool_scale[i])
        ssd_out = ssd_mixer(h[..., COL_Z:COL_XBC], h[..., COL_XBC:COL_DT], h[..., COL_DT:COL_Q],
                            ssd_conv_w[i], ssd_conv_b[i], ssd_dt_bias[i], ssd_a_log[i], ssd_d[i], ssd_norm_w[i])
        q = h[..., COL_Q:COL_K].reshape(b_, s_, SB_HEADS, HEAD_DIM)
        k = h[..., COL_K:COL_V].reshape(b_, s_, SB_HEADS, HEAD_DIM)
        v = h[..., COL_V:IN_COLS].reshape(b_, s_, SB_HEADS, HEAD_DIM)
        sb_out = stick_breaking_attention(q, k, v).reshape(b_, s_, SB_WIDTH)
        mix = jnp.concatenate([pool_out.astype(x.dtype), ssd_out.astype(x.dtype), sb_out.astype(x.dtype)], axis=-1) @ w_out[i]
        x = layer_norm(DEEPNORM_ALPHA * x + mix, ln1_g[i], ln1_b[i])
        ffn = conv_glu_ffn(x, ffn_w_up[i], ffn_conv_w[i], ffn_conv_b[i], ffn_w_down[i])
        ple = jax.nn.sigmoid(x @ ple_w_gate[i]) * (p[i] @ ple_w_proj[i])
        x = layer_norm(DEEPNORM_ALPHA * x + ffn + ple, ln2_g[i], ln2_b[i])
    return x


import jax as _jax
import jax.numpy as _jnp

TWIN_FORMAT = 'train_step'
FWD_PARAMS = ['x', 'p', 'w_in', 'pool_w', 'pool_scale', 'ssd_conv_w', 'ssd_conv_b', 'ssd_dt_bias', 'ssd_a_log', 'ssd_d', 'ssd_norm_w', 'w_out', 'ln1_g', 'ln1_b', 'ffn_w_up', 'ffn_conv_w', 'ffn_conv_b', 'ffn_w_down', 'ln2_g', 'ln2_b', 'ple_w_gate', 'ple_w_proj']
TWIN_WEIGHTS = ['w_in', 'pool_w', 'pool_scale', 'ssd_conv_w', 'ssd_conv_b', 'ssd_dt_bias', 'ssd_a_log', 'ssd_d', 'ssd_norm_w', 'w_out', 'ln1_g', 'ln1_b', 'ffn_w_up', 'ffn_conv_w', 'ffn_conv_b', 'ffn_w_down', 'ln2_g', 'ln2_b', 'ple_w_gate', 'ple_w_proj']
TWIN_DIFF_INPUT = 'x'
TWIN_INPUTS = ['x', 'p', 'w_in', 'pool_w', 'pool_scale', 'ssd_conv_w', 'ssd_conv_b', 'ssd_dt_bias', 'ssd_a_log', 'ssd_d', 'ssd_norm_w', 'w_out', 'ln1_g', 'ln1_b', 'ffn_w_up', 'ffn_conv_w', 'ffn_conv_b', 'ffn_w_down', 'ln2_g', 'ln2_b', 'ple_w_gate', 'ple_w_proj', 'loss_target', 'm_w_in', 'm_pool_w', 'm_pool_scale', 'm_ssd_conv_w', 'm_ssd_conv_b', 'm_ssd_dt_bias', 'm_ssd_a_log', 'm_ssd_d', 'm_ssd_norm_w', 'm_w_out', 'm_ln1_g', 'm_ln1_b', 'm_ffn_w_up', 'm_ffn_conv_w', 'm_ffn_conv_b', 'm_ffn_w_down', 'm_ln2_g', 'm_ln2_b', 'm_ple_w_gate', 'm_ple_w_proj', 'v_w_in', 'v_pool_w', 'v_pool_scale', 'v_ssd_conv_w', 'v_ssd_conv_b', 'v_ssd_dt_bias', 'v_ssd_a_log', 'v_ssd_d', 'v_ssd_norm_w', 'v_w_out', 'v_ln1_g', 'v_ln1_b', 'v_ffn_w_up', 'v_ffn_conv_w', 'v_ffn_conv_b', 'v_ffn_w_down', 'v_ln2_g', 'v_ln2_b', 'v_ple_w_gate', 'v_ple_w_proj']
TWIN_OUTPUTS = ['loss', 'grad_x', 'grad_w_in', 'grad_pool_w', 'grad_pool_scale', 'grad_ssd_conv_w', 'grad_ssd_conv_b', 'grad_ssd_dt_bias', 'grad_ssd_a_log', 'grad_ssd_d', 'grad_ssd_norm_w', 'grad_w_out', 'grad_ln1_g', 'grad_ln1_b', 'grad_ffn_w_up', 'grad_ffn_conv_w', 'grad_ffn_conv_b', 'grad_ffn_w_down', 'grad_ln2_g', 'grad_ln2_b', 'grad_ple_w_gate', 'grad_ple_w_proj', 'delta_w_in', 'delta_pool_w', 'delta_pool_scale', 'delta_ssd_conv_w', 'delta_ssd_conv_b', 'delta_ssd_dt_bias', 'delta_ssd_a_log', 'delta_ssd_d', 'delta_ssd_norm_w', 'delta_w_out', 'delta_ln1_g', 'delta_ln1_b', 'delta_ffn_w_up', 'delta_ffn_conv_w', 'delta_ffn_conv_b', 'delta_ffn_w_down', 'delta_ln2_g', 'delta_ln2_b', 'delta_ple_w_gate', 'delta_ple_w_proj', 'new_m_w_in', 'new_m_pool_w', 'new_m_pool_scale', 'new_m_ssd_conv_w', 'new_m_ssd_conv_b', 'new_m_ssd_dt_bias', 'new_m_ssd_a_log', 'new_m_ssd_d', 'new_m_ssd_norm_w', 'new_m_w_out', 'new_m_ln1_g', 'new_m_ln1_b', 'new_m_ffn_w_up', 'new_m_ffn_conv_w', 'new_m_ffn_conv_b', 'new_m_ffn_w_down', 'new_m_ln2_g', 'new_m_ln2_b', 'new_m_ple_w_gate', 'new_m_ple_w_proj', 'new_v_w_in', 'new_v_pool_w', 'new_v_pool_scale', 'new_v_ssd_conv_w', 'new_v_ssd_conv_b', 'new_v_ssd_dt_bias', 'new_v_ssd_a_log', 'new_v_ssd_d', 'new_v_ssd_norm_w', 'new_v_w_out', 'new_v_ln1_g', 'new_v_ln1_b', 'new_v_ffn_w_up', 'new_v_ffn_conv_w', 'new_v_ffn_conv_b', 'new_v_ffn_w_down', 'new_v_ln2_g', 'new_v_ln2_b', 'new_v_ple_w_gate', 'new_v_ple_w_proj']
TWIN_LEAF_KINDS = {'loss': 'loss', 'grad_x': 'grad_x', 'grad_w_in': 'grad_w', 'grad_pool_w': 'grad_w', 'grad_pool_scale': 'grad_w', 'grad_ssd_conv_w': 'grad_w', 'grad_ssd_conv_b': 'grad_w', 'grad_ssd_dt_bias': 'grad_w', 'grad_ssd_a_log': 'grad_w', 'grad_ssd_d': 'grad_w', 'grad_ssd_norm_w': 'grad_w', 'grad_w_out': 'grad_w', 'grad_ln1_g': 'grad_w', 'grad_ln1_b': 'grad_w', 'grad_ffn_w_up': 'grad_w', 'grad_ffn_conv_w': 'grad_w', 'grad_ffn_conv_b': 'grad_w', 'grad_ffn_w_down': 'grad_w', 'grad_ln2_g': 'grad_w', 'grad_ln2_b': 'grad_w', 'grad_ple_w_gate': 'grad_w', 'grad_ple_w_proj': 'grad_w', 'delta_w_in': 'delta_w', 'delta_pool_w': 'delta_w', 'delta_pool_scale': 'delta_w', 'delta_ssd_conv_w': 'delta_w', 'delta_ssd_conv_b': 'delta_w', 'delta_ssd_dt_bias': 'delta_w', 'delta_ssd_a_log': 'delta_w', 'delta_ssd_d': 'delta_w', 'delta_ssd_norm_w': 'delta_w', 'delta_w_out': 'delta_w', 'delta_ln1_g': 'delta_w', 'delta_ln1_b': 'delta_w', 'delta_ffn_w_up': 'delta_w', 'delta_ffn_conv_w': 'delta_w', 'delta_ffn_conv_b': 'delta_w', 'delta_ffn_w_down': 'delta_w', 'delta_ln2_g': 'delta_w', 'delta_ln2_b': 'delta_w', 'delta_ple_w_gate': 'delta_w', 'delta_ple_w_proj': 'delta_w', 'new_m_w_in': 'new_m', 'new_m_pool_w': 'new_m', 'new_m_pool_scale': 'new_m', 'new_m_ssd_conv_w': 'new_m', 'new_m_ssd_conv_b': 'new_m', 'new_m_ssd_dt_bias': 'new_m', 'new_m_ssd_a_log': 'new_m', 'new_m_ssd_d': 'new_m', 'new_m_ssd_norm_w': 'new_m', 'new_m_w_out': 'new_m', 'new_m_ln1_g': 'new_m', 'new_m_ln1_b': 'new_m', 'new_m_ffn_w_up': 'new_m', 'new_m_ffn_conv_w': 'new_m', 'new_m_ffn_conv_b': 'new_m', 'new_m_ffn_w_down': 'new_m', 'new_m_ln2_g': 'new_m', 'new_m_ln2_b': 'new_m', 'new_m_ple_w_gate': 'new_m', 'new_m_ple_w_proj': 'new_m', 'new_v_w_in': 'new_v', 'new_v_pool_w': 'new_v', 'new_v_pool_scale': 'new_v', 'new_v_ssd_conv_w': 'new_v', 'new_v_ssd_conv_b': 'new_v', 'new_v_ssd_dt_bias': 'new_v', 'new_v_ssd_a_log': 'new_v', 'new_v_ssd_d': 'new_v', 'new_v_ssd_norm_w': 'new_v', 'new_v_w_out': 'new_v', 'new_v_ln1_g': 'new_v', 'new_v_ln1_b': 'new_v', 'new_v_ffn_w_up': 'new_v', 'new_v_ffn_conv_w': 'new_v', 'new_v_ffn_conv_b': 'new_v', 'new_v_ffn_w_down': 'new_v', 'new_v_ln2_g': 'new_v', 'new_v_ln2_b': 'new_v', 'new_v_ple_w_gate': 'new_v', 'new_v_ple_w_proj': 'new_v'}


def _forward(args):
    return _fwd_reference(*[args[k] for k in FWD_PARAMS])


def _output_shape():
    out = _jax.eval_shape(lambda: _forward(_fwd_setup_inputs(0)))
    return out.shape, out.dtype

N_MICROBATCH = 1
ADAM_LR = 0.001
ADAM_B1 = 0.9
ADAM_B2 = 0.999
ADAM_EPS = 1e-08
ADAM_WD = 0.01
ADAM_STEP = 10
PER_EXAMPLE_BATCH_AXIS = {'x': 0, 'p': 1, 'loss_target': 0}
SHARED_INPUTS = []
_WEIGHT_DTYPES = {'w_in': _jnp.float32, 'pool_w': _jnp.float32, 'pool_scale': _jnp.float32, 'ssd_conv_w': _jnp.float32, 'ssd_conv_b': _jnp.float32, 'ssd_dt_bias': _jnp.float32, 'ssd_a_log': _jnp.float32, 'ssd_d': _jnp.float32, 'ssd_norm_w': _jnp.float32, 'w_out': _jnp.float32, 'ln1_g': _jnp.float32, 'ln1_b': _jnp.float32, 'ffn_w_up': _jnp.float32, 'ffn_conv_w': _jnp.float32, 'ffn_conv_b': _jnp.float32, 'ffn_w_down': _jnp.float32, 'ln2_g': _jnp.float32, 'ln2_b': _jnp.float32, 'ple_w_gate': _jnp.float32, 'ple_w_proj': _jnp.float32}
MOMENT_SCALE = {'w_in': 4.445444e-02, 'pool_w': 5.830937e-02, 'pool_scale': 5.546918e-02, 'ssd_conv_w': 4.507872e-02, 'ssd_conv_b': 6.057027e-02, 'ssd_dt_bias': 1.005586e-01, 'ssd_a_log': 1.827722e-01, 'ssd_d': 3.105772e-01, 'ssd_norm_w': 6.438180e-02, 'w_out': 1.323670e-01, 'ln1_g': 5.407645e+00, 'ln1_b': 4.185933e-01, 'ffn_w_up': 2.366411e-02, 'ffn_conv_w': 2.389391e-02, 'ffn_conv_b': 2.438311e-02, 'ffn_w_down': 9.224038e-02, 'ln2_g': 3.304931e+01, 'ln2_b': 1.448209e+00, 'ple_w_gate': 1.343347e-02, 'ple_w_proj': 8.206741e-02}


def _to_microbatches(a, axis):
    t = _jnp.moveaxis(a, axis, 0)
    t = t.reshape((N_MICROBATCH, t.shape[0] // N_MICROBATCH) + t.shape[1:])
    return _jnp.moveaxis(t, 1, axis + 1)


def setup_inputs(seed: int = 0) -> dict:
    inp = _fwd_setup_inputs(seed)
    key = _jax.random.fold_in(_jax.random.key(seed), 7919)
    shape, _ = _output_shape()
    out = dict(inp)
    out["loss_target"] = _jax.random.normal(_jax.random.fold_in(key, 0), shape, _jnp.float32)
    for i, name in enumerate(TWIN_WEIGHTS):
        w = inp[name].astype(_jnp.float32)
        if MOMENT_SCALE is None:
            s = _jnp.sqrt(_jnp.mean(_jnp.square(w)) + 1e-30)
        else:
            s = MOMENT_SCALE[name]
        km, kv = _jax.random.split(_jax.random.fold_in(key, i + 1))
        out[name] = w
        out["m_" + name] = s * _jax.random.normal(km, w.shape, _jnp.float32)
        out["v_" + name] = (s * s) * _jax.random.uniform(kv, w.shape, _jnp.float32, 0.5, 1.5)
    if N_MICROBATCH > 1:
        for name, axis in PER_EXAMPLE_BATCH_AXIS.items():
            out[name] = _to_microbatches(out[name], axis)
    return {'x': out['x'], 'p': out['p'], 'w_in': out['w_in'], 'pool_w': out['pool_w'], 'pool_scale': out['pool_scale'], 'ssd_conv_w': out['ssd_conv_w'], 'ssd_conv_b': out['ssd_conv_b'], 'ssd_dt_bias': out['ssd_dt_bias'], 'ssd_a_log': out['ssd_a_log'], 'ssd_d': out['ssd_d'], 'ssd_norm_w': out['ssd_norm_w'], 'w_out': out['w_out'], 'ln1_g': out['ln1_g'], 'ln1_b': out['ln1_b'], 'ffn_w_up': out['ffn_w_up'], 'ffn_conv_w': out['ffn_conv_w'], 'ffn_conv_b': out['ffn_conv_b'], 'ffn_w_down': out['ffn_w_down'], 'ln2_g': out['ln2_g'], 'ln2_b': out['ln2_b'], 'ple_w_gate': out['ple_w_gate'], 'ple_w_proj': out['ple_w_proj'], 'loss_target': out['loss_target'], 'm_w_in': out['m_w_in'], 'm_pool_w': out['m_pool_w'], 'm_pool_scale': out['m_pool_scale'], 'm_ssd_conv_w': out['m_ssd_conv_w'], 'm_ssd_conv_b': out['m_ssd_conv_b'], 'm_ssd_dt_bias': out['m_ssd_dt_bias'], 'm_ssd_a_log': out['m_ssd_a_log'], 'm_ssd_d': out['m_ssd_d'], 'm_ssd_norm_w': out['m_ssd_norm_w'], 'm_w_out': out['m_w_out'], 'm_ln1_g': out['m_ln1_g'], 'm_ln1_b': out['m_ln1_b'], 'm_ffn_w_up': out['m_ffn_w_up'], 'm_ffn_conv_w': out['m_ffn_conv_w'], 'm_ffn_conv_b': out['m_ffn_conv_b'], 'm_ffn_w_down': out['m_ffn_w_down'], 'm_ln2_g': out['m_ln2_g'], 'm_ln2_b': out['m_ln2_b'], 'm_ple_w_gate': out['m_ple_w_gate'], 'm_ple_w_proj': out['m_ple_w_proj'], 'v_w_in': out['v_w_in'], 'v_pool_w': out['v_pool_w'], 'v_pool_scale': out['v_pool_scale'], 'v_ssd_conv_w': out['v_ssd_conv_w'], 'v_ssd_conv_b': out['v_ssd_conv_b'], 'v_ssd_dt_bias': out['v_ssd_dt_bias'], 'v_ssd_a_log': out['v_ssd_a_log'], 'v_ssd_d': out['v_ssd_d'], 'v_ssd_norm_w': out['v_ssd_norm_w'], 'v_w_out': out['v_w_out'], 'v_ln1_g': out['v_ln1_g'], 'v_ln1_b': out['v_ln1_b'], 'v_ffn_w_up': out['v_ffn_w_up'], 'v_ffn_conv_w': out['v_ffn_conv_w'], 'v_ffn_conv_b': out['v_ffn_conv_b'], 'v_ffn_w_down': out['v_ffn_w_down'], 'v_ln2_g': out['v_ln2_g'], 'v_ln2_b': out['v_ln2_b'], 'v_ple_w_gate': out['v_ple_w_gate'], 'v_ple_w_proj': out['v_ple_w_proj']}


def _loss(weights, diff, rest, loss_target):
    with _jax.named_scope("forward"):
        args = {**rest, TWIN_DIFF_INPUT: diff, **{k: w.astype(_WEIGHT_DTYPES[k]) for k, w in weights.items()}}
        y = _forward(args)
    with _jax.named_scope("loss_head"):
        err = _jnp.square(y.astype(_jnp.float32) - loss_target)
        return 0.5 * _jnp.sum(_jnp.mean(err, axis=-1)) if err.ndim else 0.5 * err


def _adamw(w, g, m, v):
    m = ADAM_B1 * m + (1.0 - ADAM_B1) * g
    v = ADAM_B2 * v + (1.0 - ADAM_B2) * _jnp.square(g)
    m_hat = m / (1.0 - ADAM_B1 ** ADAM_STEP)
    v_hat = v / (1.0 - ADAM_B2 ** ADAM_STEP)
    delta = -ADAM_LR * (m_hat / (_jnp.sqrt(v_hat) + ADAM_EPS) + ADAM_WD * w)
    return delta, m, v


def reference(x, p, w_in, pool_w, pool_scale, ssd_conv_w, ssd_conv_b, ssd_dt_bias, ssd_a_log, ssd_d, ssd_norm_w, w_out, ln1_g, ln1_b, ffn_w_up, ffn_conv_w, ffn_conv_b, ffn_w_down, ln2_g, ln2_b, ple_w_gate, ple_w_proj, loss_target, m_w_in, m_pool_w, m_pool_scale, m_ssd_conv_w, m_ssd_conv_b, m_ssd_dt_bias, m_ssd_a_log, m_ssd_d, m_ssd_norm_w, m_w_out, m_ln1_g, m_ln1_b, m_ffn_w_up, m_ffn_conv_w, m_ffn_conv_b, m_ffn_w_down, m_ln2_g, m_ln2_b, m_ple_w_gate, m_ple_w_proj, v_w_in, v_pool_w, v_pool_scale, v_ssd_conv_w, v_ssd_conv_b, v_ssd_dt_bias, v_ssd_a_log, v_ssd_d, v_ssd_norm_w, v_w_out, v_ln1_g, v_ln1_b, v_ffn_w_up, v_ffn_conv_w, v_ffn_conv_b, v_ffn_w_down, v_ln2_g, v_ln2_b, v_ple_w_gate, v_ple_w_proj):
    given = dict(x=x, p=p, w_in=w_in, pool_w=pool_w, pool_scale=pool_scale, ssd_conv_w=ssd_conv_w, ssd_conv_b=ssd_conv_b, ssd_dt_bias=ssd_dt_bias, ssd_a_log=ssd_a_log, ssd_d=ssd_d, ssd_norm_w=ssd_norm_w, w_out=w_out, ln1_g=ln1_g, ln1_b=ln1_b, ffn_w_up=ffn_w_up, ffn_conv_w=ffn_conv_w, ffn_conv_b=ffn_conv_b, ffn_w_down=ffn_w_down, ln2_g=ln2_g, ln2_b=ln2_b, ple_w_gate=ple_w_gate, ple_w_proj=ple_w_proj, loss_target=loss_target, m_w_in=m_w_in, m_pool_w=m_pool_w, m_pool_scale=m_pool_scale, m_ssd_conv_w=m_ssd_conv_w, m_ssd_conv_b=m_ssd_conv_b, m_ssd_dt_bias=m_ssd_dt_bias, m_ssd_a_log=m_ssd_a_log, m_ssd_d=m_ssd_d, m_ssd_norm_w=m_ssd_norm_w, m_w_out=m_w_out, m_ln1_g=m_ln1_g, m_ln1_b=m_ln1_b, m_ffn_w_up=m_ffn_w_up, m_ffn_conv_w=m_ffn_conv_w, m_ffn_conv_b=m_ffn_conv_b, m_ffn_w_down=m_ffn_w_down, m_ln2_g=m_ln2_g, m_ln2_b=m_ln2_b, m_ple_w_gate=m_ple_w_gate, m_ple_w_proj=m_ple_w_proj, v_w_in=v_w_in, v_pool_w=v_pool_w, v_pool_scale=v_pool_scale, v_ssd_conv_w=v_ssd_conv_w, v_ssd_conv_b=v_ssd_conv_b, v_ssd_dt_bias=v_ssd_dt_bias, v_ssd_a_log=v_ssd_a_log, v_ssd_d=v_ssd_d, v_ssd_norm_w=v_ssd_norm_w, v_w_out=v_w_out, v_ln1_g=v_ln1_g, v_ln1_b=v_ln1_b, v_ffn_w_up=v_ffn_w_up, v_ffn_conv_w=v_ffn_conv_w, v_ffn_conv_b=v_ffn_conv_b, v_ffn_w_down=v_ffn_w_down, v_ln2_g=v_ln2_g, v_ln2_b=v_ln2_b, v_ple_w_gate=v_ple_w_gate, v_ple_w_proj=v_ple_w_proj)
    weights = {n: given[n] for n in TWIN_WEIGHTS}
    shared = {n: given[n] for n in SHARED_INPUTS}
    per_example = {n: given[n] for n in ['x', 'p']}
    grad_fn = _jax.value_and_grad(_loss, argnums=(0, 1))

    def one_microbatch(ex, loss_target):
        ex = dict(ex)
        diff = ex.pop(TWIN_DIFF_INPUT)
        return grad_fn(weights, diff, {**shared, **ex}, loss_target)

    if N_MICROBATCH == 1:
        loss, (grad_w, grad_x) = one_microbatch(per_example, given["loss_target"])
    else:
        def body(carry, xs):
            loss_sum, grad_sum = carry
            l_k, (gw_k, gx_k) = one_microbatch(xs[0], xs[1])
            with _jax.named_scope("update"):
                return (loss_sum + l_k, _jax.tree.map(_jnp.add, grad_sum, gw_k)), gx_k

        init = (_jnp.zeros((), _jnp.float32), _jax.tree.map(_jnp.zeros_like, weights))
        (loss, grad_w), grad_x = _jax.lax.scan(body, init, (per_example, given["loss_target"]))
    with _jax.named_scope("update"):
        delta_w, new_m, new_v = {}, {}, {}
        for n in TWIN_WEIGHTS:
            delta_w[n], new_m[n], new_v[n] = _adamw(weights[n], grad_w[n], given["m_" + n], given["v_" + n])
    return (loss, grad_x, *[grad_w[n] for n in TWIN_WEIGHTS], *[delta_w[n] for n in TWIN_WEIGHTS],
            *[new_m[n] for n in TWIN_WEIGHTS], *[new_v[n] for n in TWIN_WEIGHTS])
```

```python
import functools

import jax
import jax.numpy as jnp
from jax import lax
from jax.experimental import pallas as pl
from jax.experimental.pallas import tpu as pltpu

F32 = jnp.float32
BF16 = jnp.bfloat16
MXU_DTYPE = jnp.bfloat16

D_MODEL = 1024
DEPTH = 4
PLE_DIM = 256
ALPHA = (2 * DEPTH) ** 0.25
LN_EPS = 1e-5
RMS_EPS = 1e-6
HEAD_DIM = 64
POOL_WIDTH = 256
POOL_WINDOWS = (2, 4, 8, 16)
SSD_WIDTH = 384
SSD_HEADS = 6
SSD_STATE = 128
SSD_XBC = 896
SB_WIDTH = 384
IN_COLS = 2694
D_FF = 2816
N_DEV = 8

ADAM_LR = 0.001
ADAM_B1 = 0.9
ADAM_B2 = 0.999
ADAM_EPS = 1e-08
ADAM_WD = 0.01
ADAM_STEP = 10

LANES = 128
SUBLANES = 8
VMEM_LIMIT_BYTES = 56 * 1024 * 1024

H_COLS = 2816
H_BC = 0
H_POOL = 512
H_Q = 768
H_K = 1152
H_V = 1536
H_Z = 1920
H_XS = 2304
H_DT = 2688
SSD_CHUNK = 128
QB = 128
GLU_TILE = 256

NN = ((1,), (0,))
NT = ((1,), (1,))
TN = ((0,), (0,))


def _dot(a, b, dims=NN):
    return lax.dot_general(a.astype(MXU_DTYPE), b.astype(MXU_DTYPE), (dims, ((), ())), preferred_element_type=F32)


def _dot_exact01(x, m01, dims=NN, x_left=True, terms=3):
    acc = None
    r = x
    for _ in range(terms):
        hi = r.astype(BF16)
        ops = (hi, m01) if x_left else (m01, hi)
        part = lax.dot_general(ops[0], ops[1], (dims, ((), ())), preferred_element_type=F32)
        acc = part if acc is None else acc + part
        r = r - hi.astype(F32)
    return acc


def _sigmoid(v):
    return 1.0 / (1.0 + jnp.exp(-v))


def _silu(v):
    return v * _sigmoid(v)


def _dsilu(v):
    s = _sigmoid(v)
    return s * (1.0 + v * (1.0 - s))


def _softplus(v):
    return jnp.maximum(v, 0.0) + jnp.log(1.0 + jnp.exp(-jnp.abs(v)))


def _params(n_axes):
    return pltpu.CompilerParams(dimension_semantics=("arbitrary",) * n_axes, vmem_limit_bytes=VMEM_LIMIT_BYTES)


def _pick(n, pref):
    if n <= pref:
        return n
    for t in range(pref - pref % LANES, 0, -LANES):
        if n % t == 0:
            return t
    raise ValueError((n, pref))


def _mm(a, b, mode, out_dtype, name, tm=512, tn=512, tk=1024, add=None, add_coef=1.0):
    if mode == "nn":
        (m, k), (k2, n) = a.shape, b.shape
    elif mode == "nt":
        (m, k), (n, k2) = a.shape, b.shape
    else:
        (k, m), (k2, n) = a.shape, b.shape
    assert k == k2, (a.shape, b.shape, mode)
    tm, tn, tk = _pick(m, tm), _pick(n, tn), _pick(k, tk)
    nk = k // tk
    dims = {"nn": NN, "nt": NT, "tn": TN}[mode]

    def body(*refs):
        if add is None:
            a_ref, b_ref, o_ref, acc_ref = refs
            add_ref = None
        else:
            a_ref, b_ref, add_ref, o_ref, acc_ref = refs
        kk = pl.program_id(2)

        @pl.when(kk == 0)
        def _():
            acc_ref[...] = jnp.zeros_like(acc_ref)

        acc_ref[...] += _dot(a_ref[...], b_ref[...], dims)

        @pl.when(kk == nk - 1)
        def _():
            r = acc_ref[...]
            if add_ref is not None:
                r = r + add_coef * add_ref[...]
            o_ref[...] = r.astype(out_dtype)

    if mode == "tn":
        a_spec = pl.BlockSpec((tk, tm), lambda i, j, kk: (kk, i))
    else:
        a_spec = pl.BlockSpec((tm, tk), lambda i, j, kk: (i, kk))
    if mode == "nt":
        b_spec = pl.BlockSpec((tn, tk), lambda i, j, kk: (j, kk))
    else:
        b_spec = pl.BlockSpec((tk, tn), lambda i, j, kk: (kk, j))
    o_spec = pl.BlockSpec((tm, tn), lambda i, j, kk: (i, j))
    in_specs = [a_spec, b_spec] + ([o_spec] if add is not None else [])
    args = (a, b) + ((add,) if add is not None else ())
    return pl.pallas_call(
        body, name=name, grid=(m // tm, n // tn, nk), in_specs=in_specs, out_specs=o_spec,
        out_shape=jax.ShapeDtypeStruct((m, n), out_dtype), scratch_shapes=[pltpu.VMEM((tm, tn), F32)],
        compiler_params=_params(3),
    )(*args)


def _ln_fwd(x, add, gb, name, gp=None, pp=None, tr=512):
    t, d = x.shape
    tr = _pick(t, tr)
    with_ple = gp is not None

    def body(*refs):
        if with_ple:
            x_ref, a_ref, gp_ref, pp_ref, gb_ref, y_ref, r_ref = refs
        else:
            x_ref, a_ref, gb_ref, y_ref, r_ref = refs
        r = ALPHA * x_ref[...] + a_ref[...]
        if with_ple:
            r = r + _sigmoid(gp_ref[...]) * pp_ref[...]
        mu = jnp.mean(r, axis=1, keepdims=True)
        xc = r - mu
        var = jnp.mean(xc * xc, axis=1, keepdims=True)
        y_ref[...] = xc * lax.rsqrt(var + LN_EPS) * gb_ref[0:1, :] + gb_ref[1:2, :]
        r_ref[...] = r

    row = pl.BlockSpec((tr, d), lambda i: (i, 0))
    vec = pl.BlockSpec((2, d), lambda i: (0, 0))
    n_row = 4 if with_ple else 2
    args = (x, add) + ((gp, pp) if with_ple else ()) + (gb,)
    return pl.pallas_call(
        body, name=name, grid=(t // tr,), in_specs=[row] * n_row + [vec], out_specs=[row, row],
        out_shape=[jax.ShapeDtypeStruct((t, d), F32)] * 2, compiler_params=_params(1),
    )(*args)


def _ln_bwd(r, gb, dy, name, gp=None, pp=None, tr=512):
    t, d = r.shape
    tr = _pick(t, tr)
    with_ple = gp is not None

    def body(*refs):
        if with_ple:
            r_ref, dy_ref, gp_ref, pp_ref, gb_ref, dr_ref, dgp_ref, dpp_ref, st_ref = refs
        else:
            r_ref, dy_ref, gb_ref, dr_ref, st_ref = refs
        i = pl.program_id(0)

        @pl.when(i == 0)
        def _():
            st_ref[...] = jnp.zeros_like(st_ref)

        rv = r_ref[...]
        dy_v = dy_ref[...]
        mu = jnp.mean(rv, axis=1, keepdims=True)
        xc = rv - mu
        var = jnp.mean(xc * xc, axis=1, keepdims=True)
        rstd = lax.rsqrt(var + LN_EPS)
        xhat = xc * rstd
        dxh = dy_v * gb_ref[0:1, :]
        m1 = jnp.mean(dxh, axis=1, keepdims=True)
        m2 = jnp.mean(dxh * xhat, axis=1, keepdims=True)
        dr = rstd * (dxh - m1 - xhat * m2)
        dr_ref[...] = dr
        rid = lax.broadcasted_iota(jnp.int32, (2, d), 0)
        dg = jnp.sum(dy_v * xhat, axis=0, keepdims=True)
        db = jnp.sum(dy_v, axis=0, keepdims=True)
        st_ref[...] += jnp.where(rid == 0, dg, db)
        if with_ple:
            sg = _sigmoid(gp_ref[...])
            ppv = pp_ref[...]
            dgp_ref[...] = (dr * ppv * sg * (1.0 - sg)).astype(dgp_ref.dtype)
            dpp_ref[...] = (dr * sg).astype(dpp_ref.dtype)

    row = pl.BlockSpec((tr, d), lambda i: (i, 0))
    vec = pl.BlockSpec((2, d), lambda i: (0, 0))
    if with_ple:
        in_specs, args = [row] * 4 + [vec], (r, dy, gp, pp, gb)
        out_specs = [row, row, row, vec]
        out_shape = [jax.ShapeDtypeStruct((t, d), F32), jax.ShapeDtypeStruct((t, d), MXU_DTYPE),
                     jax.ShapeDtypeStruct((t, d), MXU_DTYPE), jax.ShapeDtypeStruct((2, d), F32)]
    else:
        in_specs, args = [row] * 2 + [vec], (r, dy, gb)
        out_specs = [row, vec]
        out_shape = [jax.ShapeDtypeStruct((t, d), F32), jax.ShapeDtypeStruct((2, d), F32)]
    return pl.pallas_call(body, name=name, grid=(t // tr,), in_specs=in_specs, out_specs=out_specs,
                          out_shape=out_shape, compiler_params=_params(1))(*args)


def _loss_grad(y, target, name, tr=512):
    t, d = y.shape
    tr = _pick(t, tr)

    def body(y_ref, t_ref, dy_ref, l_ref):
        i = pl.program_id(0)

        @pl.when(i == 0)
        def _():
            l_ref[...] = jnp.zeros_like(l_ref)

        e = y_ref[...] - t_ref[...]
        dy_ref[...] = e * (1.0 / d)
        per_tok = jnp.mean(e * e, axis=1, keepdims=True)
        l_ref[...] += 0.5 * jnp.sum(per_tok, axis=0, keepdims=True)

    row = pl.BlockSpec((tr, d), lambda i: (i, 0))
    acc = pl.BlockSpec((SUBLANES, LANES), lambda i: (0, 0))
    return pl.pallas_call(body, name=name, grid=(t // tr,), in_specs=[row, row], out_specs=[row, acc],
                          out_shape=[jax.ShapeDtypeStruct((t, d), F32), jax.ShapeDtypeStruct((SUBLANES, LANES), F32)],
                          compiler_params=_params(1))(y, target)


def _shift_down(v, k, row):
    return jnp.where(row >= k, pltpu.roll(v, k, 0), 0.0)


def _shift_up(v, k, row):
    n = v.shape[0]
    return jnp.where(row < n - k, pltpu.roll(v, n - k, 0), 0.0)


def _pool_window(lane):
    grp = lane // HEAD_DIM
    return jnp.where(grp == 0, POOL_WINDOWS[0], jnp.where(grp == 1, POOL_WINDOWS[1],
                     jnp.where(grp == 2, POOL_WINDOWS[2], POOL_WINDOWS[3])))


def _pool_select(lane, s2, s4, s8, s16):
    grp = lane // HEAD_DIM
    return jnp.where(grp == 0, s2, jnp.where(grp == 1, s4, jnp.where(grp == 2, s8, s16)))


def _pooled(u, row, lane):
    s2 = u + _shift_down(u, 1, row)
    s4 = s2 + _shift_down(s2, 2, row)
    s8 = s4 + _shift_down(s4, 4, row)
    s16 = s8 + _shift_down(s8, 8, row)
    cnt = jnp.minimum(row + 1, _pool_window(lane)).astype(F32)
    return _pool_select(lane, s2, s4, s8, s16) / cnt - u, cnt


def _pool_fwd(h, wbd, scale, nb, s, name):
    def body(u_ref, w_ref, sc_ref, o_ref):
        u = u_ref[...]
        row = lax.broadcasted_iota(jnp.int32, u.shape, 0)
        lane = lax.broadcasted_iota(jnp.int32, u.shape, 1)
        pooled, _ = _pooled(u, row, lane)
        o_ref[...] = (_dot(pooled, w_ref[...]) * sc_ref[...]).astype(o_ref.dtype)

    wb = POOL_WIDTH
    return pl.pallas_call(
        body, name=name, grid=(nb,),
        in_specs=[pl.BlockSpec((s, wb), lambda b: (b, H_POOL // wb)), pl.BlockSpec((wb, wb), lambda b: (0, 0)),
                  pl.BlockSpec((1, wb), lambda b: (0, 0))],
        out_specs=pl.BlockSpec((s, wb), lambda b: (b, 0)),
        out_shape=jax.ShapeDtypeStruct((nb * s, wb), MXU_DTYPE), compiler_params=_params(1),
    )(h, wbd, scale)


def _pool_bwd(h, dmix, wbd, scale, nb, s, name):
    wb = POOL_WIDTH

    def body(u_ref, do_ref, w_ref, sc_ref, du_ref, dw_ref, ds_ref):
        b = pl.program_id(0)

        @pl.when(b == 0)
        def _():
            dw_ref[...] = jnp.zeros_like(dw_ref)
            ds_ref[...] = jnp.zeros_like(ds_ref)

        u = u_ref[...]
        row = lax.broadcasted_iota(jnp.int32, u.shape, 0)
        lane = lax.broadcasted_iota(jnp.int32, u.shape, 1)
        pooled, cnt = _pooled(u, row, lane)
        mixed = _dot(pooled, w_ref[...])
        do = do_ref[...]
        ds_ref[...] += jnp.sum(do * mixed, axis=0, keepdims=True)
        dm = do * sc_ref[...]
        dw_ref[...] += _dot(pooled, dm, TN)
        dpool = _dot(dm, w_ref[...], NT)
        qv = dpool / cnt
        f2 = qv + _shift_up(qv, 1, row)
        f4 = f2 + _shift_up(f2, 2, row)
        f8 = f4 + _shift_up(f4, 4, row)
        f16 = f8 + _shift_up(f8, 8, row)
        du_ref[...] = (_pool_select(lane, f2, f4, f8, f16) - dpool).astype(du_ref.dtype)

    return pl.pallas_call(
        body, name=name, grid=(nb,),
        in_specs=[pl.BlockSpec((s, wb), lambda b: (b, H_POOL // wb)), pl.BlockSpec((s, wb), lambda b: (b, 3)),
                  pl.BlockSpec((wb, wb), lambda b: (0, 0)), pl.BlockSpec((1, wb), lambda b: (0, 0))],
        out_specs=[pl.BlockSpec((s, wb), lambda b: (b, 0)), pl.BlockSpec((wb, wb), lambda b: (0, 0)),
                   pl.BlockSpec((1, wb), lambda b: (0, 0))],
        out_shape=[jax.ShapeDtypeStruct((nb * s, wb), MXU_DTYPE), jax.ShapeDtypeStruct((wb, wb), F32),
                   jax.ShapeDtypeStruct((1, wb), F32)],
        compiler_params=_params(1),
    )(h, dmix, wbd, scale)


def _glu_conv(x, w_ref, b_ref, row):
    return (b_ref[...] + w_ref[2:3, :] * x + w_ref[1:2, :] * _shift_down(x, 1, row)
            + w_ref[0:1, :] * _shift_down(x, 2, row))


def _glu_fwd(up, cw, cb, nb, s, name):
    wt = 2 * GLU_TILE
    nt = up.shape[1] // wt

    def body(u_ref, w_ref, b_ref, o_ref):
        x = u_ref[...]
        row = lax.broadcasted_iota(jnp.int32, x.shape, 0)
        c = _glu_conv(x, w_ref, b_ref, row)
        o_ref[...] = (_silu(c[:, :GLU_TILE]) * c[:, GLU_TILE:]).astype(o_ref.dtype)

    return pl.pallas_call(
        body, name=name, grid=(nt, nb),
        in_specs=[pl.BlockSpec((s, wt), lambda j, b: (b, j)), pl.BlockSpec((3, wt), lambda j, b: (0, j)),
                  pl.BlockSpec((1, wt), lambda j, b: (0, j))],
        out_specs=pl.BlockSpec((s, GLU_TILE), lambda j, b: (b, j)),
        out_shape=jax.ShapeDtypeStruct((nb * s, nt * GLU_TILE), MXU_DTYPE), compiler_params=_params(2),
    )(up, cw, cb)


def _glu_bwd(up, dact, cw, cb, nb, s, name):
    wt = 2 * GLU_TILE
    nt = up.shape[1] // wt

    def body(u_ref, da_ref, w_ref, b_ref, du_ref, acc_ref):
        b = pl.program_id(1)

        @pl.when(b == 0)
        def _():
            acc_ref[...] = jnp.zeros_like(acc_ref)

        x = u_ref[...]
        row = lax.broadcasted_iota(jnp.int32, x.shape, 0)
        x1 = _shift_down(x, 1, row)
        x2 = _shift_down(x, 2, row)
        c = b_ref[...] + w_ref[2:3, :] * x + w_ref[1:2, :] * x1 + w_ref[0:1, :] * x2
        gate, val = c[:, :GLU_TILE], c[:, GLU_TILE:]
        da = da_ref[...]
        dc = jnp.concatenate([da * val * _dsilu(gate), da * _silu(gate)], axis=1)
        dx = (w_ref[2:3, :] * dc + w_ref[1:2, :] * _shift_up(dc, 1, row) + w_ref[0:1, :] * _shift_up(dc, 2, row))
        du_ref[...] = dx.astype(du_ref.dtype)
        rid = lax.broadcasted_iota(jnp.int32, (SUBLANES, wt), 0)
        dw0 = jnp.sum(dc * x2, axis=0, keepdims=True)
        dw1 = jnp.sum(dc * x1, axis=0, keepdims=True)
        dw2 = jnp.sum(dc * x, axis=0, keepdims=True)
        db = jnp.sum(dc, axis=0, keepdims=True)
        acc_ref[...] += (jnp.where(rid == 0, dw0, 0.0) + jnp.where(rid == 1, dw1, 0.0)
                         + jnp.where(rid == 2, dw2, 0.0) + jnp.where(rid == 3, db, 0.0))

    return pl.pallas_call(
        body, name=name, grid=(nt, nb),
        in_specs=[pl.BlockSpec((s, wt), lambda j, b: (b, j)), pl.BlockSpec((s, GLU_TILE), lambda j, b: (b, j)),
                  pl.BlockSpec((3, wt), lambda j, b: (0, j)), pl.BlockSpec((1, wt), lambda j, b: (0, j))],
        out_specs=[pl.BlockSpec((s, wt), lambda j, b: (b, j)), pl.BlockSpec((SUBLANES, wt), lambda j, b: (0, j))],
        out_shape=[jax.ShapeDtypeStruct((nb * s, nt * wt), MXU_DTYPE), jax.ShapeDtypeStruct((SUBLANES, nt * wt), F32)],
        compiler_params=_params(2),
    )(up, dact, cw, cb)


def _sb_masks():
    row = lax.broadcasted_iota(jnp.int32, (QB, QB), 0)
    col = lax.broadcasted_iota(jnp.int32, (QB, QB), 1)
    return row, col


def _sb_fwd(h, nb, s, name):
    nq = s // QB
    scale = HEAD_DIM ** -0.5

    def body(q_ref, k_ref, v_ref, o_ref):
        i = pl.program_id(2)
        row, col = _sb_masks()
        low = col < row
        later = (row > col).astype(BF16)
        for hd in range(2):
            sl = slice(hd * HEAD_DIM, (hd + 1) * HEAD_DIM)
            qh = (q_ref[:, sl] * scale).astype(MXU_DTYPE)

            def step(jj, carry, sl=sl, qh=qh):
                acc, ct = carry
                r0 = pl.multiple_of((i - jj) * QB, QB)
                kj = k_ref[pl.ds(r0, QB), sl]
                vj = v_ref[pl.ds(r0, QB), sl]
                z = _dot(qh, kj, NT)
                strict = jnp.logical_or(low, jj > 0)
                ln = jnp.where(strict, -_softplus(z), 0.0)
                tail = ct + _dot_exact01(ln, later)
                w = jnp.where(strict, jnp.exp(z + ln + tail), 0.0)
                acc = acc + _dot(w, vj)
                return acc, ct + jnp.sum(ln, axis=1, keepdims=True)

            acc, _ = lax.fori_loop(0, i + 1, step, (jnp.zeros((QB, HEAD_DIM), F32), jnp.zeros((QB, 1), F32)))
            o_ref[:, sl] = acc.astype(o_ref.dtype)

    qspec = lambda off: pl.BlockSpec((QB, LANES), lambda b, p, i: (b * nq + i, off // LANES + p))
    kvspec = lambda off: pl.BlockSpec((s, LANES), lambda b, p, i: (b, off // LANES + p))
    return pl.pallas_call(
        body, name=name, grid=(nb, 3, nq), in_specs=[qspec(H_Q), kvspec(H_K), kvspec(H_V)],
        out_specs=pl.BlockSpec((QB, LANES), lambda b, p, i: (b * nq + i, p)),
        out_shape=jax.ShapeDtypeStruct((nb * s, SB_WIDTH), MXU_DTYPE), compiler_params=_params(3),
    )(h, h, h)


def _sb_bwd(h, dmix, nb, s, name):
    nq = s // QB
    scale = HEAD_DIM ** -0.5

    def body(q_ref, k_ref, v_ref, do_ref, dq_ref, dk_ref, dv_ref, p_buf, ls_buf):
        i = pl.program_id(2)

        @pl.when(i == 0)
        def _():
            dk_ref[...] = jnp.zeros_like(dk_ref)
            dv_ref[...] = jnp.zeros_like(dv_ref)

        row, col = _sb_masks()
        low = col < row
        later = (row > col).astype(BF16)
        earlier = (row < col).astype(BF16)
        for hd in range(2):
            sl = slice(hd * HEAD_DIM, (hd + 1) * HEAD_DIM)
            q_raw = q_ref[:, sl]
            qs = (q_raw * scale).astype(MXU_DTYPE)
            doh = do_ref[:, sl]

            def sweep_down(jj, ct, sl=sl, qs=qs, doh=doh):
                j = i - jj
                r0 = pl.multiple_of(j * QB, QB)
                kj = k_ref[pl.ds(r0, QB), sl]
                vj = v_ref[pl.ds(r0, QB), sl]
                z = _dot(qs, kj, NT)
                strict = jnp.logical_or(low, jj > 0)
                ln = jnp.where(strict, -_softplus(z), 0.0)
                ls = z + ln
                tail = ct + _dot_exact01(ln, later)
                a = jnp.where(strict, jnp.exp(ls + tail), 0.0)
                p_buf[j] = _dot(doh, vj, NT) * a
                ls_buf[j] = ls
                dv_ref[pl.ds(r0, QB), sl] += _dot(a, doh, TN)
                return ct + jnp.sum(ln, axis=1, keepdims=True)

            lax.fori_loop(0, i + 1, sweep_down, jnp.zeros((QB, 1), F32))

            def sweep_up(j, carry, sl=sl, q_raw=q_raw):
                dq, cp = carry
                r0 = pl.multiple_of(j * QB, QB)
                pj = p_buf[j]
                sg = jnp.exp(ls_buf[j])
                cum = cp + _dot_exact01(pj, earlier)
                strict = jnp.logical_or(low, j < i)
                dz = jnp.where(strict, pj * (1.0 - sg) - cum * sg, 0.0) * scale
                kj = k_ref[pl.ds(r0, QB), sl]
                dq = dq + _dot(dz, kj)
                dk_ref[pl.ds(r0, QB), sl] += _dot(dz, q_raw, TN)
                return dq, cp + jnp.sum(pj, axis=1, keepdims=True)

            dq, _ = lax.fori_loop(0, i + 1, sweep_up, (jnp.zeros((QB, HEAD_DIM), F32), jnp.zeros((QB, 1), F32)))
            dq_ref[:, sl] = dq

    qspec = lambda off: pl.BlockSpec((QB, LANES), lambda b, p, i: (b * nq + i, off // LANES + p))
    kvspec = lambda off: pl.BlockSpec((s, LANES), lambda b, p, i: (b, off // LANES + p))
    blk_out = pl.BlockSpec((QB, LANES), lambda b, p, i: (b * nq + i, p))
    seq_out = pl.BlockSpec((s, LANES), lambda b, p, i: (b, p))
    shp = jax.ShapeDtypeStruct((nb * s, SB_WIDTH), F32)
    return pl.pallas_call(
        body, name=name, grid=(nb, 3, nq),
        in_specs=[qspec(H_Q), kvspec(H_K), kvspec(H_V), pl.BlockSpec((QB, LANES), lambda b, p, i: (b * nq + i, 3 + p))],
        out_specs=[blk_out, seq_out, seq_out], out_shape=[shp, shp, shp],
        scratch_shapes=[pltpu.VMEM((nq, QB, QB), F32), pltpu.VMEM((nq, QB, QB), F32)],
        compiler_params=_params(3),
    )(h, h, h, dmix)


def _ssd_conv(cur_ref, halo_ref, w_ref, b_ref, ext_ref, first):
    n = SSD_CHUNK
    cur = cur_ref[...]
    ext_ref[0:SUBLANES, :] = jnp.where(first, 0.0, halo_ref[...])
    ext_ref[SUBLANES:SUBLANES + n, :] = cur
    return (b_ref[...] + w_ref[3:4, :] * cur + w_ref[2:3, :] * ext_ref[pl.ds(SUBLANES - 1, n), :]
            + w_ref[1:2, :] * ext_ref[pl.ds(SUBLANES - 2, n), :] + w_ref[0:1, :] * ext_ref[pl.ds(SUBLANES - 3, n), :])


def _ssd_tri():
    row = lax.broadcasted_iota(jnp.int32, (SSD_CHUNK, SSD_CHUNK), 0)
    col = lax.broadcasted_iota(jnp.int32, (SSD_CHUNK, SSD_CHUNK), 1)
    return row, col


def _ssd_specs(nc, rev):
    n = SSD_CHUNK
    hb = n // SUBLANES

    def cidx(c):
        return (nc - 1 - c) if rev else c

    def blk(width, off):
        return pl.BlockSpec((n, width), lambda b, c: (b * nc + cidx(c), off // width))

    def halo(width, off):
        return pl.BlockSpec((SUBLANES, width), lambda b, c: (jnp.maximum((b * nc + cidx(c)) * hb - 1, 0), off // width))

    def full(shape):
        return pl.BlockSpec(shape, lambda b, c: (0,) * len(shape))

    return cidx, blk, halo, full


def _ssd_core_fwd(x, bc, dt, acum, acum_t, a_row, d_row, h_prev_ref, tri):
    n = SSD_CHUNK
    heads = []
    for g in range(2):
        bm = bc[:, g * SSD_STATE:(g + 1) * SSD_STATE]
        cm = bc[:, 2 * SSD_STATE + g * SSD_STATE: 2 * SSD_STATE + (g + 1) * SSD_STATE]
        gmat = _dot(cm, bm, NT)
        for r in range(3):
            hh = g * 3 + r
            ac = acum[:, hh:hh + 1]
            ar = acum_t[hh:hh + 1, :]
            dec = jnp.where(tri, jnp.exp(jnp.minimum(ac - ar, 0.0)), 0.0)
            xh = x[:, hh * HEAD_DIM:(hh + 1) * HEAD_DIM]
            dth = dt[:, hh:hh + 1]
            xdt = xh * dth
            hp = h_prev_ref[hh * HEAD_DIM:(hh + 1) * HEAD_DIM, :]
            ea = jnp.exp(ac)
            m = gmat * dec
            yo = ea * _dot(cm, hp, NT)
            al = acum[n - 1:n, hh:hh + 1]
            w = jnp.exp(al - ac)
            y = _dot(m, xdt) + yo + d_row[:, hh:hh + 1] * xh
            heads.append(dict(g=g, hh=hh, bm=bm, cm=cm, gmat=gmat, dec=dec, xh=xh, dth=dth, xdt=xdt, hp=hp, ea=ea,
                              m=m, yo=yo, al=al, w=w, y=y))
    return heads


def _ssd_prep(xs_ref, xsh_ref, bc_ref, bch_ref, dt_ref, cwx_ref, cbx_ref, cwb_ref, cbb_ref, vec_ref, xe_ref, be_ref, first):
    pre_x = _ssd_conv(xs_ref, xsh_ref, cwx_ref, cbx_ref, xe_ref, first)
    pre_bc = _ssd_conv(bc_ref, bch_ref, cwb_ref, cbb_ref, be_ref, first)
    x = _silu(pre_x)
    bc = _silu(pre_bc)
    dt_pre = dt_ref[...] + vec_ref[0:1, :]
    dt = _softplus(dt_pre)
    a_row = vec_ref[1:2, :]
    amat = dt * a_row
    row, col = _ssd_tri()
    upper = (row <= col).astype(BF16)
    lower = (col <= row).astype(BF16)
    acum = _dot_exact01(amat, lower, NN, x_left=False)
    acum_t = _dot_exact01(amat, upper, TN, x_left=True)
    return pre_x, pre_bc, x, bc, dt_pre, dt, a_row, acum, acum_t, row, col, upper


def _ssd_gate_norm(y, z, nw):
    lane = lax.broadcasted_iota(jnp.int32, y.shape, 1)
    g0 = lane < SSD_WIDTH // 2
    hg = y * _silu(z)
    sq = hg * hg
    ms0 = jnp.sum(jnp.where(g0, sq, 0.0), axis=1, keepdims=True) * (2.0 / SSD_WIDTH)
    ms1 = jnp.sum(jnp.where(g0, 0.0, sq), axis=1, keepdims=True) * (2.0 / SSD_WIDTH)
    rs = jnp.where(g0, lax.rsqrt(ms0 + RMS_EPS), lax.rsqrt(ms1 + RMS_EPS))
    return hg, rs, g0


def _ssd_fwd(h, cwx, cbx, cwb, cbb, vec, nw, nb, s, name):
    n = SSD_CHUNK
    nc = s // n
    _, blk, halo, full = _ssd_specs(nc, False)

    def body(bc_ref, bch_ref, z_ref, xs_ref, xsh_ref, dt_ref, cwx_ref, cbx_ref, cwb_ref, cbb_ref, vec_ref, nw_ref,
             o_ref, hs_ref, h_scr, xe_ref, be_ref, y_scr):
        c = pl.program_id(1)

        @pl.when(c == 0)
        def _():
            h_scr[...] = jnp.zeros_like(h_scr)

        (_, _, x, bc, _, dt, a_row, acum, acum_t, row, col, _) = _ssd_prep(
            xs_ref, xsh_ref, bc_ref, bch_ref, dt_ref, cwx_ref, cbx_ref, cwb_ref, cbb_ref, vec_ref, xe_ref, be_ref, c == 0)
        hs_ref[...] = h_scr[...]
        heads = _ssd_core_fwd(x, bc, dt, acum, acum_t, a_row, vec_ref[2:3, :], hs_ref, col <= row)
        for hd in heads:
            sl = slice(hd["hh"] * HEAD_DIM, (hd["hh"] + 1) * HEAD_DIM)
            y_scr[:, sl] = hd["y"]
            h_scr[sl, :] = jnp.exp(hd["al"]) * hd["hp"] + _dot(hd["xdt"] * hd["w"], hd["bm"], TN)
        hg, rs, _ = _ssd_gate_norm(y_scr[...], z_ref[...], nw_ref[...])
        o_ref[...] = (hg * rs * nw_ref[...]).astype(o_ref.dtype)

    t = nb * s
    return pl.pallas_call(
        body, name=name, grid=(nb, nc),
        in_specs=[blk(512, H_BC), halo(512, H_BC), blk(384, H_Z), blk(384, H_XS), halo(384, H_XS), blk(128, H_DT),
                  full((4, 384)), full((1, 384)), full((4, 512)), full((1, 512)), full((SUBLANES, LANES)), full((1, 384))],
        out_specs=[pl.BlockSpec((n, SSD_WIDTH), lambda b, c: (b * nc + c, 0)),
                   pl.BlockSpec((None, SSD_WIDTH, SSD_STATE), lambda b, c: (b * nc + c, 0, 0))],
        out_shape=[jax.ShapeDtypeStruct((t, SSD_WIDTH), MXU_DTYPE),
                   jax.ShapeDtypeStruct((nb * nc, SSD_WIDTH, SSD_STATE), F32)],
        scratch_shapes=[pltpu.VMEM((SSD_WIDTH, SSD_STATE), F32), pltpu.VMEM((n + SUBLANES, 384), F32),
                        pltpu.VMEM((n + SUBLANES, 512), F32), pltpu.VMEM((n, SSD_WIDTH), F32)],
        compiler_params=_params(2),
    )(h, h, h, h, h, h, cwx, cbx, cwb, cbb, vec, nw)


def _ssd_bwd(h, hstate, dmix, cwx, cbx, cwb, cbb, vec, nw, nb, s, name):
    n = SSD_CHUNK
    nc = s // n
    cidx, blk, halo, full = _ssd_specs(nc, True)

    def body(bc_ref, bch_ref, z_ref, xs_ref, xsh_ref, dt_ref, hs_ref, do_ref, cwx_ref, cbx_ref, cwb_ref, cbb_ref,
             vec_ref, nw_ref, dz_ref, dxs_ref, dbc_ref, ddt_ref, gx_ref, gb_ref, gv_ref, gn_ref,
             dh_scr, xe_ref, be_ref, y_scr, dx_scr, dbc_scr, dxe_ref, dbe_ref, cx_ref, cb_ref):
        b = pl.program_id(0)
        c = pl.program_id(1)
        cc = nc - 1 - c

        @pl.when(jnp.logical_and(b == 0, c == 0))
        def _():
            gx_ref[...] = jnp.zeros_like(gx_ref)
            gb_ref[...] = jnp.zeros_like(gb_ref)
            gv_ref[...] = jnp.zeros_like(gv_ref)
            gn_ref[...] = jnp.zeros_like(gn_ref)

        @pl.when(c == 0)
        def _():
            dh_scr[...] = jnp.zeros_like(dh_scr)
            cx_ref[...] = jnp.zeros_like(cx_ref)
            cb_ref[...] = jnp.zeros_like(cb_ref)

        (pre_x, pre_bc, x, bc, dt_pre, dt, a_row, acum, acum_t, row, col, upper) = _ssd_prep(
            xs_ref, xsh_ref, bc_ref, bch_ref, dt_ref, cwx_ref, cbx_ref, cwb_ref, cbb_ref, vec_ref, xe_ref, be_ref, cc == 0)
        tri = col <= row
        d_row = vec_ref[2:3, :]
        heads = _ssd_core_fwd(x, bc, dt, acum, acum_t, a_row, d_row, hs_ref, tri)
        for hd in heads:
            y_scr[:, hd["hh"] * HEAD_DIM:(hd["hh"] + 1) * HEAD_DIM] = hd["y"]
        y = y_scr[...]
        z = z_ref[...]
        nwv = nw_ref[...]
        hg, rs, g0 = _ssd_gate_norm(y, z, nwv)
        do = do_ref[...]
        nrm = hg * rs
        gn_ref[...] += jnp.sum(do * nrm, axis=0, keepdims=True)
        dn = do * nwv
        dnn = dn * nrm
        mean0 = jnp.sum(jnp.where(g0, dnn, 0.0), axis=1, keepdims=True) * (2.0 / SSD_WIDTH)
        mean1 = jnp.sum(jnp.where(g0, 0.0, dnn), axis=1, keepdims=True) * (2.0 / SSD_WIDTH)
        dhg = rs * (dn - nrm * jnp.where(g0, mean0, mean1))
        dz_ref[...] = (dhg * y * _dsilu(z)).astype(dz_ref.dtype)
        dy = dhg * _silu(z)

        lane = lax.broadcasted_iota(jnp.int32, (n, LANES), 1)
        lane1 = lax.broadcasted_iota(jnp.int32, (1, LANES), 1)
        last_row = lax.broadcasted_iota(jnp.int32, (n, 1), 0) == n - 1
        dacum_col = jnp.zeros((n, LANES), F32)
        da_rowpart = jnp.zeros((n, LANES), F32)
        ddt = jnp.zeros((n, LANES), F32)
        dd_vec = jnp.zeros((1, LANES), F32)
        for g in range(2):
            dg = jnp.zeros((n, n), F32)
            dbm = jnp.zeros((n, SSD_STATE), F32)
            dcm = jnp.zeros((n, SSD_STATE), F32)
            for hd in heads[3 * g:3 * g + 3]:
                hh = hd["hh"]
                sl = slice(hh * HEAD_DIM, (hh + 1) * HEAD_DIM)
                dyh = dy[:, sl]
                dhn = dh_scr[sl, :]
                el = jnp.exp(hd["al"])
                dd_vec = dd_vec + jnp.where(lane1 == hh, jnp.sum(dyh * hd["xh"]), 0.0)
                dcm = dcm + hd["ea"] * _dot(dyh, hd["hp"])
                dm = _dot(dyh, hd["xdt"], NT)
                dg = dg + dm * hd["dec"]
                e = dm * hd["m"]
                t2 = _dot(hd["bm"], dhn, NT)
                dxdt = _dot(hd["m"], dyh, TN) + hd["w"] * t2
                dbm = dbm + hd["w"] * _dot(hd["xdt"], dhn)
                dw_w = jnp.sum(hd["xdt"] * t2, axis=1, keepdims=True) * hd["w"]
                d_el = jnp.sum(dhn * hd["hp"])
                col_part = (jnp.sum(dyh * hd["yo"], axis=1, keepdims=True) + jnp.sum(e, axis=1, keepdims=True) - dw_w
                            + jnp.where(last_row, d_el * el + jnp.sum(dw_w), 0.0))
                dacum_col = dacum_col + jnp.where(lane == hh, col_part, 0.0)
                neg_colsum = -jnp.sum(e, axis=0, keepdims=True)
                rev = jnp.sum(jnp.where(row <= col, neg_colsum, 0.0), axis=1, keepdims=True)
                da_rowpart = da_rowpart + jnp.where(lane == hh, rev, 0.0)
                dh_scr[sl, :] = el * dhn + _dot(dyh * hd["ea"], hd["cm"], TN)
                dx_scr[:, sl] = d_row[:, hh:hh + 1] * dyh + dxdt * hd["dth"]
                ddt = ddt + jnp.where(lane == hh, jnp.sum(dxdt * hd["xh"], axis=1, keepdims=True), 0.0)
            bm, cm = heads[3 * g]["bm"], heads[3 * g]["cm"]
            dcm = dcm + _dot(dg, bm)
            dbm = dbm + _dot(dg, cm, TN)
            dbc_scr[:, g * SSD_STATE:(g + 1) * SSD_STATE] = dbm
            dbc_scr[:, 2 * SSD_STATE + g * SSD_STATE:2 * SSD_STATE + (g + 1) * SSD_STATE] = dcm
        da_mat = _dot_exact01(dacum_col, upper, NN, x_left=False) + da_rowpart
        ddt = ddt + da_mat * a_row
        da_vec = jnp.sum(da_mat * dt, axis=0, keepdims=True)
        ddt_pre = jnp.where(lane < SSD_HEADS, ddt * _sigmoid(dt_pre), 0.0)
        ddt_ref[...] = ddt_pre.astype(ddt_ref.dtype)
        rid = lax.broadcasted_iota(jnp.int32, (SUBLANES, LANES), 0)
        gv_ref[...] += (jnp.where(rid == 0, jnp.sum(ddt_pre, axis=0, keepdims=True), 0.0)
                        + jnp.where(rid == 1, da_vec, 0.0) + jnp.where(rid == 2, dd_vec, 0.0))

        def conv_bwd(dpost, pre, w_ref, ext_ref, dext_ref, carry_ref, cur_ref, out_ref, g_ref, width):
            dco = dpost * _dsilu(pre)
            dext_ref[0:n, :] = dco
            dext_ref[n:n + SUBLANES, :] = carry_ref[...]
            out_ref[...] = (w_ref[3:4, :] * dco + w_ref[2:3, :] * dext_ref[pl.ds(1, n), :]
                            + w_ref[1:2, :] * dext_ref[pl.ds(2, n), :] + w_ref[0:1, :] * dext_ref[pl.ds(3, n), :]
                            ).astype(out_ref.dtype)
            carry_ref[...] = dco[0:SUBLANES, :]
            rid8 = lax.broadcasted_iota(jnp.int32, (SUBLANES, width), 0)
            acc = jnp.where(rid8 == 3, jnp.sum(dco * cur_ref[...], axis=0, keepdims=True), 0.0)
            for j in range(3):
                sh = ext_ref[pl.ds(SUBLANES - 3 + j, n), :]
                acc = acc + jnp.where(rid8 == j, jnp.sum(dco * sh, axis=0, keepdims=True), 0.0)
            acc = acc + jnp.where(rid8 == 4, jnp.sum(dco, axis=0, keepdims=True), 0.0)
            g_ref[...] += acc

        conv_bwd(dx_scr[...], pre_x, cwx_ref, xe_ref, dxe_ref, cx_ref, xs_ref, dxs_ref, gx_ref, 384)
        conv_bwd(dbc_scr[...], pre_bc, cwb_ref, be_ref, dbe_ref, cb_ref, bc_ref, dbc_ref, gb_ref, 512)

    t = nb * s
    rowblk = lambda width: pl.BlockSpec((n, width), lambda b, c: (b * nc + cidx(c), 0))
    return pl.pallas_call(
        body, name=name, grid=(nb, nc),
        in_specs=[blk(512, H_BC), halo(512, H_BC), blk(384, H_Z), blk(384, H_XS), halo(384, H_XS), blk(128, H_DT),
                  pl.BlockSpec((None, SSD_WIDTH, SSD_STATE), lambda b, c: (b * nc + cidx(c), 0, 0)),
                  pl.BlockSpec((n, SSD_WIDTH), lambda b, c: (b * nc + cidx(c), 0)),
                  full((4, 384)), full((1, 384)), full((4, 512)), full((1, 512)), full((SUBLANES, LANES)), full((1, 384))],
        out_specs=[rowblk(384), rowblk(384), rowblk(512), rowblk(128),
                   full((SUBLANES, 384)), full((SUBLANES, 512)), full((SUBLANES, LANES)), full((1, 384))],
        out_shape=[jax.ShapeDtypeStruct((t, 384), MXU_DTYPE), jax.ShapeDtypeStruct((t, 384), MXU_DTYPE),
                   jax.ShapeDtypeStruct((t, 512), MXU_DTYPE), jax.ShapeDtypeStruct((t, 128), MXU_DTYPE),
                   jax.ShapeDtypeStruct((SUBLANES, 384), F32), jax.ShapeDtypeStruct((SUBLANES, 512), F32),
                   jax.ShapeDtypeStruct((SUBLANES, LANES), F32), jax.ShapeDtypeStruct((1, 384), F32)],
        scratch_shapes=[pltpu.VMEM((SSD_WIDTH, SSD_STATE), F32), pltpu.VMEM((n + SUBLANES, 384), F32),
                        pltpu.VMEM((n + SUBLANES, 512), F32), pltpu.VMEM((n, SSD_WIDTH), F32),
                        pltpu.VMEM((n, 384), F32), pltpu.VMEM((n, 512), F32),
                        pltpu.VMEM((n + SUBLANES, 384), F32), pltpu.VMEM((n + SUBLANES, 512), F32),
                        pltpu.VMEM((SUBLANES, 384), F32), pltpu.VMEM((SUBLANES, 512), F32)],
        compiler_params=_params(2),
    )(h, h, h, h, h, h, hstate, dmix, cwx, cbx, cwb, cbb, vec, nw)


def _adamw_math(w, g, m, v):
    m = ADAM_B1 * m + (1.0 - ADAM_B1) * g
    v = ADAM_B2 * v + (1.0 - ADAM_B2) * (g * g)
    m_hat = m / (1.0 - ADAM_B1 ** ADAM_STEP)
    v_hat = v / (1.0 - ADAM_B2 ** ADAM_STEP)
    delta = -ADAM_LR * (m_hat / (jnp.sqrt(v_hat) + ADAM_EPS) + ADAM_WD * w)
    return delta, m, v


def _adamw(w, g, m, v, name, tr=256):
    rows, cols = w.shape
    tr = rows if rows <= tr else tr
    assert rows % tr == 0, (rows, tr)

    def body(w_ref, g_ref, m_ref, v_ref, d_ref, nm_ref, nv_ref):
        d, nm, nv = _adamw_math(w_ref[...], g_ref[...], m_ref[...], v_ref[...])
        d_ref[...] = d
        nm_ref[...] = nm
        nv_ref[...] = nv

    spec = pl.BlockSpec((tr, cols), lambda i: (i, 0))
    shp = jax.ShapeDtypeStruct((rows, cols), F32)
    return pl.pallas_call(body, name=name, grid=(rows // tr,), in_specs=[spec] * 4, out_specs=[spec] * 3,
                          out_shape=[shp] * 3, compiler_params=_params(1))(w, g, m, v)


def _sum8(parts, name, tr=256):
    _, rows, cols = parts.shape
    tr = _pick(rows, tr) if rows % LANES == 0 else (rows if rows <= tr else tr)
    assert rows % tr == 0

    def body(p_ref, o_ref):
        acc = p_ref[0]
        for k in range(1, N_DEV):
            acc = acc + p_ref[k]
        o_ref[...] = acc

    return pl.pallas_call(body, name=name, grid=(rows // tr,),
                          in_specs=[pl.BlockSpec((N_DEV, tr, cols), lambda i: (0, i, 0))],
                          out_specs=pl.BlockSpec((tr, cols), lambda i: (i, 0)),
                          out_shape=jax.ShapeDtypeStruct((rows, cols), F32), compiler_params=_params(1))(parts)


MESH_ID = pl.DeviceIdType.MESH


def _flip(v, bit):
    return 1 - v if bit else v


def _all_gather_hbm(shard, name):
    def body(x_ref, out_ref, send_sems, recv_sems, local_sem):
        x, y, c = lax.axis_index("x"), lax.axis_index("y"), lax.axis_index("c")
        me, sibling = (x, y, c), (x, y, 1 - c)
        chips = [(1 - x, y), (x, 1 - y), (1 - x, 1 - y)]

        def slot(px, py, pc):
            return out_ref.at[4 * px + 2 * py + pc]

        def copy(k, block, to, src=None):
            return pltpu.make_async_remote_copy(
                src_ref=slot(*block) if src is None else src, dst_ref=slot(*block),
                send_sem=send_sems.at[k], recv_sem=recv_sems.at[k], device_id=to, device_id_type=MESH_ID)

        mine = pltpu.make_async_copy(x_ref, slot(*me), local_sem)
        mine.start()
        first = [copy(0, me, sibling, src=x_ref)]
        first += [copy(1 + j, me, (*chip, c), src=x_ref) for j, chip in enumerate(chips)]
        for cp in first:
            cp.start()
        passed = [copy(4 + j, (*chip, c), sibling) for j, chip in enumerate(chips)]
        for j, chip in enumerate(chips):
            copy(1 + j, (*chip, c), me).wait_recv()
            passed[j].start()
        copy(0, sibling, me).wait_recv()
        for j, chip in enumerate(chips):
            copy(4 + j, (*chip, 1 - c), me).wait_recv()
        for cp in first + passed:
            cp.wait_send()
        mine.wait()

    return pl.pallas_call(
        body, name=name, out_shape=jax.ShapeDtypeStruct((N_DEV,) + shard.shape, shard.dtype),
        in_specs=[pl.BlockSpec(memory_space=pl.ANY)], out_specs=pl.BlockSpec(memory_space=pl.ANY),
        scratch_shapes=[pltpu.SemaphoreType.DMA((7,)), pltpu.SemaphoreType.DMA((7,)), pltpu.SemaphoreType.DMA],
        compiler_params=pltpu.CompilerParams(has_side_effects=True),
    )(shard)


def _exchange_shards(parts, name):
    def body(g_ref, out_ref, send_sems, recv_sems, local_sem):
        x, y, c = lax.axis_index("x"), lax.axis_index("y"), lax.axis_index("c")
        me = 4 * x + 2 * y + c
        mine = pltpu.make_async_copy(g_ref.at[me], out_ref.at[me], local_sem)
        mine.start()
        sends = []
        for k in range(1, N_DEV):
            px, py, pc = _flip(x, k & 4), _flip(y, k & 2), _flip(c, k & 1)
            peer = 4 * px + 2 * py + pc
            cp = pltpu.make_async_remote_copy(
                src_ref=g_ref.at[peer], dst_ref=out_ref.at[me], send_sem=send_sems.at[k - 1],
                recv_sem=recv_sems.at[k - 1], device_id=(px, py, pc), device_id_type=MESH_ID)
            cp.start()
            sends.append((cp, peer, (px, py, pc)))
        for k, (cp, peer, pid) in enumerate(sends):
            pltpu.make_async_remote_copy(
                src_ref=g_ref.at[me], dst_ref=out_ref.at[peer], send_sem=send_sems.at[k], recv_sem=recv_sems.at[k],
                device_id=pid, device_id_type=MESH_ID).wait_recv()
        for cp, _, _ in sends:
            cp.wait_send()
        mine.wait()

    return pl.pallas_call(
        body, name=name, out_shape=jax.ShapeDtypeStruct(parts.shape, parts.dtype),
        in_specs=[pl.BlockSpec(memory_space=pl.ANY)], out_specs=pl.BlockSpec(memory_space=pl.ANY),
        scratch_shapes=[pltpu.SemaphoreType.DMA((7,)), pltpu.SemaphoreType.DMA((7,)), pltpu.SemaphoreType.DMA],
        compiler_params=pltpu.CompilerParams(has_side_effects=True),
    )(parts)


def _all_reduce_small(vec, name):
    rows, cols = vec.shape

    def body(x_ref, out_ref, gbuf, send_sems, recv_sems):
        x, y, c = lax.axis_index("x"), lax.axis_index("y"), lax.axis_index("c")
        me, sibling = (x, y, c), (x, y, 1 - c)
        chips = [(1 - x, y), (x, 1 - y), (1 - x, 1 - y)]

        def slot(px, py, pc):
            return gbuf.at[4 * px + 2 * py + pc]

        def copy(k, block, to, src=None):
            return pltpu.make_async_remote_copy(
                src_ref=slot(*block) if src is None else src, dst_ref=slot(*block),
                send_sem=send_sems.at[k], recv_sem=recv_sems.at[k], device_id=to, device_id_type=MESH_ID)

        first = [copy(0, me, sibling, src=x_ref)]
        first += [copy(1 + j, me, (*chip, c), src=x_ref) for j, chip in enumerate(chips)]
        for cp in first:
            cp.start()
        gbuf[4 * x + 2 * y + c] = x_ref[...]
        passed = [copy(4 + j, (*chip, c), sibling) for j, chip in enumerate(chips)]
        for j, chip in enumerate(chips):
            copy(1 + j, (*chip, c), me).wait_recv()
            passed[j].start()
        copy(0, sibling, me).wait_recv()
        for j, chip in enumerate(chips):
            copy(4 + j, (*chip, 1 - c), me).wait_recv()
        for cp in first + passed:
            cp.wait_send()
        acc = gbuf[0]
        for k in range(1, N_DEV):
            acc = acc + gbuf[k]
        out_ref[...] = acc

    return pl.pallas_call(
        body, name=name, out_shape=jax.ShapeDtypeStruct((rows, cols), F32),
        in_specs=[pl.BlockSpec(memory_space=pltpu.VMEM)], out_specs=pl.BlockSpec(memory_space=pltpu.VMEM),
        scratch_shapes=[pltpu.VMEM((N_DEV, rows, cols), F32), pltpu.SemaphoreType.DMA((7,)), pltpu.SemaphoreType.DMA((7,))],
        compiler_params=pltpu.CompilerParams(has_side_effects=True, vmem_limit_bytes=VMEM_LIMIT_BYTES),
    )(vec)


_COL_POOL, _COL_Z, _COL_XBC, _COL_DT, _COL_Q, _COL_K, _COL_V = 0, 256, 640, 1536, 1542, 1926, 2310
_H_SEGMENTS = ((_COL_XBC + SSD_WIDTH, 512), (_COL_POOL, 256), (_COL_Q, 384), (_COL_K, 384), (_COL_V, 384),
               (_COL_Z, 384), (_COL_XBC, 384), (_COL_DT, 6))


def _h_from_orig(w):
    parts = [w[..., o:o + n] for o, n in _H_SEGMENTS]
    pad = jnp.zeros(w.shape[:-1] + (H_COLS - IN_COLS,), w.dtype)
    return jnp.concatenate(parts + [pad], axis=-1)


def _h_to_orig(w):
    offs, o = {}, 0
    for orig, n in _H_SEGMENTS:
        offs[orig] = (o, n)
        o += n
    order = sorted(offs)
    return jnp.concatenate([w[..., offs[k][0]:offs[k][0] + offs[k][1]] for k in order], axis=-1)


def _interleave(w):
    lead = w.shape[:-1]
    nt = D_FF // GLU_TILE
    return jnp.swapaxes(w.reshape(lead + (2, nt, GLU_TILE)), -3, -2).reshape(lead + (2 * D_FF,))


def _deinterleave(w):
    lead = w.shape[:-1]
    nt = D_FF // GLU_TILE
    return jnp.swapaxes(w.reshape(lead + (nt, 2, GLU_TILE)), -3, -2).reshape(lead + (2 * D_FF,))


def _mix_rows_from_orig(w):
    return jnp.concatenate([w[256:640], w[640:1024], w[0:256]], axis=0)


def _mix_rows_to_orig(w):
    return jnp.concatenate([w[768:1024], w[0:384], w[384:768]], axis=0)


def _xbc_split(w):
    return w[..., :SSD_WIDTH], w[..., SSD_WIDTH:]


def _layer_fwd(x, p_l, wt, sp, nb, s):
    h = _mm(x, wt["w_in"], "nn", F32, "mm_in", tm=1024, tn=256)
    pool_out = _pool_fwd(h, wt["pool_bd"], sp["pool_scale"], nb, s, "pool_fwd")
    ssd_out, hstate = _ssd_fwd(h, sp["cwx"], sp["cbx"], sp["cwb"], sp["cbb"], sp["ssd_vec"], sp["ssd_norm_w"], nb, s, "ssd_fwd")
    sb_out = _sb_fwd(h, nb, s, "sb_fwd")
    mixcat = jnp.concatenate([ssd_out, sb_out, pool_out], axis=1)
    mix = _mm(mixcat, wt["w_out"], "nn", F32, "mm_out", tm=1024, tn=512)
    x1, r1 = _ln_fwd(x, mix, sp["ln1"], "ln1_fwd")
    up = _mm(x1, wt["w_up"], "nn", F32, "mm_up", tm=1024, tn=512)
    act = _glu_fwd(up, sp["ffn_cw"], sp["ffn_cb"], nb, s, "glu_fwd")
    ffn = _mm(act, wt["w_down"], "nn", F32, "mm_down", tm=1024, tn=512, tk=1408)
    gp = _mm(x1, wt["w_gate"], "nn", F32, "mm_gate", tm=1024, tn=512)
    pp = _mm(p_l, wt["w_proj"], "nn", F32, "mm_proj", tm=1024, tn=512)
    x2, r2 = _ln_fwd(x1, ffn, sp["ln2"], "ln2_fwd", gp=gp, pp=pp)
    return x2, dict(x=x, h=h, hstate=hstate, mixcat=mixcat, r1=r1, x1=x1, up=up, act=act, gp=gp, pp=pp, r2=r2)


def _layer_bwd(dx2, p_l, sv, wt, sp, nb, s):
    dr2, dgp, dpp, st2 = _ln_bwd(sv["r2"], sp["ln2"], dx2, "ln2_bwd", gp=sv["gp"], pp=sv["pp"])
    g_down = _mm(sv["act"], dr2, "tn", F32, "wg_down", tm=1408, tn=1024, tk=512)
    dact = _mm(dr2, wt["w_down"], "nt", F32, "dg_down", tm=1024, tn=256)
    dup, ffn_acc = _glu_bwd(sv["up"], dact, sp["ffn_cw"], sp["ffn_cb"], nb, s, "glu_bwd")
    g_up = _mm(sv["x1"], dup, "tn", F32, "wg_up", tm=1024, tn=512, tk=512)
    g_gate = _mm(sv["x1"], dgp, "tn", F32, "wg_gate", tm=1024, tn=1024, tk=512)
    g_proj = _mm(p_l, dpp, "tn", F32, "wg_proj", tm=256, tn=1024, tk=512)
    t1 = _mm(dgp, wt["w_gate"], "nt", F32, "dg_gate", tm=1024, tn=512, add=dr2, add_coef=ALPHA)
    dx1 = _mm(dup, wt["w_up"], "nt", F32, "dg_up", tm=1024, tn=512, tk=512, add=t1)
    dr1, st1 = _ln_bwd(sv["r1"], sp["ln1"], dx1, "ln1_bwd")
    g_out = _mm(sv["mixcat"], dr1, "tn", F32, "wg_out", tm=1024, tn=1024, tk=512)
    dmix = _mm(dr1, wt["w_out"], "nt", F32, "dg_out", tm=1024, tn=512)
    du, g_pool_bd, g_pool_scale = _pool_bwd(sv["h"], dmix, wt["pool_bd"], sp["pool_scale"], nb, s, "pool_bwd")
    dz, dxs, dbc, ddt, gx, gb, gv, gn = _ssd_bwd(sv["h"], sv["hstate"], dmix, sp["cwx"], sp["cbx"], sp["cwb"], sp["cbb"],
                                                  sp["ssd_vec"], sp["ssd_norm_w"], nb, s, "ssd_bwd")
    dq, dk, dv = _sb_bwd(sv["h"], dmix, nb, s, "sb_bwd")
    dh = jnp.concatenate([dbc, du, dq.astype(MXU_DTYPE), dk.astype(MXU_DTYPE), dv.astype(MXU_DTYPE), dz, dxs, ddt], axis=1)
    g_in = _mm(sv["x"], dh, "tn", F32, "wg_in", tm=1024, tn=256, tk=512)
    dx = _mm(dh, wt["w_in"], "nt", F32, "dg_in", tm=1024, tn=512, tk=1408, add=dr1, add_coef=ALPHA)
    small = dict(
        pool_w=jnp.stack([g_pool_bd[HEAD_DIM * g:HEAD_DIM * (g + 1), HEAD_DIM * g:HEAD_DIM * (g + 1)] for g in range(4)]),
        pool_scale=g_pool_scale[0],
        ssd_conv_w=jnp.concatenate([gx[0:4], gb[0:4]], axis=1),
        ssd_conv_b=jnp.concatenate([gx[4], gb[4]], axis=0),
        ssd_dt_bias=gv[0, :SSD_HEADS],
        ssd_a_log=gv[1, :SSD_HEADS] * sp["ssd_vec"][1, :SSD_HEADS],
        ssd_d=gv[2, :SSD_HEADS],
        ssd_norm_w=gn[0],
        ln1_g=st1[0], ln1_b=st1[1], ln2_g=st2[0], ln2_b=st2[1],
        ffn_conv_w=_deinterleave(ffn_acc[0:3]),
        ffn_conv_b=_deinterleave(ffn_acc[3]),
    )
    big = dict(w_in=g_in, w_out=g_out, ffn_w_up=g_up, ffn_w_down=g_down, ple_w_gate=g_gate, ple_w_proj=g_proj)
    return dx, big, small


def _layer_params(i, full, rep):
    pool_bd = jnp.zeros((POOL_WIDTH, POOL_WIDTH), F32)
    for g in range(4):
        pool_bd = lax.dynamic_update_slice(pool_bd, rep["pool_w"][i, g], (HEAD_DIM * g, HEAD_DIM * g))
    wt = dict(w_in=full["w_in"][i], w_out=full["w_out"][i], w_up=full["ffn_w_up"][i], w_down=full["ffn_w_down"][i],
              w_gate=full["ple_w_gate"][i], w_proj=full["ple_w_proj"][i], pool_bd=pool_bd.astype(MXU_DTYPE))
    cwx, cwb = _xbc_split(rep["ssd_conv_w"][i])
    cbx, cbb = _xbc_split(rep["ssd_conv_b"][i][None, :])
    vec = jnp.zeros((SUBLANES, LANES), F32)
    vec = vec.at[0, :SSD_HEADS].set(rep["ssd_dt_bias"][i])
    vec = vec.at[1, :SSD_HEADS].set(-jnp.exp(rep["ssd_a_log"][i]))
    vec = vec.at[2, :SSD_HEADS].set(rep["ssd_d"][i])
    sp = dict(pool_scale=rep["pool_scale"][i][None, :], cwx=cwx, cbx=cbx, cwb=cwb, cbb=cbb, ssd_vec=vec,
              ssd_norm_w=rep["ssd_norm_w"][i][None, :],
              ln1=jnp.stack([rep["ln1_g"][i], rep["ln1_b"][i]]), ln2=jnp.stack([rep["ln2_g"][i], rep["ln2_b"][i]]),
              ffn_cw=_interleave(rep["ffn_conv_w"][i]), ffn_cb=_interleave(rep["ffn_conv_b"][i][None, :]))
    return wt, sp


def _local_step(x, p, target, full, rep):
    nb, s, d = x.shape
    t = nb * s
    xf = x.reshape(t, d)
    saved, params = [], []
    for i in range(DEPTH):
        wt, sp = _layer_params(i, full, rep)
        params.append((wt, sp))
        xf, sv = _layer_fwd(xf, p[i].reshape(t, PLE_DIM), wt, sp, nb, s)
        saved.append(sv)
    dy, loss = _loss_grad(xf, target.reshape(t, d), "loss")
    bigs, smalls = [None] * DEPTH, [None] * DEPTH
    for i in reversed(range(DEPTH)):
        wt, sp = params[i]
        dy, bigs[i], smalls[i] = _layer_bwd(dy, p[i].reshape(t, PLE_DIM), saved[i], wt, sp, nb, s)
    return loss, dy.reshape(nb, s, d), bigs, smalls


BIG = ("w_in", "w_out", "ffn_w_up", "ffn_w_down", "ple_w_gate", "ple_w_proj")
SMALL_REPLICATED = ("pool_w", "pool_scale", "ssd_conv_b", "ssd_dt_bias", "ssd_a_log", "ssd_d", "ssd_norm_w",
                    "ln1_g", "ln1_b", "ffn_conv_b", "ln2_g", "ln2_b")
SMALL_SHARDED = ("ssd_conv_w", "ffn_conv_w")
WEIGHTS = ("w_in", "pool_w", "pool_scale", "ssd_conv_w", "ssd_conv_b", "ssd_dt_bias", "ssd_a_log", "ssd_d", "ssd_norm_w",
           "w_out", "ln1_g", "ln1_b", "ffn_w_up", "ffn_conv_w", "ffn_conv_b", "ffn_w_down", "ln2_g", "ln2_b",
           "ple_w_gate", "ple_w_proj")
FLAT_COLS = 1024


def _to_rows(a, cols):
    f = a.reshape(-1)
    pad = (-f.shape[0]) % cols
    if pad:
        f = jnp.concatenate([f, jnp.zeros((pad,), f.dtype)])
    return f.reshape(-1, cols)


def _pack_rows(arrs, cols, row_mult):
    rows = [_to_rows(a, cols) for a in arrs]
    flat = jnp.concatenate(rows, axis=0)
    pad = (-flat.shape[0]) % row_mult
    if pad:
        flat = jnp.concatenate([flat, jnp.zeros((pad, cols), flat.dtype)], axis=0)
    return flat


def _unpack_rows(flat, shapes, cols):
    out, r = [], 0
    for shp in shapes:
        n = 1
        for v in shp:
            n *= v
        nr = -(-n // cols)
        out.append(flat[r:r + nr].reshape(-1)[:n].reshape(shp))
        r += nr
    return out


def _big_shard_for_gather(name, w):
    if name == "w_in":
        w = _h_from_orig(w)
    return w.astype(MXU_DTYPE).reshape(DEPTH, -1, FLAT_COLS)


def _big_full_from_gathered(name, g):
    if name == "w_in":
        return jnp.swapaxes(g.reshape(N_DEV, DEPTH, 128, H_COLS), 0, 1).reshape(DEPTH, D_MODEL, H_COLS)
    if name == "w_out":
        w = jnp.swapaxes(g.reshape(N_DEV, DEPTH, 128, D_MODEL), 0, 1).reshape(DEPTH, D_MODEL, D_MODEL)
        return jnp.concatenate([w[:, 256:640], w[:, 640:1024], w[:, 0:256]], axis=1)
    if name == "ffn_w_up":
        w = jnp.transpose(g.reshape(N_DEV, DEPTH, D_MODEL, 704), (1, 2, 0, 3)).reshape(DEPTH, D_MODEL, 2 * D_FF)
        return _interleave(w)
    if name == "ffn_w_down":
        return jnp.swapaxes(g.reshape(N_DEV, DEPTH, 352, D_MODEL), 0, 1).reshape(DEPTH, D_FF, D_MODEL)
    if name == "ple_w_gate":
        return jnp.swapaxes(g.reshape(N_DEV, DEPTH, 128, D_MODEL), 0, 1).reshape(DEPTH, D_MODEL, D_MODEL)
    if name == "ple_w_proj":
        return jnp.transpose(g.reshape(N_DEV, DEPTH, PLE_DIM, 128), (1, 2, 0, 3)).reshape(DEPTH, PLE_DIM, D_MODEL)
    raise KeyError(name)


def _big_grad_to_shards(name, g):
    if name == "w_in":
        return g.reshape(N_DEV, -1, FLAT_COLS)
    if name == "w_out":
        return _mix_rows_to_orig(g).reshape(N_DEV, -1, FLAT_COLS)
    if name == "ffn_w_up":
        w = _deinterleave(g).reshape(D_MODEL, N_DEV, 704)
        return jnp.swapaxes(w, 0, 1).reshape(N_DEV, -1, FLAT_COLS)
    if name in ("ffn_w_down", "ple_w_gate"):
        return g.reshape(N_DEV, -1, FLAT_COLS)
    if name == "ple_w_proj":
        return jnp.swapaxes(g.reshape(PLE_DIM, N_DEV, 128), 0, 1).reshape(N_DEV, -1, FLAT_COLS)
    raise KeyError(name)


def kernel(x, p, w_in, pool_w, pool_scale, ssd_conv_w, ssd_conv_b, ssd_dt_bias, ssd_a_log, ssd_d, ssd_norm_w, w_out, ln1_g, ln1_b, ffn_w_up, ffn_conv_w, ffn_conv_b, ffn_w_down, ln2_g, ln2_b, ple_w_gate, ple_w_proj, loss_target, m_w_in, m_pool_w, m_pool_scale, m_ssd_conv_w, m_ssd_conv_b, m_ssd_dt_bias, m_ssd_a_log, m_ssd_d, m_ssd_norm_w, m_w_out, m_ln1_g, m_ln1_b, m_ffn_w_up, m_ffn_conv_w, m_ffn_conv_b, m_ffn_w_down, m_ln2_g, m_ln2_b, m_ple_w_gate, m_ple_w_proj, v_w_in, v_pool_w, v_pool_scale, v_ssd_conv_w, v_ssd_conv_b, v_ssd_dt_bias, v_ssd_a_log, v_ssd_d, v_ssd_norm_w, v_w_out, v_ln1_g, v_ln1_b, v_ffn_w_up, v_ffn_conv_w, v_ffn_conv_b, v_ffn_w_down, v_ln2_g, v_ln2_b, v_ple_w_gate, v_ple_w_proj):
    wts = dict(w_in=w_in, pool_w=pool_w, pool_scale=pool_scale, ssd_conv_w=ssd_conv_w, ssd_conv_b=ssd_conv_b,
               ssd_dt_bias=ssd_dt_bias, ssd_a_log=ssd_a_log, ssd_d=ssd_d, ssd_norm_w=ssd_norm_w, w_out=w_out, ln1_g=ln1_g,
               ln1_b=ln1_b, ffn_w_up=ffn_w_up, ffn_conv_w=ffn_conv_w, ffn_conv_b=ffn_conv_b, ffn_w_down=ffn_w_down,
               ln2_g=ln2_g, ln2_b=ln2_b, ple_w_gate=ple_w_gate, ple_w_proj=ple_w_proj)
    mom_m = dict(w_in=m_w_in, pool_w=m_pool_w, pool_scale=m_pool_scale, ssd_conv_w=m_ssd_conv_w, ssd_conv_b=m_ssd_conv_b,
                 ssd_dt_bias=m_ssd_dt_bias, ssd_a_log=m_ssd_a_log, ssd_d=m_ssd_d, ssd_norm_w=m_ssd_norm_w, w_out=m_w_out,
                 ln1_g=m_ln1_g, ln1_b=m_ln1_b, ffn_w_up=m_ffn_w_up, ffn_conv_w=m_ffn_conv_w, ffn_conv_b=m_ffn_conv_b,
                 ffn_w_down=m_ffn_w_down, ln2_g=m_ln2_g, ln2_b=m_ln2_b, ple_w_gate=m_ple_w_gate, ple_w_proj=m_ple_w_proj)
    mom_v = dict(w_in=v_w_in, pool_w=v_pool_w, pool_scale=v_pool_scale, ssd_conv_w=v_ssd_conv_w, ssd_conv_b=v_ssd_conv_b,
                 ssd_dt_bias=v_ssd_dt_bias, ssd_a_log=v_ssd_a_log, ssd_d=v_ssd_d, ssd_norm_w=v_ssd_norm_w, w_out=v_w_out,
                 ln1_g=v_ln1_g, ln1_b=v_ln1_b, ffn_w_up=v_ffn_w_up, ffn_conv_w=v_ffn_conv_w, ffn_conv_b=v_ffn_conv_b,
                 ffn_w_down=v_ffn_w_down, ln2_g=v_ln2_g, ln2_b=v_ln2_b, ple_w_gate=v_ple_w_gate, ple_w_proj=v_ple_w_proj)
    me = 4 * lax.axis_index("x") + 2 * lax.axis_index("y") + lax.axis_index("c")

    shards = [_big_shard_for_gather(n, wts[n]) for n in BIG]
    rows = [sh.shape[1] for sh in shards]
    gathered = _all_gather_hbm(jnp.concatenate(shards, axis=1), "gather_weights")
    full, r0 = {}, 0
    for n, nr in zip(BIG, rows):
        full[n] = _big_full_from_gathered(n, gathered[:, :, r0:r0 + nr])
        r0 += nr
    small_sh = _pack_rows([wts[n] for n in SMALL_SHARDED], LANES, SUBLANES)
    small_g = _all_gather_hbm(small_sh, "gather_conv_weights")
    rep = {n: wts[n] for n in SMALL_REPLICATED}
    per_dev = [_unpack_rows(small_g[j], [wts[n].shape for n in SMALL_SHARDED], LANES) for j in range(N_DEV)]
    for k, n in enumerate(SMALL_SHARDED):
        rep[n] = jnp.concatenate([per_dev[j][k] for j in range(N_DEV)], axis=-1)

    loss_loc, grad_x, bigs, smalls = _local_step(x, p, loss_target, full, rep)

    parts = jnp.concatenate([_big_grad_to_shards(n, bigs[i][n]) for i in range(DEPTH) for n in BIG], axis=1)
    summed = _sum8(_exchange_shards(parts, "exchange_grads"), "sum_grads")
    big_shapes = {"w_in": (128, H_COLS), "w_out": (128, D_MODEL), "ffn_w_up": (D_MODEL, 704), "ffn_w_down": (352, D_MODEL),
                  "ple_w_gate": (128, D_MODEL), "ple_w_proj": (PLE_DIM, 128)}
    pieces = _unpack_rows(summed, [big_shapes[n] for _ in range(DEPTH) for n in BIG], FLAT_COLS)
    grads = {}
    for k, n in enumerate(BIG):
        g = jnp.stack([pieces[i * len(BIG) + k] for i in range(DEPTH)])
        grads[n] = _h_to_orig(g) if n == "w_in" else g
    small_names = SMALL_REPLICATED + SMALL_SHARDED
    small_full_shapes = [rep[n].shape for n in small_names]
    small_vec = _pack_rows([jnp.stack([smalls[i][n] for i in range(DEPTH)]) for n in small_names] + [loss_loc[0, :1]],
                           LANES, SUBLANES)
    small_sum = _all_reduce_small(small_vec, "allreduce_small")
    small_out = _unpack_rows(small_sum, small_full_shapes + [(1,)], LANES)
    loss = small_out[-1][0]
    for n, g in zip(small_names, small_out[:-1]):
        if n in SMALL_SHARDED:
            width = wts[n].shape[-1]
            g = lax.dynamic_slice_in_dim(g, me * width, width, axis=g.ndim - 1)
        grads[n] = g

    delta, new_m, new_v = {}, {}, {}
    for n in BIG:
        shp = wts[n].shape
        two_d = lambda a: a.reshape(-1, shp[-1])
        tr = {"w_in": 128, "ffn_w_down": 352}.get(n, 256)
        d_, m_, v_ = _adamw(two_d(wts[n]), two_d(grads[n]), two_d(mom_m[n]), two_d(mom_v[n]), "adamw_" + n, tr=tr)
        delta[n], new_m[n], new_v[n] = d_.reshape(shp), m_.reshape(shp), v_.reshape(shp)
    packs = [_pack_rows([src[n] for n in small_names], LANES, SUBLANES) for src in (wts, grads, mom_m, mom_v)]
    outs = _adamw(*packs, "adamw_small", tr=packs[0].shape[0])
    shapes = [wts[n].shape for n in small_names]
    for dst, flat in zip((delta, new_m, new_v), outs):
        for n, a in zip(small_names, _unpack_rows(flat, shapes, LANES)):
            dst[n] = a
    return (loss, grad_x, *[grads[n] for n in WEIGHTS], *[delta[n] for n in WEIGHTS],
            *[new_m[n] for n in WEIGHTS], *[new_v[n] for n in WEIGHTS])
```

```python
import functools

import jax
import jax.numpy as jnp
from jax import lax
from jax.experimental import pallas as pl
from jax.experimental.pallas import tpu as pltpu

F32 = jnp.float32
BF16 = jnp.bfloat16
MXU_DTYPE = jnp.bfloat16

D_MODEL = 1024
DEPTH = 4
PLE_DIM = 256
ALPHA = (2 * DEPTH) ** 0.25
LN_EPS = 1e-5
RMS_EPS = 1e-6
HEAD_DIM = 64
POOL_WIDTH = 256
POOL_WINDOWS = (2, 4, 8, 16)
SSD_WIDTH = 384
SSD_HEADS = 6
SSD_STATE = 128
SSD_XBC = 896
SB_WIDTH = 384
IN_COLS = 2694
D_FF = 2816
N_DEV = 8

ADAM_LR = 0.001
ADAM_B1 = 0.9
ADAM_B2 = 0.999
ADAM_EPS = 1e-08
ADAM_WD = 0.01
ADAM_STEP = 10

LANES = 128
SUBLANES = 8
VMEM_LIMIT_BYTES = 56 * 1024 * 1024

H_COLS = 2816
H_BC = 0
H_POOL = 512
H_Q = 768
H_K = 1152
H_V = 1536
H_Z = 1920
H_XS = 2304
H_DT = 2688
SSD_CHUNK = 128
QB = 256
GLU_TILE = 256

NN = ((1,), (0,))
NT = ((1,), (1,))
TN = ((0,), (0,))


def _dot(a, b, dims=NN):
    return lax.dot_general(a.astype(MXU_DTYPE), b.astype(MXU_DTYPE), (dims, ((), ())), preferred_element_type=F32)


def _dot_exact01(x, m01, dims=NN, x_left=True, terms=3):
    acc = None
    r = x
    for _ in range(terms):
        hi = r.astype(BF16)
        ops = (hi, m01) if x_left else (m01, hi)
        part = lax.dot_general(ops[0], ops[1], (dims, ((), ())), preferred_element_type=F32)
        acc = part if acc is None else acc + part
        r = r - hi.astype(F32)
    return acc


def _sigmoid(v):
    return 1.0 / (1.0 + jnp.exp(-v))


def _silu(v):
    return v * _sigmoid(v)


def _dsilu(v):
    s = _sigmoid(v)
    return s * (1.0 + v * (1.0 - s))


def _softplus(v):
    return jnp.maximum(v, 0.0) + jnp.log(1.0 + jnp.exp(-jnp.abs(v)))


def _params(n_axes):
    return pltpu.CompilerParams(dimension_semantics=("arbitrary",) * n_axes, vmem_limit_bytes=VMEM_LIMIT_BYTES)


def _pick(n, pref):
    if n <= pref:
        return n
    for t in range(pref - pref % LANES, 0, -LANES):
        if n % t == 0:
            return t
    raise ValueError((n, pref))


def _mm(a, b, mode, out_dtype, name, tm=512, tn=512, tk=1024, add=None, add_coef=1.0):
    if mode == "nn":
        (m, k), (k2, n) = a.shape, b.shape
    elif mode == "nt":
        (m, k), (n, k2) = a.shape, b.shape
    else:
        (k, m), (k2, n) = a.shape, b.shape
    assert k == k2, (a.shape, b.shape, mode)
    tm, tn, tk = _pick(m, tm), _pick(n, tn), _pick(k, tk)
    nk = k // tk
    dims = {"nn": NN, "nt": NT, "tn": TN}[mode]

    def body(*refs):
        if add is None:
            a_ref, b_ref, o_ref, acc_ref = refs
            add_ref = None
        else:
            a_ref, b_ref, add_ref, o_ref, acc_ref = refs
        kk = pl.program_id(2)

        @pl.when(kk == 0)
        def _():
            acc_ref[...] = jnp.zeros_like(acc_ref)

        acc_ref[...] += _dot(a_ref[...], b_ref[...], dims)

        @pl.when(kk == nk - 1)
        def _():
            r = acc_ref[...]
            if add_ref is not None:
                r = r + add_coef * add_ref[...]
            o_ref[...] = r.astype(out_dtype)

    if mode == "tn":
        a_spec = pl.BlockSpec((tk, tm), lambda i, j, kk: (kk, i))
    else:
        a_spec = pl.BlockSpec((tm, tk), lambda i, j, kk: (i, kk))
    if mode == "nt":
        b_spec = pl.BlockSpec((tn, tk), lambda i, j, kk: (j, kk))
    else:
        b_spec = pl.BlockSpec((tk, tn), lambda i, j, kk: (kk, j))
    o_spec = pl.BlockSpec((tm, tn), lambda i, j, kk: (i, j))
    in_specs = [a_spec, b_spec] + ([o_spec] if add is not None else [])
    args = (a, b) + ((add,) if add is not None else ())
    return pl.pallas_call(
        body, name=name, grid=(m // tm, n // tn, nk), in_specs=in_specs, out_specs=o_spec,
        out_shape=jax.ShapeDtypeStruct((m, n), out_dtype), scratch_shapes=[pltpu.VMEM((tm, tn), F32)],
        compiler_params=_params(3),
    )(*args)


def _ln_fwd(x, add, gb, name, gp=None, pp=None, tr=512):
    t, d = x.shape
    tr = _pick(t, tr)
    with_ple = gp is not None

    def body(*refs):
        if with_ple:
            x_ref, a_ref, gp_ref, pp_ref, gb_ref, y_ref, r_ref = refs
        else:
            x_ref, a_ref, gb_ref, y_ref, r_ref = refs
        r = ALPHA * x_ref[...] + a_ref[...]
        if with_ple:
            r = r + _sigmoid(gp_ref[...]) * pp_ref[...]
        mu = jnp.mean(r, axis=1, keepdims=True)
        xc = r - mu
        var = jnp.mean(xc * xc, axis=1, keepdims=True)
        y_ref[...] = xc * lax.rsqrt(var + LN_EPS) * gb_ref[0:1, :] + gb_ref[1:2, :]
        r_ref[...] = r

    row = pl.BlockSpec((tr, d), lambda i: (i, 0))
    vec = pl.BlockSpec((2, d), lambda i: (0, 0))
    n_row = 4 if with_ple else 2
    args = (x, add) + ((gp, pp) if with_ple else ()) + (gb,)
    return pl.pallas_call(
        body, name=name, grid=(t // tr,), in_specs=[row] * n_row + [vec], out_specs=[row, row],
        out_shape=[jax.ShapeDtypeStruct((t, d), F32)] * 2, compiler_params=_params(1),
    )(*args)


def _ln_bwd(r, gb, dy, name, gp=None, pp=None, tr=512):
    t, d = r.shape
    tr = _pick(t, tr)
    with_ple = gp is not None

    def body(*refs):
        if with_ple:
            r_ref, dy_ref, gp_ref, pp_ref, gb_ref, dr_ref, dgp_ref, dpp_ref, st_ref = refs
        else:
            r_ref, dy_ref, gb_ref, dr_ref, st_ref = refs
        i = pl.program_id(0)

        @pl.when(i == 0)
        def _():
            st_ref[...] = jnp.zeros_like(st_ref)

        rv = r_ref[...]
        dy_v = dy_ref[...]
        mu = jnp.mean(rv, axis=1, keepdims=True)
        xc = rv - mu
        var = jnp.mean(xc * xc, axis=1, keepdims=True)
        rstd = lax.rsqrt(var + LN_EPS)
        xhat = xc * rstd
        dxh = dy_v * gb_ref[0:1, :]
        m1 = jnp.mean(dxh, axis=1, keepdims=True)
        m2 = jnp.mean(dxh * xhat, axis=1, keepdims=True)
        dr = rstd * (dxh - m1 - xhat * m2)
        dr_ref[...] = dr
        rid = lax.broadcasted_iota(jnp.int32, (2, d), 0)
        dg = jnp.sum(dy_v * xhat, axis=0, keepdims=True)
        db = jnp.sum(dy_v, axis=0, keepdims=True)
        st_ref[...] += jnp.where(rid == 0, dg, db)
        if with_ple:
            sg = _sigmoid(gp_ref[...])
            ppv = pp_ref[...]
            dgp_ref[...] = (dr * ppv * sg * (1.0 - sg)).astype(dgp_ref.dtype)
            dpp_ref[...] = (dr * sg).astype(dpp_ref.dtype)

    row = pl.BlockSpec((tr, d), lambda i: (i, 0))
    vec = pl.BlockSpec((2, d), lambda i: (0, 0))
    if with_ple:
        in_specs, args = [row] * 4 + [vec], (r, dy, gp, pp, gb)
        out_specs = [row, row, row, vec]
        out_shape = [jax.ShapeDtypeStruct((t, d), F32), jax.ShapeDtypeStruct((t, d), MXU_DTYPE),
                     jax.ShapeDtypeStruct((t, d), MXU_DTYPE), jax.ShapeDtypeStruct((2, d), F32)]
    else:
        in_specs, args = [row] * 2 + [vec], (r, dy, gb)
        out_specs = [row, vec]
        out_shape = [jax.ShapeDtypeStruct((t, d), F32), jax.ShapeDtypeStruct((2, d), F32)]
    return pl.pallas_call(body, name=name, grid=(t // tr,), in_specs=in_specs, out_specs=out_specs,
                          out_shape=out_shape, compiler_params=_params(1))(*args)


def _loss_grad(y, target, name, tr=512):
    t, d = y.shape
    tr = _pick(t, tr)

    def body(y_ref, t_ref, dy_ref, l_ref):
        i = pl.program_id(0)

        @pl.when(i == 0)
        def _():
            l_ref[...] = jnp.zeros_like(l_ref)

        e = y_ref[...] - t_ref[...]
        dy_ref[...] = e * (1.0 / d)
        per_tok = jnp.mean(e * e, axis=1, keepdims=True)
        l_ref[...] += 0.5 * jnp.sum(per_tok, axis=0, keepdims=True)

    row = pl.BlockSpec((tr, d), lambda i: (i, 0))
    acc = pl.BlockSpec((SUBLANES, LANES), lambda i: (0, 0))
    return pl.pallas_call(body, name=name, grid=(t // tr,), in_specs=[row, row], out_specs=[row, acc],
                          out_shape=[jax.ShapeDtypeStruct((t, d), F32), jax.ShapeDtypeStruct((SUBLANES, LANES), F32)],
                          compiler_params=_params(1))(y, target)


def _shift_down(v, k, row):
    return jnp.where(row >= k, pltpu.roll(v, k, 0), 0.0)


def _shift_up(v, k, row):
    n = v.shape[0]
    return jnp.where(row < n - k, pltpu.roll(v, n - k, 0), 0.0)


def _pool_window(lane):
    grp = lane // HEAD_DIM
    return jnp.where(grp == 0, POOL_WINDOWS[0], jnp.where(grp == 1, POOL_WINDOWS[1],
                     jnp.where(grp == 2, POOL_WINDOWS[2], POOL_WINDOWS[3])))


def _pool_select(lane, s2, s4, s8, s16):
    grp = lane // HEAD_DIM
    return jnp.where(grp == 0, s2, jnp.where(grp == 1, s4, jnp.where(grp == 2, s8, s16)))


def _pooled(u, row, lane):
    s2 = u + _shift_down(u, 1, row)
    s4 = s2 + _shift_down(s2, 2, row)
    s8 = s4 + _shift_down(s4, 4, row)
    s16 = s8 + _shift_down(s8, 8, row)
    cnt = jnp.minimum(row + 1, _pool_window(lane)).astype(F32)
    return _pool_select(lane, s2, s4, s8, s16) / cnt - u, cnt


def _pool_fwd(h, wbd, scale, nb, s, name):
    def body(u_ref, w_ref, sc_ref, o_ref):
        u = u_ref[...]
        row = lax.broadcasted_iota(jnp.int32, u.shape, 0)
        lane = lax.broadcasted_iota(jnp.int32, u.shape, 1)
        pooled, _ = _pooled(u, row, lane)
        o_ref[...] = (_dot(pooled, w_ref[...]) * sc_ref[...]).astype(o_ref.dtype)

    wb = POOL_WIDTH
    return pl.pallas_call(
        body, name=name, grid=(nb,),
        in_specs=[pl.BlockSpec((s, wb), lambda b: (b, H_POOL // wb)), pl.BlockSpec((wb, wb), lambda b: (0, 0)),
                  pl.BlockSpec((1, wb), lambda b: (0, 0))],
        out_specs=pl.BlockSpec((s, wb), lambda b: (b, 0)),
        out_shape=jax.ShapeDtypeStruct((nb * s, wb), MXU_DTYPE), compiler_params=_params(1),
    )(h, wbd, scale)


def _pool_bwd(h, dmix, wbd, scale, nb, s, name):
    wb = POOL_WIDTH

    def body(u_ref, do_ref, w_ref, sc_ref, du_ref, dw_ref, ds_ref):
        b = pl.program_id(0)

        @pl.when(b == 0)
        def _():
            dw_ref[...] = jnp.zeros_like(dw_ref)
            ds_ref[...] = jnp.zeros_like(ds_ref)

        u = u_ref[...]
        row = lax.broadcasted_iota(jnp.int32, u.shape, 0)
        lane = lax.broadcasted_iota(jnp.int32, u.shape, 1)
        pooled, cnt = _pooled(u, row, lane)
        mixed = _dot(pooled, w_ref[...])
        do = do_ref[...]
        ds_ref[...] += jnp.sum(do * mixed, axis=0, keepdims=True)
        dm = do * sc_ref[...]
        dw_ref[...] += _dot(pooled, dm, TN)
        dpool = _dot(dm, w_ref[...], NT)
        qv = dpool / cnt
        f2 = qv + _shift_up(qv, 1, row)
        f4 = f2 + _shift_up(f2, 2, row)
        f8 = f4 + _shift_up(f4, 4, row)
        f16 = f8 + _shift_up(f8, 8, row)
        du_ref[...] = (_pool_select(lane, f2, f4, f8, f16) - dpool).astype(du_ref.dtype)

    return pl.pallas_call(
        body, name=name, grid=(nb,),
        in_specs=[pl.BlockSpec((s, wb), lambda b: (b, H_POOL // wb)), pl.BlockSpec((s, wb), lambda b: (b, 3)),
                  pl.BlockSpec((wb, wb), lambda b: (0, 0)), pl.BlockSpec((1, wb), lambda b: (0, 0))],
        out_specs=[pl.BlockSpec((s, wb), lambda b: (b, 0)), pl.BlockSpec((wb, wb), lambda b: (0, 0)),
                   pl.BlockSpec((1, wb), lambda b: (0, 0))],
        out_shape=[jax.ShapeDtypeStruct((nb * s, wb), MXU_DTYPE), jax.ShapeDtypeStruct((wb, wb), F32),
                   jax.ShapeDtypeStruct((1, wb), F32)],
        compiler_params=_params(1),
    )(h, dmix, wbd, scale)


def _glu_conv(x, w_ref, b_ref, row):
    return (b_ref[...] + w_ref[2:3, :] * x + w_ref[1:2, :] * _shift_down(x, 1, row)
            + w_ref[0:1, :] * _shift_down(x, 2, row))


def _glu_fwd(up, cw, cb, nb, s, name):
    wt = 2 * GLU_TILE
    nt = up.shape[1] // wt

    def body(u_ref, w_ref, b_ref, o_ref):
        x = u_ref[...]
        row = lax.broadcasted_iota(jnp.int32, x.shape, 0)
        c = _glu_conv(x, w_ref, b_ref, row)
        o_ref[...] = (_silu(c[:, :GLU_TILE]) * c[:, GLU_TILE:]).astype(o_ref.dtype)

    return pl.pallas_call(
        body, name=name, grid=(nt, nb),
        in_specs=[pl.BlockSpec((s, wt), lambda j, b: (b, j)), pl.BlockSpec((3, wt), lambda j, b: (0, j)),
                  pl.BlockSpec((1, wt), lambda j, b: (0, j))],
        out_specs=pl.BlockSpec((s, GLU_TILE), lambda j, b: (b, j)),
        out_shape=jax.ShapeDtypeStruct((nb * s, nt * GLU_TILE), MXU_DTYPE), compiler_params=_params(2),
    )(up, cw, cb)


def _glu_bwd(up, dact, cw, cb, nb, s, name):
    wt = 2 * GLU_TILE
    nt = up.shape[1] // wt

    def body(u_ref, da_ref, w_ref, b_ref, du_ref, acc_ref):
        b = pl.program_id(1)

        @pl.when(b == 0)
        def _():
            acc_ref[...] = jnp.zeros_like(acc_ref)

        x = u_ref[...]
        row = lax.broadcasted_iota(jnp.int32, x.shape, 0)
        x1 = _shift_down(x, 1, row)
        x2 = _shift_down(x, 2, row)
        c = b_ref[...] + w_ref[2:3, :] * x + w_ref[1:2, :] * x1 + w_ref[0:1, :] * x2
        gate, val = c[:, :GLU_TILE], c[:, GLU_TILE:]
        da = da_ref[...]
        dc = jnp.concatenate([da * val * _dsilu(gate), da * _silu(gate)], axis=1)
        dx = (w_ref[2:3, :] * dc + w_ref[1:2, :] * _shift_up(dc, 1, row) + w_ref[0:1, :] * _shift_up(dc, 2, row))
        du_ref[...] = dx.astype(du_ref.dtype)
        rid = lax.broadcasted_iota(jnp.int32, (SUBLANES, wt), 0)
        dw0 = jnp.sum(dc * x2, axis=0, keepdims=True)
        dw1 = jnp.sum(dc * x1, axis=0, keepdims=True)
        dw2 = jnp.sum(dc * x, axis=0, keepdims=True)
        db = jnp.sum(dc, axis=0, keepdims=True)
        acc_ref[...] += (jnp.where(rid == 0, dw0, 0.0) + jnp.where(rid == 1, dw1, 0.0)
                         + jnp.where(rid == 2, dw2, 0.0) + jnp.where(rid == 3, db, 0.0))

    return pl.pallas_call(
        body, name=name, grid=(nt, nb),
        in_specs=[pl.BlockSpec((s, wt), lambda j, b: (b, j)), pl.BlockSpec((s, GLU_TILE), lambda j, b: (b, j)),
                  pl.BlockSpec((3, wt), lambda j, b: (0, j)), pl.BlockSpec((1, wt), lambda j, b: (0, j))],
        out_specs=[pl.BlockSpec((s, wt), lambda j, b: (b, j)), pl.BlockSpec((SUBLANES, wt), lambda j, b: (0, j))],
        out_shape=[jax.ShapeDtypeStruct((nb * s, nt * wt), MXU_DTYPE), jax.ShapeDtypeStruct((SUBLANES, nt * wt), F32)],
        compiler_params=_params(2),
    )(up, dact, cw, cb)


def _sb_masks():
    row = lax.broadcasted_iota(jnp.int32, (QB, QB), 0)
    col = lax.broadcasted_iota(jnp.int32, (QB, QB), 1)
    return row, col


def _sb_fwd(h, nb, s, name):
    nq = s // QB
    scale = HEAD_DIM ** -0.5

    def body(q_ref, k_ref, v_ref, o_ref):
        i = pl.program_id(2)
        row, col = _sb_masks()
        low = col < row
        later = (row > col).astype(BF16)
        sls = [slice(hd * HEAD_DIM, (hd + 1) * HEAD_DIM) for hd in range(2)]
        qs = [(q_ref[:, sl] * scale).astype(MXU_DTYPE) for sl in sls]

        def block(hd, r0, ct, diagonal):
            kj = k_ref[pl.ds(r0, QB), sls[hd]]
            vj = v_ref[pl.ds(r0, QB), sls[hd]]
            z = _dot(qs[hd], kj, NT)
            ln = -_softplus(z)
            if diagonal:
                ln = jnp.where(low, ln, 0.0)
            tail = ct + _dot_exact01(ln, later, terms=2)
            w = jnp.exp(z + ln + tail)
            if diagonal:
                w = jnp.where(low, w, 0.0)
            return _dot(w, vj), jnp.sum(ln, axis=1, keepdims=True)

        zero = jnp.zeros((QB, 1), F32)
        r_diag = pl.multiple_of(i * QB, QB)
        a0, c0 = block(0, r_diag, zero, True)
        a1, c1 = block(1, r_diag, zero, True)

        def step(jj, carry):
            a0, c0, a1, c1 = carry
            r0 = pl.multiple_of((i - 1 - jj) * QB, QB)
            d0, s0 = block(0, r0, c0, False)
            d1, s1 = block(1, r0, c1, False)
            return a0 + d0, c0 + s0, a1 + d1, c1 + s1

        a0, _, a1, _ = lax.fori_loop(0, i, step, (a0, c0, a1, c1))
        o_ref[:, sls[0]] = a0.astype(o_ref.dtype)
        o_ref[:, sls[1]] = a1.astype(o_ref.dtype)

    qspec = lambda off: pl.BlockSpec((QB, LANES), lambda b, p, i: (b * nq + i, off // LANES + p))
    kvspec = lambda off: pl.BlockSpec((s, LANES), lambda b, p, i: (b, off // LANES + p))
    return pl.pallas_call(
        body, name=name, grid=(nb, 3, nq), in_specs=[qspec(H_Q), kvspec(H_K), kvspec(H_V)],
        out_specs=pl.BlockSpec((QB, LANES), lambda b, p, i: (b * nq + i, p)),
        out_shape=jax.ShapeDtypeStruct((nb * s, SB_WIDTH), MXU_DTYPE), compiler_params=_params(3),
    )(h, h, h)


def _sb_bwd(h, dmix, nb, s, name):
    nq = s // QB
    scale = HEAD_DIM ** -0.5

    def body(q_ref, k_ref, v_ref, do_ref, dq_ref, dk_ref, dv_ref, p_buf, ls_buf):
        i = pl.program_id(2)

        @pl.when(i == 0)
        def _():
            dk_ref[...] = jnp.zeros_like(dk_ref)
            dv_ref[...] = jnp.zeros_like(dv_ref)

        row, col = _sb_masks()
        low = col < row
        later = (row > col).astype(BF16)
        earlier = (row < col).astype(BF16)
        sls = [slice(hd * HEAD_DIM, (hd + 1) * HEAD_DIM) for hd in range(2)]
        q_raw = [q_ref[:, sl].astype(MXU_DTYPE) for sl in sls]
        qs = [(q_ref[:, sl] * scale).astype(MXU_DTYPE) for sl in sls]
        do = [do_ref[:, sl].astype(MXU_DTYPE) for sl in sls]

        def down(hd, j, ct, diagonal):
            r0 = pl.multiple_of(j * QB, QB)
            kj = k_ref[pl.ds(r0, QB), sls[hd]]
            vj = v_ref[pl.ds(r0, QB), sls[hd]]
            z = _dot(qs[hd], kj, NT)
            ln = -_softplus(z)
            if diagonal:
                ln = jnp.where(low, ln, 0.0)
            ls = z + ln
            tail = ct + _dot_exact01(ln, later, terms=2)
            a = jnp.exp(ls + tail)
            if diagonal:
                a = jnp.where(low, a, 0.0)
                ls = jnp.where(low, ls, 0.0)
            p_buf[hd, j] = _dot(do[hd], vj, NT) * a
            ls_buf[hd, j] = ls
            dv_ref[pl.ds(r0, QB), sls[hd]] += _dot(a, do[hd], TN)
            return jnp.sum(ln, axis=1, keepdims=True)

        zero = jnp.zeros((QB, 1), F32)
        c0 = down(0, i, zero, True)
        c1 = down(1, i, zero, True)

        def down_step(jj, carry):
            c0, c1 = carry
            j = i - 1 - jj
            return c0 + down(0, j, c0, False), c1 + down(1, j, c1, False)

        lax.fori_loop(0, i, down_step, (c0, c1))

        def up(hd, j, dq, cp, diagonal):
            r0 = pl.multiple_of(j * QB, QB)
            pj = p_buf[hd, j]
            sg = jnp.exp(ls_buf[hd, j])
            cum = cp + _dot_exact01(pj, earlier)
            dz = (pj * (1.0 - sg) - cum * sg) * scale
            if diagonal:
                dz = jnp.where(low, dz, 0.0)
            kj = k_ref[pl.ds(r0, QB), sls[hd]]
            dk_ref[pl.ds(r0, QB), sls[hd]] += _dot(dz, q_raw[hd], TN)
            return dq + _dot(dz, kj), cp + jnp.sum(pj, axis=1, keepdims=True)

        def up_step(j, carry):
            dq0, cp0, dq1, cp1 = carry
            dq0, cp0 = up(0, j, dq0, cp0, False)
            dq1, cp1 = up(1, j, dq1, cp1, False)
            return dq0, cp0, dq1, cp1

        zq = jnp.zeros((QB, HEAD_DIM), F32)
        dq0, cp0, dq1, cp1 = lax.fori_loop(0, i, up_step, (zq, zero, zq, zero))
        dq0, _ = up(0, i, dq0, cp0, True)
        dq1, _ = up(1, i, dq1, cp1, True)
        dq_ref[:, sls[0]] = dq0
        dq_ref[:, sls[1]] = dq1

    qspec = lambda off: pl.BlockSpec((QB, LANES), lambda b, p, i: (b * nq + i, off // LANES + p))
    kvspec = lambda off: pl.BlockSpec((s, LANES), lambda b, p, i: (b, off // LANES + p))
    blk_out = pl.BlockSpec((QB, LANES), lambda b, p, i: (b * nq + i, p))
    seq_out = pl.BlockSpec((s, LANES), lambda b, p, i: (b, p))
    shp = jax.ShapeDtypeStruct((nb * s, SB_WIDTH), F32)
    return pl.pallas_call(
        body, name=name, grid=(nb, 3, nq),
        in_specs=[qspec(H_Q), kvspec(H_K), kvspec(H_V), pl.BlockSpec((QB, LANES), lambda b, p, i: (b * nq + i, 3 + p))],
        out_specs=[blk_out, seq_out, seq_out], out_shape=[shp, shp, shp],
        scratch_shapes=[pltpu.VMEM((2, nq, QB, QB), F32), pltpu.VMEM((2, nq, QB, QB), F32)],
        compiler_params=_params(3),
    )(h, h, h, dmix)


def _ssd_conv(cur_ref, halo_ref, w_ref, b_ref, ext_ref, first):
    n = SSD_CHUNK
    cur = cur_ref[...]
    ext_ref[0:SUBLANES, :] = jnp.where(first, 0.0, halo_ref[...])
    ext_ref[SUBLANES:SUBLANES + n, :] = cur
    return (b_ref[...] + w_ref[3:4, :] * cur + w_ref[2:3, :] * ext_ref[pl.ds(SUBLANES - 1, n), :]
            + w_ref[1:2, :] * ext_ref[pl.ds(SUBLANES - 2, n), :] + w_ref[0:1, :] * ext_ref[pl.ds(SUBLANES - 3, n), :])


def _ssd_tri():
    row = lax.broadcasted_iota(jnp.int32, (SSD_CHUNK, SSD_CHUNK), 0)
    col = lax.broadcasted_iota(jnp.int32, (SSD_CHUNK, SSD_CHUNK), 1)
    return row, col


def _ssd_specs(nc, rev):
    n = SSD_CHUNK
    hb = n // SUBLANES

    def cidx(c):
        return (nc - 1 - c) if rev else c

    def blk(width, off):
        return pl.BlockSpec((n, width), lambda b, c: (b * nc + cidx(c), off // width))

    def halo(width, off):
        return pl.BlockSpec((SUBLANES, width), lambda b, c: (jnp.maximum((b * nc + cidx(c)) * hb - 1, 0), off // width))

    def full(shape):
        return pl.BlockSpec(shape, lambda b, c: (0,) * len(shape))

    return cidx, blk, halo, full


def _ssd_core_fwd(x, bc, dt, acum, acum_t, a_row, d_row, h_prev_ref, tri):
    n = SSD_CHUNK
    heads = []
    for g in range(2):
        bm = bc[:, g * SSD_STATE:(g + 1) * SSD_STATE]
        cm = bc[:, 2 * SSD_STATE + g * SSD_STATE: 2 * SSD_STATE + (g + 1) * SSD_STATE]
        gmat = _dot(cm, bm, NT)
        for r in range(3):
            hh = g * 3 + r
            ac = acum[:, hh:hh + 1]
            ar = acum_t[hh:hh + 1, :]
            dec = jnp.where(tri, jnp.exp(jnp.minimum(ac - ar, 0.0)), 0.0)
            xh = x[:, hh * HEAD_DIM:(hh + 1) * HEAD_DIM]
            dth = dt[:, hh:hh + 1]
            xdt = xh * dth
            hp = h_prev_ref[hh * HEAD_DIM:(hh + 1) * HEAD_DIM, :]
            ea = jnp.exp(ac)
            m = gmat * dec
            yo = ea * _dot(cm, hp, NT)
            al = acum[n - 1:n, hh:hh + 1]
            w = jnp.exp(al - ac)
            y = _dot(m, xdt) + yo + d_row[:, hh:hh + 1] * xh
            heads.append(dict(g=g, hh=hh, bm=bm, cm=cm, gmat=gmat, dec=dec, xh=xh, dth=dth, xdt=xdt, hp=hp, ea=ea,
                              m=m, yo=yo, al=al, w=w, y=y))
    return heads


def _ssd_prep(xs_ref, xsh_ref, bc_ref, bch_ref, dt_ref, cwx_ref, cbx_ref, cwb_ref, cbb_ref, vec_ref, xe_ref, be_ref, first):
    pre_x = _ssd_conv(xs_ref, xsh_ref, cwx_ref, cbx_ref, xe_ref, first)
    pre_bc = _ssd_conv(bc_ref, bch_ref, cwb_ref, cbb_ref, be_ref, first)
    x = _silu(pre_x)
    bc = _silu(pre_bc)
    dt_pre = dt_ref[...] + vec_ref[0:1, :]
    dt = _softplus(dt_pre)
    a_row = vec_ref[1:2, :]
    amat = dt * a_row
    row, col = _ssd_tri()
    upper = (row <= col).astype(BF16)
    lower = (col <= row).astype(BF16)
    acum = _dot_exact01(amat, lower, NN, x_left=False)
    acum_t = _dot_exact01(amat, upper, TN, x_left=True)
    return pre_x, pre_bc, x, bc, dt_pre, dt, a_row, acum, acum_t, row, col, upper


def _ssd_gate_norm(y, z, nw):
    lane = lax.broadcasted_iota(jnp.int32, y.shape, 1)
    g0 = lane < SSD_WIDTH // 2
    hg = y * _silu(z)
    sq = hg * hg
    ms0 = jnp.sum(jnp.where(g0, sq, 0.0), axis=1, keepdims=True) * (2.0 / SSD_WIDTH)
    ms1 = jnp.sum(jnp.where(g0, 0.0, sq), axis=1, keepdims=True) * (2.0 / SSD_WIDTH)
    rs = jnp.where(g0, lax.rsqrt(ms0 + RMS_EPS), lax.rsqrt(ms1 + RMS_EPS))
    return hg, rs, g0


def _ssd_fwd(h, cwx, cbx, cwb, cbb, vec, nw, nb, s, name):
    n = SSD_CHUNK
    nc = s // n
    _, blk, halo, full = _ssd_specs(nc, False)

    def body(bc_ref, bch_ref, z_ref, xs_ref, xsh_ref, dt_ref, cwx_ref, cbx_ref, cwb_ref, cbb_ref, vec_ref, nw_ref,
             o_ref, hs_ref, h_scr, xe_ref, be_ref, y_scr):
        c = pl.program_id(1)

        @pl.when(c == 0)
        def _():
            h_scr[...] = jnp.zeros_like(h_scr)

        (_, _, x, bc, _, dt, a_row, acum, acum_t, row, col, _) = _ssd_prep(
            xs_ref, xsh_ref, bc_ref, bch_ref, dt_ref, cwx_ref, cbx_ref, cwb_ref, cbb_ref, vec_ref, xe_ref, be_ref, c == 0)
        hs_ref[...] = h_scr[...]
        heads = _ssd_core_fwd(x, bc, dt, acum, acum_t, a_row, vec_ref[2:3, :], hs_ref, col <= row)
        for hd in heads:
            sl = slice(hd["hh"] * HEAD_DIM, (hd["hh"] + 1) * HEAD_DIM)
            y_scr[:, sl] = hd["y"]
            h_scr[sl, :] = jnp.exp(hd["al"]) * hd["hp"] + _dot(hd["xdt"] * hd["w"], hd["bm"], TN)
        hg, rs, _ = _ssd_gate_norm(y_scr[...], z_ref[...], nw_ref[...])
        o_ref[...] = (hg * rs * nw_ref[...]).astype(o_ref.dtype)

    t = nb * s
    return pl.pallas_call(
        body, name=name, grid=(nb, nc),
        in_specs=[blk(512, H_BC), halo(512, H_BC), blk(384, H_Z), blk(384, H_XS), halo(384, H_XS), blk(128, H_DT),
                  full((4, 384)), full((1, 384)), full((4, 512)), full((1, 512)), full((SUBLANES, LANES)), full((1, 384))],
        out_specs=[pl.BlockSpec((n, SSD_WIDTH), lambda b, c: (b * nc + c, 0)),
                   pl.BlockSpec((None, SSD_WIDTH, SSD_STATE), lambda b, c: (b * nc + c, 0, 0))],
        out_shape=[jax.ShapeDtypeStruct((t, SSD_WIDTH), MXU_DTYPE),
                   jax.ShapeDtypeStruct((nb * nc, SSD_WIDTH, SSD_STATE), F32)],
        scratch_shapes=[pltpu.VMEM((SSD_WIDTH, SSD_STATE), F32), pltpu.VMEM((n + SUBLANES, 384), F32),
                        pltpu.VMEM((n + SUBLANES, 512), F32), pltpu.VMEM((n, SSD_WIDTH), F32)],
        compiler_params=_params(2),
    )(h, h, h, h, h, h, cwx, cbx, cwb, cbb, vec, nw)


def _ssd_bwd(h, hstate, dmix, cwx, cbx, cwb, cbb, vec, nw, nb, s, name):
    n = SSD_CHUNK
    nc = s // n
    cidx, blk, halo, full = _ssd_specs(nc, True)

    def body(bc_ref, bch_ref, z_ref, xs_ref, xsh_ref, dt_ref, hs_ref, do_ref, cwx_ref, cbx_ref, cwb_ref, cbb_ref,
             vec_ref, nw_ref, dz_ref, dxs_ref, dbc_ref, ddt_ref, gx_ref, gb_ref, gv_ref, gn_ref,
             dh_scr, xe_ref, be_ref, y_scr, dx_scr, dbc_scr, dxe_ref, dbe_ref, cx_ref, cb_ref):
        b = pl.program_id(0)
        c = pl.program_id(1)
        cc = nc - 1 - c

        @pl.when(jnp.logical_and(b == 0, c == 0))
        def _():
            gx_ref[...] = jnp.zeros_like(gx_ref)
            gb_ref[...] = jnp.zeros_like(gb_ref)
            gv_ref[...] = jnp.zeros_like(gv_ref)
            gn_ref[...] = jnp.zeros_like(gn_ref)

        @pl.when(c == 0)
        def _():
            dh_scr[...] = jnp.zeros_like(dh_scr)
            cx_ref[...] = jnp.zeros_like(cx_ref)
            cb_ref[...] = jnp.zeros_like(cb_ref)

        (pre_x, pre_bc, x, bc, dt_pre, dt, a_row, acum, acum_t, row, col, upper) = _ssd_prep(
            xs_ref, xsh_ref, bc_ref, bch_ref, dt_ref, cwx_ref, cbx_ref, cwb_ref, cbb_ref, vec_ref, xe_ref, be_ref, cc == 0)
        tri = col <= row
        d_row = vec_ref[2:3, :]
        heads = _ssd_core_fwd(x, bc, dt, acum, acum_t, a_row, d_row, hs_ref, tri)
        for hd in heads:
            y_scr[:, hd["hh"] * HEAD_DIM:(hd["hh"] + 1) * HEAD_DIM] = hd["y"]
        y = y_scr[...]
        z = z_ref[...]
        nwv = nw_ref[...]
        hg, rs, g0 = _ssd_gate_norm(y, z, nwv)
        do = do_ref[...]
        nrm = hg * rs
        gn_ref[...] += jnp.sum(do * nrm, axis=0, keepdims=True)
        dn = do * nwv
        dnn = dn * nrm
        mean0 = jnp.sum(jnp.where(g0, dnn, 0.0), axis=1, keepdims=True) * (2.0 / SSD_WIDTH)
        mean1 = jnp.sum(jnp.where(g0, 0.0, dnn), axis=1, keepdims=True) * (2.0 / SSD_WIDTH)
        dhg = rs * (dn - nrm * jnp.where(g0, mean0, mean1))
        dz_ref[...] = (dhg * y * _dsilu(z)).astype(dz_ref.dtype)
        dy = dhg * _silu(z)

        lane = lax.broadcasted_iota(jnp.int32, (n, LANES), 1)
        lane1 = lax.broadcasted_iota(jnp.int32, (1, LANES), 1)
        last_row = lax.broadcasted_iota(jnp.int32, (n, 1), 0) == n - 1
        dacum_col = jnp.zeros((n, LANES), F32)
        da_rowpart = jnp.zeros((n, LANES), F32)
        ddt = jnp.zeros((n, LANES), F32)
        dd_vec = jnp.zeros((1, LANES), F32)
        for g in range(2):
            dg = jnp.zeros((n, n), F32)
            dbm = jnp.zeros((n, SSD_STATE), F32)
            dcm = jnp.zeros((n, SSD_STATE), F32)
            for hd in heads[3 * g:3 * g + 3]:
                hh = hd["hh"]
                sl = slice(hh * HEAD_DIM, (hh + 1) * HEAD_DIM)
                dyh = dy[:, sl]
                dhn = dh_scr[sl, :]
                el = jnp.exp(hd["al"])
                dd_vec = dd_vec + jnp.where(lane1 == hh, jnp.sum(dyh * hd["xh"]), 0.0)
                dcm = dcm + hd["ea"] * _dot(dyh, hd["hp"])
                dm = _dot(dyh, hd["xdt"], NT)
                dg = dg + dm * hd["dec"]
                e = dm * hd["m"]
                t2 = _dot(hd["bm"], dhn, NT)
                dxdt = _dot(hd["m"], dyh, TN) + hd["w"] * t2
                dbm = dbm + hd["w"] * _dot(hd["xdt"], dhn)
                dw_w = jnp.sum(hd["xdt"] * t2, axis=1, keepdims=True) * hd["w"]
                d_el = jnp.sum(dhn * hd["hp"])
                col_part = (jnp.sum(dyh * hd["yo"], axis=1, keepdims=True) + jnp.sum(e, axis=1, keepdims=True) - dw_w
                            + jnp.where(last_row, d_el * el + jnp.sum(dw_w), 0.0))
                dacum_col = dacum_col + jnp.where(lane == hh, col_part, 0.0)
                neg_colsum = -jnp.sum(e, axis=0, keepdims=True)
                rev = jnp.sum(jnp.where(row <= col, neg_colsum, 0.0), axis=1, keepdims=True)
                da_rowpart = da_rowpart + jnp.where(lane == hh, rev, 0.0)
                dh_scr[sl, :] = el * dhn + _dot(dyh * hd["ea"], hd["cm"], TN)
                dx_scr[:, sl] = d_row[:, hh:hh + 1] * dyh + dxdt * hd["dth"]
                ddt = ddt + jnp.where(lane == hh, jnp.sum(dxdt * hd["xh"], axis=1, keepdims=True), 0.0)
            bm, cm = heads[3 * g]["bm"], heads[3 * g]["cm"]
            dcm = dcm + _dot(dg, bm)
            dbm = dbm + _dot(dg, cm, TN)
            dbc_scr[:, g * SSD_STATE:(g + 1) * SSD_STATE] = dbm
            dbc_scr[:, 2 * SSD_STATE + g * SSD_STATE:2 * SSD_STATE + (g + 1) * SSD_STATE] = dcm
        da_mat = _dot_exact01(dacum_col, upper, NN, x_left=False) + da_rowpart
        ddt = ddt + da_mat * a_row
        da_vec = jnp.sum(da_mat * dt, axis=0, keepdims=True)
        ddt_pre = jnp.where(lane < SSD_HEADS, ddt * _sigmoid(dt_pre), 0.0)
        ddt_ref[...] = ddt_pre.astype(ddt_ref.dtype)
        rid = lax.broadcasted_iota(jnp.int32, (SUBLANES, LANES), 0)
        gv_ref[...] += (jnp.where(rid == 0, jnp.sum(ddt_pre, axis=0, keepdims=True), 0.0)
                        + jnp.where(rid == 1, da_vec, 0.0) + jnp.where(rid == 2, dd_vec, 0.0))

        def conv_bwd(dpost, pre, w_ref, ext_ref, dext_ref, carry_ref, cur_ref, out_ref, g_ref, width):
            dco = dpost * _dsilu(pre)
            dext_ref[0:n, :] = dco
            dext_ref[n:n + SUBLANES, :] = carry_ref[...]
            out_ref[...] = (w_ref[3:4, :] * dco + w_ref[2:3, :] * dext_ref[pl.ds(1, n), :]
                            + w_ref[1:2, :] * dext_ref[pl.ds(2, n), :] + w_ref[0:1, :] * dext_ref[pl.ds(3, n), :]
                            ).astype(out_ref.dtype)
            carry_ref[...] = dco[0:SUBLANES, :]
            rid8 = lax.broadcasted_iota(jnp.int32, (SUBLANES, width), 0)
            acc = jnp.where(rid8 == 3, jnp.sum(dco * cur_ref[...], axis=0, keepdims=True), 0.0)
            for j in range(3):
                sh = ext_ref[pl.ds(SUBLANES - 3 + j, n), :]
                acc = acc + jnp.where(rid8 == j, jnp.sum(dco * sh, axis=0, keepdims=True), 0.0)
            acc = acc + jnp.where(rid8 == 4, jnp.sum(dco, axis=0, keepdims=True), 0.0)
            g_ref[...] += acc

        conv_bwd(dx_scr[...], pre_x, cwx_ref, xe_ref, dxe_ref, cx_ref, xs_ref, dxs_ref, gx_ref, 384)
        conv_bwd(dbc_scr[...], pre_bc, cwb_ref, be_ref, dbe_ref, cb_ref, bc_ref, dbc_ref, gb_ref, 512)

    t = nb * s
    rowblk = lambda width: pl.BlockSpec((n, width), lambda b, c: (b * nc + cidx(c), 0))
    return pl.pallas_call(
        body, name=name, grid=(nb, nc),
        in_specs=[blk(512, H_BC), halo(512, H_BC), blk(384, H_Z), blk(384, H_XS), halo(384, H_XS), blk(128, H_DT),
                  pl.BlockSpec((None, SSD_WIDTH, SSD_STATE), lambda b, c: (b * nc + cidx(c), 0, 0)),
                  pl.BlockSpec((n, SSD_WIDTH), lambda b, c: (b * nc + cidx(c), 0)),
                  full((4, 384)), full((1, 384)), full((4, 512)), full((1, 512)), full((SUBLANES, LANES)), full((1, 384))],
        out_specs=[rowblk(384), rowblk(384), rowblk(512), rowblk(128),
                   full((SUBLANES, 384)), full((SUBLANES, 512)), full((SUBLANES, LANES)), full((1, 384))],
        out_shape=[jax.ShapeDtypeStruct((t, 384), MXU_DTYPE), jax.ShapeDtypeStruct((t, 384), MXU_DTYPE),
                   jax.ShapeDtypeStruct((t, 512), MXU_DTYPE), jax.ShapeDtypeStruct((t, 128), MXU_DTYPE),
                   jax.ShapeDtypeStruct((SUBLANES, 384), F32), jax.ShapeDtypeStruct((SUBLANES, 512), F32),
                   jax.ShapeDtypeStruct((SUBLANES, LANES), F32), jax.ShapeDtypeStruct((1, 384), F32)],
        scratch_shapes=[pltpu.VMEM((SSD_WIDTH, SSD_STATE), F32), pltpu.VMEM((n + SUBLANES, 384), F32),
                        pltpu.VMEM((n + SUBLANES, 512), F32), pltpu.VMEM((n, SSD_WIDTH), F32),
                        pltpu.VMEM((n, 384), F32), pltpu.VMEM((n, 512), F32),
                        pltpu.VMEM((n + SUBLANES, 384), F32), pltpu.VMEM((n + SUBLANES, 512), F32),
                        pltpu.VMEM((SUBLANES, 384), F32), pltpu.VMEM((SUBLANES, 512), F32)],
        compiler_params=_params(2),
    )(h, h, h, h, h, h, hstate, dmix, cwx, cbx, cwb, cbb, vec, nw)


def _adamw_math(w, g, m, v):
    m = ADAM_B1 * m + (1.0 - ADAM_B1) * g
    v = ADAM_B2 * v + (1.0 - ADAM_B2) * (g * g)
    m_hat = m / (1.0 - ADAM_B1 ** ADAM_STEP)
    v_hat = v / (1.0 - ADAM_B2 ** ADAM_STEP)
    delta = -ADAM_LR * (m_hat / (jnp.sqrt(v_hat) + ADAM_EPS) + ADAM_WD * w)
    return delta, m, v


def _adamw(w, g, m, v, name, tr=256):
    rows, cols = w.shape
    tr = rows if rows <= tr else tr
    assert rows % tr == 0, (rows, tr)

    def body(w_ref, g_ref, m_ref, v_ref, d_ref, nm_ref, nv_ref):
        d, nm, nv = _adamw_math(w_ref[...], g_ref[...], m_ref[...], v_ref[...])
        d_ref[...] = d
        nm_ref[...] = nm
        nv_ref[...] = nv

    spec = pl.BlockSpec((tr, cols), lambda i: (i, 0))
    shp = jax.ShapeDtypeStruct((rows, cols), F32)
    return pl.pallas_call(body, name=name, grid=(rows // tr,), in_specs=[spec] * 4, out_specs=[spec] * 3,
                          out_shape=[shp] * 3, compiler_params=_params(1))(w, g, m, v)


def _sum8(parts, name, tr=256):
    _, rows, cols = parts.shape
    tr = _pick(rows, tr) if rows % LANES == 0 else (rows if rows <= tr else tr)
    assert rows % tr == 0

    def body(p_ref, o_ref):
        acc = p_ref[0]
        for k in range(1, N_DEV):
            acc = acc + p_ref[k]
        o_ref[...] = acc

    return pl.pallas_call(body, name=name, grid=(rows // tr,),
                          in_specs=[pl.BlockSpec((N_DEV, tr, cols), lambda i: (0, i, 0))],
                          out_specs=pl.BlockSpec((tr, cols), lambda i: (i, 0)),
                          out_shape=jax.ShapeDtypeStruct((rows, cols), F32), compiler_params=_params(1))(parts)


MESH_ID = pl.DeviceIdType.MESH


def _flip(v, bit):
    return 1 - v if bit else v


def _all_gather_hbm(shard, name):
    def body(x_ref, out_ref, send_sems, recv_sems, local_sem):
        x, y, c = lax.axis_index("x"), lax.axis_index("y"), lax.axis_index("c")
        me, sibling = (x, y, c), (x, y, 1 - c)
        chips = [(1 - x, y), (x, 1 - y), (1 - x, 1 - y)]

        def slot(px, py, pc):
            return out_ref.at[4 * px + 2 * py + pc]

        def copy(k, block, to, src=None):
            return pltpu.make_async_remote_copy(
                src_ref=slot(*block) if src is None else src, dst_ref=slot(*block),
                send_sem=send_sems.at[k], recv_sem=recv_sems.at[k], device_id=to, device_id_type=MESH_ID)

        mine = pltpu.make_async_copy(x_ref, slot(*me), local_sem)
        mine.start()
        first = [copy(0, me, sibling, src=x_ref)]
        first += [copy(1 + j, me, (*chip, c), src=x_ref) for j, chip in enumerate(chips)]
        for cp in first:
            cp.start()
        passed = [copy(4 + j, (*chip, c), sibling) for j, chip in enumerate(chips)]
        for j, chip in enumerate(chips):
            copy(1 + j, (*chip, c), me).wait_recv()
            passed[j].start()
        copy(0, sibling, me).wait_recv()
        for j, chip in enumerate(chips):
            copy(4 + j, (*chip, 1 - c), me).wait_recv()
        for cp in first + passed:
            cp.wait_send()
        mine.wait()

    return pl.pallas_call(
        body, name=name, out_shape=jax.ShapeDtypeStruct((N_DEV,) + shard.shape, shard.dtype),
        in_specs=[pl.BlockSpec(memory_space=pl.ANY)], out_specs=pl.BlockSpec(memory_space=pl.ANY),
        scratch_shapes=[pltpu.SemaphoreType.DMA((7,)), pltpu.SemaphoreType.DMA((7,)), pltpu.SemaphoreType.DMA],
        compiler_params=pltpu.CompilerParams(has_side_effects=True),
    )(shard)


def _exchange_shards(parts, name):
    def body(g_ref, out_ref, send_sems, recv_sems, local_sem):
        x, y, c = lax.axis_index("x"), lax.axis_index("y"), lax.axis_index("c")
        me = 4 * x + 2 * y + c
        mine = pltpu.make_async_copy(g_ref.at[me], out_ref.at[me], local_sem)
        mine.start()
        sends = []
        for k in range(1, N_DEV):
            px, py, pc = _flip(x, k & 4), _flip(y, k & 2), _flip(c, k & 1)
            peer = 4 * px + 2 * py + pc
            cp = pltpu.make_async_remote_copy(
                src_ref=g_ref.at[peer], dst_ref=out_ref.at[me], send_sem=send_sems.at[k - 1],
                recv_sem=recv_sems.at[k - 1], device_id=(px, py, pc), device_id_type=MESH_ID)
            cp.start()
            sends.append((cp, peer, (px, py, pc)))
        for k, (cp, peer, pid) in enumerate(sends):
            pltpu.make_async_remote_copy(
                src_ref=g_ref.at[me], dst_ref=out_ref.at[peer], send_sem=send_sems.at[k], recv_sem=recv_sems.at[k],
                device_id=pid, device_id_type=MESH_ID).wait_recv()
        for cp, _, _ in sends:
            cp.wait_send()
        mine.wait()

    return pl.pallas_call(
        body, name=name, out_shape=jax.ShapeDtypeStruct(parts.shape, parts.dtype),
        in_specs=[pl.BlockSpec(memory_space=pl.ANY)], out_specs=pl.BlockSpec(memory_space=pl.ANY),
        scratch_shapes=[pltpu.SemaphoreType.DMA((7,)), pltpu.SemaphoreType.DMA((7,)), pltpu.SemaphoreType.DMA],
        compiler_params=pltpu.CompilerParams(has_side_effects=True),
    )(parts)


def _all_reduce_small(vec, name):
    rows, cols = vec.shape

    def body(x_ref, out_ref, gbuf, send_sems, recv_sems):
        x, y, c = lax.axis_index("x"), lax.axis_index("y"), lax.axis_index("c")
        me, sibling = (x, y, c), (x, y, 1 - c)
        chips = [(1 - x, y), (x, 1 - y), (1 - x, 1 - y)]

        def slot(px, py, pc):
            return gbuf.at[4 * px + 2 * py + pc]

        def copy(k, block, to, src=None):
            return pltpu.make_async_remote_copy(
                src_ref=slot(*block) if src is None else src, dst_ref=slot(*block),
                send_sem=send_sems.at[k], recv_sem=recv_sems.at[k], device_id=to, device_id_type=MESH_ID)

        first = [copy(0, me, sibling, src=x_ref)]
        first += [copy(1 + j, me, (*chip, c), src=x_ref) for j, chip in enumerate(chips)]
        for cp in first:
            cp.start()
        gbuf[4 * x + 2 * y + c] = x_ref[...]
        passed = [copy(4 + j, (*chip, c), sibling) for j, chip in enumerate(chips)]
        for j, chip in enumerate(chips):
            copy(1 + j, (*chip, c), me).wait_recv()
            passed[j].start()
        copy(0, sibling, me).wait_recv()
        for j, chip in enumerate(chips):
            copy(4 + j, (*chip, 1 - c), me).wait_recv()
        for cp in first + passed:
            cp.wait_send()
        acc = gbuf[0]
        for k in range(1, N_DEV):
            acc = acc + gbuf[k]
        out_ref[...] = acc

    return pl.pallas_call(
        body, name=name, out_shape=jax.ShapeDtypeStruct((rows, cols), F32),
        in_specs=[pl.BlockSpec(memory_space=pltpu.VMEM)], out_specs=pl.BlockSpec(memory_space=pltpu.VMEM),
        scratch_shapes=[pltpu.VMEM((N_DEV, rows, cols), F32), pltpu.SemaphoreType.DMA((7,)), pltpu.SemaphoreType.DMA((7,))],
        compiler_params=pltpu.CompilerParams(has_side_effects=True, vmem_limit_bytes=VMEM_LIMIT_BYTES),
    )(vec)


_COL_POOL, _COL_Z, _COL_XBC, _COL_DT, _COL_Q, _COL_K, _COL_V = 0, 256, 640, 1536, 1542, 1926, 2310
_H_SEGMENTS = ((_COL_XBC + SSD_WIDTH, 512), (_COL_POOL, 256), (_COL_Q, 384), (_COL_K, 384), (_COL_V, 384),
               (_COL_Z, 384), (_COL_XBC, 384), (_COL_DT, 6))


def _h_from_orig(w):
    parts = [w[..., o:o + n] for o, n in _H_SEGMENTS]
    pad = jnp.zeros(w.shape[:-1] + (H_COLS - IN_COLS,), w.dtype)
    return jnp.concatenate(parts + [pad], axis=-1)


def _h_to_orig(w):
    offs, o = {}, 0
    for orig, n in _H_SEGMENTS:
        offs[orig] = (o, n)
        o += n
    order = sorted(offs)
    return jnp.concatenate([w[..., offs[k][0]:offs[k][0] + offs[k][1]] for k in order], axis=-1)


def _interleave(w):
    lead = w.shape[:-1]
    nt = D_FF // GLU_TILE
    return jnp.swapaxes(w.reshape(lead + (2, nt, GLU_TILE)), -3, -2).reshape(lead + (2 * D_FF,))


def _deinterleave(w):
    lead = w.shape[:-1]
    nt = D_FF // GLU_TILE
    return jnp.swapaxes(w.reshape(lead + (nt, 2, GLU_TILE)), -3, -2).reshape(lead + (2 * D_FF,))


def _mix_rows_from_orig(w):
    return jnp.concatenate([w[256:640], w[640:1024], w[0:256]], axis=0)


def _mix_rows_to_orig(w):
    return jnp.concatenate([w[768:1024], w[0:384], w[384:768]], axis=0)


def _xbc_split(w):
    return w[..., :SSD_WIDTH], w[..., SSD_WIDTH:]


def _layer_fwd(x, p_l, wt, sp, nb, s):
    h = _mm(x, wt["w_in"], "nn", F32, "mm_in", tm=1024, tn=256)
    pool_out = _pool_fwd(h, wt["pool_bd"], sp["pool_scale"], nb, s, "pool_fwd")
    ssd_out, hstate = _ssd_fwd(h, sp["cwx"], sp["cbx"], sp["cwb"], sp["cbb"], sp["ssd_vec"], sp["ssd_norm_w"], nb, s, "ssd_fwd")
    sb_out = _sb_fwd(h, nb, s, "sb_fwd")
    mixcat = jnp.concatenate([ssd_out, sb_out, pool_out], axis=1)
    mix = _mm(mixcat, wt["w_out"], "nn", F32, "mm_out", tm=1024, tn=512)
    x1, r1 = _ln_fwd(x, mix, sp["ln1"], "ln1_fwd")
    up = _mm(x1, wt["w_up"], "nn", F32, "mm_up", tm=1024, tn=512)
    act = _glu_fwd(up, sp["ffn_cw"], sp["ffn_cb"], nb, s, "glu_fwd")
    ffn = _mm(act, wt["w_down"], "nn", F32, "mm_down", tm=1024, tn=512, tk=1408)
    gp = _mm(x1, wt["w_gate"], "nn", F32, "mm_gate", tm=1024, tn=512)
    pp = _mm(p_l, wt["w_proj"], "nn", F32, "mm_proj", tm=1024, tn=512)
    x2, r2 = _ln_fwd(x1, ffn, sp["ln2"], "ln2_fwd", gp=gp, pp=pp)
    return x2, dict(x=x, h=h, hstate=hstate, mixcat=mixcat, r1=r1, x1=x1, up=up, act=act, gp=gp, pp=pp, r2=r2)


def _layer_bwd(dx2, p_l, sv, wt, sp, nb, s):
    dr2, dgp, dpp, st2 = _ln_bwd(sv["r2"], sp["ln2"], dx2, "ln2_bwd", gp=sv["gp"], pp=sv["pp"])
    g_down = _mm(sv["act"], dr2, "tn", F32, "wg_down", tm=1408, tn=1024, tk=512)
    dact = _mm(dr2, wt["w_down"], "nt", F32, "dg_down", tm=1024, tn=256)
    dup, ffn_acc = _glu_bwd(sv["up"], dact, sp["ffn_cw"], sp["ffn_cb"], nb, s, "glu_bwd")
    g_up = _mm(sv["x1"], dup, "tn", F32, "wg_up", tm=1024, tn=512, tk=512)
    g_gate = _mm(sv["x1"], dgp, "tn", F32, "wg_gate", tm=1024, tn=1024, tk=512)
    g_proj = _mm(p_l, dpp, "tn", F32, "wg_proj", tm=256, tn=1024, tk=512)
    t1 = _mm(dgp, wt["w_gate"], "nt", F32, "dg_gate", tm=1024, tn=512, add=dr2, add_coef=ALPHA)
    dx1 = _mm(dup, wt["w_up"], "nt", F32, "dg_up", tm=1024, tn=512, tk=512, add=t1)
    dr1, st1 = _ln_bwd(sv["r1"], sp["ln1"], dx1, "ln1_bwd")
    g_out = _mm(sv["mixcat"], dr1, "tn", F32, "wg_out", tm=1024, tn=1024, tk=512)
    dmix = _mm(dr1, wt["w_out"], "nt", F32, "dg_out", tm=1024, tn=512)
    du, g_pool_bd, g_pool_scale = _pool_bwd(sv["h"], dmix, wt["pool_bd"], sp["pool_scale"], nb, s, "pool_bwd")
    dz, dxs, dbc, ddt, gx, gb, gv, gn = _ssd_bwd(sv["h"], sv["hstate"], dmix, sp["cwx"], sp["cbx"], sp["cwb"], sp["cbb"],
                                                  sp["ssd_vec"], sp["ssd_norm_w"], nb, s, "ssd_bwd")
    dq, dk, dv = _sb_bwd(sv["h"], dmix, nb, s, "sb_bwd")
    dh = jnp.concatenate([dbc, du, dq.astype(MXU_DTYPE), dk.astype(MXU_DTYPE), dv.astype(MXU_DTYPE), dz, dxs, ddt], axis=1)
    g_in = _mm(sv["x"], dh, "tn", F32, "wg_in", tm=1024, tn=256, tk=512)
    dx = _mm(dh, wt["w_in"], "nt", F32, "dg_in", tm=1024, tn=512, tk=1408, add=dr1, add_coef=ALPHA)
    small = dict(
        pool_w=jnp.stack([g_pool_bd[HEAD_DIM * g:HEAD_DIM * (g + 1), HEAD_DIM * g:HEAD_DIM * (g + 1)] for g in range(4)]),
        pool_scale=g_pool_scale[0],
        ssd_conv_w=jnp.concatenate([gx[0:4], gb[0:4]], axis=1),
        ssd_conv_b=jnp.concatenate([gx[4], gb[4]], axis=0),
        ssd_dt_bias=gv[0, :SSD_HEADS],
        ssd_a_log=gv[1, :SSD_HEADS] * sp["ssd_vec"][1, :SSD_HEADS],
        ssd_d=gv[2, :SSD_HEADS],
        ssd_norm_w=gn[0],
        ln1_g=st1[0], ln1_b=st1[1], ln2_g=st2[0], ln2_b=st2[1],
        ffn_conv_w=_deinterleave(ffn_acc[0:3]),
        ffn_conv_b=_deinterleave(ffn_acc[3]),
    )
    big = dict(w_in=g_in, w_out=g_out, ffn_w_up=g_up, ffn_w_down=g_down, ple_w_gate=g_gate, ple_w_proj=g_proj)
    return dx, big, small


def _layer_params(i, full, rep):
    pool_bd = jnp.zeros((POOL_WIDTH, POOL_WIDTH), F32)
    for g in range(4):
        pool_bd = lax.dynamic_update_slice(pool_bd, rep["pool_w"][i, g], (HEAD_DIM * g, HEAD_DIM * g))
    wt = dict(w_in=full["w_in"][i], w_out=full["w_out"][i], w_up=full["ffn_w_up"][i], w_down=full["ffn_w_down"][i],
              w_gate=full["ple_w_gate"][i], w_proj=full["ple_w_proj"][i], pool_bd=pool_bd.astype(MXU_DTYPE))
    cwx, cwb = _xbc_split(rep["ssd_conv_w"][i])
    cbx, cbb = _xbc_split(rep["ssd_conv_b"][i][None, :])
    vec = jnp.zeros((SUBLANES, LANES), F32)
    vec = vec.at[0, :SSD_HEADS].set(rep["ssd_dt_bias"][i])
    vec = vec.at[1, :SSD_HEADS].set(-jnp.exp(rep["ssd_a_log"][i]))
    vec = vec.at[2, :SSD_HEADS].set(rep["ssd_d"][i])
    sp = dict(pool_scale=rep["pool_scale"][i][None, :], cwx=cwx, cbx=cbx, cwb=cwb, cbb=cbb, ssd_vec=vec,
              ssd_norm_w=rep["ssd_norm_w"][i][None, :],
              ln1=jnp.stack([rep["ln1_g"][i], rep["ln1_b"][i]]), ln2=jnp.stack([rep["ln2_g"][i], rep["ln2_b"][i]]),
              ffn_cw=_interleave(rep["ffn_conv_w"][i]), ffn_cb=_interleave(rep["ffn_conv_b"][i][None, :]))
    return wt, sp


def _local_step(x, p, target, full, rep):
    nb, s, d = x.shape
    t = nb * s
    xf = x.reshape(t, d)
    saved, params = [], []
    for i in range(DEPTH):
        wt, sp = _layer_params(i, full, rep)
        params.append((wt, sp))
        xf, sv = _layer_fwd(xf, p[i].reshape(t, PLE_DIM), wt, sp, nb, s)
        saved.append(sv)
    dy, loss = _loss_grad(xf, target.reshape(t, d), "loss")
    bigs, smalls = [None] * DEPTH, [None] * DEPTH
    for i in reversed(range(DEPTH)):
        wt, sp = params[i]
        dy, bigs[i], smalls[i] = _layer_bwd(dy, p[i].reshape(t, PLE_DIM), saved[i], wt, sp, nb, s)
    return loss, dy.reshape(nb, s, d), bigs, smalls


BIG = ("w_in", "w_out", "ffn_w_up", "ffn_w_down", "ple_w_gate", "ple_w_proj")
SMALL_REPLICATED = ("pool_w", "pool_scale", "ssd_conv_b", "ssd_dt_bias", "ssd_a_log", "ssd_d", "ssd_norm_w",
                    "ln1_g", "ln1_b", "ffn_conv_b", "ln2_g", "ln2_b")
SMALL_SHARDED = ("ssd_conv_w", "ffn_conv_w")
WEIGHTS = ("w_in", "pool_w", "pool_scale", "ssd_conv_w", "ssd_conv_b", "ssd_dt_bias", "ssd_a_log", "ssd_d", "ssd_norm_w",
           "w_out", "ln1_g", "ln1_b", "ffn_w_up", "ffn_conv_w", "ffn_conv_b", "ffn_w_down", "ln2_g", "ln2_b",
           "ple_w_gate", "ple_w_proj")
FLAT_COLS = 1024


def _to_rows(a, cols):
    f = a.reshape(-1)
    pad = (-f.shape[0]) % cols
    if pad:
        f = jnp.concatenate([f, jnp.zeros((pad,), f.dtype)])
    return f.reshape(-1, cols)


def _pack_rows(arrs, cols, row_mult):
    rows = [_to_rows(a, cols) for a in arrs]
    flat = jnp.concatenate(rows, axis=0)
    pad = (-flat.shape[0]) % row_mult
    if pad:
        flat = jnp.concatenate([flat, jnp.zeros((pad, cols), flat.dtype)], axis=0)
    return flat


def _unpack_rows(flat, shapes, cols):
    out, r = [], 0
    for shp in shapes:
        n = 1
        for v in shp:
            n *= v
        nr = -(-n // cols)
        out.append(flat[r:r + nr].reshape(-1)[:n].reshape(shp))
        r += nr
    return out


def _big_shard_for_gather(name, w):
    if name == "w_in":
        w = _h_from_orig(w)
    return w.astype(MXU_DTYPE).reshape(DEPTH, -1, FLAT_COLS)


def _big_full_from_gathered(name, g):
    if name == "w_in":
        return jnp.swapaxes(g.reshape(N_DEV, DEPTH, 128, H_COLS), 0, 1).reshape(DEPTH, D_MODEL, H_COLS)
    if name == "w_out":
        w = jnp.swapaxes(g.reshape(N_DEV, DEPTH, 128, D_MODEL), 0, 1).reshape(DEPTH, D_MODEL, D_MODEL)
        return jnp.concatenate([w[:, 256:640], w[:, 640:1024], w[:, 0:256]], axis=1)
    if name == "ffn_w_up":
        w = jnp.transpose(g.reshape(N_DEV, DEPTH, D_MODEL, 704), (1, 2, 0, 3)).reshape(DEPTH, D_MODEL, 2 * D_FF)
        return _interleave(w)
    if name == "ffn_w_down":
        return jnp.swapaxes(g.reshape(N_DEV, DEPTH, 352, D_MODEL), 0, 1).reshape(DEPTH, D_FF, D_MODEL)
    if name == "ple_w_gate":
        return jnp.swapaxes(g.reshape(N_DEV, DEPTH, 128, D_MODEL), 0, 1).reshape(DEPTH, D_MODEL, D_MODEL)
    if name == "ple_w_proj":
        return jnp.transpose(g.reshape(N_DEV, DEPTH, PLE_DIM, 128), (1, 2, 0, 3)).reshape(DEPTH, PLE_DIM, D_MODEL)
    raise KeyError(name)


def _big_grad_to_shards(name, g):
    if name == "w_in":
        return g.reshape(N_DEV, -1, FLAT_COLS)
    if name == "w_out":
        return _mix_rows_to_orig(g).reshape(N_DEV, -1, FLAT_COLS)
    if name == "ffn_w_up":
        w = _deinterleave(g).reshape(D_MODEL, N_DEV, 704)
        return jnp.swapaxes(w, 0, 1).reshape(N_DEV, -1, FLAT_COLS)
    if name in ("ffn_w_down", "ple_w_gate"):
        return g.reshape(N_DEV, -1, FLAT_COLS)
    if name == "ple_w_proj":
        return jnp.swapaxes(g.reshape(PLE_DIM, N_DEV, 128), 0, 1).reshape(N_DEV, -1, FLAT_COLS)
    raise KeyError(name)


def kernel(x, p, w_in, pool_w, pool_scale, ssd_conv_w, ssd_conv_b, ssd_dt_bias, ssd_a_log, ssd_d, ssd_norm_w, w_out, ln1_g, ln1_b, ffn_w_up, ffn_conv_w, ffn_conv_b, ffn_w_down, ln2_g, ln2_b, ple_w_gate, ple_w_proj, loss_target, m_w_in, m_pool_w, m_pool_scale, m_ssd_conv_w, m_ssd_conv_b, m_ssd_dt_bias, m_ssd_a_log, m_ssd_d, m_ssd_norm_w, m_w_out, m_ln1_g, m_ln1_b, m_ffn_w_up, m_ffn_conv_w, m_ffn_conv_b, m_ffn_w_down, m_ln2_g, m_ln2_b, m_ple_w_gate, m_ple_w_proj, v_w_in, v_pool_w, v_pool_scale, v_ssd_conv_w, v_ssd_conv_b, v_ssd_dt_bias, v_ssd_a_log, v_ssd_d, v_ssd_norm_w, v_w_out, v_ln1_g, v_ln1_b, v_ffn_w_up, v_ffn_conv_w, v_ffn_conv_b, v_ffn_w_down, v_ln2_g, v_ln2_b, v_ple_w_gate, v_ple_w_proj):
    wts = dict(w_in=w_in, pool_w=pool_w, pool_scale=pool_scale, ssd_conv_w=ssd_conv_w, ssd_conv_b=ssd_conv_b,
               ssd_dt_bias=ssd_dt_bias, ssd_a_log=ssd_a_log, ssd_d=ssd_d, ssd_norm_w=ssd_norm_w, w_out=w_out, ln1_g=ln1_g,
               ln1_b=ln1_b, ffn_w_up=ffn_w_up, ffn_conv_w=ffn_conv_w, ffn_conv_b=ffn_conv_b, ffn_w_down=ffn_w_down,
               ln2_g=ln2_g, ln2_b=ln2_b, ple_w_gate=ple_w_gate, ple_w_proj=ple_w_proj)
    mom_m = dict(w_in=m_w_in, pool_w=m_pool_w, pool_scale=m_pool_scale, ssd_conv_w=m_ssd_conv_w, ssd_conv_b=m_ssd_conv_b,
                 ssd_dt_bias=m_ssd_dt_bias, ssd_a_log=m_ssd_a_log, ssd_d=m_ssd_d, ssd_norm_w=m_ssd_norm_w, w_out=m_w_out,
                 ln1_g=m_ln1_g, ln1_b=m_ln1_b, ffn_w_up=m_ffn_w_up, ffn_conv_w=m_ffn_conv_w, ffn_conv_b=m_ffn_conv_b,
                 ffn_w_down=m_ffn_w_down, ln2_g=m_ln2_g, ln2_b=m_ln2_b, ple_w_gate=m_ple_w_gate, ple_w_proj=m_ple_w_proj)
    mom_v = dict(w_in=v_w_in, pool_w=v_pool_w, pool_scale=v_pool_scale, ssd_conv_w=v_ssd_conv_w, ssd_conv_b=v_ssd_conv_b,
                 ssd_dt_bias=v_ssd_dt_bias, ssd_a_log=v_ssd_a_log, ssd_d=v_ssd_d, ssd_norm_w=v_ssd_norm_w, w_out=v_w_out,
                 ln1_g=v_ln1_g, ln1_b=v_ln1_b, ffn_w_up=v_ffn_w_up, ffn_conv_w=v_ffn_conv_w, ffn_conv_b=v_ffn_conv_b,
                 ffn_w_down=v_ffn_w_down, ln2_g=v_ln2_g, ln2_b=v_ln2_b, ple_w_gate=v_ple_w_gate, ple_w_proj=v_ple_w_proj)
    me = 4 * lax.axis_index("x") + 2 * lax.axis_index("y") + lax.axis_index("c")

    shards = [_big_shard_for_gather(n, wts[n]) for n in BIG]
    rows = [sh.shape[1] for sh in shards]
    gathered = _all_gather_hbm(jnp.concatenate(shards, axis=1), "gather_weights")
    full, r0 = {}, 0
    for n, nr in zip(BIG, rows):
        full[n] = _big_full_from_gathered(n, gathered[:, :, r0:r0 + nr])
        r0 += nr
    small_sh = _pack_rows([wts[n] for n in SMALL_SHARDED], LANES, SUBLANES)
    small_g = _all_gather_hbm(small_sh, "gather_conv_weights")
    rep = {n: wts[n] for n in SMALL_REPLICATED}
    per_dev = [_unpack_rows(small_g[j], [wts[n].shape for n in SMALL_SHARDED], LANES) for j in range(N_DEV)]
    for k, n in enumerate(SMALL_SHARDED):
        rep[n] = jnp.concatenate([per_dev[j][k] for j in range(N_DEV)], axis=-1)

    loss_loc, grad_x, bigs, smalls = _local_step(x, p, loss_target, full, rep)

    parts = jnp.concatenate([_big_grad_to_shards(n, bigs[i][n]) for i in range(DEPTH) for n in BIG], axis=1)
    summed = _sum8(_exchange_shards(parts, "exchange_grads"), "sum_grads")
    big_shapes = {"w_in": (128, H_COLS), "w_out": (128, D_MODEL), "ffn_w_up": (D_MODEL, 704), "ffn_w_down": (352, D_MODEL),
                  "ple_w_gate": (128, D_MODEL), "ple_w_proj": (PLE_DIM, 128)}
    pieces = _unpack_rows(summed, [big_shapes[n] for _ in range(DEPTH) for n in BIG], FLAT_COLS)
    grads = {}
    for k, n in enumerate(BIG):
        g = jnp.stack([pieces[i * len(BIG) + k] for i in range(DEPTH)])
        grads[n] = _h_to_orig(g) if n == "w_in" else g
    small_names = SMALL_REPLICATED + SMALL_SHARDED
    small_full_shapes = [rep[n].shape for n in small_names]
    small_vec = _pack_rows([jnp.stack([smalls[i][n] for i in range(DEPTH)]) for n in small_names] + [loss_loc[0, :1]],
                           LANES, SUBLANES)
    small_sum = _all_reduce_small(small_vec, "allreduce_small")
    small_out = _unpack_rows(small_sum, small_full_shapes + [(1,)], LANES)
    loss = small_out[-1][0]
    for n, g in zip(small_names, small_out[:-1]):
        if n in SMALL_SHARDED:
            width = wts[n].shape[-1]
            g = lax.dynamic_slice_in_dim(g, me * width, width, axis=g.ndim - 1)
        grads[n] = g

    delta, new_m, new_v = {}, {}, {}
    for n in BIG:
        shp = wts[n].shape
        two_d = lambda a: a.reshape(-1, shp[-1])
        tr = {"w_in": 128, "ffn_w_down": 352}.get(n, 256)
        d_, m_, v_ = _adamw(two_d(wts[n]), two_d(grads[n]), two_d(mom_m[n]), two_d(mom_v[n]), "adamw_" + n, tr=tr)
        delta[n], new_m[n], new_v[n] = d_.reshape(shp), m_.reshape(shp), v_.reshape(shp)
    packs = [_pack_rows([src[n] for n in small_names], LANES, SUBLANES) for src in (wts, grads, mom_m, mom_v)]
    outs = _adamw(*packs, "adamw_small", tr=packs[0].shape[0])
    shapes = [wts[n].shape for n in small_names]
    for dst, flat in zip((delta, new_m, new_v), outs):
        for n, a in zip(small_names, _unpack_rows(flat, shapes, LANES)):
            dst[n] = a
    return (loss, grad_x, *[grads[n] for n in WEIGHTS], *[delta[n] for n in WEIGHTS],
            *[new_m[n] for n in WEIGHTS], *[new_v[n] for n in WEIGHTS])
```

```python
import functools

import jax
import jax.numpy as jnp
from jax import lax
from jax.experimental import pallas as pl
from jax.experimental.pallas import tpu as pltpu

F32 = jnp.float32
BF16 = jnp.bfloat16
MXU_DTYPE = jnp.bfloat16

D_MODEL = 1024
DEPTH = 4
PLE_DIM = 256
ALPHA = (2 * DEPTH) ** 0.25
LN_EPS = 1e-5
RMS_EPS = 1e-6
HEAD_DIM = 64
POOL_WIDTH = 256
POOL_WINDOWS = (2, 4, 8, 16)
SSD_WIDTH = 384
SSD_HEADS = 6
SSD_STATE = 128
SSD_XBC = 896
SB_WIDTH = 384
IN_COLS = 2694
D_FF = 2816
N_DEV = 8

ADAM_LR = 0.001
ADAM_B1 = 0.9
ADAM_B2 = 0.999
ADAM_EPS = 1e-08
ADAM_WD = 0.01
ADAM_STEP = 10

LANES = 128
SUBLANES = 8
VMEM_LIMIT_BYTES = 56 * 1024 * 1024

H_COLS = 2816
H_BC = 0
H_POOL = 512
H_Q = 768
H_K = 1152
H_V = 1536
H_Z = 1920
H_XS = 2304
H_DT = 2688
SSD_CHUNK = 128
QB = 256
GLU_TILE = 256

NN = ((1,), (0,))
NT = ((1,), (1,))
TN = ((0,), (0,))


def _dot(a, b, dims=NN):
    return lax.dot_general(a.astype(MXU_DTYPE), b.astype(MXU_DTYPE), (dims, ((), ())), preferred_element_type=F32)


def _dot_exact01(x, m01, dims=NN, x_left=True, terms=3):
    acc = None
    r = x
    for _ in range(terms):
        hi = r.astype(BF16)
        ops = (hi, m01) if x_left else (m01, hi)
        part = lax.dot_general(ops[0], ops[1], (dims, ((), ())), preferred_element_type=F32)
        acc = part if acc is None else acc + part
        r = r - hi.astype(F32)
    return acc


def _sigmoid(v):
    return 1.0 / (1.0 + jnp.exp(-v))


def _silu(v):
    return v * _sigmoid(v)


def _dsilu(v):
    s = _sigmoid(v)
    return s * (1.0 + v * (1.0 - s))


def _softplus(v):
    return jnp.maximum(v, 0.0) + jnp.log(1.0 + jnp.exp(-jnp.abs(v)))


def _params(n_axes, side_effects=False):
    return pltpu.CompilerParams(dimension_semantics=("arbitrary",) * n_axes, vmem_limit_bytes=VMEM_LIMIT_BYTES,
                                has_side_effects=side_effects)


MESH_ID = pl.DeviceIdType.MESH
_ANY = pl.BlockSpec(memory_space=pl.ANY)


def _flip(v, bit):
    return 1 - v if bit else v


def _comm_counts(comm):
    return (0, 0) if comm is None else (len(comm["inputs"]), len(comm["out_shapes"]))


def _comm_call_args(comm):
    if comm is None:
        return [], [], [], []
    n = comm["n_xfers"]
    sems = [pltpu.SemaphoreType.DMA(((N_DEV - 1) * n,)), pltpu.SemaphoreType.DMA(((N_DEV - 1) * n,)),
            pltpu.SemaphoreType.DMA((n,))]
    return list(comm["inputs"]), [_ANY] * len(comm["out_shapes"]), list(comm["out_shapes"]), sems


def _comm_descs(comm, in_refs, tail_refs, with_recvs=True):
    n_out = len(comm["out_shapes"])
    out_refs, (send_sems, recv_sems, local_sems) = tail_refs[:n_out], tail_refs[n_out:n_out + 3]
    xfers = comm["xfers"](in_refs, out_refs)
    n = len(xfers)
    assert n == comm["n_xfers"]
    x, y, c = lax.axis_index("x"), lax.axis_index("y"), lax.axis_index("c")
    me = 4 * x + 2 * y + c
    local = [pltpu.make_async_copy(src_for(me), dst_for(me), local_sems.at[t]) for t, (src_for, dst_for) in enumerate(xfers)]
    sends, recvs = [], []
    for k in range(1, N_DEV):
        pid = (_flip(x, k & 4), _flip(y, k & 2), _flip(c, k & 1))
        peer = 4 * pid[0] + 2 * pid[1] + pid[2]
        for t, (src_for, dst_for) in enumerate(xfers):
            idx = (k - 1) * n + t
            sends.append(pltpu.make_async_remote_copy(
                src_ref=src_for(peer), dst_ref=dst_for(me), send_sem=send_sems.at[idx], recv_sem=recv_sems.at[idx],
                device_id=pid, device_id_type=MESH_ID))
            if with_recvs:
                recvs.append(pltpu.make_async_remote_copy(
                    src_ref=src_for(peer), dst_ref=dst_for(peer), send_sem=send_sems.at[idx], recv_sem=recv_sems.at[idx],
                    device_id=pid, device_id_type=MESH_ID))
    return local, sends, recvs


def _comm_start(descs):
    local, sends, _ = descs
    for cp in local + sends:
        cp.start()


def _comm_wait(descs):
    local, sends, recvs = descs
    for cp in recvs:
        cp.wait_recv()
    for cp in sends:
        cp.wait_send()
    for cp in local:
        cp.wait()


def _comm_hosted(comm, in_refs, tail_refs, grid):
    if comm is None:
        return
    ids = [pl.program_id(a) for a in range(len(grid))]
    first = functools.reduce(jnp.logical_and, [i == 0 for i in ids])
    last = functools.reduce(jnp.logical_and, [i == g - 1 for i, g in zip(ids, grid)])

    @pl.when(first)
    def _():
        _comm_start(_comm_descs(comm, in_refs, tail_refs, with_recvs=False))

    @pl.when(last)
    def _():
        _comm_wait(_comm_descs(comm, in_refs, tail_refs))


def _comm_call(comm, name):
    n_in = len(comm["inputs"])

    def body(*refs):
        descs = _comm_descs(comm, refs[:n_in], refs[n_in:])
        _comm_start(descs)
        _comm_wait(descs)

    c_in, c_specs, c_shapes, c_scratch = _comm_call_args(comm)
    return pl.pallas_call(body, name=name, in_specs=[_ANY] * n_in, out_specs=c_specs, out_shape=c_shapes,
                          scratch_shapes=c_scratch, compiler_params=pltpu.CompilerParams(has_side_effects=True))(*c_in)


def _rows(ref, j, n):
    return ref.at[pl.ds(pl.multiple_of(j * n, SUBLANES), n), :]


def _gather_job(sh, conv=None):
    conv = list(conv or [])
    sds = jax.ShapeDtypeStruct
    out_shapes = [sds((D_MODEL, H_COLS), MXU_DTYPE), sds((D_MODEL, D_MODEL), MXU_DTYPE), sds((N_DEV, D_MODEL, 704), MXU_DTYPE),
                  sds((D_FF, D_MODEL), MXU_DTYPE), sds((D_MODEL, D_MODEL), MXU_DTYPE), sds((PLE_DIM, D_MODEL), MXU_DTYPE)]
    out_shapes += [sds((N_DEV,) + a.shape, a.dtype) for a in conv]

    def xfers(ins, outs):
        whole = lambda a: (lambda j: a)
        r = [(whole(ins[0]), lambda j: _rows(outs[0], j, 128)),
             (whole(ins[1]), lambda j: _rows(outs[1], lax.rem(j + 6, N_DEV), 128)),
             (whole(ins[2]), lambda j: outs[2].at[j]),
             (whole(ins[3]), lambda j: _rows(outs[3], j, 352)),
             (whole(ins[4]), lambda j: _rows(outs[4], j, 128)),
             (whole(ins[5]), lambda j: outs[5].at[:, pl.ds(pl.multiple_of(j * LANES, LANES), LANES)])]
        for t in range(len(conv)):
            r.append((whole(ins[6 + t]), lambda j, o=outs[6 + t]: o.at[j]))
        return r

    return dict(inputs=[sh[n] for n in BIG] + conv, out_shapes=out_shapes, xfers=xfers, n_xfers=6 + len(conv))


def _exchange_job(g):
    sds = jax.ShapeDtypeStruct
    shard_shapes = [(128, H_COLS), (128, D_MODEL), (D_MODEL, 704), (352, D_MODEL), (128, D_MODEL), (PLE_DIM, LANES)]

    def xfers(ins, outs):
        slot = lambda o: (lambda j: o.at[j])
        return [(lambda j: _rows(ins[0], j, 128), slot(outs[0])),
                (lambda j: _rows(ins[1], lax.rem(j + 6, N_DEV), 128), slot(outs[1])),
                (lambda j: ins[2].at[j], slot(outs[2])),
                (lambda j: _rows(ins[3], j, 352), slot(outs[3])),
                (lambda j: _rows(ins[4], j, 128), slot(outs[4])),
                (lambda j: ins[5].at[:, pl.ds(pl.multiple_of(j * LANES, LANES), LANES)], slot(outs[5]))]

    return dict(inputs=[g[n] for n in BIG], out_shapes=[sds((N_DEV,) + s, F32) for s in shard_shapes], xfers=xfers, n_xfers=6)


def _pick(n, pref):
    if n <= pref:
        return n
    for t in range(pref - pref % LANES, 0, -LANES):
        if n % t == 0:
            return t
    raise ValueError((n, pref))


def _mm(a, b, mode, out_dtype, name, tm=512, tn=512, tk=1024, add=None, add_coef=1.0):
    if mode == "nn":
        (m, k), (k2, n) = a.shape, b.shape
    elif mode == "nt":
        (m, k), (n, k2) = a.shape, b.shape
    else:
        (k, m), (k2, n) = a.shape, b.shape
    assert k == k2, (a.shape, b.shape, mode)
    tm, tn, tk = _pick(m, tm), _pick(n, tn), _pick(k, tk)
    nk = k // tk
    dims = {"nn": NN, "nt": NT, "tn": TN}[mode]

    def body(*refs):
        if add is None:
            a_ref, b_ref, o_ref, acc_ref = refs
            add_ref = None
        else:
            a_ref, b_ref, add_ref, o_ref, acc_ref = refs
        kk = pl.program_id(2)

        @pl.when(kk == 0)
        def _():
            acc_ref[...] = jnp.zeros_like(acc_ref)

        acc_ref[...] += _dot(a_ref[...], b_ref[...], dims)

        @pl.when(kk == nk - 1)
        def _():
            r = acc_ref[...]
            if add_ref is not None:
                r = r + add_coef * add_ref[...]
            o_ref[...] = r.astype(out_dtype)

    if mode == "tn":
        a_spec = pl.BlockSpec((tk, tm), lambda i, j, kk: (kk, i))
    else:
        a_spec = pl.BlockSpec((tm, tk), lambda i, j, kk: (i, kk))
    if mode == "nt":
        b_spec = pl.BlockSpec((tn, tk), lambda i, j, kk: (j, kk))
    else:
        b_spec = pl.BlockSpec((tk, tn), lambda i, j, kk: (kk, j))
    o_spec = pl.BlockSpec((tm, tn), lambda i, j, kk: (i, j))
    in_specs = [a_spec, b_spec] + ([o_spec] if add is not None else [])
    args = (a, b) + ((add,) if add is not None else ())
    return pl.pallas_call(
        body, name=name, grid=(m // tm, n // tn, nk), in_specs=in_specs, out_specs=o_spec,
        out_shape=jax.ShapeDtypeStruct((m, n), out_dtype), scratch_shapes=[pltpu.VMEM((tm, tn), F32)],
        compiler_params=_params(3),
    )(*args)


def _ln_fwd(x, add, gb, name, gp=None, pp=None, tr=512):
    t, d = x.shape
    tr = _pick(t, tr)
    with_ple = gp is not None

    def body(*refs):
        if with_ple:
            x_ref, a_ref, gp_ref, pp_ref, gb_ref, y_ref, r_ref = refs
        else:
            x_ref, a_ref, gb_ref, y_ref, r_ref = refs
        r = ALPHA * x_ref[...] + a_ref[...]
        if with_ple:
            r = r + _sigmoid(gp_ref[...]) * pp_ref[...]
        mu = jnp.mean(r, axis=1, keepdims=True)
        xc = r - mu
        var = jnp.mean(xc * xc, axis=1, keepdims=True)
        y_ref[...] = xc * lax.rsqrt(var + LN_EPS) * gb_ref[0:1, :] + gb_ref[1:2, :]
        r_ref[...] = r

    row = pl.BlockSpec((tr, d), lambda i: (i, 0))
    vec = pl.BlockSpec((2, d), lambda i: (0, 0))
    n_row = 4 if with_ple else 2
    args = (x, add) + ((gp, pp) if with_ple else ()) + (gb,)
    return pl.pallas_call(
        body, name=name, grid=(t // tr,), in_specs=[row] * n_row + [vec], out_specs=[row, row],
        out_shape=[jax.ShapeDtypeStruct((t, d), F32)] * 2, compiler_params=_params(1),
    )(*args)


def _ln_bwd(r, gb, dy, name, gp=None, pp=None, tr=512):
    t, d = r.shape
    tr = _pick(t, tr)
    with_ple = gp is not None

    def body(*refs):
        if with_ple:
            r_ref, dy_ref, gp_ref, pp_ref, gb_ref, dr_ref, dgp_ref, dpp_ref, st_ref = refs
        else:
            r_ref, dy_ref, gb_ref, dr_ref, st_ref = refs
        i = pl.program_id(0)

        @pl.when(i == 0)
        def _():
            st_ref[...] = jnp.zeros_like(st_ref)

        rv = r_ref[...]
        dy_v = dy_ref[...]
        mu = jnp.mean(rv, axis=1, keepdims=True)
        xc = rv - mu
        var = jnp.mean(xc * xc, axis=1, keepdims=True)
        rstd = lax.rsqrt(var + LN_EPS)
        xhat = xc * rstd
        dxh = dy_v * gb_ref[0:1, :]
        m1 = jnp.mean(dxh, axis=1, keepdims=True)
        m2 = jnp.mean(dxh * xhat, axis=1, keepdims=True)
        dr = rstd * (dxh - m1 - xhat * m2)
        dr_ref[...] = dr
        rid = lax.broadcasted_iota(jnp.int32, (2, d), 0)
        dg = jnp.sum(dy_v * xhat, axis=0, keepdims=True)
        db = jnp.sum(dy_v, axis=0, keepdims=True)
        st_ref[...] += jnp.where(rid == 0, dg, db)
        if with_ple:
            sg = _sigmoid(gp_ref[...])
            ppv = pp_ref[...]
            dgp_ref[...] = (dr * ppv * sg * (1.0 - sg)).astype(dgp_ref.dtype)
            dpp_ref[...] = (dr * sg).astype(dpp_ref.dtype)

    row = pl.BlockSpec((tr, d), lambda i: (i, 0))
    vec = pl.BlockSpec((2, d), lambda i: (0, 0))
    if with_ple:
        in_specs, args = [row] * 4 + [vec], (r, dy, gp, pp, gb)
        out_specs = [row, row, row, vec]
        out_shape = [jax.ShapeDtypeStruct((t, d), F32), jax.ShapeDtypeStruct((t, d), MXU_DTYPE),
                     jax.ShapeDtypeStruct((t, d), MXU_DTYPE), jax.ShapeDtypeStruct((2, d), F32)]
    else:
        in_specs, args = [row] * 2 + [vec], (r, dy, gb)
        out_specs = [row, vec]
        out_shape = [jax.ShapeDtypeStruct((t, d), F32), jax.ShapeDtypeStruct((2, d), F32)]
    return pl.pallas_call(body, name=name, grid=(t // tr,), in_specs=in_specs, out_specs=out_specs,
                          out_shape=out_shape, compiler_params=_params(1))(*args)


def _loss_grad(y, target, name, tr=512):
    t, d = y.shape
    tr = _pick(t, tr)

    def body(y_ref, t_ref, dy_ref, l_ref):
        i = pl.program_id(0)

        @pl.when(i == 0)
        def _():
            l_ref[...] = jnp.zeros_like(l_ref)

        e = y_ref[...] - t_ref[...]
        dy_ref[...] = e * (1.0 / d)
        per_tok = jnp.mean(e * e, axis=1, keepdims=True)
        l_ref[...] += 0.5 * jnp.sum(per_tok, axis=0, keepdims=True)

    row = pl.BlockSpec((tr, d), lambda i: (i, 0))
    acc = pl.BlockSpec((SUBLANES, LANES), lambda i: (0, 0))
    return pl.pallas_call(body, name=name, grid=(t // tr,), in_specs=[row, row], out_specs=[row, acc],
                          out_shape=[jax.ShapeDtypeStruct((t, d), F32), jax.ShapeDtypeStruct((SUBLANES, LANES), F32)],
                          compiler_params=_params(1))(y, target)


def _shift_down(v, k, row):
    return jnp.where(row >= k, pltpu.roll(v, k, 0), 0.0)


def _shift_up(v, k, row):
    n = v.shape[0]
    return jnp.where(row < n - k, pltpu.roll(v, n - k, 0), 0.0)


def _pool_window(lane):
    grp = lane // HEAD_DIM
    return jnp.where(grp == 0, POOL_WINDOWS[0], jnp.where(grp == 1, POOL_WINDOWS[1],
                     jnp.where(grp == 2, POOL_WINDOWS[2], POOL_WINDOWS[3])))


def _pool_select(lane, s2, s4, s8, s16):
    grp = lane // HEAD_DIM
    return jnp.where(grp == 0, s2, jnp.where(grp == 1, s4, jnp.where(grp == 2, s8, s16)))


def _pooled(u, row, lane):
    s2 = u + _shift_down(u, 1, row)
    s4 = s2 + _shift_down(s2, 2, row)
    s8 = s4 + _shift_down(s4, 4, row)
    s16 = s8 + _shift_down(s8, 8, row)
    cnt = jnp.minimum(row + 1, _pool_window(lane)).astype(F32)
    return _pool_select(lane, s2, s4, s8, s16) / cnt - u, cnt


def _pool_fwd(h, wbd, scale, nb, s, name):
    def body(u_ref, w_ref, sc_ref, o_ref):
        u = u_ref[...]
        row = lax.broadcasted_iota(jnp.int32, u.shape, 0)
        lane = lax.broadcasted_iota(jnp.int32, u.shape, 1)
        pooled, _ = _pooled(u, row, lane)
        o_ref[...] = (_dot(pooled, w_ref[...]) * sc_ref[...]).astype(o_ref.dtype)

    wb = POOL_WIDTH
    return pl.pallas_call(
        body, name=name, grid=(nb,),
        in_specs=[pl.BlockSpec((s, wb), lambda b: (b, H_POOL // wb)), pl.BlockSpec((wb, wb), lambda b: (0, 0)),
                  pl.BlockSpec((1, wb), lambda b: (0, 0))],
        out_specs=pl.BlockSpec((s, wb), lambda b: (b, 0)),
        out_shape=jax.ShapeDtypeStruct((nb * s, wb), MXU_DTYPE), compiler_params=_params(1),
    )(h, wbd, scale)


def _pool_bwd(h, dmix, wbd, scale, nb, s, name):
    wb = POOL_WIDTH

    def body(u_ref, do_ref, w_ref, sc_ref, du_ref, dw_ref, ds_ref):
        b = pl.program_id(0)

        @pl.when(b == 0)
        def _():
            dw_ref[...] = jnp.zeros_like(dw_ref)
            ds_ref[...] = jnp.zeros_like(ds_ref)

        u = u_ref[...]
        row = lax.broadcasted_iota(jnp.int32, u.shape, 0)
        lane = lax.broadcasted_iota(jnp.int32, u.shape, 1)
        pooled, cnt = _pooled(u, row, lane)
        mixed = _dot(pooled, w_ref[...])
        do = do_ref[...]
        ds_ref[...] += jnp.sum(do * mixed, axis=0, keepdims=True)
        dm = do * sc_ref[...]
        dw_ref[...] += _dot(pooled, dm, TN)
        dpool = _dot(dm, w_ref[...], NT)
        qv = dpool / cnt
        f2 = qv + _shift_up(qv, 1, row)
        f4 = f2 + _shift_up(f2, 2, row)
        f8 = f4 + _shift_up(f4, 4, row)
        f16 = f8 + _shift_up(f8, 8, row)
        du_ref[...] = (_pool_select(lane, f2, f4, f8, f16) - dpool).astype(du_ref.dtype)

    return pl.pallas_call(
        body, name=name, grid=(nb,),
        in_specs=[pl.BlockSpec((s, wb), lambda b: (b, H_POOL // wb)), pl.BlockSpec((s, wb), lambda b: (b, 3)),
                  pl.BlockSpec((wb, wb), lambda b: (0, 0)), pl.BlockSpec((1, wb), lambda b: (0, 0))],
        out_specs=[pl.BlockSpec((s, wb), lambda b: (b, 0)), pl.BlockSpec((wb, wb), lambda b: (0, 0)),
                   pl.BlockSpec((1, wb), lambda b: (0, 0))],
        out_shape=[jax.ShapeDtypeStruct((nb * s, wb), MXU_DTYPE), jax.ShapeDtypeStruct((wb, wb), F32),
                   jax.ShapeDtypeStruct((1, wb), F32)],
        compiler_params=_params(1),
    )(h, dmix, wbd, scale)


def _glu_conv(x, w_ref, b_ref, row):
    return (b_ref[...] + w_ref[2:3, :] * x + w_ref[1:2, :] * _shift_down(x, 1, row)
            + w_ref[0:1, :] * _shift_down(x, 2, row))


def _glu_fwd(up, cw, cb, nb, s, name):
    wt = 2 * GLU_TILE
    nt = up.shape[1] // wt

    def body(u_ref, w_ref, b_ref, o_ref):
        x = u_ref[...]
        row = lax.broadcasted_iota(jnp.int32, x.shape, 0)
        c = _glu_conv(x, w_ref, b_ref, row)
        o_ref[...] = (_silu(c[:, :GLU_TILE]) * c[:, GLU_TILE:]).astype(o_ref.dtype)

    return pl.pallas_call(
        body, name=name, grid=(nt, nb),
        in_specs=[pl.BlockSpec((s, wt), lambda j, b: (b, j)), pl.BlockSpec((3, wt), lambda j, b: (0, j)),
                  pl.BlockSpec((1, wt), lambda j, b: (0, j))],
        out_specs=pl.BlockSpec((s, GLU_TILE), lambda j, b: (b, j)),
        out_shape=jax.ShapeDtypeStruct((nb * s, nt * GLU_TILE), MXU_DTYPE), compiler_params=_params(2),
    )(up, cw, cb)


def _glu_bwd(up, dact, cw, cb, nb, s, name):
    wt = 2 * GLU_TILE
    nt = up.shape[1] // wt

    def body(u_ref, da_ref, w_ref, b_ref, du_ref, acc_ref):
        b = pl.program_id(1)

        @pl.when(b == 0)
        def _():
            acc_ref[...] = jnp.zeros_like(acc_ref)

        x = u_ref[...]
        row = lax.broadcasted_iota(jnp.int32, x.shape, 0)
        x1 = _shift_down(x, 1, row)
        x2 = _shift_down(x, 2, row)
        c = b_ref[...] + w_ref[2:3, :] * x + w_ref[1:2, :] * x1 + w_ref[0:1, :] * x2
        gate, val = c[:, :GLU_TILE], c[:, GLU_TILE:]
        da = da_ref[...]
        dc = jnp.concatenate([da * val * _dsilu(gate), da * _silu(gate)], axis=1)
        dx = (w_ref[2:3, :] * dc + w_ref[1:2, :] * _shift_up(dc, 1, row) + w_ref[0:1, :] * _shift_up(dc, 2, row))
        du_ref[...] = dx.astype(du_ref.dtype)
        rid = lax.broadcasted_iota(jnp.int32, (SUBLANES, wt), 0)
        dw0 = jnp.sum(dc * x2, axis=0, keepdims=True)
        dw1 = jnp.sum(dc * x1, axis=0, keepdims=True)
        dw2 = jnp.sum(dc * x, axis=0, keepdims=True)
        db = jnp.sum(dc, axis=0, keepdims=True)
        acc_ref[...] += (jnp.where(rid == 0, dw0, 0.0) + jnp.where(rid == 1, dw1, 0.0)
                         + jnp.where(rid == 2, dw2, 0.0) + jnp.where(rid == 3, db, 0.0))

    return pl.pallas_call(
        body, name=name, grid=(nt, nb),
        in_specs=[pl.BlockSpec((s, wt), lambda j, b: (b, j)), pl.BlockSpec((s, GLU_TILE), lambda j, b: (b, j)),
                  pl.BlockSpec((3, wt), lambda j, b: (0, j)), pl.BlockSpec((1, wt), lambda j, b: (0, j))],
        out_specs=[pl.BlockSpec((s, wt), lambda j, b: (b, j)), pl.BlockSpec((SUBLANES, wt), lambda j, b: (0, j))],
        out_shape=[jax.ShapeDtypeStruct((nb * s, nt * wt), MXU_DTYPE), jax.ShapeDtypeStruct((SUBLANES, nt * wt), F32)],
        compiler_params=_params(2),
    )(up, dact, cw, cb)


def _sb_masks():
    row = lax.broadcasted_iota(jnp.int32, (QB, QB), 0)
    col = lax.broadcasted_iota(jnp.int32, (QB, QB), 1)
    return row, col


def _sb_fwd(h, nb, s, name, comm=None):
    nq = s // QB
    scale = HEAD_DIM ** -0.5
    n_in, n_out = _comm_counts(comm)

    def body(q_ref, k_ref, v_ref, *rest):
        o_ref = rest[n_in]
        i = pl.program_id(2)
        _comm_hosted(comm, rest[:n_in], rest[n_in + 1:], (nb, 3, nq))
        row, col = _sb_masks()
        low = col < row
        later = (row > col).astype(BF16)
        sls = [slice(hd * HEAD_DIM, (hd + 1) * HEAD_DIM) for hd in range(2)]
        qs = [(q_ref[:, sl] * scale).astype(MXU_DTYPE) for sl in sls]

        def block(hd, r0, ct, diagonal):
            kj = k_ref[pl.ds(r0, QB), sls[hd]]
            vj = v_ref[pl.ds(r0, QB), sls[hd]]
            z = _dot(qs[hd], kj, NT)
            ln = -_softplus(z)
            if diagonal:
                ln = jnp.where(low, ln, 0.0)
            tail = ct + _dot_exact01(ln, later, terms=2)
            w = jnp.exp(z + ln + tail)
            if diagonal:
                w = jnp.where(low, w, 0.0)
            return _dot(w, vj), jnp.sum(ln, axis=1, keepdims=True)

        zero = jnp.zeros((QB, 1), F32)
        r_diag = pl.multiple_of(i * QB, QB)
        a0, c0 = block(0, r_diag, zero, True)
        a1, c1 = block(1, r_diag, zero, True)

        def step(jj, carry):
            a0, c0, a1, c1 = carry
            r0 = pl.multiple_of((i - 1 - jj) * QB, QB)
            d0, s0 = block(0, r0, c0, False)
            d1, s1 = block(1, r0, c1, False)
            return a0 + d0, c0 + s0, a1 + d1, c1 + s1

        a0, _, a1, _ = lax.fori_loop(0, i, step, (a0, c0, a1, c1))
        o_ref[:, sls[0]] = a0.astype(o_ref.dtype)
        o_ref[:, sls[1]] = a1.astype(o_ref.dtype)

    qspec = lambda off: pl.BlockSpec((QB, LANES), lambda b, p, i: (b * nq + i, off // LANES + p))
    kvspec = lambda off: pl.BlockSpec((s, LANES), lambda b, p, i: (b, off // LANES + p))
    c_in, c_specs, c_shapes, c_scratch = _comm_call_args(comm)
    res = pl.pallas_call(
        body, name=name, grid=(nb, 3, nq), in_specs=[qspec(H_Q), kvspec(H_K), kvspec(H_V)] + [_ANY] * n_in,
        out_specs=[pl.BlockSpec((QB, LANES), lambda b, p, i: (b * nq + i, p))] + c_specs,
        out_shape=[jax.ShapeDtypeStruct((nb * s, SB_WIDTH), MXU_DTYPE)] + c_shapes, scratch_shapes=c_scratch,
        compiler_params=_params(3, comm is not None),
    )(h, h, h, *c_in)
    return res[0], res[1:]


def _sb_bwd(h, dmix, nb, s, name, comm=None):
    nq = s // QB
    scale = HEAD_DIM ** -0.5
    n_in, n_out = _comm_counts(comm)

    def body(q_ref, k_ref, v_ref, do_ref, *rest):
        dq_ref, dk_ref, dv_ref = rest[n_in:n_in + 3]
        p_buf, ls_buf = rest[n_in + 3 + n_out:n_in + 5 + n_out]
        i = pl.program_id(2)
        _comm_hosted(comm, rest[:n_in], rest[n_in + 3:n_in + 3 + n_out] + rest[n_in + 5 + n_out:], (nb, 3, nq))

        @pl.when(i == 0)
        def _():
            dk_ref[...] = jnp.zeros_like(dk_ref)
            dv_ref[...] = jnp.zeros_like(dv_ref)

        row, col = _sb_masks()
        low = col < row
        later = (row > col).astype(BF16)
        earlier = (row < col).astype(BF16)
        sls = [slice(hd * HEAD_DIM, (hd + 1) * HEAD_DIM) for hd in range(2)]
        q_raw = [q_ref[:, sl].astype(MXU_DTYPE) for sl in sls]
        qs = [(q_ref[:, sl] * scale).astype(MXU_DTYPE) for sl in sls]
        do = [do_ref[:, sl].astype(MXU_DTYPE) for sl in sls]

        def down(hd, j, ct, diagonal):
            r0 = pl.multiple_of(j * QB, QB)
            kj = k_ref[pl.ds(r0, QB), sls[hd]]
            vj = v_ref[pl.ds(r0, QB), sls[hd]]
            z = _dot(qs[hd], kj, NT)
            ln = -_softplus(z)
            if diagonal:
                ln = jnp.where(low, ln, 0.0)
            ls = z + ln
            tail = ct + _dot_exact01(ln, later, terms=2)
            a = jnp.exp(ls + tail)
            if diagonal:
                a = jnp.where(low, a, 0.0)
                ls = jnp.where(low, ls, 0.0)
            p_buf[hd, j] = _dot(do[hd], vj, NT) * a
            ls_buf[hd, j] = ls
            dv_ref[pl.ds(r0, QB), sls[hd]] += _dot(a, do[hd], TN)
            return jnp.sum(ln, axis=1, keepdims=True)

        zero = jnp.zeros((QB, 1), F32)
        c0 = down(0, i, zero, True)
        c1 = down(1, i, zero, True)

        def down_step(jj, carry):
            c0, c1 = carry
            j = i - 1 - jj
            return c0 + down(0, j, c0, False), c1 + down(1, j, c1, False)

        lax.fori_loop(0, i, down_step, (c0, c1))

        def up(hd, j, dq, cp, diagonal):
            r0 = pl.multiple_of(j * QB, QB)
            pj = p_buf[hd, j]
            sg = jnp.exp(ls_buf[hd, j])
            cum = cp + _dot_exact01(pj, earlier)
            dz = (pj * (1.0 - sg) - cum * sg) * scale
            if diagonal:
                dz = jnp.where(low, dz, 0.0)
            kj = k_ref[pl.ds(r0, QB), sls[hd]]
            dk_ref[pl.ds(r0, QB), sls[hd]] += _dot(dz, q_raw[hd], TN)
            return dq + _dot(dz, kj), cp + jnp.sum(pj, axis=1, keepdims=True)

        def up_step(j, carry):
            dq0, cp0, dq1, cp1 = carry
            dq0, cp0 = up(0, j, dq0, cp0, False)
            dq1, cp1 = up(1, j, dq1, cp1, False)
            return dq0, cp0, dq1, cp1

        zq = jnp.zeros((QB, HEAD_DIM), F32)
        dq0, cp0, dq1, cp1 = lax.fori_loop(0, i, up_step, (zq, zero, zq, zero))
        dq0, _ = up(0, i, dq0, cp0, True)
        dq1, _ = up(1, i, dq1, cp1, True)
        dq_ref[:, sls[0]] = dq0
        dq_ref[:, sls[1]] = dq1

    qspec = lambda off: pl.BlockSpec((QB, LANES), lambda b, p, i: (b * nq + i, off // LANES + p))
    kvspec = lambda off: pl.BlockSpec((s, LANES), lambda b, p, i: (b, off // LANES + p))
    blk_out = pl.BlockSpec((QB, LANES), lambda b, p, i: (b * nq + i, p))
    seq_out = pl.BlockSpec((s, LANES), lambda b, p, i: (b, p))
    shp = jax.ShapeDtypeStruct((nb * s, SB_WIDTH), F32)
    c_in, c_specs, c_shapes, c_scratch = _comm_call_args(comm)
    res = pl.pallas_call(
        body, name=name, grid=(nb, 3, nq),
        in_specs=[qspec(H_Q), kvspec(H_K), kvspec(H_V), pl.BlockSpec((QB, LANES), lambda b, p, i: (b * nq + i, 3 + p))]
        + [_ANY] * n_in,
        out_specs=[blk_out, seq_out, seq_out] + c_specs, out_shape=[shp, shp, shp] + c_shapes,
        scratch_shapes=[pltpu.VMEM((2, nq, QB, QB), F32), pltpu.VMEM((2, nq, QB, QB), F32)] + c_scratch,
        compiler_params=_params(3, comm is not None),
    )(h, h, h, dmix, *c_in)
    return res[0], res[1], res[2], res[3:]


def _ssd_conv(cur_ref, halo_ref, w_ref, b_ref, ext_ref, first):
    n = SSD_CHUNK
    cur = cur_ref[...]
    ext_ref[0:SUBLANES, :] = jnp.where(first, 0.0, halo_ref[...])
    ext_ref[SUBLANES:SUBLANES + n, :] = cur
    return (b_ref[...] + w_ref[3:4, :] * cur + w_ref[2:3, :] * ext_ref[pl.ds(SUBLANES - 1, n), :]
            + w_ref[1:2, :] * ext_ref[pl.ds(SUBLANES - 2, n), :] + w_ref[0:1, :] * ext_ref[pl.ds(SUBLANES - 3, n), :])


def _ssd_tri():
    row = lax.broadcasted_iota(jnp.int32, (SSD_CHUNK, SSD_CHUNK), 0)
    col = lax.broadcasted_iota(jnp.int32, (SSD_CHUNK, SSD_CHUNK), 1)
    return row, col


def _ssd_specs(nc, rev):
    n = SSD_CHUNK
    hb = n // SUBLANES

    def cidx(c):
        return (nc - 1 - c) if rev else c

    def blk(width, off):
        return pl.BlockSpec((n, width), lambda b, c: (b * nc + cidx(c), off // width))

    def halo(width, off):
        return pl.BlockSpec((SUBLANES, width), lambda b, c: (jnp.maximum((b * nc + cidx(c)) * hb - 1, 0), off // width))

    def full(shape):
        return pl.BlockSpec(shape, lambda b, c: (0,) * len(shape))

    return cidx, blk, halo, full


def _ssd_core_fwd(x, bc, dt, acum, acum_t, a_row, d_row, h_prev_ref, tri):
    n = SSD_CHUNK
    heads = []
    for g in range(2):
        bm = bc[:, g * SSD_STATE:(g + 1) * SSD_STATE]
        cm = bc[:, 2 * SSD_STATE + g * SSD_STATE: 2 * SSD_STATE + (g + 1) * SSD_STATE]
        gmat = _dot(cm, bm, NT)
        for r in range(3):
            hh = g * 3 + r
            ac = acum[:, hh:hh + 1]
            ar = acum_t[hh:hh + 1, :]
            dec = jnp.where(tri, jnp.exp(jnp.minimum(ac - ar, 0.0)), 0.0)
            xh = x[:, hh * HEAD_DIM:(hh + 1) * HEAD_DIM]
            dth = dt[:, hh:hh + 1]
            xdt = xh * dth
            hp = h_prev_ref[hh * HEAD_DIM:(hh + 1) * HEAD_DIM, :]
            ea = jnp.exp(ac)
            m = gmat * dec
            yo = ea * _dot(cm, hp, NT)
            al = acum[n - 1:n, hh:hh + 1]
            w = jnp.exp(al - ac)
            y = _dot(m, xdt) + yo + d_row[:, hh:hh + 1] * xh
            heads.append(dict(g=g, hh=hh, bm=bm, cm=cm, gmat=gmat, dec=dec, xh=xh, dth=dth, xdt=xdt, hp=hp, ea=ea,
                              m=m, yo=yo, al=al, w=w, y=y))
    return heads


def _ssd_prep(xs_ref, xsh_ref, bc_ref, bch_ref, dt_ref, cwx_ref, cbx_ref, cwb_ref, cbb_ref, vec_ref, xe_ref, be_ref, first):
    pre_x = _ssd_conv(xs_ref, xsh_ref, cwx_ref, cbx_ref, xe_ref, first)
    pre_bc = _ssd_conv(bc_ref, bch_ref, cwb_ref, cbb_ref, be_ref, first)
    x = _silu(pre_x)
    bc = _silu(pre_bc)
    dt_pre = dt_ref[...] + vec_ref[0:1, :]
    dt = _softplus(dt_pre)
    a_row = vec_ref[1:2, :]
    amat = dt * a_row
    row, col = _ssd_tri()
    upper = (row <= col).astype(BF16)
    lower = (col <= row).astype(BF16)
    acum = _dot_exact01(amat, lower, NN, x_left=False)
    acum_t = _dot_exact01(amat, upper, TN, x_left=True)
    return pre_x, pre_bc, x, bc, dt_pre, dt, a_row, acum, acum_t, row, col, upper


def _ssd_gate_norm(y, z, nw):
    lane = lax.broadcasted_iota(jnp.int32, y.shape, 1)
    g0 = lane < SSD_WIDTH // 2
    hg = y * _silu(z)
    sq = hg * hg
    ms0 = jnp.sum(jnp.where(g0, sq, 0.0), axis=1, keepdims=True) * (2.0 / SSD_WIDTH)
    ms1 = jnp.sum(jnp.where(g0, 0.0, sq), axis=1, keepdims=True) * (2.0 / SSD_WIDTH)
    rs = jnp.where(g0, lax.rsqrt(ms0 + RMS_EPS), lax.rsqrt(ms1 + RMS_EPS))
    return hg, rs, g0


def _ssd_fwd(h, cwx, cbx, cwb, cbb, vec, nw, nb, s, name):
    n = SSD_CHUNK
    nc = s // n
    _, blk, halo, full = _ssd_specs(nc, False)

    def body(bc_ref, bch_ref, z_ref, xs_ref, xsh_ref, dt_ref, cwx_ref, cbx_ref, cwb_ref, cbb_ref, vec_ref, nw_ref,
             o_ref, hs_ref, h_scr, xe_ref, be_ref, y_scr):
        c = pl.program_id(1)

        @pl.when(c == 0)
        def _():
            h_scr[...] = jnp.zeros_like(h_scr)

        (_, _, x, bc, _, dt, a_row, acum, acum_t, row, col, _) = _ssd_prep(
            xs_ref, xsh_ref, bc_ref, bch_ref, dt_ref, cwx_ref, cbx_ref, cwb_ref, cbb_ref, vec_ref, xe_ref, be_ref, c == 0)
        hs_ref[...] = h_scr[...]
        heads = _ssd_core_fwd(x, bc, dt, acum, acum_t, a_row, vec_ref[2:3, :], hs_ref, col <= row)
        for hd in heads:
            sl = slice(hd["hh"] * HEAD_DIM, (hd["hh"] + 1) * HEAD_DIM)
            y_scr[:, sl] = hd["y"]
            h_scr[sl, :] = jnp.exp(hd["al"]) * hd["hp"] + _dot(hd["xdt"] * hd["w"], hd["bm"], TN)
        hg, rs, _ = _ssd_gate_norm(y_scr[...], z_ref[...], nw_ref[...])
        o_ref[...] = (hg * rs * nw_ref[...]).astype(o_ref.dtype)

    t = nb * s
    return pl.pallas_call(
        body, name=name, grid=(nb, nc),
        in_specs=[blk(512, H_BC), halo(512, H_BC), blk(384, H_Z), blk(384, H_XS), halo(384, H_XS), blk(128, H_DT),
                  full((4, 384)), full((1, 384)), full((4, 512)), full((1, 512)), full((SUBLANES, LANES)), full((1, 384))],
        out_specs=[pl.BlockSpec((n, SSD_WIDTH), lambda b, c: (b * nc + c, 0)),
                   pl.BlockSpec((None, SSD_WIDTH, SSD_STATE), lambda b, c: (b * nc + c, 0, 0))],
        out_shape=[jax.ShapeDtypeStruct((t, SSD_WIDTH), MXU_DTYPE),
                   jax.ShapeDtypeStruct((nb * nc, SSD_WIDTH, SSD_STATE), F32)],
        scratch_shapes=[pltpu.VMEM((SSD_WIDTH, SSD_STATE), F32), pltpu.VMEM((n + SUBLANES, 384), F32),
                        pltpu.VMEM((n + SUBLANES, 512), F32), pltpu.VMEM((n, SSD_WIDTH), F32)],
        compiler_params=_params(2),
    )(h, h, h, h, h, h, cwx, cbx, cwb, cbb, vec, nw)


def _ssd_bwd(h, hstate, dmix, cwx, cbx, cwb, cbb, vec, nw, nb, s, name):
    n = SSD_CHUNK
    nc = s // n
    cidx, blk, halo, full = _ssd_specs(nc, True)

    def body(bc_ref, bch_ref, z_ref, xs_ref, xsh_ref, dt_ref, hs_ref, do_ref, cwx_ref, cbx_ref, cwb_ref, cbb_ref,
             vec_ref, nw_ref, dz_ref, dxs_ref, dbc_ref, ddt_ref, gx_ref, gb_ref, gv_ref, gn_ref,
             dh_scr, xe_ref, be_ref, y_scr, dx_scr, dbc_scr, dxe_ref, dbe_ref, cx_ref, cb_ref):
        b = pl.program_id(0)
        c = pl.program_id(1)
        cc = nc - 1 - c

        @pl.when(jnp.logical_and(b == 0, c == 0))
        def _():
            gx_ref[...] = jnp.zeros_like(gx_ref)
            gb_ref[...] = jnp.zeros_like(gb_ref)
            gv_ref[...] = jnp.zeros_like(gv_ref)
            gn_ref[...] = jnp.zeros_like(gn_ref)

        @pl.when(c == 0)
        def _():
            dh_scr[...] = jnp.zeros_like(dh_scr)
            cx_ref[...] = jnp.zeros_like(cx_ref)
            cb_ref[...] = jnp.zeros_like(cb_ref)

        (pre_x, pre_bc, x, bc, dt_pre, dt, a_row, acum, acum_t, row, col, upper) = _ssd_prep(
            xs_ref, xsh_ref, bc_ref, bch_ref, dt_ref, cwx_ref, cbx_ref, cwb_ref, cbb_ref, vec_ref, xe_ref, be_ref, cc == 0)
        tri = col <= row
        d_row = vec_ref[2:3, :]
        heads = _ssd_core_fwd(x, bc, dt, acum, acum_t, a_row, d_row, hs_ref, tri)
        for hd in heads:
            y_scr[:, hd["hh"] * HEAD_DIM:(hd["hh"] + 1) * HEAD_DIM] = hd["y"]
        y = y_scr[...]
        z = z_ref[...]
        nwv = nw_ref[...]
        hg, rs, g0 = _ssd_gate_norm(y, z, nwv)
        do = do_ref[...]
        nrm = hg * rs
        gn_ref[...] += jnp.sum(do * nrm, axis=0, keepdims=True)
        dn = do * nwv
        dnn = dn * nrm
        mean0 = jnp.sum(jnp.where(g0, dnn, 0.0), axis=1, keepdims=True) * (2.0 / SSD_WIDTH)
        mean1 = jnp.sum(jnp.where(g0, 0.0, dnn), axis=1, keepdims=True) * (2.0 / SSD_WIDTH)
        dhg = rs * (dn - nrm * jnp.where(g0, mean0, mean1))
        dz_ref[...] = (dhg * y * _dsilu(z)).astype(dz_ref.dtype)
        dy = dhg * _silu(z)

        lane = lax.broadcasted_iota(jnp.int32, (n, LANES), 1)
        lane1 = lax.broadcasted_iota(jnp.int32, (1, LANES), 1)
        last_row = lax.broadcasted_iota(jnp.int32, (n, 1), 0) == n - 1
        dacum_col = jnp.zeros((n, LANES), F32)
        da_rowpart = jnp.zeros((n, LANES), F32)
        ddt = jnp.zeros((n, LANES), F32)
        dd_vec = jnp.zeros((1, LANES), F32)
        for g in range(2):
            dg = jnp.zeros((n, n), F32)
            dbm = jnp.zeros((n, SSD_STATE), F32)
            dcm = jnp.zeros((n, SSD_STATE), F32)
            for hd in heads[3 * g:3 * g + 3]:
                hh = hd["hh"]
                sl = slice(hh * HEAD_DIM, (hh + 1) * HEAD_DIM)
                dyh = dy[:, sl]
                dhn = dh_scr[sl, :]
                el = jnp.exp(hd["al"])
                dd_vec = dd_vec + jnp.where(lane1 == hh, jnp.sum(dyh * hd["xh"]), 0.0)
                dcm = dcm + hd["ea"] * _dot(dyh, hd["hp"])
                dm = _dot(dyh, hd["xdt"], NT)
                dg = dg + dm * hd["dec"]
                e = dm * hd["m"]
                t2 = _dot(hd["bm"], dhn, NT)
                dxdt = _dot(hd["m"], dyh, TN) + hd["w"] * t2
                dbm = dbm + hd["w"] * _dot(hd["xdt"], dhn)
                dw_w = jnp.sum(hd["xdt"] * t2, axis=1, keepdims=True) * hd["w"]
                d_el = jnp.sum(dhn * hd["hp"])
                col_part = (jnp.sum(dyh * hd["yo"], axis=1, keepdims=True) + jnp.sum(e, axis=1, keepdims=True) - dw_w
                            + jnp.where(last_row, d_el * el + jnp.sum(dw_w), 0.0))
                dacum_col = dacum_col + jnp.where(lane == hh, col_part, 0.0)
                neg_colsum = -jnp.sum(e, axis=0, keepdims=True)
                rev = jnp.sum(jnp.where(row <= col, neg_colsum, 0.0), axis=1, keepdims=True)
                da_rowpart = da_rowpart + jnp.where(lane == hh, rev, 0.0)
                dh_scr[sl, :] = el * dhn + _dot(dyh * hd["ea"], hd["cm"], TN)
                dx_scr[:, sl] = d_row[:, hh:hh + 1] * dyh + dxdt * hd["dth"]
                ddt = ddt + jnp.where(lane == hh, jnp.sum(dxdt * hd["xh"], axis=1, keepdims=True), 0.0)
            bm, cm = heads[3 * g]["bm"], heads[3 * g]["cm"]
            dcm = dcm + _dot(dg, bm)
            dbm = dbm + _dot(dg, cm, TN)
            dbc_scr[:, g * SSD_STATE:(g + 1) * SSD_STATE] = dbm
            dbc_scr[:, 2 * SSD_STATE + g * SSD_STATE:2 * SSD_STATE + (g + 1) * SSD_STATE] = dcm
        da_mat = _dot_exact01(dacum_col, upper, NN, x_left=False) + da_rowpart
        ddt = ddt + da_mat * a_row
        da_vec = jnp.sum(da_mat * dt, axis=0, keepdims=True)
        ddt_pre = jnp.where(lane < SSD_HEADS, ddt * _sigmoid(dt_pre), 0.0)
        ddt_ref[...] = ddt_pre.astype(ddt_ref.dtype)
        rid = lax.broadcasted_iota(jnp.int32, (SUBLANES, LANES), 0)
        gv_ref[...] += (jnp.where(rid == 0, jnp.sum(ddt_pre, axis=0, keepdims=True), 0.0)
                        + jnp.where(rid == 1, da_vec, 0.0) + jnp.where(rid == 2, dd_vec, 0.0))

        def conv_bwd(dpost, pre, w_ref, ext_ref, dext_ref, carry_ref, cur_ref, out_ref, g_ref, width):
            dco = dpost * _dsilu(pre)
            dext_ref[0:n, :] = dco
            dext_ref[n:n + SUBLANES, :] = carry_ref[...]
            out_ref[...] = (w_ref[3:4, :] * dco + w_ref[2:3, :] * dext_ref[pl.ds(1, n), :]
                            + w_ref[1:2, :] * dext_ref[pl.ds(2, n), :] + w_ref[0:1, :] * dext_ref[pl.ds(3, n), :]
                            ).astype(out_ref.dtype)
            carry_ref[...] = dco[0:SUBLANES, :]
            rid8 = lax.broadcasted_iota(jnp.int32, (SUBLANES, width), 0)
            acc = jnp.where(rid8 == 3, jnp.sum(dco * cur_ref[...], axis=0, keepdims=True), 0.0)
            for j in range(3):
                sh = ext_ref[pl.ds(SUBLANES - 3 + j, n), :]
                acc = acc + jnp.where(rid8 == j, jnp.sum(dco * sh, axis=0, keepdims=True), 0.0)
            acc = acc + jnp.where(rid8 == 4, jnp.sum(dco, axis=0, keepdims=True), 0.0)
            g_ref[...] += acc

        conv_bwd(dx_scr[...], pre_x, cwx_ref, xe_ref, dxe_ref, cx_ref, xs_ref, dxs_ref, gx_ref, 384)
        conv_bwd(dbc_scr[...], pre_bc, cwb_ref, be_ref, dbe_ref, cb_ref, bc_ref, dbc_ref, gb_ref, 512)

    t = nb * s
    rowblk = lambda width: pl.BlockSpec((n, width), lambda b, c: (b * nc + cidx(c), 0))
    return pl.pallas_call(
        body, name=name, grid=(nb, nc),
        in_specs=[blk(512, H_BC), halo(512, H_BC), blk(384, H_Z), blk(384, H_XS), halo(384, H_XS), blk(128, H_DT),
                  pl.BlockSpec((None, SSD_WIDTH, SSD_STATE), lambda b, c: (b * nc + cidx(c), 0, 0)),
                  pl.BlockSpec((n, SSD_WIDTH), lambda b, c: (b * nc + cidx(c), 0)),
                  full((4, 384)), full((1, 384)), full((4, 512)), full((1, 512)), full((SUBLANES, LANES)), full((1, 384))],
        out_specs=[rowblk(384), rowblk(384), rowblk(512), rowblk(128),
                   full((SUBLANES, 384)), full((SUBLANES, 512)), full((SUBLANES, LANES)), full((1, 384))],
        out_shape=[jax.ShapeDtypeStruct((t, 384), MXU_DTYPE), jax.ShapeDtypeStruct((t, 384), MXU_DTYPE),
                   jax.ShapeDtypeStruct((t, 512), MXU_DTYPE), jax.ShapeDtypeStruct((t, 128), MXU_DTYPE),
                   jax.ShapeDtypeStruct((SUBLANES, 384), F32), jax.ShapeDtypeStruct((SUBLANES, 512), F32),
                   jax.ShapeDtypeStruct((SUBLANES, LANES), F32), jax.ShapeDtypeStruct((1, 384), F32)],
        scratch_shapes=[pltpu.VMEM((SSD_WIDTH, SSD_STATE), F32), pltpu.VMEM((n + SUBLANES, 384), F32),
                        pltpu.VMEM((n + SUBLANES, 512), F32), pltpu.VMEM((n, SSD_WIDTH), F32),
                        pltpu.VMEM((n, 384), F32), pltpu.VMEM((n, 512), F32),
                        pltpu.VMEM((n + SUBLANES, 384), F32), pltpu.VMEM((n + SUBLANES, 512), F32),
                        pltpu.VMEM((SUBLANES, 384), F32), pltpu.VMEM((SUBLANES, 512), F32)],
        compiler_params=_params(2),
    )(h, h, h, h, h, h, hstate, dmix, cwx, cbx, cwb, cbb, vec, nw)


def _adamw_math(w, g, m, v):
    m = ADAM_B1 * m + (1.0 - ADAM_B1) * g
    v = ADAM_B2 * v + (1.0 - ADAM_B2) * (g * g)
    m_hat = m / (1.0 - ADAM_B1 ** ADAM_STEP)
    v_hat = v / (1.0 - ADAM_B2 ** ADAM_STEP)
    delta = -ADAM_LR * (m_hat / (jnp.sqrt(v_hat) + ADAM_EPS) + ADAM_WD * w)
    return delta, m, v


def _adamw(w, g, m, v, name, tr=256):
    rows, cols = w.shape
    tr = rows if rows <= tr else tr
    assert rows % tr == 0, (rows, tr)

    def body(w_ref, g_ref, m_ref, v_ref, d_ref, nm_ref, nv_ref):
        d, nm, nv = _adamw_math(w_ref[...], g_ref[...], m_ref[...], v_ref[...])
        d_ref[...] = d
        nm_ref[...] = nm
        nv_ref[...] = nv

    spec = pl.BlockSpec((tr, cols), lambda i: (i, 0))
    shp = jax.ShapeDtypeStruct((rows, cols), F32)
    return pl.pallas_call(body, name=name, grid=(rows // tr,), in_specs=[spec] * 4, out_specs=[spec] * 3,
                          out_shape=[shp] * 3, compiler_params=_params(1))(w, g, m, v)


def _sum8_layers(parts, name, tr):
    _, rows, cols = parts[0].shape
    assert rows % tr == 0
    nt = rows // tr

    def body(*refs):
        o_ref = refs[DEPTH]
        layer = pl.program_id(0)
        for l in range(DEPTH):
            @pl.when(layer == l)
            def _(l=l):
                acc = refs[l][0]
                for k in range(1, N_DEV):
                    acc = acc + refs[l][k]
                o_ref[...] = acc

    in_specs = [pl.BlockSpec((N_DEV, tr, cols), lambda a, i, l=l: (0, jnp.clip(i + (a - l) * nt, 0, nt - 1), 0))
                for l in range(DEPTH)]
    return pl.pallas_call(body, name=name, grid=(DEPTH, nt), in_specs=in_specs,
                          out_specs=pl.BlockSpec((None, tr, cols), lambda a, i: (a, i, 0)),
                          out_shape=jax.ShapeDtypeStruct((DEPTH, rows, cols), F32), compiler_params=_params(2))(*parts)


def _all_reduce_small(vec, name):
    rows, cols = vec.shape

    def body(x_ref, out_ref, gbuf, send_sems, recv_sems):
        x, y, c = lax.axis_index("x"), lax.axis_index("y"), lax.axis_index("c")
        me, sibling = (x, y, c), (x, y, 1 - c)
        chips = [(1 - x, y), (x, 1 - y), (1 - x, 1 - y)]

        def slot(px, py, pc):
            return gbuf.at[4 * px + 2 * py + pc]

        def copy(k, block, to, src=None):
            return pltpu.make_async_remote_copy(
                src_ref=slot(*block) if src is None else src, dst_ref=slot(*block),
                send_sem=send_sems.at[k], recv_sem=recv_sems.at[k], device_id=to, device_id_type=MESH_ID)

        first = [copy(0, me, sibling, src=x_ref)]
        first += [copy(1 + j, me, (*chip, c), src=x_ref) for j, chip in enumerate(chips)]
        for cp in first:
            cp.start()
        gbuf[4 * x + 2 * y + c] = x_ref[...]
        passed = [copy(4 + j, (*chip, c), sibling) for j, chip in enumerate(chips)]
        for j, chip in enumerate(chips):
            copy(1 + j, (*chip, c), me).wait_recv()
            passed[j].start()
        copy(0, sibling, me).wait_recv()
        for j, chip in enumerate(chips):
            copy(4 + j, (*chip, 1 - c), me).wait_recv()
        for cp in first + passed:
            cp.wait_send()
        acc = gbuf[0]
        for k in range(1, N_DEV):
            acc = acc + gbuf[k]
        out_ref[...] = acc

    return pl.pallas_call(
        body, name=name, out_shape=jax.ShapeDtypeStruct((rows, cols), F32),
        in_specs=[pl.BlockSpec(memory_space=pltpu.VMEM)], out_specs=pl.BlockSpec(memory_space=pltpu.VMEM),
        scratch_shapes=[pltpu.VMEM((N_DEV, rows, cols), F32), pltpu.SemaphoreType.DMA((7,)), pltpu.SemaphoreType.DMA((7,))],
        compiler_params=pltpu.CompilerParams(has_side_effects=True, vmem_limit_bytes=VMEM_LIMIT_BYTES),
    )(vec)


_COL_POOL, _COL_Z, _COL_XBC, _COL_DT, _COL_Q, _COL_K, _COL_V = 0, 256, 640, 1536, 1542, 1926, 2310
_H_SEGMENTS = ((_COL_XBC + SSD_WIDTH, 512), (_COL_POOL, 256), (_COL_Q, 384), (_COL_K, 384), (_COL_V, 384),
               (_COL_Z, 384), (_COL_XBC, 384), (_COL_DT, 6))


def _h_from_orig(w):
    parts = [w[..., o:o + n] for o, n in _H_SEGMENTS]
    pad = jnp.zeros(w.shape[:-1] + (H_COLS - IN_COLS,), w.dtype)
    return jnp.concatenate(parts + [pad], axis=-1)


def _h_to_orig(w):
    offs, o = {}, 0
    for orig, n in _H_SEGMENTS:
        offs[orig] = (o, n)
        o += n
    order = sorted(offs)
    return jnp.concatenate([w[..., offs[k][0]:offs[k][0] + offs[k][1]] for k in order], axis=-1)


def _interleave(w):
    lead = w.shape[:-1]
    nt = D_FF // GLU_TILE
    return jnp.swapaxes(w.reshape(lead + (2, nt, GLU_TILE)), -3, -2).reshape(lead + (2 * D_FF,))


def _deinterleave(w):
    lead = w.shape[:-1]
    nt = D_FF // GLU_TILE
    return jnp.swapaxes(w.reshape(lead + (nt, 2, GLU_TILE)), -3, -2).reshape(lead + (2 * D_FF,))


def _mix_rows_from_orig(w):
    return jnp.concatenate([w[256:640], w[640:1024], w[0:256]], axis=0)


def _mix_rows_to_orig(w):
    return jnp.concatenate([w[768:1024], w[0:384], w[384:768]], axis=0)


def _xbc_split(w):
    return w[..., :SSD_WIDTH], w[..., SSD_WIDTH:]


def _layer_fwd(x, p_l, wt, sp, nb, s, comm=None):
    h = _mm(x, wt["w_in"], "nn", F32, "mm_in", tm=1024, tn=256)
    pool_out = _pool_fwd(h, wt["pool_bd"], sp["pool_scale"], nb, s, "pool_fwd")
    ssd_out, hstate = _ssd_fwd(h, sp["cwx"], sp["cbx"], sp["cwb"], sp["cbb"], sp["ssd_vec"], sp["ssd_norm_w"], nb, s, "ssd_fwd")
    sb_out, comm_out = _sb_fwd(h, nb, s, "sb_fwd" if comm is None else "sb_fwd_gather", comm)
    mixcat = jnp.concatenate([ssd_out, sb_out, pool_out], axis=1)
    mix = _mm(mixcat, wt["w_out"], "nn", F32, "mm_out", tm=1024, tn=512)
    x1, r1 = _ln_fwd(x, mix, sp["ln1"], "ln1_fwd")
    up = _mm(x1, wt["w_up"], "nn", F32, "mm_up", tm=1024, tn=512)
    act = _glu_fwd(up, sp["ffn_cw"], sp["ffn_cb"], nb, s, "glu_fwd")
    ffn = _mm(act, wt["w_down"], "nn", F32, "mm_down", tm=1024, tn=512, tk=1408)
    gp = _mm(x1, wt["w_gate"], "nn", F32, "mm_gate", tm=1024, tn=512)
    pp = _mm(p_l, wt["w_proj"], "nn", F32, "mm_proj", tm=1024, tn=512)
    x2, r2 = _ln_fwd(x1, ffn, sp["ln2"], "ln2_fwd", gp=gp, pp=pp)
    return x2, dict(x=x, h=h, hstate=hstate, mixcat=mixcat, r1=r1, x1=x1, up=up, act=act, gp=gp, pp=pp, r2=r2), comm_out


def _layer_bwd(dx2, p_l, sv, wt, sp, nb, s, comm=None):
    dr2, dgp, dpp, st2 = _ln_bwd(sv["r2"], sp["ln2"], dx2, "ln2_bwd", gp=sv["gp"], pp=sv["pp"])
    g_down = _mm(sv["act"], dr2, "tn", F32, "wg_down", tm=1408, tn=1024, tk=512)
    dact = _mm(dr2, wt["w_down"], "nt", F32, "dg_down", tm=1024, tn=256)
    dup, ffn_acc = _glu_bwd(sv["up"], dact, sp["ffn_cw"], sp["ffn_cb"], nb, s, "glu_bwd")
    g_up = _mm(sv["x1"], dup, "tn", F32, "wg_up", tm=1024, tn=2816, tk=512)
    g_gate = _mm(sv["x1"], dgp, "tn", F32, "wg_gate", tm=1024, tn=1024, tk=512)
    g_proj = _mm(p_l, dpp, "tn", F32, "wg_proj", tm=256, tn=1024, tk=512)
    t1 = _mm(dgp, wt["w_gate"], "nt", F32, "dg_gate", tm=1024, tn=512, add=dr2, add_coef=ALPHA)
    dx1 = _mm(dup, wt["w_up"], "nt", F32, "dg_up", tm=1024, tn=512, tk=512, add=t1)
    dr1, st1 = _ln_bwd(sv["r1"], sp["ln1"], dx1, "ln1_bwd")
    g_out = _mm(sv["mixcat"], dr1, "tn", F32, "wg_out", tm=1024, tn=1024, tk=512)
    dmix = _mm(dr1, wt["w_out"], "nt", F32, "dg_out", tm=1024, tn=512)
    du, g_pool_bd, g_pool_scale = _pool_bwd(sv["h"], dmix, wt["pool_bd"], sp["pool_scale"], nb, s, "pool_bwd")
    dz, dxs, dbc, ddt, gx, gb, gv, gn = _ssd_bwd(sv["h"], sv["hstate"], dmix, sp["cwx"], sp["cbx"], sp["cwb"], sp["cbb"],
                                                  sp["ssd_vec"], sp["ssd_norm_w"], nb, s, "ssd_bwd")
    dq, dk, dv, comm_out = _sb_bwd(sv["h"], dmix, nb, s, "sb_bwd" if comm is None else "sb_bwd_exchange", comm)
    dh = jnp.concatenate([dbc, du, dq.astype(MXU_DTYPE), dk.astype(MXU_DTYPE), dv.astype(MXU_DTYPE), dz, dxs, ddt], axis=1)
    g_in = _mm(sv["x"], dh, "tn", F32, "wg_in", tm=1024, tn=2816, tk=512)
    dx = _mm(dh, wt["w_in"], "nt", F32, "dg_in", tm=1024, tn=512, tk=1408, add=dr1, add_coef=ALPHA)
    small = dict(
        pool_w=jnp.stack([g_pool_bd[HEAD_DIM * g:HEAD_DIM * (g + 1), HEAD_DIM * g:HEAD_DIM * (g + 1)] for g in range(4)]),
        pool_scale=g_pool_scale[0],
        ssd_conv_w=jnp.concatenate([gx[0:4], gb[0:4]], axis=1),
        ssd_conv_b=jnp.concatenate([gx[4], gb[4]], axis=0),
        ssd_dt_bias=gv[0, :SSD_HEADS],
        ssd_a_log=gv[1, :SSD_HEADS] * sp["ssd_vec"][1, :SSD_HEADS],
        ssd_d=gv[2, :SSD_HEADS],
        ssd_norm_w=gn[0],
        ln1_g=st1[0], ln1_b=st1[1], ln2_g=st2[0], ln2_b=st2[1],
        ffn_conv_w=_deinterleave(ffn_acc[0:3]),
        ffn_conv_b=_deinterleave(ffn_acc[3]),
    )
    big = dict(w_in=g_in, w_out=g_out, ffn_w_up=g_up, ffn_w_down=g_down, ple_w_gate=g_gate, ple_w_proj=g_proj)
    return dx, big, small, comm_out


def _layer_params(i, big, rep):
    pool_bd = jnp.zeros((POOL_WIDTH, POOL_WIDTH), F32)
    for g in range(4):
        pool_bd = lax.dynamic_update_slice(pool_bd, rep["pool_w"][i, g], (HEAD_DIM * g, HEAD_DIM * g))
    wt = dict(w_in=big["w_in"], w_out=big["w_out"], w_up=big["ffn_w_up"], w_down=big["ffn_w_down"],
              w_gate=big["ple_w_gate"], w_proj=big["ple_w_proj"], pool_bd=pool_bd.astype(MXU_DTYPE))
    cwx, cwb = _xbc_split(rep["ssd_conv_w"][i])
    cbx, cbb = _xbc_split(rep["ssd_conv_b"][i][None, :])
    vec = jnp.zeros((SUBLANES, LANES), F32)
    vec = vec.at[0, :SSD_HEADS].set(rep["ssd_dt_bias"][i])
    vec = vec.at[1, :SSD_HEADS].set(-jnp.exp(rep["ssd_a_log"][i]))
    vec = vec.at[2, :SSD_HEADS].set(rep["ssd_d"][i])
    sp = dict(pool_scale=rep["pool_scale"][i][None, :], cwx=cwx, cbx=cbx, cwb=cwb, cbb=cbb, ssd_vec=vec,
              ssd_norm_w=rep["ssd_norm_w"][i][None, :],
              ln1=jnp.stack([rep["ln1_g"][i], rep["ln1_b"][i]]), ln2=jnp.stack([rep["ln2_g"][i], rep["ln2_b"][i]]),
              ffn_cw=_interleave(rep["ffn_conv_w"][i]), ffn_cb=_interleave(rep["ffn_conv_b"][i][None, :]))
    return wt, sp


def _run_layers(x, p, target, big_w, rep, fwd_job=None, fwd_done=None, bwd_job=None, bwd_done=None):
    nb, s, d = x.shape
    t = nb * s
    xf = x.reshape(t, d)
    saved, params = [], []
    for i in range(DEPTH):
        wt, sp = _layer_params(i, big_w[i], rep)
        params.append((wt, sp))
        job = fwd_job(i) if fwd_job is not None else None
        xf, sv, res = _layer_fwd(xf, p[i].reshape(t, PLE_DIM), wt, sp, nb, s, job)
        if job is not None:
            fwd_done(i, res)
        saved.append(sv)
    dy, loss = _loss_grad(xf, target.reshape(t, d), "loss")
    bigs, smalls = [None] * DEPTH, [None] * DEPTH
    for i in reversed(range(DEPTH)):
        wt, sp = params[i]
        job = bwd_job(i, bigs) if bwd_job is not None else None
        dy, bigs[i], smalls[i], res = _layer_bwd(dy, p[i].reshape(t, PLE_DIM), saved[i], wt, sp, nb, s, job)
        if job is not None:
            bwd_done(i, res)
    return loss, dy.reshape(nb, s, d), bigs, smalls


def _local_step(x, p, target, full, rep):
    return _run_layers(x, p, target, [{n: full[n][i] for n in full} for i in range(DEPTH)], rep)


BIG = ("w_in", "w_out", "ffn_w_up", "ffn_w_down", "ple_w_gate", "ple_w_proj")
SMALL_REPLICATED = ("pool_w", "pool_scale", "ssd_conv_b", "ssd_dt_bias", "ssd_a_log", "ssd_d", "ssd_norm_w",
                    "ln1_g", "ln1_b", "ffn_conv_b", "ln2_g", "ln2_b")
SMALL_SHARDED = ("ssd_conv_w", "ffn_conv_w")
WEIGHTS = ("w_in", "pool_w", "pool_scale", "ssd_conv_w", "ssd_conv_b", "ssd_dt_bias", "ssd_a_log", "ssd_d", "ssd_norm_w",
           "w_out", "ln1_g", "ln1_b", "ffn_w_up", "ffn_conv_w", "ffn_conv_b", "ffn_w_down", "ln2_g", "ln2_b",
           "ple_w_gate", "ple_w_proj")
SUM_BLOCK_BYTES = 3 * 512 * 1024


def _to_rows(a, cols):
    f = a.reshape(-1)
    pad = (-f.shape[0]) % cols
    if pad:
        f = jnp.concatenate([f, jnp.zeros((pad,), f.dtype)])
    return f.reshape(-1, cols)


def _pack_rows(arrs, cols, row_mult):
    rows = [_to_rows(a, cols) for a in arrs]
    flat = jnp.concatenate(rows, axis=0)
    pad = (-flat.shape[0]) % row_mult
    if pad:
        flat = jnp.concatenate([flat, jnp.zeros((pad, cols), flat.dtype)], axis=0)
    return flat


def _unpack_rows(flat, shapes, cols):
    out, r = [], 0
    for shp in shapes:
        n = 1
        for v in shp:
            n *= v
        nr = -(-n // cols)
        out.append(flat[r:r + nr].reshape(-1)[:n].reshape(shp))
        r += nr
    return out


def kernel(x, p, w_in, pool_w, pool_scale, ssd_conv_w, ssd_conv_b, ssd_dt_bias, ssd_a_log, ssd_d, ssd_norm_w, w_out, ln1_g, ln1_b, ffn_w_up, ffn_conv_w, ffn_conv_b, ffn_w_down, ln2_g, ln2_b, ple_w_gate, ple_w_proj, loss_target, m_w_in, m_pool_w, m_pool_scale, m_ssd_conv_w, m_ssd_conv_b, m_ssd_dt_bias, m_ssd_a_log, m_ssd_d, m_ssd_norm_w, m_w_out, m_ln1_g, m_ln1_b, m_ffn_w_up, m_ffn_conv_w, m_ffn_conv_b, m_ffn_w_down, m_ln2_g, m_ln2_b, m_ple_w_gate, m_ple_w_proj, v_w_in, v_pool_w, v_pool_scale, v_ssd_conv_w, v_ssd_conv_b, v_ssd_dt_bias, v_ssd_a_log, v_ssd_d, v_ssd_norm_w, v_w_out, v_ln1_g, v_ln1_b, v_ffn_w_up, v_ffn_conv_w, v_ffn_conv_b, v_ffn_w_down, v_ln2_g, v_ln2_b, v_ple_w_gate, v_ple_w_proj):
    wts = dict(w_in=w_in, pool_w=pool_w, pool_scale=pool_scale, ssd_conv_w=ssd_conv_w, ssd_conv_b=ssd_conv_b,
               ssd_dt_bias=ssd_dt_bias, ssd_a_log=ssd_a_log, ssd_d=ssd_d, ssd_norm_w=ssd_norm_w, w_out=w_out, ln1_g=ln1_g,
               ln1_b=ln1_b, ffn_w_up=ffn_w_up, ffn_conv_w=ffn_conv_w, ffn_conv_b=ffn_conv_b, ffn_w_down=ffn_w_down,
               ln2_g=ln2_g, ln2_b=ln2_b, ple_w_gate=ple_w_gate, ple_w_proj=ple_w_proj)
    mom_m = dict(w_in=m_w_in, pool_w=m_pool_w, pool_scale=m_pool_scale, ssd_conv_w=m_ssd_conv_w, ssd_conv_b=m_ssd_conv_b,
                 ssd_dt_bias=m_ssd_dt_bias, ssd_a_log=m_ssd_a_log, ssd_d=m_ssd_d, ssd_norm_w=m_ssd_norm_w, w_out=m_w_out,
                 ln1_g=m_ln1_g, ln1_b=m_ln1_b, ffn_w_up=m_ffn_w_up, ffn_conv_w=m_ffn_conv_w, ffn_conv_b=m_ffn_conv_b,
                 ffn_w_down=m_ffn_w_down, ln2_g=m_ln2_g, ln2_b=m_ln2_b, ple_w_gate=m_ple_w_gate, ple_w_proj=m_ple_w_proj)
    mom_v = dict(w_in=v_w_in, pool_w=v_pool_w, pool_scale=v_pool_scale, ssd_conv_w=v_ssd_conv_w, ssd_conv_b=v_ssd_conv_b,
                 ssd_dt_bias=v_ssd_dt_bias, ssd_a_log=v_ssd_a_log, ssd_d=v_ssd_d, ssd_norm_w=v_ssd_norm_w, w_out=v_w_out,
                 ln1_g=v_ln1_g, ln1_b=v_ln1_b, ffn_w_up=v_ffn_w_up, ffn_conv_w=v_ffn_conv_w, ffn_conv_b=v_ffn_conv_b,
                 ffn_w_down=v_ffn_w_down, ln2_g=v_ln2_g, ln2_b=v_ln2_b, ple_w_gate=v_ple_w_gate, ple_w_proj=v_ple_w_proj)
    me = 4 * lax.axis_index("x") + 2 * lax.axis_index("y") + lax.axis_index("c")

    def layer_shards(i):
        sh = {n: wts[n][i].astype(MXU_DTYPE) for n in BIG}
        sh["w_in"] = _h_from_orig(wts["w_in"][i]).astype(MXU_DTYPE)
        return sh

    def gathered_weights(res):
        big = dict(zip(BIG, res[:len(BIG)]))
        big["ffn_w_up"] = _interleave(jnp.swapaxes(big["ffn_w_up"], 0, 1).reshape(D_MODEL, 2 * D_FF))
        return big

    res0 = _comm_call(_gather_job(layer_shards(0), [wts[n] for n in SMALL_SHARDED]), "gather_layer0")
    big_w = [gathered_weights(res0)] + [None] * (DEPTH - 1)
    rep = {n: wts[n] for n in SMALL_REPLICATED}
    for n, g in zip(SMALL_SHARDED, res0[len(BIG):]):
        rep[n] = jnp.transpose(g, (1, 2, 0, 3)).reshape(g.shape[1], g.shape[2], N_DEV * g.shape[3])

    def fwd_job(i):
        return _gather_job(layer_shards(i + 1)) if i + 1 < DEPTH else None

    def fwd_done(i, res):
        big_w[i + 1] = gathered_weights(res)

    def grads_for_exchange(g):
        g = dict(g)
        g["ffn_w_up"] = jnp.swapaxes(_deinterleave(g["ffn_w_up"]).reshape(D_MODEL, N_DEV, 704), 0, 1)
        return g

    received = [None] * DEPTH

    def bwd_job(i, bigs):
        return _exchange_job(grads_for_exchange(bigs[i + 1])) if i + 1 < DEPTH else None

    def bwd_done(i, res):
        received[i + 1] = res

    loss_loc, grad_x, bigs, smalls = _run_layers(x, p, loss_target, big_w, rep, fwd_job, fwd_done, bwd_job, bwd_done)
    received[0] = _comm_call(_exchange_job(grads_for_exchange(bigs[0])), "exchange_layer0")

    grads = {}
    for k, n in enumerate(BIG):
        parts = [received[i][k] for i in range(DEPTH)]
        _, rows, cols = parts[0].shape
        tr = next(t for t in (256, 128, 64, 32, 16, 8) if rows % t == 0 and N_DEV * t * cols * 4 <= SUM_BLOCK_BYTES)
        g = _sum8_layers(parts, "sum_" + n, tr)
        grads[n] = _h_to_orig(g) if n == "w_in" else g
    small_names = SMALL_REPLICATED + SMALL_SHARDED
    small_full_shapes = [rep[n].shape for n in small_names]
    small_vec = _pack_rows([jnp.stack([smalls[i][n] for i in range(DEPTH)]) for n in small_names] + [loss_loc[0, :1]],
                           LANES, SUBLANES)
    small_sum = _all_reduce_small(small_vec, "allreduce_small")
    small_out = _unpack_rows(small_sum, small_full_shapes + [(1,)], LANES)
    loss = small_out[-1][0]
    for n, g in zip(small_names, small_out[:-1]):
        if n in SMALL_SHARDED:
            width = wts[n].shape[-1]
            g = lax.dynamic_slice_in_dim(g, me * width, width, axis=g.ndim - 1)
        grads[n] = g

    delta, new_m, new_v = {}, {}, {}
    for n in BIG:
        shp = wts[n].shape
        two_d = lambda a: a.reshape(-1, shp[-1])
        tr = {"w_in": 128, "ffn_w_down": 352}.get(n, 256)
        d_, m_, v_ = _adamw(two_d(wts[n]), two_d(grads[n]), two_d(mom_m[n]), two_d(mom_v[n]), "adamw_" + n, tr=tr)
        delta[n], new_m[n], new_v[n] = d_.reshape(shp), m_.reshape(shp), v_.reshape(shp)
    packs = [_pack_rows([src[n] for n in small_names], LANES, SUBLANES) for src in (wts, grads, mom_m, mom_v)]
    outs = _adamw(*packs, "adamw_small", tr=packs[0].shape[0])
    shapes = [wts[n].shape for n in small_names]
    for dst, flat in zip((delta, new_m, new_v), outs):
        for n, a in zip(small_names, _unpack_rows(flat, shapes, LANES)):
            dst[n] = a
    return (loss, grad_x, *[grads[n] for n in WEIGHTS], *[delta[n] for n in WEIGHTS],
            *[new_m[n] for n in WEIGHTS], *[new_v[n] for n in WEIGHTS])
```

```python
import functools

import jax
import jax.numpy as jnp
from jax import lax
from jax.experimental import pallas as pl
from jax.experimental.pallas import tpu as pltpu

F32 = jnp.float32
BF16 = jnp.bfloat16
MXU_DTYPE = jnp.bfloat16

D_MODEL = 1024
DEPTH = 4
PLE_DIM = 256
ALPHA = (2 * DEPTH) ** 0.25
LN_EPS = 1e-5
RMS_EPS = 1e-6
HEAD_DIM = 64
POOL_WIDTH = 256
POOL_WINDOWS = (2, 4, 8, 16)
SSD_WIDTH = 384
SSD_HEADS = 6
SSD_STATE = 128
SSD_XBC = 896
SB_WIDTH = 384
IN_COLS = 2694
D_FF = 2816
N_DEV = 8

ADAM_LR = 0.001
ADAM_B1 = 0.9
ADAM_B2 = 0.999
ADAM_EPS = 1e-08
ADAM_WD = 0.01
ADAM_STEP = 10

LANES = 128
SUBLANES = 8
VMEM_LIMIT_BYTES = 56 * 1024 * 1024

H_COLS = 2816
H_BC = 0
H_POOL = 512
H_Q = 768
H_K = 1152
H_V = 1536
H_Z = 1920
H_XS = 2304
H_DT = 2688
SSD_CHUNK = 128
QB = 256
GLU_TILE = 256
MASKED_LOG = -1e30

NN = ((1,), (0,))
NT = ((1,), (1,))
TN = ((0,), (0,))


def _dot(a, b, dims=NN):
    return lax.dot_general(a.astype(MXU_DTYPE), b.astype(MXU_DTYPE), (dims, ((), ())), preferred_element_type=F32)


def _dot_exact01(x, m01, dims=NN, x_left=True, terms=3):
    acc = None
    r = x
    for _ in range(terms):
        hi = r.astype(BF16)
        ops = (hi, m01) if x_left else (m01, hi)
        part = lax.dot_general(ops[0], ops[1], (dims, ((), ())), preferred_element_type=F32)
        acc = part if acc is None else acc + part
        r = r - hi.astype(F32)
    return acc


def _sigmoid(v):
    return 1.0 / (1.0 + jnp.exp(-v))


def _silu(v):
    return v * _sigmoid(v)


def _dsilu(v):
    s = _sigmoid(v)
    return s * (1.0 + v * (1.0 - s))


def _softplus(v):
    return jnp.maximum(v, 0.0) + jnp.log(1.0 + jnp.exp(-jnp.abs(v)))


def _params(n_axes, side_effects=False):
    return pltpu.CompilerParams(dimension_semantics=("arbitrary",) * n_axes, vmem_limit_bytes=VMEM_LIMIT_BYTES,
                                has_side_effects=side_effects)


MESH_ID = pl.DeviceIdType.MESH
_ANY = pl.BlockSpec(memory_space=pl.ANY)


def _flip(v, bit):
    return 1 - v if bit else v


def _comm_counts(comm):
    return (0, 0) if comm is None else (len(comm["inputs"]), len(comm["out_shapes"]))


def _comm_call_args(comm):
    if comm is None:
        return [], [], [], []
    n = comm["n_xfers"]
    sems = [pltpu.SemaphoreType.DMA(((N_DEV - 1) * n,)), pltpu.SemaphoreType.DMA(((N_DEV - 1) * n,)),
            pltpu.SemaphoreType.DMA((n,))]
    return list(comm["inputs"]), [_ANY] * len(comm["out_shapes"]), list(comm["out_shapes"]), sems


def _comm_descs(comm, in_refs, tail_refs, with_recvs=True):
    n_out = len(comm["out_shapes"])
    out_refs, (send_sems, recv_sems, local_sems) = tail_refs[:n_out], tail_refs[n_out:n_out + 3]
    xfers = comm["xfers"](in_refs, out_refs)
    n = len(xfers)
    assert n == comm["n_xfers"]
    x, y, c = lax.axis_index("x"), lax.axis_index("y"), lax.axis_index("c")
    me = 4 * x + 2 * y + c
    local = [pltpu.make_async_copy(src_for(me), dst_for(me), local_sems.at[t]) for t, (src_for, dst_for) in enumerate(xfers)]
    sends, recvs = [], []
    for k in range(1, N_DEV):
        pid = (_flip(x, k & 4), _flip(y, k & 2), _flip(c, k & 1))
        peer = 4 * pid[0] + 2 * pid[1] + pid[2]
        for t, (src_for, dst_for) in enumerate(xfers):
            idx = (k - 1) * n + t
            sends.append(pltpu.make_async_remote_copy(
                src_ref=src_for(peer), dst_ref=dst_for(me), send_sem=send_sems.at[idx], recv_sem=recv_sems.at[idx],
                device_id=pid, device_id_type=MESH_ID))
            if with_recvs:
                recvs.append(pltpu.make_async_remote_copy(
                    src_ref=src_for(peer), dst_ref=dst_for(peer), send_sem=send_sems.at[idx], recv_sem=recv_sems.at[idx],
                    device_id=pid, device_id_type=MESH_ID))
    return local, sends, recvs


def _comm_start(descs):
    local, sends, _ = descs
    for cp in local + sends:
        cp.start()


def _comm_wait(descs):
    local, sends, recvs = descs
    for cp in recvs:
        cp.wait_recv()
    for cp in sends:
        cp.wait_send()
    for cp in local:
        cp.wait()


def _comm_hosted(comm, in_refs, tail_refs, grid):
    if comm is None:
        return
    ids = [pl.program_id(a) for a in range(len(grid))]
    first = functools.reduce(jnp.logical_and, [i == 0 for i in ids])
    last = functools.reduce(jnp.logical_and, [i == g - 1 for i, g in zip(ids, grid)])

    @pl.when(first)
    def _():
        _comm_start(_comm_descs(comm, in_refs, tail_refs, with_recvs=False))

    @pl.when(last)
    def _():
        _comm_wait(_comm_descs(comm, in_refs, tail_refs))


def _comm_call(comm, name):
    n_in = len(comm["inputs"])

    def body(*refs):
        descs = _comm_descs(comm, refs[:n_in], refs[n_in:])
        _comm_start(descs)
        _comm_wait(descs)

    c_in, c_specs, c_shapes, c_scratch = _comm_call_args(comm)
    return pl.pallas_call(body, name=name, in_specs=[_ANY] * n_in, out_specs=c_specs, out_shape=c_shapes,
                          scratch_shapes=c_scratch, compiler_params=pltpu.CompilerParams(has_side_effects=True))(*c_in)


def _rows(ref, j, n):
    return ref.at[pl.ds(pl.multiple_of(j * n, SUBLANES), n), :]


def _gather_job(sh, conv=None):
    conv = list(conv or [])
    sds = jax.ShapeDtypeStruct
    out_shapes = [sds((D_MODEL, H_COLS), MXU_DTYPE), sds((D_MODEL, D_MODEL), MXU_DTYPE), sds((N_DEV, D_MODEL, 704), MXU_DTYPE),
                  sds((D_FF, D_MODEL), MXU_DTYPE), sds((D_MODEL, D_MODEL), MXU_DTYPE), sds((PLE_DIM, D_MODEL), MXU_DTYPE)]
    out_shapes += [sds((N_DEV,) + a.shape, a.dtype) for a in conv]

    def xfers(ins, outs):
        whole = lambda a: (lambda j: a)
        r = [(whole(ins[0]), lambda j: _rows(outs[0], j, 128)),
             (whole(ins[1]), lambda j: _rows(outs[1], lax.rem(j + 6, N_DEV), 128)),
             (whole(ins[2]), lambda j: outs[2].at[j]),
             (whole(ins[3]), lambda j: _rows(outs[3], j, 352)),
             (whole(ins[4]), lambda j: _rows(outs[4], j, 128)),
             (whole(ins[5]), lambda j: outs[5].at[:, pl.ds(pl.multiple_of(j * LANES, LANES), LANES)])]
        for t in range(len(conv)):
            r.append((whole(ins[6 + t]), lambda j, o=outs[6 + t]: o.at[j]))
        return r

    return dict(inputs=[sh[n] for n in BIG] + conv, out_shapes=out_shapes, xfers=xfers, n_xfers=6 + len(conv))


_SHARD_SHAPES = {"w_in": (128, H_COLS), "w_out": (128, D_MODEL), "ffn_w_up": (D_MODEL, 704), "ffn_w_down": (352, D_MODEL),
                 "ple_w_gate": (128, D_MODEL), "ple_w_proj": (PLE_DIM, LANES)}


def _exchange_job(items):
    def source(name, ref):
        if name in ("w_in", "ple_w_gate"):
            return lambda j: _rows(ref, j, 128)
        if name == "w_out":
            return lambda j: _rows(ref, lax.rem(j + 6, N_DEV), 128)
        if name == "ffn_w_up":
            return lambda j: ref.at[j]
        if name == "ffn_w_down":
            return lambda j: _rows(ref, j, 352)
        assert name == "ple_w_proj"
        return lambda j: ref.at[:, pl.ds(pl.multiple_of(j * LANES, LANES), LANES)]

    def xfers(ins, outs):
        return [(source(name, i), lambda j, o=o: o.at[j]) for (name, _), i, o in zip(items, ins, outs)]

    return dict(inputs=[g for _, g in items], xfers=xfers, n_xfers=len(items),
                out_shapes=[jax.ShapeDtypeStruct((N_DEV,) + _SHARD_SHAPES[name], F32) for name, _ in items])


def _pick(n, pref):
    if n <= pref:
        return n
    for t in range(pref - pref % LANES, 0, -LANES):
        if n % t == 0:
            return t
    raise ValueError((n, pref))


def _mm(a, b, mode, out_dtype, name, tm=512, tn=512, tk=1024, add=None, add_coef=1.0):
    if mode == "nn":
        (m, k), (k2, n) = a.shape, b.shape
    elif mode == "nt":
        (m, k), (n, k2) = a.shape, b.shape
    else:
        (k, m), (k2, n) = a.shape, b.shape
    assert k == k2, (a.shape, b.shape, mode)
    tm, tn, tk = _pick(m, tm), _pick(n, tn), _pick(k, tk)
    nk = k // tk
    dims = {"nn": NN, "nt": NT, "tn": TN}[mode]

    def body(*refs):
        a_ref, b_ref = refs[:2]
        add_ref = refs[2] if add is not None else None
        o_ref = refs[3] if add is not None else refs[2]

        def finish(r):
            if add_ref is not None:
                r = r + add_coef * add_ref[...]
            o_ref[...] = r.astype(out_dtype)

        if nk == 1:
            finish(_dot(a_ref[...], b_ref[...], dims))
            return
        acc_ref = refs[-1]
        kk = pl.program_id(2)

        @pl.when(kk == 0)
        def _():
            acc_ref[...] = jnp.zeros_like(acc_ref)

        acc_ref[...] += _dot(a_ref[...], b_ref[...], dims)

        @pl.when(kk == nk - 1)
        def _():
            finish(acc_ref[...])

    if mode == "tn":
        a_spec = pl.BlockSpec((tk, tm), lambda i, j, kk: (kk, i))
    else:
        a_spec = pl.BlockSpec((tm, tk), lambda i, j, kk: (i, kk))
    if mode == "nt":
        b_spec = pl.BlockSpec((tn, tk), lambda i, j, kk: (j, kk))
    else:
        b_spec = pl.BlockSpec((tk, tn), lambda i, j, kk: (kk, j))
    o_spec = pl.BlockSpec((tm, tn), lambda i, j, kk: (i, j))
    in_specs = [a_spec, b_spec] + ([o_spec] if add is not None else [])
    args = (a, b) + ((add,) if add is not None else ())
    return pl.pallas_call(
        body, name=name, grid=(m // tm, n // tn, nk), in_specs=in_specs, out_specs=o_spec,
        out_shape=jax.ShapeDtypeStruct((m, n), out_dtype), scratch_shapes=[pltpu.VMEM((tm, tn), F32)] if nk > 1 else [],
        compiler_params=_params(3),
    )(*args)


def _ln_fwd(x, add, gb, name, gp=None, pp=None, tr=512):
    t, d = x.shape
    tr = _pick(t, tr)
    with_ple = gp is not None

    def body(*refs):
        if with_ple:
            x_ref, a_ref, gp_ref, pp_ref, gb_ref, y_ref, r_ref = refs
        else:
            x_ref, a_ref, gb_ref, y_ref, r_ref = refs
        r = ALPHA * x_ref[...] + a_ref[...]
        if with_ple:
            r = r + _sigmoid(gp_ref[...]) * pp_ref[...]
        mu = jnp.mean(r, axis=1, keepdims=True)
        xc = r - mu
        var = jnp.mean(xc * xc, axis=1, keepdims=True)
        y_ref[...] = xc * lax.rsqrt(var + LN_EPS) * gb_ref[0:1, :] + gb_ref[1:2, :]
        r_ref[...] = r

    row = pl.BlockSpec((tr, d), lambda i: (i, 0))
    vec = pl.BlockSpec((2, d), lambda i: (0, 0))
    n_row = 4 if with_ple else 2
    args = (x, add) + ((gp, pp) if with_ple else ()) + (gb,)
    return pl.pallas_call(
        body, name=name, grid=(t // tr,), in_specs=[row] * n_row + [vec], out_specs=[row, row],
        out_shape=[jax.ShapeDtypeStruct((t, d), F32)] * 2, compiler_params=_params(1),
    )(*args)


def _ln_bwd(r, gb, dy, name, gp=None, pp=None, tr=512):
    t, d = r.shape
    tr = _pick(t, tr)
    with_ple = gp is not None

    def body(*refs):
        if with_ple:
            r_ref, dy_ref, gp_ref, pp_ref, gb_ref, dr_ref, dgp_ref, dpp_ref, st_ref = refs
        else:
            r_ref, dy_ref, gb_ref, dr_ref, st_ref = refs
        i = pl.program_id(0)

        @pl.when(i == 0)
        def _():
            st_ref[...] = jnp.zeros_like(st_ref)

        rv = r_ref[...]
        dy_v = dy_ref[...]
        mu = jnp.mean(rv, axis=1, keepdims=True)
        xc = rv - mu
        var = jnp.mean(xc * xc, axis=1, keepdims=True)
        rstd = lax.rsqrt(var + LN_EPS)
        xhat = xc * rstd
        dxh = dy_v * gb_ref[0:1, :]
        m1 = jnp.mean(dxh, axis=1, keepdims=True)
        m2 = jnp.mean(dxh * xhat, axis=1, keepdims=True)
        dr = rstd * (dxh - m1 - xhat * m2)
        dr_ref[...] = dr
        rid = lax.broadcasted_iota(jnp.int32, (2, d), 0)
        dg = jnp.sum(dy_v * xhat, axis=0, keepdims=True)
        db = jnp.sum(dy_v, axis=0, keepdims=True)
        st_ref[...] += jnp.where(rid == 0, dg, db)
        if with_ple:
            sg = _sigmoid(gp_ref[...])
            ppv = pp_ref[...]
            dgp_ref[...] = (dr * ppv * sg * (1.0 - sg)).astype(dgp_ref.dtype)
            dpp_ref[...] = (dr * sg).astype(dpp_ref.dtype)

    row = pl.BlockSpec((tr, d), lambda i: (i, 0))
    vec = pl.BlockSpec((2, d), lambda i: (0, 0))
    if with_ple:
        in_specs, args = [row] * 4 + [vec], (r, dy, gp, pp, gb)
        out_specs = [row, row, row, vec]
        out_shape = [jax.ShapeDtypeStruct((t, d), F32), jax.ShapeDtypeStruct((t, d), MXU_DTYPE),
                     jax.ShapeDtypeStruct((t, d), MXU_DTYPE), jax.ShapeDtypeStruct((2, d), F32)]
    else:
        in_specs, args = [row] * 2 + [vec], (r, dy, gb)
        out_specs = [row, vec]
        out_shape = [jax.ShapeDtypeStruct((t, d), F32), jax.ShapeDtypeStruct((2, d), F32)]
    return pl.pallas_call(body, name=name, grid=(t // tr,), in_specs=in_specs, out_specs=out_specs,
                          out_shape=out_shape, compiler_params=_params(1))(*args)


def _loss_grad(y, target, name, tr=512):
    t, d = y.shape
    tr = _pick(t, tr)

    def body(y_ref, t_ref, dy_ref, l_ref):
        i = pl.program_id(0)

        @pl.when(i == 0)
        def _():
            l_ref[...] = jnp.zeros_like(l_ref)

        e = y_ref[...] - t_ref[...]
        dy_ref[...] = e * (1.0 / d)
        per_tok = jnp.mean(e * e, axis=1, keepdims=True)
        l_ref[...] += 0.5 * jnp.sum(per_tok, axis=0, keepdims=True)

    row = pl.BlockSpec((tr, d), lambda i: (i, 0))
    acc = pl.BlockSpec((SUBLANES, LANES), lambda i: (0, 0))
    return pl.pallas_call(body, name=name, grid=(t // tr,), in_specs=[row, row], out_specs=[row, acc],
                          out_shape=[jax.ShapeDtypeStruct((t, d), F32), jax.ShapeDtypeStruct((SUBLANES, LANES), F32)],
                          compiler_params=_params(1))(y, target)


def _shift_down(v, k, row):
    return jnp.where(row >= k, pltpu.roll(v, k, 0), 0.0)


def _shift_up(v, k, row):
    n = v.shape[0]
    return jnp.where(row < n - k, pltpu.roll(v, n - k, 0), 0.0)


def _pool_window(lane):
    grp = lane // HEAD_DIM
    return jnp.where(grp == 0, POOL_WINDOWS[0], jnp.where(grp == 1, POOL_WINDOWS[1],
                     jnp.where(grp == 2, POOL_WINDOWS[2], POOL_WINDOWS[3])))


def _pool_select(lane, s2, s4, s8, s16):
    grp = lane // HEAD_DIM
    return jnp.where(grp == 0, s2, jnp.where(grp == 1, s4, jnp.where(grp == 2, s8, s16)))


def _pooled(u, row, lane):
    s2 = u + _shift_down(u, 1, row)
    s4 = s2 + _shift_down(s2, 2, row)
    s8 = s4 + _shift_down(s4, 4, row)
    s16 = s8 + _shift_down(s8, 8, row)
    cnt = jnp.minimum(row + 1, _pool_window(lane)).astype(F32)
    return _pool_select(lane, s2, s4, s8, s16) / cnt - u, cnt


def _pool_fwd(h, wbd, scale, nb, s, name):
    def body(u_ref, w_ref, sc_ref, o_ref):
        u = u_ref[...]
        row = lax.broadcasted_iota(jnp.int32, u.shape, 0)
        lane = lax.broadcasted_iota(jnp.int32, u.shape, 1)
        pooled, _ = _pooled(u, row, lane)
        o_ref[...] = (_dot(pooled, w_ref[...]) * sc_ref[...]).astype(o_ref.dtype)

    wb = POOL_WIDTH
    return pl.pallas_call(
        body, name=name, grid=(nb,),
        in_specs=[pl.BlockSpec((s, wb), lambda b: (b, H_POOL // wb)), pl.BlockSpec((wb, wb), lambda b: (0, 0)),
                  pl.BlockSpec((1, wb), lambda b: (0, 0))],
        out_specs=pl.BlockSpec((s, wb), lambda b: (b, 0)),
        out_shape=jax.ShapeDtypeStruct((nb * s, wb), MXU_DTYPE), compiler_params=_params(1),
    )(h, wbd, scale)


def _pool_bwd(h, dmix, wbd, scale, nb, s, name):
    wb = POOL_WIDTH

    def body(u_ref, do_ref, w_ref, sc_ref, du_ref, dw_ref, ds_ref):
        b = pl.program_id(0)

        @pl.when(b == 0)
        def _():
            dw_ref[...] = jnp.zeros_like(dw_ref)
            ds_ref[...] = jnp.zeros_like(ds_ref)

        u = u_ref[...]
        row = lax.broadcasted_iota(jnp.int32, u.shape, 0)
        lane = lax.broadcasted_iota(jnp.int32, u.shape, 1)
        pooled, cnt = _pooled(u, row, lane)
        mixed = _dot(pooled, w_ref[...])
        do = do_ref[...]
        ds_ref[...] += jnp.sum(do * mixed, axis=0, keepdims=True)
        dm = do * sc_ref[...]
        dw_ref[...] += _dot(pooled, dm, TN)
        dpool = _dot(dm, w_ref[...], NT)
        qv = dpool / cnt
        f2 = qv + _shift_up(qv, 1, row)
        f4 = f2 + _shift_up(f2, 2, row)
        f8 = f4 + _shift_up(f4, 4, row)
        f16 = f8 + _shift_up(f8, 8, row)
        du_ref[...] = (_pool_select(lane, f2, f4, f8, f16) - dpool).astype(du_ref.dtype)

    return pl.pallas_call(
        body, name=name, grid=(nb,),
        in_specs=[pl.BlockSpec((s, wb), lambda b: (b, H_POOL // wb)), pl.BlockSpec((s, wb), lambda b: (b, 3)),
                  pl.BlockSpec((wb, wb), lambda b: (0, 0)), pl.BlockSpec((1, wb), lambda b: (0, 0))],
        out_specs=[pl.BlockSpec((s, wb), lambda b: (b, 0)), pl.BlockSpec((wb, wb), lambda b: (0, 0)),
                   pl.BlockSpec((1, wb), lambda b: (0, 0))],
        out_shape=[jax.ShapeDtypeStruct((nb * s, wb), MXU_DTYPE), jax.ShapeDtypeStruct((wb, wb), F32),
                   jax.ShapeDtypeStruct((1, wb), F32)],
        compiler_params=_params(1),
    )(h, dmix, wbd, scale)


def _glu_conv(x, w_ref, b_ref, row):
    return (b_ref[...] + w_ref[2:3, :] * x + w_ref[1:2, :] * _shift_down(x, 1, row)
            + w_ref[0:1, :] * _shift_down(x, 2, row))


def _glu_fwd(up, cw, cb, nb, s, name):
    wt = 2 * GLU_TILE
    nt = up.shape[1] // wt

    def body(u_ref, w_ref, b_ref, o_ref):
        x = u_ref[...]
        row = lax.broadcasted_iota(jnp.int32, x.shape, 0)
        c = _glu_conv(x, w_ref, b_ref, row)
        o_ref[...] = (_silu(c[:, :GLU_TILE]) * c[:, GLU_TILE:]).astype(o_ref.dtype)

    return pl.pallas_call(
        body, name=name, grid=(nt, nb),
        in_specs=[pl.BlockSpec((s, wt), lambda j, b: (b, j)), pl.BlockSpec((3, wt), lambda j, b: (0, j)),
                  pl.BlockSpec((1, wt), lambda j, b: (0, j))],
        out_specs=pl.BlockSpec((s, GLU_TILE), lambda j, b: (b, j)),
        out_shape=jax.ShapeDtypeStruct((nb * s, nt * GLU_TILE), MXU_DTYPE), compiler_params=_params(2),
    )(up, cw, cb)


def _glu_bwd(up, dact, cw, cb, nb, s, name):
    wt = 2 * GLU_TILE
    nt = up.shape[1] // wt

    def body(u_ref, da_ref, w_ref, b_ref, du_ref, acc_ref):
        b = pl.program_id(1)

        @pl.when(b == 0)
        def _():
            acc_ref[...] = jnp.zeros_like(acc_ref)

        x = u_ref[...]
        row = lax.broadcasted_iota(jnp.int32, x.shape, 0)
        x1 = _shift_down(x, 1, row)
        x2 = _shift_down(x, 2, row)
        c = b_ref[...] + w_ref[2:3, :] * x + w_ref[1:2, :] * x1 + w_ref[0:1, :] * x2
        gate, val = c[:, :GLU_TILE], c[:, GLU_TILE:]
        da = da_ref[...]
        dc = jnp.concatenate([da * val * _dsilu(gate), da * _silu(gate)], axis=1)
        dx = (w_ref[2:3, :] * dc + w_ref[1:2, :] * _shift_up(dc, 1, row) + w_ref[0:1, :] * _shift_up(dc, 2, row))
        du_ref[...] = dx.astype(du_ref.dtype)
        rid = lax.broadcasted_iota(jnp.int32, (SUBLANES, wt), 0)
        dw0 = jnp.sum(dc * x2, axis=0, keepdims=True)
        dw1 = jnp.sum(dc * x1, axis=0, keepdims=True)
        dw2 = jnp.sum(dc * x, axis=0, keepdims=True)
        db = jnp.sum(dc, axis=0, keepdims=True)
        acc_ref[...] += (jnp.where(rid == 0, dw0, 0.0) + jnp.where(rid == 1, dw1, 0.0)
                         + jnp.where(rid == 2, dw2, 0.0) + jnp.where(rid == 3, db, 0.0))

    return pl.pallas_call(
        body, name=name, grid=(nt, nb),
        in_specs=[pl.BlockSpec((s, wt), lambda j, b: (b, j)), pl.BlockSpec((s, GLU_TILE), lambda j, b: (b, j)),
                  pl.BlockSpec((3, wt), lambda j, b: (0, j)), pl.BlockSpec((1, wt), lambda j, b: (0, j))],
        out_specs=[pl.BlockSpec((s, wt), lambda j, b: (b, j)), pl.BlockSpec((SUBLANES, wt), lambda j, b: (0, j))],
        out_shape=[jax.ShapeDtypeStruct((nb * s, nt * wt), MXU_DTYPE), jax.ShapeDtypeStruct((SUBLANES, nt * wt), F32)],
        compiler_params=_params(2),
    )(up, dact, cw, cb)


def _sb_masks():
    row = lax.broadcasted_iota(jnp.int32, (QB, QB), 0)
    col = lax.broadcasted_iota(jnp.int32, (QB, QB), 1)
    return row, col


def _sb_fwd(h, nb, s, name, comm=None):
    nq = s // QB
    scale = HEAD_DIM ** -0.5
    n_in, n_out = _comm_counts(comm)

    def body(q_ref, k_ref, v_ref, *rest):
        o_ref = rest[n_in]
        i = pl.program_id(2)
        _comm_hosted(comm, rest[:n_in], rest[n_in + 1:], (nb, 3, nq))
        row, col = _sb_masks()
        low = col < row
        later = (row > col).astype(BF16)
        sls = [slice(hd * HEAD_DIM, (hd + 1) * HEAD_DIM) for hd in range(2)]
        qs = [(q_ref[:, sl] * scale).astype(MXU_DTYPE) for sl in sls]

        def scores(hd, j, diagonal):
            r0 = pl.multiple_of(j * QB, QB)
            z = _dot(qs[hd], k_ref[pl.ds(r0, QB), sls[hd]], NT)
            ln = -_softplus(z)
            ls = z + ln
            if diagonal:
                ln = jnp.where(low, ln, 0.0)
                ls = jnp.where(low, ls, MASKED_LOG)
            return ls, _dot_exact01(ln, later, terms=2), jnp.sum(ln, axis=1, keepdims=True)

        def output(hd, j, ls, tl, ct):
            r0 = pl.multiple_of(j * QB, QB)
            return _dot(jnp.exp(ls + tl + ct), v_ref[pl.ds(r0, QB), sls[hd]])

        zero = jnp.zeros((QB, 1), F32)
        zacc = jnp.zeros((QB, HEAD_DIM), F32)

        def step(t, carry):
            a0, a1, c0, c1, (ls0, tl0, s0), (ls1, tl1, s1) = carry
            j = i - t
            a0 = a0 + output(0, j + 1, ls0, tl0, c0)
            a1 = a1 + output(1, j + 1, ls1, tl1, c1)
            return a0, a1, c0 + s0, c1 + s1, scores(0, j, False), scores(1, j, False)

        first = (zacc, zacc, zero, zero, scores(0, i, True), scores(1, i, True))
        a0, a1, c0, c1, (ls0, tl0, _), (ls1, tl1, _) = lax.fori_loop(1, i + 1, step, first)
        o_ref[:, sls[0]] = (a0 + output(0, 0, ls0, tl0, c0)).astype(o_ref.dtype)
        o_ref[:, sls[1]] = (a1 + output(1, 0, ls1, tl1, c1)).astype(o_ref.dtype)

    qspec = lambda off: pl.BlockSpec((QB, LANES), lambda b, p, i: (b * nq + i, off // LANES + p))
    kvspec = lambda off: pl.BlockSpec((s, LANES), lambda b, p, i: (b, off // LANES + p))
    c_in, c_specs, c_shapes, c_scratch = _comm_call_args(comm)
    res = pl.pallas_call(
        body, name=name, grid=(nb, 3, nq), in_specs=[qspec(H_Q), kvspec(H_K), kvspec(H_V)] + [_ANY] * n_in,
        out_specs=[pl.BlockSpec((QB, LANES), lambda b, p, i: (b * nq + i, p))] + c_specs,
        out_shape=[jax.ShapeDtypeStruct((nb * s, SB_WIDTH), MXU_DTYPE)] + c_shapes, scratch_shapes=c_scratch,
        compiler_params=_params(3, comm is not None),
    )(h, h, h, *c_in)
    return res[0], res[1:]


def _sb_bwd(h, dmix, nb, s, name, comm=None):
    nq = s // QB
    scale = HEAD_DIM ** -0.5
    n_in, n_out = _comm_counts(comm)

    def body(q_ref, k_ref, v_ref, do_ref, *rest):
        dq_ref, dk_ref, dv_ref = rest[n_in:n_in + 3]
        p_buf, ls_buf = rest[n_in + 3 + n_out:n_in + 5 + n_out]
        i = pl.program_id(2)
        _comm_hosted(comm, rest[:n_in], rest[n_in + 3:n_in + 3 + n_out] + rest[n_in + 5 + n_out:], (nb, 3, nq))

        @pl.when(i == 0)
        def _():
            dk_ref[...] = jnp.zeros_like(dk_ref)
            dv_ref[...] = jnp.zeros_like(dv_ref)

        row, col = _sb_masks()
        low = col < row
        later = (row > col).astype(BF16)
        earlier = (row < col).astype(BF16)
        sls = [slice(hd * HEAD_DIM, (hd + 1) * HEAD_DIM) for hd in range(2)]
        q_raw = [q_ref[:, sl].astype(MXU_DTYPE) for sl in sls]
        qs = [(q_ref[:, sl] * scale).astype(MXU_DTYPE) for sl in sls]
        do = [do_ref[:, sl].astype(MXU_DTYPE) for sl in sls]

        def down(hd, j, ct, diagonal):
            r0 = pl.multiple_of(j * QB, QB)
            kj = k_ref[pl.ds(r0, QB), sls[hd]]
            vj = v_ref[pl.ds(r0, QB), sls[hd]]
            z = _dot(qs[hd], kj, NT)
            ln = -_softplus(z)
            if diagonal:
                ln = jnp.where(low, ln, 0.0)
            ls = z + ln
            tail = ct + _dot_exact01(ln, later, terms=2)
            a = jnp.exp(ls + tail)
            if diagonal:
                a = jnp.where(low, a, 0.0)
                ls = jnp.where(low, ls, 0.0)
            p_buf[hd, j] = _dot(do[hd], vj, NT) * a
            ls_buf[hd, j] = ls
            dv_ref[pl.ds(r0, QB), sls[hd]] += _dot(a, do[hd], TN)
            return jnp.sum(ln, axis=1, keepdims=True)

        zero = jnp.zeros((QB, 1), F32)
        c0 = down(0, i, zero, True)
        c1 = down(1, i, zero, True)

        def down_step(jj, carry):
            c0, c1 = carry
            j = i - 1 - jj
            return c0 + down(0, j, c0, False), c1 + down(1, j, c1, False)

        lax.fori_loop(0, i, down_step, (c0, c1))

        def up(hd, j, dq, cp, diagonal):
            r0 = pl.multiple_of(j * QB, QB)
            pj = p_buf[hd, j]
            sg = jnp.exp(ls_buf[hd, j])
            cum = cp + _dot_exact01(pj, earlier)
            dz = (pj * (1.0 - sg) - cum * sg) * scale
            if diagonal:
                dz = jnp.where(low, dz, 0.0)
            kj = k_ref[pl.ds(r0, QB), sls[hd]]
            dk_ref[pl.ds(r0, QB), sls[hd]] += _dot(dz, q_raw[hd], TN)
            return dq + _dot(dz, kj), cp + jnp.sum(pj, axis=1, keepdims=True)

        def up_step(j, carry):
            dq0, cp0, dq1, cp1 = carry
            dq0, cp0 = up(0, j, dq0, cp0, False)
            dq1, cp1 = up(1, j, dq1, cp1, False)
            return dq0, cp0, dq1, cp1

        zq = jnp.zeros((QB, HEAD_DIM), F32)
        dq0, cp0, dq1, cp1 = lax.fori_loop(0, i, up_step, (zq, zero, zq, zero))
        dq0, _ = up(0, i, dq0, cp0, True)
        dq1, _ = up(1, i, dq1, cp1, True)
        dq_ref[:, sls[0]] = dq0
        dq_ref[:, sls[1]] = dq1

    qspec = lambda off: pl.BlockSpec((QB, LANES), lambda b, p, i: (b * nq + i, off // LANES + p))
    kvspec = lambda off: pl.BlockSpec((s, LANES), lambda b, p, i: (b, off // LANES + p))
    blk_out = pl.BlockSpec((QB, LANES), lambda b, p, i: (b * nq + i, p))
    seq_out = pl.BlockSpec((s, LANES), lambda b, p, i: (b, p))
    shp = jax.ShapeDtypeStruct((nb * s, SB_WIDTH), F32)
    c_in, c_specs, c_shapes, c_scratch = _comm_call_args(comm)
    res = pl.pallas_call(
        body, name=name, grid=(nb, 3, nq),
        in_specs=[qspec(H_Q), kvspec(H_K), kvspec(H_V), pl.BlockSpec((QB, LANES), lambda b, p, i: (b * nq + i, 3 + p))]
        + [_ANY] * n_in,
        out_specs=[blk_out, seq_out, seq_out] + c_specs, out_shape=[shp, shp, shp] + c_shapes,
        scratch_shapes=[pltpu.VMEM((2, nq, QB, QB), F32), pltpu.VMEM((2, nq, QB, QB), F32)] + c_scratch,
        compiler_params=_params(3, comm is not None),
    )(h, h, h, dmix, *c_in)
    return res[0], res[1], res[2], res[3:]


def _ssd_conv(cur_ref, halo_ref, w_ref, b_ref, ext_ref, first):
    n = SSD_CHUNK
    cur = cur_ref[...]
    ext_ref[0:SUBLANES, :] = jnp.where(first, 0.0, halo_ref[...])
    ext_ref[SUBLANES:SUBLANES + n, :] = cur
    return (b_ref[...] + w_ref[3:4, :] * cur + w_ref[2:3, :] * ext_ref[pl.ds(SUBLANES - 1, n), :]
            + w_ref[1:2, :] * ext_ref[pl.ds(SUBLANES - 2, n), :] + w_ref[0:1, :] * ext_ref[pl.ds(SUBLANES - 3, n), :])


def _ssd_tri():
    row = lax.broadcasted_iota(jnp.int32, (SSD_CHUNK, SSD_CHUNK), 0)
    col = lax.broadcasted_iota(jnp.int32, (SSD_CHUNK, SSD_CHUNK), 1)
    return row, col


def _ssd_specs(nc, rev):
    n = SSD_CHUNK
    hb = n // SUBLANES

    def cidx(c):
        return (nc - 1 - c) if rev else c

    def blk(width, off):
        return pl.BlockSpec((n, width), lambda b, c: (b * nc + cidx(c), off // width))

    def halo(width, off):
        return pl.BlockSpec((SUBLANES, width), lambda b, c: (jnp.maximum((b * nc + cidx(c)) * hb - 1, 0), off // width))

    def full(shape):
        return pl.BlockSpec(shape, lambda b, c: (0,) * len(shape))

    return cidx, blk, halo, full


def _ssd_core_fwd(x, bc, dt, acum, acum_t, a_row, d_row, h_prev_ref, tri):
    n = SSD_CHUNK
    heads = []
    for g in range(2):
        bm = bc[:, g * SSD_STATE:(g + 1) * SSD_STATE]
        cm = bc[:, 2 * SSD_STATE + g * SSD_STATE: 2 * SSD_STATE + (g + 1) * SSD_STATE]
        gmat = _dot(cm, bm, NT)
        for r in range(3):
            hh = g * 3 + r
            ac = acum[:, hh:hh + 1]
            ar = acum_t[hh:hh + 1, :]
            dec = jnp.where(tri, jnp.exp(jnp.minimum(ac - ar, 0.0)), 0.0)
            xh = x[:, hh * HEAD_DIM:(hh + 1) * HEAD_DIM]
            dth = dt[:, hh:hh + 1]
            xdt = xh * dth
            hp = h_prev_ref[hh * HEAD_DIM:(hh + 1) * HEAD_DIM, :]
            ea = jnp.exp(ac)
            m = gmat * dec
            yo = ea * _dot(cm, hp, NT)
            al = acum[n - 1:n, hh:hh + 1]
            w = jnp.exp(al - ac)
            y = _dot(m, xdt) + yo + d_row[:, hh:hh + 1] * xh
            heads.append(dict(g=g, hh=hh, bm=bm, cm=cm, gmat=gmat, dec=dec, xh=xh, dth=dth, xdt=xdt, hp=hp, ea=ea,
                              m=m, yo=yo, al=al, w=w, y=y))
    return heads


def _ssd_prep(xs_ref, xsh_ref, bc_ref, bch_ref, dt_ref, cwx_ref, cbx_ref, cwb_ref, cbb_ref, vec_ref, xe_ref, be_ref, first):
    pre_x = _ssd_conv(xs_ref, xsh_ref, cwx_ref, cbx_ref, xe_ref, first)
    pre_bc = _ssd_conv(bc_ref, bch_ref, cwb_ref, cbb_ref, be_ref, first)
    x = _silu(pre_x)
    bc = _silu(pre_bc)
    dt_pre = dt_ref[...] + vec_ref[0:1, :]
    dt = _softplus(dt_pre)
    a_row = vec_ref[1:2, :]
    amat = dt * a_row
    row, col = _ssd_tri()
    upper = (row <= col).astype(BF16)
    lower = (col <= row).astype(BF16)
    acum = _dot_exact01(amat, lower, NN, x_left=False)
    acum_t = _dot_exact01(amat, upper, TN, x_left=True)
    return pre_x, pre_bc, x, bc, dt_pre, dt, a_row, acum, acum_t, row, col, upper


def _ssd_gate_norm(y, z, nw):
    lane = lax.broadcasted_iota(jnp.int32, y.shape, 1)
    g0 = lane < SSD_WIDTH // 2
    hg = y * _silu(z)
    sq = hg * hg
    ms0 = jnp.sum(jnp.where(g0, sq, 0.0), axis=1, keepdims=True) * (2.0 / SSD_WIDTH)
    ms1 = jnp.sum(jnp.where(g0, 0.0, sq), axis=1, keepdims=True) * (2.0 / SSD_WIDTH)
    rs = jnp.where(g0, lax.rsqrt(ms0 + RMS_EPS), lax.rsqrt(ms1 + RMS_EPS))
    return hg, rs, g0


def _ssd_fwd(h, cwx, cbx, cwb, cbb, vec, nw, nb, s, name):
    n = SSD_CHUNK
    nc = s // n
    _, blk, halo, full = _ssd_specs(nc, False)

    def body(bc_ref, bch_ref, z_ref, xs_ref, xsh_ref, dt_ref, cwx_ref, cbx_ref, cwb_ref, cbb_ref, vec_ref, nw_ref,
             o_ref, hs_ref, h_scr, xe_ref, be_ref, y_scr):
        c = pl.program_id(1)

        @pl.when(c == 0)
        def _():
            h_scr[...] = jnp.zeros_like(h_scr)

        (_, _, x, bc, _, dt, a_row, acum, acum_t, row, col, _) = _ssd_prep(
            xs_ref, xsh_ref, bc_ref, bch_ref, dt_ref, cwx_ref, cbx_ref, cwb_ref, cbb_ref, vec_ref, xe_ref, be_ref, c == 0)
        hs_ref[...] = h_scr[...]
        heads = _ssd_core_fwd(x, bc, dt, acum, acum_t, a_row, vec_ref[2:3, :], hs_ref, col <= row)
        for hd in heads:
            sl = slice(hd["hh"] * HEAD_DIM, (hd["hh"] + 1) * HEAD_DIM)
            y_scr[:, sl] = hd["y"]
            h_scr[sl, :] = jnp.exp(hd["al"]) * hd["hp"] + _dot(hd["xdt"] * hd["w"], hd["bm"], TN)
        hg, rs, _ = _ssd_gate_norm(y_scr[...], z_ref[...], nw_ref[...])
        o_ref[...] = (hg * rs * nw_ref[...]).astype(o_ref.dtype)

    t = nb * s
    return pl.pallas_call(
        body, name=name, grid=(nb, nc),
        in_specs=[blk(512, H_BC), halo(512, H_BC), blk(384, H_Z), blk(384, H_XS), halo(384, H_XS), blk(128, H_DT),
                  full((4, 384)), full((1, 384)), full((4, 512)), full((1, 512)), full((SUBLANES, LANES)), full((1, 384))],
        out_specs=[pl.BlockSpec((n, SSD_WIDTH), lambda b, c: (b * nc + c, 0)),
                   pl.BlockSpec((None, SSD_WIDTH, SSD_STATE), lambda b, c: (b * nc + c, 0, 0))],
        out_shape=[jax.ShapeDtypeStruct((t, SSD_WIDTH), MXU_DTYPE),
                   jax.ShapeDtypeStruct((nb * nc, SSD_WIDTH, SSD_STATE), F32)],
        scratch_shapes=[pltpu.VMEM((SSD_WIDTH, SSD_STATE), F32), pltpu.VMEM((n + SUBLANES, 384), F32),
                        pltpu.VMEM((n + SUBLANES, 512), F32), pltpu.VMEM((n, SSD_WIDTH), F32)],
        compiler_params=_params(2),
    )(h, h, h, h, h, h, cwx, cbx, cwb, cbb, vec, nw)


def _ssd_bwd(h, hstate, dmix, cwx, cbx, cwb, cbb, vec, nw, nb, s, name):
    n = SSD_CHUNK
    nc = s // n
    cidx, blk, halo, full = _ssd_specs(nc, True)

    def body(bc_ref, bch_ref, z_ref, xs_ref, xsh_ref, dt_ref, hs_ref, do_ref, cwx_ref, cbx_ref, cwb_ref, cbb_ref,
             vec_ref, nw_ref, dz_ref, dxs_ref, dbc_ref, ddt_ref, gx_ref, gb_ref, gv_ref, gn_ref,
             dh_scr, xe_ref, be_ref, y_scr, dx_scr, dbc_scr, dxe_ref, dbe_ref, cx_ref, cb_ref):
        b = pl.program_id(0)
        c = pl.program_id(1)
        cc = nc - 1 - c

        @pl.when(jnp.logical_and(b == 0, c == 0))
        def _():
            gx_ref[...] = jnp.zeros_like(gx_ref)
            gb_ref[...] = jnp.zeros_like(gb_ref)
            gv_ref[...] = jnp.zeros_like(gv_ref)
            gn_ref[...] = jnp.zeros_like(gn_ref)

        @pl.when(c == 0)
        def _():
            dh_scr[...] = jnp.zeros_like(dh_scr)
            cx_ref[...] = jnp.zeros_like(cx_ref)
            cb_ref[...] = jnp.zeros_like(cb_ref)

        (pre_x, pre_bc, x, bc, dt_pre, dt, a_row, acum, acum_t, row, col, upper) = _ssd_prep(
            xs_ref, xsh_ref, bc_ref, bch_ref, dt_ref, cwx_ref, cbx_ref, cwb_ref, cbb_ref, vec_ref, xe_ref, be_ref, cc == 0)
        tri = col <= row
        d_row = vec_ref[2:3, :]
        heads = _ssd_core_fwd(x, bc, dt, acum, acum_t, a_row, d_row, hs_ref, tri)
        for hd in heads:
            y_scr[:, hd["hh"] * HEAD_DIM:(hd["hh"] + 1) * HEAD_DIM] = hd["y"]
        y = y_scr[...]
        z = z_ref[...]
        nwv = nw_ref[...]
        hg, rs, g0 = _ssd_gate_norm(y, z, nwv)
        do = do_ref[...]
        nrm = hg * rs
        gn_ref[...] += jnp.sum(do * nrm, axis=0, keepdims=True)
        dn = do * nwv
        dnn = dn * nrm
        mean0 = jnp.sum(jnp.where(g0, dnn, 0.0), axis=1, keepdims=True) * (2.0 / SSD_WIDTH)
        mean1 = jnp.sum(jnp.where(g0, 0.0, dnn), axis=1, keepdims=True) * (2.0 / SSD_WIDTH)
        dhg = rs * (dn - nrm * jnp.where(g0, mean0, mean1))
        dz_ref[...] = (dhg * y * _dsilu(z)).astype(dz_ref.dtype)
        dy = dhg * _silu(z)

        lane = lax.broadcasted_iota(jnp.int32, (n, LANES), 1)
        lane1 = lax.broadcasted_iota(jnp.int32, (1, LANES), 1)
        last_row = lax.broadcasted_iota(jnp.int32, (n, 1), 0) == n - 1
        dacum_col = jnp.zeros((n, LANES), F32)
        da_rowpart = jnp.zeros((n, LANES), F32)
        ddt = jnp.zeros((n, LANES), F32)
        dd_vec = jnp.zeros((1, LANES), F32)
        for g in range(2):
            dg = jnp.zeros((n, n), F32)
            dbm = jnp.zeros((n, SSD_STATE), F32)
            dcm = jnp.zeros((n, SSD_STATE), F32)
            for hd in heads[3 * g:3 * g + 3]:
                hh = hd["hh"]
                sl = slice(hh * HEAD_DIM, (hh + 1) * HEAD_DIM)
                dyh = dy[:, sl]
                dhn = dh_scr[sl, :]
                el = jnp.exp(hd["al"])
                dd_vec = dd_vec + jnp.where(lane1 == hh, jnp.sum(dyh * hd["xh"]), 0.0)
                dcm = dcm + hd["ea"] * _dot(dyh, hd["hp"])
                dm = _dot(dyh, hd["xdt"], NT)
                dg = dg + dm * hd["dec"]
                e = dm * hd["m"]
                t2 = _dot(hd["bm"], dhn, NT)
                dxdt = _dot(hd["m"], dyh, TN) + hd["w"] * t2
                dbm = dbm + hd["w"] * _dot(hd["xdt"], dhn)
                dw_w = jnp.sum(hd["xdt"] * t2, axis=1, keepdims=True) * hd["w"]
                d_el = jnp.sum(dhn * hd["hp"])
                col_part = (jnp.sum(dyh * hd["yo"], axis=1, keepdims=True) + jnp.sum(e, axis=1, keepdims=True) - dw_w
                            + jnp.where(last_row, d_el * el + jnp.sum(dw_w), 0.0))
                dacum_col = dacum_col + jnp.where(lane == hh, col_part, 0.0)
                neg_colsum = -jnp.sum(e, axis=0, keepdims=True)
                rev = jnp.sum(jnp.where(row <= col, neg_colsum, 0.0), axis=1, keepdims=True)
                da_rowpart = da_rowpart + jnp.where(lane == hh, rev, 0.0)
                dh_scr[sl, :] = el * dhn + _dot(dyh * hd["ea"], hd["cm"], TN)
                dx_scr[:, sl] = d_row[:, hh:hh + 1] * dyh + dxdt * hd["dth"]
                ddt = ddt + jnp.where(lane == hh, jnp.sum(dxdt * hd["xh"], axis=1, keepdims=True), 0.0)
            bm, cm = heads[3 * g]["bm"], heads[3 * g]["cm"]
            dcm = dcm + _dot(dg, bm)
            dbm = dbm + _dot(dg, cm, TN)
            dbc_scr[:, g * SSD_STATE:(g + 1) * SSD_STATE] = dbm
            dbc_scr[:, 2 * SSD_STATE + g * SSD_STATE:2 * SSD_STATE + (g + 1) * SSD_STATE] = dcm
        da_mat = _dot_exact01(dacum_col, upper, NN, x_left=False) + da_rowpart
        ddt = ddt + da_mat * a_row
        da_vec = jnp.sum(da_mat * dt, axis=0, keepdims=True)
        ddt_pre = jnp.where(lane < SSD_HEADS, ddt * _sigmoid(dt_pre), 0.0)
        ddt_ref[...] = ddt_pre.astype(ddt_ref.dtype)
        rid = lax.broadcasted_iota(jnp.int32, (SUBLANES, LANES), 0)
        gv_ref[...] += (jnp.where(rid == 0, jnp.sum(ddt_pre, axis=0, keepdims=True), 0.0)
                        + jnp.where(rid == 1, da_vec, 0.0) + jnp.where(rid == 2, dd_vec, 0.0))

        def conv_bwd(dpost, pre, w_ref, ext_ref, dext_ref, carry_ref, cur_ref, out_ref, g_ref, width):
            dco = dpost * _dsilu(pre)
            dext_ref[0:n, :] = dco
            dext_ref[n:n + SUBLANES, :] = carry_ref[...]
            out_ref[...] = (w_ref[3:4, :] * dco + w_ref[2:3, :] * dext_ref[pl.ds(1, n), :]
                            + w_ref[1:2, :] * dext_ref[pl.ds(2, n), :] + w_ref[0:1, :] * dext_ref[pl.ds(3, n), :]
                            ).astype(out_ref.dtype)
            carry_ref[...] = dco[0:SUBLANES, :]
            rid8 = lax.broadcasted_iota(jnp.int32, (SUBLANES, width), 0)
            acc = jnp.where(rid8 == 3, jnp.sum(dco * cur_ref[...], axis=0, keepdims=True), 0.0)
            for j in range(3):
                sh = ext_ref[pl.ds(SUBLANES - 3 + j, n), :]
                acc = acc + jnp.where(rid8 == j, jnp.sum(dco * sh, axis=0, keepdims=True), 0.0)
            acc = acc + jnp.where(rid8 == 4, jnp.sum(dco, axis=0, keepdims=True), 0.0)
            g_ref[...] += acc

        conv_bwd(dx_scr[...], pre_x, cwx_ref, xe_ref, dxe_ref, cx_ref, xs_ref, dxs_ref, gx_ref, 384)
        conv_bwd(dbc_scr[...], pre_bc, cwb_ref, be_ref, dbe_ref, cb_ref, bc_ref, dbc_ref, gb_ref, 512)

    t = nb * s
    rowblk = lambda width: pl.BlockSpec((n, width), lambda b, c: (b * nc + cidx(c), 0))
    return pl.pallas_call(
        body, name=name, grid=(nb, nc),
        in_specs=[blk(512, H_BC), halo(512, H_BC), blk(384, H_Z), blk(384, H_XS), halo(384, H_XS), blk(128, H_DT),
                  pl.BlockSpec((None, SSD_WIDTH, SSD_STATE), lambda b, c: (b * nc + cidx(c), 0, 0)),
                  pl.BlockSpec((n, SSD_WIDTH), lambda b, c: (b * nc + cidx(c), 0)),
                  full((4, 384)), full((1, 384)), full((4, 512)), full((1, 512)), full((SUBLANES, LANES)), full((1, 384))],
        out_specs=[rowblk(384), rowblk(384), rowblk(512), rowblk(128),
                   full((SUBLANES, 384)), full((SUBLANES, 512)), full((SUBLANES, LANES)), full((1, 384))],
        out_shape=[jax.ShapeDtypeStruct((t, 384), MXU_DTYPE), jax.ShapeDtypeStruct((t, 384), MXU_DTYPE),
                   jax.ShapeDtypeStruct((t, 512), MXU_DTYPE), jax.ShapeDtypeStruct((t, 128), MXU_DTYPE),
                   jax.ShapeDtypeStruct((SUBLANES, 384), F32), jax.ShapeDtypeStruct((SUBLANES, 512), F32),
                   jax.ShapeDtypeStruct((SUBLANES, LANES), F32), jax.ShapeDtypeStruct((1, 384), F32)],
        scratch_shapes=[pltpu.VMEM((SSD_WIDTH, SSD_STATE), F32), pltpu.VMEM((n + SUBLANES, 384), F32),
                        pltpu.VMEM((n + SUBLANES, 512), F32), pltpu.VMEM((n, SSD_WIDTH), F32),
                        pltpu.VMEM((n, 384), F32), pltpu.VMEM((n, 512), F32),
                        pltpu.VMEM((n + SUBLANES, 384), F32), pltpu.VMEM((n + SUBLANES, 512), F32),
                        pltpu.VMEM((SUBLANES, 384), F32), pltpu.VMEM((SUBLANES, 512), F32)],
        compiler_params=_params(2),
    )(h, h, h, h, h, h, hstate, dmix, cwx, cbx, cwb, cbb, vec, nw)


def _adamw_math(w, g, m, v):
    m = ADAM_B1 * m + (1.0 - ADAM_B1) * g
    v = ADAM_B2 * v + (1.0 - ADAM_B2) * (g * g)
    m_hat = m / (1.0 - ADAM_B1 ** ADAM_STEP)
    v_hat = v / (1.0 - ADAM_B2 ** ADAM_STEP)
    delta = -ADAM_LR * (m_hat / (jnp.sqrt(v_hat) + ADAM_EPS) + ADAM_WD * w)
    return delta, m, v


def _adamw(w, g, m, v, name, tr=256):
    rows, cols = w.shape
    tr = rows if rows <= tr else tr
    assert rows % tr == 0, (rows, tr)

    def body(w_ref, g_ref, m_ref, v_ref, d_ref, nm_ref, nv_ref):
        d, nm, nv = _adamw_math(w_ref[...], g_ref[...], m_ref[...], v_ref[...])
        d_ref[...] = d
        nm_ref[...] = nm
        nv_ref[...] = nv

    spec = pl.BlockSpec((tr, cols), lambda i: (i, 0))
    shp = jax.ShapeDtypeStruct((rows, cols), F32)
    return pl.pallas_call(body, name=name, grid=(rows // tr,), in_specs=[spec] * 4, out_specs=[spec] * 3,
                          out_shape=[shp] * 3, compiler_params=_params(1))(w, g, m, v)


def _sum8_layers(parts, name, tr):
    _, rows, cols = parts[0].shape
    assert rows % tr == 0
    nt = rows // tr

    def body(*refs):
        o_ref = refs[DEPTH]
        layer = pl.program_id(0)
        for l in range(DEPTH):
            @pl.when(layer == l)
            def _(l=l):
                acc = refs[l][0]
                for k in range(1, N_DEV):
                    acc = acc + refs[l][k]
                o_ref[...] = acc

    in_specs = [pl.BlockSpec((N_DEV, tr, cols), lambda a, i, l=l: (0, jnp.clip(i + (a - l) * nt, 0, nt - 1), 0))
                for l in range(DEPTH)]
    return pl.pallas_call(body, name=name, grid=(DEPTH, nt), in_specs=in_specs,
                          out_specs=pl.BlockSpec((None, tr, cols), lambda a, i: (a, i, 0)),
                          out_shape=jax.ShapeDtypeStruct((DEPTH, rows, cols), F32), compiler_params=_params(2))(*parts)


def _all_reduce_small(vec, name):
    rows, cols = vec.shape

    def body(x_ref, out_ref, gbuf, send_sems, recv_sems):
        x, y, c = lax.axis_index("x"), lax.axis_index("y"), lax.axis_index("c")
        me, sibling = (x, y, c), (x, y, 1 - c)
        chips = [(1 - x, y), (x, 1 - y), (1 - x, 1 - y)]

        def slot(px, py, pc):
            return gbuf.at[4 * px + 2 * py + pc]

        def copy(k, block, to, src=None):
            return pltpu.make_async_remote_copy(
                src_ref=slot(*block) if src is None else src, dst_ref=slot(*block),
                send_sem=send_sems.at[k], recv_sem=recv_sems.at[k], device_id=to, device_id_type=MESH_ID)

        first = [copy(0, me, sibling, src=x_ref)]
        first += [copy(1 + j, me, (*chip, c), src=x_ref) for j, chip in enumerate(chips)]
        for cp in first:
            cp.start()
        gbuf[4 * x + 2 * y + c] = x_ref[...]
        passed = [copy(4 + j, (*chip, c), sibling) for j, chip in enumerate(chips)]
        for j, chip in enumerate(chips):
            copy(1 + j, (*chip, c), me).wait_recv()
            passed[j].start()
        copy(0, sibling, me).wait_recv()
        for j, chip in enumerate(chips):
            copy(4 + j, (*chip, 1 - c), me).wait_recv()
        for cp in first + passed:
            cp.wait_send()
        acc = gbuf[0]
        for k in range(1, N_DEV):
            acc = acc + gbuf[k]
        out_ref[...] = acc

    return pl.pallas_call(
        body, name=name, out_shape=jax.ShapeDtypeStruct((rows, cols), F32),
        in_specs=[pl.BlockSpec(memory_space=pltpu.VMEM)], out_specs=pl.BlockSpec(memory_space=pltpu.VMEM),
        scratch_shapes=[pltpu.VMEM((N_DEV, rows, cols), F32), pltpu.SemaphoreType.DMA((7,)), pltpu.SemaphoreType.DMA((7,))],
        compiler_params=pltpu.CompilerParams(has_side_effects=True, vmem_limit_bytes=VMEM_LIMIT_BYTES),
    )(vec)


_COL_POOL, _COL_Z, _COL_XBC, _COL_DT, _COL_Q, _COL_K, _COL_V = 0, 256, 640, 1536, 1542, 1926, 2310
_H_SEGMENTS = ((_COL_XBC + SSD_WIDTH, 512), (_COL_POOL, 256), (_COL_Q, 384), (_COL_K, 384), (_COL_V, 384),
               (_COL_Z, 384), (_COL_XBC, 384), (_COL_DT, 6))


def _h_from_orig(w):
    parts = [w[..., o:o + n] for o, n in _H_SEGMENTS]
    pad = jnp.zeros(w.shape[:-1] + (H_COLS - IN_COLS,), w.dtype)
    return jnp.concatenate(parts + [pad], axis=-1)


def _h_to_orig(w):
    offs, o = {}, 0
    for orig, n in _H_SEGMENTS:
        offs[orig] = (o, n)
        o += n
    order = sorted(offs)
    return jnp.concatenate([w[..., offs[k][0]:offs[k][0] + offs[k][1]] for k in order], axis=-1)


def _interleave(w):
    lead = w.shape[:-1]
    nt = D_FF // GLU_TILE
    return jnp.swapaxes(w.reshape(lead + (2, nt, GLU_TILE)), -3, -2).reshape(lead + (2 * D_FF,))


def _deinterleave(w):
    lead = w.shape[:-1]
    nt = D_FF // GLU_TILE
    return jnp.swapaxes(w.reshape(lead + (nt, 2, GLU_TILE)), -3, -2).reshape(lead + (2 * D_FF,))


def _mix_rows_from_orig(w):
    return jnp.concatenate([w[256:640], w[640:1024], w[0:256]], axis=0)


def _mix_rows_to_orig(w):
    return jnp.concatenate([w[768:1024], w[0:384], w[384:768]], axis=0)


def _xbc_split(w):
    return w[..., :SSD_WIDTH], w[..., SSD_WIDTH:]


def _layer_fwd(x, p_l, wt, sp, nb, s, comm=None):
    h = _mm(x, wt["w_in"], "nn", F32, "mm_in", tm=1024, tn=1408)
    pool_out = _pool_fwd(h, wt["pool_bd"], sp["pool_scale"], nb, s, "pool_fwd")
    ssd_out, hstate = _ssd_fwd(h, sp["cwx"], sp["cbx"], sp["cwb"], sp["cbb"], sp["ssd_vec"], sp["ssd_norm_w"], nb, s, "ssd_fwd")
    sb_out, comm_out = _sb_fwd(h, nb, s, "sb_fwd" if comm is None else "sb_fwd_gather", comm)
    mixcat = jnp.concatenate([ssd_out, sb_out, pool_out], axis=1)
    mix = _mm(mixcat, wt["w_out"], "nn", F32, "mm_out", tm=1024, tn=1024)
    x1, r1 = _ln_fwd(x, mix, sp["ln1"], "ln1_fwd")
    up = _mm(x1, wt["w_up"], "nn", F32, "mm_up", tm=1024, tn=1408)
    act = _glu_fwd(up, sp["ffn_cw"], sp["ffn_cb"], nb, s, "glu_fwd")
    ffn = _mm(act, wt["w_down"], "nn", F32, "mm_down", tm=1024, tn=1024, tk=1408)
    gp = _mm(x1, wt["w_gate"], "nn", F32, "mm_gate", tm=1024, tn=1024)
    pp = _mm(p_l, wt["w_proj"], "nn", F32, "mm_proj", tm=2048, tn=1024)
    x2, r2 = _ln_fwd(x1, ffn, sp["ln2"], "ln2_fwd", gp=gp, pp=pp)
    return x2, dict(x=x, h=h, hstate=hstate, mixcat=mixcat, r1=r1, x1=x1, up=up, act=act, gp=gp, pp=pp, r2=r2), comm_out


def _layer_bwd(dx2, p_l, sv, wt, sp, nb, s, comm=None):
    dr2, dgp, dpp, st2 = _ln_bwd(sv["r2"], sp["ln2"], dx2, "ln2_bwd", gp=sv["gp"], pp=sv["pp"])
    g_down = _mm(sv["act"], dr2, "tn", F32, "wg_down", tm=1408, tn=1024, tk=512)
    dact = _mm(dr2, wt["w_down"], "nt", F32, "dg_down", tm=1024, tn=1408)
    dup, ffn_acc = _glu_bwd(sv["up"], dact, sp["ffn_cw"], sp["ffn_cb"], nb, s, "glu_bwd")
    g_up = _mm(sv["x1"], dup, "tn", F32, "wg_up", tm=1024, tn=2816, tk=512)
    g_gate = _mm(sv["x1"], dgp, "tn", F32, "wg_gate", tm=1024, tn=1024, tk=512)
    g_proj = _mm(p_l, dpp, "tn", F32, "wg_proj", tm=256, tn=1024, tk=512)
    t1 = _mm(dgp, wt["w_gate"], "nt", F32, "dg_gate", tm=1024, tn=1024, add=dr2, add_coef=ALPHA)
    dx1 = _mm(dup, wt["w_up"], "nt", F32, "dg_up", tm=1024, tn=1024, tk=1408, add=t1)
    dr1, st1 = _ln_bwd(sv["r1"], sp["ln1"], dx1, "ln1_bwd")
    g_out = _mm(sv["mixcat"], dr1, "tn", F32, "wg_out", tm=1024, tn=1024, tk=512)
    dmix = _mm(dr1, wt["w_out"], "nt", F32, "dg_out", tm=1024, tn=1024)
    du, g_pool_bd, g_pool_scale = _pool_bwd(sv["h"], dmix, wt["pool_bd"], sp["pool_scale"], nb, s, "pool_bwd")
    dz, dxs, dbc, ddt, gx, gb, gv, gn = _ssd_bwd(sv["h"], sv["hstate"], dmix, sp["cwx"], sp["cbx"], sp["cwb"], sp["cbb"],
                                                  sp["ssd_vec"], sp["ssd_norm_w"], nb, s, "ssd_bwd")
    ready = dict(w_out=g_out, ffn_w_up=g_up, ffn_w_down=g_down, ple_w_gate=g_gate, ple_w_proj=g_proj)
    job = comm(ready) if comm is not None else None
    dq, dk, dv, comm_out = _sb_bwd(sv["h"], dmix, nb, s, "sb_bwd" if job is None else "sb_bwd_x%d" % job["n_xfers"], job)
    dh = jnp.concatenate([dbc, du, dq.astype(MXU_DTYPE), dk.astype(MXU_DTYPE), dv.astype(MXU_DTYPE), dz, dxs, ddt], axis=1)
    g_in = _mm(sv["x"], dh, "tn", F32, "wg_in", tm=1024, tn=2816, tk=512)
    dx = _mm(dh, wt["w_in"], "nt", F32, "dg_in", tm=1024, tn=1024, tk=1408, add=dr1, add_coef=ALPHA)
    small = dict(
        pool_w=jnp.stack([g_pool_bd[HEAD_DIM * g:HEAD_DIM * (g + 1), HEAD_DIM * g:HEAD_DIM * (g + 1)] for g in range(4)]),
        pool_scale=g_pool_scale[0],
        ssd_conv_w=jnp.concatenate([gx[0:4], gb[0:4]], axis=1),
        ssd_conv_b=jnp.concatenate([gx[4], gb[4]], axis=0),
        ssd_dt_bias=gv[0, :SSD_HEADS],
        ssd_a_log=gv[1, :SSD_HEADS] * sp["ssd_vec"][1, :SSD_HEADS],
        ssd_d=gv[2, :SSD_HEADS],
        ssd_norm_w=gn[0],
        ln1_g=st1[0], ln1_b=st1[1], ln2_g=st2[0], ln2_b=st2[1],
        ffn_conv_w=_deinterleave(ffn_acc[0:3]),
        ffn_conv_b=_deinterleave(ffn_acc[3]),
    )
    return dx, dict(ready, w_in=g_in), small, comm_out


def _layer_params(i, big, rep):
    pool_bd = jnp.zeros((POOL_WIDTH, POOL_WIDTH), F32)
    for g in range(4):
        pool_bd = lax.dynamic_update_slice(pool_bd, rep["pool_w"][i, g], (HEAD_DIM * g, HEAD_DIM * g))
    wt = dict(w_in=big["w_in"], w_out=big["w_out"], w_up=big["ffn_w_up"], w_down=big["ffn_w_down"],
              w_gate=big["ple_w_gate"], w_proj=big["ple_w_proj"], pool_bd=pool_bd.astype(MXU_DTYPE))
    cwx, cwb = _xbc_split(rep["ssd_conv_w"][i])
    cbx, cbb = _xbc_split(rep["ssd_conv_b"][i][None, :])
    vec = jnp.zeros((SUBLANES, LANES), F32)
    vec = vec.at[0, :SSD_HEADS].set(rep["ssd_dt_bias"][i])
    vec = vec.at[1, :SSD_HEADS].set(-jnp.exp(rep["ssd_a_log"][i]))
    vec = vec.at[2, :SSD_HEADS].set(rep["ssd_d"][i])
    sp = dict(pool_scale=rep["pool_scale"][i][None, :], cwx=cwx, cbx=cbx, cwb=cwb, cbb=cbb, ssd_vec=vec,
              ssd_norm_w=rep["ssd_norm_w"][i][None, :],
              ln1=jnp.stack([rep["ln1_g"][i], rep["ln1_b"][i]]), ln2=jnp.stack([rep["ln2_g"][i], rep["ln2_b"][i]]),
              ffn_cw=_interleave(rep["ffn_conv_w"][i]), ffn_cb=_interleave(rep["ffn_conv_b"][i][None, :]))
    return wt, sp


def _run_layers(x, p, target, big_w, rep, fwd_job=None, fwd_done=None, bwd_job=None, bwd_done=None):
    nb, s, d = x.shape
    t = nb * s
    xf = x.reshape(t, d)
    saved, params = [], []
    for i in range(DEPTH):
        wt, sp = _layer_params(i, big_w[i], rep)
        params.append((wt, sp))
        job = fwd_job(i) if fwd_job is not None else None
        xf, sv, res = _layer_fwd(xf, p[i].reshape(t, PLE_DIM), wt, sp, nb, s, job)
        if job is not None:
            fwd_done(i, res)
        saved.append(sv)
    dy, loss = _loss_grad(xf, target.reshape(t, d), "loss")
    bigs, smalls = [None] * DEPTH, [None] * DEPTH
    for i in reversed(range(DEPTH)):
        wt, sp = params[i]
        job = (lambda ready, i=i: bwd_job(i, bigs, ready)) if bwd_job is not None else None
        dy, bigs[i], smalls[i], res = _layer_bwd(dy, p[i].reshape(t, PLE_DIM), saved[i], wt, sp, nb, s, job)
        if job is not None:
            bwd_done(i, res)
    return loss, dy.reshape(nb, s, d), bigs, smalls


def _local_step(x, p, target, full, rep):
    return _run_layers(x, p, target, [{n: full[n][i] for n in full} for i in range(DEPTH)], rep)


BIG = ("w_in", "w_out", "ffn_w_up", "ffn_w_down", "ple_w_gate", "ple_w_proj")
SMALL_REPLICATED = ("pool_w", "pool_scale", "ssd_conv_b", "ssd_dt_bias", "ssd_a_log", "ssd_d", "ssd_norm_w",
                    "ln1_g", "ln1_b", "ffn_conv_b", "ln2_g", "ln2_b")
SMALL_SHARDED = ("ssd_conv_w", "ffn_conv_w")
WEIGHTS = ("w_in", "pool_w", "pool_scale", "ssd_conv_w", "ssd_conv_b", "ssd_dt_bias", "ssd_a_log", "ssd_d", "ssd_norm_w",
           "w_out", "ln1_g", "ln1_b", "ffn_w_up", "ffn_conv_w", "ffn_conv_b", "ffn_w_down", "ln2_g", "ln2_b",
           "ple_w_gate", "ple_w_proj")
SUM_BLOCK_BYTES = 3 * 512 * 1024


def _to_rows(a, cols):
    f = a.reshape(-1)
    pad = (-f.shape[0]) % cols
    if pad:
        f = jnp.concatenate([f, jnp.zeros((pad,), f.dtype)])
    return f.reshape(-1, cols)


def _pack_rows(arrs, cols, row_mult):
    rows = [_to_rows(a, cols) for a in arrs]
    flat = jnp.concatenate(rows, axis=0)
    pad = (-flat.shape[0]) % row_mult
    if pad:
        flat = jnp.concatenate([flat, jnp.zeros((pad, cols), flat.dtype)], axis=0)
    return flat


def _unpack_rows(flat, shapes, cols):
    out, r = [], 0
    for shp in shapes:
        n = 1
        for v in shp:
            n *= v
        nr = -(-n // cols)
        out.append(flat[r:r + nr].reshape(-1)[:n].reshape(shp))
        r += nr
    return out


def kernel(x, p, w_in, pool_w, pool_scale, ssd_conv_w, ssd_conv_b, ssd_dt_bias, ssd_a_log, ssd_d, ssd_norm_w, w_out, ln1_g, ln1_b, ffn_w_up, ffn_conv_w, ffn_conv_b, ffn_w_down, ln2_g, ln2_b, ple_w_gate, ple_w_proj, loss_target, m_w_in, m_pool_w, m_pool_scale, m_ssd_conv_w, m_ssd_conv_b, m_ssd_dt_bias, m_ssd_a_log, m_ssd_d, m_ssd_norm_w, m_w_out, m_ln1_g, m_ln1_b, m_ffn_w_up, m_ffn_conv_w, m_ffn_conv_b, m_ffn_w_down, m_ln2_g, m_ln2_b, m_ple_w_gate, m_ple_w_proj, v_w_in, v_pool_w, v_pool_scale, v_ssd_conv_w, v_ssd_conv_b, v_ssd_dt_bias, v_ssd_a_log, v_ssd_d, v_ssd_norm_w, v_w_out, v_ln1_g, v_ln1_b, v_ffn_w_up, v_ffn_conv_w, v_ffn_conv_b, v_ffn_w_down, v_ln2_g, v_ln2_b, v_ple_w_gate, v_ple_w_proj):
    wts = dict(w_in=w_in, pool_w=pool_w, pool_scale=pool_scale, ssd_conv_w=ssd_conv_w, ssd_conv_b=ssd_conv_b,
               ssd_dt_bias=ssd_dt_bias, ssd_a_log=ssd_a_log, ssd_d=ssd_d, ssd_norm_w=ssd_norm_w, w_out=w_out, ln1_g=ln1_g,
               ln1_b=ln1_b, ffn_w_up=ffn_w_up, ffn_conv_w=ffn_conv_w, ffn_conv_b=ffn_conv_b, ffn_w_down=ffn_w_down,
               ln2_g=ln2_g, ln2_b=ln2_b, ple_w_gate=ple_w_gate, ple_w_proj=ple_w_proj)
    mom_m = dict(w_in=m_w_in, pool_w=m_pool_w, pool_scale=m_pool_scale, ssd_conv_w=m_ssd_conv_w, ssd_conv_b=m_ssd_conv_b,
                 ssd_dt_bias=m_ssd_dt_bias, ssd_a_log=m_ssd_a_log, ssd_d=m_ssd_d, ssd_norm_w=m_ssd_norm_w, w_out=m_w_out,
                 ln1_g=m_ln1_g, ln1_b=m_ln1_b, ffn_w_up=m_ffn_w_up, ffn_conv_w=m_ffn_conv_w, ffn_conv_b=m_ffn_conv_b,
                 ffn_w_down=m_ffn_w_down, ln2_g=m_ln2_g, ln2_b=m_ln2_b, ple_w_gate=m_ple_w_gate, ple_w_proj=m_ple_w_proj)
    mom_v = dict(w_in=v_w_in, pool_w=v_pool_w, pool_scale=v_pool_scale, ssd_conv_w=v_ssd_conv_w, ssd_conv_b=v_ssd_conv_b,
                 ssd_dt_bias=v_ssd_dt_bias, ssd_a_log=v_ssd_a_log, ssd_d=v_ssd_d, ssd_norm_w=v_ssd_norm_w, w_out=v_w_out,
                 ln1_g=v_ln1_g, ln1_b=v_ln1_b, ffn_w_up=v_ffn_w_up, ffn_conv_w=v_ffn_conv_w, ffn_conv_b=v_ffn_conv_b,
                 ffn_w_down=v_ffn_w_down, ln2_g=v_ln2_g, ln2_b=v_ln2_b, ple_w_gate=v_ple_w_gate, ple_w_proj=v_ple_w_proj)
    me = 4 * lax.axis_index("x") + 2 * lax.axis_index("y") + lax.axis_index("c")

    def layer_shards(i):
        sh = {n: wts[n][i].astype(MXU_DTYPE) for n in BIG}
        sh["w_in"] = _h_from_orig(wts["w_in"][i]).astype(MXU_DTYPE)
        return sh

    def gathered_weights(res):
        big = dict(zip(BIG, res[:len(BIG)]))
        big["ffn_w_up"] = _interleave(jnp.swapaxes(big["ffn_w_up"], 0, 1).reshape(D_MODEL, 2 * D_FF))
        return big

    res0 = _comm_call(_gather_job(layer_shards(0), [wts[n] for n in SMALL_SHARDED]), "gather_layer0")
    big_w = [gathered_weights(res0)] + [None] * (DEPTH - 1)
    rep = {n: wts[n] for n in SMALL_REPLICATED}
    for n, g in zip(SMALL_SHARDED, res0[len(BIG):]):
        rep[n] = jnp.transpose(g, (1, 2, 0, 3)).reshape(g.shape[1], g.shape[2], N_DEV * g.shape[3])

    def fwd_job(i):
        return _gather_job(layer_shards(i + 1)) if i + 1 < DEPTH else None

    def fwd_done(i, res):
        big_w[i + 1] = gathered_weights(res)

    received = [dict() for _ in range(DEPTH)]
    carried = ("w_out", "ffn_w_up", "ffn_w_down", "ple_w_gate", "ple_w_proj")

    def bwd_items(i, bigs, ready):
        items = [(i, n, ready[n]) for n in carried] + ([(i + 1, "w_in", bigs[i + 1]["w_in"])] if i + 1 < DEPTH else [])
        return [(l, n, jnp.swapaxes(_deinterleave(g).reshape(D_MODEL, N_DEV, 704), 0, 1) if n == "ffn_w_up" else g)
                for l, n, g in items]

    pending = {}

    def bwd_job(i, bigs, ready):
        pending[i] = bwd_items(i, bigs, ready)
        return _exchange_job([(n, g) for _, n, g in pending[i]])

    def bwd_done(i, res):
        for (l, n, _), r in zip(pending[i], res):
            received[l][n] = r

    loss_loc, grad_x, bigs, smalls = _run_layers(x, p, loss_target, big_w, rep, fwd_job, fwd_done, bwd_job, bwd_done)
    received[0]["w_in"] = _comm_call(_exchange_job([("w_in", bigs[0]["w_in"])]), "exchange_w_in0")[0]

    grads = {}
    for n in BIG:
        parts = [received[i][n] for i in range(DEPTH)]
        _, rows, cols = parts[0].shape
        tr = next(t for t in (256, 128, 64, 32, 16, 8) if rows % t == 0 and N_DEV * t * cols * 4 <= SUM_BLOCK_BYTES)
        g = _sum8_layers(parts, "sum_" + n, tr)
        grads[n] = _h_to_orig(g) if n == "w_in" else g
    small_names = SMALL_REPLICATED + SMALL_SHARDED
    small_full_shapes = [rep[n].shape for n in small_names]
    small_vec = _pack_rows([jnp.stack([smalls[i][n] for i in range(DEPTH)]) for n in small_names] + [loss_loc[0, :1]],
                           LANES, SUBLANES)
    small_sum = _all_reduce_small(small_vec, "allreduce_small")
    small_out = _unpack_rows(small_sum, small_full_shapes + [(1,)], LANES)
    loss = small_out[-1][0]
    for n, g in zip(small_names, small_out[:-1]):
        if n in SMALL_SHARDED:
            width = wts[n].shape[-1]
            g = lax.dynamic_slice_in_dim(g, me * width, width, axis=g.ndim - 1)
        grads[n] = g

    delta, new_m, new_v = {}, {}, {}
    for n in BIG:
        shp = wts[n].shape
        two_d = lambda a: a.reshape(-1, shp[-1])
        tr = {"w_in": 128, "ffn_w_down": 352}.get(n, 256)
        d_, m_, v_ = _adamw(two_d(wts[n]), two_d(grads[n]), two_d(mom_m[n]), two_d(mom_v[n]), "adamw_" + n, tr=tr)
        delta[n], new_m[n], new_v[n] = d_.reshape(shp), m_.reshape(shp), v_.reshape(shp)
    packs = [_pack_rows([src[n] for n in small_names], LANES, SUBLANES) for src in (wts, grads, mom_m, mom_v)]
    outs = _adamw(*packs, "adamw_small", tr=packs[0].shape[0])
    shapes = [wts[n].shape for n in small_names]
    for dst, flat in zip((delta, new_m, new_v), outs):
        for n, a in zip(small_names, _unpack_rows(flat, shapes, LANES)):
            dst[n] = a
    return (loss, grad_x, *[grads[n] for n in WEIGHTS], *[delta[n] for n in WEIGHTS],
            *[new_m[n] for n in WEIGHTS], *[new_v[n] for n in WEIGHTS])
```

```python
import functools

import jax
import jax.numpy as jnp
from jax import lax
from jax.experimental import pallas as pl
from jax.experimental.pallas import tpu as pltpu

F32 = jnp.float32
BF16 = jnp.bfloat16
MXU_DTYPE = jnp.bfloat16

D_MODEL = 1024
DEPTH = 4
PLE_DIM = 256
ALPHA = (2 * DEPTH) ** 0.25
LN_EPS = 1e-5
RMS_EPS = 1e-6
HEAD_DIM = 64
POOL_WIDTH = 256
POOL_WINDOWS = (2, 4, 8, 16)
SSD_WIDTH = 384
SSD_HEADS = 6
SSD_STATE = 128
SSD_XBC = 896
SB_WIDTH = 384
IN_COLS = 2694
D_FF = 2816
N_DEV = 8

ADAM_LR = 0.001
ADAM_B1 = 0.9
ADAM_B2 = 0.999
ADAM_EPS = 1e-08
ADAM_WD = 0.01
ADAM_STEP = 10

LANES = 128
SUBLANES = 8
VMEM_LIMIT_BYTES = 56 * 1024 * 1024

H_COLS = 2816
H_BC = 0
H_POOL = 512
H_Q = 768
H_K = 1152
H_V = 1536
H_Z = 1920
H_XS = 2304
H_DT = 2688
SSD_CHUNK = 128
QB = 256
GLU_TILE = 256
MASKED_LOG = -1e30

NN = ((1,), (0,))
NT = ((1,), (1,))
TN = ((0,), (0,))


def _dot(a, b, dims=NN):
    return lax.dot_general(a.astype(MXU_DTYPE), b.astype(MXU_DTYPE), (dims, ((), ())), preferred_element_type=F32)


def _dot_exact01(x, m01, dims=NN, x_left=True, terms=3):
    acc = None
    r = x
    for _ in range(terms):
        hi = r.astype(BF16)
        ops = (hi, m01) if x_left else (m01, hi)
        part = lax.dot_general(ops[0], ops[1], (dims, ((), ())), preferred_element_type=F32)
        acc = part if acc is None else acc + part
        r = r - hi.astype(F32)
    return acc


def _sigmoid(v):
    return 1.0 / (1.0 + jnp.exp(-v))


def _silu(v):
    return v * _sigmoid(v)


def _dsilu(v):
    s = _sigmoid(v)
    return s * (1.0 + v * (1.0 - s))


def _softplus(v):
    return jnp.maximum(v, 0.0) + jnp.log(1.0 + jnp.exp(-jnp.abs(v)))


def _params(n_axes, side_effects=False):
    return pltpu.CompilerParams(dimension_semantics=("arbitrary",) * n_axes, vmem_limit_bytes=VMEM_LIMIT_BYTES,
                                has_side_effects=side_effects)


MESH_ID = pl.DeviceIdType.MESH
_ANY = pl.BlockSpec(memory_space=pl.ANY)


def _flip(v, bit):
    return 1 - v if bit else v


def _comm_counts(comm):
    return (0, 0) if comm is None else (len(comm["inputs"]), len(comm["out_shapes"]))


def _comm_call_args(comm):
    if comm is None:
        return [], [], [], []
    n = comm["n_xfers"]
    sems = [pltpu.SemaphoreType.DMA(((N_DEV - 1) * n,)), pltpu.SemaphoreType.DMA(((N_DEV - 1) * n,)),
            pltpu.SemaphoreType.DMA((n,))]
    return list(comm["inputs"]), [_ANY] * len(comm["out_shapes"]), list(comm["out_shapes"]), sems


def _comm_descs(comm, in_refs, tail_refs, with_recvs=True):
    n_out = len(comm["out_shapes"])
    out_refs, (send_sems, recv_sems, local_sems) = tail_refs[:n_out], tail_refs[n_out:n_out + 3]
    xfers = comm["xfers"](in_refs, out_refs)
    n = len(xfers)
    assert n == comm["n_xfers"]
    x, y, c = lax.axis_index("x"), lax.axis_index("y"), lax.axis_index("c")
    me = 4 * x + 2 * y + c
    local = [pltpu.make_async_copy(src_for(me), dst_for(me), local_sems.at[t]) for t, (src_for, dst_for) in enumerate(xfers)]
    sends, recvs = [], []
    for k in range(1, N_DEV):
        pid = (_flip(x, k & 4), _flip(y, k & 2), _flip(c, k & 1))
        peer = 4 * pid[0] + 2 * pid[1] + pid[2]
        for t, (src_for, dst_for) in enumerate(xfers):
            idx = (k - 1) * n + t
            sends.append(pltpu.make_async_remote_copy(
                src_ref=src_for(peer), dst_ref=dst_for(me), send_sem=send_sems.at[idx], recv_sem=recv_sems.at[idx],
                device_id=pid, device_id_type=MESH_ID))
            if with_recvs:
                recvs.append(pltpu.make_async_remote_copy(
                    src_ref=src_for(peer), dst_ref=dst_for(peer), send_sem=send_sems.at[idx], recv_sem=recv_sems.at[idx],
                    device_id=pid, device_id_type=MESH_ID))
    return local, sends, recvs


def _comm_start(descs):
    local, sends, _ = descs
    for cp in local + sends:
        cp.start()


def _comm_wait(descs):
    local, sends, recvs = descs
    for cp in recvs:
        cp.wait_recv()
    for cp in sends:
        cp.wait_send()
    for cp in local:
        cp.wait()


def _comm_hosted(comm, in_refs, tail_refs, grid):
    if comm is None:
        return
    ids = [pl.program_id(a) for a in range(len(grid))]
    first = functools.reduce(jnp.logical_and, [i == 0 for i in ids])
    last = functools.reduce(jnp.logical_and, [i == g - 1 for i, g in zip(ids, grid)])

    @pl.when(first)
    def _():
        _comm_start(_comm_descs(comm, in_refs, tail_refs, with_recvs=False))

    @pl.when(last)
    def _():
        _comm_wait(_comm_descs(comm, in_refs, tail_refs))


def _comm_call(comm, name):
    n_in = len(comm["inputs"])

    def body(*refs):
        descs = _comm_descs(comm, refs[:n_in], refs[n_in:])
        _comm_start(descs)
        _comm_wait(descs)

    c_in, c_specs, c_shapes, c_scratch = _comm_call_args(comm)
    return pl.pallas_call(body, name=name, in_specs=[_ANY] * n_in, out_specs=c_specs, out_shape=c_shapes,
                          scratch_shapes=c_scratch, compiler_params=pltpu.CompilerParams(has_side_effects=True))(*c_in)


def _rows(ref, j, n):
    return ref.at[pl.ds(pl.multiple_of(j * n, SUBLANES), n), :]


def _gather_job(sh, conv=None):
    conv = list(conv or [])
    sds = jax.ShapeDtypeStruct
    out_shapes = [sds((D_MODEL, H_COLS), MXU_DTYPE), sds((D_MODEL, D_MODEL), MXU_DTYPE), sds((N_DEV, D_MODEL, 704), MXU_DTYPE),
                  sds((D_FF, D_MODEL), MXU_DTYPE), sds((D_MODEL, D_MODEL), MXU_DTYPE), sds((PLE_DIM, D_MODEL), MXU_DTYPE)]
    out_shapes += [sds((N_DEV,) + a.shape, a.dtype) for a in conv]

    def xfers(ins, outs):
        whole = lambda a: (lambda j: a)
        r = [(whole(ins[0]), lambda j: _rows(outs[0], j, 128)),
             (whole(ins[1]), lambda j: _rows(outs[1], lax.rem(j + 6, N_DEV), 128)),
             (whole(ins[2]), lambda j: outs[2].at[j]),
             (whole(ins[3]), lambda j: _rows(outs[3], j, 352)),
             (whole(ins[4]), lambda j: _rows(outs[4], j, 128)),
             (whole(ins[5]), lambda j: outs[5].at[:, pl.ds(pl.multiple_of(j * LANES, LANES), LANES)])]
        for t in range(len(conv)):
            r.append((whole(ins[6 + t]), lambda j, o=outs[6 + t]: o.at[j]))
        return r

    return dict(inputs=[sh[n] for n in BIG] + conv, out_shapes=out_shapes, xfers=xfers, n_xfers=6 + len(conv))


_SHARD_SHAPES = {"w_in": (128, H_COLS), "w_out": (128, D_MODEL), "ffn_w_up": (D_MODEL, 704), "ffn_w_down": (352, D_MODEL),
                 "ple_w_gate": (128, D_MODEL), "ple_w_proj": (PLE_DIM, LANES)}


def _exchange_job(items):
    def source(name, ref):
        if name in ("w_in", "ple_w_gate"):
            return lambda j: _rows(ref, j, 128)
        if name == "w_out":
            return lambda j: _rows(ref, lax.rem(j + 6, N_DEV), 128)
        if name == "ffn_w_up":
            return lambda j: ref.at[j]
        if name == "ffn_w_down":
            return lambda j: _rows(ref, j, 352)
        assert name == "ple_w_proj"
        return lambda j: ref.at[:, pl.ds(pl.multiple_of(j * LANES, LANES), LANES)]

    def xfers(ins, outs):
        return [(source(name, i), lambda j, o=o: o.at[j]) for (name, _), i, o in zip(items, ins, outs)]

    return dict(inputs=[g for _, g in items], xfers=xfers, n_xfers=len(items),
                out_shapes=[jax.ShapeDtypeStruct((N_DEV,) + _SHARD_SHAPES[name], F32) for name, _ in items])


def _pick(n, pref):
    if n <= pref:
        return n
    for t in range(pref - pref % LANES, 0, -LANES):
        if n % t == 0:
            return t
    raise ValueError((n, pref))


def _mm(a, b, mode, out_dtype, name, tm=512, tn=512, tk=1024, add=None, add_coef=1.0):
    if mode == "nn":
        (m, k), (k2, n) = a.shape, b.shape
    elif mode == "nt":
        (m, k), (n, k2) = a.shape, b.shape
    else:
        (k, m), (k2, n) = a.shape, b.shape
    assert k == k2, (a.shape, b.shape, mode)
    tm, tn, tk = _pick(m, tm), _pick(n, tn), _pick(k, tk)
    nk = k // tk
    dims = {"nn": NN, "nt": NT, "tn": TN}[mode]

    def body(*refs):
        a_ref, b_ref = refs[:2]
        add_ref = refs[2] if add is not None else None
        o_ref = refs[3] if add is not None else refs[2]

        def finish(r):
            if add_ref is not None:
                r = r + add_coef * add_ref[...]
            o_ref[...] = r.astype(out_dtype)

        if nk == 1:
            finish(_dot(a_ref[...], b_ref[...], dims))
            return
        acc_ref = refs[-1]
        kk = pl.program_id(2)

        @pl.when(kk == 0)
        def _():
            acc_ref[...] = jnp.zeros_like(acc_ref)

        acc_ref[...] += _dot(a_ref[...], b_ref[...], dims)

        @pl.when(kk == nk - 1)
        def _():
            finish(acc_ref[...])

    if mode == "tn":
        a_spec = pl.BlockSpec((tk, tm), lambda i, j, kk: (kk, i))
    else:
        a_spec = pl.BlockSpec((tm, tk), lambda i, j, kk: (i, kk))
    if mode == "nt":
        b_spec = pl.BlockSpec((tn, tk), lambda i, j, kk: (j, kk))
    else:
        b_spec = pl.BlockSpec((tk, tn), lambda i, j, kk: (kk, j))
    o_spec = pl.BlockSpec((tm, tn), lambda i, j, kk: (i, j))
    in_specs = [a_spec, b_spec] + ([o_spec] if add is not None else [])
    args = (a, b) + ((add,) if add is not None else ())
    return pl.pallas_call(
        body, name=name, grid=(m // tm, n // tn, nk), in_specs=in_specs, out_specs=o_spec,
        out_shape=jax.ShapeDtypeStruct((m, n), out_dtype), scratch_shapes=[pltpu.VMEM((tm, tn), F32)] if nk > 1 else [],
        compiler_params=_params(3),
    )(*args)


def _ln_fwd(x, add, gb, name, gp=None, pp=None, tr=512):
    t, d = x.shape
    tr = _pick(t, tr)
    with_ple = gp is not None

    def body(*refs):
        if with_ple:
            x_ref, a_ref, gp_ref, pp_ref, gb_ref, y_ref, r_ref = refs
        else:
            x_ref, a_ref, gb_ref, y_ref, r_ref = refs
        r = ALPHA * x_ref[...] + a_ref[...]
        if with_ple:
            r = r + _sigmoid(gp_ref[...]) * pp_ref[...]
        mu = jnp.mean(r, axis=1, keepdims=True)
        xc = r - mu
        var = jnp.mean(xc * xc, axis=1, keepdims=True)
        y_ref[...] = xc * lax.rsqrt(var + LN_EPS) * gb_ref[0:1, :] + gb_ref[1:2, :]
        r_ref[...] = r

    row = pl.BlockSpec((tr, d), lambda i: (i, 0))
    vec = pl.BlockSpec((2, d), lambda i: (0, 0))
    n_row = 4 if with_ple else 2
    args = (x, add) + ((gp, pp) if with_ple else ()) + (gb,)
    return pl.pallas_call(
        body, name=name, grid=(t // tr,), in_specs=[row] * n_row + [vec], out_specs=[row, row],
        out_shape=[jax.ShapeDtypeStruct((t, d), F32)] * 2, compiler_params=_params(1),
    )(*args)


def _ln_bwd(r, gb, dy, name, gp=None, pp=None, tr=512):
    t, d = r.shape
    tr = _pick(t, tr)
    with_ple = gp is not None

    def body(*refs):
        if with_ple:
            r_ref, dy_ref, gp_ref, pp_ref, gb_ref, dr_ref, dgp_ref, dpp_ref, st_ref = refs
        else:
            r_ref, dy_ref, gb_ref, dr_ref, st_ref = refs
        i = pl.program_id(0)

        @pl.when(i == 0)
        def _():
            st_ref[...] = jnp.zeros_like(st_ref)

        rv = r_ref[...]
        dy_v = dy_ref[...]
        mu = jnp.mean(rv, axis=1, keepdims=True)
        xc = rv - mu
        var = jnp.mean(xc * xc, axis=1, keepdims=True)
        rstd = lax.rsqrt(var + LN_EPS)
        xhat = xc * rstd
        dxh = dy_v * gb_ref[0:1, :]
        m1 = jnp.mean(dxh, axis=1, keepdims=True)
        m2 = jnp.mean(dxh * xhat, axis=1, keepdims=True)
        dr = rstd * (dxh - m1 - xhat * m2)
        dr_ref[...] = dr
        rid = lax.broadcasted_iota(jnp.int32, (2, d), 0)
        dg = jnp.sum(dy_v * xhat, axis=0, keepdims=True)
        db = jnp.sum(dy_v, axis=0, keepdims=True)
        st_ref[...] += jnp.where(rid == 0, dg, db)
        if with_ple:
            sg = _sigmoid(gp_ref[...])
            ppv = pp_ref[...]
            dgp_ref[...] = (dr * ppv * sg * (1.0 - sg)).astype(dgp_ref.dtype)
            dpp_ref[...] = (dr * sg).astype(dpp_ref.dtype)

    row = pl.BlockSpec((tr, d), lambda i: (i, 0))
    vec = pl.BlockSpec((2, d), lambda i: (0, 0))
    if with_ple:
        in_specs, args = [row] * 4 + [vec], (r, dy, gp, pp, gb)
        out_specs = [row, row, row, vec]
        out_shape = [jax.ShapeDtypeStruct((t, d), F32), jax.ShapeDtypeStruct((t, d), MXU_DTYPE),
                     jax.ShapeDtypeStruct((t, d), MXU_DTYPE), jax.ShapeDtypeStruct((2, d), F32)]
    else:
        in_specs, args = [row] * 2 + [vec], (r, dy, gb)
        out_specs = [row, vec]
        out_shape = [jax.ShapeDtypeStruct((t, d), F32), jax.ShapeDtypeStruct((2, d), F32)]
    return pl.pallas_call(body, name=name, grid=(t // tr,), in_specs=in_specs, out_specs=out_specs,
                          out_shape=out_shape, compiler_params=_params(1))(*args)


def _loss_grad(y, target, name, tr=512):
    t, d = y.shape
    tr = _pick(t, tr)

    def body(y_ref, t_ref, dy_ref, l_ref):
        i = pl.program_id(0)

        @pl.when(i == 0)
        def _():
            l_ref[...] = jnp.zeros_like(l_ref)

        e = y_ref[...] - t_ref[...]
        dy_ref[...] = e * (1.0 / d)
        per_tok = jnp.mean(e * e, axis=1, keepdims=True)
        l_ref[...] += 0.5 * jnp.sum(per_tok, axis=0, keepdims=True)

    row = pl.BlockSpec((tr, d), lambda i: (i, 0))
    acc = pl.BlockSpec((SUBLANES, LANES), lambda i: (0, 0))
    return pl.pallas_call(body, name=name, grid=(t // tr,), in_specs=[row, row], out_specs=[row, acc],
                          out_shape=[jax.ShapeDtypeStruct((t, d), F32), jax.ShapeDtypeStruct((SUBLANES, LANES), F32)],
                          compiler_params=_params(1))(y, target)


def _shift_down(v, k, row):
    return jnp.where(row >= k, pltpu.roll(v, k, 0), 0.0)


def _shift_up(v, k, row):
    n = v.shape[0]
    return jnp.where(row < n - k, pltpu.roll(v, n - k, 0), 0.0)


def _pool_window(lane):
    grp = lane // HEAD_DIM
    return jnp.where(grp == 0, POOL_WINDOWS[0], jnp.where(grp == 1, POOL_WINDOWS[1],
                     jnp.where(grp == 2, POOL_WINDOWS[2], POOL_WINDOWS[3])))


def _pool_select(lane, s2, s4, s8, s16):
    grp = lane // HEAD_DIM
    return jnp.where(grp == 0, s2, jnp.where(grp == 1, s4, jnp.where(grp == 2, s8, s16)))


def _pooled(u, row, lane):
    s2 = u + _shift_down(u, 1, row)
    s4 = s2 + _shift_down(s2, 2, row)
    s8 = s4 + _shift_down(s4, 4, row)
    s16 = s8 + _shift_down(s8, 8, row)
    cnt = jnp.minimum(row + 1, _pool_window(lane)).astype(F32)
    return _pool_select(lane, s2, s4, s8, s16) / cnt - u, cnt


def _pool_fwd(h, wbd, scale, nb, s, name):
    def body(u_ref, w_ref, sc_ref, o_ref):
        u = u_ref[...]
        row = lax.broadcasted_iota(jnp.int32, u.shape, 0)
        lane = lax.broadcasted_iota(jnp.int32, u.shape, 1)
        pooled, _ = _pooled(u, row, lane)
        o_ref[...] = (_dot(pooled, w_ref[...]) * sc_ref[...]).astype(o_ref.dtype)

    wb = POOL_WIDTH
    return pl.pallas_call(
        body, name=name, grid=(nb,),
        in_specs=[pl.BlockSpec((s, wb), lambda b: (b, H_POOL // wb)), pl.BlockSpec((wb, wb), lambda b: (0, 0)),
                  pl.BlockSpec((1, wb), lambda b: (0, 0))],
        out_specs=pl.BlockSpec((s, wb), lambda b: (b, 0)),
        out_shape=jax.ShapeDtypeStruct((nb * s, wb), MXU_DTYPE), compiler_params=_params(1),
    )(h, wbd, scale)


def _pool_bwd(h, dmix, wbd, scale, nb, s, name):
    wb = POOL_WIDTH

    def body(u_ref, do_ref, w_ref, sc_ref, du_ref, dw_ref, ds_ref):
        b = pl.program_id(0)

        @pl.when(b == 0)
        def _():
            dw_ref[...] = jnp.zeros_like(dw_ref)
            ds_ref[...] = jnp.zeros_like(ds_ref)

        u = u_ref[...]
        row = lax.broadcasted_iota(jnp.int32, u.shape, 0)
        lane = lax.broadcasted_iota(jnp.int32, u.shape, 1)
        pooled, cnt = _pooled(u, row, lane)
        mixed = _dot(pooled, w_ref[...])
        do = do_ref[...]
        ds_ref[...] += jnp.sum(do * mixed, axis=0, keepdims=True)
        dm = do * sc_ref[...]
        dw_ref[...] += _dot(pooled, dm, TN)
        dpool = _dot(dm, w_ref[...], NT)
        qv = dpool / cnt
        f2 = qv + _shift_up(qv, 1, row)
        f4 = f2 + _shift_up(f2, 2, row)
        f8 = f4 + _shift_up(f4, 4, row)
        f16 = f8 + _shift_up(f8, 8, row)
        du_ref[...] = (_pool_select(lane, f2, f4, f8, f16) - dpool).astype(du_ref.dtype)

    return pl.pallas_call(
        body, name=name, grid=(nb,),
        in_specs=[pl.BlockSpec((s, wb), lambda b: (b, H_POOL // wb)), pl.BlockSpec((s, wb), lambda b: (b, 3)),
                  pl.BlockSpec((wb, wb), lambda b: (0, 0)), pl.BlockSpec((1, wb), lambda b: (0, 0))],
        out_specs=[pl.BlockSpec((s, wb), lambda b: (b, 0)), pl.BlockSpec((wb, wb), lambda b: (0, 0)),
                   pl.BlockSpec((1, wb), lambda b: (0, 0))],
        out_shape=[jax.ShapeDtypeStruct((nb * s, wb), MXU_DTYPE), jax.ShapeDtypeStruct((wb, wb), F32),
                   jax.ShapeDtypeStruct((1, wb), F32)],
        compiler_params=_params(1),
    )(h, dmix, wbd, scale)


def _glu_conv(x, w_ref, b_ref, row):
    return (b_ref[...] + w_ref[2:3, :] * x + w_ref[1:2, :] * _shift_down(x, 1, row)
            + w_ref[0:1, :] * _shift_down(x, 2, row))


def _glu_fwd(up, cw, cb, nb, s, name):
    wt = 2 * GLU_TILE
    nt = up.shape[1] // wt

    def body(u_ref, w_ref, b_ref, o_ref):
        x = u_ref[...]
        row = lax.broadcasted_iota(jnp.int32, x.shape, 0)
        c = _glu_conv(x, w_ref, b_ref, row)
        o_ref[...] = (_silu(c[:, :GLU_TILE]) * c[:, GLU_TILE:]).astype(o_ref.dtype)

    return pl.pallas_call(
        body, name=name, grid=(nt, nb),
        in_specs=[pl.BlockSpec((s, wt), lambda j, b: (b, j)), pl.BlockSpec((3, wt), lambda j, b: (0, j)),
                  pl.BlockSpec((1, wt), lambda j, b: (0, j))],
        out_specs=pl.BlockSpec((s, GLU_TILE), lambda j, b: (b, j)),
        out_shape=jax.ShapeDtypeStruct((nb * s, nt * GLU_TILE), MXU_DTYPE), compiler_params=_params(2),
    )(up, cw, cb)


def _glu_bwd(up, dact, cw, cb, nb, s, name):
    wt = 2 * GLU_TILE
    nt = up.shape[1] // wt

    def body(u_ref, da_ref, w_ref, b_ref, du_ref, acc_ref):
        b = pl.program_id(1)

        @pl.when(b == 0)
        def _():
            acc_ref[...] = jnp.zeros_like(acc_ref)

        x = u_ref[...]
        row = lax.broadcasted_iota(jnp.int32, x.shape, 0)
        x1 = _shift_down(x, 1, row)
        x2 = _shift_down(x, 2, row)
        c = b_ref[...] + w_ref[2:3, :] * x + w_ref[1:2, :] * x1 + w_ref[0:1, :] * x2
        gate, val = c[:, :GLU_TILE], c[:, GLU_TILE:]
        da = da_ref[...]
        dc = jnp.concatenate([da * val * _dsilu(gate), da * _silu(gate)], axis=1)
        dx = (w_ref[2:3, :] * dc + w_ref[1:2, :] * _shift_up(dc, 1, row) + w_ref[0:1, :] * _shift_up(dc, 2, row))
        du_ref[...] = dx.astype(du_ref.dtype)
        rid = lax.broadcasted_iota(jnp.int32, (SUBLANES, wt), 0)
        dw0 = jnp.sum(dc * x2, axis=0, keepdims=True)
        dw1 = jnp.sum(dc * x1, axis=0, keepdims=True)
        dw2 = jnp.sum(dc * x, axis=0, keepdims=True)
        db = jnp.sum(dc, axis=0, keepdims=True)
        acc_ref[...] += (jnp.where(rid == 0, dw0, 0.0) + jnp.where(rid == 1, dw1, 0.0)
                         + jnp.where(rid == 2, dw2, 0.0) + jnp.where(rid == 3, db, 0.0))

    return pl.pallas_call(
        body, name=name, grid=(nt, nb),
        in_specs=[pl.BlockSpec((s, wt), lambda j, b: (b, j)), pl.BlockSpec((s, GLU_TILE), lambda j, b: (b, j)),
                  pl.BlockSpec((3, wt), lambda j, b: (0, j)), pl.BlockSpec((1, wt), lambda j, b: (0, j))],
        out_specs=[pl.BlockSpec((s, wt), lambda j, b: (b, j)), pl.BlockSpec((SUBLANES, wt), lambda j, b: (0, j))],
        out_shape=[jax.ShapeDtypeStruct((nb * s, nt * wt), MXU_DTYPE), jax.ShapeDtypeStruct((SUBLANES, nt * wt), F32)],
        compiler_params=_params(2),
    )(up, dact, cw, cb)


def _sb_masks():
    row = lax.broadcasted_iota(jnp.int32, (QB, QB), 0)
    col = lax.broadcasted_iota(jnp.int32, (QB, QB), 1)
    return row, col


def _sb_fwd(h, nb, s, name, comm=None):
    nq = s // QB
    scale = HEAD_DIM ** -0.5
    n_in, n_out = _comm_counts(comm)

    def body(q_ref, k_ref, v_ref, *rest):
        o_ref = rest[n_in]
        i = pl.program_id(2)
        _comm_hosted(comm, rest[:n_in], rest[n_in + 1:], (nb, 3, nq))
        row, col = _sb_masks()
        low = col < row
        later = (row > col).astype(BF16)
        sls = [slice(hd * HEAD_DIM, (hd + 1) * HEAD_DIM) for hd in range(2)]
        qs = [(q_ref[:, sl] * scale).astype(MXU_DTYPE) for sl in sls]

        def scores(hd, j, diagonal=False):
            r0 = pl.multiple_of(j * QB, QB)
            z = _dot(qs[hd], k_ref[pl.ds(r0, QB), sls[hd]], NT)
            ln = -_softplus(z)
            ls = z + ln
            if diagonal:
                ln = jnp.where(low, ln, 0.0)
                ls = jnp.where(low, ls, MASKED_LOG)
            return ls, _dot_exact01(ln, later, terms=2), jnp.sum(ln, axis=1, keepdims=True)

        def output(hd, j, ls, tl, ct):
            r0 = pl.multiple_of(j * QB, QB)
            return _dot(jnp.exp(ls + tl + ct), v_ref[pl.ds(r0, QB), sls[hd]])

        def one(j, carry, diagonal=False):
            sc = [scores(hd, j, diagonal) for hd in range(2)]
            return tuple((carry[hd][0] + output(hd, j, sc[hd][0], sc[hd][1], carry[hd][1]), carry[hd][1] + sc[hd][2])
                         for hd in range(2))

        def two(t, carry):
            j = i - 1 - 2 * t
            sc = [(scores(hd, j), scores(hd, j - 1)) for hd in range(2)]
            out = []
            for hd in range(2):
                (ls1, tl1, s1), (ls2, tl2, s2) = sc[hd]
                a, c = carry[hd]
                out.append((a + output(hd, j, ls1, tl1, c) + output(hd, j - 1, ls2, tl2, c + s1), c + s1 + s2))
            return tuple(out)

        start = (jnp.zeros((QB, HEAD_DIM), F32), jnp.zeros((QB, 1), F32))
        carry = one(i, (start, start), diagonal=True)
        carry = lax.fori_loop(0, i // 2, two, carry)
        carry = lax.fori_loop(0, i % 2, lambda t, c: one(0, c), carry)
        o_ref[:, sls[0]] = carry[0][0].astype(o_ref.dtype)
        o_ref[:, sls[1]] = carry[1][0].astype(o_ref.dtype)

    qspec = lambda off: pl.BlockSpec((QB, LANES), lambda b, p, i: (b * nq + i, off // LANES + p))
    kvspec = lambda off: pl.BlockSpec((s, LANES), lambda b, p, i: (b, off // LANES + p))
    c_in, c_specs, c_shapes, c_scratch = _comm_call_args(comm)
    res = pl.pallas_call(
        body, name=name, grid=(nb, 3, nq), in_specs=[qspec(H_Q), kvspec(H_K), kvspec(H_V)] + [_ANY] * n_in,
        out_specs=[pl.BlockSpec((QB, LANES), lambda b, p, i: (b * nq + i, p))] + c_specs,
        out_shape=[jax.ShapeDtypeStruct((nb * s, SB_WIDTH), MXU_DTYPE)] + c_shapes, scratch_shapes=c_scratch,
        compiler_params=_params(3, comm is not None),
    )(h, h, h, *c_in)
    return res[0], res[1:]


def _sb_bwd(h, dmix, nb, s, name, comm=None):
    nq = s // QB
    scale = HEAD_DIM ** -0.5
    n_in, n_out = _comm_counts(comm)

    def body(q_ref, k_ref, v_ref, do_ref, *rest):
        dq_ref, dk_ref, dv_ref = rest[n_in:n_in + 3]
        p_buf, ls_buf = rest[n_in + 3 + n_out:n_in + 5 + n_out]
        i = pl.program_id(2)
        _comm_hosted(comm, rest[:n_in], rest[n_in + 3:n_in + 3 + n_out] + rest[n_in + 5 + n_out:], (nb, 3, nq))

        @pl.when(i == 0)
        def _():
            dk_ref[...] = jnp.zeros_like(dk_ref)
            dv_ref[...] = jnp.zeros_like(dv_ref)

        row, col = _sb_masks()
        low = col < row
        later = (row > col).astype(BF16)
        earlier = (row < col).astype(BF16)
        sls = [slice(hd * HEAD_DIM, (hd + 1) * HEAD_DIM) for hd in range(2)]
        q_raw = [q_ref[:, sl].astype(MXU_DTYPE) for sl in sls]
        qs = [(q_ref[:, sl] * scale).astype(MXU_DTYPE) for sl in sls]
        do = [do_ref[:, sl].astype(MXU_DTYPE) for sl in sls]

        def down_scores(hd, j, diagonal):
            r0 = pl.multiple_of(j * QB, QB)
            z = _dot(qs[hd], k_ref[pl.ds(r0, QB), sls[hd]], NT)
            ln = -_softplus(z)
            ls = z + ln
            if diagonal:
                ln = jnp.where(low, ln, 0.0)
                ls = jnp.where(low, ls, MASKED_LOG)
            da = _dot(do[hd], v_ref[pl.ds(r0, QB), sls[hd]], NT)
            return ls, _dot_exact01(ln, later, terms=2), jnp.sum(ln, axis=1, keepdims=True), da

        def down_group(blocks, carry, diagonal=False):
            sc = [[down_scores(hd, j, diagonal) for j in blocks] for hd in range(2)]
            out = []
            for hd in range(2):
                ct = carry[hd]
                for (ls, tl, sm, da), j in zip(sc[hd], blocks):
                    r0 = pl.multiple_of(j * QB, QB)
                    a = jnp.exp(ls + tl + ct)
                    p_buf[hd, j] = da * a
                    ls_buf[hd, j] = ls
                    dv_ref[pl.ds(r0, QB), sls[hd]] += _dot(a, do[hd], TN)
                    ct = ct + sm
                out.append(ct)
            return tuple(out)

        zero = jnp.zeros((QB, 1), F32)
        carry = down_group([i], (zero, zero), diagonal=True)
        carry = lax.fori_loop(0, i // 2, lambda t, c: down_group([i - 1 - 2 * t, i - 2 - 2 * t], c), carry)
        lax.fori_loop(0, i % 2, lambda t, c: down_group([0], c), carry)

        def up_group(blocks, carry, diagonal=False):
            ld = []
            for hd in range(2):
                ld.append([])
                for j in blocks:
                    pj = p_buf[hd, j]
                    ld[hd].append((pj, jnp.exp(ls_buf[hd, j]), _dot_exact01(pj, earlier), jnp.sum(pj, axis=1, keepdims=True)))
            out = []
            for hd in range(2):
                dq, cp = carry[hd]
                for (pj, sg, cm, sm), j in zip(ld[hd], blocks):
                    r0 = pl.multiple_of(j * QB, QB)
                    dz = (pj * (1.0 - sg) - (cp + cm) * sg) * scale
                    if diagonal:
                        dz = jnp.where(low, dz, 0.0)
                    dk_ref[pl.ds(r0, QB), sls[hd]] += _dot(dz, q_raw[hd], TN)
                    dq = dq + _dot(dz, k_ref[pl.ds(r0, QB), sls[hd]])
                    cp = cp + sm
                out.append((dq, cp))
            return tuple(out)

        start = (jnp.zeros((QB, HEAD_DIM), F32), zero)
        carry = lax.fori_loop(0, i % 2, lambda t, c: up_group([0], c), (start, start))
        carry = lax.fori_loop(0, i // 2, lambda t, c: up_group([i % 2 + 2 * t, i % 2 + 2 * t + 1], c), carry)
        carry = up_group([i], carry, diagonal=True)
        dq_ref[:, sls[0]] = carry[0][0]
        dq_ref[:, sls[1]] = carry[1][0]

    qspec = lambda off: pl.BlockSpec((QB, LANES), lambda b, p, i: (b * nq + i, off // LANES + p))
    kvspec = lambda off: pl.BlockSpec((s, LANES), lambda b, p, i: (b, off // LANES + p))
    blk_out = pl.BlockSpec((QB, LANES), lambda b, p, i: (b * nq + i, p))
    seq_out = pl.BlockSpec((s, LANES), lambda b, p, i: (b, p))
    shp = jax.ShapeDtypeStruct((nb * s, SB_WIDTH), F32)
    c_in, c_specs, c_shapes, c_scratch = _comm_call_args(comm)
    res = pl.pallas_call(
        body, name=name, grid=(nb, 3, nq),
        in_specs=[qspec(H_Q), kvspec(H_K), kvspec(H_V), pl.BlockSpec((QB, LANES), lambda b, p, i: (b * nq + i, 3 + p))]
        + [_ANY] * n_in,
        out_specs=[blk_out, seq_out, seq_out] + c_specs, out_shape=[shp, shp, shp] + c_shapes,
        scratch_shapes=[pltpu.VMEM((2, nq, QB, QB), F32), pltpu.VMEM((2, nq, QB, QB), F32)] + c_scratch,
        compiler_params=_params(3, comm is not None),
    )(h, h, h, dmix, *c_in)
    return res[0], res[1], res[2], res[3:]


def _ssd_conv(cur_ref, halo_ref, w_ref, b_ref, ext_ref, first):
    n = SSD_CHUNK
    cur = cur_ref[...]
    ext_ref[0:SUBLANES, :] = jnp.where(first, 0.0, halo_ref[...])
    ext_ref[SUBLANES:SUBLANES + n, :] = cur
    return (b_ref[...] + w_ref[3:4, :] * cur + w_ref[2:3, :] * ext_ref[pl.ds(SUBLANES - 1, n), :]
            + w_ref[1:2, :] * ext_ref[pl.ds(SUBLANES - 2, n), :] + w_ref[0:1, :] * ext_ref[pl.ds(SUBLANES - 3, n), :])


def _ssd_tri():
    row = lax.broadcasted_iota(jnp.int32, (SSD_CHUNK, SSD_CHUNK), 0)
    col = lax.broadcasted_iota(jnp.int32, (SSD_CHUNK, SSD_CHUNK), 1)
    return row, col


def _ssd_specs(nc, rev):
    n = SSD_CHUNK
    hb = n // SUBLANES

    def cidx(c):
        return (nc - 1 - c) if rev else c

    def blk(width, off):
        return pl.BlockSpec((n, width), lambda b, c: (b * nc + cidx(c), off // width))

    def halo(width, off):
        return pl.BlockSpec((SUBLANES, width), lambda b, c: (jnp.maximum((b * nc + cidx(c)) * hb - 1, 0), off // width))

    def full(shape):
        return pl.BlockSpec(shape, lambda b, c: (0,) * len(shape))

    return cidx, blk, halo, full


def _ssd_core_fwd(x, bc, dt, acum, acum_t, a_row, d_row, h_prev_ref, tri):
    n = SSD_CHUNK
    heads = []
    for g in range(2):
        bm = bc[:, g * SSD_STATE:(g + 1) * SSD_STATE]
        cm = bc[:, 2 * SSD_STATE + g * SSD_STATE: 2 * SSD_STATE + (g + 1) * SSD_STATE]
        gmat = _dot(cm, bm, NT)
        for r in range(3):
            hh = g * 3 + r
            ac = acum[:, hh:hh + 1]
            ar = acum_t[hh:hh + 1, :]
            dec = jnp.where(tri, jnp.exp(jnp.minimum(ac - ar, 0.0)), 0.0)
            xh = x[:, hh * HEAD_DIM:(hh + 1) * HEAD_DIM]
            dth = dt[:, hh:hh + 1]
            xdt = xh * dth
            hp = h_prev_ref[hh * HEAD_DIM:(hh + 1) * HEAD_DIM, :]
            ea = jnp.exp(ac)
            m = gmat * dec
            yo = ea * _dot(cm, hp, NT)
            al = acum[n - 1:n, hh:hh + 1]
            w = jnp.exp(al - ac)
            y = _dot(m, xdt) + yo + d_row[:, hh:hh + 1] * xh
            heads.append(dict(g=g, hh=hh, bm=bm, cm=cm, gmat=gmat, dec=dec, xh=xh, dth=dth, xdt=xdt, hp=hp, ea=ea,
                              m=m, yo=yo, al=al, w=w, y=y))
    return heads


def _ssd_prep(xs_ref, xsh_ref, bc_ref, bch_ref, dt_ref, cwx_ref, cbx_ref, cwb_ref, cbb_ref, vec_ref, xe_ref, be_ref, first):
    pre_x = _ssd_conv(xs_ref, xsh_ref, cwx_ref, cbx_ref, xe_ref, first)
    pre_bc = _ssd_conv(bc_ref, bch_ref, cwb_ref, cbb_ref, be_ref, first)
    x = _silu(pre_x)
    bc = _silu(pre_bc)
    dt_pre = dt_ref[...] + vec_ref[0:1, :]
    dt = _softplus(dt_pre)
    a_row = vec_ref[1:2, :]
    amat = dt * a_row
    row, col = _ssd_tri()
    upper = (row <= col).astype(BF16)
    lower = (col <= row).astype(BF16)
    acum = _dot_exact01(amat, lower, NN, x_left=False)
    acum_t = _dot_exact01(amat, upper, TN, x_left=True)
    return pre_x, pre_bc, x, bc, dt_pre, dt, a_row, acum, acum_t, row, col, upper


def _ssd_gate_norm(y, z, nw):
    lane = lax.broadcasted_iota(jnp.int32, y.shape, 1)
    g0 = lane < SSD_WIDTH // 2
    hg = y * _silu(z)
    sq = hg * hg
    ms0 = jnp.sum(jnp.where(g0, sq, 0.0), axis=1, keepdims=True) * (2.0 / SSD_WIDTH)
    ms1 = jnp.sum(jnp.where(g0, 0.0, sq), axis=1, keepdims=True) * (2.0 / SSD_WIDTH)
    rs = jnp.where(g0, lax.rsqrt(ms0 + RMS_EPS), lax.rsqrt(ms1 + RMS_EPS))
    return hg, rs, g0


def _ssd_fwd(h, cwx, cbx, cwb, cbb, vec, nw, nb, s, name):
    n = SSD_CHUNK
    nc = s // n
    _, blk, halo, full = _ssd_specs(nc, False)

    def body(bc_ref, bch_ref, z_ref, xs_ref, xsh_ref, dt_ref, cwx_ref, cbx_ref, cwb_ref, cbb_ref, vec_ref, nw_ref,
             o_ref, hs_ref, h_scr, xe_ref, be_ref, y_scr):
        c = pl.program_id(1)

        @pl.when(c == 0)
        def _():
            h_scr[...] = jnp.zeros_like(h_scr)

        (_, _, x, bc, _, dt, a_row, acum, acum_t, row, col, _) = _ssd_prep(
            xs_ref, xsh_ref, bc_ref, bch_ref, dt_ref, cwx_ref, cbx_ref, cwb_ref, cbb_ref, vec_ref, xe_ref, be_ref, c == 0)
        hs_ref[...] = h_scr[...]
        heads = _ssd_core_fwd(x, bc, dt, acum, acum_t, a_row, vec_ref[2:3, :], hs_ref, col <= row)
        for hd in heads:
            sl = slice(hd["hh"] * HEAD_DIM, (hd["hh"] + 1) * HEAD_DIM)
            y_scr[:, sl] = hd["y"]
            h_scr[sl, :] = jnp.exp(hd["al"]) * hd["hp"] + _dot(hd["xdt"] * hd["w"], hd["bm"], TN)
        hg, rs, _ = _ssd_gate_norm(y_scr[...], z_ref[...], nw_ref[...])
        o_ref[...] = (hg * rs * nw_ref[...]).astype(o_ref.dtype)

    t = nb * s
    return pl.pallas_call(
        body, name=name, grid=(nb, nc),
        in_specs=[blk(512, H_BC), halo(512, H_BC), blk(384, H_Z), blk(384, H_XS), halo(384, H_XS), blk(128, H_DT),
                  full((4, 384)), full((1, 384)), full((4, 512)), full((1, 512)), full((SUBLANES, LANES)), full((1, 384))],
        out_specs=[pl.BlockSpec((n, SSD_WIDTH), lambda b, c: (b * nc + c, 0)),
                   pl.BlockSpec((None, SSD_WIDTH, SSD_STATE), lambda b, c: (b * nc + c, 0, 0))],
        out_shape=[jax.ShapeDtypeStruct((t, SSD_WIDTH), MXU_DTYPE),
                   jax.ShapeDtypeStruct((nb * nc, SSD_WIDTH, SSD_STATE), F32)],
        scratch_shapes=[pltpu.VMEM((SSD_WIDTH, SSD_STATE), F32), pltpu.VMEM((n + SUBLANES, 384), F32),
                        pltpu.VMEM((n + SUBLANES, 512), F32), pltpu.VMEM((n, SSD_WIDTH), F32)],
        compiler_params=_params(2),
    )(h, h, h, h, h, h, cwx, cbx, cwb, cbb, vec, nw)


def _ssd_bwd(h, hstate, dmix, cwx, cbx, cwb, cbb, vec, nw, nb, s, name):
    n = SSD_CHUNK
    nc = s // n
    cidx, blk, halo, full = _ssd_specs(nc, True)

    def body(bc_ref, bch_ref, z_ref, xs_ref, xsh_ref, dt_ref, hs_ref, do_ref, cwx_ref, cbx_ref, cwb_ref, cbb_ref,
             vec_ref, nw_ref, dz_ref, dxs_ref, dbc_ref, ddt_ref, gx_ref, gb_ref, gv_ref, gn_ref,
             dh_scr, xe_ref, be_ref, y_scr, dx_scr, dbc_scr, dxe_ref, dbe_ref, cx_ref, cb_ref):
        b = pl.program_id(0)
        c = pl.program_id(1)
        cc = nc - 1 - c

        @pl.when(jnp.logical_and(b == 0, c == 0))
        def _():
            gx_ref[...] = jnp.zeros_like(gx_ref)
            gb_ref[...] = jnp.zeros_like(gb_ref)
            gv_ref[...] = jnp.zeros_like(gv_ref)
            gn_ref[...] = jnp.zeros_like(gn_ref)

        @pl.when(c == 0)
        def _():
            dh_scr[...] = jnp.zeros_like(dh_scr)
            cx_ref[...] = jnp.zeros_like(cx_ref)
            cb_ref[...] = jnp.zeros_like(cb_ref)

        (pre_x, pre_bc, x, bc, dt_pre, dt, a_row, acum, acum_t, row, col, upper) = _ssd_prep(
            xs_ref, xsh_ref, bc_ref, bch_ref, dt_ref, cwx_ref, cbx_ref, cwb_ref, cbb_ref, vec_ref, xe_ref, be_ref, cc == 0)
        tri = col <= row
        d_row = vec_ref[2:3, :]
        heads = _ssd_core_fwd(x, bc, dt, acum, acum_t, a_row, d_row, hs_ref, tri)
        for hd in heads:
            y_scr[:, hd["hh"] * HEAD_DIM:(hd["hh"] + 1) * HEAD_DIM] = hd["y"]
        y = y_scr[...]
        z = z_ref[...]
        nwv = nw_ref[...]
        hg, rs, g0 = _ssd_gate_norm(y, z, nwv)
        do = do_ref[...]
        nrm = hg * rs
        gn_ref[...] += jnp.sum(do * nrm, axis=0, keepdims=True)
        dn = do * nwv
        dnn = dn * nrm
        mean0 = jnp.sum(jnp.where(g0, dnn, 0.0), axis=1, keepdims=True) * (2.0 / SSD_WIDTH)
        mean1 = jnp.sum(jnp.where(g0, 0.0, dnn), axis=1, keepdims=True) * (2.0 / SSD_WIDTH)
        dhg = rs * (dn - nrm * jnp.where(g0, mean0, mean1))
        dz_ref[...] = (dhg * y * _dsilu(z)).astype(dz_ref.dtype)
        dy = dhg * _silu(z)

        lane = lax.broadcasted_iota(jnp.int32, (n, LANES), 1)
        lane1 = lax.broadcasted_iota(jnp.int32, (1, LANES), 1)
        last_row = lax.broadcasted_iota(jnp.int32, (n, 1), 0) == n - 1
        dacum_col = jnp.zeros((n, LANES), F32)
        da_rowpart = jnp.zeros((n, LANES), F32)
        ddt = jnp.zeros((n, LANES), F32)
        dd_vec = jnp.zeros((1, LANES), F32)
        for g in range(2):
            dg = jnp.zeros((n, n), F32)
            dbm = jnp.zeros((n, SSD_STATE), F32)
            dcm = jnp.zeros((n, SSD_STATE), F32)
            for hd in heads[3 * g:3 * g + 3]:
                hh = hd["hh"]
                sl = slice(hh * HEAD_DIM, (hh + 1) * HEAD_DIM)
                dyh = dy[:, sl]
                dhn = dh_scr[sl, :]
                el = jnp.exp(hd["al"])
                dd_vec = dd_vec + jnp.where(lane1 == hh, jnp.sum(dyh * hd["xh"]), 0.0)
                dcm = dcm + hd["ea"] * _dot(dyh, hd["hp"])
                dm = _dot(dyh, hd["xdt"], NT)
                dg = dg + dm * hd["dec"]
                e = dm * hd["m"]
                t2 = _dot(hd["bm"], dhn, NT)
                dxdt = _dot(hd["m"], dyh, TN) + hd["w"] * t2
                dbm = dbm + hd["w"] * _dot(hd["xdt"], dhn)
                dw_w = jnp.sum(hd["xdt"] * t2, axis=1, keepdims=True) * hd["w"]
                d_el = jnp.sum(dhn * hd["hp"])
                col_part = (jnp.sum(dyh * hd["yo"], axis=1, keepdims=True) + jnp.sum(e, axis=1, keepdims=True) - dw_w
                            + jnp.where(last_row, d_el * el + jnp.sum(dw_w), 0.0))
                dacum_col = dacum_col + jnp.where(lane == hh, col_part, 0.0)
                neg_colsum = -jnp.sum(e, axis=0, keepdims=True)
                rev = jnp.sum(jnp.where(row <= col, neg_colsum, 0.0), axis=1, keepdims=True)
                da_rowpart = da_rowpart + jnp.where(lane == hh, rev, 0.0)
                dh_scr[sl, :] = el * dhn + _dot(dyh * hd["ea"], hd["cm"], TN)
                dx_scr[:, sl] = d_row[:, hh:hh + 1] * dyh + dxdt * hd["dth"]
                ddt = ddt + jnp.where(lane == hh, jnp.sum(dxdt * hd["xh"], axis=1, keepdims=True), 0.0)
            bm, cm = heads[3 * g]["bm"], heads[3 * g]["cm"]
            dcm = dcm + _dot(dg, bm)
            dbm = dbm + _dot(dg, cm, TN)
            dbc_scr[:, g * SSD_STATE:(g + 1) * SSD_STATE] = dbm
            dbc_scr[:, 2 * SSD_STATE + g * SSD_STATE:2 * SSD_STATE + (g + 1) * SSD_STATE] = dcm
        da_mat = _dot_exact01(dacum_col, upper, NN, x_left=False) + da_rowpart
        ddt = ddt + da_mat * a_row
        da_vec = jnp.sum(da_mat * dt, axis=0, keepdims=True)
        ddt_pre = jnp.where(lane < SSD_HEADS, ddt * _sigmoid(dt_pre), 0.0)
        ddt_ref[...] = ddt_pre.astype(ddt_ref.dtype)
        rid = lax.broadcasted_iota(jnp.int32, (SUBLANES, LANES), 0)
        gv_ref[...] += (jnp.where(rid == 0, jnp.sum(ddt_pre, axis=0, keepdims=True), 0.0)
                        + jnp.where(rid == 1, da_vec, 0.0) + jnp.where(rid == 2, dd_vec, 0.0))

        def conv_bwd(dpost, pre, w_ref, ext_ref, dext_ref, carry_ref, cur_ref, out_ref, g_ref, width):
            dco = dpost * _dsilu(pre)
            dext_ref[0:n, :] = dco
            dext_ref[n:n + SUBLANES, :] = carry_ref[...]
            out_ref[...] = (w_ref[3:4, :] * dco + w_ref[2:3, :] * dext_ref[pl.ds(1, n), :]
                            + w_ref[1:2, :] * dext_ref[pl.ds(2, n), :] + w_ref[0:1, :] * dext_ref[pl.ds(3, n), :]
                            ).astype(out_ref.dtype)
            carry_ref[...] = dco[0:SUBLANES, :]
            rid8 = lax.broadcasted_iota(jnp.int32, (SUBLANES, width), 0)
            acc = jnp.where(rid8 == 3, jnp.sum(dco * cur_ref[...], axis=0, keepdims=True), 0.0)
            for j in range(3):
                sh = ext_ref[pl.ds(SUBLANES - 3 + j, n), :]
                acc = acc + jnp.where(rid8 == j, jnp.sum(dco * sh, axis=0, keepdims=True), 0.0)
            acc = acc + jnp.where(rid8 == 4, jnp.sum(dco, axis=0, keepdims=True), 0.0)
            g_ref[...] += acc

        conv_bwd(dx_scr[...], pre_x, cwx_ref, xe_ref, dxe_ref, cx_ref, xs_ref, dxs_ref, gx_ref, 384)
        conv_bwd(dbc_scr[...], pre_bc, cwb_ref, be_ref, dbe_ref, cb_ref, bc_ref, dbc_ref, gb_ref, 512)

    t = nb * s
    rowblk = lambda width: pl.BlockSpec((n, width), lambda b, c: (b * nc + cidx(c), 0))
    return pl.pallas_call(
        body, name=name, grid=(nb, nc),
        in_specs=[blk(512, H_BC), halo(512, H_BC), blk(384, H_Z), blk(384, H_XS), halo(384, H_XS), blk(128, H_DT),
                  pl.BlockSpec((None, SSD_WIDTH, SSD_STATE), lambda b, c: (b * nc + cidx(c), 0, 0)),
                  pl.BlockSpec((n, SSD_WIDTH), lambda b, c: (b * nc + cidx(c), 0)),
                  full((4, 384)), full((1, 384)), full((4, 512)), full((1, 512)), full((SUBLANES, LANES)), full((1, 384))],
        out_specs=[rowblk(384), rowblk(384), rowblk(512), rowblk(128),
                   full((SUBLANES, 384)), full((SUBLANES, 512)), full((SUBLANES, LANES)), full((1, 384))],
        out_shape=[jax.ShapeDtypeStruct((t, 384), MXU_DTYPE), jax.ShapeDtypeStruct((t, 384), MXU_DTYPE),
                   jax.ShapeDtypeStruct((t, 512), MXU_DTYPE), jax.ShapeDtypeStruct((t, 128), MXU_DTYPE),
                   jax.ShapeDtypeStruct((SUBLANES, 384), F32), jax.ShapeDtypeStruct((SUBLANES, 512), F32),
                   jax.ShapeDtypeStruct((SUBLANES, LANES), F32), jax.ShapeDtypeStruct((1, 384), F32)],
        scratch_shapes=[pltpu.VMEM((SSD_WIDTH, SSD_STATE), F32), pltpu.VMEM((n + SUBLANES, 384), F32),
                        pltpu.VMEM((n + SUBLANES, 512), F32), pltpu.VMEM((n, SSD_WIDTH), F32),
                        pltpu.VMEM((n, 384), F32), pltpu.VMEM((n, 512), F32),
                        pltpu.VMEM((n + SUBLANES, 384), F32), pltpu.VMEM((n + SUBLANES, 512), F32),
                        pltpu.VMEM((SUBLANES, 384), F32), pltpu.VMEM((SUBLANES, 512), F32)],
        compiler_params=_params(2),
    )(h, h, h, h, h, h, hstate, dmix, cwx, cbx, cwb, cbb, vec, nw)


def _adamw_math(w, g, m, v):
    m = ADAM_B1 * m + (1.0 - ADAM_B1) * g
    v = ADAM_B2 * v + (1.0 - ADAM_B2) * (g * g)
    m_hat = m / (1.0 - ADAM_B1 ** ADAM_STEP)
    v_hat = v / (1.0 - ADAM_B2 ** ADAM_STEP)
    delta = -ADAM_LR * (m_hat / (jnp.sqrt(v_hat) + ADAM_EPS) + ADAM_WD * w)
    return delta, m, v


def _adamw(w, g, m, v, name, tr=256):
    rows, cols = w.shape
    tr = rows if rows <= tr else tr
    assert rows % tr == 0, (rows, tr)

    def body(w_ref, g_ref, m_ref, v_ref, d_ref, nm_ref, nv_ref):
        d, nm, nv = _adamw_math(w_ref[...], g_ref[...], m_ref[...], v_ref[...])
        d_ref[...] = d
        nm_ref[...] = nm
        nv_ref[...] = nv

    spec = pl.BlockSpec((tr, cols), lambda i: (i, 0))
    shp = jax.ShapeDtypeStruct((rows, cols), F32)
    return pl.pallas_call(body, name=name, grid=(rows // tr,), in_specs=[spec] * 4, out_specs=[spec] * 3,
                          out_shape=[shp] * 3, compiler_params=_params(1))(w, g, m, v)


def _sum8_layers(parts, name, tr):
    _, rows, cols = parts[0].shape
    assert rows % tr == 0
    nt = rows // tr

    def body(*refs):
        o_ref = refs[DEPTH]
        layer = pl.program_id(0)
        for l in range(DEPTH):
            @pl.when(layer == l)
            def _(l=l):
                acc = refs[l][0]
                for k in range(1, N_DEV):
                    acc = acc + refs[l][k]
                o_ref[...] = acc

    in_specs = [pl.BlockSpec((N_DEV, tr, cols), lambda a, i, l=l: (0, jnp.clip(i + (a - l) * nt, 0, nt - 1), 0))
                for l in range(DEPTH)]
    return pl.pallas_call(body, name=name, grid=(DEPTH, nt), in_specs=in_specs,
                          out_specs=pl.BlockSpec((None, tr, cols), lambda a, i: (a, i, 0)),
                          out_shape=jax.ShapeDtypeStruct((DEPTH, rows, cols), F32), compiler_params=_params(2))(*parts)


def _all_reduce_small(vec, name):
    rows, cols = vec.shape

    def body(x_ref, out_ref, gbuf, send_sems, recv_sems):
        x, y, c = lax.axis_index("x"), lax.axis_index("y"), lax.axis_index("c")
        me, sibling = (x, y, c), (x, y, 1 - c)
        chips = [(1 - x, y), (x, 1 - y), (1 - x, 1 - y)]

        def slot(px, py, pc):
            return gbuf.at[4 * px + 2 * py + pc]

        def copy(k, block, to, src=None):
            return pltpu.make_async_remote_copy(
                src_ref=slot(*block) if src is None else src, dst_ref=slot(*block),
                send_sem=send_sems.at[k], recv_sem=recv_sems.at[k], device_id=to, device_id_type=MESH_ID)

        first = [copy(0, me, sibling, src=x_ref)]
        first += [copy(1 + j, me, (*chip, c), src=x_ref) for j, chip in enumerate(chips)]
        for cp in first:
            cp.start()
        gbuf[4 * x + 2 * y + c] = x_ref[...]
        passed = [copy(4 + j, (*chip, c), sibling) for j, chip in enumerate(chips)]
        for j, chip in enumerate(chips):
            copy(1 + j, (*chip, c), me).wait_recv()
            passed[j].start()
        copy(0, sibling, me).wait_recv()
        for j, chip in enumerate(chips):
            copy(4 + j, (*chip, 1 - c), me).wait_recv()
        for cp in first + passed:
            cp.wait_send()
        acc = gbuf[0]
        for k in range(1, N_DEV):
            acc = acc + gbuf[k]
        out_ref[...] = acc

    return pl.pallas_call(
        body, name=name, out_shape=jax.ShapeDtypeStruct((rows, cols), F32),
        in_specs=[pl.BlockSpec(memory_space=pltpu.VMEM)], out_specs=pl.BlockSpec(memory_space=pltpu.VMEM),
        scratch_shapes=[pltpu.VMEM((N_DEV, rows, cols), F32), pltpu.SemaphoreType.DMA((7,)), pltpu.SemaphoreType.DMA((7,))],
        compiler_params=pltpu.CompilerParams(has_side_effects=True, vmem_limit_bytes=VMEM_LIMIT_BYTES),
    )(vec)


_COL_POOL, _COL_Z, _COL_XBC, _COL_DT, _COL_Q, _COL_K, _COL_V = 0, 256, 640, 1536, 1542, 1926, 2310
_H_SEGMENTS = ((_COL_XBC + SSD_WIDTH, 512), (_COL_POOL, 256), (_COL_Q, 384), (_COL_K, 384), (_COL_V, 384),
               (_COL_Z, 384), (_COL_XBC, 384), (_COL_DT, 6))


def _h_from_orig(w):
    parts = [w[..., o:o + n] for o, n in _H_SEGMENTS]
    pad = jnp.zeros(w.shape[:-1] + (H_COLS - IN_COLS,), w.dtype)
    return jnp.concatenate(parts + [pad], axis=-1)


def _h_to_orig(w):
    offs, o = {}, 0
    for orig, n in _H_SEGMENTS:
        offs[orig] = (o, n)
        o += n
    order = sorted(offs)
    return jnp.concatenate([w[..., offs[k][0]:offs[k][0] + offs[k][1]] for k in order], axis=-1)


def _interleave(w):
    lead = w.shape[:-1]
    nt = D_FF // GLU_TILE
    return jnp.swapaxes(w.reshape(lead + (2, nt, GLU_TILE)), -3, -2).reshape(lead + (2 * D_FF,))


def _deinterleave(w):
    lead = w.shape[:-1]
    nt = D_FF // GLU_TILE
    return jnp.swapaxes(w.reshape(lead + (nt, 2, GLU_TILE)), -3, -2).reshape(lead + (2 * D_FF,))


def _mix_rows_from_orig(w):
    return jnp.concatenate([w[256:640], w[640:1024], w[0:256]], axis=0)


def _mix_rows_to_orig(w):
    return jnp.concatenate([w[768:1024], w[0:384], w[384:768]], axis=0)


def _xbc_split(w):
    return w[..., :SSD_WIDTH], w[..., SSD_WIDTH:]


def _layer_fwd(x, p_l, wt, sp, nb, s, comm=None):
    h = _mm(x, wt["w_in"], "nn", F32, "mm_in", tm=1024, tn=1408)
    pool_out = _pool_fwd(h, wt["pool_bd"], sp["pool_scale"], nb, s, "pool_fwd")
    ssd_out, hstate = _ssd_fwd(h, sp["cwx"], sp["cbx"], sp["cwb"], sp["cbb"], sp["ssd_vec"], sp["ssd_norm_w"], nb, s, "ssd_fwd")
    sb_out, comm_out = _sb_fwd(h, nb, s, "sb_fwd" if comm is None else "sb_fwd_gather", comm)
    mixcat = jnp.concatenate([ssd_out, sb_out, pool_out], axis=1)
    mix = _mm(mixcat, wt["w_out"], "nn", F32, "mm_out", tm=1024, tn=1024)
    x1, r1 = _ln_fwd(x, mix, sp["ln1"], "ln1_fwd")
    up = _mm(x1, wt["w_up"], "nn", F32, "mm_up", tm=1024, tn=1408)
    act = _glu_fwd(up, sp["ffn_cw"], sp["ffn_cb"], nb, s, "glu_fwd")
    ffn = _mm(act, wt["w_down"], "nn", F32, "mm_down", tm=1024, tn=1024, tk=1408)
    gp = _mm(x1, wt["w_gate"], "nn", F32, "mm_gate", tm=1024, tn=1024)
    pp = _mm(p_l, wt["w_proj"], "nn", F32, "mm_proj", tm=2048, tn=1024)
    x2, r2 = _ln_fwd(x1, ffn, sp["ln2"], "ln2_fwd", gp=gp, pp=pp)
    return x2, dict(x=x, h=h, hstate=hstate, mixcat=mixcat, r1=r1, x1=x1, up=up, act=act, gp=gp, pp=pp, r2=r2), comm_out


def _layer_bwd(dx2, p_l, sv, wt, sp, nb, s, comm=None):
    dr2, dgp, dpp, st2 = _ln_bwd(sv["r2"], sp["ln2"], dx2, "ln2_bwd", gp=sv["gp"], pp=sv["pp"])
    g_down = _mm(sv["act"], dr2, "tn", F32, "wg_down", tm=1408, tn=1024, tk=512)
    dact = _mm(dr2, wt["w_down"], "nt", F32, "dg_down", tm=1024, tn=1408)
    dup, ffn_acc = _glu_bwd(sv["up"], dact, sp["ffn_cw"], sp["ffn_cb"], nb, s, "glu_bwd")
    g_up = _mm(sv["x1"], dup, "tn", F32, "wg_up", tm=1024, tn=2816, tk=512)
    g_gate = _mm(sv["x1"], dgp, "tn", F32, "wg_gate", tm=1024, tn=1024, tk=512)
    g_proj = _mm(p_l, dpp, "tn", F32, "wg_proj", tm=256, tn=1024, tk=512)
    t1 = _mm(dgp, wt["w_gate"], "nt", F32, "dg_gate", tm=1024, tn=1024, add=dr2, add_coef=ALPHA)
    dx1 = _mm(dup, wt["w_up"], "nt", F32, "dg_up", tm=1024, tn=1024, tk=1408, add=t1)
    dr1, st1 = _ln_bwd(sv["r1"], sp["ln1"], dx1, "ln1_bwd")
    g_out = _mm(sv["mixcat"], dr1, "tn", F32, "wg_out", tm=1024, tn=1024, tk=512)
    dmix = _mm(dr1, wt["w_out"], "nt", F32, "dg_out", tm=1024, tn=1024)
    du, g_pool_bd, g_pool_scale = _pool_bwd(sv["h"], dmix, wt["pool_bd"], sp["pool_scale"], nb, s, "pool_bwd")
    dz, dxs, dbc, ddt, gx, gb, gv, gn = _ssd_bwd(sv["h"], sv["hstate"], dmix, sp["cwx"], sp["cbx"], sp["cwb"], sp["cbb"],
                                                  sp["ssd_vec"], sp["ssd_norm_w"], nb, s, "ssd_bwd")
    ready = dict(w_out=g_out, ffn_w_up=g_up, ffn_w_down=g_down, ple_w_gate=g_gate, ple_w_proj=g_proj)
    job = comm(ready) if comm is not None else None
    dq, dk, dv, comm_out = _sb_bwd(sv["h"], dmix, nb, s, "sb_bwd" if job is None else "sb_bwd_x%d" % job["n_xfers"], job)
    dh = jnp.concatenate([dbc, du, dq.astype(MXU_DTYPE), dk.astype(MXU_DTYPE), dv.astype(MXU_DTYPE), dz, dxs, ddt], axis=1)
    g_in = _mm(sv["x"], dh, "tn", F32, "wg_in", tm=1024, tn=2816, tk=512)
    dx = _mm(dh, wt["w_in"], "nt", F32, "dg_in", tm=1024, tn=1024, tk=1408, add=dr1, add_coef=ALPHA)
    small = dict(
        pool_w=jnp.stack([g_pool_bd[HEAD_DIM * g:HEAD_DIM * (g + 1), HEAD_DIM * g:HEAD_DIM * (g + 1)] for g in range(4)]),
        pool_scale=g_pool_scale[0],
        ssd_conv_w=jnp.concatenate([gx[0:4], gb[0:4]], axis=1),
        ssd_conv_b=jnp.concatenate([gx[4], gb[4]], axis=0),
        ssd_dt_bias=gv[0, :SSD_HEADS],
        ssd_a_log=gv[1, :SSD_HEADS] * sp["ssd_vec"][1, :SSD_HEADS],
        ssd_d=gv[2, :SSD_HEADS],
        ssd_norm_w=gn[0],
        ln1_g=st1[0], ln1_b=st1[1], ln2_g=st2[0], ln2_b=st2[1],
        ffn_conv_w=_deinterleave(ffn_acc[0:3]),
        ffn_conv_b=_deinterleave(ffn_acc[3]),
    )
    return dx, dict(ready, w_in=g_in), small, comm_out


def _layer_params(i, big, rep):
    pool_bd = jnp.zeros((POOL_WIDTH, POOL_WIDTH), F32)
    for g in range(4):
        pool_bd = lax.dynamic_update_slice(pool_bd, rep["pool_w"][i, g], (HEAD_DIM * g, HEAD_DIM * g))
    wt = dict(w_in=big["w_in"], w_out=big["w_out"], w_up=big["ffn_w_up"], w_down=big["ffn_w_down"],
              w_gate=big["ple_w_gate"], w_proj=big["ple_w_proj"], pool_bd=pool_bd.astype(MXU_DTYPE))
    cwx, cwb = _xbc_split(rep["ssd_conv_w"][i])
    cbx, cbb = _xbc_split(rep["ssd_conv_b"][i][None, :])
    vec = jnp.zeros((SUBLANES, LANES), F32)
    vec = vec.at[0, :SSD_HEADS].set(rep["ssd_dt_bias"][i])
    vec = vec.at[1, :SSD_HEADS].set(-jnp.exp(rep["ssd_a_log"][i]))
    vec = vec.at[2, :SSD_HEADS].set(rep["ssd_d"][i])
    sp = dict(pool_scale=rep["pool_scale"][i][None, :], cwx=cwx, cbx=cbx, cwb=cwb, cbb=cbb, ssd_vec=vec,
              ssd_norm_w=rep["ssd_norm_w"][i][None, :],
              ln1=jnp.stack([rep["ln1_g"][i], rep["ln1_b"][i]]), ln2=jnp.stack([rep["ln2_g"][i], rep["ln2_b"][i]]),
              ffn_cw=_interleave(rep["ffn_conv_w"][i]), ffn_cb=_interleave(rep["ffn_conv_b"][i][None, :]))
    return wt, sp


def _run_layers(x, p, target, big_w, rep, fwd_job=None, fwd_done=None, bwd_job=None, bwd_done=None):
    nb, s, d = x.shape
    t = nb * s
    xf = x.reshape(t, d)
    saved, params = [], []
    for i in range(DEPTH):
        wt, sp = _layer_params(i, big_w[i], rep)
        params.append((wt, sp))
        job = fwd_job(i) if fwd_job is not None else None
        xf, sv, res = _layer_fwd(xf, p[i].reshape(t, PLE_DIM), wt, sp, nb, s, job)
        if job is not None:
            fwd_done(i, res)
        saved.append(sv)
    dy, loss = _loss_grad(xf, target.reshape(t, d), "loss")
    bigs, smalls = [None] * DEPTH, [None] * DEPTH
    for i in reversed(range(DEPTH)):
        wt, sp = params[i]
        job = (lambda ready, i=i: bwd_job(i, bigs, ready)) if bwd_job is not None else None
        dy, bigs[i], smalls[i], res = _layer_bwd(dy, p[i].reshape(t, PLE_DIM), saved[i], wt, sp, nb, s, job)
        if job is not None:
            bwd_done(i, res)
    return loss, dy.reshape(nb, s, d), bigs, smalls


def _local_step(x, p, target, full, rep):
    return _run_layers(x, p, target, [{n: full[n][i] for n in full} for i in range(DEPTH)], rep)


BIG = ("w_in", "w_out", "ffn_w_up", "ffn_w_down", "ple_w_gate", "ple_w_proj")
SMALL_REPLICATED = ("pool_w", "pool_scale", "ssd_conv_b", "ssd_dt_bias", "ssd_a_log", "ssd_d", "ssd_norm_w",
                    "ln1_g", "ln1_b", "ffn_conv_b", "ln2_g", "ln2_b")
SMALL_SHARDED = ("ssd_conv_w", "ffn_conv_w")
WEIGHTS = ("w_in", "pool_w", "pool_scale", "ssd_conv_w", "ssd_conv_b", "ssd_dt_bias", "ssd_a_log", "ssd_d", "ssd_norm_w",
           "w_out", "ln1_g", "ln1_b", "ffn_w_up", "ffn_conv_w", "ffn_conv_b", "ffn_w_down", "ln2_g", "ln2_b",
           "ple_w_gate", "ple_w_proj")
SUM_BLOCK_BYTES = 3 * 512 * 1024


def _to_rows(a, cols):
    f = a.reshape(-1)
    pad = (-f.shape[0]) % cols
    if pad:
        f = jnp.concatenate([f, jnp.zeros((pad,), f.dtype)])
    return f.reshape(-1, cols)


def _pack_rows(arrs, cols, row_mult):
    rows = [_to_rows(a, cols) for a in arrs]
    flat = jnp.concatenate(rows, axis=0)
    pad = (-flat.shape[0]) % row_mult
    if pad:
        flat = jnp.concatenate([flat, jnp.zeros((pad, cols), flat.dtype)], axis=0)
    return flat


def _unpack_rows(flat, shapes, cols):
    out, r = [], 0
    for shp in shapes:
        n = 1
        for v in shp:
            n *= v
        nr = -(-n // cols)
        out.append(flat[r:r + nr].reshape(-1)[:n].reshape(shp))
        r += nr
    return out


def kernel(x, p, w_in, pool_w, pool_scale, ssd_conv_w, ssd_conv_b, ssd_dt_bias, ssd_a_log, ssd_d, ssd_norm_w, w_out, ln1_g, ln1_b, ffn_w_up, ffn_conv_w, ffn_conv_b, ffn_w_down, ln2_g, ln2_b, ple_w_gate, ple_w_proj, loss_target, m_w_in, m_pool_w, m_pool_scale, m_ssd_conv_w, m_ssd_conv_b, m_ssd_dt_bias, m_ssd_a_log, m_ssd_d, m_ssd_norm_w, m_w_out, m_ln1_g, m_ln1_b, m_ffn_w_up, m_ffn_conv_w, m_ffn_conv_b, m_ffn_w_down, m_ln2_g, m_ln2_b, m_ple_w_gate, m_ple_w_proj, v_w_in, v_pool_w, v_pool_scale, v_ssd_conv_w, v_ssd_conv_b, v_ssd_dt_bias, v_ssd_a_log, v_ssd_d, v_ssd_norm_w, v_w_out, v_ln1_g, v_ln1_b, v_ffn_w_up, v_ffn_conv_w, v_ffn_conv_b, v_ffn_w_down, v_ln2_g, v_ln2_b, v_ple_w_gate, v_ple_w_proj):
    wts = dict(w_in=w_in, pool_w=pool_w, pool_scale=pool_scale, ssd_conv_w=ssd_conv_w, ssd_conv_b=ssd_conv_b,
               ssd_dt_bias=ssd_dt_bias, ssd_a_log=ssd_a_log, ssd_d=ssd_d, ssd_norm_w=ssd_norm_w, w_out=w_out, ln1_g=ln1_g,
               ln1_b=ln1_b, ffn_w_up=ffn_w_up, ffn_conv_w=ffn_conv_w, ffn_conv_b=ffn_conv_b, ffn_w_down=ffn_w_down,
               ln2_g=ln2_g, ln2_b=ln2_b, ple_w_gate=ple_w_gate, ple_w_proj=ple_w_proj)
    mom_m = dict(w_in=m_w_in, pool_w=m_pool_w, pool_scale=m_pool_scale, ssd_conv_w=m_ssd_conv_w, ssd_conv_b=m_ssd_conv_b,
                 ssd_dt_bias=m_ssd_dt_bias, ssd_a_log=m_ssd_a_log, ssd_d=m_ssd_d, ssd_norm_w=m_ssd_norm_w, w_out=m_w_out,
                 ln1_g=m_ln1_g, ln1_b=m_ln1_b, ffn_w_up=m_ffn_w_up, ffn_conv_w=m_ffn_conv_w, ffn_conv_b=m_ffn_conv_b,
                 ffn_w_down=m_ffn_w_down, ln2_g=m_ln2_g, ln2_b=m_ln2_b, ple_w_gate=m_ple_w_gate, ple_w_proj=m_ple_w_proj)
    mom_v = dict(w_in=v_w_in, pool_w=v_pool_w, pool_scale=v_pool_scale, ssd_conv_w=v_ssd_conv_w, ssd_conv_b=v_ssd_conv_b,
                 ssd_dt_bias=v_ssd_dt_bias, ssd_a_log=v_ssd_a_log, ssd_d=v_ssd_d, ssd_norm_w=v_ssd_norm_w, w_out=v_w_out,
                 ln1_g=v_ln1_g, ln1_b=v_ln1_b, ffn_w_up=v_ffn_w_up, ffn_conv_w=v_ffn_conv_w, ffn_conv_b=v_ffn_conv_b,
                 ffn_w_down=v_ffn_w_down, ln2_g=v_ln2_g, ln2_b=v_ln2_b, ple_w_gate=v_ple_w_gate, ple_w_proj=v_ple_w_proj)
    me = 4 * lax.axis_index("x") + 2 * lax.axis_index("y") + lax.axis_index("c")

    def layer_shards(i):
        sh = {n: wts[n][i].astype(MXU_DTYPE) for n in BIG}
        sh["w_in"] = _h_from_orig(wts["w_in"][i]).astype(MXU_DTYPE)
        return sh

    def gathered_weights(res):
        big = dict(zip(BIG, res[:len(BIG)]))
        big["ffn_w_up"] = _interleave(jnp.swapaxes(big["ffn_w_up"], 0, 1).reshape(D_MODEL, 2 * D_FF))
        return big

    res0 = _comm_call(_gather_job(layer_shards(0), [wts[n] for n in SMALL_SHARDED]), "gather_layer0")
    big_w = [gathered_weights(res0)] + [None] * (DEPTH - 1)
    rep = {n: wts[n] for n in SMALL_REPLICATED}
    for n, g in zip(SMALL_SHARDED, res0[len(BIG):]):
        rep[n] = jnp.transpose(g, (1, 2, 0, 3)).reshape(g.shape[1], g.shape[2], N_DEV * g.shape[3])

    def fwd_job(i):
        return _gather_job(layer_shards(i + 1)) if i + 1 < DEPTH else None

    def fwd_done(i, res):
        big_w[i + 1] = gathered_weights(res)

    received = [dict() for _ in range(DEPTH)]
    carried = ("w_out", "ffn_w_up", "ffn_w_down", "ple_w_gate", "ple_w_proj")

    def bwd_items(i, bigs, ready):
        items = [(i, n, ready[n]) for n in carried] + ([(i + 1, "w_in", bigs[i + 1]["w_in"])] if i + 1 < DEPTH else [])
        return [(l, n, jnp.swapaxes(_deinterleave(g).reshape(D_MODEL, N_DEV, 704), 0, 1) if n == "ffn_w_up" else g)
                for l, n, g in items]

    pending = {}

    def bwd_job(i, bigs, ready):
        pending[i] = bwd_items(i, bigs, ready)
        return _exchange_job([(n, g) for _, n, g in pending[i]])

    def bwd_done(i, res):
        for (l, n, _), r in zip(pending[i], res):
            received[l][n] = r

    loss_loc, grad_x, bigs, smalls = _run_layers(x, p, loss_target, big_w, rep, fwd_job, fwd_done, bwd_job, bwd_done)
    received[0]["w_in"] = _comm_call(_exchange_job([("w_in", bigs[0]["w_in"])]), "exchange_w_in0")[0]

    grads = {}
    for n in BIG:
        parts = [received[i][n] for i in range(DEPTH)]
        _, rows, cols = parts[0].shape
        tr = next(t for t in (256, 128, 64, 32, 16, 8) if rows % t == 0 and N_DEV * t * cols * 4 <= SUM_BLOCK_BYTES)
        g = _sum8_layers(parts, "sum_" + n, tr)
        grads[n] = _h_to_orig(g) if n == "w_in" else g
    small_names = SMALL_REPLICATED + SMALL_SHARDED
    small_full_shapes = [rep[n].shape for n in small_names]
    small_vec = _pack_rows([jnp.stack([smalls[i][n] for i in range(DEPTH)]) for n in small_names] + [loss_loc[0, :1]],
                           LANES, SUBLANES)
    small_sum = _all_reduce_small(small_vec, "allreduce_small")
    small_out = _unpack_rows(small_sum, small_full_shapes + [(1,)], LANES)
    loss = small_out[-1][0]
    for n, g in zip(small_names, small_out[:-1]):
        if n in SMALL_SHARDED:
            width = wts[n].shape[-1]
            g = lax.dynamic_slice_in_dim(g, me * width, width, axis=g.ndim - 1)
        grads[n] = g

    delta, new_m, new_v = {}, {}, {}
    for n in BIG:
        shp = wts[n].shape
        two_d = lambda a: a.reshape(-1, shp[-1])
        tr = {"w_in": 128, "ffn_w_down": 352}.get(n, 256)
        d_, m_, v_ = _adamw(two_d(wts[n]), two_d(grads[n]), two_d(mom_m[n]), two_d(mom_v[n]), "adamw_" + n, tr=tr)
        delta[n], new_m[n], new_v[n] = d_.reshape(shp), m_.reshape(shp), v_.reshape(shp)
    packs = [_pack_rows([src[n] for n in small_names], LANES, SUBLANES) for src in (wts, grads, mom_m, mom_v)]
    outs = _adamw(*packs, "adamw_small", tr=packs[0].shape[0])
    shapes = [wts[n].shape for n in small_names]
    for dst, flat in zip((delta, new_m, new_v), outs):
        for n, a in zip(small_names, _unpack_rows(flat, shapes, LANES)):
            dst[n] = a
    return (loss, grad_x, *[grads[n] for n in WEIGHTS], *[delta[n] for n in WEIGHTS],
            *[new_m[n] for n in WEIGHTS], *[new_v[n] for n in WEIGHTS])
```

```python
import functools

import jax
import jax.numpy as jnp
from jax import lax
from jax.experimental import pallas as pl
from jax.experimental.pallas import tpu as pltpu

F32 = jnp.float32
BF16 = jnp.bfloat16
MXU_DTYPE = jnp.bfloat16

D_MODEL = 1024
DEPTH = 4
PLE_DIM = 256
ALPHA = (2 * DEPTH) ** 0.25
LN_EPS = 1e-5
RMS_EPS = 1e-6
HEAD_DIM = 64
POOL_WIDTH = 256
POOL_WINDOWS = (2, 4, 8, 16)
SSD_WIDTH = 384
SSD_HEADS = 6
SSD_STATE = 128
SSD_XBC = 896
SB_WIDTH = 384
IN_COLS = 2694
D_FF = 2816
N_DEV = 8

ADAM_LR = 0.001
ADAM_B1 = 0.9
ADAM_B2 = 0.999
ADAM_EPS = 1e-08
ADAM_WD = 0.01
ADAM_STEP = 10

LANES = 128
SUBLANES = 8
VMEM_LIMIT_BYTES = 56 * 1024 * 1024

H_COLS = 2816
H_BC = 0
H_POOL = 512
H_Q = 768
H_K = 1152
H_V = 1536
H_Z = 1920
H_XS = 2304
H_DT = 2688
SSD_CHUNK = 128
QB = 256
GLU_TILE = 256
MASKED_LOG = -1e30

NN = ((1,), (0,))
NT = ((1,), (1,))
TN = ((0,), (0,))


def _dot(a, b, dims=NN):
    return lax.dot_general(a.astype(MXU_DTYPE), b.astype(MXU_DTYPE), (dims, ((), ())), preferred_element_type=F32)


def _dot_exact01(x, m01, dims=NN, x_left=True, terms=3):
    acc = None
    r = x
    for _ in range(terms):
        hi = r.astype(BF16)
        ops = (hi, m01) if x_left else (m01, hi)
        part = lax.dot_general(ops[0], ops[1], (dims, ((), ())), preferred_element_type=F32)
        acc = part if acc is None else acc + part
        r = r - hi.astype(F32)
    return acc


def _sigmoid(v):
    return 1.0 / (1.0 + jnp.exp(-v))


def _silu(v):
    return v * _sigmoid(v)


def _dsilu(v):
    s = _sigmoid(v)
    return s * (1.0 + v * (1.0 - s))


def _softplus(v):
    return jnp.maximum(v, 0.0) + jnp.log(1.0 + jnp.exp(-jnp.abs(v)))


def _params(n_axes, side_effects=False):
    return pltpu.CompilerParams(dimension_semantics=("arbitrary",) * n_axes, vmem_limit_bytes=VMEM_LIMIT_BYTES,
                                has_side_effects=side_effects)


MESH_ID = pl.DeviceIdType.MESH
_ANY = pl.BlockSpec(memory_space=pl.ANY)


def _flip(v, bit):
    return 1 - v if bit else v


def _comm_counts(comm):
    return (0, 0) if comm is None else (len(comm["inputs"]), len(comm["out_shapes"]))


def _comm_call_args(comm):
    if comm is None:
        return [], [], [], []
    n = comm["n_xfers"]
    sems = [pltpu.SemaphoreType.DMA(((N_DEV - 1) * n,)), pltpu.SemaphoreType.DMA(((N_DEV - 1) * n,)),
            pltpu.SemaphoreType.DMA((n,))]
    return list(comm["inputs"]), [_ANY] * len(comm["out_shapes"]), list(comm["out_shapes"]), sems


def _comm_descs(comm, in_refs, tail_refs, with_recvs=True):
    n_out = len(comm["out_shapes"])
    out_refs, (send_sems, recv_sems, local_sems) = tail_refs[:n_out], tail_refs[n_out:n_out + 3]
    xfers = comm["xfers"](in_refs, out_refs)
    n = len(xfers)
    assert n == comm["n_xfers"]
    x, y, c = lax.axis_index("x"), lax.axis_index("y"), lax.axis_index("c")
    me = 4 * x + 2 * y + c
    local = [pltpu.make_async_copy(src_for(me), dst_for(me), local_sems.at[t]) for t, (src_for, dst_for) in enumerate(xfers)]
    sends, recvs = [], []
    for k in range(1, N_DEV):
        pid = (_flip(x, k & 4), _flip(y, k & 2), _flip(c, k & 1))
        peer = 4 * pid[0] + 2 * pid[1] + pid[2]
        for t, (src_for, dst_for) in enumerate(xfers):
            idx = (k - 1) * n + t
            sends.append(pltpu.make_async_remote_copy(
                src_ref=src_for(peer), dst_ref=dst_for(me), send_sem=send_sems.at[idx], recv_sem=recv_sems.at[idx],
                device_id=pid, device_id_type=MESH_ID))
            if with_recvs:
                recvs.append(pltpu.make_async_remote_copy(
                    src_ref=src_for(peer), dst_ref=dst_for(peer), send_sem=send_sems.at[idx], recv_sem=recv_sems.at[idx],
                    device_id=pid, device_id_type=MESH_ID))
    return local, sends, recvs


def _comm_start(descs):
    local, sends, _ = descs
    for cp in local + sends:
        cp.start()


def _comm_wait(descs):
    local, sends, recvs = descs
    for cp in recvs:
        cp.wait_recv()
    for cp in sends:
        cp.wait_send()
    for cp in local:
        cp.wait()


def _comm_hosted(comm, in_refs, tail_refs, grid):
    if comm is None:
        return
    ids = [pl.program_id(a) for a in range(len(grid))]
    first = functools.reduce(jnp.logical_and, [i == 0 for i in ids])
    last = functools.reduce(jnp.logical_and, [i == g - 1 for i, g in zip(ids, grid)])

    @pl.when(first)
    def _():
        _comm_start(_comm_descs(comm, in_refs, tail_refs, with_recvs=False))

    @pl.when(last)
    def _():
        _comm_wait(_comm_descs(comm, in_refs, tail_refs))


def _comm_call(comm, name):
    n_in = len(comm["inputs"])

    def body(*refs):
        descs = _comm_descs(comm, refs[:n_in], refs[n_in:])
        _comm_start(descs)
        _comm_wait(descs)

    c_in, c_specs, c_shapes, c_scratch = _comm_call_args(comm)
    return pl.pallas_call(body, name=name, in_specs=[_ANY] * n_in, out_specs=c_specs, out_shape=c_shapes,
                          scratch_shapes=c_scratch, compiler_params=pltpu.CompilerParams(has_side_effects=True))(*c_in)


def _rows(ref, j, n):
    return ref.at[pl.ds(pl.multiple_of(j * n, SUBLANES), n), :]


def _gather_job(sh, conv=None):
    conv = list(conv or [])
    sds = jax.ShapeDtypeStruct
    out_shapes = [sds((D_MODEL, H_COLS), MXU_DTYPE), sds((D_MODEL, D_MODEL), MXU_DTYPE), sds((N_DEV, D_MODEL, 704), MXU_DTYPE),
                  sds((D_FF, D_MODEL), MXU_DTYPE), sds((D_MODEL, D_MODEL), MXU_DTYPE), sds((PLE_DIM, D_MODEL), MXU_DTYPE)]
    out_shapes += [sds((N_DEV,) + a.shape, a.dtype) for a in conv]

    def xfers(ins, outs):
        whole = lambda a: (lambda j: a)
        r = [(whole(ins[0]), lambda j: _rows(outs[0], j, 128)),
             (whole(ins[1]), lambda j: _rows(outs[1], lax.rem(j + 6, N_DEV), 128)),
             (whole(ins[2]), lambda j: outs[2].at[j]),
             (whole(ins[3]), lambda j: _rows(outs[3], j, 352)),
             (whole(ins[4]), lambda j: _rows(outs[4], j, 128)),
             (whole(ins[5]), lambda j: outs[5].at[:, pl.ds(pl.multiple_of(j * LANES, LANES), LANES)])]
        for t in range(len(conv)):
            r.append((whole(ins[6 + t]), lambda j, o=outs[6 + t]: o.at[j]))
        return r

    return dict(inputs=[sh[n] for n in BIG] + conv, out_shapes=out_shapes, xfers=xfers, n_xfers=6 + len(conv))


_SHARD_SHAPES = {"w_in": (128, H_COLS), "w_out": (128, D_MODEL), "ffn_w_up": (D_MODEL, 704), "ffn_w_down": (352, D_MODEL),
                 "ple_w_gate": (128, D_MODEL), "ple_w_proj": (PLE_DIM, LANES)}


def _exchange_job(items):
    def source(name, ref):
        if name in ("w_in", "ple_w_gate"):
            return lambda j: _rows(ref, j, 128)
        if name == "w_out":
            return lambda j: _rows(ref, lax.rem(j + 6, N_DEV), 128)
        if name == "ffn_w_up":
            return lambda j: ref.at[j]
        if name == "ffn_w_down":
            return lambda j: _rows(ref, j, 352)
        assert name == "ple_w_proj"
        return lambda j: ref.at[:, pl.ds(pl.multiple_of(j * LANES, LANES), LANES)]

    def xfers(ins, outs):
        return [(source(name, i), lambda j, o=o: o.at[j]) for (name, _), i, o in zip(items, ins, outs)]

    return dict(inputs=[g for _, g in items], xfers=xfers, n_xfers=len(items),
                out_shapes=[jax.ShapeDtypeStruct((N_DEV,) + _SHARD_SHAPES[name], F32) for name, _ in items])


def _pick(n, pref):
    if n <= pref:
        return n
    for t in range(pref - pref % LANES, 0, -LANES):
        if n % t == 0:
            return t
    raise ValueError((n, pref))


def _mm(a, b, mode, out_dtype, name, tm=512, tn=512, tk=1024, add=None, add_coef=1.0):
    if mode == "nn":
        (m, k), (k2, n) = a.shape, b.shape
    elif mode == "nt":
        (m, k), (n, k2) = a.shape, b.shape
    else:
        (k, m), (k2, n) = a.shape, b.shape
    assert k == k2, (a.shape, b.shape, mode)
    tm, tn, tk = _pick(m, tm), _pick(n, tn), _pick(k, tk)
    nk = k // tk
    dims = {"nn": NN, "nt": NT, "tn": TN}[mode]

    def body(*refs):
        a_ref, b_ref = refs[:2]
        add_ref = refs[2] if add is not None else None
        o_ref = refs[3] if add is not None else refs[2]

        def finish(r):
            if add_ref is not None:
                r = r + add_coef * add_ref[...]
            o_ref[...] = r.astype(out_dtype)

        if nk == 1:
            finish(_dot(a_ref[...], b_ref[...], dims))
            return
        acc_ref = refs[-1]
        kk = pl.program_id(2)

        @pl.when(kk == 0)
        def _():
            acc_ref[...] = jnp.zeros_like(acc_ref)

        acc_ref[...] += _dot(a_ref[...], b_ref[...], dims)

        @pl.when(kk == nk - 1)
        def _():
            finish(acc_ref[...])

    if mode == "tn":
        a_spec = pl.BlockSpec((tk, tm), lambda i, j, kk: (kk, i))
    else:
        a_spec = pl.BlockSpec((tm, tk), lambda i, j, kk: (i, kk))
    if mode == "nt":
        b_spec = pl.BlockSpec((tn, tk), lambda i, j, kk: (j, kk))
    else:
        b_spec = pl.BlockSpec((tk, tn), lambda i, j, kk: (kk, j))
    o_spec = pl.BlockSpec((tm, tn), lambda i, j, kk: (i, j))
    in_specs = [a_spec, b_spec] + ([o_spec] if add is not None else [])
    args = (a, b) + ((add,) if add is not None else ())
    return pl.pallas_call(
        body, name=name, grid=(m // tm, n // tn, nk), in_specs=in_specs, out_specs=o_spec,
        out_shape=jax.ShapeDtypeStruct((m, n), out_dtype), scratch_shapes=[pltpu.VMEM((tm, tn), F32)] if nk > 1 else [],
        compiler_params=_params(3),
    )(*args)


def _ln_fwd(x, add, gb, name, gp=None, pp=None, tr=512):
    t, d = x.shape
    tr = _pick(t, tr)
    with_ple = gp is not None

    def body(*refs):
        if with_ple:
            x_ref, a_ref, gp_ref, pp_ref, gb_ref, y_ref, r_ref = refs
        else:
            x_ref, a_ref, gb_ref, y_ref, r_ref = refs
        r = ALPHA * x_ref[...] + a_ref[...]
        if with_ple:
            r = r + _sigmoid(gp_ref[...]) * pp_ref[...]
        mu = jnp.mean(r, axis=1, keepdims=True)
        xc = r - mu
        var = jnp.mean(xc * xc, axis=1, keepdims=True)
        y_ref[...] = xc * lax.rsqrt(var + LN_EPS) * gb_ref[0:1, :] + gb_ref[1:2, :]
        r_ref[...] = r

    row = pl.BlockSpec((tr, d), lambda i: (i, 0))
    vec = pl.BlockSpec((2, d), lambda i: (0, 0))
    n_row = 4 if with_ple else 2
    args = (x, add) + ((gp, pp) if with_ple else ()) + (gb,)
    return pl.pallas_call(
        body, name=name, grid=(t // tr,), in_specs=[row] * n_row + [vec], out_specs=[row, row],
        out_shape=[jax.ShapeDtypeStruct((t, d), F32)] * 2, compiler_params=_params(1),
    )(*args)


def _ln_bwd(r, gb, dy, name, gp=None, pp=None, tr=512):
    t, d = r.shape
    tr = _pick(t, tr)
    with_ple = gp is not None

    def body(*refs):
        if with_ple:
            r_ref, dy_ref, gp_ref, pp_ref, gb_ref, dr_ref, dgp_ref, dpp_ref, st_ref = refs
        else:
            r_ref, dy_ref, gb_ref, dr_ref, st_ref = refs
        i = pl.program_id(0)

        @pl.when(i == 0)
        def _():
            st_ref[...] = jnp.zeros_like(st_ref)

        rv = r_ref[...]
        dy_v = dy_ref[...]
        mu = jnp.mean(rv, axis=1, keepdims=True)
        xc = rv - mu
        var = jnp.mean(xc * xc, axis=1, keepdims=True)
        rstd = lax.rsqrt(var + LN_EPS)
        xhat = xc * rstd
        dxh = dy_v * gb_ref[0:1, :]
        m1 = jnp.mean(dxh, axis=1, keepdims=True)
        m2 = jnp.mean(dxh * xhat, axis=1, keepdims=True)
        dr = rstd * (dxh - m1 - xhat * m2)
        dr_ref[...] = dr
        rid = lax.broadcasted_iota(jnp.int32, (2, d), 0)
        dg = jnp.sum(dy_v * xhat, axis=0, keepdims=True)
        db = jnp.sum(dy_v, axis=0, keepdims=True)
        st_ref[...] += jnp.where(rid == 0, dg, db)
        if with_ple:
            sg = _sigmoid(gp_ref[...])
            ppv = pp_ref[...]
            dgp_ref[...] = (dr * ppv * sg * (1.0 - sg)).astype(dgp_ref.dtype)
            dpp_ref[...] = (dr * sg).astype(dpp_ref.dtype)

    row = pl.BlockSpec((tr, d), lambda i: (i, 0))
    vec = pl.BlockSpec((2, d), lambda i: (0, 0))
    if with_ple:
        in_specs, args = [row] * 4 + [vec], (r, dy, gp, pp, gb)
        out_specs = [row, row, row, vec]
        out_shape = [jax.ShapeDtypeStruct((t, d), F32), jax.ShapeDtypeStruct((t, d), MXU_DTYPE),
                     jax.ShapeDtypeStruct((t, d), MXU_DTYPE), jax.ShapeDtypeStruct((2, d), F32)]
    else:
        in_specs, args = [row] * 2 + [vec], (r, dy, gb)
        out_specs = [row, vec]
        out_shape = [jax.ShapeDtypeStruct((t, d), F32), jax.ShapeDtypeStruct((2, d), F32)]
    return pl.pallas_call(body, name=name, grid=(t // tr,), in_specs=in_specs, out_specs=out_specs,
                          out_shape=out_shape, compiler_params=_params(1))(*args)


def _loss_grad(y, target, name, tr=512):
    t, d = y.shape
    tr = _pick(t, tr)

    def body(y_ref, t_ref, dy_ref, l_ref):
        i = pl.program_id(0)

        @pl.when(i == 0)
        def _():
            l_ref[...] = jnp.zeros_like(l_ref)

        e = y_ref[...] - t_ref[...]
        dy_ref[...] = e * (1.0 / d)
        per_tok = jnp.mean(e * e, axis=1, keepdims=True)
        l_ref[...] += 0.5 * jnp.sum(per_tok, axis=0, keepdims=True)

    row = pl.BlockSpec((tr, d), lambda i: (i, 0))
    acc = pl.BlockSpec((SUBLANES, LANES), lambda i: (0, 0))
    return pl.pallas_call(body, name=name, grid=(t // tr,), in_specs=[row, row], out_specs=[row, acc],
                          out_shape=[jax.ShapeDtypeStruct((t, d), F32), jax.ShapeDtypeStruct((SUBLANES, LANES), F32)],
                          compiler_params=_params(1))(y, target)


def _shift_down(v, k, row):
    return jnp.where(row >= k, pltpu.roll(v, k, 0), 0.0)


def _shift_up(v, k, row):
    n = v.shape[0]
    return jnp.where(row < n - k, pltpu.roll(v, n - k, 0), 0.0)


def _pool_window(lane):
    grp = lane // HEAD_DIM
    return jnp.where(grp == 0, POOL_WINDOWS[0], jnp.where(grp == 1, POOL_WINDOWS[1],
                     jnp.where(grp == 2, POOL_WINDOWS[2], POOL_WINDOWS[3])))


def _pool_select(lane, s2, s4, s8, s16):
    grp = lane // HEAD_DIM
    return jnp.where(grp == 0, s2, jnp.where(grp == 1, s4, jnp.where(grp == 2, s8, s16)))


def _pooled(u, row, lane):
    s2 = u + _shift_down(u, 1, row)
    s4 = s2 + _shift_down(s2, 2, row)
    s8 = s4 + _shift_down(s4, 4, row)
    s16 = s8 + _shift_down(s8, 8, row)
    cnt = jnp.minimum(row + 1, _pool_window(lane)).astype(F32)
    return _pool_select(lane, s2, s4, s8, s16) / cnt - u, cnt


def _pool_fwd(h, wbd, scale, nb, s, name):
    def body(u_ref, w_ref, sc_ref, o_ref):
        u = u_ref[...]
        row = lax.broadcasted_iota(jnp.int32, u.shape, 0)
        lane = lax.broadcasted_iota(jnp.int32, u.shape, 1)
        pooled, _ = _pooled(u, row, lane)
        o_ref[...] = (_dot(pooled, w_ref[...]) * sc_ref[...]).astype(o_ref.dtype)

    wb = POOL_WIDTH
    return pl.pallas_call(
        body, name=name, grid=(nb,),
        in_specs=[pl.BlockSpec((s, wb), lambda b: (b, H_POOL // wb)), pl.BlockSpec((wb, wb), lambda b: (0, 0)),
                  pl.BlockSpec((1, wb), lambda b: (0, 0))],
        out_specs=pl.BlockSpec((s, wb), lambda b: (b, 0)),
        out_shape=jax.ShapeDtypeStruct((nb * s, wb), MXU_DTYPE), compiler_params=_params(1),
    )(h, wbd, scale)


def _pool_bwd(h, dmix, wbd, scale, nb, s, name):
    wb = POOL_WIDTH

    def body(u_ref, do_ref, w_ref, sc_ref, du_ref, dw_ref, ds_ref):
        b = pl.program_id(0)

        @pl.when(b == 0)
        def _():
            dw_ref[...] = jnp.zeros_like(dw_ref)
            ds_ref[...] = jnp.zeros_like(ds_ref)

        u = u_ref[...]
        row = lax.broadcasted_iota(jnp.int32, u.shape, 0)
        lane = lax.broadcasted_iota(jnp.int32, u.shape, 1)
        pooled, cnt = _pooled(u, row, lane)
        mixed = _dot(pooled, w_ref[...])
        do = do_ref[...]
        ds_ref[...] += jnp.sum(do * mixed, axis=0, keepdims=True)
        dm = do * sc_ref[...]
        dw_ref[...] += _dot(pooled, dm, TN)
        dpool = _dot(dm, w_ref[...], NT)
        qv = dpool / cnt
        f2 = qv + _shift_up(qv, 1, row)
        f4 = f2 + _shift_up(f2, 2, row)
        f8 = f4 + _shift_up(f4, 4, row)
        f16 = f8 + _shift_up(f8, 8, row)
        du_ref[...] = (_pool_select(lane, f2, f4, f8, f16) - dpool).astype(du_ref.dtype)

    return pl.pallas_call(
        body, name=name, grid=(nb,),
        in_specs=[pl.BlockSpec((s, wb), lambda b: (b, H_POOL // wb)), pl.BlockSpec((s, wb), lambda b: (b, 3)),
                  pl.BlockSpec((wb, wb), lambda b: (0, 0)), pl.BlockSpec((1, wb), lambda b: (0, 0))],
        out_specs=[pl.BlockSpec((s, wb), lambda b: (b, 0)), pl.BlockSpec((wb, wb), lambda b: (0, 0)),
                   pl.BlockSpec((1, wb), lambda b: (0, 0))],
        out_shape=[jax.ShapeDtypeStruct((nb * s, wb), MXU_DTYPE), jax.ShapeDtypeStruct((wb, wb), F32),
                   jax.ShapeDtypeStruct((1, wb), F32)],
        compiler_params=_params(1),
    )(h, dmix, wbd, scale)


def _glu_conv(x, w_ref, b_ref, row):
    return (b_ref[...] + w_ref[2:3, :] * x + w_ref[1:2, :] * _shift_down(x, 1, row)
            + w_ref[0:1, :] * _shift_down(x, 2, row))


def _glu_fwd(up, cw, cb, nb, s, name):
    wt = 2 * GLU_TILE
    nt = up.shape[1] // wt

    def body(u_ref, w_ref, b_ref, o_ref):
        x = u_ref[...]
        row = lax.broadcasted_iota(jnp.int32, x.shape, 0)
        c = _glu_conv(x, w_ref, b_ref, row)
        o_ref[...] = (_silu(c[:, :GLU_TILE]) * c[:, GLU_TILE:]).astype(o_ref.dtype)

    return pl.pallas_call(
        body, name=name, grid=(nt, nb),
        in_specs=[pl.BlockSpec((s, wt), lambda j, b: (b, j)), pl.BlockSpec((3, wt), lambda j, b: (0, j)),
                  pl.BlockSpec((1, wt), lambda j, b: (0, j))],
        out_specs=pl.BlockSpec((s, GLU_TILE), lambda j, b: (b, j)),
        out_shape=jax.ShapeDtypeStruct((nb * s, nt * GLU_TILE), MXU_DTYPE), compiler_params=_params(2),
    )(up, cw, cb)


def _glu_bwd(up, dact, cw, cb, nb, s, name):
    wt = 2 * GLU_TILE
    nt = up.shape[1] // wt

    def body(u_ref, da_ref, w_ref, b_ref, du_ref, acc_ref):
        b = pl.program_id(1)

        @pl.when(b == 0)
        def _():
            acc_ref[...] = jnp.zeros_like(acc_ref)

        x = u_ref[...]
        row = lax.broadcasted_iota(jnp.int32, x.shape, 0)
        x1 = _shift_down(x, 1, row)
        x2 = _shift_down(x, 2, row)
        c = b_ref[...] + w_ref[2:3, :] * x + w_ref[1:2, :] * x1 + w_ref[0:1, :] * x2
        gate, val = c[:, :GLU_TILE], c[:, GLU_TILE:]
        da = da_ref[...]
        dc = jnp.concatenate([da * val * _dsilu(gate), da * _silu(gate)], axis=1)
        dx = (w_ref[2:3, :] * dc + w_ref[1:2, :] * _shift_up(dc, 1, row) + w_ref[0:1, :] * _shift_up(dc, 2, row))
        du_ref[...] = dx.astype(du_ref.dtype)
        rid = lax.broadcasted_iota(jnp.int32, (SUBLANES, wt), 0)
        dw0 = jnp.sum(dc * x2, axis=0, keepdims=True)
        dw1 = jnp.sum(dc * x1, axis=0, keepdims=True)
        dw2 = jnp.sum(dc * x, axis=0, keepdims=True)
        db = jnp.sum(dc, axis=0, keepdims=True)
        acc_ref[...] += (jnp.where(rid == 0, dw0, 0.0) + jnp.where(rid == 1, dw1, 0.0)
                         + jnp.where(rid == 2, dw2, 0.0) + jnp.where(rid == 3, db, 0.0))

    return pl.pallas_call(
        body, name=name, grid=(nt, nb),
        in_specs=[pl.BlockSpec((s, wt), lambda j, b: (b, j)), pl.BlockSpec((s, GLU_TILE), lambda j, b: (b, j)),
                  pl.BlockSpec((3, wt), lambda j, b: (0, j)), pl.BlockSpec((1, wt), lambda j, b: (0, j))],
        out_specs=[pl.BlockSpec((s, wt), lambda j, b: (b, j)), pl.BlockSpec((SUBLANES, wt), lambda j, b: (0, j))],
        out_shape=[jax.ShapeDtypeStruct((nb * s, nt * wt), MXU_DTYPE), jax.ShapeDtypeStruct((SUBLANES, nt * wt), F32)],
        compiler_params=_params(2),
    )(up, dact, cw, cb)


def _sb_constants():
    row = lax.broadcasted_iota(jnp.int32, (QB, QB), 0)
    col = lax.broadcasted_iota(jnp.int32, (QB, QB), 1)
    return jnp.stack([row > col, row < col, col < row]).astype(BF16)


_SB_CONST_SPEC = pl.BlockSpec((3, QB, QB), lambda b, p, i: (0, 0, 0))


def _sb_fwd(h, nb, s, name, comm=None):
    nq = s // QB
    scale = HEAD_DIM ** -0.5
    n_in, n_out = _comm_counts(comm)

    def body(q_ref, k_ref, v_ref, tri_ref, *rest):
        o_ref = rest[n_in]
        i = pl.program_id(2)
        _comm_hosted(comm, rest[:n_in], rest[n_in + 1:], (nb, 3, nq))
        sls = [slice(hd * HEAD_DIM, (hd + 1) * HEAD_DIM) for hd in range(2)]
        qs = [(q_ref[:, sl] * scale).astype(MXU_DTYPE) for sl in sls]

        def scores(hd, j, diagonal=False):
            r0 = pl.multiple_of(j * QB, QB)
            z = _dot(qs[hd], k_ref[pl.ds(r0, QB), sls[hd]], NT)
            ln = -_softplus(z)
            ls = z + ln
            if diagonal:
                low = tri_ref[2] > 0
                ln = jnp.where(low, ln, 0.0)
                ls = jnp.where(low, ls, MASKED_LOG)
            return ls, _dot_exact01(ln, tri_ref[0], terms=2), jnp.sum(ln, axis=1, keepdims=True)

        def output(hd, j, ls, tl, ct):
            r0 = pl.multiple_of(j * QB, QB)
            return _dot(jnp.exp(ls + tl + ct), v_ref[pl.ds(r0, QB), sls[hd]])

        def group(blocks, carry, diagonal_first=False):
            sc = [[scores(hd, j, diagonal_first and n == 0) for n, j in enumerate(blocks)] for hd in range(2)]
            out = []
            for hd in range(2):
                a, c = carry[hd]
                for (ls, tl, sm), j in zip(sc[hd], blocks):
                    a = a + output(hd, j, ls, tl, c)
                    c = c + sm
                out.append((a, c))
            return tuple(out)

        start = (jnp.zeros((QB, HEAD_DIM), F32), jnp.zeros((QB, 1), F32))
        below = jnp.minimum(i, 1)
        left = i - below
        carry = lax.fori_loop(0, below, lambda t, c: group([i, i - 1], c, True), (start, start))
        carry = lax.fori_loop(0, 1 - below, lambda t, c: group([i], c, True), carry)
        carry = lax.fori_loop(0, left // 2, lambda t, c: group([left - 1 - 2 * t, left - 2 - 2 * t], c), carry)
        carry = lax.fori_loop(0, left % 2, lambda t, c: group([0], c), carry)
        o_ref[:, sls[0]] = carry[0][0].astype(o_ref.dtype)
        o_ref[:, sls[1]] = carry[1][0].astype(o_ref.dtype)

    qspec = lambda off: pl.BlockSpec((QB, LANES), lambda b, p, i: (b * nq + i, off // LANES + p))
    kvspec = lambda off: pl.BlockSpec((s, LANES), lambda b, p, i: (b, off // LANES + p))
    c_in, c_specs, c_shapes, c_scratch = _comm_call_args(comm)
    res = pl.pallas_call(
        body, name=name, grid=(nb, 3, nq), in_specs=[qspec(H_Q), kvspec(H_K), kvspec(H_V), _SB_CONST_SPEC] + [_ANY] * n_in,
        out_specs=[pl.BlockSpec((QB, LANES), lambda b, p, i: (b * nq + i, p))] + c_specs,
        out_shape=[jax.ShapeDtypeStruct((nb * s, SB_WIDTH), MXU_DTYPE)] + c_shapes, scratch_shapes=c_scratch,
        compiler_params=_params(3, comm is not None),
    )(h, h, h, _sb_constants(), *c_in)
    return res[0], res[1:]


def _sb_bwd(h, dmix, nb, s, name, comm=None):
    nq = s // QB
    scale = HEAD_DIM ** -0.5
    n_in, n_out = _comm_counts(comm)

    def body(q_ref, k_ref, v_ref, do_ref, tri_ref, *rest):
        dq_ref, dk_out, dv_out = rest[n_in:n_in + 3]
        p_buf, ls_buf, dk_ref, dv_ref = rest[n_in + 3 + n_out:n_in + 7 + n_out]
        i = pl.program_id(2)
        _comm_hosted(comm, rest[:n_in], rest[n_in + 3:n_in + 3 + n_out] + rest[n_in + 7 + n_out:], (nb, 3, nq))

        @pl.when(i == 0)
        def _():
            dk_ref[...] = jnp.zeros_like(dk_ref)
            dv_ref[...] = jnp.zeros_like(dv_ref)

        sls = [slice(hd * HEAD_DIM, (hd + 1) * HEAD_DIM) for hd in range(2)]
        q_raw = [q_ref[:, sl].astype(MXU_DTYPE) for sl in sls]
        qs = [(q_ref[:, sl] * scale).astype(MXU_DTYPE) for sl in sls]
        do = [do_ref[:, sl].astype(MXU_DTYPE) for sl in sls]

        def down_scores(hd, j, diagonal):
            r0 = pl.multiple_of(j * QB, QB)
            z = _dot(qs[hd], k_ref[pl.ds(r0, QB), sls[hd]], NT)
            ln = -_softplus(z)
            ls = z + ln
            if diagonal:
                low = tri_ref[2] > 0
                ln = jnp.where(low, ln, 0.0)
                ls = jnp.where(low, ls, MASKED_LOG)
            da = _dot(do[hd], v_ref[pl.ds(r0, QB), sls[hd]], NT)
            return ls, _dot_exact01(ln, tri_ref[0], terms=2), jnp.sum(ln, axis=1, keepdims=True), da

        def down_group(blocks, carry, diagonal_first=False):
            sc = [[down_scores(hd, j, diagonal_first and n == 0) for n, j in enumerate(blocks)] for hd in range(2)]
            out = []
            for hd in range(2):
                ct = carry[hd]
                for (ls, tl, sm, da), j in zip(sc[hd], blocks):
                    r0 = pl.multiple_of(j * QB, QB)
                    a = jnp.exp(ls + tl + ct)
                    p_buf[hd, j] = da * a
                    ls_buf[hd, j] = ls
                    dv_ref[pl.ds(r0, QB), sls[hd]] += _dot(a, do[hd], TN)
                    ct = ct + sm
                out.append(ct)
            return tuple(out)

        zero = jnp.zeros((QB, 1), F32)
        below = jnp.minimum(i, 1)
        left = i - below
        carry = lax.fori_loop(0, below, lambda t, c: down_group([i, i - 1], c, True), (zero, zero))
        carry = lax.fori_loop(0, 1 - below, lambda t, c: down_group([i], c, True), carry)
        carry = lax.fori_loop(0, left // 2, lambda t, c: down_group([left - 1 - 2 * t, left - 2 - 2 * t], c), carry)
        lax.fori_loop(0, left % 2, lambda t, c: down_group([0], c), carry)

        def up_group(blocks, carry):
            ld = []
            for hd in range(2):
                ld.append([])
                for j in blocks:
                    pj = p_buf[hd, j]
                    ld[hd].append((pj, jnp.exp(ls_buf[hd, j]), _dot_exact01(pj, tri_ref[1]), jnp.sum(pj, axis=1, keepdims=True)))
            out = []
            for hd in range(2):
                dq, cp = carry[hd]
                for (pj, sg, cm, sm), j in zip(ld[hd], blocks):
                    r0 = pl.multiple_of(j * QB, QB)
                    dz = (pj * (1.0 - sg) - (cp + cm) * sg) * scale
                    dk_ref[pl.ds(r0, QB), sls[hd]] += _dot(dz, q_raw[hd], TN)
                    dq = dq + _dot(dz, k_ref[pl.ds(r0, QB), sls[hd]])
                    cp = cp + sm
                out.append((dq, cp))
            return tuple(out)

        start = (jnp.zeros((QB, HEAD_DIM), F32), zero)
        odd = (i + 1) % 2
        carry = lax.fori_loop(0, odd, lambda t, c: up_group([0], c), (start, start))
        carry = lax.fori_loop(0, (i + 1) // 2, lambda t, c: up_group([odd + 2 * t, odd + 2 * t + 1], c), carry)
        dq_ref[:, sls[0]] = carry[0][0].astype(dq_ref.dtype)
        dq_ref[:, sls[1]] = carry[1][0].astype(dq_ref.dtype)

        @pl.when(i == nq - 1)
        def _():
            dk_out[...] = dk_ref[...].astype(dk_out.dtype)
            dv_out[...] = dv_ref[...].astype(dv_out.dtype)

    qspec = lambda off: pl.BlockSpec((QB, LANES), lambda b, p, i: (b * nq + i, off // LANES + p))
    kvspec = lambda off: pl.BlockSpec((s, LANES), lambda b, p, i: (b, off // LANES + p))
    blk_out = pl.BlockSpec((QB, LANES), lambda b, p, i: (b * nq + i, p))
    seq_out = pl.BlockSpec((s, LANES), lambda b, p, i: (b, p))
    shp = jax.ShapeDtypeStruct((nb * s, SB_WIDTH), MXU_DTYPE)
    c_in, c_specs, c_shapes, c_scratch = _comm_call_args(comm)
    res = pl.pallas_call(
        body, name=name, grid=(nb, 3, nq),
        in_specs=[qspec(H_Q), kvspec(H_K), kvspec(H_V), pl.BlockSpec((QB, LANES), lambda b, p, i: (b * nq + i, 3 + p)),
                  _SB_CONST_SPEC] + [_ANY] * n_in,
        out_specs=[blk_out, seq_out, seq_out] + c_specs, out_shape=[shp, shp, shp] + c_shapes,
        scratch_shapes=[pltpu.VMEM((2, nq, QB, QB), F32), pltpu.VMEM((2, nq, QB, QB), F32),
                        pltpu.VMEM((s, LANES), F32), pltpu.VMEM((s, LANES), F32)] + c_scratch,
        compiler_params=_params(3, comm is not None),
    )(h, h, h, dmix, _sb_constants(), *c_in)
    return res[0], res[1], res[2], res[3:]


def _ssd_conv(cur_ref, halo_ref, w_ref, b_ref, ext_ref, first):
    n = SSD_CHUNK
    cur = cur_ref[...]
    ext_ref[0:SUBLANES, :] = jnp.where(first, 0.0, halo_ref[...])
    ext_ref[SUBLANES:SUBLANES + n, :] = cur
    return (b_ref[...] + w_ref[3:4, :] * cur + w_ref[2:3, :] * ext_ref[pl.ds(SUBLANES - 1, n), :]
            + w_ref[1:2, :] * ext_ref[pl.ds(SUBLANES - 2, n), :] + w_ref[0:1, :] * ext_ref[pl.ds(SUBLANES - 3, n), :])


def _ssd_tri():
    row = lax.broadcasted_iota(jnp.int32, (SSD_CHUNK, SSD_CHUNK), 0)
    col = lax.broadcasted_iota(jnp.int32, (SSD_CHUNK, SSD_CHUNK), 1)
    return row, col


def _ssd_specs(nc, rev):
    n = SSD_CHUNK
    hb = n // SUBLANES

    def cidx(c):
        return (nc - 1 - c) if rev else c

    def blk(width, off):
        return pl.BlockSpec((n, width), lambda b, c: (b * nc + cidx(c), off // width))

    def halo(width, off):
        return pl.BlockSpec((SUBLANES, width), lambda b, c: (jnp.maximum((b * nc + cidx(c)) * hb - 1, 0), off // width))

    def full(shape):
        return pl.BlockSpec(shape, lambda b, c: (0,) * len(shape))

    return cidx, blk, halo, full


def _ssd_core_fwd(x, bc, dt, acum, acum_t, a_row, d_row, h_prev_ref, tri):
    n = SSD_CHUNK
    heads = []
    for g in range(2):
        bm = bc[:, g * SSD_STATE:(g + 1) * SSD_STATE]
        cm = bc[:, 2 * SSD_STATE + g * SSD_STATE: 2 * SSD_STATE + (g + 1) * SSD_STATE]
        gmat = _dot(cm, bm, NT)
        for r in range(3):
            hh = g * 3 + r
            ac = acum[:, hh:hh + 1]
            ar = acum_t[hh:hh + 1, :]
            dec = jnp.where(tri, jnp.exp(jnp.minimum(ac - ar, 0.0)), 0.0)
            xh = x[:, hh * HEAD_DIM:(hh + 1) * HEAD_DIM]
            dth = dt[:, hh:hh + 1]
            xdt = xh * dth
            hp = h_prev_ref[hh * HEAD_DIM:(hh + 1) * HEAD_DIM, :]
            ea = jnp.exp(ac)
            m = gmat * dec
            yo = ea * _dot(cm, hp, NT)
            al = acum[n - 1:n, hh:hh + 1]
            w = jnp.exp(al - ac)
            y = _dot(m, xdt) + yo + d_row[:, hh:hh + 1] * xh
            heads.append(dict(g=g, hh=hh, bm=bm, cm=cm, gmat=gmat, dec=dec, xh=xh, dth=dth, xdt=xdt, hp=hp, ea=ea,
                              m=m, yo=yo, al=al, w=w, y=y))
    return heads


def _ssd_prep(xs_ref, xsh_ref, bc_ref, bch_ref, dt_ref, cwx_ref, cbx_ref, cwb_ref, cbb_ref, vec_ref, xe_ref, be_ref, first):
    pre_x = _ssd_conv(xs_ref, xsh_ref, cwx_ref, cbx_ref, xe_ref, first)
    pre_bc = _ssd_conv(bc_ref, bch_ref, cwb_ref, cbb_ref, be_ref, first)
    x = _silu(pre_x)
    bc = _silu(pre_bc)
    dt_pre = dt_ref[...] + vec_ref[0:1, :]
    dt = _softplus(dt_pre)
    a_row = vec_ref[1:2, :]
    amat = dt * a_row
    row, col = _ssd_tri()
    upper = (row <= col).astype(BF16)
    lower = (col <= row).astype(BF16)
    acum = _dot_exact01(amat, lower, NN, x_left=False)
    acum_t = _dot_exact01(amat, upper, TN, x_left=True)
    return pre_x, pre_bc, x, bc, dt_pre, dt, a_row, acum, acum_t, row, col, upper


def _ssd_gate_norm(y, z, nw):
    lane = lax.broadcasted_iota(jnp.int32, y.shape, 1)
    g0 = lane < SSD_WIDTH // 2
    hg = y * _silu(z)
    sq = hg * hg
    ms0 = jnp.sum(jnp.where(g0, sq, 0.0), axis=1, keepdims=True) * (2.0 / SSD_WIDTH)
    ms1 = jnp.sum(jnp.where(g0, 0.0, sq), axis=1, keepdims=True) * (2.0 / SSD_WIDTH)
    rs = jnp.where(g0, lax.rsqrt(ms0 + RMS_EPS), lax.rsqrt(ms1 + RMS_EPS))
    return hg, rs, g0


def _ssd_fwd(h, cwx, cbx, cwb, cbb, vec, nw, nb, s, name):
    n = SSD_CHUNK
    nc = s // n
    _, blk, halo, full = _ssd_specs(nc, False)

    def body(bc_ref, bch_ref, z_ref, xs_ref, xsh_ref, dt_ref, cwx_ref, cbx_ref, cwb_ref, cbb_ref, vec_ref, nw_ref,
             o_ref, hs_ref, h_scr, xe_ref, be_ref, y_scr):
        c = pl.program_id(1)

        @pl.when(c == 0)
        def _():
            h_scr[...] = jnp.zeros_like(h_scr)

        (_, _, x, bc, _, dt, a_row, acum, acum_t, row, col, _) = _ssd_prep(
            xs_ref, xsh_ref, bc_ref, bch_ref, dt_ref, cwx_ref, cbx_ref, cwb_ref, cbb_ref, vec_ref, xe_ref, be_ref, c == 0)
        hs_ref[...] = h_scr[...]
        heads = _ssd_core_fwd(x, bc, dt, acum, acum_t, a_row, vec_ref[2:3, :], hs_ref, col <= row)
        for hd in heads:
            sl = slice(hd["hh"] * HEAD_DIM, (hd["hh"] + 1) * HEAD_DIM)
            y_scr[:, sl] = hd["y"]
            h_scr[sl, :] = jnp.exp(hd["al"]) * hd["hp"] + _dot(hd["xdt"] * hd["w"], hd["bm"], TN)
        hg, rs, _ = _ssd_gate_norm(y_scr[...], z_ref[...], nw_ref[...])
        o_ref[...] = (hg * rs * nw_ref[...]).astype(o_ref.dtype)

    t = nb * s
    return pl.pallas_call(
        body, name=name, grid=(nb, nc),
        in_specs=[blk(512, H_BC), halo(512, H_BC), blk(384, H_Z), blk(384, H_XS), halo(384, H_XS), blk(128, H_DT),
                  full((4, 384)), full((1, 384)), full((4, 512)), full((1, 512)), full((SUBLANES, LANES)), full((1, 384))],
        out_specs=[pl.BlockSpec((n, SSD_WIDTH), lambda b, c: (b * nc + c, 0)),
                   pl.BlockSpec((None, SSD_WIDTH, SSD_STATE), lambda b, c: (b * nc + c, 0, 0))],
        out_shape=[jax.ShapeDtypeStruct((t, SSD_WIDTH), MXU_DTYPE),
                   jax.ShapeDtypeStruct((nb * nc, SSD_WIDTH, SSD_STATE), F32)],
        scratch_shapes=[pltpu.VMEM((SSD_WIDTH, SSD_STATE), F32), pltpu.VMEM((n + SUBLANES, 384), F32),
                        pltpu.VMEM((n + SUBLANES, 512), F32), pltpu.VMEM((n, SSD_WIDTH), F32)],
        compiler_params=_params(2),
    )(h, h, h, h, h, h, cwx, cbx, cwb, cbb, vec, nw)


def _ssd_bwd(h, hstate, dmix, cwx, cbx, cwb, cbb, vec, nw, nb, s, name):
    n = SSD_CHUNK
    nc = s // n
    cidx, blk, halo, full = _ssd_specs(nc, True)

    def body(bc_ref, bch_ref, z_ref, xs_ref, xsh_ref, dt_ref, hs_ref, do_ref, cwx_ref, cbx_ref, cwb_ref, cbb_ref,
             vec_ref, nw_ref, dz_ref, dxs_ref, dbc_ref, ddt_ref, gx_ref, gb_ref, gv_ref, gn_ref,
             dh_scr, xe_ref, be_ref, y_scr, dx_scr, dbc_scr, dxe_ref, dbe_ref, cx_ref, cb_ref):
        b = pl.program_id(0)
        c = pl.program_id(1)
        cc = nc - 1 - c

        @pl.when(jnp.logical_and(b == 0, c == 0))
        def _():
            gx_ref[...] = jnp.zeros_like(gx_ref)
            gb_ref[...] = jnp.zeros_like(gb_ref)
            gv_ref[...] = jnp.zeros_like(gv_ref)
            gn_ref[...] = jnp.zeros_like(gn_ref)

        @pl.when(c == 0)
        def _():
            dh_scr[...] = jnp.zeros_like(dh_scr)
            cx_ref[...] = jnp.zeros_like(cx_ref)
            cb_ref[...] = jnp.zeros_like(cb_ref)

        (pre_x, pre_bc, x, bc, dt_pre, dt, a_row, acum, acum_t, row, col, upper) = _ssd_prep(
            xs_ref, xsh_ref, bc_ref, bch_ref, dt_ref, cwx_ref, cbx_ref, cwb_ref, cbb_ref, vec_ref, xe_ref, be_ref, cc == 0)
        tri = col <= row
        d_row = vec_ref[2:3, :]
        heads = _ssd_core_fwd(x, bc, dt, acum, acum_t, a_row, d_row, hs_ref, tri)
        for hd in heads:
            y_scr[:, hd["hh"] * HEAD_DIM:(hd["hh"] + 1) * HEAD_DIM] = hd["y"]
        y = y_scr[...]
        z = z_ref[...]
        nwv = nw_ref[...]
        hg, rs, g0 = _ssd_gate_norm(y, z, nwv)
        do = do_ref[...]
        nrm = hg * rs
        gn_ref[...] += jnp.sum(do * nrm, axis=0, keepdims=True)
        dn = do * nwv
        dnn = dn * nrm
        mean0 = jnp.sum(jnp.where(g0, dnn, 0.0), axis=1, keepdims=True) * (2.0 / SSD_WIDTH)
        mean1 = jnp.sum(jnp.where(g0, 0.0, dnn), axis=1, keepdims=True) * (2.0 / SSD_WIDTH)
        dhg = rs * (dn - nrm * jnp.where(g0, mean0, mean1))
        dz_ref[...] = (dhg * y * _dsilu(z)).astype(dz_ref.dtype)
        dy = dhg * _silu(z)

        lane = lax.broadcasted_iota(jnp.int32, (n, LANES), 1)
        lane1 = lax.broadcasted_iota(jnp.int32, (1, LANES), 1)
        last_row = lax.broadcasted_iota(jnp.int32, (n, 1), 0) == n - 1
        dacum_col = jnp.zeros((n, LANES), F32)
        da_rowpart = jnp.zeros((n, LANES), F32)
        ddt = jnp.zeros((n, LANES), F32)
        dd_vec = jnp.zeros((1, LANES), F32)
        for g in range(2):
            dg = jnp.zeros((n, n), F32)
            dbm = jnp.zeros((n, SSD_STATE), F32)
            dcm = jnp.zeros((n, SSD_STATE), F32)
            for hd in heads[3 * g:3 * g + 3]:
                hh = hd["hh"]
                sl = slice(hh * HEAD_DIM, (hh + 1) * HEAD_DIM)
                dyh = dy[:, sl]
                dhn = dh_scr[sl, :]
                el = jnp.exp(hd["al"])
                dd_vec = dd_vec + jnp.where(lane1 == hh, jnp.sum(dyh * hd["xh"]), 0.0)
                dcm = dcm + hd["ea"] * _dot(dyh, hd["hp"])
                dm = _dot(dyh, hd["xdt"], NT)
                dg = dg + dm * hd["dec"]
                e = dm * hd["m"]
                t2 = _dot(hd["bm"], dhn, NT)
                dxdt = _dot(hd["m"], dyh, TN) + hd["w"] * t2
                dbm = dbm + hd["w"] * _dot(hd["xdt"], dhn)
                dw_w = jnp.sum(hd["xdt"] * t2, axis=1, keepdims=True) * hd["w"]
                d_el = jnp.sum(dhn * hd["hp"])
                col_part = (jnp.sum(dyh * hd["yo"], axis=1, keepdims=True) + jnp.sum(e, axis=1, keepdims=True) - dw_w
                            + jnp.where(last_row, d_el * el + jnp.sum(dw_w), 0.0))
                dacum_col = dacum_col + jnp.where(lane == hh, col_part, 0.0)
                neg_colsum = -jnp.sum(e, axis=0, keepdims=True)
                rev = jnp.sum(jnp.where(row <= col, neg_colsum, 0.0), axis=1, keepdims=True)
                da_rowpart = da_rowpart + jnp.where(lane == hh, rev, 0.0)
                dh_scr[sl, :] = el * dhn + _dot(dyh * hd["ea"], hd["cm"], TN)
                dx_scr[:, sl] = d_row[:, hh:hh + 1] * dyh + dxdt * hd["dth"]
                ddt = ddt + jnp.where(lane == hh, jnp.sum(dxdt * hd["xh"], axis=1, keepdims=True), 0.0)
            bm, cm = heads[3 * g]["bm"], heads[3 * g]["cm"]
            dcm = dcm + _dot(dg, bm)
            dbm = dbm + _dot(dg, cm, TN)
            dbc_scr[:, g * SSD_STATE:(g + 1) * SSD_STATE] = dbm
            dbc_scr[:, 2 * SSD_STATE + g * SSD_STATE:2 * SSD_STATE + (g + 1) * SSD_STATE] = dcm
        da_mat = _dot_exact01(dacum_col, upper, NN, x_left=False) + da_rowpart
        ddt = ddt + da_mat * a_row
        da_vec = jnp.sum(da_mat * dt, axis=0, keepdims=True)
        ddt_pre = jnp.where(lane < SSD_HEADS, ddt * _sigmoid(dt_pre), 0.0)
        ddt_ref[...] = ddt_pre.astype(ddt_ref.dtype)
        rid = lax.broadcasted_iota(jnp.int32, (SUBLANES, LANES), 0)
        gv_ref[...] += (jnp.where(rid == 0, jnp.sum(ddt_pre, axis=0, keepdims=True), 0.0)
                        + jnp.where(rid == 1, da_vec, 0.0) + jnp.where(rid == 2, dd_vec, 0.0))

        def conv_bwd(dpost, pre, w_ref, ext_ref, dext_ref, carry_ref, cur_ref, out_ref, g_ref, width):
            dco = dpost * _dsilu(pre)
            dext_ref[0:n, :] = dco
            dext_ref[n:n + SUBLANES, :] = carry_ref[...]
            out_ref[...] = (w_ref[3:4, :] * dco + w_ref[2:3, :] * dext_ref[pl.ds(1, n), :]
                            + w_ref[1:2, :] * dext_ref[pl.ds(2, n), :] + w_ref[0:1, :] * dext_ref[pl.ds(3, n), :]
                            ).astype(out_ref.dtype)
            carry_ref[...] = dco[0:SUBLANES, :]
            rid8 = lax.broadcasted_iota(jnp.int32, (SUBLANES, width), 0)
            acc = jnp.where(rid8 == 3, jnp.sum(dco * cur_ref[...], axis=0, keepdims=True), 0.0)
            for j in range(3):
                sh = ext_ref[pl.ds(SUBLANES - 3 + j, n), :]
                acc = acc + jnp.where(rid8 == j, jnp.sum(dco * sh, axis=0, keepdims=True), 0.0)
            acc = acc + jnp.where(rid8 == 4, jnp.sum(dco, axis=0, keepdims=True), 0.0)
            g_ref[...] += acc

        conv_bwd(dx_scr[...], pre_x, cwx_ref, xe_ref, dxe_ref, cx_ref, xs_ref, dxs_ref, gx_ref, 384)
        conv_bwd(dbc_scr[...], pre_bc, cwb_ref, be_ref, dbe_ref, cb_ref, bc_ref, dbc_ref, gb_ref, 512)

    t = nb * s
    rowblk = lambda width: pl.BlockSpec((n, width), lambda b, c: (b * nc + cidx(c), 0))
    return pl.pallas_call(
        body, name=name, grid=(nb, nc),
        in_specs=[blk(512, H_BC), halo(512, H_BC), blk(384, H_Z), blk(384, H_XS), halo(384, H_XS), blk(128, H_DT),
                  pl.BlockSpec((None, SSD_WIDTH, SSD_STATE), lambda b, c: (b * nc + cidx(c), 0, 0)),
                  pl.BlockSpec((n, SSD_WIDTH), lambda b, c: (b * nc + cidx(c), 0)),
                  full((4, 384)), full((1, 384)), full((4, 512)), full((1, 512)), full((SUBLANES, LANES)), full((1, 384))],
        out_specs=[rowblk(384), rowblk(384), rowblk(512), rowblk(128),
                   full((SUBLANES, 384)), full((SUBLANES, 512)), full((SUBLANES, LANES)), full((1, 384))],
        out_shape=[jax.ShapeDtypeStruct((t, 384), MXU_DTYPE), jax.ShapeDtypeStruct((t, 384), MXU_DTYPE),
                   jax.ShapeDtypeStruct((t, 512), MXU_DTYPE), jax.ShapeDtypeStruct((t, 128), MXU_DTYPE),
                   jax.ShapeDtypeStruct((SUBLANES, 384), F32), jax.ShapeDtypeStruct((SUBLANES, 512), F32),
                   jax.ShapeDtypeStruct((SUBLANES, LANES), F32), jax.ShapeDtypeStruct((1, 384), F32)],
        scratch_shapes=[pltpu.VMEM((SSD_WIDTH, SSD_STATE), F32), pltpu.VMEM((n + SUBLANES, 384), F32),
                        pltpu.VMEM((n + SUBLANES, 512), F32), pltpu.VMEM((n, SSD_WIDTH), F32),
                        pltpu.VMEM((n, 384), F32), pltpu.VMEM((n, 512), F32),
                        pltpu.VMEM((n + SUBLANES, 384), F32), pltpu.VMEM((n + SUBLANES, 512), F32),
                        pltpu.VMEM((SUBLANES, 384), F32), pltpu.VMEM((SUBLANES, 512), F32)],
        compiler_params=_params(2),
    )(h, h, h, h, h, h, hstate, dmix, cwx, cbx, cwb, cbb, vec, nw)


def _adamw_math(w, g, m, v):
    m = ADAM_B1 * m + (1.0 - ADAM_B1) * g
    v = ADAM_B2 * v + (1.0 - ADAM_B2) * (g * g)
    m_hat = m / (1.0 - ADAM_B1 ** ADAM_STEP)
    v_hat = v / (1.0 - ADAM_B2 ** ADAM_STEP)
    delta = -ADAM_LR * (m_hat / (jnp.sqrt(v_hat) + ADAM_EPS) + ADAM_WD * w)
    return delta, m, v


def _adamw(w, g, m, v, name, tr=256):
    rows, cols = w.shape
    tr = rows if rows <= tr else tr
    assert rows % tr == 0, (rows, tr)

    def body(w_ref, g_ref, m_ref, v_ref, d_ref, nm_ref, nv_ref):
        d, nm, nv = _adamw_math(w_ref[...], g_ref[...], m_ref[...], v_ref[...])
        d_ref[...] = d
        nm_ref[...] = nm
        nv_ref[...] = nv

    spec = pl.BlockSpec((tr, cols), lambda i: (i, 0))
    shp = jax.ShapeDtypeStruct((rows, cols), F32)
    return pl.pallas_call(body, name=name, grid=(rows // tr,), in_specs=[spec] * 4, out_specs=[spec] * 3,
                          out_shape=[shp] * 3, compiler_params=_params(1))(w, g, m, v)


def _sum8_layers(parts, name, tr):
    _, rows, cols = parts[0].shape
    assert rows % tr == 0
    nt = rows // tr

    def body(*refs):
        o_ref = refs[DEPTH]
        layer = pl.program_id(0)
        for l in range(DEPTH):
            @pl.when(layer == l)
            def _(l=l):
                acc = refs[l][0]
                for k in range(1, N_DEV):
                    acc = acc + refs[l][k]
                o_ref[...] = acc

    in_specs = [pl.BlockSpec((N_DEV, tr, cols), lambda a, i, l=l: (0, jnp.clip(i + (a - l) * nt, 0, nt - 1), 0))
                for l in range(DEPTH)]
    return pl.pallas_call(body, name=name, grid=(DEPTH, nt), in_specs=in_specs,
                          out_specs=pl.BlockSpec((None, tr, cols), lambda a, i: (a, i, 0)),
                          out_shape=jax.ShapeDtypeStruct((DEPTH, rows, cols), F32), compiler_params=_params(2))(*parts)


def _all_reduce_small(vec, name):
    rows, cols = vec.shape

    def body(x_ref, out_ref, gbuf, send_sems, recv_sems):
        x, y, c = lax.axis_index("x"), lax.axis_index("y"), lax.axis_index("c")
        me, sibling = (x, y, c), (x, y, 1 - c)
        chips = [(1 - x, y), (x, 1 - y), (1 - x, 1 - y)]

        def slot(px, py, pc):
            return gbuf.at[4 * px + 2 * py + pc]

        def copy(k, block, to, src=None):
            return pltpu.make_async_remote_copy(
                src_ref=slot(*block) if src is None else src, dst_ref=slot(*block),
                send_sem=send_sems.at[k], recv_sem=recv_sems.at[k], device_id=to, device_id_type=MESH_ID)

        first = [copy(0, me, sibling, src=x_ref)]
        first += [copy(1 + j, me, (*chip, c), src=x_ref) for j, chip in enumerate(chips)]
        for cp in first:
            cp.start()
        gbuf[4 * x + 2 * y + c] = x_ref[...]
        passed = [copy(4 + j, (*chip, c), sibling) for j, chip in enumerate(chips)]
        for j, chip in enumerate(chips):
            copy(1 + j, (*chip, c), me).wait_recv()
            passed[j].start()
        copy(0, sibling, me).wait_recv()
        for j, chip in enumerate(chips):
            copy(4 + j, (*chip, 1 - c), me).wait_recv()
        for cp in first + passed:
            cp.wait_send()
        acc = gbuf[0]
        for k in range(1, N_DEV):
            acc = acc + gbuf[k]
        out_ref[...] = acc

    return pl.pallas_call(
        body, name=name, out_shape=jax.ShapeDtypeStruct((rows, cols), F32),
        in_specs=[pl.BlockSpec(memory_space=pltpu.VMEM)], out_specs=pl.BlockSpec(memory_space=pltpu.VMEM),
        scratch_shapes=[pltpu.VMEM((N_DEV, rows, cols), F32), pltpu.SemaphoreType.DMA((7,)), pltpu.SemaphoreType.DMA((7,))],
        compiler_params=pltpu.CompilerParams(has_side_effects=True, vmem_limit_bytes=VMEM_LIMIT_BYTES),
    )(vec)


_COL_POOL, _COL_Z, _COL_XBC, _COL_DT, _COL_Q, _COL_K, _COL_V = 0, 256, 640, 1536, 1542, 1926, 2310
_H_SEGMENTS = ((_COL_XBC + SSD_WIDTH, 512), (_COL_POOL, 256), (_COL_Q, 384), (_COL_K, 384), (_COL_V, 384),
               (_COL_Z, 384), (_COL_XBC, 384), (_COL_DT, 6))


def _h_from_orig(w):
    parts = [w[..., o:o + n] for o, n in _H_SEGMENTS]
    pad = jnp.zeros(w.shape[:-1] + (H_COLS - IN_COLS,), w.dtype)
    return jnp.concatenate(parts + [pad], axis=-1)


def _h_to_orig(w):
    offs, o = {}, 0
    for orig, n in _H_SEGMENTS:
        offs[orig] = (o, n)
        o += n
    order = sorted(offs)
    return jnp.concatenate([w[..., offs[k][0]:offs[k][0] + offs[k][1]] for k in order], axis=-1)


def _interleave(w):
    lead = w.shape[:-1]
    nt = D_FF // GLU_TILE
    return jnp.swapaxes(w.reshape(lead + (2, nt, GLU_TILE)), -3, -2).reshape(lead + (2 * D_FF,))


def _deinterleave(w):
    lead = w.shape[:-1]
    nt = D_FF // GLU_TILE
    return jnp.swapaxes(w.reshape(lead + (nt, 2, GLU_TILE)), -3, -2).reshape(lead + (2 * D_FF,))


def _up_segments():
    segs = []
    for j in range(N_DEV):
        half, base = j // 4, 704 * (j % 4)
        c = base
        while c < base + 704:
            t, r = divmod(c, GLU_TILE)
            n = min(GLU_TILE - r, base + 704 - c)
            segs.append((j, c - base, 2 * GLU_TILE * t + GLU_TILE * half + r, n))
            c += n
    return segs


def _up_to_interleaved(w, name, tr=256):
    def body(i_ref, o_ref):
        for j, src, dst, n in _up_segments():
            o_ref[:, dst:dst + n] = i_ref[j, :, src:src + n]

    return pl.pallas_call(
        body, name=name, grid=(D_MODEL // tr,), in_specs=[pl.BlockSpec((N_DEV, tr, 704), lambda r: (0, r, 0))],
        out_specs=pl.BlockSpec((tr, 2 * D_FF), lambda r: (r, 0)),
        out_shape=jax.ShapeDtypeStruct((D_MODEL, 2 * D_FF), w.dtype), compiler_params=_params(1))(w)


def _up_from_interleaved(g, name, tr=128):
    def body(i_ref, o_ref):
        for j, src, dst, n in _up_segments():
            o_ref[j, :, src:src + n] = i_ref[:, dst:dst + n]

    return pl.pallas_call(
        body, name=name, grid=(D_MODEL // tr,), in_specs=[pl.BlockSpec((tr, 2 * D_FF), lambda r: (r, 0))],
        out_specs=pl.BlockSpec((N_DEV, tr, 704), lambda r: (0, r, 0)),
        out_shape=jax.ShapeDtypeStruct((N_DEV, D_MODEL, 704), g.dtype), compiler_params=_params(1))(g)


def _mix_rows_from_orig(w):
    return jnp.concatenate([w[256:640], w[640:1024], w[0:256]], axis=0)


def _mix_rows_to_orig(w):
    return jnp.concatenate([w[768:1024], w[0:384], w[384:768]], axis=0)


def _xbc_split(w):
    return w[..., :SSD_WIDTH], w[..., SSD_WIDTH:]


def _layer_fwd(x, p_l, wt, sp, nb, s, comm=None):
    h = _mm(x, wt["w_in"], "nn", F32, "mm_in", tm=1024, tn=1408)
    pool_out = _pool_fwd(h, wt["pool_bd"], sp["pool_scale"], nb, s, "pool_fwd")
    ssd_out, hstate = _ssd_fwd(h, sp["cwx"], sp["cbx"], sp["cwb"], sp["cbb"], sp["ssd_vec"], sp["ssd_norm_w"], nb, s, "ssd_fwd")
    sb_out, comm_out = _sb_fwd(h, nb, s, "sb_fwd" if comm is None else "sb_fwd_gather", comm)
    mixcat = jnp.concatenate([ssd_out, sb_out, pool_out], axis=1)
    mix = _mm(mixcat, wt["w_out"], "nn", F32, "mm_out", tm=1024, tn=1024)
    x1, r1 = _ln_fwd(x, mix, sp["ln1"], "ln1_fwd")
    up = _mm(x1, wt["w_up"], "nn", F32, "mm_up", tm=1024, tn=1408)
    act = _glu_fwd(up, sp["ffn_cw"], sp["ffn_cb"], nb, s, "glu_fwd")
    ffn = _mm(act, wt["w_down"], "nn", F32, "mm_down", tm=1024, tn=1024, tk=1408)
    gp = _mm(x1, wt["w_gate"], "nn", F32, "mm_gate", tm=1024, tn=1024)
    pp = _mm(p_l, wt["w_proj"], "nn", F32, "mm_proj", tm=2048, tn=1024)
    x2, r2 = _ln_fwd(x1, ffn, sp["ln2"], "ln2_fwd", gp=gp, pp=pp)
    return x2, dict(x=x, h=h, hstate=hstate, mixcat=mixcat, r1=r1, x1=x1, up=up, act=act, gp=gp, pp=pp, r2=r2), comm_out


def _layer_bwd(dx2, p_l, sv, wt, sp, nb, s, comm=None):
    dr2, dgp, dpp, st2 = _ln_bwd(sv["r2"], sp["ln2"], dx2, "ln2_bwd", gp=sv["gp"], pp=sv["pp"])
    g_down = _mm(sv["act"], dr2, "tn", F32, "wg_down", tm=1408, tn=1024, tk=512)
    dact = _mm(dr2, wt["w_down"], "nt", F32, "dg_down", tm=1024, tn=1408)
    dup, ffn_acc = _glu_bwd(sv["up"], dact, sp["ffn_cw"], sp["ffn_cb"], nb, s, "glu_bwd")
    g_up = _mm(sv["x1"], dup, "tn", F32, "wg_up", tm=1024, tn=2816, tk=512)
    g_gate = _mm(sv["x1"], dgp, "tn", F32, "wg_gate", tm=1024, tn=1024, tk=512)
    g_proj = _mm(p_l, dpp, "tn", F32, "wg_proj", tm=256, tn=1024, tk=512)
    t1 = _mm(dgp, wt["w_gate"], "nt", F32, "dg_gate", tm=1024, tn=1024, add=dr2, add_coef=ALPHA)
    dx1 = _mm(dup, wt["w_up"], "nt", F32, "dg_up", tm=1024, tn=1024, tk=1408, add=t1)
    dr1, st1 = _ln_bwd(sv["r1"], sp["ln1"], dx1, "ln1_bwd")
    g_out = _mm(sv["mixcat"], dr1, "tn", F32, "wg_out", tm=1024, tn=1024, tk=512)
    dmix = _mm(dr1, wt["w_out"], "nt", F32, "dg_out", tm=1024, tn=1024)
    du, g_pool_bd, g_pool_scale = _pool_bwd(sv["h"], dmix, wt["pool_bd"], sp["pool_scale"], nb, s, "pool_bwd")
    dz, dxs, dbc, ddt, gx, gb, gv, gn = _ssd_bwd(sv["h"], sv["hstate"], dmix, sp["cwx"], sp["cbx"], sp["cwb"], sp["cbb"],
                                                  sp["ssd_vec"], sp["ssd_norm_w"], nb, s, "ssd_bwd")
    ready = dict(w_out=g_out, ffn_w_up=g_up, ffn_w_down=g_down, ple_w_gate=g_gate, ple_w_proj=g_proj)
    job = comm(ready) if comm is not None else None
    dq, dk, dv, comm_out = _sb_bwd(sv["h"], dmix, nb, s, "sb_bwd" if job is None else "sb_bwd_x%d" % job["n_xfers"], job)
    dh = jnp.concatenate([dbc, du, dq, dk, dv, dz, dxs, ddt], axis=1)
    g_in = _mm(sv["x"], dh, "tn", F32, "wg_in", tm=1024, tn=2816, tk=512)
    dx = _mm(dh, wt["w_in"], "nt", F32, "dg_in", tm=1024, tn=1024, tk=1408, add=dr1, add_coef=ALPHA)
    small = dict(
        pool_w=jnp.stack([g_pool_bd[HEAD_DIM * g:HEAD_DIM * (g + 1), HEAD_DIM * g:HEAD_DIM * (g + 1)] for g in range(4)]),
        pool_scale=g_pool_scale[0],
        ssd_conv_w=jnp.concatenate([gx[0:4], gb[0:4]], axis=1),
        ssd_conv_b=jnp.concatenate([gx[4], gb[4]], axis=0),
        ssd_dt_bias=gv[0, :SSD_HEADS],
        ssd_a_log=gv[1, :SSD_HEADS] * sp["ssd_vec"][1, :SSD_HEADS],
        ssd_d=gv[2, :SSD_HEADS],
        ssd_norm_w=gn[0],
        ln1_g=st1[0], ln1_b=st1[1], ln2_g=st2[0], ln2_b=st2[1],
        ffn_conv_w=_deinterleave(ffn_acc[0:3]),
        ffn_conv_b=_deinterleave(ffn_acc[3]),
    )
    return dx, dict(ready, w_in=g_in), small, comm_out


def _layer_params(i, big, rep):
    pool_bd = jnp.zeros((POOL_WIDTH, POOL_WIDTH), F32)
    for g in range(4):
        pool_bd = lax.dynamic_update_slice(pool_bd, rep["pool_w"][i, g], (HEAD_DIM * g, HEAD_DIM * g))
    wt = dict(w_in=big["w_in"], w_out=big["w_out"], w_up=big["ffn_w_up"], w_down=big["ffn_w_down"],
              w_gate=big["ple_w_gate"], w_proj=big["ple_w_proj"], pool_bd=pool_bd.astype(MXU_DTYPE))
    cwx, cwb = _xbc_split(rep["ssd_conv_w"][i])
    cbx, cbb = _xbc_split(rep["ssd_conv_b"][i][None, :])
    vec = jnp.zeros((SUBLANES, LANES), F32)
    vec = vec.at[0, :SSD_HEADS].set(rep["ssd_dt_bias"][i])
    vec = vec.at[1, :SSD_HEADS].set(-jnp.exp(rep["ssd_a_log"][i]))
    vec = vec.at[2, :SSD_HEADS].set(rep["ssd_d"][i])
    sp = dict(pool_scale=rep["pool_scale"][i][None, :], cwx=cwx, cbx=cbx, cwb=cwb, cbb=cbb, ssd_vec=vec,
              ssd_norm_w=rep["ssd_norm_w"][i][None, :],
              ln1=jnp.stack([rep["ln1_g"][i], rep["ln1_b"][i]]), ln2=jnp.stack([rep["ln2_g"][i], rep["ln2_b"][i]]),
              ffn_cw=_interleave(rep["ffn_conv_w"][i]), ffn_cb=_interleave(rep["ffn_conv_b"][i][None, :]))
    return wt, sp


def _run_layers(x, p, target, big_w, rep, fwd_job=None, fwd_done=None, bwd_job=None, bwd_done=None):
    nb, s, d = x.shape
    t = nb * s
    xf = x.reshape(t, d)
    saved, params = [], []
    for i in range(DEPTH):
        wt, sp = _layer_params(i, big_w[i], rep)
        params.append((wt, sp))
        job = fwd_job(i) if fwd_job is not None else None
        xf, sv, res = _layer_fwd(xf, p[i].reshape(t, PLE_DIM), wt, sp, nb, s, job)
        if job is not None:
            fwd_done(i, res)
        saved.append(sv)
    dy, loss = _loss_grad(xf, target.reshape(t, d), "loss")
    bigs, smalls = [None] * DEPTH, [None] * DEPTH
    for i in reversed(range(DEPTH)):
        wt, sp = params[i]
        job = (lambda ready, i=i: bwd_job(i, bigs, ready)) if bwd_job is not None else None
        dy, bigs[i], smalls[i], res = _layer_bwd(dy, p[i].reshape(t, PLE_DIM), saved[i], wt, sp, nb, s, job)
        if job is not None:
            bwd_done(i, res)
    return loss, dy.reshape(nb, s, d), bigs, smalls


def _local_step(x, p, target, full, rep):
    return _run_layers(x, p, target, [{n: full[n][i] for n in full} for i in range(DEPTH)], rep)


BIG = ("w_in", "w_out", "ffn_w_up", "ffn_w_down", "ple_w_gate", "ple_w_proj")
SMALL_REPLICATED = ("pool_w", "pool_scale", "ssd_conv_b", "ssd_dt_bias", "ssd_a_log", "ssd_d", "ssd_norm_w",
                    "ln1_g", "ln1_b", "ffn_conv_b", "ln2_g", "ln2_b")
SMALL_SHARDED = ("ssd_conv_w", "ffn_conv_w")
WEIGHTS = ("w_in", "pool_w", "pool_scale", "ssd_conv_w", "ssd_conv_b", "ssd_dt_bias", "ssd_a_log", "ssd_d", "ssd_norm_w",
           "w_out", "ln1_g", "ln1_b", "ffn_w_up", "ffn_conv_w", "ffn_conv_b", "ffn_w_down", "ln2_g", "ln2_b",
           "ple_w_gate", "ple_w_proj")
SUM_BLOCK_BYTES = 3 * 512 * 1024


def _to_rows(a, cols):
    f = a.reshape(-1)
    pad = (-f.shape[0]) % cols
    if pad:
        f = jnp.concatenate([f, jnp.zeros((pad,), f.dtype)])
    return f.reshape(-1, cols)


def _pack_rows(arrs, cols, row_mult):
    rows = [_to_rows(a, cols) for a in arrs]
    flat = jnp.concatenate(rows, axis=0)
    pad = (-flat.shape[0]) % row_mult
    if pad:
        flat = jnp.concatenate([flat, jnp.zeros((pad, cols), flat.dtype)], axis=0)
    return flat


def _unpack_rows(flat, shapes, cols):
    out, r = [], 0
    for shp in shapes:
        n = 1
        for v in shp:
            n *= v
        nr = -(-n // cols)
        out.append(flat[r:r + nr].reshape(-1)[:n].reshape(shp))
        r += nr
    return out


def kernel(x, p, w_in, pool_w, pool_scale, ssd_conv_w, ssd_conv_b, ssd_dt_bias, ssd_a_log, ssd_d, ssd_norm_w, w_out, ln1_g, ln1_b, ffn_w_up, ffn_conv_w, ffn_conv_b, ffn_w_down, ln2_g, ln2_b, ple_w_gate, ple_w_proj, loss_target, m_w_in, m_pool_w, m_pool_scale, m_ssd_conv_w, m_ssd_conv_b, m_ssd_dt_bias, m_ssd_a_log, m_ssd_d, m_ssd_norm_w, m_w_out, m_ln1_g, m_ln1_b, m_ffn_w_up, m_ffn_conv_w, m_ffn_conv_b, m_ffn_w_down, m_ln2_g, m_ln2_b, m_ple_w_gate, m_ple_w_proj, v_w_in, v_pool_w, v_pool_scale, v_ssd_conv_w, v_ssd_conv_b, v_ssd_dt_bias, v_ssd_a_log, v_ssd_d, v_ssd_norm_w, v_w_out, v_ln1_g, v_ln1_b, v_ffn_w_up, v_ffn_conv_w, v_ffn_conv_b, v_ffn_w_down, v_ln2_g, v_ln2_b, v_ple_w_gate, v_ple_w_proj):
    wts = dict(w_in=w_in, pool_w=pool_w, pool_scale=pool_scale, ssd_conv_w=ssd_conv_w, ssd_conv_b=ssd_conv_b,
               ssd_dt_bias=ssd_dt_bias, ssd_a_log=ssd_a_log, ssd_d=ssd_d, ssd_norm_w=ssd_norm_w, w_out=w_out, ln1_g=ln1_g,
               ln1_b=ln1_b, ffn_w_up=ffn_w_up, ffn_conv_w=ffn_conv_w, ffn_conv_b=ffn_conv_b, ffn_w_down=ffn_w_down,
               ln2_g=ln2_g, ln2_b=ln2_b, ple_w_gate=ple_w_gate, ple_w_proj=ple_w_proj)
    mom_m = dict(w_in=m_w_in, pool_w=m_pool_w, pool_scale=m_pool_scale, ssd_conv_w=m_ssd_conv_w, ssd_conv_b=m_ssd_conv_b,
                 ssd_dt_bias=m_ssd_dt_bias, ssd_a_log=m_ssd_a_log, ssd_d=m_ssd_d, ssd_norm_w=m_ssd_norm_w, w_out=m_w_out,
                 ln1_g=m_ln1_g, ln1_b=m_ln1_b, ffn_w_up=m_ffn_w_up, ffn_conv_w=m_ffn_conv_w, ffn_conv_b=m_ffn_conv_b,
                 ffn_w_down=m_ffn_w_down, ln2_g=m_ln2_g, ln2_b=m_ln2_b, ple_w_gate=m_ple_w_gate, ple_w_proj=m_ple_w_proj)
    mom_v = dict(w_in=v_w_in, pool_w=v_pool_w, pool_scale=v_pool_scale, ssd_conv_w=v_ssd_conv_w, ssd_conv_b=v_ssd_conv_b,
                 ssd_dt_bias=v_ssd_dt_bias, ssd_a_log=v_ssd_a_log, ssd_d=v_ssd_d, ssd_norm_w=v_ssd_norm_w, w_out=v_w_out,
                 ln1_g=v_ln1_g, ln1_b=v_ln1_b, ffn_w_up=v_ffn_w_up, ffn_conv_w=v_ffn_conv_w, ffn_conv_b=v_ffn_conv_b,
                 ffn_w_down=v_ffn_w_down, ln2_g=v_ln2_g, ln2_b=v_ln2_b, ple_w_gate=v_ple_w_gate, ple_w_proj=v_ple_w_proj)
    me = 4 * lax.axis_index("x") + 2 * lax.axis_index("y") + lax.axis_index("c")

    def layer_shards(i):
        sh = {n: wts[n][i].astype(MXU_DTYPE) for n in BIG}
        sh["w_in"] = _h_from_orig(wts["w_in"][i]).astype(MXU_DTYPE)
        return sh

    def gathered_weights(res):
        big = dict(zip(BIG, res[:len(BIG)]))
        big["ffn_w_up"] = _up_to_interleaved(big["ffn_w_up"], "up_to_interleaved")
        return big

    res0 = _comm_call(_gather_job(layer_shards(0), [wts[n] for n in SMALL_SHARDED]), "gather_layer0")
    big_w = [gathered_weights(res0)] + [None] * (DEPTH - 1)
    rep = {n: wts[n] for n in SMALL_REPLICATED}
    for n, g in zip(SMALL_SHARDED, res0[len(BIG):]):
        rep[n] = jnp.transpose(g, (1, 2, 0, 3)).reshape(g.shape[1], g.shape[2], N_DEV * g.shape[3])

    def fwd_job(i):
        return _gather_job(layer_shards(i + 1)) if i + 1 < DEPTH else None

    def fwd_done(i, res):
        big_w[i + 1] = gathered_weights(res)

    received = [dict() for _ in range(DEPTH)]
    carried = ("w_out", "ffn_w_up", "ffn_w_down", "ple_w_gate", "ple_w_proj")

    def bwd_items(i, bigs, ready):
        items = [(i, n, ready[n]) for n in carried] + ([(i + 1, "w_in", bigs[i + 1]["w_in"])] if i + 1 < DEPTH else [])
        return [(l, n, _up_from_interleaved(g, "up_from_interleaved") if n == "ffn_w_up" else g) for l, n, g in items]

    pending = {}

    def bwd_job(i, bigs, ready):
        pending[i] = bwd_items(i, bigs, ready)
        return _exchange_job([(n, g) for _, n, g in pending[i]])

    def bwd_done(i, res):
        for (l, n, _), r in zip(pending[i], res):
            received[l][n] = r

    loss_loc, grad_x, bigs, smalls = _run_layers(x, p, loss_target, big_w, rep, fwd_job, fwd_done, bwd_job, bwd_done)
    received[0]["w_in"] = _comm_call(_exchange_job([("w_in", bigs[0]["w_in"])]), "exchange_w_in0")[0]

    grads = {}
    for n in BIG:
        parts = [received[i][n] for i in range(DEPTH)]
        _, rows, cols = parts[0].shape
        tr = next(t for t in (256, 128, 64, 32, 16, 8) if rows % t == 0 and N_DEV * t * cols * 4 <= SUM_BLOCK_BYTES)
        g = _sum8_layers(parts, "sum_" + n, tr)
        grads[n] = _h_to_orig(g) if n == "w_in" else g
    small_names = SMALL_REPLICATED + SMALL_SHARDED
    small_full_shapes = [rep[n].shape for n in small_names]
    small_vec = _pack_rows([jnp.stack([smalls[i][n] for i in range(DEPTH)]) for n in small_names] + [loss_loc[0, :1]],
                           LANES, SUBLANES)
    small_sum = _all_reduce_small(small_vec, "allreduce_small")
    small_out = _unpack_rows(small_sum, small_full_shapes + [(1,)], LANES)
    loss = small_out[-1][0]
    for n, g in zip(small_names, small_out[:-1]):
        if n in SMALL_SHARDED:
            width = wts[n].shape[-1]
            g = lax.dynamic_slice_in_dim(g, me * width, width, axis=g.ndim - 1)
        grads[n] = g

    delta, new_m, new_v = {}, {}, {}
    for n in BIG:
        shp = wts[n].shape
        two_d = lambda a: a.reshape(-1, shp[-1])
        tr = {"w_in": 128, "ffn_w_down": 352}.get(n, 256)
        d_, m_, v_ = _adamw(two_d(wts[n]), two_d(grads[n]), two_d(mom_m[n]), two_d(mom_v[n]), "adamw_" + n, tr=tr)
        delta[n], new_m[n], new_v[n] = d_.reshape(shp), m_.reshape(shp), v_.reshape(shp)
    packs = [_pack_rows([src[n] for n in small_names], LANES, SUBLANES) for src in (wts, grads, mom_m, mom_v)]
    outs = _adamw(*packs, "adamw_small", tr=packs[0].shape[0])
    shapes = [wts[n].shape for n in small_names]
    for dst, flat in zip((delta, new_m, new_v), outs):
        for n, a in zip(small_names, _unpack_rows(flat, shapes, LANES)):
            dst[n] = a
    return (loss, grad_x, *[grads[n] for n in WEIGHTS], *[delta[n] for n in WEIGHTS],
            *[new_m[n] for n in WEIGHTS], *[new_v[n] for n in WEIGHTS])
```

```python
import functools

import jax
import jax.numpy as jnp
from jax import lax
from jax.experimental import pallas as pl
from jax.experimental.pallas import tpu as pltpu

F32 = jnp.float32
BF16 = jnp.bfloat16
MXU_DTYPE = jnp.bfloat16

D_MODEL = 1024
DEPTH = 4
PLE_DIM = 256
ALPHA = (2 * DEPTH) ** 0.25
LN_EPS = 1e-5
RMS_EPS = 1e-6
HEAD_DIM = 64
POOL_WIDTH = 256
POOL_WINDOWS = (2, 4, 8, 16)
SSD_WIDTH = 384
SSD_HEADS = 6
SSD_STATE = 128
SSD_XBC = 896
SB_WIDTH = 384
IN_COLS = 2694
D_FF = 2816
N_DEV = 8

ADAM_LR = 0.001
ADAM_B1 = 0.9
ADAM_B2 = 0.999
ADAM_EPS = 1e-08
ADAM_WD = 0.01
ADAM_STEP = 10

LANES = 128
SUBLANES = 8
VMEM_LIMIT_BYTES = 56 * 1024 * 1024

H_COLS = 2816
H_BC = 0
H_POOL = 512
H_Q = 768
H_K = 1152
H_V = 1536
H_Z = 1920
H_XS = 2304
H_DT = 2688
SSD_CHUNK = 256
QB = 256
GLU_TILE = 256
MASKED_LOG = -1e30

NN = ((1,), (0,))
NT = ((1,), (1,))
TN = ((0,), (0,))


def _dot(a, b, dims=NN):
    return lax.dot_general(a.astype(MXU_DTYPE), b.astype(MXU_DTYPE), (dims, ((), ())), preferred_element_type=F32)


def _dot_exact01(x, m01, dims=NN, x_left=True, terms=3):
    acc = None
    r = x
    for _ in range(terms):
        hi = r.astype(BF16)
        ops = (hi, m01) if x_left else (m01, hi)
        part = lax.dot_general(ops[0], ops[1], (dims, ((), ())), preferred_element_type=F32)
        acc = part if acc is None else acc + part
        r = r - hi.astype(F32)
    return acc


def _sigmoid(v):
    return 1.0 / (1.0 + jnp.exp(-v))


def _silu(v):
    return v * _sigmoid(v)


def _dsilu(v):
    s = _sigmoid(v)
    return s * (1.0 + v * (1.0 - s))


def _softplus(v):
    return jnp.maximum(v, 0.0) + jnp.log(1.0 + jnp.exp(-jnp.abs(v)))


def _params(n_axes, side_effects=False):
    return pltpu.CompilerParams(dimension_semantics=("arbitrary",) * n_axes, vmem_limit_bytes=VMEM_LIMIT_BYTES,
                                has_side_effects=side_effects)


MESH_ID = pl.DeviceIdType.MESH
_ANY = pl.BlockSpec(memory_space=pl.ANY)


def _flip(v, bit):
    return 1 - v if bit else v


def _comm_counts(comm):
    return (0, 0) if comm is None else (len(comm["inputs"]), len(comm["out_shapes"]))


def _comm_call_args(comm):
    if comm is None:
        return [], [], [], []
    n = comm["n_xfers"]
    sems = [pltpu.SemaphoreType.DMA(((N_DEV - 1) * n,)), pltpu.SemaphoreType.DMA(((N_DEV - 1) * n,)),
            pltpu.SemaphoreType.DMA((n,))]
    return list(comm["inputs"]), [_ANY] * len(comm["out_shapes"]), list(comm["out_shapes"]), sems


def _comm_descs(comm, in_refs, tail_refs, with_recvs=True):
    n_out = len(comm["out_shapes"])
    out_refs, (send_sems, recv_sems, local_sems) = tail_refs[:n_out], tail_refs[n_out:n_out + 3]
    xfers = comm["xfers"](in_refs, out_refs)
    n = len(xfers)
    assert n == comm["n_xfers"]
    x, y, c = lax.axis_index("x"), lax.axis_index("y"), lax.axis_index("c")
    me = 4 * x + 2 * y + c
    local = [pltpu.make_async_copy(src_for(me), dst_for(me), local_sems.at[t]) for t, (src_for, dst_for) in enumerate(xfers)]
    sends, recvs = [], []
    for k in range(1, N_DEV):
        pid = (_flip(x, k & 4), _flip(y, k & 2), _flip(c, k & 1))
        peer = 4 * pid[0] + 2 * pid[1] + pid[2]
        for t, (src_for, dst_for) in enumerate(xfers):
            idx = (k - 1) * n + t
            sends.append(pltpu.make_async_remote_copy(
                src_ref=src_for(peer), dst_ref=dst_for(me), send_sem=send_sems.at[idx], recv_sem=recv_sems.at[idx],
                device_id=pid, device_id_type=MESH_ID))
            if with_recvs:
                recvs.append(pltpu.make_async_remote_copy(
                    src_ref=src_for(peer), dst_ref=dst_for(peer), send_sem=send_sems.at[idx], recv_sem=recv_sems.at[idx],
                    device_id=pid, device_id_type=MESH_ID))
    return local, sends, recvs


def _comm_start(descs):
    local, sends, _ = descs
    for cp in local + sends:
        cp.start()


def _comm_wait(descs):
    local, sends, recvs = descs
    for cp in recvs:
        cp.wait_recv()
    for cp in sends:
        cp.wait_send()
    for cp in local:
        cp.wait()


def _comm_hosted(comm, in_refs, tail_refs, grid):
    if comm is None:
        return
    ids = [pl.program_id(a) for a in range(len(grid))]
    first = functools.reduce(jnp.logical_and, [i == 0 for i in ids])
    last = functools.reduce(jnp.logical_and, [i == g - 1 for i, g in zip(ids, grid)])

    @pl.when(first)
    def _():
        _comm_start(_comm_descs(comm, in_refs, tail_refs, with_recvs=False))

    @pl.when(last)
    def _():
        _comm_wait(_comm_descs(comm, in_refs, tail_refs))


def _comm_call(comm, name):
    n_in = len(comm["inputs"])

    def body(*refs):
        descs = _comm_descs(comm, refs[:n_in], refs[n_in:])
        _comm_start(descs)
        _comm_wait(descs)

    c_in, c_specs, c_shapes, c_scratch = _comm_call_args(comm)
    return pl.pallas_call(body, name=name, in_specs=[_ANY] * n_in, out_specs=c_specs, out_shape=c_shapes,
                          scratch_shapes=c_scratch, compiler_params=pltpu.CompilerParams(has_side_effects=True))(*c_in)


def _rows(ref, j, n):
    return ref.at[pl.ds(pl.multiple_of(j * n, SUBLANES), n), :]


def _gather_job(sh, conv=None):
    conv = list(conv or [])
    sds = jax.ShapeDtypeStruct
    out_shapes = [sds((D_MODEL, H_COLS), MXU_DTYPE), sds((D_MODEL, D_MODEL), MXU_DTYPE), sds((N_DEV, D_MODEL, 704), MXU_DTYPE),
                  sds((D_FF, D_MODEL), MXU_DTYPE), sds((D_MODEL, D_MODEL), MXU_DTYPE), sds((PLE_DIM, D_MODEL), MXU_DTYPE)]
    out_shapes += [sds((N_DEV,) + a.shape, a.dtype) for a in conv]

    def xfers(ins, outs):
        whole = lambda a: (lambda j: a)
        r = [(whole(ins[0]), lambda j: _rows(outs[0], j, 128)),
             (whole(ins[1]), lambda j: _rows(outs[1], lax.rem(j + 6, N_DEV), 128)),
             (whole(ins[2]), lambda j: outs[2].at[j]),
             (whole(ins[3]), lambda j: _rows(outs[3], j, 352)),
             (whole(ins[4]), lambda j: _rows(outs[4], j, 128)),
             (whole(ins[5]), lambda j: outs[5].at[:, pl.ds(pl.multiple_of(j * LANES, LANES), LANES)])]
        for t in range(len(conv)):
            r.append((whole(ins[6 + t]), lambda j, o=outs[6 + t]: o.at[j]))
        return r

    return dict(inputs=[sh[n] for n in BIG] + conv, out_shapes=out_shapes, xfers=xfers, n_xfers=6 + len(conv))


_SHARD_SHAPES = {"w_in": (128, H_COLS), "w_out": (128, D_MODEL), "ffn_w_up": (D_MODEL, 704), "ffn_w_down": (352, D_MODEL),
                 "ple_w_gate": (128, D_MODEL), "ple_w_proj": (PLE_DIM, LANES)}


def _exchange_job(items):
    def source(name, ref):
        if name in ("w_in", "ple_w_gate"):
            return lambda j: _rows(ref, j, 128)
        if name == "w_out":
            return lambda j: _rows(ref, lax.rem(j + 6, N_DEV), 128)
        if name == "ffn_w_up":
            return lambda j: ref.at[j]
        if name == "ffn_w_down":
            return lambda j: _rows(ref, j, 352)
        assert name == "ple_w_proj"
        return lambda j: ref.at[:, pl.ds(pl.multiple_of(j * LANES, LANES), LANES)]

    def xfers(ins, outs):
        return [(source(name, i), lambda j, o=o: o.at[j]) for (name, _), i, o in zip(items, ins, outs)]

    return dict(inputs=[g for _, g in items], xfers=xfers, n_xfers=len(items),
                out_shapes=[jax.ShapeDtypeStruct((N_DEV,) + _SHARD_SHAPES[name], F32) for name, _ in items])


def _pick(n, pref):
    if n <= pref:
        return n
    for t in range(pref - pref % LANES, 0, -LANES):
        if n % t == 0:
            return t
    raise ValueError((n, pref))


def _mm(a, b, mode, out_dtype, name, tm=512, tn=512, tk=1024, add=None, add_coef=1.0):
    if mode == "nn":
        (m, k), (k2, n) = a.shape, b.shape
    elif mode == "nt":
        (m, k), (n, k2) = a.shape, b.shape
    else:
        (k, m), (k2, n) = a.shape, b.shape
    assert k == k2, (a.shape, b.shape, mode)
    tm, tn, tk = _pick(m, tm), _pick(n, tn), _pick(k, tk)
    nk = k // tk
    dims = {"nn": NN, "nt": NT, "tn": TN}[mode]

    def body(*refs):
        a_ref, b_ref = refs[:2]
        add_ref = refs[2] if add is not None else None
        o_ref = refs[3] if add is not None else refs[2]

        def finish(r):
            if add_ref is not None:
                r = r + add_coef * add_ref[...]
            o_ref[...] = r.astype(out_dtype)

        if nk == 1:
            finish(_dot(a_ref[...], b_ref[...], dims))
            return
        acc_ref = refs[-1]
        kk = pl.program_id(2)

        @pl.when(kk == 0)
        def _():
            acc_ref[...] = jnp.zeros_like(acc_ref)

        acc_ref[...] += _dot(a_ref[...], b_ref[...], dims)

        @pl.when(kk == nk - 1)
        def _():
            finish(acc_ref[...])

    if mode == "tn":
        a_spec = pl.BlockSpec((tk, tm), lambda i, j, kk: (kk, i))
    else:
        a_spec = pl.BlockSpec((tm, tk), lambda i, j, kk: (i, kk))
    if mode == "nt":
        b_spec = pl.BlockSpec((tn, tk), lambda i, j, kk: (j, kk))
    else:
        b_spec = pl.BlockSpec((tk, tn), lambda i, j, kk: (kk, j))
    o_spec = pl.BlockSpec((tm, tn), lambda i, j, kk: (i, j))
    in_specs = [a_spec, b_spec] + ([o_spec] if add is not None else [])
    args = (a, b) + ((add,) if add is not None else ())
    return pl.pallas_call(
        body, name=name, grid=(m // tm, n // tn, nk), in_specs=in_specs, out_specs=o_spec,
        out_shape=jax.ShapeDtypeStruct((m, n), out_dtype), scratch_shapes=[pltpu.VMEM((tm, tn), F32)] if nk > 1 else [],
        compiler_params=_params(3),
    )(*args)


def _ln_fwd(x, add, gb, name, gp=None, pp=None, tr=512):
    t, d = x.shape
    tr = _pick(t, tr)
    with_ple = gp is not None

    def body(*refs):
        if with_ple:
            x_ref, a_ref, gp_ref, pp_ref, gb_ref, y_ref, r_ref = refs
        else:
            x_ref, a_ref, gb_ref, y_ref, r_ref = refs
        r = ALPHA * x_ref[...] + a_ref[...]
        if with_ple:
            r = r + _sigmoid(gp_ref[...]) * pp_ref[...]
        mu = jnp.mean(r, axis=1, keepdims=True)
        xc = r - mu
        var = jnp.mean(xc * xc, axis=1, keepdims=True)
        y_ref[...] = xc * lax.rsqrt(var + LN_EPS) * gb_ref[0:1, :] + gb_ref[1:2, :]
        r_ref[...] = r

    row = pl.BlockSpec((tr, d), lambda i: (i, 0))
    vec = pl.BlockSpec((2, d), lambda i: (0, 0))
    n_row = 4 if with_ple else 2
    args = (x, add) + ((gp, pp) if with_ple else ()) + (gb,)
    return pl.pallas_call(
        body, name=name, grid=(t // tr,), in_specs=[row] * n_row + [vec], out_specs=[row, row],
        out_shape=[jax.ShapeDtypeStruct((t, d), F32)] * 2, compiler_params=_params(1),
    )(*args)


def _ln_bwd(r, gb, dy, name, gp=None, pp=None, tr=512):
    t, d = r.shape
    tr = _pick(t, tr)
    with_ple = gp is not None

    def body(*refs):
        if with_ple:
            r_ref, dy_ref, gp_ref, pp_ref, gb_ref, dr_ref, dgp_ref, dpp_ref, st_ref = refs
        else:
            r_ref, dy_ref, gb_ref, dr_ref, st_ref = refs
        i = pl.program_id(0)

        @pl.when(i == 0)
        def _():
            st_ref[...] = jnp.zeros_like(st_ref)

        rv = r_ref[...]
        dy_v = dy_ref[...]
        mu = jnp.mean(rv, axis=1, keepdims=True)
        xc = rv - mu
        var = jnp.mean(xc * xc, axis=1, keepdims=True)
        rstd = lax.rsqrt(var + LN_EPS)
        xhat = xc * rstd
        dxh = dy_v * gb_ref[0:1, :]
        m1 = jnp.mean(dxh, axis=1, keepdims=True)
        m2 = jnp.mean(dxh * xhat, axis=1, keepdims=True)
        dr = rstd * (dxh - m1 - xhat * m2)
        dr_ref[...] = dr
        rid = lax.broadcasted_iota(jnp.int32, (2, d), 0)
        dg = jnp.sum(dy_v * xhat, axis=0, keepdims=True)
        db = jnp.sum(dy_v, axis=0, keepdims=True)
        st_ref[...] += jnp.where(rid == 0, dg, db)
        if with_ple:
            sg = _sigmoid(gp_ref[...])
            ppv = pp_ref[...]
            dgp_ref[...] = (dr * ppv * sg * (1.0 - sg)).astype(dgp_ref.dtype)
            dpp_ref[...] = (dr * sg).astype(dpp_ref.dtype)

    row = pl.BlockSpec((tr, d), lambda i: (i, 0))
    vec = pl.BlockSpec((2, d), lambda i: (0, 0))
    if with_ple:
        in_specs, args = [row] * 4 + [vec], (r, dy, gp, pp, gb)
        out_specs = [row, row, row, vec]
        out_shape = [jax.ShapeDtypeStruct((t, d), F32), jax.ShapeDtypeStruct((t, d), MXU_DTYPE),
                     jax.ShapeDtypeStruct((t, d), MXU_DTYPE), jax.ShapeDtypeStruct((2, d), F32)]
    else:
        in_specs, args = [row] * 2 + [vec], (r, dy, gb)
        out_specs = [row, vec]
        out_shape = [jax.ShapeDtypeStruct((t, d), F32), jax.ShapeDtypeStruct((2, d), F32)]
    return pl.pallas_call(body, name=name, grid=(t // tr,), in_specs=in_specs, out_specs=out_specs,
                          out_shape=out_shape, compiler_params=_params(1))(*args)


def _loss_grad(y, target, name, tr=512):
    t, d = y.shape
    tr = _pick(t, tr)

    def body(y_ref, t_ref, dy_ref, l_ref):
        i = pl.program_id(0)

        @pl.when(i == 0)
        def _():
            l_ref[...] = jnp.zeros_like(l_ref)

        e = y_ref[...] - t_ref[...]
        dy_ref[...] = e * (1.0 / d)
        per_tok = jnp.mean(e * e, axis=1, keepdims=True)
        l_ref[...] += 0.5 * jnp.sum(per_tok, axis=0, keepdims=True)

    row = pl.BlockSpec((tr, d), lambda i: (i, 0))
    acc = pl.BlockSpec((SUBLANES, LANES), lambda i: (0, 0))
    return pl.pallas_call(body, name=name, grid=(t // tr,), in_specs=[row, row], out_specs=[row, acc],
                          out_shape=[jax.ShapeDtypeStruct((t, d), F32), jax.ShapeDtypeStruct((SUBLANES, LANES), F32)],
                          compiler_params=_params(1))(y, target)


def _shift_down(v, k, row):
    return jnp.where(row >= k, pltpu.roll(v, k, 0), 0.0)


def _shift_up(v, k, row):
    n = v.shape[0]
    return jnp.where(row < n - k, pltpu.roll(v, n - k, 0), 0.0)


def _pool_window(lane):
    grp = lane // HEAD_DIM
    return jnp.where(grp == 0, POOL_WINDOWS[0], jnp.where(grp == 1, POOL_WINDOWS[1],
                     jnp.where(grp == 2, POOL_WINDOWS[2], POOL_WINDOWS[3])))


def _pool_select(lane, s2, s4, s8, s16):
    grp = lane // HEAD_DIM
    return jnp.where(grp == 0, s2, jnp.where(grp == 1, s4, jnp.where(grp == 2, s8, s16)))


def _pooled(u, row, lane):
    s2 = u + _shift_down(u, 1, row)
    s4 = s2 + _shift_down(s2, 2, row)
    s8 = s4 + _shift_down(s4, 4, row)
    s16 = s8 + _shift_down(s8, 8, row)
    cnt = jnp.minimum(row + 1, _pool_window(lane)).astype(F32)
    return _pool_select(lane, s2, s4, s8, s16) / cnt - u, cnt


def _pool_fwd(h, wbd, scale, nb, s, name):
    def body(u_ref, w_ref, sc_ref, o_ref):
        u = u_ref[...]
        row = lax.broadcasted_iota(jnp.int32, u.shape, 0)
        lane = lax.broadcasted_iota(jnp.int32, u.shape, 1)
        pooled, _ = _pooled(u, row, lane)
        o_ref[...] = (_dot(pooled, w_ref[...]) * sc_ref[...]).astype(o_ref.dtype)

    wb = POOL_WIDTH
    return pl.pallas_call(
        body, name=name, grid=(nb,),
        in_specs=[pl.BlockSpec((s, wb), lambda b: (b, H_POOL // wb)), pl.BlockSpec((wb, wb), lambda b: (0, 0)),
                  pl.BlockSpec((1, wb), lambda b: (0, 0))],
        out_specs=pl.BlockSpec((s, wb), lambda b: (b, 0)),
        out_shape=jax.ShapeDtypeStruct((nb * s, wb), MXU_DTYPE), compiler_params=_params(1),
    )(h, wbd, scale)


def _pool_bwd(h, dmix, wbd, scale, nb, s, name):
    wb = POOL_WIDTH

    def body(u_ref, do_ref, w_ref, sc_ref, du_ref, dw_ref, ds_ref):
        b = pl.program_id(0)

        @pl.when(b == 0)
        def _():
            dw_ref[...] = jnp.zeros_like(dw_ref)
            ds_ref[...] = jnp.zeros_like(ds_ref)

        u = u_ref[...]
        row = lax.broadcasted_iota(jnp.int32, u.shape, 0)
        lane = lax.broadcasted_iota(jnp.int32, u.shape, 1)
        pooled, cnt = _pooled(u, row, lane)
        mixed = _dot(pooled, w_ref[...])
        do = do_ref[...]
        ds_ref[...] += jnp.sum(do * mixed, axis=0, keepdims=True)
        dm = do * sc_ref[...]
        dw_ref[...] += _dot(pooled, dm, TN)
        dpool = _dot(dm, w_ref[...], NT)
        qv = dpool / cnt
        f2 = qv + _shift_up(qv, 1, row)
        f4 = f2 + _shift_up(f2, 2, row)
        f8 = f4 + _shift_up(f4, 4, row)
        f16 = f8 + _shift_up(f8, 8, row)
        du_ref[...] = (_pool_select(lane, f2, f4, f8, f16) - dpool).astype(du_ref.dtype)

    return pl.pallas_call(
        body, name=name, grid=(nb,),
        in_specs=[pl.BlockSpec((s, wb), lambda b: (b, H_POOL // wb)), pl.BlockSpec((s, wb), lambda b: (b, 3)),
                  pl.BlockSpec((wb, wb), lambda b: (0, 0)), pl.BlockSpec((1, wb), lambda b: (0, 0))],
        out_specs=[pl.BlockSpec((s, wb), lambda b: (b, 0)), pl.BlockSpec((wb, wb), lambda b: (0, 0)),
                   pl.BlockSpec((1, wb), lambda b: (0, 0))],
        out_shape=[jax.ShapeDtypeStruct((nb * s, wb), MXU_DTYPE), jax.ShapeDtypeStruct((wb, wb), F32),
                   jax.ShapeDtypeStruct((1, wb), F32)],
        compiler_params=_params(1),
    )(h, dmix, wbd, scale)


def _glu_conv(x, w_ref, b_ref, row):
    return (b_ref[...] + w_ref[2:3, :] * x + w_ref[1:2, :] * _shift_down(x, 1, row)
            + w_ref[0:1, :] * _shift_down(x, 2, row))


def _glu_fwd(up, cw, cb, nb, s, name):
    wt = 2 * GLU_TILE
    nt = up.shape[1] // wt

    def body(u_ref, w_ref, b_ref, o_ref):
        x = u_ref[...]
        row = lax.broadcasted_iota(jnp.int32, x.shape, 0)
        c = _glu_conv(x, w_ref, b_ref, row)
        o_ref[...] = (_silu(c[:, :GLU_TILE]) * c[:, GLU_TILE:]).astype(o_ref.dtype)

    return pl.pallas_call(
        body, name=name, grid=(nt, nb),
        in_specs=[pl.BlockSpec((s, wt), lambda j, b: (b, j)), pl.BlockSpec((3, wt), lambda j, b: (0, j)),
                  pl.BlockSpec((1, wt), lambda j, b: (0, j))],
        out_specs=pl.BlockSpec((s, GLU_TILE), lambda j, b: (b, j)),
        out_shape=jax.ShapeDtypeStruct((nb * s, nt * GLU_TILE), MXU_DTYPE), compiler_params=_params(2),
    )(up, cw, cb)


def _glu_bwd(up, dact, cw, cb, nb, s, name):
    wt = 2 * GLU_TILE
    nt = up.shape[1] // wt

    def body(u_ref, da_ref, w_ref, b_ref, du_ref, acc_ref):
        b = pl.program_id(1)

        @pl.when(b == 0)
        def _():
            acc_ref[...] = jnp.zeros_like(acc_ref)

        x = u_ref[...]
        row = lax.broadcasted_iota(jnp.int32, x.shape, 0)
        x1 = _shift_down(x, 1, row)
        x2 = _shift_down(x, 2, row)
        c = b_ref[...] + w_ref[2:3, :] * x + w_ref[1:2, :] * x1 + w_ref[0:1, :] * x2
        gate, val = c[:, :GLU_TILE], c[:, GLU_TILE:]
        da = da_ref[...]
        dc = jnp.concatenate([da * val * _dsilu(gate), da * _silu(gate)], axis=1)
        dx = (w_ref[2:3, :] * dc + w_ref[1:2, :] * _shift_up(dc, 1, row) + w_ref[0:1, :] * _shift_up(dc, 2, row))
        du_ref[...] = dx.astype(du_ref.dtype)
        rid = lax.broadcasted_iota(jnp.int32, (SUBLANES, wt), 0)
        dw0 = jnp.sum(dc * x2, axis=0, keepdims=True)
        dw1 = jnp.sum(dc * x1, axis=0, keepdims=True)
        dw2 = jnp.sum(dc * x, axis=0, keepdims=True)
        db = jnp.sum(dc, axis=0, keepdims=True)
        acc_ref[...] += (jnp.where(rid == 0, dw0, 0.0) + jnp.where(rid == 1, dw1, 0.0)
                         + jnp.where(rid == 2, dw2, 0.0) + jnp.where(rid == 3, db, 0.0))

    return pl.pallas_call(
        body, name=name, grid=(nt, nb),
        in_specs=[pl.BlockSpec((s, wt), lambda j, b: (b, j)), pl.BlockSpec((s, GLU_TILE), lambda j, b: (b, j)),
                  pl.BlockSpec((3, wt), lambda j, b: (0, j)), pl.BlockSpec((1, wt), lambda j, b: (0, j))],
        out_specs=[pl.BlockSpec((s, wt), lambda j, b: (b, j)), pl.BlockSpec((SUBLANES, wt), lambda j, b: (0, j))],
        out_shape=[jax.ShapeDtypeStruct((nb * s, nt * wt), MXU_DTYPE), jax.ShapeDtypeStruct((SUBLANES, nt * wt), F32)],
        compiler_params=_params(2),
    )(up, dact, cw, cb)


def _sb_constants():
    row = lax.broadcasted_iota(jnp.int32, (QB, QB), 0)
    col = lax.broadcasted_iota(jnp.int32, (QB, QB), 1)
    return jnp.stack([row > col, row < col, col < row]).astype(BF16)


_SB_CONST_SPEC = pl.BlockSpec((3, QB, QB), lambda b, p, i: (0, 0, 0))


def _sb_fwd(h, nb, s, name, comm=None):
    nq = s // QB
    scale = HEAD_DIM ** -0.5
    n_in, n_out = _comm_counts(comm)

    def body(q_ref, k_ref, v_ref, tri_ref, *rest):
        o_ref = rest[n_in]
        i = pl.program_id(2)
        _comm_hosted(comm, rest[:n_in], rest[n_in + 1:], (nb, 3, nq))
        sls = [slice(hd * HEAD_DIM, (hd + 1) * HEAD_DIM) for hd in range(2)]
        qs = [(q_ref[:, sl] * scale).astype(MXU_DTYPE) for sl in sls]

        def scores(hd, j, diagonal=False):
            r0 = pl.multiple_of(j * QB, QB)
            z = _dot(qs[hd], k_ref[pl.ds(r0, QB), sls[hd]], NT)
            ln = -_softplus(z)
            ls = z + ln
            if diagonal:
                low = tri_ref[2] > 0
                ln = jnp.where(low, ln, 0.0)
                ls = jnp.where(low, ls, MASKED_LOG)
            return ls, _dot_exact01(ln, tri_ref[0], terms=2), jnp.sum(ln, axis=1, keepdims=True)

        def output(hd, j, ls, tl, ct):
            r0 = pl.multiple_of(j * QB, QB)
            return _dot(jnp.exp(ls + tl + ct), v_ref[pl.ds(r0, QB), sls[hd]])

        def group(blocks, carry, diagonal_first=False):
            sc = [[scores(hd, j, diagonal_first and n == 0) for n, j in enumerate(blocks)] for hd in range(2)]
            out = []
            for hd in range(2):
                a, c = carry[hd]
                for (ls, tl, sm), j in zip(sc[hd], blocks):
                    a = a + output(hd, j, ls, tl, c)
                    c = c + sm
                out.append((a, c))
            return tuple(out)

        start = (jnp.zeros((QB, HEAD_DIM), F32), jnp.zeros((QB, 1), F32))
        below = jnp.minimum(i, 1)
        left = i - below
        carry = lax.fori_loop(0, below, lambda t, c: group([i, i - 1], c, True), (start, start))
        carry = lax.fori_loop(0, 1 - below, lambda t, c: group([i], c, True), carry)
        carry = lax.fori_loop(0, left // 2, lambda t, c: group([left - 1 - 2 * t, left - 2 - 2 * t], c), carry)
        carry = lax.fori_loop(0, left % 2, lambda t, c: group([0], c), carry)
        o_ref[:, sls[0]] = carry[0][0].astype(o_ref.dtype)
        o_ref[:, sls[1]] = carry[1][0].astype(o_ref.dtype)

    qspec = lambda off: pl.BlockSpec((QB, LANES), lambda b, p, i: (b * nq + i, off // LANES + p))
    kvspec = lambda off: pl.BlockSpec((s, LANES), lambda b, p, i: (b, off // LANES + p))
    c_in, c_specs, c_shapes, c_scratch = _comm_call_args(comm)
    res = pl.pallas_call(
        body, name=name, grid=(nb, 3, nq), in_specs=[qspec(H_Q), kvspec(H_K), kvspec(H_V), _SB_CONST_SPEC] + [_ANY] * n_in,
        out_specs=[pl.BlockSpec((QB, LANES), lambda b, p, i: (b * nq + i, p))] + c_specs,
        out_shape=[jax.ShapeDtypeStruct((nb * s, SB_WIDTH), MXU_DTYPE)] + c_shapes, scratch_shapes=c_scratch,
        compiler_params=_params(3, comm is not None),
    )(h, h, h, _sb_constants(), *c_in)
    return res[0], res[1:]


def _sb_bwd(h, dmix, nb, s, name, comm=None):
    nq = s // QB
    scale = HEAD_DIM ** -0.5
    n_in, n_out = _comm_counts(comm)

    def body(q_ref, k_ref, v_ref, do_ref, tri_ref, *rest):
        dq_ref, dk_out, dv_out = rest[n_in:n_in + 3]
        p_buf, ls_buf, dk_ref, dv_ref = rest[n_in + 3 + n_out:n_in + 7 + n_out]
        i = pl.program_id(2)
        _comm_hosted(comm, rest[:n_in], rest[n_in + 3:n_in + 3 + n_out] + rest[n_in + 7 + n_out:], (nb, 3, nq))

        @pl.when(i == 0)
        def _():
            dk_ref[...] = jnp.zeros_like(dk_ref)
            dv_ref[...] = jnp.zeros_like(dv_ref)

        sls = [slice(hd * HEAD_DIM, (hd + 1) * HEAD_DIM) for hd in range(2)]
        q_raw = [q_ref[:, sl].astype(MXU_DTYPE) for sl in sls]
        qs = [(q_ref[:, sl] * scale).astype(MXU_DTYPE) for sl in sls]
        do = [do_ref[:, sl].astype(MXU_DTYPE) for sl in sls]

        def down_scores(hd, j, diagonal):
            r0 = pl.multiple_of(j * QB, QB)
            z = _dot(qs[hd], k_ref[pl.ds(r0, QB), sls[hd]], NT)
            ln = -_softplus(z)
            ls = z + ln
            if diagonal:
                low = tri_ref[2] > 0
                ln = jnp.where(low, ln, 0.0)
                ls = jnp.where(low, ls, MASKED_LOG)
            da = _dot(do[hd], v_ref[pl.ds(r0, QB), sls[hd]], NT)
            return ls, _dot_exact01(ln, tri_ref[0], terms=2), jnp.sum(ln, axis=1, keepdims=True), da

        def down_group(blocks, carry, diagonal_first=False):
            sc = [[down_scores(hd, j, diagonal_first and n == 0) for n, j in enumerate(blocks)] for hd in range(2)]
            out = []
            for hd in range(2):
                ct = carry[hd]
                for (ls, tl, sm, da), j in zip(sc[hd], blocks):
                    r0 = pl.multiple_of(j * QB, QB)
                    a = jnp.exp(ls + tl + ct)
                    p_buf[hd, j] = da * a
                    ls_buf[hd, j] = ls
                    dv_ref[pl.ds(r0, QB), sls[hd]] += _dot(a, do[hd], TN)
                    ct = ct + sm
                out.append(ct)
            return tuple(out)

        zero = jnp.zeros((QB, 1), F32)
        below = jnp.minimum(i, 1)
        left = i - below
        carry = lax.fori_loop(0, below, lambda t, c: down_group([i, i - 1], c, True), (zero, zero))
        carry = lax.fori_loop(0, 1 - below, lambda t, c: down_group([i], c, True), carry)
        carry = lax.fori_loop(0, left // 2, lambda t, c: down_group([left - 1 - 2 * t, left - 2 - 2 * t], c), carry)
        lax.fori_loop(0, left % 2, lambda t, c: down_group([0], c), carry)

        def up_group(blocks, carry):
            ld = []
            for hd in range(2):
                ld.append([])
                for j in blocks:
                    pj = p_buf[hd, j]
                    ld[hd].append((pj, jnp.exp(ls_buf[hd, j]), _dot_exact01(pj, tri_ref[1]), jnp.sum(pj, axis=1, keepdims=True)))
            out = []
            for hd in range(2):
                dq, cp = carry[hd]
                for (pj, sg, cm, sm), j in zip(ld[hd], blocks):
                    r0 = pl.multiple_of(j * QB, QB)
                    dz = (pj * (1.0 - sg) - (cp + cm) * sg) * scale
                    dk_ref[pl.ds(r0, QB), sls[hd]] += _dot(dz, q_raw[hd], TN)
                    dq = dq + _dot(dz, k_ref[pl.ds(r0, QB), sls[hd]])
                    cp = cp + sm
                out.append((dq, cp))
            return tuple(out)

        start = (jnp.zeros((QB, HEAD_DIM), F32), zero)
        odd = (i + 1) % 2
        carry = lax.fori_loop(0, odd, lambda t, c: up_group([0], c), (start, start))
        carry = lax.fori_loop(0, (i + 1) // 2, lambda t, c: up_group([odd + 2 * t, odd + 2 * t + 1], c), carry)
        dq_ref[:, sls[0]] = carry[0][0].astype(dq_ref.dtype)
        dq_ref[:, sls[1]] = carry[1][0].astype(dq_ref.dtype)

        @pl.when(i == nq - 1)
        def _():
            dk_out[...] = dk_ref[...].astype(dk_out.dtype)
            dv_out[...] = dv_ref[...].astype(dv_out.dtype)

    qspec = lambda off: pl.BlockSpec((QB, LANES), lambda b, p, i: (b * nq + i, off // LANES + p))
    kvspec = lambda off: pl.BlockSpec((s, LANES), lambda b, p, i: (b, off // LANES + p))
    blk_out = pl.BlockSpec((QB, LANES), lambda b, p, i: (b * nq + i, p))
    seq_out = pl.BlockSpec((s, LANES), lambda b, p, i: (b, p))
    shp = jax.ShapeDtypeStruct((nb * s, SB_WIDTH), MXU_DTYPE)
    c_in, c_specs, c_shapes, c_scratch = _comm_call_args(comm)
    res = pl.pallas_call(
        body, name=name, grid=(nb, 3, nq),
        in_specs=[qspec(H_Q), kvspec(H_K), kvspec(H_V), pl.BlockSpec((QB, LANES), lambda b, p, i: (b * nq + i, 3 + p)),
                  _SB_CONST_SPEC] + [_ANY] * n_in,
        out_specs=[blk_out, seq_out, seq_out] + c_specs, out_shape=[shp, shp, shp] + c_shapes,
        scratch_shapes=[pltpu.VMEM((2, nq, QB, QB), F32), pltpu.VMEM((2, nq, QB, QB), F32),
                        pltpu.VMEM((s, LANES), F32), pltpu.VMEM((s, LANES), F32)] + c_scratch,
        compiler_params=_params(3, comm is not None),
    )(h, h, h, dmix, _sb_constants(), *c_in)
    return res[0], res[1], res[2], res[3:]


def _ssd_conv(cur_ref, halo_ref, w_ref, b_ref, ext_ref, first):
    n = SSD_CHUNK
    cur = cur_ref[...]
    ext_ref[0:SUBLANES, :] = jnp.where(first, 0.0, halo_ref[...])
    ext_ref[SUBLANES:SUBLANES + n, :] = cur
    return (b_ref[...] + w_ref[3:4, :] * cur + w_ref[2:3, :] * ext_ref[pl.ds(SUBLANES - 1, n), :]
            + w_ref[1:2, :] * ext_ref[pl.ds(SUBLANES - 2, n), :] + w_ref[0:1, :] * ext_ref[pl.ds(SUBLANES - 3, n), :])


def _ssd_tri():
    row = lax.broadcasted_iota(jnp.int32, (SSD_CHUNK, SSD_CHUNK), 0)
    col = lax.broadcasted_iota(jnp.int32, (SSD_CHUNK, SSD_CHUNK), 1)
    return row, col


def _ssd_specs(nc, rev):
    n = SSD_CHUNK
    hb = n // SUBLANES

    def cidx(c):
        return (nc - 1 - c) if rev else c

    def blk(width, off):
        return pl.BlockSpec((n, width), lambda b, c: (b * nc + cidx(c), off // width))

    def halo(width, off):
        return pl.BlockSpec((SUBLANES, width), lambda b, c: (jnp.maximum((b * nc + cidx(c)) * hb - 1, 0), off // width))

    def full(shape):
        return pl.BlockSpec(shape, lambda b, c: (0,) * len(shape))

    return cidx, blk, halo, full


def _ssd_core_fwd(x, bc, dt, acum, acum_t, a_row, d_row, h_prev_ref, tri):
    n = SSD_CHUNK
    heads = []
    for g in range(2):
        bm = bc[:, g * SSD_STATE:(g + 1) * SSD_STATE]
        cm = bc[:, 2 * SSD_STATE + g * SSD_STATE: 2 * SSD_STATE + (g + 1) * SSD_STATE]
        gmat = _dot(cm, bm, NT)
        for r in range(3):
            hh = g * 3 + r
            hp = h_prev_ref[hh * HEAD_DIM:(hh + 1) * HEAD_DIM, :]
            heads.append(dict(g=g, hh=hh, bm=bm, cm=cm, gmat=gmat, hp=hp, cmh=_dot(cm, hp, NT)))
    for hd in heads:
        hh = hd["hh"]
        ac = acum[:, hh:hh + 1]
        ar = acum_t[hh:hh + 1, :]
        hd["dec"] = jnp.where(tri, jnp.exp(jnp.minimum(ac - ar, 0.0)), 0.0)
        hd["xh"] = x[:, hh * HEAD_DIM:(hh + 1) * HEAD_DIM]
        hd["dth"] = dt[:, hh:hh + 1]
        hd["xdt"] = hd["xh"] * hd["dth"]
        hd["ea"] = jnp.exp(ac)
        hd["m"] = hd["gmat"] * hd["dec"]
        hd["al"] = acum[n - 1:n, hh:hh + 1]
        hd["w"] = jnp.exp(hd["al"] - ac)
    for hd in heads:
        hd["yd"] = _dot(hd["m"], hd["xdt"])
    for hd in heads:
        hd["yo"] = hd["ea"] * hd["cmh"]
        hd["y"] = hd["yd"] + hd["yo"] + d_row[:, hd["hh"]:hd["hh"] + 1] * hd["xh"]
    return heads


def _ssd_prep(xs_ref, xsh_ref, bc_ref, bch_ref, dt_ref, cwx_ref, cbx_ref, cwb_ref, cbb_ref, vec_ref, xe_ref, be_ref, first):
    pre_x = _ssd_conv(xs_ref, xsh_ref, cwx_ref, cbx_ref, xe_ref, first)
    pre_bc = _ssd_conv(bc_ref, bch_ref, cwb_ref, cbb_ref, be_ref, first)
    x = _silu(pre_x)
    bc = _silu(pre_bc)
    dt_pre = dt_ref[...] + vec_ref[0:1, :]
    dt = _softplus(dt_pre)
    a_row = vec_ref[1:2, :]
    amat = dt * a_row
    row, col = _ssd_tri()
    upper = (row <= col).astype(BF16)
    lower = (col <= row).astype(BF16)
    acum = _dot_exact01(amat, lower, NN, x_left=False)
    acum_t = _dot_exact01(amat, upper, TN, x_left=True)
    return pre_x, pre_bc, x, bc, dt_pre, dt, a_row, acum, acum_t, row, col, upper


def _ssd_gate_norm(y, z, nw):
    lane = lax.broadcasted_iota(jnp.int32, y.shape, 1)
    g0 = lane < SSD_WIDTH // 2
    hg = y * _silu(z)
    sq = hg * hg
    ms0 = jnp.sum(jnp.where(g0, sq, 0.0), axis=1, keepdims=True) * (2.0 / SSD_WIDTH)
    ms1 = jnp.sum(jnp.where(g0, 0.0, sq), axis=1, keepdims=True) * (2.0 / SSD_WIDTH)
    rs = jnp.where(g0, lax.rsqrt(ms0 + RMS_EPS), lax.rsqrt(ms1 + RMS_EPS))
    return hg, rs, g0


def _ssd_fwd(h, cwx, cbx, cwb, cbb, vec, nw, nb, s, name):
    n = SSD_CHUNK
    nc = s // n
    _, blk, halo, full = _ssd_specs(nc, False)

    def body(bc_ref, bch_ref, z_ref, xs_ref, xsh_ref, dt_ref, cwx_ref, cbx_ref, cwb_ref, cbb_ref, vec_ref, nw_ref,
             o_ref, hs_ref, h_scr, xe_ref, be_ref, y_scr):
        c = pl.program_id(1)

        @pl.when(c == 0)
        def _():
            h_scr[...] = jnp.zeros_like(h_scr)

        (_, _, x, bc, _, dt, a_row, acum, acum_t, row, col, _) = _ssd_prep(
            xs_ref, xsh_ref, bc_ref, bch_ref, dt_ref, cwx_ref, cbx_ref, cwb_ref, cbb_ref, vec_ref, xe_ref, be_ref, c == 0)
        hs_ref[...] = h_scr[...]
        heads = _ssd_core_fwd(x, bc, dt, acum, acum_t, a_row, vec_ref[2:3, :], hs_ref, col <= row)
        for hd in heads:
            sl = slice(hd["hh"] * HEAD_DIM, (hd["hh"] + 1) * HEAD_DIM)
            y_scr[:, sl] = hd["y"]
            h_scr[sl, :] = jnp.exp(hd["al"]) * hd["hp"] + _dot(hd["xdt"] * hd["w"], hd["bm"], TN)
        hg, rs, _ = _ssd_gate_norm(y_scr[...], z_ref[...], nw_ref[...])
        o_ref[...] = (hg * rs * nw_ref[...]).astype(o_ref.dtype)

    t = nb * s
    return pl.pallas_call(
        body, name=name, grid=(nb, nc),
        in_specs=[blk(512, H_BC), halo(512, H_BC), blk(384, H_Z), blk(384, H_XS), halo(384, H_XS), blk(128, H_DT),
                  full((4, 384)), full((1, 384)), full((4, 512)), full((1, 512)), full((SUBLANES, LANES)), full((1, 384))],
        out_specs=[pl.BlockSpec((n, SSD_WIDTH), lambda b, c: (b * nc + c, 0)),
                   pl.BlockSpec((None, SSD_WIDTH, SSD_STATE), lambda b, c: (b * nc + c, 0, 0))],
        out_shape=[jax.ShapeDtypeStruct((t, SSD_WIDTH), MXU_DTYPE),
                   jax.ShapeDtypeStruct((nb * nc, SSD_WIDTH, SSD_STATE), F32)],
        scratch_shapes=[pltpu.VMEM((SSD_WIDTH, SSD_STATE), F32), pltpu.VMEM((n + SUBLANES, 384), F32),
                        pltpu.VMEM((n + SUBLANES, 512), F32), pltpu.VMEM((n, SSD_WIDTH), F32)],
        compiler_params=_params(2),
    )(h, h, h, h, h, h, cwx, cbx, cwb, cbb, vec, nw)


def _ssd_bwd(h, hstate, dmix, cwx, cbx, cwb, cbb, vec, nw, nb, s, name):
    n = SSD_CHUNK
    nc = s // n
    cidx, blk, halo, full = _ssd_specs(nc, True)

    def body(bc_ref, bch_ref, z_ref, xs_ref, xsh_ref, dt_ref, hs_ref, do_ref, cwx_ref, cbx_ref, cwb_ref, cbb_ref,
             vec_ref, nw_ref, dz_ref, dxs_ref, dbc_ref, ddt_ref, gx_ref, gb_ref, gv_ref, gn_ref,
             dh_scr, xe_ref, be_ref, y_scr, dx_scr, dbc_scr, dxe_ref, dbe_ref, cx_ref, cb_ref):
        b = pl.program_id(0)
        c = pl.program_id(1)
        cc = nc - 1 - c

        @pl.when(jnp.logical_and(b == 0, c == 0))
        def _():
            gx_ref[...] = jnp.zeros_like(gx_ref)
            gb_ref[...] = jnp.zeros_like(gb_ref)
            gv_ref[...] = jnp.zeros_like(gv_ref)
            gn_ref[...] = jnp.zeros_like(gn_ref)

        @pl.when(c == 0)
        def _():
            dh_scr[...] = jnp.zeros_like(dh_scr)
            cx_ref[...] = jnp.zeros_like(cx_ref)
            cb_ref[...] = jnp.zeros_like(cb_ref)

        (pre_x, pre_bc, x, bc, dt_pre, dt, a_row, acum, acum_t, row, col, upper) = _ssd_prep(
            xs_ref, xsh_ref, bc_ref, bch_ref, dt_ref, cwx_ref, cbx_ref, cwb_ref, cbb_ref, vec_ref, xe_ref, be_ref, cc == 0)
        tri = col <= row
        d_row = vec_ref[2:3, :]
        heads = _ssd_core_fwd(x, bc, dt, acum, acum_t, a_row, d_row, hs_ref, tri)
        for hd in heads:
            y_scr[:, hd["hh"] * HEAD_DIM:(hd["hh"] + 1) * HEAD_DIM] = hd["y"]
        y = y_scr[...]
        z = z_ref[...]
        nwv = nw_ref[...]
        hg, rs, g0 = _ssd_gate_norm(y, z, nwv)
        do = do_ref[...]
        nrm = hg * rs
        gn_ref[...] += jnp.sum(do * nrm, axis=0, keepdims=True)
        dn = do * nwv
        dnn = dn * nrm
        mean0 = jnp.sum(jnp.where(g0, dnn, 0.0), axis=1, keepdims=True) * (2.0 / SSD_WIDTH)
        mean1 = jnp.sum(jnp.where(g0, 0.0, dnn), axis=1, keepdims=True) * (2.0 / SSD_WIDTH)
        dhg = rs * (dn - nrm * jnp.where(g0, mean0, mean1))
        dz_ref[...] = (dhg * y * _dsilu(z)).astype(dz_ref.dtype)
        dy = dhg * _silu(z)

        lane = lax.broadcasted_iota(jnp.int32, (n, LANES), 1)
        lane1 = lax.broadcasted_iota(jnp.int32, (1, LANES), 1)
        last_row = lax.broadcasted_iota(jnp.int32, (n, 1), 0) == n - 1
        dacum_col = jnp.zeros((n, LANES), F32)
        da_rowpart = jnp.zeros((n, LANES), F32)
        ddt = jnp.zeros((n, LANES), F32)
        dd_vec = jnp.zeros((1, LANES), F32)
        for hd in heads:
            sl = slice(hd["hh"] * HEAD_DIM, (hd["hh"] + 1) * HEAD_DIM)
            dyh = dy[:, sl]
            dhn = dh_scr[sl, :]
            hd.update(sl=sl, dyh=dyh, dhn=dhn, t1=_dot(dyh, hd["hp"]), dm=_dot(dyh, hd["xdt"], NT),
                      t2=_dot(hd["bm"], dhn, NT), mtdy=_dot(hd["m"], dyh, TN), xdhn=_dot(hd["xdt"], dhn),
                      dhp=_dot(dyh * hd["ea"], hd["cm"], TN))
        dgs, dbms, dcms = [], [], []
        for g in range(2):
            dg = jnp.zeros((n, n), F32)
            dbm = jnp.zeros((n, SSD_STATE), F32)
            dcm = jnp.zeros((n, SSD_STATE), F32)
            for hd in heads[3 * g:3 * g + 3]:
                hh, sl, dyh, dhn, t2 = hd["hh"], hd["sl"], hd["dyh"], hd["dhn"], hd["t2"]
                el = jnp.exp(hd["al"])
                dd_vec = dd_vec + jnp.where(lane1 == hh, jnp.sum(dyh * hd["xh"]), 0.0)
                dcm = dcm + hd["ea"] * hd["t1"]
                dg = dg + hd["dm"] * hd["dec"]
                e = hd["dm"] * hd["m"]
                dxdt = hd["mtdy"] + hd["w"] * t2
                dbm = dbm + hd["w"] * hd["xdhn"]
                dw_w = jnp.sum(hd["xdt"] * t2, axis=1, keepdims=True) * hd["w"]
                d_el = jnp.sum(dhn * hd["hp"])
                col_part = (jnp.sum(dyh * hd["yo"], axis=1, keepdims=True) + jnp.sum(e, axis=1, keepdims=True) - dw_w
                            + jnp.where(last_row, d_el * el + jnp.sum(dw_w), 0.0))
                dacum_col = dacum_col + jnp.where(lane == hh, col_part, 0.0)
                neg_colsum = -jnp.sum(e, axis=0, keepdims=True)
                rev = jnp.sum(jnp.where(row <= col, neg_colsum, 0.0), axis=1, keepdims=True)
                da_rowpart = da_rowpart + jnp.where(lane == hh, rev, 0.0)
                dh_scr[sl, :] = el * dhn + hd["dhp"]
                dx_scr[:, sl] = d_row[:, hh:hh + 1] * dyh + dxdt * hd["dth"]
                ddt = ddt + jnp.where(lane == hh, jnp.sum(dxdt * hd["xh"], axis=1, keepdims=True), 0.0)
            dgs.append(dg)
            dbms.append(dbm)
            dcms.append(dcm)
        for g in range(2):
            bm, cm = heads[3 * g]["bm"], heads[3 * g]["cm"]
            dbc_scr[:, g * SSD_STATE:(g + 1) * SSD_STATE] = dbms[g] + _dot(dgs[g], cm, TN)
            dbc_scr[:, 2 * SSD_STATE + g * SSD_STATE:2 * SSD_STATE + (g + 1) * SSD_STATE] = dcms[g] + _dot(dgs[g], bm)
        da_mat = _dot_exact01(dacum_col, upper, NN, x_left=False) + da_rowpart
        ddt = ddt + da_mat * a_row
        da_vec = jnp.sum(da_mat * dt, axis=0, keepdims=True)
        ddt_pre = jnp.where(lane < SSD_HEADS, ddt * _sigmoid(dt_pre), 0.0)
        ddt_ref[...] = ddt_pre.astype(ddt_ref.dtype)
        rid = lax.broadcasted_iota(jnp.int32, (SUBLANES, LANES), 0)
        gv_ref[...] += (jnp.where(rid == 0, jnp.sum(ddt_pre, axis=0, keepdims=True), 0.0)
                        + jnp.where(rid == 1, da_vec, 0.0) + jnp.where(rid == 2, dd_vec, 0.0))

        def conv_bwd(dpost, pre, w_ref, ext_ref, dext_ref, carry_ref, cur_ref, out_ref, g_ref, width):
            dco = dpost * _dsilu(pre)
            dext_ref[0:n, :] = dco
            dext_ref[n:n + SUBLANES, :] = carry_ref[...]
            out_ref[...] = (w_ref[3:4, :] * dco + w_ref[2:3, :] * dext_ref[pl.ds(1, n), :]
                            + w_ref[1:2, :] * dext_ref[pl.ds(2, n), :] + w_ref[0:1, :] * dext_ref[pl.ds(3, n), :]
                            ).astype(out_ref.dtype)
            carry_ref[...] = dco[0:SUBLANES, :]
            rid8 = lax.broadcasted_iota(jnp.int32, (SUBLANES, width), 0)
            acc = jnp.where(rid8 == 3, jnp.sum(dco * cur_ref[...], axis=0, keepdims=True), 0.0)
            for j in range(3):
                sh = ext_ref[pl.ds(SUBLANES - 3 + j, n), :]
                acc = acc + jnp.where(rid8 == j, jnp.sum(dco * sh, axis=0, keepdims=True), 0.0)
            acc = acc + jnp.where(rid8 == 4, jnp.sum(dco, axis=0, keepdims=True), 0.0)
            g_ref[...] += acc

        conv_bwd(dx_scr[...], pre_x, cwx_ref, xe_ref, dxe_ref, cx_ref, xs_ref, dxs_ref, gx_ref, 384)
        conv_bwd(dbc_scr[...], pre_bc, cwb_ref, be_ref, dbe_ref, cb_ref, bc_ref, dbc_ref, gb_ref, 512)

    t = nb * s
    rowblk = lambda width: pl.BlockSpec((n, width), lambda b, c: (b * nc + cidx(c), 0))
    return pl.pallas_call(
        body, name=name, grid=(nb, nc),
        in_specs=[blk(512, H_BC), halo(512, H_BC), blk(384, H_Z), blk(384, H_XS), halo(384, H_XS), blk(128, H_DT),
                  pl.BlockSpec((None, SSD_WIDTH, SSD_STATE), lambda b, c: (b * nc + cidx(c), 0, 0)),
                  pl.BlockSpec((n, SSD_WIDTH), lambda b, c: (b * nc + cidx(c), 0)),
                  full((4, 384)), full((1, 384)), full((4, 512)), full((1, 512)), full((SUBLANES, LANES)), full((1, 384))],
        out_specs=[rowblk(384), rowblk(384), rowblk(512), rowblk(128),
                   full((SUBLANES, 384)), full((SUBLANES, 512)), full((SUBLANES, LANES)), full((1, 384))],
        out_shape=[jax.ShapeDtypeStruct((t, 384), MXU_DTYPE), jax.ShapeDtypeStruct((t, 384), MXU_DTYPE),
                   jax.ShapeDtypeStruct((t, 512), MXU_DTYPE), jax.ShapeDtypeStruct((t, 128), MXU_DTYPE),
                   jax.ShapeDtypeStruct((SUBLANES, 384), F32), jax.ShapeDtypeStruct((SUBLANES, 512), F32),
                   jax.ShapeDtypeStruct((SUBLANES, LANES), F32), jax.ShapeDtypeStruct((1, 384), F32)],
        scratch_shapes=[pltpu.VMEM((SSD_WIDTH, SSD_STATE), F32), pltpu.VMEM((n + SUBLANES, 384), F32),
                        pltpu.VMEM((n + SUBLANES, 512), F32), pltpu.VMEM((n, SSD_WIDTH), F32),
                        pltpu.VMEM((n, 384), F32), pltpu.VMEM((n, 512), F32),
                        pltpu.VMEM((n + SUBLANES, 384), F32), pltpu.VMEM((n + SUBLANES, 512), F32),
                        pltpu.VMEM((SUBLANES, 384), F32), pltpu.VMEM((SUBLANES, 512), F32)],
        compiler_params=_params(2),
    )(h, h, h, h, h, h, hstate, dmix, cwx, cbx, cwb, cbb, vec, nw)


def _adamw_math(w, g, m, v):
    m = ADAM_B1 * m + (1.0 - ADAM_B1) * g
    v = ADAM_B2 * v + (1.0 - ADAM_B2) * (g * g)
    m_hat = m / (1.0 - ADAM_B1 ** ADAM_STEP)
    v_hat = v / (1.0 - ADAM_B2 ** ADAM_STEP)
    delta = -ADAM_LR * (m_hat / (jnp.sqrt(v_hat) + ADAM_EPS) + ADAM_WD * w)
    return delta, m, v


def _adamw(w, g, m, v, name, tr=256):
    rows, cols = w.shape
    tr = rows if rows <= tr else tr
    assert rows % tr == 0, (rows, tr)

    def body(w_ref, g_ref, m_ref, v_ref, d_ref, nm_ref, nv_ref):
        d, nm, nv = _adamw_math(w_ref[...], g_ref[...], m_ref[...], v_ref[...])
        d_ref[...] = d
        nm_ref[...] = nm
        nv_ref[...] = nv

    spec = pl.BlockSpec((tr, cols), lambda i: (i, 0))
    shp = jax.ShapeDtypeStruct((rows, cols), F32)
    return pl.pallas_call(body, name=name, grid=(rows // tr,), in_specs=[spec] * 4, out_specs=[spec] * 3,
                          out_shape=[shp] * 3, compiler_params=_params(1))(w, g, m, v)


def _sum8_layers(parts, name, tr):
    _, rows, cols = parts[0].shape
    assert rows % tr == 0
    nt = rows // tr

    def body(*refs):
        o_ref = refs[DEPTH]
        layer = pl.program_id(0)
        for l in range(DEPTH):
            @pl.when(layer == l)
            def _(l=l):
                acc = refs[l][0]
                for k in range(1, N_DEV):
                    acc = acc + refs[l][k]
                o_ref[...] = acc

    in_specs = [pl.BlockSpec((N_DEV, tr, cols), lambda a, i, l=l: (0, jnp.clip(i + (a - l) * nt, 0, nt - 1), 0))
                for l in range(DEPTH)]
    return pl.pallas_call(body, name=name, grid=(DEPTH, nt), in_specs=in_specs,
                          out_specs=pl.BlockSpec((None, tr, cols), lambda a, i: (a, i, 0)),
                          out_shape=jax.ShapeDtypeStruct((DEPTH, rows, cols), F32), compiler_params=_params(2))(*parts)


def _all_reduce_small(vec, name):
    rows, cols = vec.shape

    def body(x_ref, out_ref, gbuf, send_sems, recv_sems):
        x, y, c = lax.axis_index("x"), lax.axis_index("y"), lax.axis_index("c")
        me, sibling = (x, y, c), (x, y, 1 - c)
        chips = [(1 - x, y), (x, 1 - y), (1 - x, 1 - y)]

        def slot(px, py, pc):
            return gbuf.at[4 * px + 2 * py + pc]

        def copy(k, block, to, src=None):
            return pltpu.make_async_remote_copy(
                src_ref=slot(*block) if src is None else src, dst_ref=slot(*block),
                send_sem=send_sems.at[k], recv_sem=recv_sems.at[k], device_id=to, device_id_type=MESH_ID)

        first = [copy(0, me, sibling, src=x_ref)]
        first += [copy(1 + j, me, (*chip, c), src=x_ref) for j, chip in enumerate(chips)]
        for cp in first:
            cp.start()
        gbuf[4 * x + 2 * y + c] = x_ref[...]
        passed = [copy(4 + j, (*chip, c), sibling) for j, chip in enumerate(chips)]
        for j, chip in enumerate(chips):
            copy(1 + j, (*chip, c), me).wait_recv()
            passed[j].start()
        copy(0, sibling, me).wait_recv()
        for j, chip in enumerate(chips):
            copy(4 + j, (*chip, 1 - c), me).wait_recv()
        for cp in first + passed:
            cp.wait_send()
        acc = gbuf[0]
        for k in range(1, N_DEV):
            acc = acc + gbuf[k]
        out_ref[...] = acc

    return pl.pallas_call(
        body, name=name, out_shape=jax.ShapeDtypeStruct((rows, cols), F32),
        in_specs=[pl.BlockSpec(memory_space=pltpu.VMEM)], out_specs=pl.BlockSpec(memory_space=pltpu.VMEM),
        scratch_shapes=[pltpu.VMEM((N_DEV, rows, cols), F32), pltpu.SemaphoreType.DMA((7,)), pltpu.SemaphoreType.DMA((7,))],
        compiler_params=pltpu.CompilerParams(has_side_effects=True, vmem_limit_bytes=VMEM_LIMIT_BYTES),
    )(vec)


_COL_POOL, _COL_Z, _COL_XBC, _COL_DT, _COL_Q, _COL_K, _COL_V = 0, 256, 640, 1536, 1542, 1926, 2310
_H_SEGMENTS = ((_COL_XBC + SSD_WIDTH, 512), (_COL_POOL, 256), (_COL_Q, 384), (_COL_K, 384), (_COL_V, 384),
               (_COL_Z, 384), (_COL_XBC, 384), (_COL_DT, 6))


def _h_from_orig(w):
    parts = [w[..., o:o + n] for o, n in _H_SEGMENTS]
    pad = jnp.zeros(w.shape[:-1] + (H_COLS - IN_COLS,), w.dtype)
    return jnp.concatenate(parts + [pad], axis=-1)


def _h_to_orig(w):
    offs, o = {}, 0
    for orig, n in _H_SEGMENTS:
        offs[orig] = (o, n)
        o += n
    order = sorted(offs)
    return jnp.concatenate([w[..., offs[k][0]:offs[k][0] + offs[k][1]] for k in order], axis=-1)


def _interleave(w):
    lead = w.shape[:-1]
    nt = D_FF // GLU_TILE
    return jnp.swapaxes(w.reshape(lead + (2, nt, GLU_TILE)), -3, -2).reshape(lead + (2 * D_FF,))


def _deinterleave(w):
    lead = w.shape[:-1]
    nt = D_FF // GLU_TILE
    return jnp.swapaxes(w.reshape(lead + (nt, 2, GLU_TILE)), -3, -2).reshape(lead + (2 * D_FF,))


def _up_segments():
    segs = []
    for j in range(N_DEV):
        half, base = j // 4, 704 * (j % 4)
        c = base
        while c < base + 704:
            t, r = divmod(c, GLU_TILE)
            n = min(GLU_TILE - r, base + 704 - c)
            segs.append((j, c - base, 2 * GLU_TILE * t + GLU_TILE * half + r, n))
            c += n
    return segs


def _up_to_interleaved(w, name, tr=256):
    def body(i_ref, o_ref):
        for j, src, dst, n in _up_segments():
            o_ref[:, dst:dst + n] = i_ref[j, :, src:src + n]

    return pl.pallas_call(
        body, name=name, grid=(D_MODEL // tr,), in_specs=[pl.BlockSpec((N_DEV, tr, 704), lambda r: (0, r, 0))],
        out_specs=pl.BlockSpec((tr, 2 * D_FF), lambda r: (r, 0)),
        out_shape=jax.ShapeDtypeStruct((D_MODEL, 2 * D_FF), w.dtype), compiler_params=_params(1))(w)


def _up_from_interleaved(g, name, tr=128):
    def body(i_ref, o_ref):
        for j, src, dst, n in _up_segments():
            o_ref[j, :, src:src + n] = i_ref[:, dst:dst + n]

    return pl.pallas_call(
        body, name=name, grid=(D_MODEL // tr,), in_specs=[pl.BlockSpec((tr, 2 * D_FF), lambda r: (r, 0))],
        out_specs=pl.BlockSpec((N_DEV, tr, 704), lambda r: (0, r, 0)),
        out_shape=jax.ShapeDtypeStruct((N_DEV, D_MODEL, 704), g.dtype), compiler_params=_params(1))(g)


def _mix_rows_from_orig(w):
    return jnp.concatenate([w[256:640], w[640:1024], w[0:256]], axis=0)


def _mix_rows_to_orig(w):
    return jnp.concatenate([w[768:1024], w[0:384], w[384:768]], axis=0)


def _xbc_split(w):
    return w[..., :SSD_WIDTH], w[..., SSD_WIDTH:]


def _layer_fwd(x, p_l, wt, sp, nb, s, comm=None):
    h = _mm(x, wt["w_in"], "nn", F32, "mm_in", tm=1024, tn=1408)
    pool_out = _pool_fwd(h, wt["pool_bd"], sp["pool_scale"], nb, s, "pool_fwd")
    ssd_out, hstate = _ssd_fwd(h, sp["cwx"], sp["cbx"], sp["cwb"], sp["cbb"], sp["ssd_vec"], sp["ssd_norm_w"], nb, s, "ssd_fwd")
    sb_out, comm_out = _sb_fwd(h, nb, s, "sb_fwd" if comm is None else "sb_fwd_gather", comm)
    mixcat = jnp.concatenate([ssd_out, sb_out, pool_out], axis=1)
    mix = _mm(mixcat, wt["w_out"], "nn", F32, "mm_out", tm=1024, tn=1024)
    x1, r1 = _ln_fwd(x, mix, sp["ln1"], "ln1_fwd")
    up = _mm(x1, wt["w_up"], "nn", F32, "mm_up", tm=1024, tn=1408)
    act = _glu_fwd(up, sp["ffn_cw"], sp["ffn_cb"], nb, s, "glu_fwd")
    ffn = _mm(act, wt["w_down"], "nn", F32, "mm_down", tm=1024, tn=1024, tk=1408)
    gp = _mm(x1, wt["w_gate"], "nn", F32, "mm_gate", tm=1024, tn=1024)
    pp = _mm(p_l, wt["w_proj"], "nn", F32, "mm_proj", tm=2048, tn=1024)
    x2, r2 = _ln_fwd(x1, ffn, sp["ln2"], "ln2_fwd", gp=gp, pp=pp)
    return x2, dict(x=x, h=h, hstate=hstate, mixcat=mixcat, r1=r1, x1=x1, up=up, act=act, gp=gp, pp=pp, r2=r2), comm_out


def _layer_bwd(dx2, p_l, sv, wt, sp, nb, s, comm=None):
    dr2, dgp, dpp, st2 = _ln_bwd(sv["r2"], sp["ln2"], dx2, "ln2_bwd", gp=sv["gp"], pp=sv["pp"])
    g_down = _mm(sv["act"], dr2, "tn", F32, "wg_down", tm=1408, tn=1024, tk=512)
    dact = _mm(dr2, wt["w_down"], "nt", F32, "dg_down", tm=1024, tn=1408)
    dup, ffn_acc = _glu_bwd(sv["up"], dact, sp["ffn_cw"], sp["ffn_cb"], nb, s, "glu_bwd")
    g_up = _mm(sv["x1"], dup, "tn", F32, "wg_up", tm=1024, tn=2816, tk=512)
    g_gate = _mm(sv["x1"], dgp, "tn", F32, "wg_gate", tm=1024, tn=1024, tk=512)
    g_proj = _mm(p_l, dpp, "tn", F32, "wg_proj", tm=256, tn=1024, tk=512)
    t1 = _mm(dgp, wt["w_gate"], "nt", F32, "dg_gate", tm=1024, tn=1024, add=dr2, add_coef=ALPHA)
    dx1 = _mm(dup, wt["w_up"], "nt", F32, "dg_up", tm=1024, tn=1024, tk=1408, add=t1)
    dr1, st1 = _ln_bwd(sv["r1"], sp["ln1"], dx1, "ln1_bwd")
    g_out = _mm(sv["mixcat"], dr1, "tn", F32, "wg_out", tm=1024, tn=1024, tk=512)
    dmix = _mm(dr1, wt["w_out"], "nt", F32, "dg_out", tm=1024, tn=1024)
    du, g_pool_bd, g_pool_scale = _pool_bwd(sv["h"], dmix, wt["pool_bd"], sp["pool_scale"], nb, s, "pool_bwd")
    dz, dxs, dbc, ddt, gx, gb, gv, gn = _ssd_bwd(sv["h"], sv["hstate"], dmix, sp["cwx"], sp["cbx"], sp["cwb"], sp["cbb"],
                                                  sp["ssd_vec"], sp["ssd_norm_w"], nb, s, "ssd_bwd")
    ready = dict(w_out=g_out, ffn_w_up=g_up, ffn_w_down=g_down, ple_w_gate=g_gate, ple_w_proj=g_proj)
    job = comm(ready) if comm is not None else None
    dq, dk, dv, comm_out = _sb_bwd(sv["h"], dmix, nb, s, "sb_bwd" if job is None else "sb_bwd_x%d" % job["n_xfers"], job)
    dh = jnp.concatenate([dbc, du, dq, dk, dv, dz, dxs, ddt], axis=1)
    g_in = _mm(sv["x"], dh, "tn", F32, "wg_in", tm=1024, tn=2816, tk=512)
    dx = _mm(dh, wt["w_in"], "nt", F32, "dg_in", tm=1024, tn=1024, tk=1408, add=dr1, add_coef=ALPHA)
    small = dict(
        pool_w=jnp.stack([g_pool_bd[HEAD_DIM * g:HEAD_DIM * (g + 1), HEAD_DIM * g:HEAD_DIM * (g + 1)] for g in range(4)]),
        pool_scale=g_pool_scale[0],
        ssd_conv_w=jnp.concatenate([gx[0:4], gb[0:4]], axis=1),
        ssd_conv_b=jnp.concatenate([gx[4], gb[4]], axis=0),
        ssd_dt_bias=gv[0, :SSD_HEADS],
        ssd_a_log=gv[1, :SSD_HEADS] * sp["ssd_vec"][1, :SSD_HEADS],
        ssd_d=gv[2, :SSD_HEADS],
        ssd_norm_w=gn[0],
        ln1_g=st1[0], ln1_b=st1[1], ln2_g=st2[0], ln2_b=st2[1],
        ffn_conv_w=_deinterleave(ffn_acc[0:3]),
        ffn_conv_b=_deinterleave(ffn_acc[3]),
    )
    return dx, dict(ready, w_in=g_in), small, comm_out


def _layer_params(i, big, rep):
    pool_bd = jnp.zeros((POOL_WIDTH, POOL_WIDTH), F32)
    for g in range(4):
        pool_bd = lax.dynamic_update_slice(pool_bd, rep["pool_w"][i, g], (HEAD_DIM * g, HEAD_DIM * g))
    wt = dict(w_in=big["w_in"], w_out=big["w_out"], w_up=big["ffn_w_up"], w_down=big["ffn_w_down"],
              w_gate=big["ple_w_gate"], w_proj=big["ple_w_proj"], pool_bd=pool_bd.astype(MXU_DTYPE))
    cwx, cwb = _xbc_split(rep["ssd_conv_w"][i])
    cbx, cbb = _xbc_split(rep["ssd_conv_b"][i][None, :])
    vec = jnp.zeros((SUBLANES, LANES), F32)
    vec = vec.at[0, :SSD_HEADS].set(rep["ssd_dt_bias"][i])
    vec = vec.at[1, :SSD_HEADS].set(-jnp.exp(rep["ssd_a_log"][i]))
    vec = vec.at[2, :SSD_HEADS].set(rep["ssd_d"][i])
    sp = dict(pool_scale=rep["pool_scale"][i][None, :], cwx=cwx, cbx=cbx, cwb=cwb, cbb=cbb, ssd_vec=vec,
              ssd_norm_w=rep["ssd_norm_w"][i][None, :],
              ln1=jnp.stack([rep["ln1_g"][i], rep["ln1_b"][i]]), ln2=jnp.stack([rep["ln2_g"][i], rep["ln2_b"][i]]),
              ffn_cw=_interleave(rep["ffn_conv_w"][i]), ffn_cb=_interleave(rep["ffn_conv_b"][i][None, :]))
    return wt, sp


def _run_layers(x, p, target, big_w, rep, fwd_job=None, fwd_done=None, bwd_job=None, bwd_done=None):
    nb, s, d = x.shape
    t = nb * s
    xf = x.reshape(t, d)
    saved, params = [], []
    for i in range(DEPTH):
        wt, sp = _layer_params(i, big_w[i], rep)
        params.append((wt, sp))
        job = fwd_job(i) if fwd_job is not None else None
        xf, sv, res = _layer_fwd(xf, p[i].reshape(t, PLE_DIM), wt, sp, nb, s, job)
        if job is not None:
            fwd_done(i, res)
        saved.append(sv)
    dy, loss = _loss_grad(xf, target.reshape(t, d), "loss")
    bigs, smalls = [None] * DEPTH, [None] * DEPTH
    for i in reversed(range(DEPTH)):
        wt, sp = params[i]
        job = (lambda ready, i=i: bwd_job(i, bigs, ready)) if bwd_job is not None else None
        dy, bigs[i], smalls[i], res = _layer_bwd(dy, p[i].reshape(t, PLE_DIM), saved[i], wt, sp, nb, s, job)
        if job is not None:
            bwd_done(i, res)
    return loss, dy.reshape(nb, s, d), bigs, smalls


def _local_step(x, p, target, full, rep):
    return _run_layers(x, p, target, [{n: full[n][i] for n in full} for i in range(DEPTH)], rep)


BIG = ("w_in", "w_out", "ffn_w_up", "ffn_w_down", "ple_w_gate", "ple_w_proj")
SMALL_REPLICATED = ("pool_w", "pool_scale", "ssd_conv_b", "ssd_dt_bias", "ssd_a_log", "ssd_d", "ssd_norm_w",
                    "ln1_g", "ln1_b", "ffn_conv_b", "ln2_g", "ln2_b")
SMALL_SHARDED = ("ssd_conv_w", "ffn_conv_w")
WEIGHTS = ("w_in", "pool_w", "pool_scale", "ssd_conv_w", "ssd_conv_b", "ssd_dt_bias", "ssd_a_log", "ssd_d", "ssd_norm_w",
           "w_out", "ln1_g", "ln1_b", "ffn_w_up", "ffn_conv_w", "ffn_conv_b", "ffn_w_down", "ln2_g", "ln2_b",
           "ple_w_gate", "ple_w_proj")
SUM_BLOCK_BYTES = 3 * 512 * 1024


def _to_rows(a, cols):
    f = a.reshape(-1)
    pad = (-f.shape[0]) % cols
    if pad:
        f = jnp.concatenate([f, jnp.zeros((pad,), f.dtype)])
    return f.reshape(-1, cols)


def _pack_rows(arrs, cols, row_mult):
    rows = [_to_rows(a, cols) for a in arrs]
    flat = jnp.concatenate(rows, axis=0)
    pad = (-flat.shape[0]) % row_mult
    if pad:
        flat = jnp.concatenate([flat, jnp.zeros((pad, cols), flat.dtype)], axis=0)
    return flat


def _unpack_rows(flat, shapes, cols):
    out, r = [], 0
    for shp in shapes:
        n = 1
        for v in shp:
            n *= v
        nr = -(-n // cols)
        out.append(flat[r:r + nr].reshape(-1)[:n].reshape(shp))
        r += nr
    return out


def kernel(x, p, w_in, pool_w, pool_scale, ssd_conv_w, ssd_conv_b, ssd_dt_bias, ssd_a_log, ssd_d, ssd_norm_w, w_out, ln1_g, ln1_b, ffn_w_up, ffn_conv_w, ffn_conv_b, ffn_w_down, ln2_g, ln2_b, ple_w_gate, ple_w_proj, loss_target, m_w_in, m_pool_w, m_pool_scale, m_ssd_conv_w, m_ssd_conv_b, m_ssd_dt_bias, m_ssd_a_log, m_ssd_d, m_ssd_norm_w, m_w_out, m_ln1_g, m_ln1_b, m_ffn_w_up, m_ffn_conv_w, m_ffn_conv_b, m_ffn_w_down, m_ln2_g, m_ln2_b, m_ple_w_gate, m_ple_w_proj, v_w_in, v_pool_w, v_pool_scale, v_ssd_conv_w, v_ssd_conv_b, v_ssd_dt_bias, v_ssd_a_log, v_ssd_d, v_ssd_norm_w, v_w_out, v_ln1_g, v_ln1_b, v_ffn_w_up, v_ffn_conv_w, v_ffn_conv_b, v_ffn_w_down, v_ln2_g, v_ln2_b, v_ple_w_gate, v_ple_w_proj):
    wts = dict(w_in=w_in, pool_w=pool_w, pool_scale=pool_scale, ssd_conv_w=ssd_conv_w, ssd_conv_b=ssd_conv_b,
               ssd_dt_bias=ssd_dt_bias, ssd_a_log=ssd_a_log, ssd_d=ssd_d, ssd_norm_w=ssd_norm_w, w_out=w_out, ln1_g=ln1_g,
               ln1_b=ln1_b, ffn_w_up=ffn_w_up, ffn_conv_w=ffn_conv_w, ffn_conv_b=ffn_conv_b, ffn_w_down=ffn_w_down,
               ln2_g=ln2_g, ln2_b=ln2_b, ple_w_gate=ple_w_gate, ple_w_proj=ple_w_proj)
    mom_m = dict(w_in=m_w_in, pool_w=m_pool_w, pool_scale=m_pool_scale, ssd_conv_w=m_ssd_conv_w, ssd_conv_b=m_ssd_conv_b,
                 ssd_dt_bias=m_ssd_dt_bias, ssd_a_log=m_ssd_a_log, ssd_d=m_ssd_d, ssd_norm_w=m_ssd_norm_w, w_out=m_w_out,
                 ln1_g=m_ln1_g, ln1_b=m_ln1_b, ffn_w_up=m_ffn_w_up, ffn_conv_w=m_ffn_conv_w, ffn_conv_b=m_ffn_conv_b,
                 ffn_w_down=m_ffn_w_down, ln2_g=m_ln2_g, ln2_b=m_ln2_b, ple_w_gate=m_ple_w_gate, ple_w_proj=m_ple_w_proj)
    mom_v = dict(w_in=v_w_in, pool_w=v_pool_w, pool_scale=v_pool_scale, ssd_conv_w=v_ssd_conv_w, ssd_conv_b=v_ssd_conv_b,
                 ssd_dt_bias=v_ssd_dt_bias, ssd_a_log=v_ssd_a_log, ssd_d=v_ssd_d, ssd_norm_w=v_ssd_norm_w, w_out=v_w_out,
                 ln1_g=v_ln1_g, ln1_b=v_ln1_b, ffn_w_up=v_ffn_w_up, ffn_conv_w=v_ffn_conv_w, ffn_conv_b=v_ffn_conv_b,
                 ffn_w_down=v_ffn_w_down, ln2_g=v_ln2_g, ln2_b=v_ln2_b, ple_w_gate=v_ple_w_gate, ple_w_proj=v_ple_w_proj)
    me = 4 * lax.axis_index("x") + 2 * lax.axis_index("y") + lax.axis_index("c")

    def layer_shards(i):
        sh = {n: wts[n][i].astype(MXU_DTYPE) for n in BIG}
        sh["w_in"] = _h_from_orig(wts["w_in"][i]).astype(MXU_DTYPE)
        return sh

    def gathered_weights(res):
        big = dict(zip(BIG, res[:len(BIG)]))
        big["ffn_w_up"] = _up_to_interleaved(big["ffn_w_up"], "up_to_interleaved")
        return big

    res0 = _comm_call(_gather_job(layer_shards(0), [wts[n] for n in SMALL_SHARDED]), "gather_layer0")
    big_w = [gathered_weights(res0)] + [None] * (DEPTH - 1)
    rep = {n: wts[n] for n in SMALL_REPLICATED}
    for n, g in zip(SMALL_SHARDED, res0[len(BIG):]):
        rep[n] = jnp.transpose(g, (1, 2, 0, 3)).reshape(g.shape[1], g.shape[2], N_DEV * g.shape[3])

    def fwd_job(i):
        return _gather_job(layer_shards(i + 1)) if i + 1 < DEPTH else None

    def fwd_done(i, res):
        big_w[i + 1] = gathered_weights(res)

    received = [dict() for _ in range(DEPTH)]
    carried = ("w_out", "ffn_w_up", "ffn_w_down", "ple_w_gate", "ple_w_proj")

    def bwd_items(i, bigs, ready):
        items = [(i, n, ready[n]) for n in carried] + ([(i + 1, "w_in", bigs[i + 1]["w_in"])] if i + 1 < DEPTH else [])
        return [(l, n, _up_from_interleaved(g, "up_from_interleaved") if n == "ffn_w_up" else g) for l, n, g in items]

    pending = {}

    def bwd_job(i, bigs, ready):
        pending[i] = bwd_items(i, bigs, ready)
        return _exchange_job([(n, g) for _, n, g in pending[i]])

    def bwd_done(i, res):
        for (l, n, _), r in zip(pending[i], res):
            received[l][n] = r

    loss_loc, grad_x, bigs, smalls = _run_layers(x, p, loss_target, big_w, rep, fwd_job, fwd_done, bwd_job, bwd_done)
    received[0]["w_in"] = _comm_call(_exchange_job([("w_in", bigs[0]["w_in"])]), "exchange_w_in0")[0]

    grads = {}
    for n in BIG:
        parts = [received[i][n] for i in range(DEPTH)]
        _, rows, cols = parts[0].shape
        tr = next(t for t in (256, 128, 64, 32, 16, 8) if rows % t == 0 and N_DEV * t * cols * 4 <= SUM_BLOCK_BYTES)
        g = _sum8_layers(parts, "sum_" + n, tr)
        grads[n] = _h_to_orig(g) if n == "w_in" else g
    small_names = SMALL_REPLICATED + SMALL_SHARDED
    small_full_shapes = [rep[n].shape for n in small_names]
    small_vec = _pack_rows([jnp.stack([smalls[i][n] for i in range(DEPTH)]) for n in small_names] + [loss_loc[0, :1]],
                           LANES, SUBLANES)
    small_sum = _all_reduce_small(small_vec, "allreduce_small")
    small_out = _unpack_rows(small_sum, small_full_shapes + [(1,)], LANES)
    loss = small_out[-1][0]
    for n, g in zip(small_names, small_out[:-1]):
        if n in SMALL_SHARDED:
            width = wts[n].shape[-1]
            g = lax.dynamic_slice_in_dim(g, me * width, width, axis=g.ndim - 1)
        grads[n] = g

    delta, new_m, new_v = {}, {}, {}
    for n in BIG:
        shp = wts[n].shape
        two_d = lambda a: a.reshape(-1, shp[-1])
        tr = {"w_in": 128, "ffn_w_down": 352}.get(n, 256)
        d_, m_, v_ = _adamw(two_d(wts[n]), two_d(grads[n]), two_d(mom_m[n]), two_d(mom_v[n]), "adamw_" + n, tr=tr)
        delta[n], new_m[n], new_v[n] = d_.reshape(shp), m_.reshape(shp), v_.reshape(shp)
    packs = [_pack_rows([src[n] for n in small_names], LANES, SUBLANES) for src in (wts, grads, mom_m, mom_v)]
    outs = _adamw(*packs, "adamw_small", tr=packs[0].shape[0])
    shapes = [wts[n].shape for n in small_names]
    for dst, flat in zip((delta, new_m, new_v), outs):
        for n, a in zip(small_names, _unpack_rows(flat, shapes, LANES)):
            dst[n] = a
    return (loss, grad_x, *[grads[n] for n in WEIGHTS], *[delta[n] for n in WEIGHTS],
            *[new_m[n] for n in WEIGHTS], *[new_v[n] for n in WEIGHTS])
```

```python
import functools

import jax
import jax.numpy as jnp
from jax import lax
from jax.experimental import pallas as pl
from jax.experimental.pallas import tpu as pltpu

F32 = jnp.float32
BF16 = jnp.bfloat16
MXU_DTYPE = jnp.bfloat16

D_MODEL = 1024
DEPTH = 4
PLE_DIM = 256
ALPHA = (2 * DEPTH) ** 0.25
LN_EPS = 1e-5
RMS_EPS = 1e-6
HEAD_DIM = 64
POOL_WIDTH = 256
POOL_WINDOWS = (2, 4, 8, 16)
SSD_WIDTH = 384
SSD_HEADS = 6
SSD_STATE = 128
SSD_XBC = 896
SB_WIDTH = 384
IN_COLS = 2694
D_FF = 2816
N_DEV = 8

ADAM_LR = 0.001
ADAM_B1 = 0.9
ADAM_B2 = 0.999
ADAM_EPS = 1e-08
ADAM_WD = 0.01
ADAM_STEP = 10

LANES = 128
SUBLANES = 8
VMEM_LIMIT_BYTES = 56 * 1024 * 1024

H_COLS = 2816
H_BC = 0
H_POOL = 512
H_Q = 768
H_K = 1152
H_V = 1536
H_Z = 1920
H_XS = 2304
H_DT = 2688
SSD_CHUNK = 512
QB = 256
GLU_TILE = 256
MASKED_LOG = -1e30

NN = ((1,), (0,))
NT = ((1,), (1,))
TN = ((0,), (0,))


def _dot(a, b, dims=NN):
    return lax.dot_general(a.astype(MXU_DTYPE), b.astype(MXU_DTYPE), (dims, ((), ())), preferred_element_type=F32)


def _dot_exact01(x, m01, dims=NN, x_left=True, terms=3):
    acc = None
    r = x
    for _ in range(terms):
        hi = r.astype(BF16)
        ops = (hi, m01) if x_left else (m01, hi)
        part = lax.dot_general(ops[0], ops[1], (dims, ((), ())), preferred_element_type=F32)
        acc = part if acc is None else acc + part
        r = r - hi.astype(F32)
    return acc


def _sigmoid(v):
    return 1.0 / (1.0 + jnp.exp(-v))


def _silu(v):
    return v * _sigmoid(v)


def _dsilu(v):
    s = _sigmoid(v)
    return s * (1.0 + v * (1.0 - s))


def _softplus(v):
    return jnp.maximum(v, 0.0) + jnp.log(1.0 + jnp.exp(-jnp.abs(v)))


def _params(n_axes, side_effects=False):
    return pltpu.CompilerParams(dimension_semantics=("arbitrary",) * n_axes, vmem_limit_bytes=VMEM_LIMIT_BYTES,
                                has_side_effects=side_effects)


MESH_ID = pl.DeviceIdType.MESH
_ANY = pl.BlockSpec(memory_space=pl.ANY)


def _flip(v, bit):
    return 1 - v if bit else v


def _comm_counts(comm):
    return (0, 0) if comm is None else (len(comm["inputs"]), len(comm["out_shapes"]))


def _comm_call_args(comm):
    if comm is None:
        return [], [], [], []
    n = comm["n_xfers"]
    sems = [pltpu.SemaphoreType.DMA(((N_DEV - 1) * n,)), pltpu.SemaphoreType.DMA(((N_DEV - 1) * n,)),
            pltpu.SemaphoreType.DMA((n,))]
    return list(comm["inputs"]), [_ANY] * len(comm["out_shapes"]), list(comm["out_shapes"]), sems


def _comm_descs(comm, in_refs, tail_refs, with_recvs=True):
    n_out = len(comm["out_shapes"])
    out_refs, (send_sems, recv_sems, local_sems) = tail_refs[:n_out], tail_refs[n_out:n_out + 3]
    xfers = comm["xfers"](in_refs, out_refs)
    n = len(xfers)
    assert n == comm["n_xfers"]
    x, y, c = lax.axis_index("x"), lax.axis_index("y"), lax.axis_index("c")
    me = 4 * x + 2 * y + c
    local = [pltpu.make_async_copy(src_for(me), dst_for(me), local_sems.at[t]) for t, (src_for, dst_for) in enumerate(xfers)]
    sends, recvs = [], []
    for k in range(1, N_DEV):
        pid = (_flip(x, k & 4), _flip(y, k & 2), _flip(c, k & 1))
        peer = 4 * pid[0] + 2 * pid[1] + pid[2]
        for t, (src_for, dst_for) in enumerate(xfers):
            idx = (k - 1) * n + t
            sends.append(pltpu.make_async_remote_copy(
                src_ref=src_for(peer), dst_ref=dst_for(me), send_sem=send_sems.at[idx], recv_sem=recv_sems.at[idx],
                device_id=pid, device_id_type=MESH_ID))
            if with_recvs:
                recvs.append(pltpu.make_async_remote_copy(
                    src_ref=src_for(peer), dst_ref=dst_for(peer), send_sem=send_sems.at[idx], recv_sem=recv_sems.at[idx],
                    device_id=pid, device_id_type=MESH_ID))
    return local, sends, recvs


def _comm_start(descs):
    local, sends, _ = descs
    for cp in local + sends:
        cp.start()


def _comm_wait(descs):
    local, sends, recvs = descs
    for cp in recvs:
        cp.wait_recv()
    for cp in sends:
        cp.wait_send()
    for cp in local:
        cp.wait()


def _comm_hosted(comm, in_refs, tail_refs, grid):
    if comm is None:
        return
    ids = [pl.program_id(a) for a in range(len(grid))]
    first = functools.reduce(jnp.logical_and, [i == 0 for i in ids])
    last = functools.reduce(jnp.logical_and, [i == g - 1 for i, g in zip(ids, grid)])

    @pl.when(first)
    def _():
        _comm_start(_comm_descs(comm, in_refs, tail_refs, with_recvs=False))

    @pl.when(last)
    def _():
        _comm_wait(_comm_descs(comm, in_refs, tail_refs))


def _comm_call(comm, name):
    n_in = len(comm["inputs"])

    def body(*refs):
        descs = _comm_descs(comm, refs[:n_in], refs[n_in:])
        _comm_start(descs)
        _comm_wait(descs)

    c_in, c_specs, c_shapes, c_scratch = _comm_call_args(comm)
    return pl.pallas_call(body, name=name, in_specs=[_ANY] * n_in, out_specs=c_specs, out_shape=c_shapes,
                          scratch_shapes=c_scratch, compiler_params=pltpu.CompilerParams(has_side_effects=True))(*c_in)


def _gather_call_two_level(comm, name):
    n_in, n_out = len(comm["inputs"]), len(comm["out_shapes"])

    def body(*refs):
        in_refs, out_refs = refs[:n_in], refs[n_in:n_in + n_out]
        send_sems, recv_sems, local_sems = refs[n_in + n_out:]
        xfers = comm["xfers"](in_refs, out_refs)
        x, y, c = lax.axis_index("x"), lax.axis_index("y"), lax.axis_index("c")
        pos = lambda px, py, pc: 4 * px + 2 * py + pc
        me, sibling = (x, y, c), (x, y, 1 - c)
        chips = [(1 - x, y), (x, 1 - y), (1 - x, 1 - y)]

        def copy(t, k, block, to, own):
            src_for, dst_for = xfers[t]
            return pltpu.make_async_remote_copy(
                src_ref=src_for(pos(*me)) if own else dst_for(pos(*block)), dst_ref=dst_for(pos(*block)),
                send_sem=send_sems.at[7 * t + k], recv_sem=recv_sems.at[7 * t + k], device_id=to, device_id_type=MESH_ID)

        nt = len(xfers)
        local = [pltpu.make_async_copy(xfers[t][0](pos(*me)), xfers[t][1](pos(*me)), local_sems.at[t]) for t in range(nt)]
        first = [copy(t, 0, me, sibling, True) for t in range(nt)]
        first += [copy(t, 1 + j, me, (*chip, c), True) for t in range(nt) for j, chip in enumerate(chips)]
        for cp in local + first:
            cp.start()
        passed = []
        for j, chip in enumerate(chips):
            for t in range(nt):
                copy(t, 1 + j, (*chip, c), me, False).wait_recv()
                fwd = copy(t, 4 + j, (*chip, c), sibling, False)
                fwd.start()
                passed.append(fwd)
        for t in range(nt):
            copy(t, 0, sibling, me, False).wait_recv()
            for j, chip in enumerate(chips):
                copy(t, 4 + j, (*chip, 1 - c), me, False).wait_recv()
        for cp in first + passed:
            cp.wait_send()
        for cp in local:
            cp.wait()

    n = comm["n_xfers"]
    return pl.pallas_call(
        body, name=name, in_specs=[_ANY] * n_in, out_specs=[_ANY] * n_out, out_shape=list(comm["out_shapes"]),
        scratch_shapes=[pltpu.SemaphoreType.DMA((7 * n,)), pltpu.SemaphoreType.DMA((7 * n,)), pltpu.SemaphoreType.DMA((n,))],
        compiler_params=pltpu.CompilerParams(has_side_effects=True))(*comm["inputs"])


def _rows(ref, j, n):
    return ref.at[pl.ds(pl.multiple_of(j * n, SUBLANES), n), :]


def _gather_job(sh, conv=None):
    conv = list(conv or [])
    sds = jax.ShapeDtypeStruct
    out_shapes = [sds((D_MODEL, H_COLS), MXU_DTYPE), sds((D_MODEL, D_MODEL), MXU_DTYPE), sds((N_DEV, D_MODEL, 704), MXU_DTYPE),
                  sds((D_FF, D_MODEL), MXU_DTYPE), sds((D_MODEL, D_MODEL), MXU_DTYPE), sds((PLE_DIM, D_MODEL), MXU_DTYPE)]
    out_shapes += [sds((N_DEV,) + a.shape, a.dtype) for a in conv]

    def xfers(ins, outs):
        whole = lambda a: (lambda j: a)
        r = [(whole(ins[0]), lambda j: _rows(outs[0], j, 128)),
             (whole(ins[1]), lambda j: _rows(outs[1], lax.rem(j + 6, N_DEV), 128)),
             (whole(ins[2]), lambda j: outs[2].at[j]),
             (whole(ins[3]), lambda j: _rows(outs[3], j, 352)),
             (whole(ins[4]), lambda j: _rows(outs[4], j, 128)),
             (whole(ins[5]), lambda j: outs[5].at[:, pl.ds(pl.multiple_of(j * LANES, LANES), LANES)])]
        for t in range(len(conv)):
            r.append((whole(ins[6 + t]), lambda j, o=outs[6 + t]: o.at[j]))
        return r

    return dict(inputs=[sh[n] for n in BIG] + conv, out_shapes=out_shapes, xfers=xfers, n_xfers=6 + len(conv))


_SHARD_SHAPES = {"w_in": (128, H_COLS), "w_out": (128, D_MODEL), "ffn_w_up": (D_MODEL, 704), "ffn_w_down": (352, D_MODEL),
                 "ple_w_gate": (128, D_MODEL), "ple_w_proj": (PLE_DIM, LANES)}


def _exchange_job(items):
    def source(name, ref):
        if name in ("w_in", "ple_w_gate"):
            return lambda j: _rows(ref, j, 128)
        if name == "w_out":
            return lambda j: _rows(ref, lax.rem(j + 6, N_DEV), 128)
        if name == "ffn_w_up":
            return lambda j: ref.at[j]
        if name == "ffn_w_down":
            return lambda j: _rows(ref, j, 352)
        assert name == "ple_w_proj"
        return lambda j: ref.at[:, pl.ds(pl.multiple_of(j * LANES, LANES), LANES)]

    def xfers(ins, outs):
        return [(source(name, i), lambda j, o=o: o.at[j]) for (name, _), i, o in zip(items, ins, outs)]

    return dict(inputs=[g for _, g in items], xfers=xfers, n_xfers=len(items),
                out_shapes=[jax.ShapeDtypeStruct((N_DEV,) + _SHARD_SHAPES[name], F32) for name, _ in items])


def _pick(n, pref):
    if n <= pref:
        return n
    for t in range(pref - pref % LANES, 0, -LANES):
        if n % t == 0:
            return t
    raise ValueError((n, pref))


def _mm(a, b, mode, out_dtype, name, tm=512, tn=512, tk=1024, add=None, add_coef=1.0):
    if mode == "nn":
        (m, k), (k2, n) = a.shape, b.shape
    elif mode == "nt":
        (m, k), (n, k2) = a.shape, b.shape
    else:
        (k, m), (k2, n) = a.shape, b.shape
    assert k == k2, (a.shape, b.shape, mode)
    tm, tn, tk = _pick(m, tm), _pick(n, tn), _pick(k, tk)
    nk = k // tk
    dims = {"nn": NN, "nt": NT, "tn": TN}[mode]

    def body(*refs):
        a_ref, b_ref = refs[:2]
        add_ref = refs[2] if add is not None else None
        o_ref = refs[3] if add is not None else refs[2]

        def finish(r):
            if add_ref is not None:
                r = r + add_coef * add_ref[...]
            o_ref[...] = r.astype(out_dtype)

        if nk == 1:
            finish(_dot(a_ref[...], b_ref[...], dims))
            return
        acc_ref = refs[-1]
        kk = pl.program_id(2)

        @pl.when(kk == 0)
        def _():
            acc_ref[...] = jnp.zeros_like(acc_ref)

        acc_ref[...] += _dot(a_ref[...], b_ref[...], dims)

        @pl.when(kk == nk - 1)
        def _():
            finish(acc_ref[...])

    if mode == "tn":
        a_spec = pl.BlockSpec((tk, tm), lambda i, j, kk: (kk, i))
    else:
        a_spec = pl.BlockSpec((tm, tk), lambda i, j, kk: (i, kk))
    if mode == "nt":
        b_spec = pl.BlockSpec((tn, tk), lambda i, j, kk: (j, kk))
    else:
        b_spec = pl.BlockSpec((tk, tn), lambda i, j, kk: (kk, j))
    o_spec = pl.BlockSpec((tm, tn), lambda i, j, kk: (i, j))
    in_specs = [a_spec, b_spec] + ([o_spec] if add is not None else [])
    args = (a, b) + ((add,) if add is not None else ())
    return pl.pallas_call(
        body, name=name, grid=(m // tm, n // tn, nk), in_specs=in_specs, out_specs=o_spec,
        out_shape=jax.ShapeDtypeStruct((m, n), out_dtype), scratch_shapes=[pltpu.VMEM((tm, tn), F32)] if nk > 1 else [],
        compiler_params=_params(3),
    )(*args)


def _ln_fwd(x, add, gb, name, gp=None, pp=None, tr=512):
    t, d = x.shape
    tr = _pick(t, tr)
    with_ple = gp is not None

    def body(*refs):
        if with_ple:
            x_ref, a_ref, gp_ref, pp_ref, gb_ref, y_ref, r_ref = refs
        else:
            x_ref, a_ref, gb_ref, y_ref, r_ref = refs
        r = ALPHA * x_ref[...] + a_ref[...]
        if with_ple:
            r = r + _sigmoid(gp_ref[...]) * pp_ref[...]
        mu = jnp.mean(r, axis=1, keepdims=True)
        xc = r - mu
        var = jnp.mean(xc * xc, axis=1, keepdims=True)
        y_ref[...] = xc * lax.rsqrt(var + LN_EPS) * gb_ref[0:1, :] + gb_ref[1:2, :]
        r_ref[...] = r

    row = pl.BlockSpec((tr, d), lambda i: (i, 0))
    vec = pl.BlockSpec((2, d), lambda i: (0, 0))
    n_row = 4 if with_ple else 2
    args = (x, add) + ((gp, pp) if with_ple else ()) + (gb,)
    return pl.pallas_call(
        body, name=name, grid=(t // tr,), in_specs=[row] * n_row + [vec], out_specs=[row, row],
        out_shape=[jax.ShapeDtypeStruct((t, d), F32)] * 2, compiler_params=_params(1),
    )(*args)


def _ln_bwd(r, gb, dy, name, gp=None, pp=None, tr=512):
    t, d = r.shape
    tr = _pick(t, tr)
    with_ple = gp is not None

    def body(*refs):
        if with_ple:
            r_ref, dy_ref, gp_ref, pp_ref, gb_ref, dr_ref, dgp_ref, dpp_ref, st_ref = refs
        else:
            r_ref, dy_ref, gb_ref, dr_ref, st_ref = refs
        i = pl.program_id(0)

        @pl.when(i == 0)
        def _():
            st_ref[...] = jnp.zeros_like(st_ref)

        rv = r_ref[...]
        dy_v = dy_ref[...]
        mu = jnp.mean(rv, axis=1, keepdims=True)
        xc = rv - mu
        var = jnp.mean(xc * xc, axis=1, keepdims=True)
        rstd = lax.rsqrt(var + LN_EPS)
        xhat = xc * rstd
        dxh = dy_v * gb_ref[0:1, :]
        m1 = jnp.mean(dxh, axis=1, keepdims=True)
        m2 = jnp.mean(dxh * xhat, axis=1, keepdims=True)
        dr = rstd * (dxh - m1 - xhat * m2)
        dr_ref[...] = dr
        rid = lax.broadcasted_iota(jnp.int32, (2, d), 0)
        dg = jnp.sum(dy_v * xhat, axis=0, keepdims=True)
        db = jnp.sum(dy_v, axis=0, keepdims=True)
        st_ref[...] += jnp.where(rid == 0, dg, db)
        if with_ple:
            sg = _sigmoid(gp_ref[...])
            ppv = pp_ref[...]
            dgp_ref[...] = (dr * ppv * sg * (1.0 - sg)).astype(dgp_ref.dtype)
            dpp_ref[...] = (dr * sg).astype(dpp_ref.dtype)

    row = pl.BlockSpec((tr, d), lambda i: (i, 0))
    vec = pl.BlockSpec((2, d), lambda i: (0, 0))
    if with_ple:
        in_specs, args = [row] * 4 + [vec], (r, dy, gp, pp, gb)
        out_specs = [row, row, row, vec]
        out_shape = [jax.ShapeDtypeStruct((t, d), F32), jax.ShapeDtypeStruct((t, d), MXU_DTYPE),
                     jax.ShapeDtypeStruct((t, d), MXU_DTYPE), jax.ShapeDtypeStruct((2, d), F32)]
    else:
        in_specs, args = [row] * 2 + [vec], (r, dy, gb)
        out_specs = [row, vec]
        out_shape = [jax.ShapeDtypeStruct((t, d), F32), jax.ShapeDtypeStruct((2, d), F32)]
    return pl.pallas_call(body, name=name, grid=(t // tr,), in_specs=in_specs, out_specs=out_specs,
                          out_shape=out_shape, compiler_params=_params(1))(*args)


def _loss_grad(y, target, name, tr=512):
    t, d = y.shape
    tr = _pick(t, tr)

    def body(y_ref, t_ref, dy_ref, l_ref):
        i = pl.program_id(0)

        @pl.when(i == 0)
        def _():
            l_ref[...] = jnp.zeros_like(l_ref)

        e = y_ref[...] - t_ref[...]
        dy_ref[...] = e * (1.0 / d)
        per_tok = jnp.mean(e * e, axis=1, keepdims=True)
        l_ref[...] += 0.5 * jnp.sum(per_tok, axis=0, keepdims=True)

    row = pl.BlockSpec((tr, d), lambda i: (i, 0))
    acc = pl.BlockSpec((SUBLANES, LANES), lambda i: (0, 0))
    return pl.pallas_call(body, name=name, grid=(t // tr,), in_specs=[row, row], out_specs=[row, acc],
                          out_shape=[jax.ShapeDtypeStruct((t, d), F32), jax.ShapeDtypeStruct((SUBLANES, LANES), F32)],
                          compiler_params=_params(1))(y, target)


def _shift_down(v, k, row):
    return jnp.where(row >= k, pltpu.roll(v, k, 0), 0.0)


def _shift_up(v, k, row):
    n = v.shape[0]
    return jnp.where(row < n - k, pltpu.roll(v, n - k, 0), 0.0)


def _pool_window(lane):
    grp = lane // HEAD_DIM
    return jnp.where(grp == 0, POOL_WINDOWS[0], jnp.where(grp == 1, POOL_WINDOWS[1],
                     jnp.where(grp == 2, POOL_WINDOWS[2], POOL_WINDOWS[3])))


def _pool_select(lane, s2, s4, s8, s16):
    grp = lane // HEAD_DIM
    return jnp.where(grp == 0, s2, jnp.where(grp == 1, s4, jnp.where(grp == 2, s8, s16)))


def _pooled(u, row, lane):
    s2 = u + _shift_down(u, 1, row)
    s4 = s2 + _shift_down(s2, 2, row)
    s8 = s4 + _shift_down(s4, 4, row)
    s16 = s8 + _shift_down(s8, 8, row)
    cnt = jnp.minimum(row + 1, _pool_window(lane)).astype(F32)
    return _pool_select(lane, s2, s4, s8, s16) / cnt - u, cnt


def _pool_fwd(h, wbd, scale, nb, s, name):
    def body(u_ref, w_ref, sc_ref, o_ref):
        u = u_ref[...]
        row = lax.broadcasted_iota(jnp.int32, u.shape, 0)
        lane = lax.broadcasted_iota(jnp.int32, u.shape, 1)
        pooled, _ = _pooled(u, row, lane)
        o_ref[...] = (_dot(pooled, w_ref[...]) * sc_ref[...]).astype(o_ref.dtype)

    wb = POOL_WIDTH
    return pl.pallas_call(
        body, name=name, grid=(nb,),
        in_specs=[pl.BlockSpec((s, wb), lambda b: (b, H_POOL // wb)), pl.BlockSpec((wb, wb), lambda b: (0, 0)),
                  pl.BlockSpec((1, wb), lambda b: (0, 0))],
        out_specs=pl.BlockSpec((s, wb), lambda b: (b, 0)),
        out_shape=jax.ShapeDtypeStruct((nb * s, wb), MXU_DTYPE), compiler_params=_params(1),
    )(h, wbd, scale)


def _pool_bwd(h, dmix, wbd, scale, nb, s, name):
    wb = POOL_WIDTH

    def body(u_ref, do_ref, w_ref, sc_ref, du_ref, dw_ref, ds_ref):
        b = pl.program_id(0)

        @pl.when(b == 0)
        def _():
            dw_ref[...] = jnp.zeros_like(dw_ref)
            ds_ref[...] = jnp.zeros_like(ds_ref)

        u = u_ref[...]
        row = lax.broadcasted_iota(jnp.int32, u.shape, 0)
        lane = lax.broadcasted_iota(jnp.int32, u.shape, 1)
        pooled, cnt = _pooled(u, row, lane)
        mixed = _dot(pooled, w_ref[...])
        do = do_ref[...]
        ds_ref[...] += jnp.sum(do * mixed, axis=0, keepdims=True)
        dm = do * sc_ref[...]
        dw_ref[...] += _dot(pooled, dm, TN)
        dpool = _dot(dm, w_ref[...], NT)
        qv = dpool / cnt
        f2 = qv + _shift_up(qv, 1, row)
        f4 = f2 + _shift_up(f2, 2, row)
        f8 = f4 + _shift_up(f4, 4, row)
        f16 = f8 + _shift_up(f8, 8, row)
        du_ref[...] = (_pool_select(lane, f2, f4, f8, f16) - dpool).astype(du_ref.dtype)

    return pl.pallas_call(
        body, name=name, grid=(nb,),
        in_specs=[pl.BlockSpec((s, wb), lambda b: (b, H_POOL // wb)), pl.BlockSpec((s, wb), lambda b: (b, 3)),
                  pl.BlockSpec((wb, wb), lambda b: (0, 0)), pl.BlockSpec((1, wb), lambda b: (0, 0))],
        out_specs=[pl.BlockSpec((s, wb), lambda b: (b, 0)), pl.BlockSpec((wb, wb), lambda b: (0, 0)),
                   pl.BlockSpec((1, wb), lambda b: (0, 0))],
        out_shape=[jax.ShapeDtypeStruct((nb * s, wb), MXU_DTYPE), jax.ShapeDtypeStruct((wb, wb), F32),
                   jax.ShapeDtypeStruct((1, wb), F32)],
        compiler_params=_params(1),
    )(h, dmix, wbd, scale)


def _glu_conv(x, w_ref, b_ref, row):
    return (b_ref[...] + w_ref[2:3, :] * x + w_ref[1:2, :] * _shift_down(x, 1, row)
            + w_ref[0:1, :] * _shift_down(x, 2, row))


def _glu_fwd(up, cw, cb, nb, s, name):
    wt = 2 * GLU_TILE
    nt = up.shape[1] // wt

    def body(u_ref, w_ref, b_ref, o_ref):
        x = u_ref[...]
        row = lax.broadcasted_iota(jnp.int32, x.shape, 0)
        c = _glu_conv(x, w_ref, b_ref, row)
        o_ref[...] = (_silu(c[:, :GLU_TILE]) * c[:, GLU_TILE:]).astype(o_ref.dtype)

    return pl.pallas_call(
        body, name=name, grid=(nt, nb),
        in_specs=[pl.BlockSpec((s, wt), lambda j, b: (b, j)), pl.BlockSpec((3, wt), lambda j, b: (0, j)),
                  pl.BlockSpec((1, wt), lambda j, b: (0, j))],
        out_specs=pl.BlockSpec((s, GLU_TILE), lambda j, b: (b, j)),
        out_shape=jax.ShapeDtypeStruct((nb * s, nt * GLU_TILE), MXU_DTYPE), compiler_params=_params(2),
    )(up, cw, cb)


def _glu_bwd(up, dact, cw, cb, nb, s, name):
    wt = 2 * GLU_TILE
    nt = up.shape[1] // wt

    def body(u_ref, da_ref, w_ref, b_ref, du_ref, acc_ref):
        b = pl.program_id(1)

        @pl.when(b == 0)
        def _():
            acc_ref[...] = jnp.zeros_like(acc_ref)

        x = u_ref[...]
        row = lax.broadcasted_iota(jnp.int32, x.shape, 0)
        x1 = _shift_down(x, 1, row)
        x2 = _shift_down(x, 2, row)
        c = b_ref[...] + w_ref[2:3, :] * x + w_ref[1:2, :] * x1 + w_ref[0:1, :] * x2
        gate, val = c[:, :GLU_TILE], c[:, GLU_TILE:]
        da = da_ref[...]
        dc = jnp.concatenate([da * val * _dsilu(gate), da * _silu(gate)], axis=1)
        dx = (w_ref[2:3, :] * dc + w_ref[1:2, :] * _shift_up(dc, 1, row) + w_ref[0:1, :] * _shift_up(dc, 2, row))
        du_ref[...] = dx.astype(du_ref.dtype)
        rid = lax.broadcasted_iota(jnp.int32, (SUBLANES, wt), 0)
        dw0 = jnp.sum(dc * x2, axis=0, keepdims=True)
        dw1 = jnp.sum(dc * x1, axis=0, keepdims=True)
        dw2 = jnp.sum(dc * x, axis=0, keepdims=True)
        db = jnp.sum(dc, axis=0, keepdims=True)
        acc_ref[...] += (jnp.where(rid == 0, dw0, 0.0) + jnp.where(rid == 1, dw1, 0.0)
                         + jnp.where(rid == 2, dw2, 0.0) + jnp.where(rid == 3, db, 0.0))

    return pl.pallas_call(
        body, name=name, grid=(nt, nb),
        in_specs=[pl.BlockSpec((s, wt), lambda j, b: (b, j)), pl.BlockSpec((s, GLU_TILE), lambda j, b: (b, j)),
                  pl.BlockSpec((3, wt), lambda j, b: (0, j)), pl.BlockSpec((1, wt), lambda j, b: (0, j))],
        out_specs=[pl.BlockSpec((s, wt), lambda j, b: (b, j)), pl.BlockSpec((SUBLANES, wt), lambda j, b: (0, j))],
        out_shape=[jax.ShapeDtypeStruct((nb * s, nt * wt), MXU_DTYPE), jax.ShapeDtypeStruct((SUBLANES, nt * wt), F32)],
        compiler_params=_params(2),
    )(up, dact, cw, cb)


def _sb_constants():
    row = lax.broadcasted_iota(jnp.int32, (QB, QB), 0)
    col = lax.broadcasted_iota(jnp.int32, (QB, QB), 1)
    return jnp.stack([row > col, row < col, col < row]).astype(BF16)


_SB_CONST_SPEC = pl.BlockSpec((3, QB, QB), lambda b, p, i: (0, 0, 0))


def _sb_fwd(h, nb, s, name, comm=None):
    nq = s // QB
    scale = HEAD_DIM ** -0.5
    n_in, n_out = _comm_counts(comm)

    def body(q_ref, k_ref, v_ref, tri_ref, *rest):
        o_ref = rest[n_in]
        i = pl.program_id(2)
        _comm_hosted(comm, rest[:n_in], rest[n_in + 1:], (nb, 3, nq))
        sls = [slice(hd * HEAD_DIM, (hd + 1) * HEAD_DIM) for hd in range(2)]
        qs = [(q_ref[:, sl] * scale).astype(MXU_DTYPE) for sl in sls]

        def scores(hd, j, diagonal=False):
            r0 = pl.multiple_of(j * QB, QB)
            z = _dot(qs[hd], k_ref[pl.ds(r0, QB), sls[hd]], NT)
            ln = -_softplus(z)
            ls = z + ln
            if diagonal:
                low = tri_ref[2] > 0
                ln = jnp.where(low, ln, 0.0)
                ls = jnp.where(low, ls, MASKED_LOG)
            return ls, _dot_exact01(ln, tri_ref[0], terms=2), jnp.sum(ln, axis=1, keepdims=True)

        def output(hd, j, ls, tl, ct):
            r0 = pl.multiple_of(j * QB, QB)
            return _dot(jnp.exp(ls + tl + ct), v_ref[pl.ds(r0, QB), sls[hd]])

        def group(blocks, carry, diagonal_first=False):
            sc = [[scores(hd, j, diagonal_first and n == 0) for n, j in enumerate(blocks)] for hd in range(2)]
            out = []
            for hd in range(2):
                a, c = carry[hd]
                for (ls, tl, sm), j in zip(sc[hd], blocks):
                    a = a + output(hd, j, ls, tl, c)
                    c = c + sm
                out.append((a, c))
            return tuple(out)

        start = (jnp.zeros((QB, HEAD_DIM), F32), jnp.zeros((QB, 1), F32))
        below = jnp.minimum(i, 1)
        left = i - below
        carry = lax.fori_loop(0, below, lambda t, c: group([i, i - 1], c, True), (start, start))
        carry = lax.fori_loop(0, 1 - below, lambda t, c: group([i], c, True), carry)
        carry = lax.fori_loop(0, left // 2, lambda t, c: group([left - 1 - 2 * t, left - 2 - 2 * t], c), carry)
        carry = lax.fori_loop(0, left % 2, lambda t, c: group([0], c), carry)
        o_ref[:, sls[0]] = carry[0][0].astype(o_ref.dtype)
        o_ref[:, sls[1]] = carry[1][0].astype(o_ref.dtype)

    qspec = lambda off: pl.BlockSpec((QB, LANES), lambda b, p, i: (b * nq + i, off // LANES + p))
    kvspec = lambda off: pl.BlockSpec((s, LANES), lambda b, p, i: (b, off // LANES + p))
    c_in, c_specs, c_shapes, c_scratch = _comm_call_args(comm)
    res = pl.pallas_call(
        body, name=name, grid=(nb, 3, nq), in_specs=[qspec(H_Q), kvspec(H_K), kvspec(H_V), _SB_CONST_SPEC] + [_ANY] * n_in,
        out_specs=[pl.BlockSpec((QB, LANES), lambda b, p, i: (b * nq + i, p))] + c_specs,
        out_shape=[jax.ShapeDtypeStruct((nb * s, SB_WIDTH), MXU_DTYPE)] + c_shapes, scratch_shapes=c_scratch,
        compiler_params=_params(3, comm is not None),
    )(h, h, h, _sb_constants(), *c_in)
    return res[0], res[1:]


def _sb_bwd(h, dmix, nb, s, name, comm=None):
    nq = s // QB
    scale = HEAD_DIM ** -0.5
    n_in, n_out = _comm_counts(comm)

    def body(q_ref, k_ref, v_ref, do_ref, tri_ref, *rest):
        dq_ref, dk_out, dv_out = rest[n_in:n_in + 3]
        p_buf, ls_buf, dk_ref, dv_ref = rest[n_in + 3 + n_out:n_in + 7 + n_out]
        i = pl.program_id(2)
        _comm_hosted(comm, rest[:n_in], rest[n_in + 3:n_in + 3 + n_out] + rest[n_in + 7 + n_out:], (nb, 3, nq))

        @pl.when(i == 0)
        def _():
            dk_ref[...] = jnp.zeros_like(dk_ref)
            dv_ref[...] = jnp.zeros_like(dv_ref)

        sls = [slice(hd * HEAD_DIM, (hd + 1) * HEAD_DIM) for hd in range(2)]
        q_raw = [q_ref[:, sl].astype(MXU_DTYPE) for sl in sls]
        qs = [(q_ref[:, sl] * scale).astype(MXU_DTYPE) for sl in sls]
        do = [do_ref[:, sl].astype(MXU_DTYPE) for sl in sls]

        def down_scores(hd, j, diagonal):
            r0 = pl.multiple_of(j * QB, QB)
            z = _dot(qs[hd], k_ref[pl.ds(r0, QB), sls[hd]], NT)
            ln = -_softplus(z)
            ls = z + ln
            if diagonal:
                low = tri_ref[2] > 0
                ln = jnp.where(low, ln, 0.0)
                ls = jnp.where(low, ls, MASKED_LOG)
            da = _dot(do[hd], v_ref[pl.ds(r0, QB), sls[hd]], NT)
            return ls, _dot_exact01(ln, tri_ref[0], terms=2), jnp.sum(ln, axis=1, keepdims=True), da

        def down_group(blocks, carry, diagonal_first=False):
            sc = [[down_scores(hd, j, diagonal_first and n == 0) for n, j in enumerate(blocks)] for hd in range(2)]
            out = []
            for hd in range(2):
                ct = carry[hd]
                for (ls, tl, sm, da), j in zip(sc[hd], blocks):
                    r0 = pl.multiple_of(j * QB, QB)
                    a = jnp.exp(ls + tl + ct)
                    p_buf[hd, j] = da * a
                    ls_buf[hd, j] = ls
                    dv_ref[pl.ds(r0, QB), sls[hd]] += _dot(a, do[hd], TN)
                    ct = ct + sm
                out.append(ct)
            return tuple(out)

        zero = jnp.zeros((QB, 1), F32)
        below = jnp.minimum(i, 1)
        left = i - below
        carry = lax.fori_loop(0, below, lambda t, c: down_group([i, i - 1], c, True), (zero, zero))
        carry = lax.fori_loop(0, 1 - below, lambda t, c: down_group([i], c, True), carry)
        carry = lax.fori_loop(0, left // 2, lambda t, c: down_group([left - 1 - 2 * t, left - 2 - 2 * t], c), carry)
        lax.fori_loop(0, left % 2, lambda t, c: down_group([0], c), carry)

        def up_group(blocks, carry):
            ld = []
            for hd in range(2):
                ld.append([])
                for j in blocks:
                    pj = p_buf[hd, j]
                    ld[hd].append((pj, jnp.exp(ls_buf[hd, j]), _dot_exact01(pj, tri_ref[1]), jnp.sum(pj, axis=1, keepdims=True)))
            out = []
            for hd in range(2):
                dq, cp = carry[hd]
                for (pj, sg, cm, sm), j in zip(ld[hd], blocks):
                    r0 = pl.multiple_of(j * QB, QB)
                    dz = (pj * (1.0 - sg) - (cp + cm) * sg) * scale
                    dk_ref[pl.ds(r0, QB), sls[hd]] += _dot(dz, q_raw[hd], TN)
                    dq = dq + _dot(dz, k_ref[pl.ds(r0, QB), sls[hd]])
                    cp = cp + sm
                out.append((dq, cp))
            return tuple(out)

        start = (jnp.zeros((QB, HEAD_DIM), F32), zero)
        odd = (i + 1) % 2
        carry = lax.fori_loop(0, odd, lambda t, c: up_group([0], c), (start, start))
        carry = lax.fori_loop(0, (i + 1) // 2, lambda t, c: up_group([odd + 2 * t, odd + 2 * t + 1], c), carry)
        dq_ref[:, sls[0]] = carry[0][0].astype(dq_ref.dtype)
        dq_ref[:, sls[1]] = carry[1][0].astype(dq_ref.dtype)

        @pl.when(i == nq - 1)
        def _():
            dk_out[...] = dk_ref[...].astype(dk_out.dtype)
            dv_out[...] = dv_ref[...].astype(dv_out.dtype)

    qspec = lambda off: pl.BlockSpec((QB, LANES), lambda b, p, i: (b * nq + i, off // LANES + p))
    kvspec = lambda off: pl.BlockSpec((s, LANES), lambda b, p, i: (b, off // LANES + p))
    blk_out = pl.BlockSpec((QB, LANES), lambda b, p, i: (b * nq + i, p))
    seq_out = pl.BlockSpec((s, LANES), lambda b, p, i: (b, p))
    shp = jax.ShapeDtypeStruct((nb * s, SB_WIDTH), MXU_DTYPE)
    c_in, c_specs, c_shapes, c_scratch = _comm_call_args(comm)
    res = pl.pallas_call(
        body, name=name, grid=(nb, 3, nq),
        in_specs=[qspec(H_Q), kvspec(H_K), kvspec(H_V), pl.BlockSpec((QB, LANES), lambda b, p, i: (b * nq + i, 3 + p)),
                  _SB_CONST_SPEC] + [_ANY] * n_in,
        out_specs=[blk_out, seq_out, seq_out] + c_specs, out_shape=[shp, shp, shp] + c_shapes,
        scratch_shapes=[pltpu.VMEM((2, nq, QB, QB), F32), pltpu.VMEM((2, nq, QB, QB), F32),
                        pltpu.VMEM((s, LANES), F32), pltpu.VMEM((s, LANES), F32)] + c_scratch,
        compiler_params=_params(3, comm is not None),
    )(h, h, h, dmix, _sb_constants(), *c_in)
    return res[0], res[1], res[2], res[3:]


def _ssd_conv(cur_ref, halo_ref, w_ref, b_ref, ext_ref, first):
    n = SSD_CHUNK
    cur = cur_ref[...]
    ext_ref[0:SUBLANES, :] = jnp.where(first, 0.0, halo_ref[...])
    ext_ref[SUBLANES:SUBLANES + n, :] = cur
    return (b_ref[...] + w_ref[3:4, :] * cur + w_ref[2:3, :] * ext_ref[pl.ds(SUBLANES - 1, n), :]
            + w_ref[1:2, :] * ext_ref[pl.ds(SUBLANES - 2, n), :] + w_ref[0:1, :] * ext_ref[pl.ds(SUBLANES - 3, n), :])


def _ssd_tri():
    row = lax.broadcasted_iota(jnp.int32, (SSD_CHUNK, SSD_CHUNK), 0)
    col = lax.broadcasted_iota(jnp.int32, (SSD_CHUNK, SSD_CHUNK), 1)
    return row, col


def _ssd_specs(nc, rev):
    n = SSD_CHUNK
    hb = n // SUBLANES

    def cidx(c):
        return (nc - 1 - c) if rev else c

    def blk(width, off):
        return pl.BlockSpec((n, width), lambda b, c: (b * nc + cidx(c), off // width))

    def halo(width, off):
        return pl.BlockSpec((SUBLANES, width), lambda b, c: (jnp.maximum((b * nc + cidx(c)) * hb - 1, 0), off // width))

    def full(shape):
        return pl.BlockSpec(shape, lambda b, c: (0,) * len(shape))

    return cidx, blk, halo, full


def _ssd_core_fwd(x, bc, dt, acum, acum_t, a_row, d_row, h_prev_ref, tri):
    n = SSD_CHUNK
    heads = []
    for g in range(2):
        bm = bc[:, g * SSD_STATE:(g + 1) * SSD_STATE]
        cm = bc[:, 2 * SSD_STATE + g * SSD_STATE: 2 * SSD_STATE + (g + 1) * SSD_STATE]
        gmat = _dot(cm, bm, NT)
        for r in range(3):
            hh = g * 3 + r
            hp = h_prev_ref[hh * HEAD_DIM:(hh + 1) * HEAD_DIM, :]
            heads.append(dict(g=g, hh=hh, bm=bm, cm=cm, gmat=gmat, hp=hp, cmh=_dot(cm, hp, NT)))
    for hd in heads:
        hh = hd["hh"]
        ac = acum[:, hh:hh + 1]
        ar = acum_t[hh:hh + 1, :]
        hd["dec"] = jnp.where(tri, jnp.exp(jnp.minimum(ac - ar, 0.0)), 0.0)
        hd["xh"] = x[:, hh * HEAD_DIM:(hh + 1) * HEAD_DIM]
        hd["dth"] = dt[:, hh:hh + 1]
        hd["xdt"] = hd["xh"] * hd["dth"]
        hd["ea"] = jnp.exp(ac)
        hd["m"] = hd["gmat"] * hd["dec"]
        hd["al"] = acum[n - 1:n, hh:hh + 1]
        hd["w"] = jnp.exp(hd["al"] - ac)
    for hd in heads:
        hd["yd"] = _dot(hd["m"], hd["xdt"])
    for hd in heads:
        hd["yo"] = hd["ea"] * hd["cmh"]
        hd["y"] = hd["yd"] + hd["yo"] + d_row[:, hd["hh"]:hd["hh"] + 1] * hd["xh"]
    return heads


def _ssd_prep(xs_ref, xsh_ref, bc_ref, bch_ref, dt_ref, cwx_ref, cbx_ref, cwb_ref, cbb_ref, vec_ref, xe_ref, be_ref, first):
    pre_x = _ssd_conv(xs_ref, xsh_ref, cwx_ref, cbx_ref, xe_ref, first)
    pre_bc = _ssd_conv(bc_ref, bch_ref, cwb_ref, cbb_ref, be_ref, first)
    x = _silu(pre_x)
    bc = _silu(pre_bc)
    dt_pre = dt_ref[...] + vec_ref[0:1, :]
    dt = _softplus(dt_pre)
    a_row = vec_ref[1:2, :]
    amat = dt * a_row
    row, col = _ssd_tri()
    upper = (row <= col).astype(BF16)
    lower = (col <= row).astype(BF16)
    acum = _dot_exact01(amat, lower, NN, x_left=False)
    acum_t = _dot_exact01(amat, upper, TN, x_left=True)
    return pre_x, pre_bc, x, bc, dt_pre, dt, a_row, acum, acum_t, row, col, upper


def _ssd_gate_norm(y, z, nw):
    lane = lax.broadcasted_iota(jnp.int32, y.shape, 1)
    g0 = lane < SSD_WIDTH // 2
    hg = y * _silu(z)
    sq = hg * hg
    ms0 = jnp.sum(jnp.where(g0, sq, 0.0), axis=1, keepdims=True) * (2.0 / SSD_WIDTH)
    ms1 = jnp.sum(jnp.where(g0, 0.0, sq), axis=1, keepdims=True) * (2.0 / SSD_WIDTH)
    rs = jnp.where(g0, lax.rsqrt(ms0 + RMS_EPS), lax.rsqrt(ms1 + RMS_EPS))
    return hg, rs, g0


def _ssd_fwd(h, cwx, cbx, cwb, cbb, vec, nw, nb, s, name):
    n = SSD_CHUNK
    nc = s // n
    _, blk, halo, full = _ssd_specs(nc, False)

    def body(bc_ref, bch_ref, z_ref, xs_ref, xsh_ref, dt_ref, cwx_ref, cbx_ref, cwb_ref, cbb_ref, vec_ref, nw_ref,
             o_ref, hs_ref, h_scr, xe_ref, be_ref, y_scr):
        c = pl.program_id(1)

        @pl.when(c == 0)
        def _():
            h_scr[...] = jnp.zeros_like(h_scr)

        (_, _, x, bc, _, dt, a_row, acum, acum_t, row, col, _) = _ssd_prep(
            xs_ref, xsh_ref, bc_ref, bch_ref, dt_ref, cwx_ref, cbx_ref, cwb_ref, cbb_ref, vec_ref, xe_ref, be_ref, c == 0)
        hs_ref[...] = h_scr[...]
        heads = _ssd_core_fwd(x, bc, dt, acum, acum_t, a_row, vec_ref[2:3, :], hs_ref, col <= row)
        for hd in heads:
            sl = slice(hd["hh"] * HEAD_DIM, (hd["hh"] + 1) * HEAD_DIM)
            y_scr[:, sl] = hd["y"]
            h_scr[sl, :] = jnp.exp(hd["al"]) * hd["hp"] + _dot(hd["xdt"] * hd["w"], hd["bm"], TN)
        hg, rs, _ = _ssd_gate_norm(y_scr[...], z_ref[...], nw_ref[...])
        o_ref[...] = (hg * rs * nw_ref[...]).astype(o_ref.dtype)

    t = nb * s
    return pl.pallas_call(
        body, name=name, grid=(nb, nc),
        in_specs=[blk(512, H_BC), halo(512, H_BC), blk(384, H_Z), blk(384, H_XS), halo(384, H_XS), blk(128, H_DT),
                  full((4, 384)), full((1, 384)), full((4, 512)), full((1, 512)), full((SUBLANES, LANES)), full((1, 384))],
        out_specs=[pl.BlockSpec((n, SSD_WIDTH), lambda b, c: (b * nc + c, 0)),
                   pl.BlockSpec((None, SSD_WIDTH, SSD_STATE), lambda b, c: (b * nc + c, 0, 0))],
        out_shape=[jax.ShapeDtypeStruct((t, SSD_WIDTH), MXU_DTYPE),
                   jax.ShapeDtypeStruct((nb * nc, SSD_WIDTH, SSD_STATE), F32)],
        scratch_shapes=[pltpu.VMEM((SSD_WIDTH, SSD_STATE), F32), pltpu.VMEM((n + SUBLANES, 384), F32),
                        pltpu.VMEM((n + SUBLANES, 512), F32), pltpu.VMEM((n, SSD_WIDTH), F32)],
        compiler_params=_params(2),
    )(h, h, h, h, h, h, cwx, cbx, cwb, cbb, vec, nw)


def _ssd_bwd(h, hstate, dmix, cwx, cbx, cwb, cbb, vec, nw, nb, s, name):
    n = SSD_CHUNK
    nc = s // n
    cidx, blk, halo, full = _ssd_specs(nc, True)

    def body(bc_ref, bch_ref, z_ref, xs_ref, xsh_ref, dt_ref, hs_ref, do_ref, cwx_ref, cbx_ref, cwb_ref, cbb_ref,
             vec_ref, nw_ref, dz_ref, dxs_ref, dbc_ref, ddt_ref, gx_ref, gb_ref, gv_ref, gn_ref,
             dh_scr, xe_ref, be_ref, y_scr, dx_scr, dbc_scr, dxe_ref, dbe_ref, cx_ref, cb_ref):
        b = pl.program_id(0)
        c = pl.program_id(1)
        cc = nc - 1 - c

        @pl.when(jnp.logical_and(b == 0, c == 0))
        def _():
            gx_ref[...] = jnp.zeros_like(gx_ref)
            gb_ref[...] = jnp.zeros_like(gb_ref)
            gv_ref[...] = jnp.zeros_like(gv_ref)
            gn_ref[...] = jnp.zeros_like(gn_ref)

        @pl.when(c == 0)
        def _():
            dh_scr[...] = jnp.zeros_like(dh_scr)
            cx_ref[...] = jnp.zeros_like(cx_ref)
            cb_ref[...] = jnp.zeros_like(cb_ref)

        (pre_x, pre_bc, x, bc, dt_pre, dt, a_row, acum, acum_t, row, col, upper) = _ssd_prep(
            xs_ref, xsh_ref, bc_ref, bch_ref, dt_ref, cwx_ref, cbx_ref, cwb_ref, cbb_ref, vec_ref, xe_ref, be_ref, cc == 0)
        tri = col <= row
        d_row = vec_ref[2:3, :]
        heads = _ssd_core_fwd(x, bc, dt, acum, acum_t, a_row, d_row, hs_ref, tri)
        for hd in heads:
            y_scr[:, hd["hh"] * HEAD_DIM:(hd["hh"] + 1) * HEAD_DIM] = hd["y"]
        y = y_scr[...]
        z = z_ref[...]
        nwv = nw_ref[...]
        hg, rs, g0 = _ssd_gate_norm(y, z, nwv)
        do = do_ref[...]
        nrm = hg * rs
        gn_ref[...] += jnp.sum(do * nrm, axis=0, keepdims=True)
        dn = do * nwv
        dnn = dn * nrm
        mean0 = jnp.sum(jnp.where(g0, dnn, 0.0), axis=1, keepdims=True) * (2.0 / SSD_WIDTH)
        mean1 = jnp.sum(jnp.where(g0, 0.0, dnn), axis=1, keepdims=True) * (2.0 / SSD_WIDTH)
        dhg = rs * (dn - nrm * jnp.where(g0, mean0, mean1))
        dz_ref[...] = (dhg * y * _dsilu(z)).astype(dz_ref.dtype)
        dy = dhg * _silu(z)

        lane = lax.broadcasted_iota(jnp.int32, (n, LANES), 1)
        lane1 = lax.broadcasted_iota(jnp.int32, (1, LANES), 1)
        last_row = lax.broadcasted_iota(jnp.int32, (n, 1), 0) == n - 1
        dacum_col = jnp.zeros((n, LANES), F32)
        da_rowpart = jnp.zeros((n, LANES), F32)
        ddt = jnp.zeros((n, LANES), F32)
        dd_vec = jnp.zeros((1, LANES), F32)
        for hd in heads:
            sl = slice(hd["hh"] * HEAD_DIM, (hd["hh"] + 1) * HEAD_DIM)
            dyh = dy[:, sl]
            dhn = dh_scr[sl, :]
            hd.update(sl=sl, dyh=dyh, dhn=dhn, t1=_dot(dyh, hd["hp"]), dm=_dot(dyh, hd["xdt"], NT),
                      t2=_dot(hd["bm"], dhn, NT), mtdy=_dot(hd["m"], dyh, TN), xdhn=_dot(hd["xdt"], dhn),
                      dhp=_dot(dyh * hd["ea"], hd["cm"], TN))
        dgs, dbms, dcms = [], [], []
        for g in range(2):
            dg = jnp.zeros((n, n), F32)
            dbm = jnp.zeros((n, SSD_STATE), F32)
            dcm = jnp.zeros((n, SSD_STATE), F32)
            for hd in heads[3 * g:3 * g + 3]:
                hh, sl, dyh, dhn, t2 = hd["hh"], hd["sl"], hd["dyh"], hd["dhn"], hd["t2"]
                el = jnp.exp(hd["al"])
                dd_vec = dd_vec + jnp.where(lane1 == hh, jnp.sum(dyh * hd["xh"]), 0.0)
                dcm = dcm + hd["ea"] * hd["t1"]
                dg = dg + hd["dm"] * hd["dec"]
                e = hd["dm"] * hd["m"]
                dxdt = hd["mtdy"] + hd["w"] * t2
                dbm = dbm + hd["w"] * hd["xdhn"]
                dw_w = jnp.sum(hd["xdt"] * t2, axis=1, keepdims=True) * hd["w"]
                d_el = jnp.sum(dhn * hd["hp"])
                col_part = (jnp.sum(dyh * hd["yo"], axis=1, keepdims=True) + jnp.sum(e, axis=1, keepdims=True) - dw_w
                            + jnp.where(last_row, d_el * el + jnp.sum(dw_w), 0.0))
                dacum_col = dacum_col + jnp.where(lane == hh, col_part, 0.0)
                neg_colsum = -jnp.sum(e, axis=0, keepdims=True)
                rev = jnp.sum(jnp.where(row <= col, neg_colsum, 0.0), axis=1, keepdims=True)
                da_rowpart = da_rowpart + jnp.where(lane == hh, rev, 0.0)
                dh_scr[sl, :] = el * dhn + hd["dhp"]
                dx_scr[:, sl] = d_row[:, hh:hh + 1] * dyh + dxdt * hd["dth"]
                ddt = ddt + jnp.where(lane == hh, jnp.sum(dxdt * hd["xh"], axis=1, keepdims=True), 0.0)
            dgs.append(dg)
            dbms.append(dbm)
            dcms.append(dcm)
        for g in range(2):
            bm, cm = heads[3 * g]["bm"], heads[3 * g]["cm"]
            dbc_scr[:, g * SSD_STATE:(g + 1) * SSD_STATE] = dbms[g] + _dot(dgs[g], cm, TN)
            dbc_scr[:, 2 * SSD_STATE + g * SSD_STATE:2 * SSD_STATE + (g + 1) * SSD_STATE] = dcms[g] + _dot(dgs[g], bm)
        da_mat = _dot_exact01(dacum_col, upper, NN, x_left=False) + da_rowpart
        ddt = ddt + da_mat * a_row
        da_vec = jnp.sum(da_mat * dt, axis=0, keepdims=True)
        ddt_pre = jnp.where(lane < SSD_HEADS, ddt * _sigmoid(dt_pre), 0.0)
        ddt_ref[...] = ddt_pre.astype(ddt_ref.dtype)
        rid = lax.broadcasted_iota(jnp.int32, (SUBLANES, LANES), 0)
        gv_ref[...] += (jnp.where(rid == 0, jnp.sum(ddt_pre, axis=0, keepdims=True), 0.0)
                        + jnp.where(rid == 1, da_vec, 0.0) + jnp.where(rid == 2, dd_vec, 0.0))

        def conv_bwd(dpost, pre, w_ref, ext_ref, dext_ref, carry_ref, cur_ref, out_ref, g_ref, width):
            dco = dpost * _dsilu(pre)
            dext_ref[0:n, :] = dco
            dext_ref[n:n + SUBLANES, :] = carry_ref[...]
            out_ref[...] = (w_ref[3:4, :] * dco + w_ref[2:3, :] * dext_ref[pl.ds(1, n), :]
                            + w_ref[1:2, :] * dext_ref[pl.ds(2, n), :] + w_ref[0:1, :] * dext_ref[pl.ds(3, n), :]
                            ).astype(out_ref.dtype)
            carry_ref[...] = dco[0:SUBLANES, :]
            rid8 = lax.broadcasted_iota(jnp.int32, (SUBLANES, width), 0)
            acc = jnp.where(rid8 == 3, jnp.sum(dco * cur_ref[...], axis=0, keepdims=True), 0.0)
            for j in range(3):
                sh = ext_ref[pl.ds(SUBLANES - 3 + j, n), :]
                acc = acc + jnp.where(rid8 == j, jnp.sum(dco * sh, axis=0, keepdims=True), 0.0)
            acc = acc + jnp.where(rid8 == 4, jnp.sum(dco, axis=0, keepdims=True), 0.0)
            g_ref[...] += acc

        conv_bwd(dx_scr[...], pre_x, cwx_ref, xe_ref, dxe_ref, cx_ref, xs_ref, dxs_ref, gx_ref, 384)
        conv_bwd(dbc_scr[...], pre_bc, cwb_ref, be_ref, dbe_ref, cb_ref, bc_ref, dbc_ref, gb_ref, 512)

    t = nb * s
    rowblk = lambda width: pl.BlockSpec((n, width), lambda b, c: (b * nc + cidx(c), 0))
    return pl.pallas_call(
        body, name=name, grid=(nb, nc),
        in_specs=[blk(512, H_BC), halo(512, H_BC), blk(384, H_Z), blk(384, H_XS), halo(384, H_XS), blk(128, H_DT),
                  pl.BlockSpec((None, SSD_WIDTH, SSD_STATE), lambda b, c: (b * nc + cidx(c), 0, 0)),
                  pl.BlockSpec((n, SSD_WIDTH), lambda b, c: (b * nc + cidx(c), 0)),
                  full((4, 384)), full((1, 384)), full((4, 512)), full((1, 512)), full((SUBLANES, LANES)), full((1, 384))],
        out_specs=[rowblk(384), rowblk(384), rowblk(512), rowblk(128),
                   full((SUBLANES, 384)), full((SUBLANES, 512)), full((SUBLANES, LANES)), full((1, 384))],
        out_shape=[jax.ShapeDtypeStruct((t, 384), MXU_DTYPE), jax.ShapeDtypeStruct((t, 384), MXU_DTYPE),
                   jax.ShapeDtypeStruct((t, 512), MXU_DTYPE), jax.ShapeDtypeStruct((t, 128), MXU_DTYPE),
                   jax.ShapeDtypeStruct((SUBLANES, 384), F32), jax.ShapeDtypeStruct((SUBLANES, 512), F32),
                   jax.ShapeDtypeStruct((SUBLANES, LANES), F32), jax.ShapeDtypeStruct((1, 384), F32)],
        scratch_shapes=[pltpu.VMEM((SSD_WIDTH, SSD_STATE), F32), pltpu.VMEM((n + SUBLANES, 384), F32),
                        pltpu.VMEM((n + SUBLANES, 512), F32), pltpu.VMEM((n, SSD_WIDTH), F32),
                        pltpu.VMEM((n, 384), F32), pltpu.VMEM((n, 512), F32),
                        pltpu.VMEM((n + SUBLANES, 384), F32), pltpu.VMEM((n + SUBLANES, 512), F32),
                        pltpu.VMEM((SUBLANES, 384), F32), pltpu.VMEM((SUBLANES, 512), F32)],
        compiler_params=_params(2),
    )(h, h, h, h, h, h, hstate, dmix, cwx, cbx, cwb, cbb, vec, nw)


def _adamw_math(w, g, m, v):
    m = ADAM_B1 * m + (1.0 - ADAM_B1) * g
    v = ADAM_B2 * v + (1.0 - ADAM_B2) * (g * g)
    m_hat = m / (1.0 - ADAM_B1 ** ADAM_STEP)
    v_hat = v / (1.0 - ADAM_B2 ** ADAM_STEP)
    delta = -ADAM_LR * (m_hat / (jnp.sqrt(v_hat) + ADAM_EPS) + ADAM_WD * w)
    return delta, m, v


def _adamw(w, g, m, v, name, tr=256):
    rows, cols = w.shape
    tr = rows if rows <= tr else tr
    assert rows % tr == 0, (rows, tr)

    def body(w_ref, g_ref, m_ref, v_ref, d_ref, nm_ref, nv_ref):
        d, nm, nv = _adamw_math(w_ref[...], g_ref[...], m_ref[...], v_ref[...])
        d_ref[...] = d
        nm_ref[...] = nm
        nv_ref[...] = nv

    spec = pl.BlockSpec((tr, cols), lambda i: (i, 0))
    shp = jax.ShapeDtypeStruct((rows, cols), F32)
    return pl.pallas_call(body, name=name, grid=(rows // tr,), in_specs=[spec] * 4, out_specs=[spec] * 3,
                          out_shape=[shp] * 3, compiler_params=_params(1))(w, g, m, v)


def _sum8_layers(parts, name, tr):
    _, rows, cols = parts[0].shape
    assert rows % tr == 0
    nt = rows // tr

    def body(*refs):
        o_ref = refs[DEPTH]
        layer = pl.program_id(0)
        for l in range(DEPTH):
            @pl.when(layer == l)
            def _(l=l):
                acc = refs[l][0]
                for k in range(1, N_DEV):
                    acc = acc + refs[l][k]
                o_ref[...] = acc

    in_specs = [pl.BlockSpec((N_DEV, tr, cols), lambda a, i, l=l: (0, jnp.clip(i + (a - l) * nt, 0, nt - 1), 0))
                for l in range(DEPTH)]
    return pl.pallas_call(body, name=name, grid=(DEPTH, nt), in_specs=in_specs,
                          out_specs=pl.BlockSpec((None, tr, cols), lambda a, i: (a, i, 0)),
                          out_shape=jax.ShapeDtypeStruct((DEPTH, rows, cols), F32), compiler_params=_params(2))(*parts)


def _all_reduce_small(vec, name):
    rows, cols = vec.shape

    def body(x_ref, out_ref, gbuf, send_sems, recv_sems):
        x, y, c = lax.axis_index("x"), lax.axis_index("y"), lax.axis_index("c")
        me, sibling = (x, y, c), (x, y, 1 - c)
        chips = [(1 - x, y), (x, 1 - y), (1 - x, 1 - y)]

        def slot(px, py, pc):
            return gbuf.at[4 * px + 2 * py + pc]

        def copy(k, block, to, src=None):
            return pltpu.make_async_remote_copy(
                src_ref=slot(*block) if src is None else src, dst_ref=slot(*block),
                send_sem=send_sems.at[k], recv_sem=recv_sems.at[k], device_id=to, device_id_type=MESH_ID)

        first = [copy(0, me, sibling, src=x_ref)]
        first += [copy(1 + j, me, (*chip, c), src=x_ref) for j, chip in enumerate(chips)]
        for cp in first:
            cp.start()
        gbuf[4 * x + 2 * y + c] = x_ref[...]
        passed = [copy(4 + j, (*chip, c), sibling) for j, chip in enumerate(chips)]
        for j, chip in enumerate(chips):
            copy(1 + j, (*chip, c), me).wait_recv()
            passed[j].start()
        copy(0, sibling, me).wait_recv()
        for j, chip in enumerate(chips):
            copy(4 + j, (*chip, 1 - c), me).wait_recv()
        for cp in first + passed:
            cp.wait_send()
        acc = gbuf[0]
        for k in range(1, N_DEV):
            acc = acc + gbuf[k]
        out_ref[...] = acc

    return pl.pallas_call(
        body, name=name, out_shape=jax.ShapeDtypeStruct((rows, cols), F32),
        in_specs=[pl.BlockSpec(memory_space=pltpu.VMEM)], out_specs=pl.BlockSpec(memory_space=pltpu.VMEM),
        scratch_shapes=[pltpu.VMEM((N_DEV, rows, cols), F32), pltpu.SemaphoreType.DMA((7,)), pltpu.SemaphoreType.DMA((7,))],
        compiler_params=pltpu.CompilerParams(has_side_effects=True, vmem_limit_bytes=VMEM_LIMIT_BYTES),
    )(vec)


_COL_POOL, _COL_Z, _COL_XBC, _COL_DT, _COL_Q, _COL_K, _COL_V = 0, 256, 640, 1536, 1542, 1926, 2310
_H_SEGMENTS = ((_COL_XBC + SSD_WIDTH, 512), (_COL_POOL, 256), (_COL_Q, 384), (_COL_K, 384), (_COL_V, 384),
               (_COL_Z, 384), (_COL_XBC, 384), (_COL_DT, 6))


def _h_from_orig(w):
    parts = [w[..., o:o + n] for o, n in _H_SEGMENTS]
    pad = jnp.zeros(w.shape[:-1] + (H_COLS - IN_COLS,), w.dtype)
    return jnp.concatenate(parts + [pad], axis=-1)


def _h_to_orig(w):
    offs, o = {}, 0
    for orig, n in _H_SEGMENTS:
        offs[orig] = (o, n)
        o += n
    order = sorted(offs)
    return jnp.concatenate([w[..., offs[k][0]:offs[k][0] + offs[k][1]] for k in order], axis=-1)


def _interleave(w):
    lead = w.shape[:-1]
    nt = D_FF // GLU_TILE
    return jnp.swapaxes(w.reshape(lead + (2, nt, GLU_TILE)), -3, -2).reshape(lead + (2 * D_FF,))


def _deinterleave(w):
    lead = w.shape[:-1]
    nt = D_FF // GLU_TILE
    return jnp.swapaxes(w.reshape(lead + (nt, 2, GLU_TILE)), -3, -2).reshape(lead + (2 * D_FF,))


def _up_segments():
    segs = []
    for j in range(N_DEV):
        half, base = j // 4, 704 * (j % 4)
        c = base
        while c < base + 704:
            t, r = divmod(c, GLU_TILE)
            n = min(GLU_TILE - r, base + 704 - c)
            segs.append((j, c - base, 2 * GLU_TILE * t + GLU_TILE * half + r, n))
            c += n
    return segs


def _up_to_interleaved(w, name, tr=256):
    def body(i_ref, o_ref):
        for j, src, dst, n in _up_segments():
            o_ref[:, dst:dst + n] = i_ref[j, :, src:src + n]

    return pl.pallas_call(
        body, name=name, grid=(D_MODEL // tr,), in_specs=[pl.BlockSpec((N_DEV, tr, 704), lambda r: (0, r, 0))],
        out_specs=pl.BlockSpec((tr, 2 * D_FF), lambda r: (r, 0)),
        out_shape=jax.ShapeDtypeStruct((D_MODEL, 2 * D_FF), w.dtype), compiler_params=_params(1))(w)


def _up_from_interleaved(g, name, tr=128):
    def body(i_ref, o_ref):
        for j, src, dst, n in _up_segments():
            o_ref[j, :, src:src + n] = i_ref[:, dst:dst + n]

    return pl.pallas_call(
        body, name=name, grid=(D_MODEL // tr,), in_specs=[pl.BlockSpec((tr, 2 * D_FF), lambda r: (r, 0))],
        out_specs=pl.BlockSpec((N_DEV, tr, 704), lambda r: (0, r, 0)),
        out_shape=jax.ShapeDtypeStruct((N_DEV, D_MODEL, 704), g.dtype), compiler_params=_params(1))(g)


def _mix_rows_from_orig(w):
    return jnp.concatenate([w[256:640], w[640:1024], w[0:256]], axis=0)


def _mix_rows_to_orig(w):
    return jnp.concatenate([w[768:1024], w[0:384], w[384:768]], axis=0)


def _xbc_split(w):
    return w[..., :SSD_WIDTH], w[..., SSD_WIDTH:]


def _layer_fwd(x, p_l, wt, sp, nb, s, comm=None):
    h = _mm(x, wt["w_in"], "nn", F32, "mm_in", tm=1024, tn=1408)
    pool_out = _pool_fwd(h, wt["pool_bd"], sp["pool_scale"], nb, s, "pool_fwd")
    ssd_out, hstate = _ssd_fwd(h, sp["cwx"], sp["cbx"], sp["cwb"], sp["cbb"], sp["ssd_vec"], sp["ssd_norm_w"], nb, s, "ssd_fwd")
    sb_out, comm_out = _sb_fwd(h, nb, s, "sb_fwd" if comm is None else "sb_fwd_gather", comm)
    mixcat = jnp.concatenate([ssd_out, sb_out, pool_out], axis=1)
    mix = _mm(mixcat, wt["w_out"], "nn", F32, "mm_out", tm=1024, tn=1024)
    x1, r1 = _ln_fwd(x, mix, sp["ln1"], "ln1_fwd")
    up = _mm(x1, wt["w_up"], "nn", F32, "mm_up", tm=1024, tn=1408)
    act = _glu_fwd(up, sp["ffn_cw"], sp["ffn_cb"], nb, s, "glu_fwd")
    ffn = _mm(act, wt["w_down"], "nn", F32, "mm_down", tm=1024, tn=1024, tk=1408)
    gp = _mm(x1, wt["w_gate"], "nn", F32, "mm_gate", tm=1024, tn=1024)
    pp = _mm(p_l, wt["w_proj"], "nn", F32, "mm_proj", tm=2048, tn=1024)
    x2, r2 = _ln_fwd(x1, ffn, sp["ln2"], "ln2_fwd", gp=gp, pp=pp)
    return x2, dict(x=x, h=h, hstate=hstate, mixcat=mixcat, r1=r1, x1=x1, up=up, act=act, gp=gp, pp=pp, r2=r2), comm_out


def _layer_bwd(dx2, p_l, sv, wt, sp, nb, s, comm=None):
    dr2, dgp, dpp, st2 = _ln_bwd(sv["r2"], sp["ln2"], dx2, "ln2_bwd", gp=sv["gp"], pp=sv["pp"])
    g_down = _mm(sv["act"], dr2, "tn", F32, "wg_down", tm=1408, tn=1024, tk=512)
    dact = _mm(dr2, wt["w_down"], "nt", F32, "dg_down", tm=1024, tn=1408)
    dup, ffn_acc = _glu_bwd(sv["up"], dact, sp["ffn_cw"], sp["ffn_cb"], nb, s, "glu_bwd")
    g_up = _mm(sv["x1"], dup, "tn", F32, "wg_up", tm=1024, tn=2816, tk=512)
    g_gate = _mm(sv["x1"], dgp, "tn", F32, "wg_gate", tm=1024, tn=1024, tk=512)
    g_proj = _mm(p_l, dpp, "tn", F32, "wg_proj", tm=256, tn=1024, tk=512)
    t1 = _mm(dgp, wt["w_gate"], "nt", F32, "dg_gate", tm=1024, tn=1024, add=dr2, add_coef=ALPHA)
    dx1 = _mm(dup, wt["w_up"], "nt", F32, "dg_up", tm=1024, tn=1024, tk=1408, add=t1)
    dr1, st1 = _ln_bwd(sv["r1"], sp["ln1"], dx1, "ln1_bwd")
    g_out = _mm(sv["mixcat"], dr1, "tn", F32, "wg_out", tm=1024, tn=1024, tk=512)
    dmix = _mm(dr1, wt["w_out"], "nt", F32, "dg_out", tm=1024, tn=1024)
    du, g_pool_bd, g_pool_scale = _pool_bwd(sv["h"], dmix, wt["pool_bd"], sp["pool_scale"], nb, s, "pool_bwd")
    dz, dxs, dbc, ddt, gx, gb, gv, gn = _ssd_bwd(sv["h"], sv["hstate"], dmix, sp["cwx"], sp["cbx"], sp["cwb"], sp["cbb"],
                                                  sp["ssd_vec"], sp["ssd_norm_w"], nb, s, "ssd_bwd")
    ready = dict(w_out=g_out, ffn_w_up=g_up, ffn_w_down=g_down, ple_w_gate=g_gate, ple_w_proj=g_proj)
    job = comm(ready) if comm is not None else None
    dq, dk, dv, comm_out = _sb_bwd(sv["h"], dmix, nb, s, "sb_bwd" if job is None else "sb_bwd_x%d" % job["n_xfers"], job)
    dh = jnp.concatenate([dbc, du, dq, dk, dv, dz, dxs, ddt], axis=1)
    g_in = _mm(sv["x"], dh, "tn", F32, "wg_in", tm=1024, tn=2816, tk=512)
    dx = _mm(dh, wt["w_in"], "nt", F32, "dg_in", tm=1024, tn=1024, tk=1408, add=dr1, add_coef=ALPHA)
    small = dict(
        pool_w=jnp.stack([g_pool_bd[HEAD_DIM * g:HEAD_DIM * (g + 1), HEAD_DIM * g:HEAD_DIM * (g + 1)] for g in range(4)]),
        pool_scale=g_pool_scale[0],
        ssd_conv_w=jnp.concatenate([gx[0:4], gb[0:4]], axis=1),
        ssd_conv_b=jnp.concatenate([gx[4], gb[4]], axis=0),
        ssd_dt_bias=gv[0, :SSD_HEADS],
        ssd_a_log=gv[1, :SSD_HEADS] * sp["ssd_vec"][1, :SSD_HEADS],
        ssd_d=gv[2, :SSD_HEADS],
        ssd_norm_w=gn[0],
        ln1_g=st1[0], ln1_b=st1[1], ln2_g=st2[0], ln2_b=st2[1],
        ffn_conv_w=_deinterleave(ffn_acc[0:3]),
        ffn_conv_b=_deinterleave(ffn_acc[3]),
    )
    return dx, dict(ready, w_in=g_in), small, comm_out


def _layer_params(i, big, rep):
    pool_bd = jnp.zeros((POOL_WIDTH, POOL_WIDTH), F32)
    for g in range(4):
        pool_bd = lax.dynamic_update_slice(pool_bd, rep["pool_w"][i, g], (HEAD_DIM * g, HEAD_DIM * g))
    wt = dict(w_in=big["w_in"], w_out=big["w_out"], w_up=big["ffn_w_up"], w_down=big["ffn_w_down"],
              w_gate=big["ple_w_gate"], w_proj=big["ple_w_proj"], pool_bd=pool_bd.astype(MXU_DTYPE))
    cwx, cwb = _xbc_split(rep["ssd_conv_w"][i])
    cbx, cbb = _xbc_split(rep["ssd_conv_b"][i][None, :])
    vec = jnp.zeros((SUBLANES, LANES), F32)
    vec = vec.at[0, :SSD_HEADS].set(rep["ssd_dt_bias"][i])
    vec = vec.at[1, :SSD_HEADS].set(-jnp.exp(rep["ssd_a_log"][i]))
    vec = vec.at[2, :SSD_HEADS].set(rep["ssd_d"][i])
    sp = dict(pool_scale=rep["pool_scale"][i][None, :], cwx=cwx, cbx=cbx, cwb=cwb, cbb=cbb, ssd_vec=vec,
              ssd_norm_w=rep["ssd_norm_w"][i][None, :],
              ln1=jnp.stack([rep["ln1_g"][i], rep["ln1_b"][i]]), ln2=jnp.stack([rep["ln2_g"][i], rep["ln2_b"][i]]),
              ffn_cw=_interleave(rep["ffn_conv_w"][i]), ffn_cb=_interleave(rep["ffn_conv_b"][i][None, :]))
    return wt, sp


def _run_layers(x, p, target, big_w, rep, fwd_job=None, fwd_done=None, bwd_job=None, bwd_done=None):
    nb, s, d = x.shape
    t = nb * s
    xf = x.reshape(t, d)
    saved, params = [], []
    for i in range(DEPTH):
        wt, sp = _layer_params(i, big_w[i], rep)
        params.append((wt, sp))
        job = fwd_job(i) if fwd_job is not None else None
        xf, sv, res = _layer_fwd(xf, p[i].reshape(t, PLE_DIM), wt, sp, nb, s, job)
        if job is not None:
            fwd_done(i, res)
        saved.append(sv)
    dy, loss = _loss_grad(xf, target.reshape(t, d), "loss")
    bigs, smalls = [None] * DEPTH, [None] * DEPTH
    for i in reversed(range(DEPTH)):
        wt, sp = params[i]
        job = (lambda ready, i=i: bwd_job(i, bigs, ready)) if bwd_job is not None else None
        dy, bigs[i], smalls[i], res = _layer_bwd(dy, p[i].reshape(t, PLE_DIM), saved[i], wt, sp, nb, s, job)
        if job is not None:
            bwd_done(i, res)
    return loss, dy.reshape(nb, s, d), bigs, smalls


def _local_step(x, p, target, full, rep):
    return _run_layers(x, p, target, [{n: full[n][i] for n in full} for i in range(DEPTH)], rep)


BIG = ("w_in", "w_out", "ffn_w_up", "ffn_w_down", "ple_w_gate", "ple_w_proj")
SMALL_REPLICATED = ("pool_w", "pool_scale", "ssd_conv_b", "ssd_dt_bias", "ssd_a_log", "ssd_d", "ssd_norm_w",
                    "ln1_g", "ln1_b", "ffn_conv_b", "ln2_g", "ln2_b")
SMALL_SHARDED = ("ssd_conv_w", "ffn_conv_w")
WEIGHTS = ("w_in", "pool_w", "pool_scale", "ssd_conv_w", "ssd_conv_b", "ssd_dt_bias", "ssd_a_log", "ssd_d", "ssd_norm_w",
           "w_out", "ln1_g", "ln1_b", "ffn_w_up", "ffn_conv_w", "ffn_conv_b", "ffn_w_down", "ln2_g", "ln2_b",
           "ple_w_gate", "ple_w_proj")
SUM_BLOCK_BYTES = 3 * 512 * 1024


def _to_rows(a, cols):
    f = a.reshape(-1)
    pad = (-f.shape[0]) % cols
    if pad:
        f = jnp.concatenate([f, jnp.zeros((pad,), f.dtype)])
    return f.reshape(-1, cols)


def _pack_rows(arrs, cols, row_mult):
    rows = [_to_rows(a, cols) for a in arrs]
    flat = jnp.concatenate(rows, axis=0)
    pad = (-flat.shape[0]) % row_mult
    if pad:
        flat = jnp.concatenate([flat, jnp.zeros((pad, cols), flat.dtype)], axis=0)
    return flat


def _unpack_rows(flat, shapes, cols):
    out, r = [], 0
    for shp in shapes:
        n = 1
        for v in shp:
            n *= v
        nr = -(-n // cols)
        out.append(flat[r:r + nr].reshape(-1)[:n].reshape(shp))
        r += nr
    return out


def kernel(x, p, w_in, pool_w, pool_scale, ssd_conv_w, ssd_conv_b, ssd_dt_bias, ssd_a_log, ssd_d, ssd_norm_w, w_out, ln1_g, ln1_b, ffn_w_up, ffn_conv_w, ffn_conv_b, ffn_w_down, ln2_g, ln2_b, ple_w_gate, ple_w_proj, loss_target, m_w_in, m_pool_w, m_pool_scale, m_ssd_conv_w, m_ssd_conv_b, m_ssd_dt_bias, m_ssd_a_log, m_ssd_d, m_ssd_norm_w, m_w_out, m_ln1_g, m_ln1_b, m_ffn_w_up, m_ffn_conv_w, m_ffn_conv_b, m_ffn_w_down, m_ln2_g, m_ln2_b, m_ple_w_gate, m_ple_w_proj, v_w_in, v_pool_w, v_pool_scale, v_ssd_conv_w, v_ssd_conv_b, v_ssd_dt_bias, v_ssd_a_log, v_ssd_d, v_ssd_norm_w, v_w_out, v_ln1_g, v_ln1_b, v_ffn_w_up, v_ffn_conv_w, v_ffn_conv_b, v_ffn_w_down, v_ln2_g, v_ln2_b, v_ple_w_gate, v_ple_w_proj):
    wts = dict(w_in=w_in, pool_w=pool_w, pool_scale=pool_scale, ssd_conv_w=ssd_conv_w, ssd_conv_b=ssd_conv_b,
               ssd_dt_bias=ssd_dt_bias, ssd_a_log=ssd_a_log, ssd_d=ssd_d, ssd_norm_w=ssd_norm_w, w_out=w_out, ln1_g=ln1_g,
               ln1_b=ln1_b, ffn_w_up=ffn_w_up, ffn_conv_w=ffn_conv_w, ffn_conv_b=ffn_conv_b, ffn_w_down=ffn_w_down,
               ln2_g=ln2_g, ln2_b=ln2_b, ple_w_gate=ple_w_gate, ple_w_proj=ple_w_proj)
    mom_m = dict(w_in=m_w_in, pool_w=m_pool_w, pool_scale=m_pool_scale, ssd_conv_w=m_ssd_conv_w, ssd_conv_b=m_ssd_conv_b,
                 ssd_dt_bias=m_ssd_dt_bias, ssd_a_log=m_ssd_a_log, ssd_d=m_ssd_d, ssd_norm_w=m_ssd_norm_w, w_out=m_w_out,
                 ln1_g=m_ln1_g, ln1_b=m_ln1_b, ffn_w_up=m_ffn_w_up, ffn_conv_w=m_ffn_conv_w, ffn_conv_b=m_ffn_conv_b,
                 ffn_w_down=m_ffn_w_down, ln2_g=m_ln2_g, ln2_b=m_ln2_b, ple_w_gate=m_ple_w_gate, ple_w_proj=m_ple_w_proj)
    mom_v = dict(w_in=v_w_in, pool_w=v_pool_w, pool_scale=v_pool_scale, ssd_conv_w=v_ssd_conv_w, ssd_conv_b=v_ssd_conv_b,
                 ssd_dt_bias=v_ssd_dt_bias, ssd_a_log=v_ssd_a_log, ssd_d=v_ssd_d, ssd_norm_w=v_ssd_norm_w, w_out=v_w_out,
                 ln1_g=v_ln1_g, ln1_b=v_ln1_b, ffn_w_up=v_ffn_w_up, ffn_conv_w=v_ffn_conv_w, ffn_conv_b=v_ffn_conv_b,
                 ffn_w_down=v_ffn_w_down, ln2_g=v_ln2_g, ln2_b=v_ln2_b, ple_w_gate=v_ple_w_gate, ple_w_proj=v_ple_w_proj)
    me = 4 * lax.axis_index("x") + 2 * lax.axis_index("y") + lax.axis_index("c")

    def layer_shards(i):
        sh = {n: wts[n][i].astype(MXU_DTYPE) for n in BIG}
        sh["w_in"] = _h_from_orig(wts["w_in"][i]).astype(MXU_DTYPE)
        return sh

    def gathered_weights(res):
        big = dict(zip(BIG, res[:len(BIG)]))
        big["ffn_w_up"] = _up_to_interleaved(big["ffn_w_up"], "up_to_interleaved")
        return big

    res0 = _gather_call_two_level(_gather_job(layer_shards(0), [wts[n] for n in SMALL_SHARDED]), "gather_layer0")
    big_w = [gathered_weights(res0)] + [None] * (DEPTH - 1)
    rep = {n: wts[n] for n in SMALL_REPLICATED}
    for n, g in zip(SMALL_SHARDED, res0[len(BIG):]):
        rep[n] = jnp.transpose(g, (1, 2, 0, 3)).reshape(g.shape[1], g.shape[2], N_DEV * g.shape[3])

    def fwd_job(i):
        return _gather_job(layer_shards(i + 1)) if i + 1 < DEPTH else None

    def fwd_done(i, res):
        big_w[i + 1] = gathered_weights(res)

    received = [dict() for _ in range(DEPTH)]
    carried = ("w_out", "ffn_w_up", "ffn_w_down", "ple_w_gate", "ple_w_proj")

    def bwd_items(i, bigs, ready):
        items = [(i, n, ready[n]) for n in carried] + ([(i + 1, "w_in", bigs[i + 1]["w_in"])] if i + 1 < DEPTH else [])
        return [(l, n, _up_from_interleaved(g, "up_from_interleaved") if n == "ffn_w_up" else g) for l, n, g in items]

    pending = {}

    def bwd_job(i, bigs, ready):
        pending[i] = bwd_items(i, bigs, ready)
        return _exchange_job([(n, g) for _, n, g in pending[i]])

    def bwd_done(i, res):
        for (l, n, _), r in zip(pending[i], res):
            received[l][n] = r

    loss_loc, grad_x, bigs, smalls = _run_layers(x, p, loss_target, big_w, rep, fwd_job, fwd_done, bwd_job, bwd_done)
    received[0]["w_in"] = _comm_call(_exchange_job([("w_in", bigs[0]["w_in"])]), "exchange_w_in0")[0]

    grads = {}
    for n in BIG:
        parts = [received[i][n] for i in range(DEPTH)]
        _, rows, cols = parts[0].shape
        tr = next(t for t in (256, 128, 64, 32, 16, 8) if rows % t == 0 and N_DEV * t * cols * 4 <= SUM_BLOCK_BYTES)
        g = _sum8_layers(parts, "sum_" + n, tr)
        grads[n] = _h_to_orig(g) if n == "w_in" else g
    small_names = SMALL_REPLICATED + SMALL_SHARDED
    small_full_shapes = [rep[n].shape for n in small_names]
    small_vec = _pack_rows([jnp.stack([smalls[i][n] for i in range(DEPTH)]) for n in small_names] + [loss_loc[0, :1]],
                           LANES, SUBLANES)
    small_sum = _all_reduce_small(small_vec, "allreduce_small")
    small_out = _unpack_rows(small_sum, small_full_shapes + [(1,)], LANES)
    loss = small_out[-1][0]
    for n, g in zip(small_names, small_out[:-1]):
        if n in SMALL_SHARDED:
            width = wts[n].shape[-1]
            g = lax.dynamic_slice_in_dim(g, me * width, width, axis=g.ndim - 1)
        grads[n] = g

    delta, new_m, new_v = {}, {}, {}
    for n in BIG:
        shp = wts[n].shape
        two_d = lambda a: a.reshape(-1, shp[-1])
        tr = {"w_in": 128, "ffn_w_down": 352}.get(n, 256)
        d_, m_, v_ = _adamw(two_d(wts[n]), two_d(grads[n]), two_d(mom_m[n]), two_d(mom_v[n]), "adamw_" + n, tr=tr)
        delta[n], new_m[n], new_v[n] = d_.reshape(shp), m_.reshape(shp), v_.reshape(shp)
    packs = [_pack_rows([src[n] for n in small_names], LANES, SUBLANES) for src in (wts, grads, mom_m, mom_v)]
    outs = _adamw(*packs, "adamw_small", tr=packs[0].shape[0])
    shapes = [wts[n].shape for n in small_names]
    for dst, flat in zip((delta, new_m, new_v), outs):
        for n, a in zip(small_names, _unpack_rows(flat, shapes, LANES)):
            dst[n] = a
    return (loss, grad_x, *[grads[n] for n in WEIGHTS], *[delta[n] for n in WEIGHTS],
            *[new_m[n] for n in WEIGHTS], *[new_v[n] for n in WEIGHTS])
```

```python
import functools

import jax
import jax.numpy as jnp
from jax import lax
from jax.experimental import pallas as pl
from jax.experimental.pallas import tpu as pltpu

F32 = jnp.float32
BF16 = jnp.bfloat16
MXU_DTYPE = jnp.bfloat16

D_MODEL = 1024
DEPTH = 4
PLE_DIM = 256
ALPHA = (2 * DEPTH) ** 0.25
LN_EPS = 1e-5
RMS_EPS = 1e-6
HEAD_DIM = 64
POOL_WIDTH = 256
POOL_WINDOWS = (2, 4, 8, 16)
SSD_WIDTH = 384
SSD_HEADS = 6
SSD_STATE = 128
SSD_XBC = 896
SB_WIDTH = 384
IN_COLS = 2694
D_FF = 2816
N_DEV = 8
UP_SHARD = 2 * D_FF // N_DEV
DOWN_SHARD = D_FF // N_DEV

ADAM_LR = 0.001
ADAM_B1 = 0.9
ADAM_B2 = 0.999
ADAM_EPS = 1e-08
ADAM_WD = 0.01
ADAM_STEP = 10

LANES = 128
SUBLANES = 8
VMEM_LIMIT_BYTES = 56 * 1024 * 1024

H_COLS = 2816
H_BC = 0
H_POOL = 512
H_Q = 768
H_K = 1152
H_V = 1536
H_Z = 1920
H_XS = 2304
H_DT = 2688
SSD_CHUNK = 512
QB = 256
GLU_TILE = 256
MASKED_LOG = -1e30

NN = ((1,), (0,))
NT = ((1,), (1,))
TN = ((0,), (0,))


def _dot(a, b, dims=NN):
    return lax.dot_general(a.astype(MXU_DTYPE), b.astype(MXU_DTYPE), (dims, ((), ())), preferred_element_type=F32)


def _dot_exact01(x, m01, dims=NN, x_left=True, terms=3):
    acc = None
    r = x
    for _ in range(terms):
        hi = r.astype(BF16)
        ops = (hi, m01) if x_left else (m01, hi)
        part = lax.dot_general(ops[0], ops[1], (dims, ((), ())), preferred_element_type=F32)
        acc = part if acc is None else acc + part
        r = r - hi.astype(F32)
    return acc


def _sigmoid(v):
    return 1.0 / (1.0 + jnp.exp(-v))


def _silu(v):
    return v * _sigmoid(v)


def _dsilu(v):
    s = _sigmoid(v)
    return s * (1.0 + v * (1.0 - s))


def _softplus(v):
    return jnp.maximum(v, 0.0) + jnp.log(1.0 + jnp.exp(-jnp.abs(v)))


def _params(n_axes, side_effects=False):
    return pltpu.CompilerParams(dimension_semantics=("arbitrary",) * n_axes, vmem_limit_bytes=VMEM_LIMIT_BYTES,
                                has_side_effects=side_effects)


MESH_ID = pl.DeviceIdType.MESH
_ANY = pl.BlockSpec(memory_space=pl.ANY)


def _flip(v, bit):
    return 1 - v if bit else v


def _comm_counts(comm):
    return (0, 0) if comm is None else (len(comm["inputs"]), len(comm["out_shapes"]))


def _comm_call_args(comm):
    if comm is None:
        return [], [], [], []
    n = comm["n_xfers"]
    sems = [pltpu.SemaphoreType.DMA(((N_DEV - 1) * n,)), pltpu.SemaphoreType.DMA(((N_DEV - 1) * n,)),
            pltpu.SemaphoreType.DMA((n,))]
    return list(comm["inputs"]), [_ANY] * len(comm["out_shapes"]), list(comm["out_shapes"]), sems


def _comm_descs(comm, in_refs, tail_refs, with_recvs=True):
    n_out = len(comm["out_shapes"])
    out_refs, (send_sems, recv_sems, local_sems) = tail_refs[:n_out], tail_refs[n_out:n_out + 3]
    xfers = comm["xfers"](in_refs, out_refs)
    n = len(xfers)
    assert n == comm["n_xfers"]
    x, y, c = lax.axis_index("x"), lax.axis_index("y"), lax.axis_index("c")
    me = 4 * x + 2 * y + c
    local = [pltpu.make_async_copy(src_for(me), dst_for(me), local_sems.at[t]) for t, (src_for, dst_for) in enumerate(xfers)]
    sends, recvs = [], []
    for k in range(1, N_DEV):
        pid = (_flip(x, k & 4), _flip(y, k & 2), _flip(c, k & 1))
        peer = 4 * pid[0] + 2 * pid[1] + pid[2]
        for t, (src_for, dst_for) in enumerate(xfers):
            idx = (k - 1) * n + t
            sends.append(pltpu.make_async_remote_copy(
                src_ref=src_for(peer), dst_ref=dst_for(me), send_sem=send_sems.at[idx], recv_sem=recv_sems.at[idx],
                device_id=pid, device_id_type=MESH_ID))
            if with_recvs:
                recvs.append(pltpu.make_async_remote_copy(
                    src_ref=src_for(peer), dst_ref=dst_for(peer), send_sem=send_sems.at[idx], recv_sem=recv_sems.at[idx],
                    device_id=pid, device_id_type=MESH_ID))
    return local, sends, recvs


def _comm_start(descs):
    local, sends, _ = descs
    for cp in local + sends:
        cp.start()


def _comm_wait(descs):
    local, sends, recvs = descs
    for cp in recvs:
        cp.wait_recv()
    for cp in sends:
        cp.wait_send()
    for cp in local:
        cp.wait()


def _comm_hosted(comm, in_refs, tail_refs, grid):
    if comm is None:
        return
    ids = [pl.program_id(a) for a in range(len(grid))]
    first = functools.reduce(jnp.logical_and, [i == 0 for i in ids])
    last = functools.reduce(jnp.logical_and, [i == g - 1 for i, g in zip(ids, grid)])

    @pl.when(first)
    def _():
        _comm_start(_comm_descs(comm, in_refs, tail_refs, with_recvs=False))

    @pl.when(last)
    def _():
        _comm_wait(_comm_descs(comm, in_refs, tail_refs))


def _gather_call_two_level(comm, name):
    n_in, n_out = len(comm["inputs"]), len(comm["out_shapes"])

    def body(*refs):
        in_refs, out_refs = refs[:n_in], refs[n_in:n_in + n_out]
        send_sems, recv_sems, local_sems = refs[n_in + n_out:]
        xfers = comm["xfers"](in_refs, out_refs)
        x, y, c = lax.axis_index("x"), lax.axis_index("y"), lax.axis_index("c")
        pos = lambda px, py, pc: 4 * px + 2 * py + pc
        me, sibling = (x, y, c), (x, y, 1 - c)
        chips = [(1 - x, y), (x, 1 - y), (1 - x, 1 - y)]

        def copy(t, k, block, to, own):
            src_for, dst_for = xfers[t]
            return pltpu.make_async_remote_copy(
                src_ref=src_for(pos(*me)) if own else dst_for(pos(*block)), dst_ref=dst_for(pos(*block)),
                send_sem=send_sems.at[7 * t + k], recv_sem=recv_sems.at[7 * t + k], device_id=to, device_id_type=MESH_ID)

        nt = len(xfers)
        local = [pltpu.make_async_copy(xfers[t][0](pos(*me)), xfers[t][1](pos(*me)), local_sems.at[t]) for t in range(nt)]
        first = [copy(t, 0, me, sibling, True) for t in range(nt)]
        first += [copy(t, 1 + j, me, (*chip, c), True) for t in range(nt) for j, chip in enumerate(chips)]
        for cp in local + first:
            cp.start()
        passed = []
        for j, chip in enumerate(chips):
            for t in range(nt):
                copy(t, 1 + j, (*chip, c), me, False).wait_recv()
                fwd = copy(t, 4 + j, (*chip, c), sibling, False)
                fwd.start()
                passed.append(fwd)
        for t in range(nt):
            copy(t, 0, sibling, me, False).wait_recv()
            for j, chip in enumerate(chips):
                copy(t, 4 + j, (*chip, 1 - c), me, False).wait_recv()
        for cp in first + passed:
            cp.wait_send()
        for cp in local:
            cp.wait()

    n = comm["n_xfers"]
    return pl.pallas_call(
        body, name=name, in_specs=[_ANY] * n_in, out_specs=[_ANY] * n_out, out_shape=list(comm["out_shapes"]),
        scratch_shapes=[pltpu.SemaphoreType.DMA((7 * n,)), pltpu.SemaphoreType.DMA((7 * n,)), pltpu.SemaphoreType.DMA((n,))],
        compiler_params=pltpu.CompilerParams(has_side_effects=True))(*comm["inputs"])


def _rows(ref, j, n):
    return ref.at[pl.ds(pl.multiple_of(j * n, SUBLANES), n), :]


def _gather_job(sh, conv=None):
    conv = list(conv or [])
    sds = jax.ShapeDtypeStruct
    out_shapes = [sds((D_MODEL, H_COLS), MXU_DTYPE), sds((D_MODEL, D_MODEL), MXU_DTYPE), sds((N_DEV, D_MODEL, UP_SHARD), MXU_DTYPE),
                  sds((D_FF, D_MODEL), MXU_DTYPE), sds((D_MODEL, D_MODEL), MXU_DTYPE), sds((PLE_DIM, D_MODEL), MXU_DTYPE)]
    out_shapes += [sds((N_DEV,) + a.shape, a.dtype) for a in conv]

    def xfers(ins, outs):
        whole = lambda a: (lambda j: a)
        r = [(whole(ins[0]), lambda j: _rows(outs[0], j, 128)),
             (whole(ins[1]), lambda j: _rows(outs[1], lax.rem(j + 6, N_DEV), 128)),
             (whole(ins[2]), lambda j: outs[2].at[j]),
             (whole(ins[3]), lambda j: _rows(outs[3], j, DOWN_SHARD)),
             (whole(ins[4]), lambda j: _rows(outs[4], j, 128)),
             (whole(ins[5]), lambda j: outs[5].at[:, pl.ds(pl.multiple_of(j * LANES, LANES), LANES)])]
        for t in range(len(conv)):
            r.append((whole(ins[6 + t]), lambda j, o=outs[6 + t]: o.at[j]))
        return r

    return dict(inputs=[sh[n] for n in BIG] + conv, out_shapes=out_shapes, xfers=xfers, n_xfers=6 + len(conv))


_SHARD_SHAPES = {"w_in": (128, H_COLS), "w_out": (128, D_MODEL), "ffn_w_up": (D_MODEL, UP_SHARD), "ffn_w_down": (DOWN_SHARD, D_MODEL),
                 "ple_w_gate": (128, D_MODEL), "ple_w_proj": (PLE_DIM, LANES)}


def _exchange_job(items):
    def source(name, ref):
        if name in ("w_in", "ple_w_gate"):
            return lambda j: _rows(ref, j, 128)
        if name == "w_out":
            return lambda j: _rows(ref, lax.rem(j + 6, N_DEV), 128)
        if name == "ffn_w_up":
            return lambda j: ref.at[j]
        if name == "ffn_w_down":
            return lambda j: _rows(ref, j, DOWN_SHARD)
        assert name == "ple_w_proj"
        return lambda j: ref.at[:, pl.ds(pl.multiple_of(j * LANES, LANES), LANES)]

    def xfers(ins, outs):
        return [(source(name, i), lambda j, o=o: o.at[j]) for (name, _), i, o in zip(items, ins, outs)]

    return dict(inputs=[g for _, g in items], xfers=xfers, n_xfers=len(items),
                out_shapes=[jax.ShapeDtypeStruct((N_DEV,) + _SHARD_SHAPES[name], F32) for name, _ in items])


def _pick(n, pref):
    if n <= pref:
        return n
    for t in range(pref - pref % LANES, 0, -LANES):
        if n % t == 0:
            return t
    raise ValueError((n, pref))


def _mm(a, b, mode, out_dtype, name, tm=512, tn=512, tk=1024, add=None, add_coef=1.0, comm=None):
    n_in, n_out = _comm_counts(comm)
    if mode == "nn":
        (m, k), (k2, n) = a.shape, b.shape
    elif mode == "nt":
        (m, k), (n, k2) = a.shape, b.shape
    else:
        (k, m), (k2, n) = a.shape, b.shape
    assert k == k2, (a.shape, b.shape, mode)
    tm, tn, tk = _pick(m, tm), _pick(n, tn), _pick(k, tk)
    nk = k // tk
    dims = {"nn": NN, "nt": NT, "tn": TN}[mode]

    def body(*refs):
        a_ref, b_ref = refs[:2]
        n_add = int(add is not None)
        add_ref = refs[2] if n_add else None
        o_ref = refs[2 + n_add + n_in]
        tail = refs[3 + n_add + n_in:]
        if comm is not None:
            _comm_hosted(comm, refs[2 + n_add:2 + n_add + n_in], tail[:n_out] + tail[n_out + int(nk > 1):],
                         (m // tm, n // tn, nk))

        def finish(r):
            if add_ref is not None:
                r = r + add_coef * add_ref[...]
            o_ref[...] = r.astype(out_dtype)

        if nk == 1:
            finish(_dot(a_ref[...], b_ref[...], dims))
            return
        acc_ref = tail[n_out]
        kk = pl.program_id(2)

        @pl.when(kk == 0)
        def _():
            acc_ref[...] = jnp.zeros_like(acc_ref)

        acc_ref[...] += _dot(a_ref[...], b_ref[...], dims)

        @pl.when(kk == nk - 1)
        def _():
            finish(acc_ref[...])

    if mode == "tn":
        a_spec = pl.BlockSpec((tk, tm), lambda i, j, kk: (kk, i))
    else:
        a_spec = pl.BlockSpec((tm, tk), lambda i, j, kk: (i, kk))
    if mode == "nt":
        b_spec = pl.BlockSpec((tn, tk), lambda i, j, kk: (j, kk))
    else:
        b_spec = pl.BlockSpec((tk, tn), lambda i, j, kk: (kk, j))
    o_spec = pl.BlockSpec((tm, tn), lambda i, j, kk: (i, j))
    in_specs = [a_spec, b_spec] + ([o_spec] if add is not None else [])
    args = (a, b) + ((add,) if add is not None else ())
    c_in, c_specs, c_shapes, c_scratch = _comm_call_args(comm)
    res = pl.pallas_call(
        body, name=name, grid=(m // tm, n // tn, nk), in_specs=in_specs + [_ANY] * n_in, out_specs=[o_spec] + c_specs,
        out_shape=[jax.ShapeDtypeStruct((m, n), out_dtype)] + c_shapes,
        scratch_shapes=([pltpu.VMEM((tm, tn), F32)] if nk > 1 else []) + c_scratch,
        compiler_params=_params(3, comm is not None),
    )(*args, *c_in)
    return res[0] if comm is None else (res[0], res[1:])


def _ln_fwd(x, add, gb, name, gp=None, pp=None, tr=512):
    t, d = x.shape
    tr = _pick(t, tr)
    with_ple = gp is not None

    def body(*refs):
        if with_ple:
            x_ref, a_ref, gp_ref, pp_ref, gb_ref, y_ref, r_ref = refs
        else:
            x_ref, a_ref, gb_ref, y_ref, r_ref = refs
        r = ALPHA * x_ref[...] + a_ref[...]
        if with_ple:
            r = r + _sigmoid(gp_ref[...]) * pp_ref[...]
        mu = jnp.mean(r, axis=1, keepdims=True)
        xc = r - mu
        var = jnp.mean(xc * xc, axis=1, keepdims=True)
        y_ref[...] = xc * lax.rsqrt(var + LN_EPS) * gb_ref[0:1, :] + gb_ref[1:2, :]
        r_ref[...] = r

    row = pl.BlockSpec((tr, d), lambda i: (i, 0))
    vec = pl.BlockSpec((2, d), lambda i: (0, 0))
    n_row = 4 if with_ple else 2
    args = (x, add) + ((gp, pp) if with_ple else ()) + (gb,)
    return pl.pallas_call(
        body, name=name, grid=(t // tr,), in_specs=[row] * n_row + [vec], out_specs=[row, row],
        out_shape=[jax.ShapeDtypeStruct((t, d), F32)] * 2, compiler_params=_params(1),
    )(*args)


def _ln_bwd(r, gb, dy, name, gp=None, pp=None, tr=512):
    t, d = r.shape
    tr = _pick(t, tr)
    with_ple = gp is not None

    def body(*refs):
        if with_ple:
            r_ref, dy_ref, gp_ref, pp_ref, gb_ref, dr_ref, dgp_ref, dpp_ref, st_ref = refs
        else:
            r_ref, dy_ref, gb_ref, dr_ref, st_ref = refs
        i = pl.program_id(0)

        @pl.when(i == 0)
        def _():
            st_ref[...] = jnp.zeros_like(st_ref)

        rv = r_ref[...]
        dy_v = dy_ref[...]
        mu = jnp.mean(rv, axis=1, keepdims=True)
        xc = rv - mu
        var = jnp.mean(xc * xc, axis=1, keepdims=True)
        rstd = lax.rsqrt(var + LN_EPS)
        xhat = xc * rstd
        dxh = dy_v * gb_ref[0:1, :]
        m1 = jnp.mean(dxh, axis=1, keepdims=True)
        m2 = jnp.mean(dxh * xhat, axis=1, keepdims=True)
        dr = rstd * (dxh - m1 - xhat * m2)
        dr_ref[...] = dr
        rid = lax.broadcasted_iota(jnp.int32, (2, d), 0)
        dg = jnp.sum(dy_v * xhat, axis=0, keepdims=True)
        db = jnp.sum(dy_v, axis=0, keepdims=True)
        st_ref[...] += jnp.where(rid == 0, dg, db)
        if with_ple:
            sg = _sigmoid(gp_ref[...])
            ppv = pp_ref[...]
            dgp_ref[...] = (dr * ppv * sg * (1.0 - sg)).astype(dgp_ref.dtype)
            dpp_ref[...] = (dr * sg).astype(dpp_ref.dtype)

    row = pl.BlockSpec((tr, d), lambda i: (i, 0))
    vec = pl.BlockSpec((2, d), lambda i: (0, 0))
    if with_ple:
        in_specs, args = [row] * 4 + [vec], (r, dy, gp, pp, gb)
        out_specs = [row, row, row, vec]
        out_shape = [jax.ShapeDtypeStruct((t, d), F32), jax.ShapeDtypeStruct((t, d), MXU_DTYPE),
                     jax.ShapeDtypeStruct((t, d), MXU_DTYPE), jax.ShapeDtypeStruct((2, d), F32)]
    else:
        in_specs, args = [row] * 2 + [vec], (r, dy, gb)
        out_specs = [row, vec]
        out_shape = [jax.ShapeDtypeStruct((t, d), F32), jax.ShapeDtypeStruct((2, d), F32)]
    return pl.pallas_call(body, name=name, grid=(t // tr,), in_specs=in_specs, out_specs=out_specs,
                          out_shape=out_shape, compiler_params=_params(1))(*args)


def _loss_grad(y, target, name, tr=512):
    t, d = y.shape
    tr = _pick(t, tr)

    def body(y_ref, t_ref, dy_ref, l_ref):
        i = pl.program_id(0)

        @pl.when(i == 0)
        def _():
            l_ref[...] = jnp.zeros_like(l_ref)

        e = y_ref[...] - t_ref[...]
        dy_ref[...] = e * (1.0 / d)
        per_tok = jnp.mean(e * e, axis=1, keepdims=True)
        l_ref[...] += 0.5 * jnp.sum(per_tok, axis=0, keepdims=True)

    row = pl.BlockSpec((tr, d), lambda i: (i, 0))
    acc = pl.BlockSpec((SUBLANES, LANES), lambda i: (0, 0))
    return pl.pallas_call(body, name=name, grid=(t // tr,), in_specs=[row, row], out_specs=[row, acc],
                          out_shape=[jax.ShapeDtypeStruct((t, d), F32), jax.ShapeDtypeStruct((SUBLANES, LANES), F32)],
                          compiler_params=_params(1))(y, target)


def _shift_down(v, k, row):
    return jnp.where(row >= k, pltpu.roll(v, k, 0), 0.0)


def _shift_up(v, k, row):
    n = v.shape[0]
    return jnp.where(row < n - k, pltpu.roll(v, n - k, 0), 0.0)


def _pool_window(lane):
    grp = lane // HEAD_DIM
    return jnp.where(grp == 0, POOL_WINDOWS[0], jnp.where(grp == 1, POOL_WINDOWS[1],
                     jnp.where(grp == 2, POOL_WINDOWS[2], POOL_WINDOWS[3])))


def _pool_select(lane, s2, s4, s8, s16):
    grp = lane // HEAD_DIM
    return jnp.where(grp == 0, s2, jnp.where(grp == 1, s4, jnp.where(grp == 2, s8, s16)))


def _pooled(u, row, lane):
    s2 = u + _shift_down(u, 1, row)
    s4 = s2 + _shift_down(s2, 2, row)
    s8 = s4 + _shift_down(s4, 4, row)
    s16 = s8 + _shift_down(s8, 8, row)
    cnt = jnp.minimum(row + 1, _pool_window(lane)).astype(F32)
    return _pool_select(lane, s2, s4, s8, s16) / cnt - u, cnt


def _pool_fwd(h, wbd, scale, nb, s, name):
    def body(u_ref, w_ref, sc_ref, o_ref):
        u = u_ref[...]
        row = lax.broadcasted_iota(jnp.int32, u.shape, 0)
        lane = lax.broadcasted_iota(jnp.int32, u.shape, 1)
        pooled, _ = _pooled(u, row, lane)
        o_ref[...] = (_dot(pooled, w_ref[...]) * sc_ref[...]).astype(o_ref.dtype)

    wb = POOL_WIDTH
    return pl.pallas_call(
        body, name=name, grid=(nb,),
        in_specs=[pl.BlockSpec((s, wb), lambda b: (b, H_POOL // wb)), pl.BlockSpec((wb, wb), lambda b: (0, 0)),
                  pl.BlockSpec((1, wb), lambda b: (0, 0))],
        out_specs=pl.BlockSpec((s, wb), lambda b: (b, 0)),
        out_shape=jax.ShapeDtypeStruct((nb * s, wb), MXU_DTYPE), compiler_params=_params(1),
    )(h, wbd, scale)


def _pool_bwd(h, dmix, wbd, scale, nb, s, name):
    wb = POOL_WIDTH

    def body(u_ref, do_ref, w_ref, sc_ref, du_ref, dw_ref, ds_ref):
        b = pl.program_id(0)

        @pl.when(b == 0)
        def _():
            dw_ref[...] = jnp.zeros_like(dw_ref)
            ds_ref[...] = jnp.zeros_like(ds_ref)

        u = u_ref[...]
        row = lax.broadcasted_iota(jnp.int32, u.shape, 0)
        lane = lax.broadcasted_iota(jnp.int32, u.shape, 1)
        pooled, cnt = _pooled(u, row, lane)
        mixed = _dot(pooled, w_ref[...])
        do = do_ref[...]
        ds_ref[...] += jnp.sum(do * mixed, axis=0, keepdims=True)
        dm = do * sc_ref[...]
        dw_ref[...] += _dot(pooled, dm, TN)
        dpool = _dot(dm, w_ref[...], NT)
        qv = dpool / cnt
        f2 = qv + _shift_up(qv, 1, row)
        f4 = f2 + _shift_up(f2, 2, row)
        f8 = f4 + _shift_up(f4, 4, row)
        f16 = f8 + _shift_up(f8, 8, row)
        du_ref[...] = (_pool_select(lane, f2, f4, f8, f16) - dpool).astype(du_ref.dtype)

    return pl.pallas_call(
        body, name=name, grid=(nb,),
        in_specs=[pl.BlockSpec((s, wb), lambda b: (b, H_POOL // wb)), pl.BlockSpec((s, wb), lambda b: (b, 3)),
                  pl.BlockSpec((wb, wb), lambda b: (0, 0)), pl.BlockSpec((1, wb), lambda b: (0, 0))],
        out_specs=[pl.BlockSpec((s, wb), lambda b: (b, 0)), pl.BlockSpec((wb, wb), lambda b: (0, 0)),
                   pl.BlockSpec((1, wb), lambda b: (0, 0))],
        out_shape=[jax.ShapeDtypeStruct((nb * s, wb), MXU_DTYPE), jax.ShapeDtypeStruct((wb, wb), F32),
                   jax.ShapeDtypeStruct((1, wb), F32)],
        compiler_params=_params(1),
    )(h, dmix, wbd, scale)


def _glu_conv(x, w_ref, b_ref, row):
    return (b_ref[...] + w_ref[2:3, :] * x + w_ref[1:2, :] * _shift_down(x, 1, row)
            + w_ref[0:1, :] * _shift_down(x, 2, row))


def _glu_fwd(up, cw, cb, nb, s, name):
    wt = 2 * GLU_TILE
    nt = up.shape[1] // wt

    def body(u_ref, w_ref, b_ref, o_ref):
        x = u_ref[...]
        row = lax.broadcasted_iota(jnp.int32, x.shape, 0)
        c = _glu_conv(x, w_ref, b_ref, row)
        o_ref[...] = (_silu(c[:, :GLU_TILE]) * c[:, GLU_TILE:]).astype(o_ref.dtype)

    return pl.pallas_call(
        body, name=name, grid=(nt, nb),
        in_specs=[pl.BlockSpec((s, wt), lambda j, b: (b, j)), pl.BlockSpec((3, wt), lambda j, b: (0, j)),
                  pl.BlockSpec((1, wt), lambda j, b: (0, j))],
        out_specs=pl.BlockSpec((s, GLU_TILE), lambda j, b: (b, j)),
        out_shape=jax.ShapeDtypeStruct((nb * s, nt * GLU_TILE), MXU_DTYPE), compiler_params=_params(2),
    )(up, cw, cb)


def _glu_bwd(up, dact, cw, cb, nb, s, name):
    wt = 2 * GLU_TILE
    nt = up.shape[1] // wt

    def body(u_ref, da_ref, w_ref, b_ref, du_ref, acc_ref):
        b = pl.program_id(1)

        @pl.when(b == 0)
        def _():
            acc_ref[...] = jnp.zeros_like(acc_ref)

        x = u_ref[...]
        row = lax.broadcasted_iota(jnp.int32, x.shape, 0)
        x1 = _shift_down(x, 1, row)
        x2 = _shift_down(x, 2, row)
        c = b_ref[...] + w_ref[2:3, :] * x + w_ref[1:2, :] * x1 + w_ref[0:1, :] * x2
        gate, val = c[:, :GLU_TILE], c[:, GLU_TILE:]
        da = da_ref[...]
        dc = jnp.concatenate([da * val * _dsilu(gate), da * _silu(gate)], axis=1)
        dx = (w_ref[2:3, :] * dc + w_ref[1:2, :] * _shift_up(dc, 1, row) + w_ref[0:1, :] * _shift_up(dc, 2, row))
        du_ref[...] = dx.astype(du_ref.dtype)
        rid = lax.broadcasted_iota(jnp.int32, (SUBLANES, wt), 0)
        dw0 = jnp.sum(dc * x2, axis=0, keepdims=True)
        dw1 = jnp.sum(dc * x1, axis=0, keepdims=True)
        dw2 = jnp.sum(dc * x, axis=0, keepdims=True)
        db = jnp.sum(dc, axis=0, keepdims=True)
        acc_ref[...] += (jnp.where(rid == 0, dw0, 0.0) + jnp.where(rid == 1, dw1, 0.0)
                         + jnp.where(rid == 2, dw2, 0.0) + jnp.where(rid == 3, db, 0.0))

    return pl.pallas_call(
        body, name=name, grid=(nt, nb),
        in_specs=[pl.BlockSpec((s, wt), lambda j, b: (b, j)), pl.BlockSpec((s, GLU_TILE), lambda j, b: (b, j)),
                  pl.BlockSpec((3, wt), lambda j, b: (0, j)), pl.BlockSpec((1, wt), lambda j, b: (0, j))],
        out_specs=[pl.BlockSpec((s, wt), lambda j, b: (b, j)), pl.BlockSpec((SUBLANES, wt), lambda j, b: (0, j))],
        out_shape=[jax.ShapeDtypeStruct((nb * s, nt * wt), MXU_DTYPE), jax.ShapeDtypeStruct((SUBLANES, nt * wt), F32)],
        compiler_params=_params(2),
    )(up, dact, cw, cb)


def _sb_constants():
    row = lax.broadcasted_iota(jnp.int32, (QB, QB), 0)
    col = lax.broadcasted_iota(jnp.int32, (QB, QB), 1)
    return jnp.stack([row > col, row < col, col < row]).astype(BF16)


_SB_CONST_SPEC = pl.BlockSpec((3, QB, QB), lambda b, p, i: (0, 0, 0))


def _sb_fwd(h, nb, s, name, comm=None):
    nq = s // QB
    scale = HEAD_DIM ** -0.5
    n_in, n_out = _comm_counts(comm)

    def body(q_ref, k_ref, v_ref, tri_ref, *rest):
        o_ref = rest[n_in]
        i = pl.program_id(2)
        _comm_hosted(comm, rest[:n_in], rest[n_in + 1:], (nb, 3, nq))
        sls = [slice(hd * HEAD_DIM, (hd + 1) * HEAD_DIM) for hd in range(2)]
        qs = [(q_ref[:, sl] * scale).astype(MXU_DTYPE) for sl in sls]

        def scores(hd, j, diagonal=False):
            r0 = pl.multiple_of(j * QB, QB)
            z = _dot(qs[hd], k_ref[pl.ds(r0, QB), sls[hd]], NT)
            ln = -_softplus(z)
            ls = z + ln
            if diagonal:
                low = tri_ref[2] > 0
                ln = jnp.where(low, ln, 0.0)
                ls = jnp.where(low, ls, MASKED_LOG)
            return ls, _dot_exact01(ln, tri_ref[0], terms=2), jnp.sum(ln, axis=1, keepdims=True)

        def output(hd, j, ls, tl, ct):
            r0 = pl.multiple_of(j * QB, QB)
            return _dot(jnp.exp(ls + tl + ct), v_ref[pl.ds(r0, QB), sls[hd]])

        def group(blocks, carry, diagonal_first=False):
            sc = [[scores(hd, j, diagonal_first and n == 0) for n, j in enumerate(blocks)] for hd in range(2)]
            out = []
            for hd in range(2):
                a, c = carry[hd]
                for (ls, tl, sm), j in zip(sc[hd], blocks):
                    a = a + output(hd, j, ls, tl, c)
                    c = c + sm
                out.append((a, c))
            return tuple(out)

        start = (jnp.zeros((QB, HEAD_DIM), F32), jnp.zeros((QB, 1), F32))
        below = jnp.minimum(i, 1)
        left = i - below
        carry = lax.fori_loop(0, below, lambda t, c: group([i, i - 1], c, True), (start, start))
        carry = lax.fori_loop(0, 1 - below, lambda t, c: group([i], c, True), carry)
        carry = lax.fori_loop(0, left // 2, lambda t, c: group([left - 1 - 2 * t, left - 2 - 2 * t], c), carry)
        carry = lax.fori_loop(0, left % 2, lambda t, c: group([0], c), carry)
        o_ref[:, sls[0]] = carry[0][0].astype(o_ref.dtype)
        o_ref[:, sls[1]] = carry[1][0].astype(o_ref.dtype)

    qspec = lambda off: pl.BlockSpec((QB, LANES), lambda b, p, i: (b * nq + i, off // LANES + p))
    kvspec = lambda off: pl.BlockSpec((s, LANES), lambda b, p, i: (b, off // LANES + p))
    c_in, c_specs, c_shapes, c_scratch = _comm_call_args(comm)
    res = pl.pallas_call(
        body, name=name, grid=(nb, 3, nq), in_specs=[qspec(H_Q), kvspec(H_K), kvspec(H_V), _SB_CONST_SPEC] + [_ANY] * n_in,
        out_specs=[pl.BlockSpec((QB, LANES), lambda b, p, i: (b * nq + i, p))] + c_specs,
        out_shape=[jax.ShapeDtypeStruct((nb * s, SB_WIDTH), MXU_DTYPE)] + c_shapes, scratch_shapes=c_scratch,
        compiler_params=_params(3, comm is not None),
    )(h, h, h, _sb_constants(), *c_in)
    return res[0], res[1:]


def _sb_bwd(h, dmix, nb, s, name, comm=None):
    nq = s // QB
    scale = HEAD_DIM ** -0.5
    n_in, n_out = _comm_counts(comm)

    def body(q_ref, k_ref, v_ref, do_ref, tri_ref, *rest):
        dq_ref, dk_out, dv_out = rest[n_in:n_in + 3]
        p_buf, ls_buf, dk_ref, dv_ref = rest[n_in + 3 + n_out:n_in + 7 + n_out]
        i = pl.program_id(2)
        _comm_hosted(comm, rest[:n_in], rest[n_in + 3:n_in + 3 + n_out] + rest[n_in + 7 + n_out:], (nb, 3, nq))

        @pl.when(i == 0)
        def _():
            dk_ref[...] = jnp.zeros_like(dk_ref)
            dv_ref[...] = jnp.zeros_like(dv_ref)

        sls = [slice(hd * HEAD_DIM, (hd + 1) * HEAD_DIM) for hd in range(2)]
        q_raw = [q_ref[:, sl].astype(MXU_DTYPE) for sl in sls]
        qs = [(q_ref[:, sl] * scale).astype(MXU_DTYPE) for sl in sls]
        do = [do_ref[:, sl].astype(MXU_DTYPE) for sl in sls]

        def down_scores(hd, j, diagonal):
            r0 = pl.multiple_of(j * QB, QB)
            z = _dot(qs[hd], k_ref[pl.ds(r0, QB), sls[hd]], NT)
            ln = -_softplus(z)
            ls = z + ln
            if diagonal:
                low = tri_ref[2] > 0
                ln = jnp.where(low, ln, 0.0)
                ls = jnp.where(low, ls, MASKED_LOG)
            da = _dot(do[hd], v_ref[pl.ds(r0, QB), sls[hd]], NT)
            return ls, _dot_exact01(ln, tri_ref[0], terms=2), jnp.sum(ln, axis=1, keepdims=True), da

        def down_group(blocks, carry, diagonal_first=False):
            sc = [[down_scores(hd, j, diagonal_first and n == 0) for n, j in enumerate(blocks)] for hd in range(2)]
            out = []
            for hd in range(2):
                ct = carry[hd]
                for (ls, tl, sm, da), j in zip(sc[hd], blocks):
                    r0 = pl.multiple_of(j * QB, QB)
                    a = jnp.exp(ls + tl + ct)
                    p_buf[hd, j] = da * a
                    ls_buf[hd, j] = ls
                    dv_ref[pl.ds(r0, QB), sls[hd]] += _dot(a, do[hd], TN)
                    ct = ct + sm
                out.append(ct)
            return tuple(out)

        zero = jnp.zeros((QB, 1), F32)
        below = jnp.minimum(i, 1)
        left = i - below
        carry = lax.fori_loop(0, below, lambda t, c: down_group([i, i - 1], c, True), (zero, zero))
        carry = lax.fori_loop(0, 1 - below, lambda t, c: down_group([i], c, True), carry)
        carry = lax.fori_loop(0, left // 2, lambda t, c: down_group([left - 1 - 2 * t, left - 2 - 2 * t], c), carry)
        lax.fori_loop(0, left % 2, lambda t, c: down_group([0], c), carry)

        def up_group(blocks, carry):
            ld = []
            for hd in range(2):
                ld.append([])
                for j in blocks:
                    pj = p_buf[hd, j]
                    ld[hd].append((pj, jnp.exp(ls_buf[hd, j]), _dot_exact01(pj, tri_ref[1]), jnp.sum(pj, axis=1, keepdims=True)))
            out = []
            for hd in range(2):
                dq, cp = carry[hd]
                for (pj, sg, cm, sm), j in zip(ld[hd], blocks):
                    r0 = pl.multiple_of(j * QB, QB)
                    dz = (pj * (1.0 - sg) - (cp + cm) * sg) * scale
                    dk_ref[pl.ds(r0, QB), sls[hd]] += _dot(dz, q_raw[hd], TN)
                    dq = dq + _dot(dz, k_ref[pl.ds(r0, QB), sls[hd]])
                    cp = cp + sm
                out.append((dq, cp))
            return tuple(out)

        start = (jnp.zeros((QB, HEAD_DIM), F32), zero)
        odd = (i + 1) % 2
        carry = lax.fori_loop(0, odd, lambda t, c: up_group([0], c), (start, start))
        carry = lax.fori_loop(0, (i + 1) // 2, lambda t, c: up_group([odd + 2 * t, odd + 2 * t + 1], c), carry)
        dq_ref[:, sls[0]] = carry[0][0].astype(dq_ref.dtype)
        dq_ref[:, sls[1]] = carry[1][0].astype(dq_ref.dtype)

        @pl.when(i == nq - 1)
        def _():
            dk_out[...] = dk_ref[...].astype(dk_out.dtype)
            dv_out[...] = dv_ref[...].astype(dv_out.dtype)

    qspec = lambda off: pl.BlockSpec((QB, LANES), lambda b, p, i: (b * nq + i, off // LANES + p))
    kvspec = lambda off: pl.BlockSpec((s, LANES), lambda b, p, i: (b, off // LANES + p))
    blk_out = pl.BlockSpec((QB, LANES), lambda b, p, i: (b * nq + i, p))
    seq_out = pl.BlockSpec((s, LANES), lambda b, p, i: (b, p))
    shp = jax.ShapeDtypeStruct((nb * s, SB_WIDTH), MXU_DTYPE)
    c_in, c_specs, c_shapes, c_scratch = _comm_call_args(comm)
    res = pl.pallas_call(
        body, name=name, grid=(nb, 3, nq),
        in_specs=[qspec(H_Q), kvspec(H_K), kvspec(H_V), pl.BlockSpec((QB, LANES), lambda b, p, i: (b * nq + i, 3 + p)),
                  _SB_CONST_SPEC] + [_ANY] * n_in,
        out_specs=[blk_out, seq_out, seq_out] + c_specs, out_shape=[shp, shp, shp] + c_shapes,
        scratch_shapes=[pltpu.VMEM((2, nq, QB, QB), F32), pltpu.VMEM((2, nq, QB, QB), F32),
                        pltpu.VMEM((s, LANES), F32), pltpu.VMEM((s, LANES), F32)] + c_scratch,
        compiler_params=_params(3, comm is not None),
    )(h, h, h, dmix, _sb_constants(), *c_in)
    return res[0], res[1], res[2], res[3:]


def _ssd_conv(cur_ref, halo_ref, w_ref, b_ref, ext_ref, first):
    n = SSD_CHUNK
    cur = cur_ref[...]
    ext_ref[0:SUBLANES, :] = jnp.where(first, 0.0, halo_ref[...])
    ext_ref[SUBLANES:SUBLANES + n, :] = cur
    return (b_ref[...] + w_ref[3:4, :] * cur + w_ref[2:3, :] * ext_ref[pl.ds(SUBLANES - 1, n), :]
            + w_ref[1:2, :] * ext_ref[pl.ds(SUBLANES - 2, n), :] + w_ref[0:1, :] * ext_ref[pl.ds(SUBLANES - 3, n), :])


def _ssd_tri():
    row = lax.broadcasted_iota(jnp.int32, (SSD_CHUNK, SSD_CHUNK), 0)
    col = lax.broadcasted_iota(jnp.int32, (SSD_CHUNK, SSD_CHUNK), 1)
    return row, col


def _ssd_specs(nc, rev):
    n = SSD_CHUNK
    hb = n // SUBLANES

    def cidx(c):
        return (nc - 1 - c) if rev else c

    def blk(width, off):
        return pl.BlockSpec((n, width), lambda b, c: (b * nc + cidx(c), off // width))

    def halo(width, off):
        return pl.BlockSpec((SUBLANES, width), lambda b, c: (jnp.maximum((b * nc + cidx(c)) * hb - 1, 0), off // width))

    def full(shape):
        return pl.BlockSpec(shape, lambda b, c: (0,) * len(shape))

    return cidx, blk, halo, full


def _ssd_core_fwd(x, bc, dt, acum, acum_t, a_row, d_row, h_prev_ref, tri):
    n = SSD_CHUNK
    heads = []
    for g in range(2):
        bm = bc[:, g * SSD_STATE:(g + 1) * SSD_STATE]
        cm = bc[:, 2 * SSD_STATE + g * SSD_STATE: 2 * SSD_STATE + (g + 1) * SSD_STATE]
        gmat = _dot(cm, bm, NT)
        for r in range(3):
            hh = g * 3 + r
            hp = h_prev_ref[hh * HEAD_DIM:(hh + 1) * HEAD_DIM, :]
            heads.append(dict(g=g, hh=hh, bm=bm, cm=cm, gmat=gmat, hp=hp, cmh=_dot(cm, hp, NT)))
    for hd in heads:
        hh = hd["hh"]
        ac = acum[:, hh:hh + 1]
        ar = acum_t[hh:hh + 1, :]
        hd["dec"] = jnp.where(tri, jnp.exp(jnp.minimum(ac - ar, 0.0)), 0.0)
        hd["xh"] = x[:, hh * HEAD_DIM:(hh + 1) * HEAD_DIM]
        hd["dth"] = dt[:, hh:hh + 1]
        hd["xdt"] = hd["xh"] * hd["dth"]
        hd["ea"] = jnp.exp(ac)
        hd["m"] = hd["gmat"] * hd["dec"]
        hd["al"] = acum[n - 1:n, hh:hh + 1]
        hd["w"] = jnp.exp(hd["al"] - ac)
    for hd in heads:
        hd["yd"] = _dot(hd["m"], hd["xdt"])
    for hd in heads:
        hd["yo"] = hd["ea"] * hd["cmh"]
        hd["y"] = hd["yd"] + hd["yo"] + d_row[:, hd["hh"]:hd["hh"] + 1] * hd["xh"]
    return heads


def _ssd_prep(xs_ref, xsh_ref, bc_ref, bch_ref, dt_ref, cwx_ref, cbx_ref, cwb_ref, cbb_ref, vec_ref, xe_ref, be_ref, first):
    pre_x = _ssd_conv(xs_ref, xsh_ref, cwx_ref, cbx_ref, xe_ref, first)
    pre_bc = _ssd_conv(bc_ref, bch_ref, cwb_ref, cbb_ref, be_ref, first)
    x = _silu(pre_x)
    bc = _silu(pre_bc)
    dt_pre = dt_ref[...] + vec_ref[0:1, :]
    dt = _softplus(dt_pre)
    a_row = vec_ref[1:2, :]
    amat = dt * a_row
    row, col = _ssd_tri()
    upper = (row <= col).astype(BF16)
    lower = (col <= row).astype(BF16)
    acum = _dot_exact01(amat, lower, NN, x_left=False)
    acum_t = _dot_exact01(amat, upper, TN, x_left=True)
    return pre_x, pre_bc, x, bc, dt_pre, dt, a_row, acum, acum_t, row, col, upper


def _ssd_gate_norm(y, z, nw):
    lane = lax.broadcasted_iota(jnp.int32, y.shape, 1)
    g0 = lane < SSD_WIDTH // 2
    hg = y * _silu(z)
    sq = hg * hg
    ms0 = jnp.sum(jnp.where(g0, sq, 0.0), axis=1, keepdims=True) * (2.0 / SSD_WIDTH)
    ms1 = jnp.sum(jnp.where(g0, 0.0, sq), axis=1, keepdims=True) * (2.0 / SSD_WIDTH)
    rs = jnp.where(g0, lax.rsqrt(ms0 + RMS_EPS), lax.rsqrt(ms1 + RMS_EPS))
    return hg, rs, g0


def _ssd_fwd(h, cwx, cbx, cwb, cbb, vec, nw, nb, s, name):
    n = SSD_CHUNK
    nc = s // n
    _, blk, halo, full = _ssd_specs(nc, False)

    def body(bc_ref, bch_ref, z_ref, xs_ref, xsh_ref, dt_ref, cwx_ref, cbx_ref, cwb_ref, cbb_ref, vec_ref, nw_ref,
             o_ref, hs_ref, h_scr, xe_ref, be_ref, y_scr):
        c = pl.program_id(1)

        @pl.when(c == 0)
        def _():
            h_scr[...] = jnp.zeros_like(h_scr)

        (_, _, x, bc, _, dt, a_row, acum, acum_t, row, col, _) = _ssd_prep(
            xs_ref, xsh_ref, bc_ref, bch_ref, dt_ref, cwx_ref, cbx_ref, cwb_ref, cbb_ref, vec_ref, xe_ref, be_ref, c == 0)
        hs_ref[...] = h_scr[...]
        heads = _ssd_core_fwd(x, bc, dt, acum, acum_t, a_row, vec_ref[2:3, :], hs_ref, col <= row)
        for hd in heads:
            sl = slice(hd["hh"] * HEAD_DIM, (hd["hh"] + 1) * HEAD_DIM)
            y_scr[:, sl] = hd["y"]
            h_scr[sl, :] = jnp.exp(hd["al"]) * hd["hp"] + _dot(hd["xdt"] * hd["w"], hd["bm"], TN)
        hg, rs, _ = _ssd_gate_norm(y_scr[...], z_ref[...], nw_ref[...])
        o_ref[...] = (hg * rs * nw_ref[...]).astype(o_ref.dtype)

    t = nb * s
    return pl.pallas_call(
        body, name=name, grid=(nb, nc),
        in_specs=[blk(512, H_BC), halo(512, H_BC), blk(384, H_Z), blk(384, H_XS), halo(384, H_XS), blk(128, H_DT),
                  full((4, 384)), full((1, 384)), full((4, 512)), full((1, 512)), full((SUBLANES, LANES)), full((1, 384))],
        out_specs=[pl.BlockSpec((n, SSD_WIDTH), lambda b, c: (b * nc + c, 0)),
                   pl.BlockSpec((None, SSD_WIDTH, SSD_STATE), lambda b, c: (b * nc + c, 0, 0))],
        out_shape=[jax.ShapeDtypeStruct((t, SSD_WIDTH), MXU_DTYPE),
                   jax.ShapeDtypeStruct((nb * nc, SSD_WIDTH, SSD_STATE), F32)],
        scratch_shapes=[pltpu.VMEM((SSD_WIDTH, SSD_STATE), F32), pltpu.VMEM((n + SUBLANES, 384), F32),
                        pltpu.VMEM((n + SUBLANES, 512), F32), pltpu.VMEM((n, SSD_WIDTH), F32)],
        compiler_params=_params(2),
    )(h, h, h, h, h, h, cwx, cbx, cwb, cbb, vec, nw)


def _ssd_bwd(h, hstate, dmix, cwx, cbx, cwb, cbb, vec, nw, nb, s, name):
    n = SSD_CHUNK
    nc = s // n
    cidx, blk, halo, full = _ssd_specs(nc, True)

    def body(bc_ref, bch_ref, z_ref, xs_ref, xsh_ref, dt_ref, hs_ref, do_ref, cwx_ref, cbx_ref, cwb_ref, cbb_ref,
             vec_ref, nw_ref, dz_ref, dxs_ref, dbc_ref, ddt_ref, gx_ref, gb_ref, gv_ref, gn_ref,
             dh_scr, xe_ref, be_ref, y_scr, dx_scr, dbc_scr, dxe_ref, dbe_ref, cx_ref, cb_ref):
        b = pl.program_id(0)
        c = pl.program_id(1)
        cc = nc - 1 - c

        @pl.when(jnp.logical_and(b == 0, c == 0))
        def _():
            gx_ref[...] = jnp.zeros_like(gx_ref)
            gb_ref[...] = jnp.zeros_like(gb_ref)
            gv_ref[...] = jnp.zeros_like(gv_ref)
            gn_ref[...] = jnp.zeros_like(gn_ref)

        @pl.when(c == 0)
        def _():
            dh_scr[...] = jnp.zeros_like(dh_scr)
            cx_ref[...] = jnp.zeros_like(cx_ref)
            cb_ref[...] = jnp.zeros_like(cb_ref)

        (pre_x, pre_bc, x, bc, dt_pre, dt, a_row, acum, acum_t, row, col, upper) = _ssd_prep(
            xs_ref, xsh_ref, bc_ref, bch_ref, dt_ref, cwx_ref, cbx_ref, cwb_ref, cbb_ref, vec_ref, xe_ref, be_ref, cc == 0)
        tri = col <= row
        d_row = vec_ref[2:3, :]
        heads = _ssd_core_fwd(x, bc, dt, acum, acum_t, a_row, d_row, hs_ref, tri)
        for hd in heads:
            y_scr[:, hd["hh"] * HEAD_DIM:(hd["hh"] + 1) * HEAD_DIM] = hd["y"]
        y = y_scr[...]
        z = z_ref[...]
        nwv = nw_ref[...]
        hg, rs, g0 = _ssd_gate_norm(y, z, nwv)
        do = do_ref[...]
        nrm = hg * rs
        gn_ref[...] += jnp.sum(do * nrm, axis=0, keepdims=True)
        dn = do * nwv
        dnn = dn * nrm
        mean0 = jnp.sum(jnp.where(g0, dnn, 0.0), axis=1, keepdims=True) * (2.0 / SSD_WIDTH)
        mean1 = jnp.sum(jnp.where(g0, 0.0, dnn), axis=1, keepdims=True) * (2.0 / SSD_WIDTH)
        dhg = rs * (dn - nrm * jnp.where(g0, mean0, mean1))
        dz_ref[...] = (dhg * y * _dsilu(z)).astype(dz_ref.dtype)
        dy = dhg * _silu(z)

        lane = lax.broadcasted_iota(jnp.int32, (n, LANES), 1)
        lane1 = lax.broadcasted_iota(jnp.int32, (1, LANES), 1)
        last_row = lax.broadcasted_iota(jnp.int32, (n, 1), 0) == n - 1
        dacum_col = jnp.zeros((n, LANES), F32)
        da_rowpart = jnp.zeros((n, LANES), F32)
        ddt = jnp.zeros((n, LANES), F32)
        dd_vec = jnp.zeros((1, LANES), F32)
        for hd in heads:
            sl = slice(hd["hh"] * HEAD_DIM, (hd["hh"] + 1) * HEAD_DIM)
            dyh = dy[:, sl]
            dhn = dh_scr[sl, :]
            hd.update(sl=sl, dyh=dyh, dhn=dhn, t1=_dot(dyh, hd["hp"]), dm=_dot(dyh, hd["xdt"], NT),
                      t2=_dot(hd["bm"], dhn, NT), mtdy=_dot(hd["m"], dyh, TN), xdhn=_dot(hd["xdt"], dhn),
                      dhp=_dot(dyh * hd["ea"], hd["cm"], TN))
        dgs, dbms, dcms = [], [], []
        for g in range(2):
            dg = jnp.zeros((n, n), F32)
            dbm = jnp.zeros((n, SSD_STATE), F32)
            dcm = jnp.zeros((n, SSD_STATE), F32)
            for hd in heads[3 * g:3 * g + 3]:
                hh, sl, dyh, dhn, t2 = hd["hh"], hd["sl"], hd["dyh"], hd["dhn"], hd["t2"]
                el = jnp.exp(hd["al"])
                dd_vec = dd_vec + jnp.where(lane1 == hh, jnp.sum(dyh * hd["xh"]), 0.0)
                dcm = dcm + hd["ea"] * hd["t1"]
                dg = dg + hd["dm"] * hd["dec"]
                e = hd["dm"] * hd["m"]
                dxdt = hd["mtdy"] + hd["w"] * t2
                dbm = dbm + hd["w"] * hd["xdhn"]
                dw_w = jnp.sum(hd["xdt"] * t2, axis=1, keepdims=True) * hd["w"]
                d_el = jnp.sum(dhn * hd["hp"])
                col_part = (jnp.sum(dyh * hd["yo"], axis=1, keepdims=True) + jnp.sum(e, axis=1, keepdims=True) - dw_w
                            + jnp.where(last_row, d_el * el + jnp.sum(dw_w), 0.0))
                dacum_col = dacum_col + jnp.where(lane == hh, col_part, 0.0)
                neg_colsum = -jnp.sum(e, axis=0, keepdims=True)
                rev = jnp.sum(jnp.where(row <= col, neg_colsum, 0.0), axis=1, keepdims=True)
                da_rowpart = da_rowpart + jnp.where(lane == hh, rev, 0.0)
                dh_scr[sl, :] = el * dhn + hd["dhp"]
                dx_scr[:, sl] = d_row[:, hh:hh + 1] * dyh + dxdt * hd["dth"]
                ddt = ddt + jnp.where(lane == hh, jnp.sum(dxdt * hd["xh"], axis=1, keepdims=True), 0.0)
            dgs.append(dg)
            dbms.append(dbm)
            dcms.append(dcm)
        for g in range(2):
            bm, cm = heads[3 * g]["bm"], heads[3 * g]["cm"]
            dbc_scr[:, g * SSD_STATE:(g + 1) * SSD_STATE] = dbms[g] + _dot(dgs[g], cm, TN)
            dbc_scr[:, 2 * SSD_STATE + g * SSD_STATE:2 * SSD_STATE + (g + 1) * SSD_STATE] = dcms[g] + _dot(dgs[g], bm)
        da_mat = _dot_exact01(dacum_col, upper, NN, x_left=False) + da_rowpart
        ddt = ddt + da_mat * a_row
        da_vec = jnp.sum(da_mat * dt, axis=0, keepdims=True)
        ddt_pre = jnp.where(lane < SSD_HEADS, ddt * _sigmoid(dt_pre), 0.0)
        ddt_ref[...] = ddt_pre.astype(ddt_ref.dtype)
        rid = lax.broadcasted_iota(jnp.int32, (SUBLANES, LANES), 0)
        gv_ref[...] += (jnp.where(rid == 0, jnp.sum(ddt_pre, axis=0, keepdims=True), 0.0)
                        + jnp.where(rid == 1, da_vec, 0.0) + jnp.where(rid == 2, dd_vec, 0.0))

        def conv_bwd(dpost, pre, w_ref, ext_ref, dext_ref, carry_ref, cur_ref, out_ref, g_ref, width):
            dco = dpost * _dsilu(pre)
            dext_ref[0:n, :] = dco
            dext_ref[n:n + SUBLANES, :] = carry_ref[...]
            out_ref[...] = (w_ref[3:4, :] * dco + w_ref[2:3, :] * dext_ref[pl.ds(1, n), :]
                            + w_ref[1:2, :] * dext_ref[pl.ds(2, n), :] + w_ref[0:1, :] * dext_ref[pl.ds(3, n), :]
                            ).astype(out_ref.dtype)
            carry_ref[...] = dco[0:SUBLANES, :]
            rid8 = lax.broadcasted_iota(jnp.int32, (SUBLANES, width), 0)
            acc = jnp.where(rid8 == 3, jnp.sum(dco * cur_ref[...], axis=0, keepdims=True), 0.0)
            for j in range(3):
                sh = ext_ref[pl.ds(SUBLANES - 3 + j, n), :]
                acc = acc + jnp.where(rid8 == j, jnp.sum(dco * sh, axis=0, keepdims=True), 0.0)
            acc = acc + jnp.where(rid8 == 4, jnp.sum(dco, axis=0, keepdims=True), 0.0)
            g_ref[...] += acc

        conv_bwd(dx_scr[...], pre_x, cwx_ref, xe_ref, dxe_ref, cx_ref, xs_ref, dxs_ref, gx_ref, 384)
        conv_bwd(dbc_scr[...], pre_bc, cwb_ref, be_ref, dbe_ref, cb_ref, bc_ref, dbc_ref, gb_ref, 512)

    t = nb * s
    rowblk = lambda width: pl.BlockSpec((n, width), lambda b, c: (b * nc + cidx(c), 0))
    return pl.pallas_call(
        body, name=name, grid=(nb, nc),
        in_specs=[blk(512, H_BC), halo(512, H_BC), blk(384, H_Z), blk(384, H_XS), halo(384, H_XS), blk(128, H_DT),
                  pl.BlockSpec((None, SSD_WIDTH, SSD_STATE), lambda b, c: (b * nc + cidx(c), 0, 0)),
                  pl.BlockSpec((n, SSD_WIDTH), lambda b, c: (b * nc + cidx(c), 0)),
                  full((4, 384)), full((1, 384)), full((4, 512)), full((1, 512)), full((SUBLANES, LANES)), full((1, 384))],
        out_specs=[rowblk(384), rowblk(384), rowblk(512), rowblk(128),
                   full((SUBLANES, 384)), full((SUBLANES, 512)), full((SUBLANES, LANES)), full((1, 384))],
        out_shape=[jax.ShapeDtypeStruct((t, 384), MXU_DTYPE), jax.ShapeDtypeStruct((t, 384), MXU_DTYPE),
                   jax.ShapeDtypeStruct((t, 512), MXU_DTYPE), jax.ShapeDtypeStruct((t, 128), MXU_DTYPE),
                   jax.ShapeDtypeStruct((SUBLANES, 384), F32), jax.ShapeDtypeStruct((SUBLANES, 512), F32),
                   jax.ShapeDtypeStruct((SUBLANES, LANES), F32), jax.ShapeDtypeStruct((1, 384), F32)],
        scratch_shapes=[pltpu.VMEM((SSD_WIDTH, SSD_STATE), F32), pltpu.VMEM((n + SUBLANES, 384), F32),
                        pltpu.VMEM((n + SUBLANES, 512), F32), pltpu.VMEM((n, SSD_WIDTH), F32),
                        pltpu.VMEM((n, 384), F32), pltpu.VMEM((n, 512), F32),
                        pltpu.VMEM((n + SUBLANES, 384), F32), pltpu.VMEM((n + SUBLANES, 512), F32),
                        pltpu.VMEM((SUBLANES, 384), F32), pltpu.VMEM((SUBLANES, 512), F32)],
        compiler_params=_params(2),
    )(h, h, h, h, h, h, hstate, dmix, cwx, cbx, cwb, cbb, vec, nw)


def _adamw_math(w, g, m, v):
    m = ADAM_B1 * m + (1.0 - ADAM_B1) * g
    v = ADAM_B2 * v + (1.0 - ADAM_B2) * (g * g)
    m_hat = m / (1.0 - ADAM_B1 ** ADAM_STEP)
    v_hat = v / (1.0 - ADAM_B2 ** ADAM_STEP)
    delta = -ADAM_LR * (m_hat / (jnp.sqrt(v_hat) + ADAM_EPS) + ADAM_WD * w)
    return delta, m, v


def _adamw(w, g, m, v, name, tr=256):
    rows, cols = w.shape
    tr = rows if rows <= tr else tr
    assert rows % tr == 0, (rows, tr)

    def body(w_ref, g_ref, m_ref, v_ref, d_ref, nm_ref, nv_ref):
        d, nm, nv = _adamw_math(w_ref[...], g_ref[...], m_ref[...], v_ref[...])
        d_ref[...] = d
        nm_ref[...] = nm
        nv_ref[...] = nv

    spec = pl.BlockSpec((tr, cols), lambda i: (i, 0))
    shp = jax.ShapeDtypeStruct((rows, cols), F32)
    return pl.pallas_call(body, name=name, grid=(rows // tr,), in_specs=[spec] * 4, out_specs=[spec] * 3,
                          out_shape=[shp] * 3, compiler_params=_params(1))(w, g, m, v)


def _sum8_layers(parts, name, tr):
    _, rows, cols = parts[0].shape
    assert rows % tr == 0
    nt = rows // tr

    def body(*refs):
        o_ref = refs[DEPTH]
        layer = pl.program_id(0)
        for l in range(DEPTH):
            @pl.when(layer == l)
            def _(l=l):
                acc = refs[l][0]
                for k in range(1, N_DEV):
                    acc = acc + refs[l][k]
                o_ref[...] = acc

    in_specs = [pl.BlockSpec((N_DEV, tr, cols), lambda a, i, l=l: (0, jnp.clip(i + (a - l) * nt, 0, nt - 1), 0))
                for l in range(DEPTH)]
    return pl.pallas_call(body, name=name, grid=(DEPTH, nt), in_specs=in_specs,
                          out_specs=pl.BlockSpec((None, tr, cols), lambda a, i: (a, i, 0)),
                          out_shape=jax.ShapeDtypeStruct((DEPTH, rows, cols), F32), compiler_params=_params(2))(*parts)


def _all_reduce_small(vec, name):
    rows, cols = vec.shape

    def body(x_ref, out_ref, gbuf, send_sems, recv_sems):
        x, y, c = lax.axis_index("x"), lax.axis_index("y"), lax.axis_index("c")
        me, sibling = (x, y, c), (x, y, 1 - c)
        chips = [(1 - x, y), (x, 1 - y), (1 - x, 1 - y)]

        def slot(px, py, pc):
            return gbuf.at[4 * px + 2 * py + pc]

        def copy(k, block, to, src=None):
            return pltpu.make_async_remote_copy(
                src_ref=slot(*block) if src is None else src, dst_ref=slot(*block),
                send_sem=send_sems.at[k], recv_sem=recv_sems.at[k], device_id=to, device_id_type=MESH_ID)

        first = [copy(0, me, sibling, src=x_ref)]
        first += [copy(1 + j, me, (*chip, c), src=x_ref) for j, chip in enumerate(chips)]
        for cp in first:
            cp.start()
        gbuf[4 * x + 2 * y + c] = x_ref[...]
        passed = [copy(4 + j, (*chip, c), sibling) for j, chip in enumerate(chips)]
        for j, chip in enumerate(chips):
            copy(1 + j, (*chip, c), me).wait_recv()
            passed[j].start()
        copy(0, sibling, me).wait_recv()
        for j, chip in enumerate(chips):
            copy(4 + j, (*chip, 1 - c), me).wait_recv()
        for cp in first + passed:
            cp.wait_send()
        acc = gbuf[0]
        for k in range(1, N_DEV):
            acc = acc + gbuf[k]
        out_ref[...] = acc

    return pl.pallas_call(
        body, name=name, out_shape=jax.ShapeDtypeStruct((rows, cols), F32),
        in_specs=[pl.BlockSpec(memory_space=pltpu.VMEM)], out_specs=pl.BlockSpec(memory_space=pltpu.VMEM),
        scratch_shapes=[pltpu.VMEM((N_DEV, rows, cols), F32), pltpu.SemaphoreType.DMA((7,)), pltpu.SemaphoreType.DMA((7,))],
        compiler_params=pltpu.CompilerParams(has_side_effects=True, vmem_limit_bytes=VMEM_LIMIT_BYTES),
    )(vec)


_COL_POOL, _COL_Z, _COL_XBC, _COL_DT, _COL_Q, _COL_K, _COL_V = 0, 256, 640, 1536, 1542, 1926, 2310
_H_SEGMENTS = ((_COL_XBC + SSD_WIDTH, 512), (_COL_POOL, 256), (_COL_Q, 384), (_COL_K, 384), (_COL_V, 384),
               (_COL_Z, 384), (_COL_XBC, 384), (_COL_DT, 6))


def _h_from_orig(w):
    parts = [w[..., o:o + n] for o, n in _H_SEGMENTS]
    pad = jnp.zeros(w.shape[:-1] + (H_COLS - IN_COLS,), w.dtype)
    return jnp.concatenate(parts + [pad], axis=-1)


def _h_to_orig(w):
    offs, o = {}, 0
    for orig, n in _H_SEGMENTS:
        offs[orig] = (o, n)
        o += n
    order = sorted(offs)
    return jnp.concatenate([w[..., offs[k][0]:offs[k][0] + offs[k][1]] for k in order], axis=-1)


def _interleave(w):
    lead = w.shape[:-1]
    nt = D_FF // GLU_TILE
    return jnp.swapaxes(w.reshape(lead + (2, nt, GLU_TILE)), -3, -2).reshape(lead + (2 * D_FF,))


def _deinterleave(w):
    lead = w.shape[:-1]
    nt = D_FF // GLU_TILE
    return jnp.swapaxes(w.reshape(lead + (nt, 2, GLU_TILE)), -3, -2).reshape(lead + (2 * D_FF,))


def _up_segments():
    segs = []
    for j in range(N_DEV):
        half, base = j // 4, UP_SHARD * (j % 4)
        c = base
        while c < base + UP_SHARD:
            t, r = divmod(c, GLU_TILE)
            n = min(GLU_TILE - r, base + UP_SHARD - c)
            segs.append((j, c - base, 2 * GLU_TILE * t + GLU_TILE * half + r, n))
            c += n
    return segs


def _up_to_interleaved(w, name, tr=256):
    def body(i_ref, o_ref):
        for j, src, dst, n in _up_segments():
            o_ref[:, dst:dst + n] = i_ref[j, :, src:src + n]

    return pl.pallas_call(
        body, name=name, grid=(D_MODEL // tr,), in_specs=[pl.BlockSpec((N_DEV, tr, UP_SHARD), lambda r: (0, r, 0))],
        out_specs=pl.BlockSpec((tr, 2 * D_FF), lambda r: (r, 0)),
        out_shape=jax.ShapeDtypeStruct((D_MODEL, 2 * D_FF), w.dtype), compiler_params=_params(1))(w)


def _up_from_interleaved(g, name, tr=128):
    def body(i_ref, o_ref):
        for j, src, dst, n in _up_segments():
            o_ref[j, :, src:src + n] = i_ref[:, dst:dst + n]

    return pl.pallas_call(
        body, name=name, grid=(D_MODEL // tr,), in_specs=[pl.BlockSpec((tr, 2 * D_FF), lambda r: (r, 0))],
        out_specs=pl.BlockSpec((N_DEV, tr, UP_SHARD), lambda r: (0, r, 0)),
        out_shape=jax.ShapeDtypeStruct((N_DEV, D_MODEL, UP_SHARD), g.dtype), compiler_params=_params(1))(g)


def _mix_rows_from_orig(w):
    return jnp.concatenate([w[256:640], w[640:1024], w[0:256]], axis=0)


def _mix_rows_to_orig(w):
    return jnp.concatenate([w[768:1024], w[0:384], w[384:768]], axis=0)


def _xbc_split(w):
    return w[..., :SSD_WIDTH], w[..., SSD_WIDTH:]


def _layer_fwd(x, p_l, wt, sp, nb, s, comm=None):
    h = _mm(x, wt["w_in"], "nn", F32, "mm_in", tm=1024, tn=1408)
    pool_out = _pool_fwd(h, wt["pool_bd"], sp["pool_scale"], nb, s, "pool_fwd")
    ssd_out, hstate = _ssd_fwd(h, sp["cwx"], sp["cbx"], sp["cwb"], sp["cbb"], sp["ssd_vec"], sp["ssd_norm_w"], nb, s, "ssd_fwd")
    sb_out, comm_out = _sb_fwd(h, nb, s, "sb_fwd" if comm is None else "sb_fwd_gather", comm)
    mixcat = jnp.concatenate([ssd_out, sb_out, pool_out], axis=1)
    mix = _mm(mixcat, wt["w_out"], "nn", F32, "mm_out", tm=1024, tn=1024)
    x1, r1 = _ln_fwd(x, mix, sp["ln1"], "ln1_fwd")
    up = _mm(x1, wt["w_up"], "nn", F32, "mm_up", tm=1024, tn=1408)
    act = _glu_fwd(up, sp["ffn_cw"], sp["ffn_cb"], nb, s, "glu_fwd")
    ffn = _mm(act, wt["w_down"], "nn", F32, "mm_down", tm=1024, tn=1024, tk=1408)
    gp = _mm(x1, wt["w_gate"], "nn", F32, "mm_gate", tm=1024, tn=1024)
    pp = _mm(p_l, wt["w_proj"], "nn", F32, "mm_proj", tm=2048, tn=1024)
    x2, r2 = _ln_fwd(x1, ffn, sp["ln2"], "ln2_fwd", gp=gp, pp=pp)
    return x2, dict(x=x, h=h, hstate=hstate, mixcat=mixcat, r1=r1, x1=x1, up=up, act=act, gp=gp, pp=pp, r2=r2), comm_out


def _layer_bwd(dx2, p_l, sv, wt, sp, nb, s, comm=None, tail_comm=None):
    dr2, dgp, dpp, st2 = _ln_bwd(sv["r2"], sp["ln2"], dx2, "ln2_bwd", gp=sv["gp"], pp=sv["pp"])
    g_down = _mm(sv["act"], dr2, "tn", F32, "wg_down", tm=1408, tn=1024, tk=512)
    dact = _mm(dr2, wt["w_down"], "nt", F32, "dg_down", tm=1024, tn=1408)
    dup, ffn_acc = _glu_bwd(sv["up"], dact, sp["ffn_cw"], sp["ffn_cb"], nb, s, "glu_bwd")
    g_up = _mm(sv["x1"], dup, "tn", F32, "wg_up", tm=1024, tn=2816, tk=512)
    g_gate = _mm(sv["x1"], dgp, "tn", F32, "wg_gate", tm=1024, tn=1024, tk=512)
    g_proj = _mm(p_l, dpp, "tn", F32, "wg_proj", tm=256, tn=1024, tk=512)
    t1 = _mm(dgp, wt["w_gate"], "nt", F32, "dg_gate", tm=1024, tn=1024, add=dr2, add_coef=ALPHA)
    dx1 = _mm(dup, wt["w_up"], "nt", F32, "dg_up", tm=1024, tn=1024, tk=1408, add=t1)
    dr1, st1 = _ln_bwd(sv["r1"], sp["ln1"], dx1, "ln1_bwd")
    g_out = _mm(sv["mixcat"], dr1, "tn", F32, "wg_out", tm=1024, tn=1024, tk=512)
    dmix = _mm(dr1, wt["w_out"], "nt", F32, "dg_out", tm=1024, tn=1024)
    du, g_pool_bd, g_pool_scale = _pool_bwd(sv["h"], dmix, wt["pool_bd"], sp["pool_scale"], nb, s, "pool_bwd")
    dz, dxs, dbc, ddt, gx, gb, gv, gn = _ssd_bwd(sv["h"], sv["hstate"], dmix, sp["cwx"], sp["cbx"], sp["cwb"], sp["cbb"],
                                                  sp["ssd_vec"], sp["ssd_norm_w"], nb, s, "ssd_bwd")
    ready = dict(w_out=g_out, ffn_w_up=g_up, ffn_w_down=g_down, ple_w_gate=g_gate, ple_w_proj=g_proj)
    job = comm(ready) if comm is not None else None
    dq, dk, dv, comm_out = _sb_bwd(sv["h"], dmix, nb, s, "sb_bwd" if job is None else "sb_bwd_x%d" % job["n_xfers"], job)
    dh = jnp.concatenate([dbc, du, dq, dk, dv, dz, dxs, ddt], axis=1)
    g_in = _mm(sv["x"], dh, "tn", F32, "wg_in", tm=1024, tn=2816, tk=512)
    tail_job = tail_comm(g_in) if tail_comm is not None else None
    dx = _mm(dh, wt["w_in"], "nt", F32, "dg_in" if tail_job is None else "dg_in_x", tm=1024, tn=1024, tk=1408, add=dr1,
             add_coef=ALPHA, comm=tail_job)
    dx, tail_out = dx if tail_job is not None else (dx, None)
    small = dict(
        pool_w=jnp.stack([g_pool_bd[HEAD_DIM * g:HEAD_DIM * (g + 1), HEAD_DIM * g:HEAD_DIM * (g + 1)] for g in range(4)]),
        pool_scale=g_pool_scale[0],
        ssd_conv_w=jnp.concatenate([gx[0:4], gb[0:4]], axis=1),
        ssd_conv_b=jnp.concatenate([gx[4], gb[4]], axis=0),
        ssd_dt_bias=gv[0, :SSD_HEADS],
        ssd_a_log=gv[1, :SSD_HEADS] * sp["ssd_vec"][1, :SSD_HEADS],
        ssd_d=gv[2, :SSD_HEADS],
        ssd_norm_w=gn[0],
        ln1_g=st1[0], ln1_b=st1[1], ln2_g=st2[0], ln2_b=st2[1],
        ffn_conv_w=_deinterleave(ffn_acc[0:3]),
        ffn_conv_b=_deinterleave(ffn_acc[3]),
    )
    return dx, dict(ready, w_in=g_in), small, comm_out, tail_out


def _layer_params(i, big, rep):
    pool_bd = jnp.zeros((POOL_WIDTH, POOL_WIDTH), F32)
    for g in range(4):
        pool_bd = lax.dynamic_update_slice(pool_bd, rep["pool_w"][i, g], (HEAD_DIM * g, HEAD_DIM * g))
    wt = dict(w_in=big["w_in"], w_out=big["w_out"], w_up=big["ffn_w_up"], w_down=big["ffn_w_down"],
              w_gate=big["ple_w_gate"], w_proj=big["ple_w_proj"], pool_bd=pool_bd.astype(MXU_DTYPE))
    cwx, cwb = _xbc_split(rep["ssd_conv_w"][i])
    cbx, cbb = _xbc_split(rep["ssd_conv_b"][i][None, :])
    vec = jnp.zeros((SUBLANES, LANES), F32)
    vec = vec.at[0, :SSD_HEADS].set(rep["ssd_dt_bias"][i])
    vec = vec.at[1, :SSD_HEADS].set(-jnp.exp(rep["ssd_a_log"][i]))
    vec = vec.at[2, :SSD_HEADS].set(rep["ssd_d"][i])
    sp = dict(pool_scale=rep["pool_scale"][i][None, :], cwx=cwx, cbx=cbx, cwb=cwb, cbb=cbb, ssd_vec=vec,
              ssd_norm_w=rep["ssd_norm_w"][i][None, :],
              ln1=jnp.stack([rep["ln1_g"][i], rep["ln1_b"][i]]), ln2=jnp.stack([rep["ln2_g"][i], rep["ln2_b"][i]]),
              ffn_cw=_interleave(rep["ffn_conv_w"][i]), ffn_cb=_interleave(rep["ffn_conv_b"][i][None, :]))
    return wt, sp


def _run_layers(x, p, target, big_w, rep, fwd_job=None, fwd_done=None, bwd_job=None, bwd_done=None, tail_job=None,
                tail_done=None):
    nb, s, d = x.shape
    t = nb * s
    xf = x.reshape(t, d)
    saved, params = [], []
    for i in range(DEPTH):
        wt, sp = _layer_params(i, big_w[i], rep)
        params.append((wt, sp))
        job = fwd_job(i) if fwd_job is not None else None
        xf, sv, res = _layer_fwd(xf, p[i].reshape(t, PLE_DIM), wt, sp, nb, s, job)
        if job is not None:
            fwd_done(i, res)
        saved.append(sv)
    dy, loss = _loss_grad(xf, target.reshape(t, d), "loss")
    bigs, smalls = [None] * DEPTH, [None] * DEPTH
    for i in reversed(range(DEPTH)):
        wt, sp = params[i]
        job = (lambda ready, i=i: bwd_job(i, bigs, ready)) if bwd_job is not None else None
        dy, bigs[i], smalls[i], res, tail = _layer_bwd(dy, p[i].reshape(t, PLE_DIM), saved[i], wt, sp, nb, s, job,
                                                         tail_job if i == 0 else None)
        if job is not None:
            bwd_done(i, res)
        if tail is not None:
            tail_done(tail)
    return loss, dy.reshape(nb, s, d), bigs, smalls


def _local_step(x, p, target, full, rep):
    return _run_layers(x, p, target, [{n: full[n][i] for n in full} for i in range(DEPTH)], rep)


BIG = ("w_in", "w_out", "ffn_w_up", "ffn_w_down", "ple_w_gate", "ple_w_proj")
SMALL_REPLICATED = ("pool_w", "pool_scale", "ssd_conv_b", "ssd_dt_bias", "ssd_a_log", "ssd_d", "ssd_norm_w",
                    "ln1_g", "ln1_b", "ffn_conv_b", "ln2_g", "ln2_b")
SMALL_SHARDED = ("ssd_conv_w", "ffn_conv_w")
WEIGHTS = ("w_in", "pool_w", "pool_scale", "ssd_conv_w", "ssd_conv_b", "ssd_dt_bias", "ssd_a_log", "ssd_d", "ssd_norm_w",
           "w_out", "ln1_g", "ln1_b", "ffn_w_up", "ffn_conv_w", "ffn_conv_b", "ffn_w_down", "ln2_g", "ln2_b",
           "ple_w_gate", "ple_w_proj")
SUM_BLOCK_BYTES = 3 * 1024 * 1024


def _to_rows(a, cols):
    f = a.reshape(-1)
    pad = (-f.shape[0]) % cols
    if pad:
        f = jnp.concatenate([f, jnp.zeros((pad,), f.dtype)])
    return f.reshape(-1, cols)


def _pack_rows(arrs, cols, row_mult):
    rows = [_to_rows(a, cols) for a in arrs]
    flat = jnp.concatenate(rows, axis=0)
    pad = (-flat.shape[0]) % row_mult
    if pad:
        flat = jnp.concatenate([flat, jnp.zeros((pad, cols), flat.dtype)], axis=0)
    return flat


def _unpack_rows(flat, shapes, cols):
    out, r = [], 0
    for shp in shapes:
        n = 1
        for v in shp:
            n *= v
        nr = -(-n // cols)
        out.append(flat[r:r + nr].reshape(-1)[:n].reshape(shp))
        r += nr
    return out


def kernel(x, p, w_in, pool_w, pool_scale, ssd_conv_w, ssd_conv_b, ssd_dt_bias, ssd_a_log, ssd_d, ssd_norm_w, w_out, ln1_g, ln1_b, ffn_w_up, ffn_conv_w, ffn_conv_b, ffn_w_down, ln2_g, ln2_b, ple_w_gate, ple_w_proj, loss_target, m_w_in, m_pool_w, m_pool_scale, m_ssd_conv_w, m_ssd_conv_b, m_ssd_dt_bias, m_ssd_a_log, m_ssd_d, m_ssd_norm_w, m_w_out, m_ln1_g, m_ln1_b, m_ffn_w_up, m_ffn_conv_w, m_ffn_conv_b, m_ffn_w_down, m_ln2_g, m_ln2_b, m_ple_w_gate, m_ple_w_proj, v_w_in, v_pool_w, v_pool_scale, v_ssd_conv_w, v_ssd_conv_b, v_ssd_dt_bias, v_ssd_a_log, v_ssd_d, v_ssd_norm_w, v_w_out, v_ln1_g, v_ln1_b, v_ffn_w_up, v_ffn_conv_w, v_ffn_conv_b, v_ffn_w_down, v_ln2_g, v_ln2_b, v_ple_w_gate, v_ple_w_proj):
    wts = dict(w_in=w_in, pool_w=pool_w, pool_scale=pool_scale, ssd_conv_w=ssd_conv_w, ssd_conv_b=ssd_conv_b,
               ssd_dt_bias=ssd_dt_bias, ssd_a_log=ssd_a_log, ssd_d=ssd_d, ssd_norm_w=ssd_norm_w, w_out=w_out, ln1_g=ln1_g,
               ln1_b=ln1_b, ffn_w_up=ffn_w_up, ffn_conv_w=ffn_conv_w, ffn_conv_b=ffn_conv_b, ffn_w_down=ffn_w_down,
               ln2_g=ln2_g, ln2_b=ln2_b, ple_w_gate=ple_w_gate, ple_w_proj=ple_w_proj)
    mom_m = dict(w_in=m_w_in, pool_w=m_pool_w, pool_scale=m_pool_scale, ssd_conv_w=m_ssd_conv_w, ssd_conv_b=m_ssd_conv_b,
                 ssd_dt_bias=m_ssd_dt_bias, ssd_a_log=m_ssd_a_log, ssd_d=m_ssd_d, ssd_norm_w=m_ssd_norm_w, w_out=m_w_out,
                 ln1_g=m_ln1_g, ln1_b=m_ln1_b, ffn_w_up=m_ffn_w_up, ffn_conv_w=m_ffn_conv_w, ffn_conv_b=m_ffn_conv_b,
                 ffn_w_down=m_ffn_w_down, ln2_g=m_ln2_g, ln2_b=m_ln2_b, ple_w_gate=m_ple_w_gate, ple_w_proj=m_ple_w_proj)
    mom_v = dict(w_in=v_w_in, pool_w=v_pool_w, pool_scale=v_pool_scale, ssd_conv_w=v_ssd_conv_w, ssd_conv_b=v_ssd_conv_b,
                 ssd_dt_bias=v_ssd_dt_bias, ssd_a_log=v_ssd_a_log, ssd_d=v_ssd_d, ssd_norm_w=v_ssd_norm_w, w_out=v_w_out,
                 ln1_g=v_ln1_g, ln1_b=v_ln1_b, ffn_w_up=v_ffn_w_up, ffn_conv_w=v_ffn_conv_w, ffn_conv_b=v_ffn_conv_b,
                 ffn_w_down=v_ffn_w_down, ln2_g=v_ln2_g, ln2_b=v_ln2_b, ple_w_gate=v_ple_w_gate, ple_w_proj=v_ple_w_proj)
    me = 4 * lax.axis_index("x") + 2 * lax.axis_index("y") + lax.axis_index("c")

    def layer_shards(i):
        sh = {n: wts[n][i].astype(MXU_DTYPE) for n in BIG}
        sh["w_in"] = _h_from_orig(wts["w_in"][i]).astype(MXU_DTYPE)
        return sh

    def gathered_weights(res):
        big = dict(zip(BIG, res[:len(BIG)]))
        big["ffn_w_up"] = _up_to_interleaved(big["ffn_w_up"], "up_to_interleaved")
        return big

    res0 = _gather_call_two_level(_gather_job(layer_shards(0), [wts[n] for n in SMALL_SHARDED]), "gather_layer0")
    big_w = [gathered_weights(res0)] + [None] * (DEPTH - 1)
    rep = {n: wts[n] for n in SMALL_REPLICATED}
    for n, g in zip(SMALL_SHARDED, res0[len(BIG):]):
        rep[n] = jnp.transpose(g, (1, 2, 0, 3)).reshape(g.shape[1], g.shape[2], N_DEV * g.shape[3])

    def fwd_job(i):
        return _gather_job(layer_shards(i + 1)) if i + 1 < DEPTH else None

    def fwd_done(i, res):
        big_w[i + 1] = gathered_weights(res)

    received = [dict() for _ in range(DEPTH)]
    carried = ("w_out", "ffn_w_up", "ffn_w_down", "ple_w_gate", "ple_w_proj")

    def bwd_items(i, bigs, ready):
        items = [(i, n, ready[n]) for n in carried] + ([(i + 1, "w_in", bigs[i + 1]["w_in"])] if i + 1 < DEPTH else [])
        return [(l, n, _up_from_interleaved(g, "up_from_interleaved") if n == "ffn_w_up" else g) for l, n, g in items]

    pending = {}

    def bwd_job(i, bigs, ready):
        pending[i] = bwd_items(i, bigs, ready)
        return _exchange_job([(n, g) for _, n, g in pending[i]])

    def bwd_done(i, res):
        for (l, n, _), r in zip(pending[i], res):
            received[l][n] = r

    def tail_done(res):
        received[0]["w_in"] = res[0]

    loss_loc, grad_x, bigs, smalls = _run_layers(x, p, loss_target, big_w, rep, fwd_job, fwd_done, bwd_job, bwd_done,
                                                 lambda g_in: _exchange_job([("w_in", g_in)]), tail_done)

    grads = {}
    for n in BIG:
        parts = [received[i][n] for i in range(DEPTH)]
        _, rows, cols = parts[0].shape
        tr = next(t for t in (256, 128, 64, 32, 16, 8) if rows % t == 0 and N_DEV * t * cols * 4 <= SUM_BLOCK_BYTES)
        g = _sum8_layers(parts, "sum_" + n, tr)
        grads[n] = _h_to_orig(g) if n == "w_in" else g
    small_names = SMALL_REPLICATED + SMALL_SHARDED
    small_full_shapes = [rep[n].shape for n in small_names]
    small_vec = _pack_rows([jnp.stack([smalls[i][n] for i in range(DEPTH)]) for n in small_names] + [loss_loc[0, :1]],
                           LANES, SUBLANES)
    small_sum = _all_reduce_small(small_vec, "allreduce_small")
    small_out = _unpack_rows(small_sum, small_full_shapes + [(1,)], LANES)
    loss = small_out[-1][0]
    for n, g in zip(small_names, small_out[:-1]):
        if n in SMALL_SHARDED:
            width = wts[n].shape[-1]
            g = lax.dynamic_slice_in_dim(g, me * width, width, axis=g.ndim - 1)
        grads[n] = g

    delta, new_m, new_v = {}, {}, {}
    for n in BIG:
        shp = wts[n].shape
        two_d = lambda a: a.reshape(-1, shp[-1])
        tr = {"w_in": 128, "ffn_w_down": DOWN_SHARD}.get(n, 256)
        d_, m_, v_ = _adamw(two_d(wts[n]), two_d(grads[n]), two_d(mom_m[n]), two_d(mom_v[n]), "adamw_" + n, tr=tr)
        delta[n], new_m[n], new_v[n] = d_.reshape(shp), m_.reshape(shp), v_.reshape(shp)
    packs = [_pack_rows([src[n] for n in small_names], LANES, SUBLANES) for src in (wts, grads, mom_m, mom_v)]
    outs = _adamw(*packs, "adamw_small", tr=packs[0].shape[0])
    shapes = [wts[n].shape for n in small_names]
    for dst, flat in zip((delta, new_m, new_v), outs):
        for n, a in zip(small_names, _unpack_rows(flat, shapes, LANES)):
            dst[n] = a
    return (loss, grad_x, *[grads[n] for n in WEIGHTS], *[delta[n] for n in WEIGHTS],
            *[new_m[n] for n in WEIGHTS], *[new_v[n] for n in WEIGHTS])
```

```python
import functools

import jax
import jax.numpy as jnp
from jax import lax
from jax.experimental import pallas as pl
from jax.experimental.pallas import tpu as pltpu

F32 = jnp.float32
BF16 = jnp.bfloat16
MXU_DTYPE = jnp.bfloat16

D_MODEL = 1024
DEPTH = 4
PLE_DIM = 256
ALPHA = (2 * DEPTH) ** 0.25
LN_EPS = 1e-5
RMS_EPS = 1e-6
HEAD_DIM = 64
POOL_WIDTH = 256
POOL_WINDOWS = (2, 4, 8, 16)
SSD_WIDTH = 384
SSD_HEADS = 6
SSD_STATE = 128
SSD_XBC = 896
SB_WIDTH = 384
IN_COLS = 2694
D_FF = 2816
N_DEV = 8
UP_SHARD = 2 * D_FF // N_DEV
DOWN_SHARD = D_FF // N_DEV

ADAM_LR = 0.001
ADAM_B1 = 0.9
ADAM_B2 = 0.999
ADAM_EPS = 1e-08
ADAM_WD = 0.01
ADAM_STEP = 10

LANES = 128
SUBLANES = 8
VMEM_LIMIT_BYTES = 56 * 1024 * 1024

H_COLS = 2816
H_BC = 0
H_POOL = 512
H_Q = 768
H_K = 1152
H_V = 1536
H_Z = 1920
H_XS = 2304
H_DT = 2688
SSD_CHUNK = 512
QB = 256
GLU_TILE = 256
MASKED_LOG = -1e30

NN = ((1,), (0,))
NT = ((1,), (1,))
TN = ((0,), (0,))


def _dot(a, b, dims=NN):
    return lax.dot_general(a.astype(MXU_DTYPE), b.astype(MXU_DTYPE), (dims, ((), ())), preferred_element_type=F32)


def _dot_exact01(x, m01, dims=NN, x_left=True, terms=3):
    acc = None
    r = x
    for _ in range(terms):
        hi = r.astype(BF16)
        ops = (hi, m01) if x_left else (m01, hi)
        part = lax.dot_general(ops[0], ops[1], (dims, ((), ())), preferred_element_type=F32)
        acc = part if acc is None else acc + part
        r = r - hi.astype(F32)
    return acc


def _sigmoid(v):
    return 1.0 / (1.0 + jnp.exp(-v))


def _silu(v):
    return v * _sigmoid(v)


def _dsilu(v):
    s = _sigmoid(v)
    return s * (1.0 + v * (1.0 - s))


def _softplus(v):
    return jnp.maximum(v, 0.0) + jnp.log(1.0 + jnp.exp(-jnp.abs(v)))


def _params(n_axes, side_effects=False):
    return pltpu.CompilerParams(dimension_semantics=("arbitrary",) * n_axes, vmem_limit_bytes=VMEM_LIMIT_BYTES,
                                has_side_effects=side_effects)


MESH_ID = pl.DeviceIdType.MESH
_ANY = pl.BlockSpec(memory_space=pl.ANY)


def _flip(v, bit):
    return 1 - v if bit else v


def _comm_counts(comm):
    return (0, 0) if comm is None else (len(comm["inputs"]), len(comm["out_shapes"]))


def _comm_call_args(comm):
    if comm is None:
        return [], [], [], []
    n = comm["n_xfers"]
    sems = [pltpu.SemaphoreType.DMA(((N_DEV - 1) * n,)), pltpu.SemaphoreType.DMA(((N_DEV - 1) * n,)),
            pltpu.SemaphoreType.DMA((n,))]
    return list(comm["inputs"]), [_ANY] * len(comm["out_shapes"]), list(comm["out_shapes"]), sems


def _comm_descs(comm, in_refs, tail_refs, with_recvs=True):
    n_out = len(comm["out_shapes"])
    out_refs, (send_sems, recv_sems, local_sems) = tail_refs[:n_out], tail_refs[n_out:n_out + 3]
    xfers = comm["xfers"](in_refs, out_refs)
    n = len(xfers)
    assert n == comm["n_xfers"]
    x, y, c = lax.axis_index("x"), lax.axis_index("y"), lax.axis_index("c")
    me = 4 * x + 2 * y + c
    local = [pltpu.make_async_copy(src_for(me), dst_for(me), local_sems.at[t]) for t, (src_for, dst_for) in enumerate(xfers)]
    sends, recvs = [], []
    for k in range(1, N_DEV):
        pid = (_flip(x, k & 4), _flip(y, k & 2), _flip(c, k & 1))
        peer = 4 * pid[0] + 2 * pid[1] + pid[2]
        for t, (src_for, dst_for) in enumerate(xfers):
            idx = (k - 1) * n + t
            sends.append(pltpu.make_async_remote_copy(
                src_ref=src_for(peer), dst_ref=dst_for(me), send_sem=send_sems.at[idx], recv_sem=recv_sems.at[idx],
                device_id=pid, device_id_type=MESH_ID))
            if with_recvs:
                recvs.append(pltpu.make_async_remote_copy(
                    src_ref=src_for(peer), dst_ref=dst_for(peer), send_sem=send_sems.at[idx], recv_sem=recv_sems.at[idx],
                    device_id=pid, device_id_type=MESH_ID))
    return local, sends, recvs


def _comm_start(descs):
    local, sends, _ = descs
    for cp in local + sends:
        cp.start()


def _comm_wait(descs):
    local, sends, recvs = descs
    for cp in recvs:
        cp.wait_recv()
    for cp in sends:
        cp.wait_send()
    for cp in local:
        cp.wait()


def _comm_hosted(comm, in_refs, tail_refs, grid):
    if comm is None:
        return
    ids = [pl.program_id(a) for a in range(len(grid))]
    first = functools.reduce(jnp.logical_and, [i == 0 for i in ids])
    last = functools.reduce(jnp.logical_and, [i == g - 1 for i, g in zip(ids, grid)])

    @pl.when(first)
    def _():
        _comm_start(_comm_descs(comm, in_refs, tail_refs, with_recvs=False))

    @pl.when(last)
    def _():
        _comm_wait(_comm_descs(comm, in_refs, tail_refs))


def _gather_call_two_level(comm, name):
    n_in, n_out = len(comm["inputs"]), len(comm["out_shapes"])

    def body(*refs):
        in_refs, out_refs = refs[:n_in], refs[n_in:n_in + n_out]
        send_sems, recv_sems, local_sems = refs[n_in + n_out:]
        xfers = comm["xfers"](in_refs, out_refs)
        x, y, c = lax.axis_index("x"), lax.axis_index("y"), lax.axis_index("c")
        pos = lambda px, py, pc: 4 * px + 2 * py + pc
        me, sibling = (x, y, c), (x, y, 1 - c)
        chips = [(1 - x, y), (x, 1 - y), (1 - x, 1 - y)]

        def copy(t, k, block, to, own):
            src_for, dst_for = xfers[t]
            return pltpu.make_async_remote_copy(
                src_ref=src_for(pos(*me)) if own else dst_for(pos(*block)), dst_ref=dst_for(pos(*block)),
                send_sem=send_sems.at[7 * t + k], recv_sem=recv_sems.at[7 * t + k], device_id=to, device_id_type=MESH_ID)

        nt = len(xfers)
        local = [pltpu.make_async_copy(xfers[t][0](pos(*me)), xfers[t][1](pos(*me)), local_sems.at[t]) for t in range(nt)]
        first = [copy(t, 0, me, sibling, True) for t in range(nt)]
        first += [copy(t, 1 + j, me, (*chip, c), True) for t in range(nt) for j, chip in enumerate(chips)]
        for cp in local + first:
            cp.start()
        passed = []
        for j, chip in enumerate(chips):
            for t in range(nt):
                copy(t, 1 + j, (*chip, c), me, False).wait_recv()
                fwd = copy(t, 4 + j, (*chip, c), sibling, False)
                fwd.start()
                passed.append(fwd)
        for t in range(nt):
            copy(t, 0, sibling, me, False).wait_recv()
            for j, chip in enumerate(chips):
                copy(t, 4 + j, (*chip, 1 - c), me, False).wait_recv()
        for cp in first + passed:
            cp.wait_send()
        for cp in local:
            cp.wait()

    n = comm["n_xfers"]
    return pl.pallas_call(
        body, name=name, in_specs=[_ANY] * n_in, out_specs=[_ANY] * n_out, out_shape=list(comm["out_shapes"]),
        scratch_shapes=[pltpu.SemaphoreType.DMA((7 * n,)), pltpu.SemaphoreType.DMA((7 * n,)), pltpu.SemaphoreType.DMA((n,))],
        compiler_params=pltpu.CompilerParams(has_side_effects=True))(*comm["inputs"])


def _rows(ref, j, n):
    return ref.at[pl.ds(pl.multiple_of(j * n, SUBLANES), n), :]


def _gather_job(sh, conv=None):
    conv = list(conv or [])
    sds = jax.ShapeDtypeStruct
    out_shapes = [sds((D_MODEL, H_COLS), MXU_DTYPE), sds((D_MODEL, D_MODEL), MXU_DTYPE), sds((N_DEV, D_MODEL, UP_SHARD), MXU_DTYPE),
                  sds((D_FF, D_MODEL), MXU_DTYPE), sds((D_MODEL, D_MODEL), MXU_DTYPE), sds((PLE_DIM, D_MODEL), MXU_DTYPE)]
    out_shapes += [sds((N_DEV,) + a.shape, a.dtype) for a in conv]

    def xfers(ins, outs):
        whole = lambda a: (lambda j: a)
        r = [(whole(ins[0]), lambda j: _rows(outs[0], j, 128)),
             (whole(ins[1]), lambda j: _rows(outs[1], lax.rem(j + 6, N_DEV), 128)),
             (whole(ins[2]), lambda j: outs[2].at[j]),
             (whole(ins[3]), lambda j: _rows(outs[3], j, DOWN_SHARD)),
             (whole(ins[4]), lambda j: _rows(outs[4], j, 128)),
             (whole(ins[5]), lambda j: outs[5].at[:, pl.ds(pl.multiple_of(j * LANES, LANES), LANES)])]
        for t in range(len(conv)):
            r.append((whole(ins[6 + t]), lambda j, o=outs[6 + t]: o.at[j]))
        return r

    return dict(inputs=[sh[n] for n in BIG] + conv, out_shapes=out_shapes, xfers=xfers, n_xfers=6 + len(conv))


_SHARD_SHAPES = {"w_in": (128, H_COLS), "w_out": (128, D_MODEL), "ffn_w_up": (D_MODEL, UP_SHARD), "ffn_w_down": (DOWN_SHARD, D_MODEL),
                 "ple_w_gate": (128, D_MODEL), "ple_w_proj": (PLE_DIM, LANES)}


def _exchange_job(items):
    def source(name, ref):
        if name in ("w_in", "ple_w_gate"):
            return lambda j: _rows(ref, j, 128)
        if name == "w_out":
            return lambda j: _rows(ref, lax.rem(j + 6, N_DEV), 128)
        if name == "ffn_w_up":
            return lambda j: ref.at[j]
        if name == "ffn_w_down":
            return lambda j: _rows(ref, j, DOWN_SHARD)
        assert name == "ple_w_proj"
        return lambda j: ref.at[:, pl.ds(pl.multiple_of(j * LANES, LANES), LANES)]

    def xfers(ins, outs):
        return [(source(name, i), lambda j, o=o: o.at[j]) for (name, _), i, o in zip(items, ins, outs)]

    return dict(inputs=[g for _, g in items], xfers=xfers, n_xfers=len(items),
                out_shapes=[jax.ShapeDtypeStruct((N_DEV,) + _SHARD_SHAPES[name], F32) for name, _ in items])


def _pick(n, pref):
    if n <= pref:
        return n
    for t in range(pref - pref % LANES, 0, -LANES):
        if n % t == 0:
            return t
    raise ValueError((n, pref))


def _mm(a, b, mode, out_dtype, name, tm=512, tn=512, tk=1024, add=None, add_coef=1.0, comm=None):
    n_in, n_out = _comm_counts(comm)
    if mode == "nn":
        (m, k), (k2, n) = a.shape, b.shape
    elif mode == "nt":
        (m, k), (n, k2) = a.shape, b.shape
    else:
        (k, m), (k2, n) = a.shape, b.shape
    assert k == k2, (a.shape, b.shape, mode)
    tm, tn, tk = _pick(m, tm), _pick(n, tn), _pick(k, tk)
    nk = k // tk
    dims = {"nn": NN, "nt": NT, "tn": TN}[mode]

    def body(*refs):
        a_ref, b_ref = refs[:2]
        n_add = int(add is not None)
        add_ref = refs[2] if n_add else None
        o_ref = refs[2 + n_add + n_in]
        tail = refs[3 + n_add + n_in:]
        if comm is not None:
            _comm_hosted(comm, refs[2 + n_add:2 + n_add + n_in], tail[:n_out] + tail[n_out + int(nk > 1):],
                         (m // tm, n // tn, nk))

        def finish(r):
            if add_ref is not None:
                r = r + add_coef * add_ref[...]
            o_ref[...] = r.astype(out_dtype)

        if nk == 1:
            finish(_dot(a_ref[...], b_ref[...], dims))
            return
        acc_ref = tail[n_out]
        kk = pl.program_id(2)

        @pl.when(kk == 0)
        def _():
            acc_ref[...] = jnp.zeros_like(acc_ref)

        acc_ref[...] += _dot(a_ref[...], b_ref[...], dims)

        @pl.when(kk == nk - 1)
        def _():
            finish(acc_ref[...])

    if mode == "tn":
        a_spec = pl.BlockSpec((tk, tm), lambda i, j, kk: (kk, i))
    else:
        a_spec = pl.BlockSpec((tm, tk), lambda i, j, kk: (i, kk))
    if mode == "nt":
        b_spec = pl.BlockSpec((tn, tk), lambda i, j, kk: (j, kk))
    else:
        b_spec = pl.BlockSpec((tk, tn), lambda i, j, kk: (kk, j))
    o_spec = pl.BlockSpec((tm, tn), lambda i, j, kk: (i, j))
    in_specs = [a_spec, b_spec] + ([o_spec] if add is not None else [])
    args = (a, b) + ((add,) if add is not None else ())
    c_in, c_specs, c_shapes, c_scratch = _comm_call_args(comm)
    res = pl.pallas_call(
        body, name=name, grid=(m // tm, n // tn, nk), in_specs=in_specs + [_ANY] * n_in, out_specs=[o_spec] + c_specs,
        out_shape=[jax.ShapeDtypeStruct((m, n), out_dtype)] + c_shapes,
        scratch_shapes=([pltpu.VMEM((tm, tn), F32)] if nk > 1 else []) + c_scratch,
        compiler_params=_params(3, comm is not None),
    )(*args, *c_in)
    return res[0] if comm is None else (res[0], res[1:])


def _mm_ln(a, b, x, gb, name, gp=None, pp=None, tm=512, tk=1024):
    (m, k), (k2, d) = a.shape, b.shape
    assert k == k2 and x.shape == (m, d)
    tm, tk = _pick(m, tm), _pick(k, tk)
    nk = k // tk
    with_ple = gp is not None

    def body(*refs):
        a_ref, b_ref, x_ref = refs[:3]
        gp_ref, pp_ref = refs[3:5] if with_ple else (None, None)
        gb_ref, y_ref, r_ref, acc_ref = refs[-4:]
        kk = pl.program_id(1)

        @pl.when(kk == 0)
        def _():
            acc_ref[...] = jnp.zeros_like(acc_ref)

        acc_ref[...] += _dot(a_ref[...], b_ref[...])

        @pl.when(kk == nk - 1)
        def _():
            r = ALPHA * x_ref[...] + acc_ref[...]
            if with_ple:
                r = r + _sigmoid(gp_ref[...]) * pp_ref[...]
            mu = jnp.mean(r, axis=1, keepdims=True)
            xc = r - mu
            var = jnp.mean(xc * xc, axis=1, keepdims=True)
            y_ref[...] = xc * lax.rsqrt(var + LN_EPS) * gb_ref[0:1, :] + gb_ref[1:2, :]
            r_ref[...] = r

    row = pl.BlockSpec((tm, d), lambda i, kk: (i, 0))
    in_specs = [pl.BlockSpec((tm, tk), lambda i, kk: (i, kk)), pl.BlockSpec((tk, d), lambda i, kk: (kk, 0)), row]
    in_specs += ([row, row] if with_ple else []) + [pl.BlockSpec((2, d), lambda i, kk: (0, 0))]
    args = (a, b, x) + ((gp, pp) if with_ple else ()) + (gb,)
    return pl.pallas_call(
        body, name=name, grid=(m // tm, nk), in_specs=in_specs, out_specs=[row, row],
        out_shape=[jax.ShapeDtypeStruct((m, d), F32)] * 2, scratch_shapes=[pltpu.VMEM((tm, d), F32)],
        compiler_params=_params(2),
    )(*args)


def _ln_bwd(r, gb, dy, name, gp=None, pp=None, tr=512):
    t, d = r.shape
    tr = _pick(t, tr)
    with_ple = gp is not None

    def body(*refs):
        if with_ple:
            r_ref, dy_ref, gp_ref, pp_ref, gb_ref, dr_ref, dgp_ref, dpp_ref, st_ref = refs
        else:
            r_ref, dy_ref, gb_ref, dr_ref, st_ref = refs
        i = pl.program_id(0)

        @pl.when(i == 0)
        def _():
            st_ref[...] = jnp.zeros_like(st_ref)

        rv = r_ref[...]
        dy_v = dy_ref[...]
        mu = jnp.mean(rv, axis=1, keepdims=True)
        xc = rv - mu
        var = jnp.mean(xc * xc, axis=1, keepdims=True)
        rstd = lax.rsqrt(var + LN_EPS)
        xhat = xc * rstd
        dxh = dy_v * gb_ref[0:1, :]
        m1 = jnp.mean(dxh, axis=1, keepdims=True)
        m2 = jnp.mean(dxh * xhat, axis=1, keepdims=True)
        dr = rstd * (dxh - m1 - xhat * m2)
        dr_ref[...] = dr
        rid = lax.broadcasted_iota(jnp.int32, (2, d), 0)
        dg = jnp.sum(dy_v * xhat, axis=0, keepdims=True)
        db = jnp.sum(dy_v, axis=0, keepdims=True)
        st_ref[...] += jnp.where(rid == 0, dg, db)
        if with_ple:
            sg = _sigmoid(gp_ref[...])
            ppv = pp_ref[...]
            dgp_ref[...] = (dr * ppv * sg * (1.0 - sg)).astype(dgp_ref.dtype)
            dpp_ref[...] = (dr * sg).astype(dpp_ref.dtype)

    row = pl.BlockSpec((tr, d), lambda i: (i, 0))
    vec = pl.BlockSpec((2, d), lambda i: (0, 0))
    if with_ple:
        in_specs, args = [row] * 4 + [vec], (r, dy, gp, pp, gb)
        out_specs = [row, row, row, vec]
        out_shape = [jax.ShapeDtypeStruct((t, d), F32), jax.ShapeDtypeStruct((t, d), MXU_DTYPE),
                     jax.ShapeDtypeStruct((t, d), MXU_DTYPE), jax.ShapeDtypeStruct((2, d), F32)]
    else:
        in_specs, args = [row] * 2 + [vec], (r, dy, gb)
        out_specs = [row, vec]
        out_shape = [jax.ShapeDtypeStruct((t, d), F32), jax.ShapeDtypeStruct((2, d), F32)]
    return pl.pallas_call(body, name=name, grid=(t // tr,), in_specs=in_specs, out_specs=out_specs,
                          out_shape=out_shape, compiler_params=_params(1))(*args)


def _loss_grad(y, target, name, tr=512):
    t, d = y.shape
    tr = _pick(t, tr)

    def body(y_ref, t_ref, dy_ref, l_ref):
        i = pl.program_id(0)

        @pl.when(i == 0)
        def _():
            l_ref[...] = jnp.zeros_like(l_ref)

        e = y_ref[...] - t_ref[...]
        dy_ref[...] = e * (1.0 / d)
        per_tok = jnp.mean(e * e, axis=1, keepdims=True)
        l_ref[...] += 0.5 * jnp.sum(per_tok, axis=0, keepdims=True)

    row = pl.BlockSpec((tr, d), lambda i: (i, 0))
    acc = pl.BlockSpec((SUBLANES, LANES), lambda i: (0, 0))
    return pl.pallas_call(body, name=name, grid=(t // tr,), in_specs=[row, row], out_specs=[row, acc],
                          out_shape=[jax.ShapeDtypeStruct((t, d), F32), jax.ShapeDtypeStruct((SUBLANES, LANES), F32)],
                          compiler_params=_params(1))(y, target)


def _shift_down(v, k, row):
    return jnp.where(row >= k, pltpu.roll(v, k, 0), 0.0)


def _shift_up(v, k, row):
    n = v.shape[0]
    return jnp.where(row < n - k, pltpu.roll(v, n - k, 0), 0.0)


def _pool_window(lane):
    grp = lane // HEAD_DIM
    return jnp.where(grp == 0, POOL_WINDOWS[0], jnp.where(grp == 1, POOL_WINDOWS[1],
                     jnp.where(grp == 2, POOL_WINDOWS[2], POOL_WINDOWS[3])))


def _pool_select(lane, s2, s4, s8, s16):
    grp = lane // HEAD_DIM
    return jnp.where(grp == 0, s2, jnp.where(grp == 1, s4, jnp.where(grp == 2, s8, s16)))


def _pooled(u, row, lane):
    s2 = u + _shift_down(u, 1, row)
    s4 = s2 + _shift_down(s2, 2, row)
    s8 = s4 + _shift_down(s4, 4, row)
    s16 = s8 + _shift_down(s8, 8, row)
    cnt = jnp.minimum(row + 1, _pool_window(lane)).astype(F32)
    return _pool_select(lane, s2, s4, s8, s16) / cnt - u, cnt


def _pool_fwd(h, wbd, scale, nb, s, name):
    def body(u_ref, w_ref, sc_ref, o_ref):
        u = u_ref[...]
        row = lax.broadcasted_iota(jnp.int32, u.shape, 0)
        lane = lax.broadcasted_iota(jnp.int32, u.shape, 1)
        pooled, _ = _pooled(u, row, lane)
        o_ref[...] = (_dot(pooled, w_ref[...]) * sc_ref[...]).astype(o_ref.dtype)

    wb = POOL_WIDTH
    return pl.pallas_call(
        body, name=name, grid=(nb,),
        in_specs=[pl.BlockSpec((s, wb), lambda b: (b, H_POOL // wb)), pl.BlockSpec((wb, wb), lambda b: (0, 0)),
                  pl.BlockSpec((1, wb), lambda b: (0, 0))],
        out_specs=pl.BlockSpec((s, wb), lambda b: (b, 0)),
        out_shape=jax.ShapeDtypeStruct((nb * s, wb), MXU_DTYPE), compiler_params=_params(1),
    )(h, wbd, scale)


def _pool_bwd(h, dmix, wbd, scale, nb, s, name):
    wb = POOL_WIDTH

    def body(u_ref, do_ref, w_ref, sc_ref, du_ref, dw_ref, ds_ref):
        b = pl.program_id(0)

        @pl.when(b == 0)
        def _():
            dw_ref[...] = jnp.zeros_like(dw_ref)
            ds_ref[...] = jnp.zeros_like(ds_ref)

        u = u_ref[...]
        row = lax.broadcasted_iota(jnp.int32, u.shape, 0)
        lane = lax.broadcasted_iota(jnp.int32, u.shape, 1)
        pooled, cnt = _pooled(u, row, lane)
        mixed = _dot(pooled, w_ref[...])
        do = do_ref[...]
        ds_ref[...] += jnp.sum(do * mixed, axis=0, keepdims=True)
        dm = do * sc_ref[...]
        dw_ref[...] += _dot(pooled, dm, TN)
        dpool = _dot(dm, w_ref[...], NT)
        qv = dpool / cnt
        f2 = qv + _shift_up(qv, 1, row)
        f4 = f2 + _shift_up(f2, 2, row)
        f8 = f4 + _shift_up(f4, 4, row)
        f16 = f8 + _shift_up(f8, 8, row)
        du_ref[...] = (_pool_select(lane, f2, f4, f8, f16) - dpool).astype(du_ref.dtype)

    return pl.pallas_call(
        body, name=name, grid=(nb,),
        in_specs=[pl.BlockSpec((s, wb), lambda b: (b, H_POOL // wb)), pl.BlockSpec((s, wb), lambda b: (b, 3)),
                  pl.BlockSpec((wb, wb), lambda b: (0, 0)), pl.BlockSpec((1, wb), lambda b: (0, 0))],
        out_specs=[pl.BlockSpec((s, wb), lambda b: (b, 0)), pl.BlockSpec((wb, wb), lambda b: (0, 0)),
                   pl.BlockSpec((1, wb), lambda b: (0, 0))],
        out_shape=[jax.ShapeDtypeStruct((nb * s, wb), MXU_DTYPE), jax.ShapeDtypeStruct((wb, wb), F32),
                   jax.ShapeDtypeStruct((1, wb), F32)],
        compiler_params=_params(1),
    )(h, dmix, wbd, scale)


def _glu_conv(x, w_ref, b_ref, row):
    return (b_ref[...] + w_ref[2:3, :] * x + w_ref[1:2, :] * _shift_down(x, 1, row)
            + w_ref[0:1, :] * _shift_down(x, 2, row))


def _glu_fwd(up, cw, cb, nb, s, name):
    wt = 2 * GLU_TILE
    nt = up.shape[1] // wt

    def body(u_ref, w_ref, b_ref, o_ref):
        x = u_ref[...]
        row = lax.broadcasted_iota(jnp.int32, x.shape, 0)
        c = _glu_conv(x, w_ref, b_ref, row)
        o_ref[...] = (_silu(c[:, :GLU_TILE]) * c[:, GLU_TILE:]).astype(o_ref.dtype)

    return pl.pallas_call(
        body, name=name, grid=(nt, nb),
        in_specs=[pl.BlockSpec((s, wt), lambda j, b: (b, j)), pl.BlockSpec((3, wt), lambda j, b: (0, j)),
                  pl.BlockSpec((1, wt), lambda j, b: (0, j))],
        out_specs=pl.BlockSpec((s, GLU_TILE), lambda j, b: (b, j)),
        out_shape=jax.ShapeDtypeStruct((nb * s, nt * GLU_TILE), MXU_DTYPE), compiler_params=_params(2),
    )(up, cw, cb)


def _glu_bwd(up, dact, cw, cb, nb, s, name):
    wt = 2 * GLU_TILE
    nt = up.shape[1] // wt

    def body(u_ref, da_ref, w_ref, b_ref, du_ref, acc_ref):
        b = pl.program_id(1)

        @pl.when(b == 0)
        def _():
            acc_ref[...] = jnp.zeros_like(acc_ref)

        x = u_ref[...]
        row = lax.broadcasted_iota(jnp.int32, x.shape, 0)
        x1 = _shift_down(x, 1, row)
        x2 = _shift_down(x, 2, row)
        c = b_ref[...] + w_ref[2:3, :] * x + w_ref[1:2, :] * x1 + w_ref[0:1, :] * x2
        gate, val = c[:, :GLU_TILE], c[:, GLU_TILE:]
        da = da_ref[...]
        dc = jnp.concatenate([da * val * _dsilu(gate), da * _silu(gate)], axis=1)
        dx = (w_ref[2:3, :] * dc + w_ref[1:2, :] * _shift_up(dc, 1, row) + w_ref[0:1, :] * _shift_up(dc, 2, row))
        du_ref[...] = dx.astype(du_ref.dtype)
        rid = lax.broadcasted_iota(jnp.int32, (SUBLANES, wt), 0)
        dw0 = jnp.sum(dc * x2, axis=0, keepdims=True)
        dw1 = jnp.sum(dc * x1, axis=0, keepdims=True)
        dw2 = jnp.sum(dc * x, axis=0, keepdims=True)
        db = jnp.sum(dc, axis=0, keepdims=True)
        acc_ref[...] += (jnp.where(rid == 0, dw0, 0.0) + jnp.where(rid == 1, dw1, 0.0)
                         + jnp.where(rid == 2, dw2, 0.0) + jnp.where(rid == 3, db, 0.0))

    return pl.pallas_call(
        body, name=name, grid=(nt, nb),
        in_specs=[pl.BlockSpec((s, wt), lambda j, b: (b, j)), pl.BlockSpec((s, GLU_TILE), lambda j, b: (b, j)),
                  pl.BlockSpec((3, wt), lambda j, b: (0, j)), pl.BlockSpec((1, wt), lambda j, b: (0, j))],
        out_specs=[pl.BlockSpec((s, wt), lambda j, b: (b, j)), pl.BlockSpec((SUBLANES, wt), lambda j, b: (0, j))],
        out_shape=[jax.ShapeDtypeStruct((nb * s, nt * wt), MXU_DTYPE), jax.ShapeDtypeStruct((SUBLANES, nt * wt), F32)],
        compiler_params=_params(2),
    )(up, dact, cw, cb)


def _sb_constants():
    row = lax.broadcasted_iota(jnp.int32, (QB, QB), 0)
    col = lax.broadcasted_iota(jnp.int32, (QB, QB), 1)
    return jnp.stack([row > col, row < col, col < row]).astype(BF16)


_SB_CONST_SPEC = pl.BlockSpec((3, QB, QB), lambda b, p, i: (0, 0, 0))


def _sb_fwd(h, nb, s, name, comm=None):
    nq = s // QB
    scale = HEAD_DIM ** -0.5
    n_in, n_out = _comm_counts(comm)

    def body(q_ref, k_ref, v_ref, tri_ref, *rest):
        o_ref = rest[n_in]
        i = pl.program_id(2)
        _comm_hosted(comm, rest[:n_in], rest[n_in + 1:], (nb, 3, nq))
        sls = [slice(hd * HEAD_DIM, (hd + 1) * HEAD_DIM) for hd in range(2)]
        qs = [(q_ref[:, sl] * scale).astype(MXU_DTYPE) for sl in sls]

        def scores(hd, j, diagonal=False):
            r0 = pl.multiple_of(j * QB, QB)
            z = _dot(qs[hd], k_ref[pl.ds(r0, QB), sls[hd]], NT)
            ln = -_softplus(z)
            ls = z + ln
            if diagonal:
                low = tri_ref[2] > 0
                ln = jnp.where(low, ln, 0.0)
                ls = jnp.where(low, ls, MASKED_LOG)
            return ls, _dot_exact01(ln, tri_ref[0], terms=2), jnp.sum(ln, axis=1, keepdims=True)

        def output(hd, j, ls, tl, ct):
            r0 = pl.multiple_of(j * QB, QB)
            return _dot(jnp.exp(ls + tl + ct), v_ref[pl.ds(r0, QB), sls[hd]])

        def group(blocks, carry, diagonal_first=False):
            sc = [[scores(hd, j, diagonal_first and n == 0) for n, j in enumerate(blocks)] for hd in range(2)]
            out = []
            for hd in range(2):
                a, c = carry[hd]
                for (ls, tl, sm), j in zip(sc[hd], blocks):
                    a = a + output(hd, j, ls, tl, c)
                    c = c + sm
                out.append((a, c))
            return tuple(out)

        start = (jnp.zeros((QB, HEAD_DIM), F32), jnp.zeros((QB, 1), F32))
        below = jnp.minimum(i, 1)
        left = i - below
        carry = lax.fori_loop(0, below, lambda t, c: group([i, i - 1], c, True), (start, start))
        carry = lax.fori_loop(0, 1 - below, lambda t, c: group([i], c, True), carry)
        carry = lax.fori_loop(0, left // 2, lambda t, c: group([left - 1 - 2 * t, left - 2 - 2 * t], c), carry)
        carry = lax.fori_loop(0, left % 2, lambda t, c: group([0], c), carry)
        o_ref[:, sls[0]] = carry[0][0].astype(o_ref.dtype)
        o_ref[:, sls[1]] = carry[1][0].astype(o_ref.dtype)

    qspec = lambda off: pl.BlockSpec((QB, LANES), lambda b, p, i: (b * nq + i, off // LANES + p))
    kvspec = lambda off: pl.BlockSpec((s, LANES), lambda b, p, i: (b, off // LANES + p))
    c_in, c_specs, c_shapes, c_scratch = _comm_call_args(comm)
    res = pl.pallas_call(
        body, name=name, grid=(nb, 3, nq), in_specs=[qspec(H_Q), kvspec(H_K), kvspec(H_V), _SB_CONST_SPEC] + [_ANY] * n_in,
        out_specs=[pl.BlockSpec((QB, LANES), lambda b, p, i: (b * nq + i, p))] + c_specs,
        out_shape=[jax.ShapeDtypeStruct((nb * s, SB_WIDTH), MXU_DTYPE)] + c_shapes, scratch_shapes=c_scratch,
        compiler_params=_params(3, comm is not None),
    )(h, h, h, _sb_constants(), *c_in)
    return res[0], res[1:]


def _sb_bwd(h, dmix, nb, s, name, comm=None):
    nq = s // QB
    scale = HEAD_DIM ** -0.5
    n_in, n_out = _comm_counts(comm)

    def body(q_ref, k_ref, v_ref, do_ref, tri_ref, *rest):
        dq_ref, dk_out, dv_out = rest[n_in:n_in + 3]
        p_buf, ls_buf, dk_ref, dv_ref = rest[n_in + 3 + n_out:n_in + 7 + n_out]
        i = pl.program_id(2)
        _comm_hosted(comm, rest[:n_in], rest[n_in + 3:n_in + 3 + n_out] + rest[n_in + 7 + n_out:], (nb, 3, nq))

        @pl.when(i == 0)
        def _():
            dk_ref[...] = jnp.zeros_like(dk_ref)
            dv_ref[...] = jnp.zeros_like(dv_ref)

        sls = [slice(hd * HEAD_DIM, (hd + 1) * HEAD_DIM) for hd in range(2)]
        q_raw = [q_ref[:, sl].astype(MXU_DTYPE) for sl in sls]
        qs = [(q_ref[:, sl] * scale).astype(MXU_DTYPE) for sl in sls]
        do = [do_ref[:, sl].astype(MXU_DTYPE) for sl in sls]

        def down_scores(hd, j, diagonal):
            r0 = pl.multiple_of(j * QB, QB)
            z = _dot(qs[hd], k_ref[pl.ds(r0, QB), sls[hd]], NT)
            ln = -_softplus(z)
            ls = z + ln
            if diagonal:
                low = tri_ref[2] > 0
                ln = jnp.where(low, ln, 0.0)
                ls = jnp.where(low, ls, MASKED_LOG)
            da = _dot(do[hd], v_ref[pl.ds(r0, QB), sls[hd]], NT)
            return ls, _dot_exact01(ln, tri_ref[0], terms=2), jnp.sum(ln, axis=1, keepdims=True), da

        def down_group(blocks, carry, diagonal_first=False):
            sc = [[down_scores(hd, j, diagonal_first and n == 0) for n, j in enumerate(blocks)] for hd in range(2)]
            out = []
            for hd in range(2):
                ct = carry[hd]
                for (ls, tl, sm, da), j in zip(sc[hd], blocks):
                    r0 = pl.multiple_of(j * QB, QB)
                    a = jnp.exp(ls + tl + ct)
                    p_buf[hd, j] = da * a
                    ls_buf[hd, j] = ls
                    dv_ref[pl.ds(r0, QB), sls[hd]] += _dot(a, do[hd], TN)
                    ct = ct + sm
                out.append(ct)
            return tuple(out)

        zero = jnp.zeros((QB, 1), F32)
        below = jnp.minimum(i, 1)
        left = i - below
        carry = lax.fori_loop(0, below, lambda t, c: down_group([i, i - 1], c, True), (zero, zero))
        carry = lax.fori_loop(0, 1 - below, lambda t, c: down_group([i], c, True), carry)
        carry = lax.fori_loop(0, left // 2, lambda t, c: down_group([left - 1 - 2 * t, left - 2 - 2 * t], c), carry)
        lax.fori_loop(0, left % 2, lambda t, c: down_group([0], c), carry)

        def up_group(blocks, carry):
            ld = []
            for hd in range(2):
                ld.append([])
                for j in blocks:
                    pj = p_buf[hd, j]
                    ld[hd].append((pj, jnp.exp(ls_buf[hd, j]), _dot_exact01(pj, tri_ref[1]), jnp.sum(pj, axis=1, keepdims=True)))
            out = []
            for hd in range(2):
                dq, cp = carry[hd]
                for (pj, sg, cm, sm), j in zip(ld[hd], blocks):
                    r0 = pl.multiple_of(j * QB, QB)
                    dz = (pj * (1.0 - sg) - (cp + cm) * sg) * scale
                    dk_ref[pl.ds(r0, QB), sls[hd]] += _dot(dz, q_raw[hd], TN)
                    dq = dq + _dot(dz, k_ref[pl.ds(r0, QB), sls[hd]])
                    cp = cp + sm
                out.append((dq, cp))
            return tuple(out)

        start = (jnp.zeros((QB, HEAD_DIM), F32), zero)
        odd = (i + 1) % 2
        carry = lax.fori_loop(0, odd, lambda t, c: up_group([0], c), (start, start))
        carry = lax.fori_loop(0, (i + 1) // 2, lambda t, c: up_group([odd + 2 * t, odd + 2 * t + 1], c), carry)
        dq_ref[:, sls[0]] = carry[0][0].astype(dq_ref.dtype)
        dq_ref[:, sls[1]] = carry[1][0].astype(dq_ref.dtype)

        @pl.when(i == nq - 1)
        def _():
            dk_out[...] = dk_ref[...].astype(dk_out.dtype)
            dv_out[...] = dv_ref[...].astype(dv_out.dtype)

    qspec = lambda off: pl.BlockSpec((QB, LANES), lambda b, p, i: (b * nq + i, off // LANES + p))
    kvspec = lambda off: pl.BlockSpec((s, LANES), lambda b, p, i: (b, off // LANES + p))
    blk_out = pl.BlockSpec((QB, LANES), lambda b, p, i: (b * nq + i, p))
    seq_out = pl.BlockSpec((s, LANES), lambda b, p, i: (b, p))
    shp = jax.ShapeDtypeStruct((nb * s, SB_WIDTH), MXU_DTYPE)
    c_in, c_specs, c_shapes, c_scratch = _comm_call_args(comm)
    res = pl.pallas_call(
        body, name=name, grid=(nb, 3, nq),
        in_specs=[qspec(H_Q), kvspec(H_K), kvspec(H_V), pl.BlockSpec((QB, LANES), lambda b, p, i: (b * nq + i, 3 + p)),
                  _SB_CONST_SPEC] + [_ANY] * n_in,
        out_specs=[blk_out, seq_out, seq_out] + c_specs, out_shape=[shp, shp, shp] + c_shapes,
        scratch_shapes=[pltpu.VMEM((2, nq, QB, QB), F32), pltpu.VMEM((2, nq, QB, QB), F32),
                        pltpu.VMEM((s, LANES), F32), pltpu.VMEM((s, LANES), F32)] + c_scratch,
        compiler_params=_params(3, comm is not None),
    )(h, h, h, dmix, _sb_constants(), *c_in)
    return res[0], res[1], res[2], res[3:]


def _ssd_conv(cur_ref, halo_ref, w_ref, b_ref, ext_ref, first):
    n = SSD_CHUNK
    cur = cur_ref[...]
    ext_ref[0:SUBLANES, :] = jnp.where(first, 0.0, halo_ref[...])
    ext_ref[SUBLANES:SUBLANES + n, :] = cur
    return (b_ref[...] + w_ref[3:4, :] * cur + w_ref[2:3, :] * ext_ref[pl.ds(SUBLANES - 1, n), :]
            + w_ref[1:2, :] * ext_ref[pl.ds(SUBLANES - 2, n), :] + w_ref[0:1, :] * ext_ref[pl.ds(SUBLANES - 3, n), :])


def _ssd_tri():
    row = lax.broadcasted_iota(jnp.int32, (SSD_CHUNK, SSD_CHUNK), 0)
    col = lax.broadcasted_iota(jnp.int32, (SSD_CHUNK, SSD_CHUNK), 1)
    return row, col


def _ssd_specs(nc, rev):
    n = SSD_CHUNK
    hb = n // SUBLANES

    def cidx(c):
        return (nc - 1 - c) if rev else c

    def blk(width, off):
        return pl.BlockSpec((n, width), lambda b, c: (b * nc + cidx(c), off // width))

    def halo(width, off):
        return pl.BlockSpec((SUBLANES, width), lambda b, c: (jnp.maximum((b * nc + cidx(c)) * hb - 1, 0), off // width))

    def full(shape):
        return pl.BlockSpec(shape, lambda b, c: (0,) * len(shape))

    return cidx, blk, halo, full


def _ssd_core_fwd(x, bc, dt, acum, acum_t, a_row, d_row, h_prev_ref, tri):
    n = SSD_CHUNK
    heads = []
    for g in range(2):
        bm = bc[:, g * SSD_STATE:(g + 1) * SSD_STATE]
        cm = bc[:, 2 * SSD_STATE + g * SSD_STATE: 2 * SSD_STATE + (g + 1) * SSD_STATE]
        gmat = _dot(cm, bm, NT)
        for r in range(3):
            hh = g * 3 + r
            hp = h_prev_ref[hh * HEAD_DIM:(hh + 1) * HEAD_DIM, :]
            heads.append(dict(g=g, hh=hh, bm=bm, cm=cm, gmat=gmat, hp=hp, cmh=_dot(cm, hp, NT)))
    for hd in heads:
        hh = hd["hh"]
        ac = acum[:, hh:hh + 1]
        ar = acum_t[hh:hh + 1, :]
        hd["dec"] = jnp.where(tri, jnp.exp(jnp.minimum(ac - ar, 0.0)), 0.0)
        hd["xh"] = x[:, hh * HEAD_DIM:(hh + 1) * HEAD_DIM]
        hd["dth"] = dt[:, hh:hh + 1]
        hd["xdt"] = hd["xh"] * hd["dth"]
        hd["ea"] = jnp.exp(ac)
        hd["m"] = hd["gmat"] * hd["dec"]
        hd["al"] = acum[n - 1:n, hh:hh + 1]
        hd["w"] = jnp.exp(hd["al"] - ac)
    for hd in heads:
        hd["yd"] = _dot(hd["m"], hd["xdt"])
    for hd in heads:
        hd["yo"] = hd["ea"] * hd["cmh"]
        hd["y"] = hd["yd"] + hd["yo"] + d_row[:, hd["hh"]:hd["hh"] + 1] * hd["xh"]
    return heads


def _ssd_prep(xs_ref, xsh_ref, bc_ref, bch_ref, dt_ref, cwx_ref, cbx_ref, cwb_ref, cbb_ref, vec_ref, xe_ref, be_ref, first):
    pre_x = _ssd_conv(xs_ref, xsh_ref, cwx_ref, cbx_ref, xe_ref, first)
    pre_bc = _ssd_conv(bc_ref, bch_ref, cwb_ref, cbb_ref, be_ref, first)
    x = _silu(pre_x)
    bc = _silu(pre_bc)
    dt_pre = dt_ref[...] + vec_ref[0:1, :]
    dt = _softplus(dt_pre)
    a_row = vec_ref[1:2, :]
    amat = dt * a_row
    row, col = _ssd_tri()
    upper = (row <= col).astype(BF16)
    lower = (col <= row).astype(BF16)
    acum = _dot_exact01(amat, lower, NN, x_left=False)
    acum_t = _dot_exact01(amat, upper, TN, x_left=True)
    return pre_x, pre_bc, x, bc, dt_pre, dt, a_row, acum, acum_t, row, col, upper


def _ssd_gate_norm(y, z, nw):
    lane = lax.broadcasted_iota(jnp.int32, y.shape, 1)
    g0 = lane < SSD_WIDTH // 2
    hg = y * _silu(z)
    sq = hg * hg
    ms0 = jnp.sum(jnp.where(g0, sq, 0.0), axis=1, keepdims=True) * (2.0 / SSD_WIDTH)
    ms1 = jnp.sum(jnp.where(g0, 0.0, sq), axis=1, keepdims=True) * (2.0 / SSD_WIDTH)
    rs = jnp.where(g0, lax.rsqrt(ms0 + RMS_EPS), lax.rsqrt(ms1 + RMS_EPS))
    return hg, rs, g0


def _ssd_fwd(h, cwx, cbx, cwb, cbb, vec, nw, nb, s, name):
    n = SSD_CHUNK
    nc = s // n
    _, blk, halo, full = _ssd_specs(nc, False)

    def body(bc_ref, bch_ref, z_ref, xs_ref, xsh_ref, dt_ref, cwx_ref, cbx_ref, cwb_ref, cbb_ref, vec_ref, nw_ref,
             o_ref, hs_ref, h_scr, xe_ref, be_ref, y_scr):
        c = pl.program_id(1)

        @pl.when(c == 0)
        def _():
            h_scr[...] = jnp.zeros_like(h_scr)

        (_, _, x, bc, _, dt, a_row, acum, acum_t, row, col, _) = _ssd_prep(
            xs_ref, xsh_ref, bc_ref, bch_ref, dt_ref, cwx_ref, cbx_ref, cwb_ref, cbb_ref, vec_ref, xe_ref, be_ref, c == 0)
        hs_ref[...] = h_scr[...]
        heads = _ssd_core_fwd(x, bc, dt, acum, acum_t, a_row, vec_ref[2:3, :], hs_ref, col <= row)
        for hd in heads:
            sl = slice(hd["hh"] * HEAD_DIM, (hd["hh"] + 1) * HEAD_DIM)
            y_scr[:, sl] = hd["y"]
            h_scr[sl, :] = jnp.exp(hd["al"]) * hd["hp"] + _dot(hd["xdt"] * hd["w"], hd["bm"], TN)
        hg, rs, _ = _ssd_gate_norm(y_scr[...], z_ref[...], nw_ref[...])
        o_ref[...] = (hg * rs * nw_ref[...]).astype(o_ref.dtype)

    t = nb * s
    return pl.pallas_call(
        body, name=name, grid=(nb, nc),
        in_specs=[blk(512, H_BC), halo(512, H_BC), blk(384, H_Z), blk(384, H_XS), halo(384, H_XS), blk(128, H_DT),
                  full((4, 384)), full((1, 384)), full((4, 512)), full((1, 512)), full((SUBLANES, LANES)), full((1, 384))],
        out_specs=[pl.BlockSpec((n, SSD_WIDTH), lambda b, c: (b * nc + c, 0)),
                   pl.BlockSpec((None, SSD_WIDTH, SSD_STATE), lambda b, c: (b * nc + c, 0, 0))],
        out_shape=[jax.ShapeDtypeStruct((t, SSD_WIDTH), MXU_DTYPE),
                   jax.ShapeDtypeStruct((nb * nc, SSD_WIDTH, SSD_STATE), F32)],
        scratch_shapes=[pltpu.VMEM((SSD_WIDTH, SSD_STATE), F32), pltpu.VMEM((n + SUBLANES, 384), F32),
                        pltpu.VMEM((n + SUBLANES, 512), F32), pltpu.VMEM((n, SSD_WIDTH), F32)],
        compiler_params=_params(2),
    )(h, h, h, h, h, h, cwx, cbx, cwb, cbb, vec, nw)


def _ssd_bwd(h, hstate, dmix, cwx, cbx, cwb, cbb, vec, nw, nb, s, name):
    n = SSD_CHUNK
    nc = s // n
    cidx, blk, halo, full = _ssd_specs(nc, True)

    def body(bc_ref, bch_ref, z_ref, xs_ref, xsh_ref, dt_ref, hs_ref, do_ref, cwx_ref, cbx_ref, cwb_ref, cbb_ref,
             vec_ref, nw_ref, dz_ref, dxs_ref, dbc_ref, ddt_ref, gx_ref, gb_ref, gv_ref, gn_ref,
             dh_scr, xe_ref, be_ref, y_scr, dx_scr, dbc_scr, dxe_ref, dbe_ref, cx_ref, cb_ref):
        b = pl.program_id(0)
        c = pl.program_id(1)
        cc = nc - 1 - c

        @pl.when(jnp.logical_and(b == 0, c == 0))
        def _():
            gx_ref[...] = jnp.zeros_like(gx_ref)
            gb_ref[...] = jnp.zeros_like(gb_ref)
            gv_ref[...] = jnp.zeros_like(gv_ref)
            gn_ref[...] = jnp.zeros_like(gn_ref)

        @pl.when(c == 0)
        def _():
            dh_scr[...] = jnp.zeros_like(dh_scr)
            cx_ref[...] = jnp.zeros_like(cx_ref)
            cb_ref[...] = jnp.zeros_like(cb_ref)

        (pre_x, pre_bc, x, bc, dt_pre, dt, a_row, acum, acum_t, row, col, upper) = _ssd_prep(
            xs_ref, xsh_ref, bc_ref, bch_ref, dt_ref, cwx_ref, cbx_ref, cwb_ref, cbb_ref, vec_ref, xe_ref, be_ref, cc == 0)
        tri = col <= row
        d_row = vec_ref[2:3, :]
        heads = _ssd_core_fwd(x, bc, dt, acum, acum_t, a_row, d_row, hs_ref, tri)
        for hd in heads:
            y_scr[:, hd["hh"] * HEAD_DIM:(hd["hh"] + 1) * HEAD_DIM] = hd["y"]
        y = y_scr[...]
        z = z_ref[...]
        nwv = nw_ref[...]
        hg, rs, g0 = _ssd_gate_norm(y, z, nwv)
        do = do_ref[...]
        nrm = hg * rs
        gn_ref[...] += jnp.sum(do * nrm, axis=0, keepdims=True)
        dn = do * nwv
        dnn = dn * nrm
        mean0 = jnp.sum(jnp.where(g0, dnn, 0.0), axis=1, keepdims=True) * (2.0 / SSD_WIDTH)
        mean1 = jnp.sum(jnp.where(g0, 0.0, dnn), axis=1, keepdims=True) * (2.0 / SSD_WIDTH)
        dhg = rs * (dn - nrm * jnp.where(g0, mean0, mean1))
        dz_ref[...] = (dhg * y * _dsilu(z)).astype(dz_ref.dtype)
        dy = dhg * _silu(z)

        lane = lax.broadcasted_iota(jnp.int32, (n, LANES), 1)
        lane1 = lax.broadcasted_iota(jnp.int32, (1, LANES), 1)
        last_row = lax.broadcasted_iota(jnp.int32, (n, 1), 0) == n - 1
        dacum_col = jnp.zeros((n, LANES), F32)
        da_rowpart = jnp.zeros((n, LANES), F32)
        ddt = jnp.zeros((n, LANES), F32)
        dd_vec = jnp.zeros((1, LANES), F32)
        for hd in heads:
            sl = slice(hd["hh"] * HEAD_DIM, (hd["hh"] + 1) * HEAD_DIM)
            dyh = dy[:, sl]
            dhn = dh_scr[sl, :]
            hd.update(sl=sl, dyh=dyh, dhn=dhn, t1=_dot(dyh, hd["hp"]), dm=_dot(dyh, hd["xdt"], NT),
                      t2=_dot(hd["bm"], dhn, NT), mtdy=_dot(hd["m"], dyh, TN), xdhn=_dot(hd["xdt"], dhn),
                      dhp=_dot(dyh * hd["ea"], hd["cm"], TN))
        dgs, dbms, dcms = [], [], []
        for g in range(2):
            dg = jnp.zeros((n, n), F32)
            dbm = jnp.zeros((n, SSD_STATE), F32)
            dcm = jnp.zeros((n, SSD_STATE), F32)
            for hd in heads[3 * g:3 * g + 3]:
                hh, sl, dyh, dhn, t2 = hd["hh"], hd["sl"], hd["dyh"], hd["dhn"], hd["t2"]
                el = jnp.exp(hd["al"])
                dd_vec = dd_vec + jnp.where(lane1 == hh, jnp.sum(dyh * hd["xh"]), 0.0)
                dcm = dcm + hd["ea"] * hd["t1"]
                dg = dg + hd["dm"] * hd["dec"]
                e = hd["dm"] * hd["m"]
                dxdt = hd["mtdy"] + hd["w"] * t2
                dbm = dbm + hd["w"] * hd["xdhn"]
                dw_w = jnp.sum(hd["xdt"] * t2, axis=1, keepdims=True) * hd["w"]
                d_el = jnp.sum(dhn * hd["hp"])
                col_part = (jnp.sum(dyh * hd["yo"], axis=1, keepdims=True) + jnp.sum(e, axis=1, keepdims=True) - dw_w
                            + jnp.where(last_row, d_el * el + jnp.sum(dw_w), 0.0))
                dacum_col = dacum_col + jnp.where(lane == hh, col_part, 0.0)
                neg_colsum = -jnp.sum(e, axis=0, keepdims=True)
                rev = jnp.sum(jnp.where(row <= col, neg_colsum, 0.0), axis=1, keepdims=True)
                da_rowpart = da_rowpart + jnp.where(lane == hh, rev, 0.0)
                dh_scr[sl, :] = el * dhn + hd["dhp"]
                dx_scr[:, sl] = d_row[:, hh:hh + 1] * dyh + dxdt * hd["dth"]
                ddt = ddt + jnp.where(lane == hh, jnp.sum(dxdt * hd["xh"], axis=1, keepdims=True), 0.0)
            dgs.append(dg)
            dbms.append(dbm)
            dcms.append(dcm)
        for g in range(2):
            bm, cm = heads[3 * g]["bm"], heads[3 * g]["cm"]
            dbc_scr[:, g * SSD_STATE:(g + 1) * SSD_STATE] = dbms[g] + _dot(dgs[g], cm, TN)
            dbc_scr[:, 2 * SSD_STATE + g * SSD_STATE:2 * SSD_STATE + (g + 1) * SSD_STATE] = dcms[g] + _dot(dgs[g], bm)
        da_mat = _dot_exact01(dacum_col, upper, NN, x_left=False) + da_rowpart
        ddt = ddt + da_mat * a_row
        da_vec = jnp.sum(da_mat * dt, axis=0, keepdims=True)
        ddt_pre = jnp.where(lane < SSD_HEADS, ddt * _sigmoid(dt_pre), 0.0)
        ddt_ref[...] = ddt_pre.astype(ddt_ref.dtype)
        rid = lax.broadcasted_iota(jnp.int32, (SUBLANES, LANES), 0)
        gv_ref[...] += (jnp.where(rid == 0, jnp.sum(ddt_pre, axis=0, keepdims=True), 0.0)
                        + jnp.where(rid == 1, da_vec, 0.0) + jnp.where(rid == 2, dd_vec, 0.0))

        def conv_bwd(dpost, pre, w_ref, ext_ref, dext_ref, carry_ref, cur_ref, out_ref, g_ref, width):
            dco = dpost * _dsilu(pre)
            dext_ref[0:n, :] = dco
            dext_ref[n:n + SUBLANES, :] = carry_ref[...]
            out_ref[...] = (w_ref[3:4, :] * dco + w_ref[2:3, :] * dext_ref[pl.ds(1, n), :]
                            + w_ref[1:2, :] * dext_ref[pl.ds(2, n), :] + w_ref[0:1, :] * dext_ref[pl.ds(3, n), :]
                            ).astype(out_ref.dtype)
            carry_ref[...] = dco[0:SUBLANES, :]
            rid8 = lax.broadcasted_iota(jnp.int32, (SUBLANES, width), 0)
            acc = jnp.where(rid8 == 3, jnp.sum(dco * cur_ref[...], axis=0, keepdims=True), 0.0)
            for j in range(3):
                sh = ext_ref[pl.ds(SUBLANES - 3 + j, n), :]
                acc = acc + jnp.where(rid8 == j, jnp.sum(dco * sh, axis=0, keepdims=True), 0.0)
            acc = acc + jnp.where(rid8 == 4, jnp.sum(dco, axis=0, keepdims=True), 0.0)
            g_ref[...] += acc

        conv_bwd(dx_scr[...], pre_x, cwx_ref, xe_ref, dxe_ref, cx_ref, xs_ref, dxs_ref, gx_ref, 384)
        conv_bwd(dbc_scr[...], pre_bc, cwb_ref, be_ref, dbe_ref, cb_ref, bc_ref, dbc_ref, gb_ref, 512)

    t = nb * s
    rowblk = lambda width: pl.BlockSpec((n, width), lambda b, c: (b * nc + cidx(c), 0))
    return pl.pallas_call(
        body, name=name, grid=(nb, nc),
        in_specs=[blk(512, H_BC), halo(512, H_BC), blk(384, H_Z), blk(384, H_XS), halo(384, H_XS), blk(128, H_DT),
                  pl.BlockSpec((None, SSD_WIDTH, SSD_STATE), lambda b, c: (b * nc + cidx(c), 0, 0)),
                  pl.BlockSpec((n, SSD_WIDTH), lambda b, c: (b * nc + cidx(c), 0)),
                  full((4, 384)), full((1, 384)), full((4, 512)), full((1, 512)), full((SUBLANES, LANES)), full((1, 384))],
        out_specs=[rowblk(384), rowblk(384), rowblk(512), rowblk(128),
                   full((SUBLANES, 384)), full((SUBLANES, 512)), full((SUBLANES, LANES)), full((1, 384))],
        out_shape=[jax.ShapeDtypeStruct((t, 384), MXU_DTYPE), jax.ShapeDtypeStruct((t, 384), MXU_DTYPE),
                   jax.ShapeDtypeStruct((t, 512), MXU_DTYPE), jax.ShapeDtypeStruct((t, 128), MXU_DTYPE),
                   jax.ShapeDtypeStruct((SUBLANES, 384), F32), jax.ShapeDtypeStruct((SUBLANES, 512), F32),
                   jax.ShapeDtypeStruct((SUBLANES, LANES), F32), jax.ShapeDtypeStruct((1, 384), F32)],
        scratch_shapes=[pltpu.VMEM((SSD_WIDTH, SSD_STATE), F32), pltpu.VMEM((n + SUBLANES, 384), F32),
                        pltpu.VMEM((n + SUBLANES, 512), F32), pltpu.VMEM((n, SSD_WIDTH), F32),
                        pltpu.VMEM((n, 384), F32), pltpu.VMEM((n, 512), F32),
                        pltpu.VMEM((n + SUBLANES, 384), F32), pltpu.VMEM((n + SUBLANES, 512), F32),
                        pltpu.VMEM((SUBLANES, 384), F32), pltpu.VMEM((SUBLANES, 512), F32)],
        compiler_params=_params(2),
    )(h, h, h, h, h, h, hstate, dmix, cwx, cbx, cwb, cbb, vec, nw)


def _adamw_math(w, g, m, v):
    m = ADAM_B1 * m + (1.0 - ADAM_B1) * g
    v = ADAM_B2 * v + (1.0 - ADAM_B2) * (g * g)
    m_hat = m / (1.0 - ADAM_B1 ** ADAM_STEP)
    v_hat = v / (1.0 - ADAM_B2 ** ADAM_STEP)
    delta = -ADAM_LR * (m_hat / (jnp.sqrt(v_hat) + ADAM_EPS) + ADAM_WD * w)
    return delta, m, v


def _adamw(w, g, m, v, name, tr=256):
    rows, cols = w.shape
    tr = rows if rows <= tr else tr
    assert rows % tr == 0, (rows, tr)

    def body(w_ref, g_ref, m_ref, v_ref, d_ref, nm_ref, nv_ref):
        d, nm, nv = _adamw_math(w_ref[...], g_ref[...], m_ref[...], v_ref[...])
        d_ref[...] = d
        nm_ref[...] = nm
        nv_ref[...] = nv

    spec = pl.BlockSpec((tr, cols), lambda i: (i, 0))
    shp = jax.ShapeDtypeStruct((rows, cols), F32)
    return pl.pallas_call(body, name=name, grid=(rows // tr,), in_specs=[spec] * 4, out_specs=[spec] * 3,
                          out_shape=[shp] * 3, compiler_params=_params(1))(w, g, m, v)


def _sum8_layers(parts, name, tr):
    _, rows, cols = parts[0].shape
    assert rows % tr == 0
    nt = rows // tr

    def body(*refs):
        o_ref = refs[DEPTH]
        layer = pl.program_id(0)
        for l in range(DEPTH):
            @pl.when(layer == l)
            def _(l=l):
                acc = refs[l][0]
                for k in range(1, N_DEV):
                    acc = acc + refs[l][k]
                o_ref[...] = acc

    in_specs = [pl.BlockSpec((N_DEV, tr, cols), lambda a, i, l=l: (0, jnp.clip(i + (a - l) * nt, 0, nt - 1), 0))
                for l in range(DEPTH)]
    return pl.pallas_call(body, name=name, grid=(DEPTH, nt), in_specs=in_specs,
                          out_specs=pl.BlockSpec((None, tr, cols), lambda a, i: (a, i, 0)),
                          out_shape=jax.ShapeDtypeStruct((DEPTH, rows, cols), F32), compiler_params=_params(2))(*parts)


def _all_reduce_small(vec, name):
    rows, cols = vec.shape

    def body(x_ref, out_ref, gbuf, send_sems, recv_sems):
        x, y, c = lax.axis_index("x"), lax.axis_index("y"), lax.axis_index("c")
        me, sibling = (x, y, c), (x, y, 1 - c)
        chips = [(1 - x, y), (x, 1 - y), (1 - x, 1 - y)]

        def slot(px, py, pc):
            return gbuf.at[4 * px + 2 * py + pc]

        def copy(k, block, to, src=None):
            return pltpu.make_async_remote_copy(
                src_ref=slot(*block) if src is None else src, dst_ref=slot(*block),
                send_sem=send_sems.at[k], recv_sem=recv_sems.at[k], device_id=to, device_id_type=MESH_ID)

        first = [copy(0, me, sibling, src=x_ref)]
        first += [copy(1 + j, me, (*chip, c), src=x_ref) for j, chip in enumerate(chips)]
        for cp in first:
            cp.start()
        gbuf[4 * x + 2 * y + c] = x_ref[...]
        passed = [copy(4 + j, (*chip, c), sibling) for j, chip in enumerate(chips)]
        for j, chip in enumerate(chips):
            copy(1 + j, (*chip, c), me).wait_recv()
            passed[j].start()
        copy(0, sibling, me).wait_recv()
        for j, chip in enumerate(chips):
            copy(4 + j, (*chip, 1 - c), me).wait_recv()
        for cp in first + passed:
            cp.wait_send()
        acc = gbuf[0]
        for k in range(1, N_DEV):
            acc = acc + gbuf[k]
        out_ref[...] = acc

    return pl.pallas_call(
        body, name=name, out_shape=jax.ShapeDtypeStruct((rows, cols), F32),
        in_specs=[pl.BlockSpec(memory_space=pltpu.VMEM)], out_specs=pl.BlockSpec(memory_space=pltpu.VMEM),
        scratch_shapes=[pltpu.VMEM((N_DEV, rows, cols), F32), pltpu.SemaphoreType.DMA((7,)), pltpu.SemaphoreType.DMA((7,))],
        compiler_params=pltpu.CompilerParams(has_side_effects=True, vmem_limit_bytes=VMEM_LIMIT_BYTES),
    )(vec)


_COL_POOL, _COL_Z, _COL_XBC, _COL_DT, _COL_Q, _COL_K, _COL_V = 0, 256, 640, 1536, 1542, 1926, 2310
_H_SEGMENTS = ((_COL_XBC + SSD_WIDTH, 512), (_COL_POOL, 256), (_COL_Q, 384), (_COL_K, 384), (_COL_V, 384),
               (_COL_Z, 384), (_COL_XBC, 384), (_COL_DT, 6))


def _h_from_orig(w):
    parts = [w[..., o:o + n] for o, n in _H_SEGMENTS]
    pad = jnp.zeros(w.shape[:-1] + (H_COLS - IN_COLS,), w.dtype)
    return jnp.concatenate(parts + [pad], axis=-1)


def _h_to_orig(w):
    offs, o = {}, 0
    for orig, n in _H_SEGMENTS:
        offs[orig] = (o, n)
        o += n
    order = sorted(offs)
    return jnp.concatenate([w[..., offs[k][0]:offs[k][0] + offs[k][1]] for k in order], axis=-1)


def _interleave(w):
    lead = w.shape[:-1]
    nt = D_FF // GLU_TILE
    return jnp.swapaxes(w.reshape(lead + (2, nt, GLU_TILE)), -3, -2).reshape(lead + (2 * D_FF,))


def _deinterleave(w):
    lead = w.shape[:-1]
    nt = D_FF // GLU_TILE
    return jnp.swapaxes(w.reshape(lead + (nt, 2, GLU_TILE)), -3, -2).reshape(lead + (2 * D_FF,))


def _up_segments():
    segs = []
    for j in range(N_DEV):
        half, base = j // 4, UP_SHARD * (j % 4)
        c = base
        while c < base + UP_SHARD:
            t, r = divmod(c, GLU_TILE)
            n = min(GLU_TILE - r, base + UP_SHARD - c)
            segs.append((j, c - base, 2 * GLU_TILE * t + GLU_TILE * half + r, n))
            c += n
    return segs


def _up_to_interleaved(w, name, tr=256):
    def body(i_ref, o_ref):
        for j, src, dst, n in _up_segments():
            o_ref[:, dst:dst + n] = i_ref[j, :, src:src + n]

    return pl.pallas_call(
        body, name=name, grid=(D_MODEL // tr,), in_specs=[pl.BlockSpec((N_DEV, tr, UP_SHARD), lambda r: (0, r, 0))],
        out_specs=pl.BlockSpec((tr, 2 * D_FF), lambda r: (r, 0)),
        out_shape=jax.ShapeDtypeStruct((D_MODEL, 2 * D_FF), w.dtype), compiler_params=_params(1))(w)


def _up_from_interleaved(g, name, tr=128):
    def body(i_ref, o_ref):
        for j, src, dst, n in _up_segments():
            o_ref[j, :, src:src + n] = i_ref[:, dst:dst + n]

    return pl.pallas_call(
        body, name=name, grid=(D_MODEL // tr,), in_specs=[pl.BlockSpec((tr, 2 * D_FF), lambda r: (r, 0))],
        out_specs=pl.BlockSpec((N_DEV, tr, UP_SHARD), lambda r: (0, r, 0)),
        out_shape=jax.ShapeDtypeStruct((N_DEV, D_MODEL, UP_SHARD), g.dtype), compiler_params=_params(1))(g)


def _mix_rows_from_orig(w):
    return jnp.concatenate([w[256:640], w[640:1024], w[0:256]], axis=0)


def _mix_rows_to_orig(w):
    return jnp.concatenate([w[768:1024], w[0:384], w[384:768]], axis=0)


def _xbc_split(w):
    return w[..., :SSD_WIDTH], w[..., SSD_WIDTH:]


def _layer_fwd(x, p_l, wt, sp, nb, s, comm=None):
    h = _mm(x, wt["w_in"], "nn", F32, "mm_in", tm=1024, tn=1408)
    pool_out = _pool_fwd(h, wt["pool_bd"], sp["pool_scale"], nb, s, "pool_fwd")
    ssd_out, hstate = _ssd_fwd(h, sp["cwx"], sp["cbx"], sp["cwb"], sp["cbb"], sp["ssd_vec"], sp["ssd_norm_w"], nb, s, "ssd_fwd")
    sb_out, comm_out = _sb_fwd(h, nb, s, "sb_fwd" if comm is None else "sb_fwd_gather", comm)
    mixcat = jnp.concatenate([ssd_out, sb_out, pool_out], axis=1)
    x1, r1 = _mm_ln(mixcat, wt["w_out"], x, sp["ln1"], "mm_out_ln1", tm=1024, tk=1024)
    up = _mm(x1, wt["w_up"], "nn", F32, "mm_up", tm=1024, tn=1408)
    act = _glu_fwd(up, sp["ffn_cw"], sp["ffn_cb"], nb, s, "glu_fwd")
    gp = _mm(x1, wt["w_gate"], "nn", F32, "mm_gate", tm=1024, tn=1024)
    pp = _mm(p_l, wt["w_proj"], "nn", F32, "mm_proj", tm=2048, tn=1024)
    x2, r2 = _mm_ln(act, wt["w_down"], x1, sp["ln2"], "mm_down_ln2", gp=gp, pp=pp, tm=512, tk=1408)
    return x2, dict(x=x, h=h, hstate=hstate, mixcat=mixcat, r1=r1, x1=x1, up=up, act=act, gp=gp, pp=pp, r2=r2), comm_out


def _layer_bwd(dx2, p_l, sv, wt, sp, nb, s, comm=None, tail_comm=None):
    dr2, dgp, dpp, st2 = _ln_bwd(sv["r2"], sp["ln2"], dx2, "ln2_bwd", gp=sv["gp"], pp=sv["pp"])
    g_down = _mm(sv["act"], dr2, "tn", F32, "wg_down", tm=1408, tn=1024, tk=512)
    dact = _mm(dr2, wt["w_down"], "nt", F32, "dg_down", tm=1024, tn=1408)
    dup, ffn_acc = _glu_bwd(sv["up"], dact, sp["ffn_cw"], sp["ffn_cb"], nb, s, "glu_bwd")
    g_up = _mm(sv["x1"], dup, "tn", F32, "wg_up", tm=1024, tn=2816, tk=512)
    g_gate = _mm(sv["x1"], dgp, "tn", F32, "wg_gate", tm=1024, tn=1024, tk=512)
    g_proj = _mm(p_l, dpp, "tn", F32, "wg_proj", tm=256, tn=1024, tk=512)
    t1 = _mm(dgp, wt["w_gate"], "nt", F32, "dg_gate", tm=1024, tn=1024, add=dr2, add_coef=ALPHA)
    dx1 = _mm(dup, wt["w_up"], "nt", F32, "dg_up", tm=1024, tn=1024, tk=1408, add=t1)
    dr1, st1 = _ln_bwd(sv["r1"], sp["ln1"], dx1, "ln1_bwd")
    g_out = _mm(sv["mixcat"], dr1, "tn", F32, "wg_out", tm=1024, tn=1024, tk=512)
    dmix = _mm(dr1, wt["w_out"], "nt", F32, "dg_out", tm=1024, tn=1024)
    du, g_pool_bd, g_pool_scale = _pool_bwd(sv["h"], dmix, wt["pool_bd"], sp["pool_scale"], nb, s, "pool_bwd")
    dz, dxs, dbc, ddt, gx, gb, gv, gn = _ssd_bwd(sv["h"], sv["hstate"], dmix, sp["cwx"], sp["cbx"], sp["cwb"], sp["cbb"],
                                                  sp["ssd_vec"], sp["ssd_norm_w"], nb, s, "ssd_bwd")
    ready = dict(w_out=g_out, ffn_w_up=g_up, ffn_w_down=g_down, ple_w_gate=g_gate, ple_w_proj=g_proj)
    job = comm(ready) if comm is not None else None
    dq, dk, dv, comm_out = _sb_bwd(sv["h"], dmix, nb, s, "sb_bwd" if job is None else "sb_bwd_x%d" % job["n_xfers"], job)
    dh = jnp.concatenate([dbc, du, dq, dk, dv, dz, dxs, ddt], axis=1)
    g_in = _mm(sv["x"], dh, "tn", F32, "wg_in", tm=1024, tn=2816, tk=512)
    tail_job = tail_comm(g_in) if tail_comm is not None else None
    dx = _mm(dh, wt["w_in"], "nt", F32, "dg_in" if tail_job is None else "dg_in_x", tm=1024, tn=1024, tk=1408, add=dr1,
             add_coef=ALPHA, comm=tail_job)
    dx, tail_out = dx if tail_job is not None else (dx, None)
    small = dict(
        pool_w=jnp.stack([g_pool_bd[HEAD_DIM * g:HEAD_DIM * (g + 1), HEAD_DIM * g:HEAD_DIM * (g + 1)] for g in range(4)]),
        pool_scale=g_pool_scale[0],
        ssd_conv_w=jnp.concatenate([gx[0:4], gb[0:4]], axis=1),
        ssd_conv_b=jnp.concatenate([gx[4], gb[4]], axis=0),
        ssd_dt_bias=gv[0, :SSD_HEADS],
        ssd_a_log=gv[1, :SSD_HEADS] * sp["ssd_vec"][1, :SSD_HEADS],
        ssd_d=gv[2, :SSD_HEADS],
        ssd_norm_w=gn[0],
        ln1_g=st1[0], ln1_b=st1[1], ln2_g=st2[0], ln2_b=st2[1],
        ffn_conv_w=_deinterleave(ffn_acc[0:3]),
        ffn_conv_b=_deinterleave(ffn_acc[3]),
    )
    return dx, dict(ready, w_in=g_in), small, comm_out, tail_out


def _layer_params(i, big, rep):
    pool_bd = jnp.zeros((POOL_WIDTH, POOL_WIDTH), F32)
    for g in range(4):
        pool_bd = lax.dynamic_update_slice(pool_bd, rep["pool_w"][i, g], (HEAD_DIM * g, HEAD_DIM * g))
    wt = dict(w_in=big["w_in"], w_out=big["w_out"], w_up=big["ffn_w_up"], w_down=big["ffn_w_down"],
              w_gate=big["ple_w_gate"], w_proj=big["ple_w_proj"], pool_bd=pool_bd.astype(MXU_DTYPE))
    cwx, cwb = _xbc_split(rep["ssd_conv_w"][i])
    cbx, cbb = _xbc_split(rep["ssd_conv_b"][i][None, :])
    vec = jnp.zeros((SUBLANES, LANES), F32)
    vec = vec.at[0, :SSD_HEADS].set(rep["ssd_dt_bias"][i])
    vec = vec.at[1, :SSD_HEADS].set(-jnp.exp(rep["ssd_a_log"][i]))
    vec = vec.at[2, :SSD_HEADS].set(rep["ssd_d"][i])
    sp = dict(pool_scale=rep["pool_scale"][i][None, :], cwx=cwx, cbx=cbx, cwb=cwb, cbb=cbb, ssd_vec=vec,
              ssd_norm_w=rep["ssd_norm_w"][i][None, :],
              ln1=jnp.stack([rep["ln1_g"][i], rep["ln1_b"][i]]), ln2=jnp.stack([rep["ln2_g"][i], rep["ln2_b"][i]]),
              ffn_cw=_interleave(rep["ffn_conv_w"][i]), ffn_cb=_interleave(rep["ffn_conv_b"][i][None, :]))
    return wt, sp


def _run_layers(x, p, target, big_w, rep, fwd_job=None, fwd_done=None, bwd_job=None, bwd_done=None, tail_job=None,
                tail_done=None):
    nb, s, d = x.shape
    t = nb * s
    xf = x.reshape(t, d)
    saved, params = [], []
    for i in range(DEPTH):
        wt, sp = _layer_params(i, big_w[i], rep)
        params.append((wt, sp))
        job = fwd_job(i) if fwd_job is not None else None
        xf, sv, res = _layer_fwd(xf, p[i].reshape(t, PLE_DIM), wt, sp, nb, s, job)
        if job is not None:
            fwd_done(i, res)
        saved.append(sv)
    dy, loss = _loss_grad(xf, target.reshape(t, d), "loss")
    bigs, smalls = [None] * DEPTH, [None] * DEPTH
    for i in reversed(range(DEPTH)):
        wt, sp = params[i]
        job = (lambda ready, i=i: bwd_job(i, bigs, ready)) if bwd_job is not None else None
        dy, bigs[i], smalls[i], res, tail = _layer_bwd(dy, p[i].reshape(t, PLE_DIM), saved[i], wt, sp, nb, s, job,
                                                         tail_job if i == 0 else None)
        if job is not None:
            bwd_done(i, res)
        if tail is not None:
            tail_done(tail)
    return loss, dy.reshape(nb, s, d), bigs, smalls


def _local_step(x, p, target, full, rep):
    return _run_layers(x, p, target, [{n: full[n][i] for n in full} for i in range(DEPTH)], rep)


BIG = ("w_in", "w_out", "ffn_w_up", "ffn_w_down", "ple_w_gate", "ple_w_proj")
SMALL_REPLICATED = ("pool_w", "pool_scale", "ssd_conv_b", "ssd_dt_bias", "ssd_a_log", "ssd_d", "ssd_norm_w",
                    "ln1_g", "ln1_b", "ffn_conv_b", "ln2_g", "ln2_b")
SMALL_SHARDED = ("ssd_conv_w", "ffn_conv_w")
WEIGHTS = ("w_in", "pool_w", "pool_scale", "ssd_conv_w", "ssd_conv_b", "ssd_dt_bias", "ssd_a_log", "ssd_d", "ssd_norm_w",
           "w_out", "ln1_g", "ln1_b", "ffn_w_up", "ffn_conv_w", "ffn_conv_b", "ffn_w_down", "ln2_g", "ln2_b",
           "ple_w_gate", "ple_w_proj")
SUM_BLOCK_BYTES = 3 * 1024 * 1024


def _to_rows(a, cols):
    f = a.reshape(-1)
    pad = (-f.shape[0]) % cols
    if pad:
        f = jnp.concatenate([f, jnp.zeros((pad,), f.dtype)])
    return f.reshape(-1, cols)


def _pack_rows(arrs, cols, row_mult):
    rows = [_to_rows(a, cols) for a in arrs]
    flat = jnp.concatenate(rows, axis=0)
    pad = (-flat.shape[0]) % row_mult
    if pad:
        flat = jnp.concatenate([flat, jnp.zeros((pad, cols), flat.dtype)], axis=0)
    return flat


def _unpack_rows(flat, shapes, cols):
    out, r = [], 0
    for shp in shapes:
        n = 1
        for v in shp:
            n *= v
        nr = -(-n // cols)
        out.append(flat[r:r + nr].reshape(-1)[:n].reshape(shp))
        r += nr
    return out


def kernel(x, p, w_in, pool_w, pool_scale, ssd_conv_w, ssd_conv_b, ssd_dt_bias, ssd_a_log, ssd_d, ssd_norm_w, w_out, ln1_g, ln1_b, ffn_w_up, ffn_conv_w, ffn_conv_b, ffn_w_down, ln2_g, ln2_b, ple_w_gate, ple_w_proj, loss_target, m_w_in, m_pool_w, m_pool_scale, m_ssd_conv_w, m_ssd_conv_b, m_ssd_dt_bias, m_ssd_a_log, m_ssd_d, m_ssd_norm_w, m_w_out, m_ln1_g, m_ln1_b, m_ffn_w_up, m_ffn_conv_w, m_ffn_conv_b, m_ffn_w_down, m_ln2_g, m_ln2_b, m_ple_w_gate, m_ple_w_proj, v_w_in, v_pool_w, v_pool_scale, v_ssd_conv_w, v_ssd_conv_b, v_ssd_dt_bias, v_ssd_a_log, v_ssd_d, v_ssd_norm_w, v_w_out, v_ln1_g, v_ln1_b, v_ffn_w_up, v_ffn_conv_w, v_ffn_conv_b, v_ffn_w_down, v_ln2_g, v_ln2_b, v_ple_w_gate, v_ple_w_proj):
    wts = dict(w_in=w_in, pool_w=pool_w, pool_scale=pool_scale, ssd_conv_w=ssd_conv_w, ssd_conv_b=ssd_conv_b,
               ssd_dt_bias=ssd_dt_bias, ssd_a_log=ssd_a_log, ssd_d=ssd_d, ssd_norm_w=ssd_norm_w, w_out=w_out, ln1_g=ln1_g,
               ln1_b=ln1_b, ffn_w_up=ffn_w_up, ffn_conv_w=ffn_conv_w, ffn_conv_b=ffn_conv_b, ffn_w_down=ffn_w_down,
               ln2_g=ln2_g, ln2_b=ln2_b, ple_w_gate=ple_w_gate, ple_w_proj=ple_w_proj)
    mom_m = dict(w_in=m_w_in, pool_w=m_pool_w, pool_scale=m_pool_scale, ssd_conv_w=m_ssd_conv_w, ssd_conv_b=m_ssd_conv_b,
                 ssd_dt_bias=m_ssd_dt_bias, ssd_a_log=m_ssd_a_log, ssd_d=m_ssd_d, ssd_norm_w=m_ssd_norm_w, w_out=m_w_out,
                 ln1_g=m_ln1_g, ln1_b=m_ln1_b, ffn_w_up=m_ffn_w_up, ffn_conv_w=m_ffn_conv_w, ffn_conv_b=m_ffn_conv_b,
                 ffn_w_down=m_ffn_w_down, ln2_g=m_ln2_g, ln2_b=m_ln2_b, ple_w_gate=m_ple_w_gate, ple_w_proj=m_ple_w_proj)
    mom_v = dict(w_in=v_w_in, pool_w=v_pool_w, pool_scale=v_pool_scale, ssd_conv_w=v_ssd_conv_w, ssd_conv_b=v_ssd_conv_b,
                 ssd_dt_bias=v_ssd_dt_bias, ssd_a_log=v_ssd_a_log, ssd_d=v_ssd_d, ssd_norm_w=v_ssd_norm_w, w_out=v_w_out,
                 ln1_g=v_ln1_g, ln1_b=v_ln1_b, ffn_w_up=v_ffn_w_up, ffn_conv_w=v_ffn_conv_w, ffn_conv_b=v_ffn_conv_b,
                 ffn_w_down=v_ffn_w_down, ln2_g=v_ln2_g, ln2_b=v_ln2_b, ple_w_gate=v_ple_w_gate, ple_w_proj=v_ple_w_proj)
    me = 4 * lax.axis_index("x") + 2 * lax.axis_index("y") + lax.axis_index("c")

    def layer_shards(i):
        sh = {n: wts[n][i].astype(MXU_DTYPE) for n in BIG}
        sh["w_in"] = _h_from_orig(wts["w_in"][i]).astype(MXU_DTYPE)
        return sh

    def gathered_weights(res):
        big = dict(zip(BIG, res[:len(BIG)]))
        big["ffn_w_up"] = _up_to_interleaved(big["ffn_w_up"], "up_to_interleaved")
        return big

    res0 = _gather_call_two_level(_gather_job(layer_shards(0), [wts[n] for n in SMALL_SHARDED]), "gather_layer0")
    big_w = [gathered_weights(res0)] + [None] * (DEPTH - 1)
    rep = {n: wts[n] for n in SMALL_REPLICATED}
    for n, g in zip(SMALL_SHARDED, res0[len(BIG):]):
        rep[n] = jnp.transpose(g, (1, 2, 0, 3)).reshape(g.shape[1], g.shape[2], N_DEV * g.shape[3])

    def fwd_job(i):
        return _gather_job(layer_shards(i + 1)) if i + 1 < DEPTH else None

    def fwd_done(i, res):
        big_w[i + 1] = gathered_weights(res)

    received = [dict() for _ in range(DEPTH)]
    carried = ("w_out", "ffn_w_up", "ffn_w_down", "ple_w_gate", "ple_w_proj")

    def bwd_items(i, bigs, ready):
        items = [(i, n, ready[n]) for n in carried] + ([(i + 1, "w_in", bigs[i + 1]["w_in"])] if i + 1 < DEPTH else [])
        return [(l, n, _up_from_interleaved(g, "up_from_interleaved") if n == "ffn_w_up" else g) for l, n, g in items]

    pending = {}

    def bwd_job(i, bigs, ready):
        pending[i] = bwd_items(i, bigs, ready)
        return _exchange_job([(n, g) for _, n, g in pending[i]])

    def bwd_done(i, res):
        for (l, n, _), r in zip(pending[i], res):
            received[l][n] = r

    def tail_done(res):
        received[0]["w_in"] = res[0]

    loss_loc, grad_x, bigs, smalls = _run_layers(x, p, loss_target, big_w, rep, fwd_job, fwd_done, bwd_job, bwd_done,
                                                 lambda g_in: _exchange_job([("w_in", g_in)]), tail_done)

    grads = {}
    for n in BIG:
        parts = [received[i][n] for i in range(DEPTH)]
        _, rows, cols = parts[0].shape
        tr = next(t for t in (256, 128, 64, 32, 16, 8) if rows % t == 0 and N_DEV * t * cols * 4 <= SUM_BLOCK_BYTES)
        g = _sum8_layers(parts, "sum_" + n, tr)
        grads[n] = _h_to_orig(g) if n == "w_in" else g
    small_names = SMALL_REPLICATED + SMALL_SHARDED
    small_full_shapes = [rep[n].shape for n in small_names]
    small_vec = _pack_rows([jnp.stack([smalls[i][n] for i in range(DEPTH)]) for n in small_names] + [loss_loc[0, :1]],
                           LANES, SUBLANES)
    small_sum = _all_reduce_small(small_vec, "allreduce_small")
    small_out = _unpack_rows(small_sum, small_full_shapes + [(1,)], LANES)
    loss = small_out[-1][0]
    for n, g in zip(small_names, small_out[:-1]):
        if n in SMALL_SHARDED:
            width = wts[n].shape[-1]
            g = lax.dynamic_slice_in_dim(g, me * width, width, axis=g.ndim - 1)
        grads[n] = g

    delta, new_m, new_v = {}, {}, {}
    for n in BIG:
        shp = wts[n].shape
        two_d = lambda a: a.reshape(-1, shp[-1])
        tr = {"w_in": 128, "ffn_w_down": DOWN_SHARD}.get(n, 256)
        d_, m_, v_ = _adamw(two_d(wts[n]), two_d(grads[n]), two_d(mom_m[n]), two_d(mom_v[n]), "adamw_" + n, tr=tr)
        delta[n], new_m[n], new_v[n] = d_.reshape(shp), m_.reshape(shp), v_.reshape(shp)
    packs = [_pack_rows([src[n] for n in small_names], LANES, SUBLANES) for src in (wts, grads, mom_m, mom_v)]
    outs = _adamw(*packs, "adamw_small", tr=packs[0].shape[0])
    shapes = [wts[n].shape for n in small_names]
    for dst, flat in zip((delta, new_m, new_v), outs):
        for n, a in zip(small_names, _unpack_rows(flat, shapes, LANES)):
            dst[n] = a
    return (loss, grad_x, *[grads[n] for n in WEIGHTS], *[delta[n] for n in WEIGHTS],
            *[new_m[n] for n in WEIGHTS], *[new_v[n] for n in WEIGHTS])
```

```python
import functools

import jax
import jax.numpy as jnp
from jax import lax
from jax.experimental import pallas as pl
from jax.experimental.pallas import tpu as pltpu

F32 = jnp.float32
BF16 = jnp.bfloat16
MXU_DTYPE = jnp.bfloat16

D_MODEL = 1024
DEPTH = 4
PLE_DIM = 256
ALPHA = (2 * DEPTH) ** 0.25
LN_EPS = 1e-5
RMS_EPS = 1e-6
HEAD_DIM = 64
POOL_WIDTH = 256
POOL_WINDOWS = (2, 4, 8, 16)
SSD_WIDTH = 384
SSD_HEADS = 6
SSD_STATE = 128
SSD_XBC = 896
SB_WIDTH = 384
IN_COLS = 2694
D_FF = 2816
N_DEV = 8
UP_SHARD = 2 * D_FF // N_DEV
DOWN_SHARD = D_FF // N_DEV

ADAM_LR = 0.001
ADAM_B1 = 0.9
ADAM_B2 = 0.999
ADAM_EPS = 1e-08
ADAM_WD = 0.01
ADAM_STEP = 10

LANES = 128
SUBLANES = 8
VMEM_LIMIT_BYTES = 56 * 1024 * 1024

H_COLS = 2816
H_BC = 0
H_POOL = 512
H_Q = 768
H_K = 1152
H_V = 1536
H_Z = 1920
H_XS = 2304
H_DT = 2688
SSD_CHUNK = 512
QB = 256
GLU_TILE = 256
MASKED_LOG = -1e30

NN = ((1,), (0,))
NT = ((1,), (1,))
TN = ((0,), (0,))


def _dot(a, b, dims=NN):
    return lax.dot_general(a.astype(MXU_DTYPE), b.astype(MXU_DTYPE), (dims, ((), ())), preferred_element_type=F32)


def _dot_exact01(x, m01, dims=NN, x_left=True, terms=3):
    acc = None
    r = x
    for _ in range(terms):
        hi = r.astype(BF16)
        ops = (hi, m01) if x_left else (m01, hi)
        part = lax.dot_general(ops[0], ops[1], (dims, ((), ())), preferred_element_type=F32)
        acc = part if acc is None else acc + part
        r = r - hi.astype(F32)
    return acc


def _sigmoid(v):
    return 1.0 / (1.0 + jnp.exp(-v))


def _silu(v):
    return v * _sigmoid(v)


def _dsilu(v):
    s = _sigmoid(v)
    return s * (1.0 + v * (1.0 - s))


def _softplus(v):
    return jnp.maximum(v, 0.0) + jnp.log(1.0 + jnp.exp(-jnp.abs(v)))


def _params(n_axes, side_effects=False):
    return pltpu.CompilerParams(dimension_semantics=("arbitrary",) * n_axes, vmem_limit_bytes=VMEM_LIMIT_BYTES,
                                has_side_effects=side_effects)


MESH_ID = pl.DeviceIdType.MESH
_ANY = pl.BlockSpec(memory_space=pl.ANY)


def _flip(v, bit):
    return 1 - v if bit else v


def _comm_counts(comm):
    return (0, 0) if comm is None else (len(comm["inputs"]), len(comm["out_shapes"]))


def _comm_call_args(comm):
    if comm is None:
        return [], [], [], []
    n = comm["n_xfers"]
    sems = [pltpu.SemaphoreType.DMA(((N_DEV - 1) * n,)), pltpu.SemaphoreType.DMA(((N_DEV - 1) * n,)),
            pltpu.SemaphoreType.DMA((n,))]
    return list(comm["inputs"]), [_ANY] * len(comm["out_shapes"]), list(comm["out_shapes"]), sems


def _comm_descs(comm, in_refs, tail_refs, with_recvs=True):
    n_out = len(comm["out_shapes"])
    out_refs, (send_sems, recv_sems, local_sems) = tail_refs[:n_out], tail_refs[n_out:n_out + 3]
    xfers = comm["xfers"](in_refs, out_refs)
    n = len(xfers)
    assert n == comm["n_xfers"]
    x, y, c = lax.axis_index("x"), lax.axis_index("y"), lax.axis_index("c")
    me = 4 * x + 2 * y + c
    local = [pltpu.make_async_copy(src_for(me), dst_for(me), local_sems.at[t]) for t, (src_for, dst_for) in enumerate(xfers)]
    sends, recvs = [], []
    for k in range(1, N_DEV):
        pid = (_flip(x, k & 4), _flip(y, k & 2), _flip(c, k & 1))
        peer = 4 * pid[0] + 2 * pid[1] + pid[2]
        for t, (src_for, dst_for) in enumerate(xfers):
            idx = (k - 1) * n + t
            sends.append(pltpu.make_async_remote_copy(
                src_ref=src_for(peer), dst_ref=dst_for(me), send_sem=send_sems.at[idx], recv_sem=recv_sems.at[idx],
                device_id=pid, device_id_type=MESH_ID))
            if with_recvs:
                recvs.append(pltpu.make_async_remote_copy(
                    src_ref=src_for(peer), dst_ref=dst_for(peer), send_sem=send_sems.at[idx], recv_sem=recv_sems.at[idx],
                    device_id=pid, device_id_type=MESH_ID))
    return local, sends, recvs


def _comm_start(descs):
    local, sends, _ = descs
    for cp in local + sends:
        cp.start()


def _comm_wait(descs):
    local, sends, recvs = descs
    for cp in recvs:
        cp.wait_recv()
    for cp in sends:
        cp.wait_send()
    for cp in local:
        cp.wait()


def _comm_hosted(comm, in_refs, tail_refs, grid):
    if comm is None:
        return
    ids = [pl.program_id(a) for a in range(len(grid))]
    first = functools.reduce(jnp.logical_and, [i == 0 for i in ids])
    last = functools.reduce(jnp.logical_and, [i == g - 1 for i, g in zip(ids, grid)])

    @pl.when(first)
    def _():
        _comm_start(_comm_descs(comm, in_refs, tail_refs, with_recvs=False))

    @pl.when(last)
    def _():
        _comm_wait(_comm_descs(comm, in_refs, tail_refs))


def _gather_call_two_level(comm, name):
    n_in, n_out = len(comm["inputs"]), len(comm["out_shapes"])

    def body(*refs):
        in_refs, out_refs = refs[:n_in], refs[n_in:n_in + n_out]
        send_sems, recv_sems, local_sems = refs[n_in + n_out:]
        xfers = comm["xfers"](in_refs, out_refs)
        x, y, c = lax.axis_index("x"), lax.axis_index("y"), lax.axis_index("c")
        pos = lambda px, py, pc: 4 * px + 2 * py + pc
        me, sibling = (x, y, c), (x, y, 1 - c)
        chips = [(1 - x, y), (x, 1 - y), (1 - x, 1 - y)]

        def copy(t, k, block, to, own):
            src_for, dst_for = xfers[t]
            return pltpu.make_async_remote_copy(
                src_ref=src_for(pos(*me)) if own else dst_for(pos(*block)), dst_ref=dst_for(pos(*block)),
                send_sem=send_sems.at[7 * t + k], recv_sem=recv_sems.at[7 * t + k], device_id=to, device_id_type=MESH_ID)

        nt = len(xfers)
        local = [pltpu.make_async_copy(xfers[t][0](pos(*me)), xfers[t][1](pos(*me)), local_sems.at[t]) for t in range(nt)]
        first = [copy(t, 0, me, sibling, True) for t in range(nt)]
        first += [copy(t, 1 + j, me, (*chip, c), True) for t in range(nt) for j, chip in enumerate(chips)]
        for cp in local + first:
            cp.start()
        passed = []
        for j, chip in enumerate(chips):
            for t in range(nt):
                copy(t, 1 + j, (*chip, c), me, False).wait_recv()
                fwd = copy(t, 4 + j, (*chip, c), sibling, False)
                fwd.start()
                passed.append(fwd)
        for t in range(nt):
            copy(t, 0, sibling, me, False).wait_recv()
            for j, chip in enumerate(chips):
                copy(t, 4 + j, (*chip, 1 - c), me, False).wait_recv()
        for cp in first + passed:
            cp.wait_send()
        for cp in local:
            cp.wait()

    n = comm["n_xfers"]
    return pl.pallas_call(
        body, name=name, in_specs=[_ANY] * n_in, out_specs=[_ANY] * n_out, out_shape=list(comm["out_shapes"]),
        scratch_shapes=[pltpu.SemaphoreType.DMA((7 * n,)), pltpu.SemaphoreType.DMA((7 * n,)), pltpu.SemaphoreType.DMA((n,))],
        compiler_params=pltpu.CompilerParams(has_side_effects=True))(*comm["inputs"])


def _rows(ref, j, n):
    return ref.at[pl.ds(pl.multiple_of(j * n, SUBLANES), n), :]


def _gather_job(sh, conv=None):
    conv = list(conv or [])
    sds = jax.ShapeDtypeStruct
    out_shapes = [sds((D_MODEL, H_COLS), MXU_DTYPE), sds((D_MODEL, D_MODEL), MXU_DTYPE), sds((N_DEV, D_MODEL, UP_SHARD), MXU_DTYPE),
                  sds((D_FF, D_MODEL), MXU_DTYPE), sds((D_MODEL, D_MODEL), MXU_DTYPE), sds((PLE_DIM, D_MODEL), MXU_DTYPE)]
    out_shapes += [sds((N_DEV,) + a.shape, a.dtype) for a in conv]

    def xfers(ins, outs):
        whole = lambda a: (lambda j: a)
        r = [(whole(ins[0]), lambda j: _rows(outs[0], j, 128)),
             (whole(ins[1]), lambda j: _rows(outs[1], lax.rem(j + 6, N_DEV), 128)),
             (whole(ins[2]), lambda j: outs[2].at[j]),
             (whole(ins[3]), lambda j: _rows(outs[3], j, DOWN_SHARD)),
             (whole(ins[4]), lambda j: _rows(outs[4], j, 128)),
             (whole(ins[5]), lambda j: outs[5].at[:, pl.ds(pl.multiple_of(j * LANES, LANES), LANES)])]
        for t in range(len(conv)):
            r.append((whole(ins[6 + t]), lambda j, o=outs[6 + t]: o.at[j]))
        return r

    return dict(inputs=[sh[n] for n in BIG] + conv, out_shapes=out_shapes, xfers=xfers, n_xfers=6 + len(conv))


_SHARD_SHAPES = {"w_in": (128, H_COLS), "w_out": (128, D_MODEL), "ffn_w_up": (D_MODEL, UP_SHARD), "ffn_w_down": (DOWN_SHARD, D_MODEL),
                 "ple_w_gate": (128, D_MODEL), "ple_w_proj": (PLE_DIM, LANES)}


def _exchange_job(items):
    def source(name, ref):
        if name in ("w_in", "ple_w_gate"):
            return lambda j: _rows(ref, j, 128)
        if name == "w_out":
            return lambda j: _rows(ref, lax.rem(j + 6, N_DEV), 128)
        if name == "ffn_w_up":
            return lambda j: ref.at[j]
        if name == "ffn_w_down":
            return lambda j: _rows(ref, j, DOWN_SHARD)
        assert name == "ple_w_proj"
        return lambda j: ref.at[:, pl.ds(pl.multiple_of(j * LANES, LANES), LANES)]

    def xfers(ins, outs):
        return [(source(name, i), lambda j, o=o: o.at[j]) for (name, _), i, o in zip(items, ins, outs)]

    return dict(inputs=[g for _, g in items], xfers=xfers, n_xfers=len(items),
                out_shapes=[jax.ShapeDtypeStruct((N_DEV,) + _SHARD_SHAPES[name], F32) for name, _ in items])


def _pick(n, pref):
    if n <= pref:
        return n
    for t in range(pref - pref % LANES, 0, -LANES):
        if n % t == 0:
            return t
    raise ValueError((n, pref))


def _mm(a, b, mode, out_dtype, name, tm=512, tn=512, tk=1024, add=None, add_coef=1.0, comm=None):
    n_in, n_out = _comm_counts(comm)
    if mode == "nn":
        (m, k), (k2, n) = a.shape, b.shape
    elif mode == "nt":
        (m, k), (n, k2) = a.shape, b.shape
    else:
        (k, m), (k2, n) = a.shape, b.shape
    assert k == k2, (a.shape, b.shape, mode)
    tm, tn, tk = _pick(m, tm), _pick(n, tn), _pick(k, tk)
    nk = k // tk
    dims = {"nn": NN, "nt": NT, "tn": TN}[mode]

    def body(*refs):
        a_ref, b_ref = refs[:2]
        n_add = int(add is not None)
        add_ref = refs[2] if n_add else None
        o_ref = refs[2 + n_add + n_in]
        tail = refs[3 + n_add + n_in:]
        if comm is not None:
            _comm_hosted(comm, refs[2 + n_add:2 + n_add + n_in], tail[:n_out] + tail[n_out + int(nk > 1):],
                         (m // tm, n // tn, nk))

        def finish(r):
            if add_ref is not None:
                r = r + add_coef * add_ref[...]
            o_ref[...] = r.astype(out_dtype)

        if nk == 1:
            finish(_dot(a_ref[...], b_ref[...], dims))
            return
        acc_ref = tail[n_out]
        kk = pl.program_id(2)

        @pl.when(kk == 0)
        def _():
            acc_ref[...] = jnp.zeros_like(acc_ref)

        acc_ref[...] += _dot(a_ref[...], b_ref[...], dims)

        @pl.when(kk == nk - 1)
        def _():
            finish(acc_ref[...])

    if mode == "tn":
        a_spec = pl.BlockSpec((tk, tm), lambda i, j, kk: (kk, i))
    else:
        a_spec = pl.BlockSpec((tm, tk), lambda i, j, kk: (i, kk))
    if mode == "nt":
        b_spec = pl.BlockSpec((tn, tk), lambda i, j, kk: (j, kk))
    else:
        b_spec = pl.BlockSpec((tk, tn), lambda i, j, kk: (kk, j))
    o_spec = pl.BlockSpec((tm, tn), lambda i, j, kk: (i, j))
    in_specs = [a_spec, b_spec] + ([o_spec] if add is not None else [])
    args = (a, b) + ((add,) if add is not None else ())
    c_in, c_specs, c_shapes, c_scratch = _comm_call_args(comm)
    res = pl.pallas_call(
        body, name=name, grid=(m // tm, n // tn, nk), in_specs=in_specs + [_ANY] * n_in, out_specs=[o_spec] + c_specs,
        out_shape=[jax.ShapeDtypeStruct((m, n), out_dtype)] + c_shapes,
        scratch_shapes=([pltpu.VMEM((tm, tn), F32)] if nk > 1 else []) + c_scratch,
        compiler_params=_params(3, comm is not None),
    )(*args, *c_in)
    return res[0] if comm is None else (res[0], res[1:])


def _mm_ln(a, b, x, gb, name, gp=None, pp=None, tm=512, tk=1024):
    (m, k), (k2, d) = a.shape, b.shape
    assert k == k2 and x.shape == (m, d)
    tm, tk = _pick(m, tm), _pick(k, tk)
    nk = k // tk
    with_ple = gp is not None

    def body(*refs):
        a_ref, b_ref, x_ref = refs[:3]
        gp_ref, pp_ref = refs[3:5] if with_ple else (None, None)
        gb_ref, y_ref, r_ref, acc_ref = refs[-4:]
        kk = pl.program_id(1)

        @pl.when(kk == 0)
        def _():
            acc_ref[...] = jnp.zeros_like(acc_ref)

        acc_ref[...] += _dot(a_ref[...], b_ref[...])

        @pl.when(kk == nk - 1)
        def _():
            r = ALPHA * x_ref[...] + acc_ref[...]
            if with_ple:
                r = r + _sigmoid(gp_ref[...]) * pp_ref[...]
            mu = jnp.mean(r, axis=1, keepdims=True)
            xc = r - mu
            var = jnp.mean(xc * xc, axis=1, keepdims=True)
            y_ref[...] = xc * lax.rsqrt(var + LN_EPS) * gb_ref[0:1, :] + gb_ref[1:2, :]
            r_ref[...] = r

    row = pl.BlockSpec((tm, d), lambda i, kk: (i, 0))
    in_specs = [pl.BlockSpec((tm, tk), lambda i, kk: (i, kk)), pl.BlockSpec((tk, d), lambda i, kk: (kk, 0)), row]
    in_specs += ([row, row] if with_ple else []) + [pl.BlockSpec((2, d), lambda i, kk: (0, 0))]
    args = (a, b, x) + ((gp, pp) if with_ple else ()) + (gb,)
    return pl.pallas_call(
        body, name=name, grid=(m // tm, nk), in_specs=in_specs, out_specs=[row, row],
        out_shape=[jax.ShapeDtypeStruct((m, d), F32)] * 2, scratch_shapes=[pltpu.VMEM((tm, d), F32)],
        compiler_params=_params(2),
    )(*args)


def _mm_ln_bwd(a, b, add, r, gb, name, tm=512, tk=1408):
    (m, k), (d, k2) = a.shape, b.shape
    assert k == k2 and add.shape == (m, d) and r.shape == (m, d)
    tm, tk = _pick(m, tm), _pick(k, tk)
    nk = k // tk

    def body(a_ref, b_ref, add_ref, r_ref, gb_ref, dr_ref, st_ref, acc_ref):
        i, kk = pl.program_id(0), pl.program_id(1)

        @pl.when(jnp.logical_and(i == 0, kk == 0))
        def _():
            st_ref[...] = jnp.zeros_like(st_ref)

        @pl.when(kk == 0)
        def _():
            acc_ref[...] = jnp.zeros_like(acc_ref)

        acc_ref[...] += _dot(a_ref[...], b_ref[...], NT)

        @pl.when(kk == nk - 1)
        def _():
            dy_v = acc_ref[...] + add_ref[...]
            rv = r_ref[...]
            mu = jnp.mean(rv, axis=1, keepdims=True)
            xc = rv - mu
            var = jnp.mean(xc * xc, axis=1, keepdims=True)
            rstd = lax.rsqrt(var + LN_EPS)
            xhat = xc * rstd
            dxh = dy_v * gb_ref[0:1, :]
            m1 = jnp.mean(dxh, axis=1, keepdims=True)
            m2 = jnp.mean(dxh * xhat, axis=1, keepdims=True)
            dr_ref[...] = rstd * (dxh - m1 - xhat * m2)
            rid = lax.broadcasted_iota(jnp.int32, (2, d), 0)
            st_ref[...] += jnp.where(rid == 0, jnp.sum(dy_v * xhat, axis=0, keepdims=True),
                                     jnp.sum(dy_v, axis=0, keepdims=True))

    row = pl.BlockSpec((tm, d), lambda i, kk: (i, 0))
    vec = pl.BlockSpec((2, d), lambda i, kk: (0, 0))
    return pl.pallas_call(
        body, name=name, grid=(m // tm, nk),
        in_specs=[pl.BlockSpec((tm, tk), lambda i, kk: (i, kk)), pl.BlockSpec((d, tk), lambda i, kk: (0, kk)), row, row, vec],
        out_specs=[row, vec], out_shape=[jax.ShapeDtypeStruct((m, d), F32), jax.ShapeDtypeStruct((2, d), F32)],
        scratch_shapes=[pltpu.VMEM((tm, d), F32)], compiler_params=_params(2),
    )(a, b, add, r, gb)


def _ln_bwd(r, gb, dy, name, gp=None, pp=None, tr=512):
    t, d = r.shape
    tr = _pick(t, tr)
    with_ple = gp is not None

    def body(*refs):
        if with_ple:
            r_ref, dy_ref, gp_ref, pp_ref, gb_ref, dr_ref, dgp_ref, dpp_ref, st_ref = refs
        else:
            r_ref, dy_ref, gb_ref, dr_ref, st_ref = refs
        i = pl.program_id(0)

        @pl.when(i == 0)
        def _():
            st_ref[...] = jnp.zeros_like(st_ref)

        rv = r_ref[...]
        dy_v = dy_ref[...]
        mu = jnp.mean(rv, axis=1, keepdims=True)
        xc = rv - mu
        var = jnp.mean(xc * xc, axis=1, keepdims=True)
        rstd = lax.rsqrt(var + LN_EPS)
        xhat = xc * rstd
        dxh = dy_v * gb_ref[0:1, :]
        m1 = jnp.mean(dxh, axis=1, keepdims=True)
        m2 = jnp.mean(dxh * xhat, axis=1, keepdims=True)
        dr = rstd * (dxh - m1 - xhat * m2)
        dr_ref[...] = dr
        rid = lax.broadcasted_iota(jnp.int32, (2, d), 0)
        dg = jnp.sum(dy_v * xhat, axis=0, keepdims=True)
        db = jnp.sum(dy_v, axis=0, keepdims=True)
        st_ref[...] += jnp.where(rid == 0, dg, db)
        if with_ple:
            sg = _sigmoid(gp_ref[...])
            ppv = pp_ref[...]
            dgp_ref[...] = (dr * ppv * sg * (1.0 - sg)).astype(dgp_ref.dtype)
            dpp_ref[...] = (dr * sg).astype(dpp_ref.dtype)

    row = pl.BlockSpec((tr, d), lambda i: (i, 0))
    vec = pl.BlockSpec((2, d), lambda i: (0, 0))
    if with_ple:
        in_specs, args = [row] * 4 + [vec], (r, dy, gp, pp, gb)
        out_specs = [row, row, row, vec]
        out_shape = [jax.ShapeDtypeStruct((t, d), F32), jax.ShapeDtypeStruct((t, d), MXU_DTYPE),
                     jax.ShapeDtypeStruct((t, d), MXU_DTYPE), jax.ShapeDtypeStruct((2, d), F32)]
    else:
        in_specs, args = [row] * 2 + [vec], (r, dy, gb)
        out_specs = [row, vec]
        out_shape = [jax.ShapeDtypeStruct((t, d), F32), jax.ShapeDtypeStruct((2, d), F32)]
    return pl.pallas_call(body, name=name, grid=(t // tr,), in_specs=in_specs, out_specs=out_specs,
                          out_shape=out_shape, compiler_params=_params(1))(*args)


def _loss_grad(y, target, name, tr=512):
    t, d = y.shape
    tr = _pick(t, tr)

    def body(y_ref, t_ref, dy_ref, l_ref):
        i = pl.program_id(0)

        @pl.when(i == 0)
        def _():
            l_ref[...] = jnp.zeros_like(l_ref)

        e = y_ref[...] - t_ref[...]
        dy_ref[...] = e * (1.0 / d)
        per_tok = jnp.mean(e * e, axis=1, keepdims=True)
        l_ref[...] += 0.5 * jnp.sum(per_tok, axis=0, keepdims=True)

    row = pl.BlockSpec((tr, d), lambda i: (i, 0))
    acc = pl.BlockSpec((SUBLANES, LANES), lambda i: (0, 0))
    return pl.pallas_call(body, name=name, grid=(t // tr,), in_specs=[row, row], out_specs=[row, acc],
                          out_shape=[jax.ShapeDtypeStruct((t, d), F32), jax.ShapeDtypeStruct((SUBLANES, LANES), F32)],
                          compiler_params=_params(1))(y, target)


def _shift_down(v, k, row):
    return jnp.where(row >= k, pltpu.roll(v, k, 0), 0.0)


def _shift_up(v, k, row):
    n = v.shape[0]
    return jnp.where(row < n - k, pltpu.roll(v, n - k, 0), 0.0)


def _pool_window(lane):
    grp = lane // HEAD_DIM
    return jnp.where(grp == 0, POOL_WINDOWS[0], jnp.where(grp == 1, POOL_WINDOWS[1],
                     jnp.where(grp == 2, POOL_WINDOWS[2], POOL_WINDOWS[3])))


def _pool_select(lane, s2, s4, s8, s16):
    grp = lane // HEAD_DIM
    return jnp.where(grp == 0, s2, jnp.where(grp == 1, s4, jnp.where(grp == 2, s8, s16)))


def _pooled(u, row, lane):
    s2 = u + _shift_down(u, 1, row)
    s4 = s2 + _shift_down(s2, 2, row)
    s8 = s4 + _shift_down(s4, 4, row)
    s16 = s8 + _shift_down(s8, 8, row)
    cnt = jnp.minimum(row + 1, _pool_window(lane)).astype(F32)
    return _pool_select(lane, s2, s4, s8, s16) / cnt - u, cnt


def _pool_fwd(h, wbd, scale, nb, s, name):
    def body(u_ref, w_ref, sc_ref, o_ref):
        u = u_ref[...]
        row = lax.broadcasted_iota(jnp.int32, u.shape, 0)
        lane = lax.broadcasted_iota(jnp.int32, u.shape, 1)
        pooled, _ = _pooled(u, row, lane)
        o_ref[...] = (_dot(pooled, w_ref[...]) * sc_ref[...]).astype(o_ref.dtype)

    wb = POOL_WIDTH
    return pl.pallas_call(
        body, name=name, grid=(nb,),
        in_specs=[pl.BlockSpec((s, wb), lambda b: (b, H_POOL // wb)), pl.BlockSpec((wb, wb), lambda b: (0, 0)),
                  pl.BlockSpec((1, wb), lambda b: (0, 0))],
        out_specs=pl.BlockSpec((s, wb), lambda b: (b, 0)),
        out_shape=jax.ShapeDtypeStruct((nb * s, wb), MXU_DTYPE), compiler_params=_params(1),
    )(h, wbd, scale)


def _pool_bwd(h, dmix, wbd, scale, nb, s, name):
    wb = POOL_WIDTH

    def body(u_ref, do_ref, w_ref, sc_ref, du_ref, dw_ref, ds_ref):
        b = pl.program_id(0)

        @pl.when(b == 0)
        def _():
            dw_ref[...] = jnp.zeros_like(dw_ref)
            ds_ref[...] = jnp.zeros_like(ds_ref)

        u = u_ref[...]
        row = lax.broadcasted_iota(jnp.int32, u.shape, 0)
        lane = lax.broadcasted_iota(jnp.int32, u.shape, 1)
        pooled, cnt = _pooled(u, row, lane)
        mixed = _dot(pooled, w_ref[...])
        do = do_ref[...]
        ds_ref[...] += jnp.sum(do * mixed, axis=0, keepdims=True)
        dm = do * sc_ref[...]
        dw_ref[...] += _dot(pooled, dm, TN)
        dpool = _dot(dm, w_ref[...], NT)
        qv = dpool / cnt
        f2 = qv + _shift_up(qv, 1, row)
        f4 = f2 + _shift_up(f2, 2, row)
        f8 = f4 + _shift_up(f4, 4, row)
        f16 = f8 + _shift_up(f8, 8, row)
        du_ref[...] = (_pool_select(lane, f2, f4, f8, f16) - dpool).astype(du_ref.dtype)

    return pl.pallas_call(
        body, name=name, grid=(nb,),
        in_specs=[pl.BlockSpec((s, wb), lambda b: (b, H_POOL // wb)), pl.BlockSpec((s, wb), lambda b: (b, 3)),
                  pl.BlockSpec((wb, wb), lambda b: (0, 0)), pl.BlockSpec((1, wb), lambda b: (0, 0))],
        out_specs=[pl.BlockSpec((s, wb), lambda b: (b, 0)), pl.BlockSpec((wb, wb), lambda b: (0, 0)),
                   pl.BlockSpec((1, wb), lambda b: (0, 0))],
        out_shape=[jax.ShapeDtypeStruct((nb * s, wb), MXU_DTYPE), jax.ShapeDtypeStruct((wb, wb), F32),
                   jax.ShapeDtypeStruct((1, wb), F32)],
        compiler_params=_params(1),
    )(h, dmix, wbd, scale)


def _glu_conv(x, w_ref, b_ref, row):
    return (b_ref[...] + w_ref[2:3, :] * x + w_ref[1:2, :] * _shift_down(x, 1, row)
            + w_ref[0:1, :] * _shift_down(x, 2, row))


def _glu_fwd(up, cw, cb, nb, s, name):
    wt = 2 * GLU_TILE
    nt = up.shape[1] // wt

    def body(u_ref, w_ref, b_ref, o_ref):
        x = u_ref[...]
        row = lax.broadcasted_iota(jnp.int32, x.shape, 0)
        c = _glu_conv(x, w_ref, b_ref, row)
        o_ref[...] = (_silu(c[:, :GLU_TILE]) * c[:, GLU_TILE:]).astype(o_ref.dtype)

    return pl.pallas_call(
        body, name=name, grid=(nt, nb),
        in_specs=[pl.BlockSpec((s, wt), lambda j, b: (b, j)), pl.BlockSpec((3, wt), lambda j, b: (0, j)),
                  pl.BlockSpec((1, wt), lambda j, b: (0, j))],
        out_specs=pl.BlockSpec((s, GLU_TILE), lambda j, b: (b, j)),
        out_shape=jax.ShapeDtypeStruct((nb * s, nt * GLU_TILE), MXU_DTYPE), compiler_params=_params(2),
    )(up, cw, cb)


def _glu_bwd(up, dact, cw, cb, nb, s, name):
    wt = 2 * GLU_TILE
    nt = up.shape[1] // wt

    def body(u_ref, da_ref, w_ref, b_ref, du_ref, acc_ref):
        b = pl.program_id(1)

        @pl.when(b == 0)
        def _():
            acc_ref[...] = jnp.zeros_like(acc_ref)

        x = u_ref[...]
        row = lax.broadcasted_iota(jnp.int32, x.shape, 0)
        x1 = _shift_down(x, 1, row)
        x2 = _shift_down(x, 2, row)
        c = b_ref[...] + w_ref[2:3, :] * x + w_ref[1:2, :] * x1 + w_ref[0:1, :] * x2
        gate, val = c[:, :GLU_TILE], c[:, GLU_TILE:]
        da = da_ref[...]
        dc = jnp.concatenate([da * val * _dsilu(gate), da * _silu(gate)], axis=1)
        dx = (w_ref[2:3, :] * dc + w_ref[1:2, :] * _shift_up(dc, 1, row) + w_ref[0:1, :] * _shift_up(dc, 2, row))
        du_ref[...] = dx.astype(du_ref.dtype)
        rid = lax.broadcasted_iota(jnp.int32, (SUBLANES, wt), 0)
        dw0 = jnp.sum(dc * x2, axis=0, keepdims=True)
        dw1 = jnp.sum(dc * x1, axis=0, keepdims=True)
        dw2 = jnp.sum(dc * x, axis=0, keepdims=True)
        db = jnp.sum(dc, axis=0, keepdims=True)
        acc_ref[...] += (jnp.where(rid == 0, dw0, 0.0) + jnp.where(rid == 1, dw1, 0.0)
                         + jnp.where(rid == 2, dw2, 0.0) + jnp.where(rid == 3, db, 0.0))

    return pl.pallas_call(
        body, name=name, grid=(nt, nb),
        in_specs=[pl.BlockSpec((s, wt), lambda j, b: (b, j)), pl.BlockSpec((s, GLU_TILE), lambda j, b: (b, j)),
                  pl.BlockSpec((3, wt), lambda j, b: (0, j)), pl.BlockSpec((1, wt), lambda j, b: (0, j))],
        out_specs=[pl.BlockSpec((s, wt), lambda j, b: (b, j)), pl.BlockSpec((SUBLANES, wt), lambda j, b: (0, j))],
        out_shape=[jax.ShapeDtypeStruct((nb * s, nt * wt), MXU_DTYPE), jax.ShapeDtypeStruct((SUBLANES, nt * wt), F32)],
        compiler_params=_params(2),
    )(up, dact, cw, cb)


def _sb_constants():
    row = lax.broadcasted_iota(jnp.int32, (QB, QB), 0)
    col = lax.broadcasted_iota(jnp.int32, (QB, QB), 1)
    return jnp.stack([row > col, row < col, col < row]).astype(BF16)


_SB_CONST_SPEC = pl.BlockSpec((3, QB, QB), lambda b, p, i: (0, 0, 0))


def _sb_fwd(h, nb, s, name, comm=None):
    nq = s // QB
    scale = HEAD_DIM ** -0.5
    n_in, n_out = _comm_counts(comm)

    def body(q_ref, k_ref, v_ref, tri_ref, *rest):
        o_ref = rest[n_in]
        i = pl.program_id(2)
        _comm_hosted(comm, rest[:n_in], rest[n_in + 1:], (nb, 3, nq))
        sls = [slice(hd * HEAD_DIM, (hd + 1) * HEAD_DIM) for hd in range(2)]
        qs = [(q_ref[:, sl] * scale).astype(MXU_DTYPE) for sl in sls]

        def scores(hd, j, diagonal=False):
            r0 = pl.multiple_of(j * QB, QB)
            z = _dot(qs[hd], k_ref[pl.ds(r0, QB), sls[hd]], NT)
            ln = -_softplus(z)
            ls = z + ln
            if diagonal:
                low = tri_ref[2] > 0
                ln = jnp.where(low, ln, 0.0)
                ls = jnp.where(low, ls, MASKED_LOG)
            return ls, _dot_exact01(ln, tri_ref[0], terms=2), jnp.sum(ln, axis=1, keepdims=True)

        def output(hd, j, ls, tl, ct):
            r0 = pl.multiple_of(j * QB, QB)
            return _dot(jnp.exp(ls + tl + ct), v_ref[pl.ds(r0, QB), sls[hd]])

        def group(blocks, carry, diagonal_first=False):
            sc = [[scores(hd, j, diagonal_first and n == 0) for n, j in enumerate(blocks)] for hd in range(2)]
            out = []
            for hd in range(2):
                a, c = carry[hd]
                for (ls, tl, sm), j in zip(sc[hd], blocks):
                    a = a + output(hd, j, ls, tl, c)
                    c = c + sm
                out.append((a, c))
            return tuple(out)

        start = (jnp.zeros((QB, HEAD_DIM), F32), jnp.zeros((QB, 1), F32))
        below = jnp.minimum(i, 1)
        left = i - below
        carry = lax.fori_loop(0, below, lambda t, c: group([i, i - 1], c, True), (start, start))
        carry = lax.fori_loop(0, 1 - below, lambda t, c: group([i], c, True), carry)
        carry = lax.fori_loop(0, left // 2, lambda t, c: group([left - 1 - 2 * t, left - 2 - 2 * t], c), carry)
        carry = lax.fori_loop(0, left % 2, lambda t, c: group([0], c), carry)
        o_ref[:, sls[0]] = carry[0][0].astype(o_ref.dtype)
        o_ref[:, sls[1]] = carry[1][0].astype(o_ref.dtype)

    qspec = lambda off: pl.BlockSpec((QB, LANES), lambda b, p, i: (b * nq + i, off // LANES + p))
    kvspec = lambda off: pl.BlockSpec((s, LANES), lambda b, p, i: (b, off // LANES + p))
    c_in, c_specs, c_shapes, c_scratch = _comm_call_args(comm)
    res = pl.pallas_call(
        body, name=name, grid=(nb, 3, nq), in_specs=[qspec(H_Q), kvspec(H_K), kvspec(H_V), _SB_CONST_SPEC] + [_ANY] * n_in,
        out_specs=[pl.BlockSpec((QB, LANES), lambda b, p, i: (b * nq + i, p))] + c_specs,
        out_shape=[jax.ShapeDtypeStruct((nb * s, SB_WIDTH), MXU_DTYPE)] + c_shapes, scratch_shapes=c_scratch,
        compiler_params=_params(3, comm is not None),
    )(h, h, h, _sb_constants(), *c_in)
    return res[0], res[1:]


def _sb_bwd(h, dmix, nb, s, name, comm=None):
    nq = s // QB
    scale = HEAD_DIM ** -0.5
    n_in, n_out = _comm_counts(comm)

    def body(q_ref, k_ref, v_ref, do_ref, tri_ref, *rest):
        dq_ref, dk_out, dv_out = rest[n_in:n_in + 3]
        p_buf, ls_buf, dk_ref, dv_ref = rest[n_in + 3 + n_out:n_in + 7 + n_out]
        i = pl.program_id(2)
        _comm_hosted(comm, rest[:n_in], rest[n_in + 3:n_in + 3 + n_out] + rest[n_in + 7 + n_out:], (nb, 3, nq))

        @pl.when(i == 0)
        def _():
            dk_ref[...] = jnp.zeros_like(dk_ref)
            dv_ref[...] = jnp.zeros_like(dv_ref)

        sls = [slice(hd * HEAD_DIM, (hd + 1) * HEAD_DIM) for hd in range(2)]
        q_raw = [q_ref[:, sl].astype(MXU_DTYPE) for sl in sls]
        qs = [(q_ref[:, sl] * scale).astype(MXU_DTYPE) for sl in sls]
        do = [do_ref[:, sl].astype(MXU_DTYPE) for sl in sls]

        def down_scores(hd, j, diagonal):
            r0 = pl.multiple_of(j * QB, QB)
            z = _dot(qs[hd], k_ref[pl.ds(r0, QB), sls[hd]], NT)
            ln = -_softplus(z)
            ls = z + ln
            if diagonal:
                low = tri_ref[2] > 0
                ln = jnp.where(low, ln, 0.0)
                ls = jnp.where(low, ls, MASKED_LOG)
            da = _dot(do[hd], v_ref[pl.ds(r0, QB), sls[hd]], NT)
            return ls, _dot_exact01(ln, tri_ref[0], terms=2), jnp.sum(ln, axis=1, keepdims=True), da

        def down_group(blocks, carry, diagonal_first=False):
            sc = [[down_scores(hd, j, diagonal_first and n == 0) for n, j in enumerate(blocks)] for hd in range(2)]
            out = []
            for hd in range(2):
                ct = carry[hd]
                for (ls, tl, sm, da), j in zip(sc[hd], blocks):
                    r0 = pl.multiple_of(j * QB, QB)
                    a = jnp.exp(ls + tl + ct)
                    p_buf[hd, j] = da * a
                    ls_buf[hd, j] = ls
                    dv_ref[pl.ds(r0, QB), sls[hd]] += _dot(a, do[hd], TN)
                    ct = ct + sm
                out.append(ct)
            return tuple(out)

        zero = jnp.zeros((QB, 1), F32)
        below = jnp.minimum(i, 1)
        left = i - below
        carry = lax.fori_loop(0, below, lambda t, c: down_group([i, i - 1], c, True), (zero, zero))
        carry = lax.fori_loop(0, 1 - below, lambda t, c: down_group([i], c, True), carry)
        carry = lax.fori_loop(0, left // 2, lambda t, c: down_group([left - 1 - 2 * t, left - 2 - 2 * t], c), carry)
        lax.fori_loop(0, left % 2, lambda t, c: down_group([0], c), carry)

        def up_group(blocks, carry):
            ld = []
            for hd in range(2):
                ld.append([])
                for j in blocks:
                    pj = p_buf[hd, j]
                    ld[hd].append((pj, jnp.exp(ls_buf[hd, j]), _dot_exact01(pj, tri_ref[1]), jnp.sum(pj, axis=1, keepdims=True)))
            out = []
            for hd in range(2):
                dq, cp = carry[hd]
                for (pj, sg, cm, sm), j in zip(ld[hd], blocks):
                    r0 = pl.multiple_of(j * QB, QB)
                    dz = (pj * (1.0 - sg) - (cp + cm) * sg) * scale
                    dk_ref[pl.ds(r0, QB), sls[hd]] += _dot(dz, q_raw[hd], TN)
                    dq = dq + _dot(dz, k_ref[pl.ds(r0, QB), sls[hd]])
                    cp = cp + sm
                out.append((dq, cp))
            return tuple(out)

        start = (jnp.zeros((QB, HEAD_DIM), F32), zero)
        odd = (i + 1) % 2
        carry = lax.fori_loop(0, odd, lambda t, c: up_group([0], c), (start, start))
        carry = lax.fori_loop(0, (i + 1) // 2, lambda t, c: up_group([odd + 2 * t, odd + 2 * t + 1], c), carry)
        dq_ref[:, sls[0]] = carry[0][0].astype(dq_ref.dtype)
        dq_ref[:, sls[1]] = carry[1][0].astype(dq_ref.dtype)

        @pl.when(i == nq - 1)
        def _():
            dk_out[...] = dk_ref[...].astype(dk_out.dtype)
            dv_out[...] = dv_ref[...].astype(dv_out.dtype)

    qspec = lambda off: pl.BlockSpec((QB, LANES), lambda b, p, i: (b * nq + i, off // LANES + p))
    kvspec = lambda off: pl.BlockSpec((s, LANES), lambda b, p, i: (b, off // LANES + p))
    blk_out = pl.BlockSpec((QB, LANES), lambda b, p, i: (b * nq + i, p))
    seq_out = pl.BlockSpec((s, LANES), lambda b, p, i: (b, p))
    shp = jax.ShapeDtypeStruct((nb * s, SB_WIDTH), MXU_DTYPE)
    c_in, c_specs, c_shapes, c_scratch = _comm_call_args(comm)
    res = pl.pallas_call(
        body, name=name, grid=(nb, 3, nq),
        in_specs=[qspec(H_Q), kvspec(H_K), kvspec(H_V), pl.BlockSpec((QB, LANES), lambda b, p, i: (b * nq + i, 3 + p)),
                  _SB_CONST_SPEC] + [_ANY] * n_in,
        out_specs=[blk_out, seq_out, seq_out] + c_specs, out_shape=[shp, shp, shp] + c_shapes,
        scratch_shapes=[pltpu.VMEM((2, nq, QB, QB), F32), pltpu.VMEM((2, nq, QB, QB), F32),
                        pltpu.VMEM((s, LANES), F32), pltpu.VMEM((s, LANES), F32)] + c_scratch,
        compiler_params=_params(3, comm is not None),
    )(h, h, h, dmix, _sb_constants(), *c_in)
    return res[0], res[1], res[2], res[3:]


def _ssd_conv(cur_ref, halo_ref, w_ref, b_ref, ext_ref, first):
    n = SSD_CHUNK
    cur = cur_ref[...]
    ext_ref[0:SUBLANES, :] = jnp.where(first, 0.0, halo_ref[...])
    ext_ref[SUBLANES:SUBLANES + n, :] = cur
    return (b_ref[...] + w_ref[3:4, :] * cur + w_ref[2:3, :] * ext_ref[pl.ds(SUBLANES - 1, n), :]
            + w_ref[1:2, :] * ext_ref[pl.ds(SUBLANES - 2, n), :] + w_ref[0:1, :] * ext_ref[pl.ds(SUBLANES - 3, n), :])


def _ssd_tri():
    row = lax.broadcasted_iota(jnp.int32, (SSD_CHUNK, SSD_CHUNK), 0)
    col = lax.broadcasted_iota(jnp.int32, (SSD_CHUNK, SSD_CHUNK), 1)
    return row, col


def _ssd_specs(nc, rev):
    n = SSD_CHUNK
    hb = n // SUBLANES

    def cidx(c):
        return (nc - 1 - c) if rev else c

    def blk(width, off):
        return pl.BlockSpec((n, width), lambda b, c: (b * nc + cidx(c), off // width))

    def halo(width, off):
        return pl.BlockSpec((SUBLANES, width), lambda b, c: (jnp.maximum((b * nc + cidx(c)) * hb - 1, 0), off // width))

    def full(shape):
        return pl.BlockSpec(shape, lambda b, c: (0,) * len(shape))

    return cidx, blk, halo, full


def _ssd_core_fwd(x, bc, dt, acum, acum_t, a_row, d_row, h_prev_ref, tri):
    n = SSD_CHUNK
    heads = []
    for g in range(2):
        bm = bc[:, g * SSD_STATE:(g + 1) * SSD_STATE]
        cm = bc[:, 2 * SSD_STATE + g * SSD_STATE: 2 * SSD_STATE + (g + 1) * SSD_STATE]
        gmat = _dot(cm, bm, NT)
        for r in range(3):
            hh = g * 3 + r
            hp = h_prev_ref[hh * HEAD_DIM:(hh + 1) * HEAD_DIM, :]
            heads.append(dict(g=g, hh=hh, bm=bm, cm=cm, gmat=gmat, hp=hp, cmh=_dot(cm, hp, NT)))
    for hd in heads:
        hh = hd["hh"]
        ac = acum[:, hh:hh + 1]
        ar = acum_t[hh:hh + 1, :]
        hd["dec"] = jnp.where(tri, jnp.exp(jnp.minimum(ac - ar, 0.0)), 0.0)
        hd["xh"] = x[:, hh * HEAD_DIM:(hh + 1) * HEAD_DIM]
        hd["dth"] = dt[:, hh:hh + 1]
        hd["xdt"] = hd["xh"] * hd["dth"]
        hd["ea"] = jnp.exp(ac)
        hd["m"] = hd["gmat"] * hd["dec"]
        hd["al"] = acum[n - 1:n, hh:hh + 1]
        hd["w"] = jnp.exp(hd["al"] - ac)
    for hd in heads:
        hd["yd"] = _dot(hd["m"], hd["xdt"])
    for hd in heads:
        hd["yo"] = hd["ea"] * hd["cmh"]
        hd["y"] = hd["yd"] + hd["yo"] + d_row[:, hd["hh"]:hd["hh"] + 1] * hd["xh"]
    return heads


def _ssd_prep(xs_ref, xsh_ref, bc_ref, bch_ref, dt_ref, cwx_ref, cbx_ref, cwb_ref, cbb_ref, vec_ref, xe_ref, be_ref, first):
    pre_x = _ssd_conv(xs_ref, xsh_ref, cwx_ref, cbx_ref, xe_ref, first)
    pre_bc = _ssd_conv(bc_ref, bch_ref, cwb_ref, cbb_ref, be_ref, first)
    x = _silu(pre_x)
    bc = _silu(pre_bc)
    dt_pre = dt_ref[...] + vec_ref[0:1, :]
    dt = _softplus(dt_pre)
    a_row = vec_ref[1:2, :]
    amat = dt * a_row
    row, col = _ssd_tri()
    upper = (row <= col).astype(BF16)
    lower = (col <= row).astype(BF16)
    acum = _dot_exact01(amat, lower, NN, x_left=False)
    acum_t = _dot_exact01(amat, upper, TN, x_left=True)
    return pre_x, pre_bc, x, bc, dt_pre, dt, a_row, acum, acum_t, row, col, upper


def _ssd_gate_norm(y, z, nw):
    lane = lax.broadcasted_iota(jnp.int32, y.shape, 1)
    g0 = lane < SSD_WIDTH // 2
    hg = y * _silu(z)
    sq = hg * hg
    ms0 = jnp.sum(jnp.where(g0, sq, 0.0), axis=1, keepdims=True) * (2.0 / SSD_WIDTH)
    ms1 = jnp.sum(jnp.where(g0, 0.0, sq), axis=1, keepdims=True) * (2.0 / SSD_WIDTH)
    rs = jnp.where(g0, lax.rsqrt(ms0 + RMS_EPS), lax.rsqrt(ms1 + RMS_EPS))
    return hg, rs, g0


def _ssd_fwd(h, cwx, cbx, cwb, cbb, vec, nw, nb, s, name):
    n = SSD_CHUNK
    nc = s // n
    _, blk, halo, full = _ssd_specs(nc, False)

    def body(bc_ref, bch_ref, z_ref, xs_ref, xsh_ref, dt_ref, cwx_ref, cbx_ref, cwb_ref, cbb_ref, vec_ref, nw_ref,
             o_ref, hs_ref, h_scr, xe_ref, be_ref, y_scr):
        c = pl.program_id(1)

        @pl.when(c == 0)
        def _():
            h_scr[...] = jnp.zeros_like(h_scr)

        (_, _, x, bc, _, dt, a_row, acum, acum_t, row, col, _) = _ssd_prep(
            xs_ref, xsh_ref, bc_ref, bch_ref, dt_ref, cwx_ref, cbx_ref, cwb_ref, cbb_ref, vec_ref, xe_ref, be_ref, c == 0)
        hs_ref[...] = h_scr[...]
        heads = _ssd_core_fwd(x, bc, dt, acum, acum_t, a_row, vec_ref[2:3, :], hs_ref, col <= row)
        for hd in heads:
            sl = slice(hd["hh"] * HEAD_DIM, (hd["hh"] + 1) * HEAD_DIM)
            y_scr[:, sl] = hd["y"]
            h_scr[sl, :] = jnp.exp(hd["al"]) * hd["hp"] + _dot(hd["xdt"] * hd["w"], hd["bm"], TN)
        hg, rs, _ = _ssd_gate_norm(y_scr[...], z_ref[...], nw_ref[...])
        o_ref[...] = (hg * rs * nw_ref[...]).astype(o_ref.dtype)

    t = nb * s
    return pl.pallas_call(
        body, name=name, grid=(nb, nc),
        in_specs=[blk(512, H_BC), halo(512, H_BC), blk(384, H_Z), blk(384, H_XS), halo(384, H_XS), blk(128, H_DT),
                  full((4, 384)), full((1, 384)), full((4, 512)), full((1, 512)), full((SUBLANES, LANES)), full((1, 384))],
        out_specs=[pl.BlockSpec((n, SSD_WIDTH), lambda b, c: (b * nc + c, 0)),
                   pl.BlockSpec((None, SSD_WIDTH, SSD_STATE), lambda b, c: (b * nc + c, 0, 0))],
        out_shape=[jax.ShapeDtypeStruct((t, SSD_WIDTH), MXU_DTYPE),
                   jax.ShapeDtypeStruct((nb * nc, SSD_WIDTH, SSD_STATE), F32)],
        scratch_shapes=[pltpu.VMEM((SSD_WIDTH, SSD_STATE), F32), pltpu.VMEM((n + SUBLANES, 384), F32),
                        pltpu.VMEM((n + SUBLANES, 512), F32), pltpu.VMEM((n, SSD_WIDTH), F32)],
        compiler_params=_params(2),
    )(h, h, h, h, h, h, cwx, cbx, cwb, cbb, vec, nw)


def _ssd_bwd(h, hstate, dmix, cwx, cbx, cwb, cbb, vec, nw, nb, s, name):
    n = SSD_CHUNK
    nc = s // n
    cidx, blk, halo, full = _ssd_specs(nc, True)

    def body(bc_ref, bch_ref, z_ref, xs_ref, xsh_ref, dt_ref, hs_ref, do_ref, cwx_ref, cbx_ref, cwb_ref, cbb_ref,
             vec_ref, nw_ref, dz_ref, dxs_ref, dbc_ref, ddt_ref, gx_ref, gb_ref, gv_ref, gn_ref,
             dh_scr, xe_ref, be_ref, y_scr, dx_scr, dbc_scr, dxe_ref, dbe_ref, cx_ref, cb_ref):
        b = pl.program_id(0)
        c = pl.program_id(1)
        cc = nc - 1 - c

        @pl.when(jnp.logical_and(b == 0, c == 0))
        def _():
            gx_ref[...] = jnp.zeros_like(gx_ref)
            gb_ref[...] = jnp.zeros_like(gb_ref)
            gv_ref[...] = jnp.zeros_like(gv_ref)
            gn_ref[...] = jnp.zeros_like(gn_ref)

        @pl.when(c == 0)
        def _():
            dh_scr[...] = jnp.zeros_like(dh_scr)
            cx_ref[...] = jnp.zeros_like(cx_ref)
            cb_ref[...] = jnp.zeros_like(cb_ref)

        (pre_x, pre_bc, x, bc, dt_pre, dt, a_row, acum, acum_t, row, col, upper) = _ssd_prep(
            xs_ref, xsh_ref, bc_ref, bch_ref, dt_ref, cwx_ref, cbx_ref, cwb_ref, cbb_ref, vec_ref, xe_ref, be_ref, cc == 0)
        tri = col <= row
        d_row = vec_ref[2:3, :]
        heads = _ssd_core_fwd(x, bc, dt, acum, acum_t, a_row, d_row, hs_ref, tri)
        for hd in heads:
            y_scr[:, hd["hh"] * HEAD_DIM:(hd["hh"] + 1) * HEAD_DIM] = hd["y"]
        y = y_scr[...]
        z = z_ref[...]
        nwv = nw_ref[...]
        hg, rs, g0 = _ssd_gate_norm(y, z, nwv)
        do = do_ref[...]
        nrm = hg * rs
        gn_ref[...] += jnp.sum(do * nrm, axis=0, keepdims=True)
        dn = do * nwv
        dnn = dn * nrm
        mean0 = jnp.sum(jnp.where(g0, dnn, 0.0), axis=1, keepdims=True) * (2.0 / SSD_WIDTH)
        mean1 = jnp.sum(jnp.where(g0, 0.0, dnn), axis=1, keepdims=True) * (2.0 / SSD_WIDTH)
        dhg = rs * (dn - nrm * jnp.where(g0, mean0, mean1))
        dz_ref[...] = (dhg * y * _dsilu(z)).astype(dz_ref.dtype)
        dy = dhg * _silu(z)

        lane = lax.broadcasted_iota(jnp.int32, (n, LANES), 1)
        lane1 = lax.broadcasted_iota(jnp.int32, (1, LANES), 1)
        last_row = lax.broadcasted_iota(jnp.int32, (n, 1), 0) == n - 1
        dacum_col = jnp.zeros((n, LANES), F32)
        da_rowpart = jnp.zeros((n, LANES), F32)
        ddt = jnp.zeros((n, LANES), F32)
        dd_vec = jnp.zeros((1, LANES), F32)
        for hd in heads:
            sl = slice(hd["hh"] * HEAD_DIM, (hd["hh"] + 1) * HEAD_DIM)
            dyh = dy[:, sl]
            dhn = dh_scr[sl, :]
            hd.update(sl=sl, dyh=dyh, dhn=dhn, t1=_dot(dyh, hd["hp"]), dm=_dot(dyh, hd["xdt"], NT),
                      t2=_dot(hd["bm"], dhn, NT), mtdy=_dot(hd["m"], dyh, TN), xdhn=_dot(hd["xdt"], dhn),
                      dhp=_dot(dyh * hd["ea"], hd["cm"], TN))
        dgs, dbms, dcms = [], [], []
        for g in range(2):
            dg = jnp.zeros((n, n), F32)
            dbm = jnp.zeros((n, SSD_STATE), F32)
            dcm = jnp.zeros((n, SSD_STATE), F32)
            for hd in heads[3 * g:3 * g + 3]:
                hh, sl, dyh, dhn, t2 = hd["hh"], hd["sl"], hd["dyh"], hd["dhn"], hd["t2"]
                el = jnp.exp(hd["al"])
                dd_vec = dd_vec + jnp.where(lane1 == hh, jnp.sum(dyh * hd["xh"]), 0.0)
                dcm = dcm + hd["ea"] * hd["t1"]
                dg = dg + hd["dm"] * hd["dec"]
                e = hd["dm"] * hd["m"]
                dxdt = hd["mtdy"] + hd["w"] * t2
                dbm = dbm + hd["w"] * hd["xdhn"]
                dw_w = jnp.sum(hd["xdt"] * t2, axis=1, keepdims=True) * hd["w"]
                d_el = jnp.sum(dhn * hd["hp"])
                col_part = (jnp.sum(dyh * hd["yo"], axis=1, keepdims=True) + jnp.sum(e, axis=1, keepdims=True) - dw_w
                            + jnp.where(last_row, d_el * el + jnp.sum(dw_w), 0.0))
                dacum_col = dacum_col + jnp.where(lane == hh, col_part, 0.0)
                neg_colsum = -jnp.sum(e, axis=0, keepdims=True)
                rev = jnp.sum(jnp.where(row <= col, neg_colsum, 0.0), axis=1, keepdims=True)
                da_rowpart = da_rowpart + jnp.where(lane == hh, rev, 0.0)
                dh_scr[sl, :] = el * dhn + hd["dhp"]
                dx_scr[:, sl] = d_row[:, hh:hh + 1] * dyh + dxdt * hd["dth"]
                ddt = ddt + jnp.where(lane == hh, jnp.sum(dxdt * hd["xh"], axis=1, keepdims=True), 0.0)
            dgs.append(dg)
            dbms.append(dbm)
            dcms.append(dcm)
        for g in range(2):
            bm, cm = heads[3 * g]["bm"], heads[3 * g]["cm"]
            dbc_scr[:, g * SSD_STATE:(g + 1) * SSD_STATE] = dbms[g] + _dot(dgs[g], cm, TN)
            dbc_scr[:, 2 * SSD_STATE + g * SSD_STATE:2 * SSD_STATE + (g + 1) * SSD_STATE] = dcms[g] + _dot(dgs[g], bm)
        da_mat = _dot_exact01(dacum_col, upper, NN, x_left=False) + da_rowpart
        ddt = ddt + da_mat * a_row
        da_vec = jnp.sum(da_mat * dt, axis=0, keepdims=True)
        ddt_pre = jnp.where(lane < SSD_HEADS, ddt * _sigmoid(dt_pre), 0.0)
        ddt_ref[...] = ddt_pre.astype(ddt_ref.dtype)
        rid = lax.broadcasted_iota(jnp.int32, (SUBLANES, LANES), 0)
        gv_ref[...] += (jnp.where(rid == 0, jnp.sum(ddt_pre, axis=0, keepdims=True), 0.0)
                        + jnp.where(rid == 1, da_vec, 0.0) + jnp.where(rid == 2, dd_vec, 0.0))

        def conv_bwd(dpost, pre, w_ref, ext_ref, dext_ref, carry_ref, cur_ref, out_ref, g_ref, width):
            dco = dpost * _dsilu(pre)
            dext_ref[0:n, :] = dco
            dext_ref[n:n + SUBLANES, :] = carry_ref[...]
            out_ref[...] = (w_ref[3:4, :] * dco + w_ref[2:3, :] * dext_ref[pl.ds(1, n), :]
                            + w_ref[1:2, :] * dext_ref[pl.ds(2, n), :] + w_ref[0:1, :] * dext_ref[pl.ds(3, n), :]
                            ).astype(out_ref.dtype)
            carry_ref[...] = dco[0:SUBLANES, :]
            rid8 = lax.broadcasted_iota(jnp.int32, (SUBLANES, width), 0)
            acc = jnp.where(rid8 == 3, jnp.sum(dco * cur_ref[...], axis=0, keepdims=True), 0.0)
            for j in range(3):
                sh = ext_ref[pl.ds(SUBLANES - 3 + j, n), :]
                acc = acc + jnp.where(rid8 == j, jnp.sum(dco * sh, axis=0, keepdims=True), 0.0)
            acc = acc + jnp.where(rid8 == 4, jnp.sum(dco, axis=0, keepdims=True), 0.0)
            g_ref[...] += acc

        conv_bwd(dx_scr[...], pre_x, cwx_ref, xe_ref, dxe_ref, cx_ref, xs_ref, dxs_ref, gx_ref, 384)
        conv_bwd(dbc_scr[...], pre_bc, cwb_ref, be_ref, dbe_ref, cb_ref, bc_ref, dbc_ref, gb_ref, 512)

    t = nb * s
    rowblk = lambda width: pl.BlockSpec((n, width), lambda b, c: (b * nc + cidx(c), 0))
    return pl.pallas_call(
        body, name=name, grid=(nb, nc),
        in_specs=[blk(512, H_BC), halo(512, H_BC), blk(384, H_Z), blk(384, H_XS), halo(384, H_XS), blk(128, H_DT),
                  pl.BlockSpec((None, SSD_WIDTH, SSD_STATE), lambda b, c: (b * nc + cidx(c), 0, 0)),
                  pl.BlockSpec((n, SSD_WIDTH), lambda b, c: (b * nc + cidx(c), 0)),
                  full((4, 384)), full((1, 384)), full((4, 512)), full((1, 512)), full((SUBLANES, LANES)), full((1, 384))],
        out_specs=[rowblk(384), rowblk(384), rowblk(512), rowblk(128),
                   full((SUBLANES, 384)), full((SUBLANES, 512)), full((SUBLANES, LANES)), full((1, 384))],
        out_shape=[jax.ShapeDtypeStruct((t, 384), MXU_DTYPE), jax.ShapeDtypeStruct((t, 384), MXU_DTYPE),
                   jax.ShapeDtypeStruct((t, 512), MXU_DTYPE), jax.ShapeDtypeStruct((t, 128), MXU_DTYPE),
                   jax.ShapeDtypeStruct((SUBLANES, 384), F32), jax.ShapeDtypeStruct((SUBLANES, 512), F32),
                   jax.ShapeDtypeStruct((SUBLANES, LANES), F32), jax.ShapeDtypeStruct((1, 384), F32)],
        scratch_shapes=[pltpu.VMEM((SSD_WIDTH, SSD_STATE), F32), pltpu.VMEM((n + SUBLANES, 384), F32),
                        pltpu.VMEM((n + SUBLANES, 512), F32), pltpu.VMEM((n, SSD_WIDTH), F32),
                        pltpu.VMEM((n, 384), F32), pltpu.VMEM((n, 512), F32),
                        pltpu.VMEM((n + SUBLANES, 384), F32), pltpu.VMEM((n + SUBLANES, 512), F32),
                        pltpu.VMEM((SUBLANES, 384), F32), pltpu.VMEM((SUBLANES, 512), F32)],
        compiler_params=_params(2),
    )(h, h, h, h, h, h, hstate, dmix, cwx, cbx, cwb, cbb, vec, nw)


def _adamw_math(w, g, m, v):
    m = ADAM_B1 * m + (1.0 - ADAM_B1) * g
    v = ADAM_B2 * v + (1.0 - ADAM_B2) * (g * g)
    m_hat = m / (1.0 - ADAM_B1 ** ADAM_STEP)
    v_hat = v / (1.0 - ADAM_B2 ** ADAM_STEP)
    delta = -ADAM_LR * (m_hat / (jnp.sqrt(v_hat) + ADAM_EPS) + ADAM_WD * w)
    return delta, m, v


def _adamw(w, g, m, v, name, tr=256):
    rows, cols = w.shape
    tr = rows if rows <= tr else tr
    assert rows % tr == 0, (rows, tr)

    def body(w_ref, g_ref, m_ref, v_ref, d_ref, nm_ref, nv_ref):
        d, nm, nv = _adamw_math(w_ref[...], g_ref[...], m_ref[...], v_ref[...])
        d_ref[...] = d
        nm_ref[...] = nm
        nv_ref[...] = nv

    spec = pl.BlockSpec((tr, cols), lambda i: (i, 0))
    shp = jax.ShapeDtypeStruct((rows, cols), F32)
    return pl.pallas_call(body, name=name, grid=(rows // tr,), in_specs=[spec] * 4, out_specs=[spec] * 3,
                          out_shape=[shp] * 3, compiler_params=_params(1))(w, g, m, v)


def _sum8_layers(parts, name, tr):
    _, rows, cols = parts[0].shape
    assert rows % tr == 0
    nt = rows // tr

    def body(*refs):
        o_ref = refs[DEPTH]
        layer = pl.program_id(0)
        for l in range(DEPTH):
            @pl.when(layer == l)
            def _(l=l):
                acc = refs[l][0]
                for k in range(1, N_DEV):
                    acc = acc + refs[l][k]
                o_ref[...] = acc

    in_specs = [pl.BlockSpec((N_DEV, tr, cols), lambda a, i, l=l: (0, jnp.clip(i + (a - l) * nt, 0, nt - 1), 0))
                for l in range(DEPTH)]
    return pl.pallas_call(body, name=name, grid=(DEPTH, nt), in_specs=in_specs,
                          out_specs=pl.BlockSpec((None, tr, cols), lambda a, i: (a, i, 0)),
                          out_shape=jax.ShapeDtypeStruct((DEPTH, rows, cols), F32), compiler_params=_params(2))(*parts)


def _all_reduce_small(vec, name):
    rows, cols = vec.shape

    def body(x_ref, out_ref, gbuf, send_sems, recv_sems):
        x, y, c = lax.axis_index("x"), lax.axis_index("y"), lax.axis_index("c")
        me, sibling = (x, y, c), (x, y, 1 - c)
        chips = [(1 - x, y), (x, 1 - y), (1 - x, 1 - y)]

        def slot(px, py, pc):
            return gbuf.at[4 * px + 2 * py + pc]

        def copy(k, block, to, src=None):
            return pltpu.make_async_remote_copy(
                src_ref=slot(*block) if src is None else src, dst_ref=slot(*block),
                send_sem=send_sems.at[k], recv_sem=recv_sems.at[k], device_id=to, device_id_type=MESH_ID)

        first = [copy(0, me, sibling, src=x_ref)]
        first += [copy(1 + j, me, (*chip, c), src=x_ref) for j, chip in enumerate(chips)]
        for cp in first:
            cp.start()
        gbuf[4 * x + 2 * y + c] = x_ref[...]
        passed = [copy(4 + j, (*chip, c), sibling) for j, chip in enumerate(chips)]
        for j, chip in enumerate(chips):
            copy(1 + j, (*chip, c), me).wait_recv()
            passed[j].start()
        copy(0, sibling, me).wait_recv()
        for j, chip in enumerate(chips):
            copy(4 + j, (*chip, 1 - c), me).wait_recv()
        for cp in first + passed:
            cp.wait_send()
        acc = gbuf[0]
        for k in range(1, N_DEV):
            acc = acc + gbuf[k]
        out_ref[...] = acc

    return pl.pallas_call(
        body, name=name, out_shape=jax.ShapeDtypeStruct((rows, cols), F32),
        in_specs=[pl.BlockSpec(memory_space=pltpu.VMEM)], out_specs=pl.BlockSpec(memory_space=pltpu.VMEM),
        scratch_shapes=[pltpu.VMEM((N_DEV, rows, cols), F32), pltpu.SemaphoreType.DMA((7,)), pltpu.SemaphoreType.DMA((7,))],
        compiler_params=pltpu.CompilerParams(has_side_effects=True, vmem_limit_bytes=VMEM_LIMIT_BYTES),
    )(vec)


_COL_POOL, _COL_Z, _COL_XBC, _COL_DT, _COL_Q, _COL_K, _COL_V = 0, 256, 640, 1536, 1542, 1926, 2310
_H_SEGMENTS = ((_COL_XBC + SSD_WIDTH, 512), (_COL_POOL, 256), (_COL_Q, 384), (_COL_K, 384), (_COL_V, 384),
               (_COL_Z, 384), (_COL_XBC, 384), (_COL_DT, 6))


def _h_from_orig(w):
    parts = [w[..., o:o + n] for o, n in _H_SEGMENTS]
    pad = jnp.zeros(w.shape[:-1] + (H_COLS - IN_COLS,), w.dtype)
    return jnp.concatenate(parts + [pad], axis=-1)


def _h_to_orig(w):
    offs, o = {}, 0
    for orig, n in _H_SEGMENTS:
        offs[orig] = (o, n)
        o += n
    order = sorted(offs)
    return jnp.concatenate([w[..., offs[k][0]:offs[k][0] + offs[k][1]] for k in order], axis=-1)


def _interleave(w):
    lead = w.shape[:-1]
    nt = D_FF // GLU_TILE
    return jnp.swapaxes(w.reshape(lead + (2, nt, GLU_TILE)), -3, -2).reshape(lead + (2 * D_FF,))


def _deinterleave(w):
    lead = w.shape[:-1]
    nt = D_FF // GLU_TILE
    return jnp.swapaxes(w.reshape(lead + (nt, 2, GLU_TILE)), -3, -2).reshape(lead + (2 * D_FF,))


def _up_segments():
    segs = []
    for j in range(N_DEV):
        half, base = j // 4, UP_SHARD * (j % 4)
        c = base
        while c < base + UP_SHARD:
            t, r = divmod(c, GLU_TILE)
            n = min(GLU_TILE - r, base + UP_SHARD - c)
            segs.append((j, c - base, 2 * GLU_TILE * t + GLU_TILE * half + r, n))
            c += n
    return segs


def _up_to_interleaved(w, name, tr=256):
    def body(i_ref, o_ref):
        for j, src, dst, n in _up_segments():
            o_ref[:, dst:dst + n] = i_ref[j, :, src:src + n]

    return pl.pallas_call(
        body, name=name, grid=(D_MODEL // tr,), in_specs=[pl.BlockSpec((N_DEV, tr, UP_SHARD), lambda r: (0, r, 0))],
        out_specs=pl.BlockSpec((tr, 2 * D_FF), lambda r: (r, 0)),
        out_shape=jax.ShapeDtypeStruct((D_MODEL, 2 * D_FF), w.dtype), compiler_params=_params(1))(w)


def _up_from_interleaved(g, name, tr=128):
    def body(i_ref, o_ref):
        for j, src, dst, n in _up_segments():
            o_ref[j, :, src:src + n] = i_ref[:, dst:dst + n]

    return pl.pallas_call(
        body, name=name, grid=(D_MODEL // tr,), in_specs=[pl.BlockSpec((tr, 2 * D_FF), lambda r: (r, 0))],
        out_specs=pl.BlockSpec((N_DEV, tr, UP_SHARD), lambda r: (0, r, 0)),
        out_shape=jax.ShapeDtypeStruct((N_DEV, D_MODEL, UP_SHARD), g.dtype), compiler_params=_params(1))(g)


def _mix_rows_from_orig(w):
    return jnp.concatenate([w[256:640], w[640:1024], w[0:256]], axis=0)


def _mix_rows_to_orig(w):
    return jnp.concatenate([w[768:1024], w[0:384], w[384:768]], axis=0)


def _xbc_split(w):
    return w[..., :SSD_WIDTH], w[..., SSD_WIDTH:]


def _layer_fwd(x, p_l, wt, sp, nb, s, comm=None):
    h = _mm(x, wt["w_in"], "nn", F32, "mm_in", tm=1024, tn=1408)
    pool_out = _pool_fwd(h, wt["pool_bd"], sp["pool_scale"], nb, s, "pool_fwd")
    ssd_out, hstate = _ssd_fwd(h, sp["cwx"], sp["cbx"], sp["cwb"], sp["cbb"], sp["ssd_vec"], sp["ssd_norm_w"], nb, s, "ssd_fwd")
    sb_out, comm_out = _sb_fwd(h, nb, s, "sb_fwd" if comm is None else "sb_fwd_gather", comm)
    mixcat = jnp.concatenate([ssd_out, sb_out, pool_out], axis=1)
    x1, r1 = _mm_ln(mixcat, wt["w_out"], x, sp["ln1"], "mm_out_ln1", tm=1024, tk=1024)
    up = _mm(x1, wt["w_up"], "nn", F32, "mm_up", tm=1024, tn=1408)
    act = _glu_fwd(up, sp["ffn_cw"], sp["ffn_cb"], nb, s, "glu_fwd")
    gp = _mm(x1, wt["w_gate"], "nn", F32, "mm_gate", tm=1024, tn=1024)
    pp = _mm(p_l, wt["w_proj"], "nn", F32, "mm_proj", tm=2048, tn=1024)
    x2, r2 = _mm_ln(act, wt["w_down"], x1, sp["ln2"], "mm_down_ln2", gp=gp, pp=pp, tm=512, tk=1408)
    return x2, dict(x=x, h=h, hstate=hstate, mixcat=mixcat, r1=r1, x1=x1, up=up, act=act, gp=gp, pp=pp, r2=r2), comm_out


def _layer_bwd(dx2, p_l, sv, wt, sp, nb, s, comm=None, tail_comm=None):
    dr2, dgp, dpp, st2 = _ln_bwd(sv["r2"], sp["ln2"], dx2, "ln2_bwd", gp=sv["gp"], pp=sv["pp"])
    g_down = _mm(sv["act"], dr2, "tn", F32, "wg_down", tm=1408, tn=1024, tk=512)
    dact = _mm(dr2, wt["w_down"], "nt", F32, "dg_down", tm=1024, tn=1408)
    dup, ffn_acc = _glu_bwd(sv["up"], dact, sp["ffn_cw"], sp["ffn_cb"], nb, s, "glu_bwd")
    g_up = _mm(sv["x1"], dup, "tn", F32, "wg_up", tm=1024, tn=2816, tk=512)
    g_gate = _mm(sv["x1"], dgp, "tn", F32, "wg_gate", tm=1024, tn=1024, tk=512)
    g_proj = _mm(p_l, dpp, "tn", F32, "wg_proj", tm=256, tn=1024, tk=512)
    t1 = _mm(dgp, wt["w_gate"], "nt", F32, "dg_gate", tm=1024, tn=1024, add=dr2, add_coef=ALPHA)
    dr1, st1 = _mm_ln_bwd(dup, wt["w_up"], t1, sv["r1"], sp["ln1"], "dg_up_ln1", tm=512, tk=1408)
    g_out = _mm(sv["mixcat"], dr1, "tn", F32, "wg_out", tm=1024, tn=1024, tk=512)
    dmix = _mm(dr1, wt["w_out"], "nt", F32, "dg_out", tm=1024, tn=1024)
    du, g_pool_bd, g_pool_scale = _pool_bwd(sv["h"], dmix, wt["pool_bd"], sp["pool_scale"], nb, s, "pool_bwd")
    dz, dxs, dbc, ddt, gx, gb, gv, gn = _ssd_bwd(sv["h"], sv["hstate"], dmix, sp["cwx"], sp["cbx"], sp["cwb"], sp["cbb"],
                                                  sp["ssd_vec"], sp["ssd_norm_w"], nb, s, "ssd_bwd")
    ready = dict(w_out=g_out, ffn_w_up=g_up, ffn_w_down=g_down, ple_w_gate=g_gate, ple_w_proj=g_proj)
    job = comm(ready) if comm is not None else None
    dq, dk, dv, comm_out = _sb_bwd(sv["h"], dmix, nb, s, "sb_bwd" if job is None else "sb_bwd_x%d" % job["n_xfers"], job)
    dh = jnp.concatenate([dbc, du, dq, dk, dv, dz, dxs, ddt], axis=1)
    g_in = _mm(sv["x"], dh, "tn", F32, "wg_in", tm=1024, tn=2816, tk=512)
    tail_job = tail_comm(g_in) if tail_comm is not None else None
    dx = _mm(dh, wt["w_in"], "nt", F32, "dg_in" if tail_job is None else "dg_in_x", tm=1024, tn=1024, tk=1408, add=dr1,
             add_coef=ALPHA, comm=tail_job)
    dx, tail_out = dx if tail_job is not None else (dx, None)
    small = dict(
        pool_w=jnp.stack([g_pool_bd[HEAD_DIM * g:HEAD_DIM * (g + 1), HEAD_DIM * g:HEAD_DIM * (g + 1)] for g in range(4)]),
        pool_scale=g_pool_scale[0],
        ssd_conv_w=jnp.concatenate([gx[0:4], gb[0:4]], axis=1),
        ssd_conv_b=jnp.concatenate([gx[4], gb[4]], axis=0),
        ssd_dt_bias=gv[0, :SSD_HEADS],
        ssd_a_log=gv[1, :SSD_HEADS] * sp["ssd_vec"][1, :SSD_HEADS],
        ssd_d=gv[2, :SSD_HEADS],
        ssd_norm_w=gn[0],
        ln1_g=st1[0], ln1_b=st1[1], ln2_g=st2[0], ln2_b=st2[1],
        ffn_conv_w=_deinterleave(ffn_acc[0:3]),
        ffn_conv_b=_deinterleave(ffn_acc[3]),
    )
    return dx, dict(ready, w_in=g_in), small, comm_out, tail_out


def _layer_params(i, big, rep):
    pool_bd = jnp.zeros((POOL_WIDTH, POOL_WIDTH), F32)
    for g in range(4):
        pool_bd = lax.dynamic_update_slice(pool_bd, rep["pool_w"][i, g], (HEAD_DIM * g, HEAD_DIM * g))
    wt = dict(w_in=big["w_in"], w_out=big["w_out"], w_up=big["ffn_w_up"], w_down=big["ffn_w_down"],
              w_gate=big["ple_w_gate"], w_proj=big["ple_w_proj"], pool_bd=pool_bd.astype(MXU_DTYPE))
    cwx, cwb = _xbc_split(rep["ssd_conv_w"][i])
    cbx, cbb = _xbc_split(rep["ssd_conv_b"][i][None, :])
    vec = jnp.zeros((SUBLANES, LANES), F32)
    vec = vec.at[0, :SSD_HEADS].set(rep["ssd_dt_bias"][i])
    vec = vec.at[1, :SSD_HEADS].set(-jnp.exp(rep["ssd_a_log"][i]))
    vec = vec.at[2, :SSD_HEADS].set(rep["ssd_d"][i])
    sp = dict(pool_scale=rep["pool_scale"][i][None, :], cwx=cwx, cbx=cbx, cwb=cwb, cbb=cbb, ssd_vec=vec,
              ssd_norm_w=rep["ssd_norm_w"][i][None, :],
              ln1=jnp.stack([rep["ln1_g"][i], rep["ln1_b"][i]]), ln2=jnp.stack([rep["ln2_g"][i], rep["ln2_b"][i]]),
              ffn_cw=_interleave(rep["ffn_conv_w"][i]), ffn_cb=_interleave(rep["ffn_conv_b"][i][None, :]))
    return wt, sp


def _run_layers(x, p, target, big_w, rep, fwd_job=None, fwd_done=None, bwd_job=None, bwd_done=None, tail_job=None,
                tail_done=None):
    nb, s, d = x.shape
    t = nb * s
    xf = x.reshape(t, d)
    saved, params = [], []
    for i in range(DEPTH):
        wt, sp = _layer_params(i, big_w[i], rep)
        params.append((wt, sp))
        job = fwd_job(i) if fwd_job is not None else None
        xf, sv, res = _layer_fwd(xf, p[i].reshape(t, PLE_DIM), wt, sp, nb, s, job)
        if job is not None:
            fwd_done(i, res)
        saved.append(sv)
    dy, loss = _loss_grad(xf, target.reshape(t, d), "loss")
    bigs, smalls = [None] * DEPTH, [None] * DEPTH
    for i in reversed(range(DEPTH)):
        wt, sp = params[i]
        job = (lambda ready, i=i: bwd_job(i, bigs, ready)) if bwd_job is not None else None
        dy, bigs[i], smalls[i], res, tail = _layer_bwd(dy, p[i].reshape(t, PLE_DIM), saved[i], wt, sp, nb, s, job,
                                                         tail_job if i == 0 else None)
        if job is not None:
            bwd_done(i, res)
        if tail is not None:
            tail_done(tail)
    return loss, dy.reshape(nb, s, d), bigs, smalls


def _local_step(x, p, target, full, rep):
    return _run_layers(x, p, target, [{n: full[n][i] for n in full} for i in range(DEPTH)], rep)


BIG = ("w_in", "w_out", "ffn_w_up", "ffn_w_down", "ple_w_gate", "ple_w_proj")
SMALL_REPLICATED = ("pool_w", "pool_scale", "ssd_conv_b", "ssd_dt_bias", "ssd_a_log", "ssd_d", "ssd_norm_w",
                    "ln1_g", "ln1_b", "ffn_conv_b", "ln2_g", "ln2_b")
SMALL_SHARDED = ("ssd_conv_w", "ffn_conv_w")
WEIGHTS = ("w_in", "pool_w", "pool_scale", "ssd_conv_w", "ssd_conv_b", "ssd_dt_bias", "ssd_a_log", "ssd_d", "ssd_norm_w",
           "w_out", "ln1_g", "ln1_b", "ffn_w_up", "ffn_conv_w", "ffn_conv_b", "ffn_w_down", "ln2_g", "ln2_b",
           "ple_w_gate", "ple_w_proj")
SUM_BLOCK_BYTES = 3 * 1024 * 1024


def _to_rows(a, cols):
    f = a.reshape(-1)
    pad = (-f.shape[0]) % cols
    if pad:
        f = jnp.concatenate([f, jnp.zeros((pad,), f.dtype)])
    return f.reshape(-1, cols)


def _pack_rows(arrs, cols, row_mult):
    rows = [_to_rows(a, cols) for a in arrs]
    flat = jnp.concatenate(rows, axis=0)
    pad = (-flat.shape[0]) % row_mult
    if pad:
        flat = jnp.concatenate([flat, jnp.zeros((pad, cols), flat.dtype)], axis=0)
    return flat


def _unpack_rows(flat, shapes, cols):
    out, r = [], 0
    for shp in shapes:
        n = 1
        for v in shp:
            n *= v
        nr = -(-n // cols)
        out.append(flat[r:r + nr].reshape(-1)[:n].reshape(shp))
        r += nr
    return out


def kernel(x, p, w_in, pool_w, pool_scale, ssd_conv_w, ssd_conv_b, ssd_dt_bias, ssd_a_log, ssd_d, ssd_norm_w, w_out, ln1_g, ln1_b, ffn_w_up, ffn_conv_w, ffn_conv_b, ffn_w_down, ln2_g, ln2_b, ple_w_gate, ple_w_proj, loss_target, m_w_in, m_pool_w, m_pool_scale, m_ssd_conv_w, m_ssd_conv_b, m_ssd_dt_bias, m_ssd_a_log, m_ssd_d, m_ssd_norm_w, m_w_out, m_ln1_g, m_ln1_b, m_ffn_w_up, m_ffn_conv_w, m_ffn_conv_b, m_ffn_w_down, m_ln2_g, m_ln2_b, m_ple_w_gate, m_ple_w_proj, v_w_in, v_pool_w, v_pool_scale, v_ssd_conv_w, v_ssd_conv_b, v_ssd_dt_bias, v_ssd_a_log, v_ssd_d, v_ssd_norm_w, v_w_out, v_ln1_g, v_ln1_b, v_ffn_w_up, v_ffn_conv_w, v_ffn_conv_b, v_ffn_w_down, v_ln2_g, v_ln2_b, v_ple_w_gate, v_ple_w_proj):
    wts = dict(w_in=w_in, pool_w=pool_w, pool_scale=pool_scale, ssd_conv_w=ssd_conv_w, ssd_conv_b=ssd_conv_b,
               ssd_dt_bias=ssd_dt_bias, ssd_a_log=ssd_a_log, ssd_d=ssd_d, ssd_norm_w=ssd_norm_w, w_out=w_out, ln1_g=ln1_g,
               ln1_b=ln1_b, ffn_w_up=ffn_w_up, ffn_conv_w=ffn_conv_w, ffn_conv_b=ffn_conv_b, ffn_w_down=ffn_w_down,
               ln2_g=ln2_g, ln2_b=ln2_b, ple_w_gate=ple_w_gate, ple_w_proj=ple_w_proj)
    mom_m = dict(w_in=m_w_in, pool_w=m_pool_w, pool_scale=m_pool_scale, ssd_conv_w=m_ssd_conv_w, ssd_conv_b=m_ssd_conv_b,
                 ssd_dt_bias=m_ssd_dt_bias, ssd_a_log=m_ssd_a_log, ssd_d=m_ssd_d, ssd_norm_w=m_ssd_norm_w, w_out=m_w_out,
                 ln1_g=m_ln1_g, ln1_b=m_ln1_b, ffn_w_up=m_ffn_w_up, ffn_conv_w=m_ffn_conv_w, ffn_conv_b=m_ffn_conv_b,
                 ffn_w_down=m_ffn_w_down, ln2_g=m_ln2_g, ln2_b=m_ln2_b, ple_w_gate=m_ple_w_gate, ple_w_proj=m_ple_w_proj)
    mom_v = dict(w_in=v_w_in, pool_w=v_pool_w, pool_scale=v_pool_scale, ssd_conv_w=v_ssd_conv_w, ssd_conv_b=v_ssd_conv_b,
                 ssd_dt_bias=v_ssd_dt_bias, ssd_a_log=v_ssd_a_log, ssd_d=v_ssd_d, ssd_norm_w=v_ssd_norm_w, w_out=v_w_out,
                 ln1_g=v_ln1_g, ln1_b=v_ln1_b, ffn_w_up=v_ffn_w_up, ffn_conv_w=v_ffn_conv_w, ffn_conv_b=v_ffn_conv_b,
                 ffn_w_down=v_ffn_w_down, ln2_g=v_ln2_g, ln2_b=v_ln2_b, ple_w_gate=v_ple_w_gate, ple_w_proj=v_ple_w_proj)
    me = 4 * lax.axis_index("x") + 2 * lax.axis_index("y") + lax.axis_index("c")

    def layer_shards(i):
        sh = {n: wts[n][i].astype(MXU_DTYPE) for n in BIG}
        sh["w_in"] = _h_from_orig(wts["w_in"][i]).astype(MXU_DTYPE)
        return sh

    def gathered_weights(res):
        big = dict(zip(BIG, res[:len(BIG)]))
        big["ffn_w_up"] = _up_to_interleaved(big["ffn_w_up"], "up_to_interleaved")
        return big

    res0 = _gather_call_two_level(_gather_job(layer_shards(0), [wts[n] for n in SMALL_SHARDED]), "gather_layer0")
    big_w = [gathered_weights(res0)] + [None] * (DEPTH - 1)
    rep = {n: wts[n] for n in SMALL_REPLICATED}
    for n, g in zip(SMALL_SHARDED, res0[len(BIG):]):
        rep[n] = jnp.transpose(g, (1, 2, 0, 3)).reshape(g.shape[1], g.shape[2], N_DEV * g.shape[3])

    def fwd_job(i):
        return _gather_job(layer_shards(i + 1)) if i + 1 < DEPTH else None

    def fwd_done(i, res):
        big_w[i + 1] = gathered_weights(res)

    received = [dict() for _ in range(DEPTH)]
    carried = ("w_out", "ffn_w_up", "ffn_w_down", "ple_w_gate", "ple_w_proj")

    def bwd_items(i, bigs, ready):
        items = [(i, n, ready[n]) for n in carried] + ([(i + 1, "w_in", bigs[i + 1]["w_in"])] if i + 1 < DEPTH else [])
        return [(l, n, _up_from_interleaved(g, "up_from_interleaved") if n == "ffn_w_up" else g) for l, n, g in items]

    pending = {}

    def bwd_job(i, bigs, ready):
        pending[i] = bwd_items(i, bigs, ready)
        return _exchange_job([(n, g) for _, n, g in pending[i]])

    def bwd_done(i, res):
        for (l, n, _), r in zip(pending[i], res):
            received[l][n] = r

    def tail_done(res):
        received[0]["w_in"] = res[0]

    loss_loc, grad_x, bigs, smalls = _run_layers(x, p, loss_target, big_w, rep, fwd_job, fwd_done, bwd_job, bwd_done,
                                                 lambda g_in: _exchange_job([("w_in", g_in)]), tail_done)

    grads = {}
    for n in BIG:
        parts = [received[i][n] for i in range(DEPTH)]
        _, rows, cols = parts[0].shape
        tr = next(t for t in (256, 128, 64, 32, 16, 8) if rows % t == 0 and N_DEV * t * cols * 4 <= SUM_BLOCK_BYTES)
        g = _sum8_layers(parts, "sum_" + n, tr)
        grads[n] = _h_to_orig(g) if n == "w_in" else g
    small_names = SMALL_REPLICATED + SMALL_SHARDED
    small_full_shapes = [rep[n].shape for n in small_names]
    small_vec = _pack_rows([jnp.stack([smalls[i][n] for i in range(DEPTH)]) for n in small_names] + [loss_loc[0, :1]],
                           LANES, SUBLANES)
    small_sum = _all_reduce_small(small_vec, "allreduce_small")
    small_out = _unpack_rows(small_sum, small_full_shapes + [(1,)], LANES)
    loss = small_out[-1][0]
    for n, g in zip(small_names, small_out[:-1]):
        if n in SMALL_SHARDED:
            width = wts[n].shape[-1]
            g = lax.dynamic_slice_in_dim(g, me * width, width, axis=g.ndim - 1)
        grads[n] = g

    delta, new_m, new_v = {}, {}, {}
    for n in BIG:
        shp = wts[n].shape
        two_d = lambda a: a.reshape(-1, shp[-1])
        tr = {"w_in": 128, "ffn_w_down": DOWN_SHARD}.get(n, 256)
        d_, m_, v_ = _adamw(two_d(wts[n]), two_d(grads[n]), two_d(mom_m[n]), two_d(mom_v[n]), "adamw_" + n, tr=tr)
        delta[n], new_m[n], new_v[n] = d_.reshape(shp), m_.reshape(shp), v_.reshape(shp)
    packs = [_pack_rows([src[n] for n in small_names], LANES, SUBLANES) for src in (wts, grads, mom_m, mom_v)]
    outs = _adamw(*packs, "adamw_small", tr=packs[0].shape[0])
    shapes = [wts[n].shape for n in small_names]
    for dst, flat in zip((delta, new_m, new_v), outs):
        for n, a in zip(small_names, _unpack_rows(flat, shapes, LANES)):
            dst[n] = a
    return (loss, grad_x, *[grads[n] for n in WEIGHTS], *[delta[n] for n in WEIGHTS],
            *[new_m[n] for n in WEIGHTS], *[new_v[n] for n in WEIGHTS])
```

```python
import functools

import jax
import jax.numpy as jnp
from jax import lax
from jax.experimental import pallas as pl
from jax.experimental.pallas import tpu as pltpu

F32 = jnp.float32
BF16 = jnp.bfloat16
MXU_DTYPE = jnp.bfloat16

D_MODEL = 1024
DEPTH = 4
PLE_DIM = 256
ALPHA = (2 * DEPTH) ** 0.25
LN_EPS = 1e-5
RMS_EPS = 1e-6
HEAD_DIM = 64
POOL_WIDTH = 256
POOL_WINDOWS = (2, 4, 8, 16)
SSD_WIDTH = 384
SSD_HEADS = 6
SSD_STATE = 128
SSD_XBC = 896
SB_WIDTH = 384
IN_COLS = 2694
D_FF = 2816
N_DEV = 8
UP_SHARD = 2 * D_FF // N_DEV
DOWN_SHARD = D_FF // N_DEV

ADAM_LR = 0.001
ADAM_B1 = 0.9
ADAM_B2 = 0.999
ADAM_EPS = 1e-08
ADAM_WD = 0.01
ADAM_STEP = 10

LANES = 128
SUBLANES = 8
VMEM_LIMIT_BYTES = 56 * 1024 * 1024

H_COLS = 2816
H_BC = 0
H_POOL = 512
H_Q = 768
H_K = 1152
H_V = 1536
H_Z = 1920
H_XS = 2304
H_DT = 2688
SSD_CHUNK = 512
QB = 256
GLU_TILE = 256
MASKED_LOG = -1e30

NN = ((1,), (0,))
NT = ((1,), (1,))
TN = ((0,), (0,))


def _dot(a, b, dims=NN):
    return lax.dot_general(a.astype(MXU_DTYPE), b.astype(MXU_DTYPE), (dims, ((), ())), preferred_element_type=F32)


def _dot_exact01(x, m01, dims=NN, x_left=True, terms=3):
    acc = None
    r = x
    for _ in range(terms):
        hi = r.astype(BF16)
        ops = (hi, m01) if x_left else (m01, hi)
        part = lax.dot_general(ops[0], ops[1], (dims, ((), ())), preferred_element_type=F32)
        acc = part if acc is None else acc + part
        r = r - hi.astype(F32)
    return acc


def _sigmoid(v):
    return 1.0 / (1.0 + jnp.exp(-v))


def _silu(v):
    return v * _sigmoid(v)


def _dsilu(v):
    s = _sigmoid(v)
    return s * (1.0 + v * (1.0 - s))


def _softplus(v):
    return jnp.maximum(v, 0.0) + jnp.log(1.0 + jnp.exp(-jnp.abs(v)))


def _params(n_axes, side_effects=False):
    return pltpu.CompilerParams(dimension_semantics=("arbitrary",) * n_axes, vmem_limit_bytes=VMEM_LIMIT_BYTES,
                                has_side_effects=side_effects)


MESH_ID = pl.DeviceIdType.MESH
_ANY = pl.BlockSpec(memory_space=pl.ANY)


def _flip(v, bit):
    return 1 - v if bit else v


def _comm_counts(comm):
    return (0, 0) if comm is None else (len(comm["inputs"]), len(comm["out_shapes"]))


def _comm_call_args(comm):
    if comm is None:
        return [], [], [], []
    n = comm["n_xfers"]
    sems = [pltpu.SemaphoreType.DMA(((N_DEV - 1) * n,)), pltpu.SemaphoreType.DMA(((N_DEV - 1) * n,)),
            pltpu.SemaphoreType.DMA((n,))]
    return list(comm["inputs"]), [_ANY] * len(comm["out_shapes"]), list(comm["out_shapes"]), sems


def _comm_descs(comm, in_refs, tail_refs, with_recvs=True):
    n_out = len(comm["out_shapes"])
    out_refs, (send_sems, recv_sems, local_sems) = tail_refs[:n_out], tail_refs[n_out:n_out + 3]
    xfers = comm["xfers"](in_refs, out_refs)
    n = len(xfers)
    assert n == comm["n_xfers"]
    x, y, c = lax.axis_index("x"), lax.axis_index("y"), lax.axis_index("c")
    me = 4 * x + 2 * y + c
    local = [pltpu.make_async_copy(src_for(me), dst_for(me), local_sems.at[t]) for t, (src_for, dst_for) in enumerate(xfers)]
    sends, recvs = [], []
    for k in range(1, N_DEV):
        pid = (_flip(x, k & 4), _flip(y, k & 2), _flip(c, k & 1))
        peer = 4 * pid[0] + 2 * pid[1] + pid[2]
        for t, (src_for, dst_for) in enumerate(xfers):
            idx = (k - 1) * n + t
            sends.append(pltpu.make_async_remote_copy(
                src_ref=src_for(peer), dst_ref=dst_for(me), send_sem=send_sems.at[idx], recv_sem=recv_sems.at[idx],
                device_id=pid, device_id_type=MESH_ID))
            if with_recvs:
                recvs.append(pltpu.make_async_remote_copy(
                    src_ref=src_for(peer), dst_ref=dst_for(peer), send_sem=send_sems.at[idx], recv_sem=recv_sems.at[idx],
                    device_id=pid, device_id_type=MESH_ID))
    return local, sends, recvs


def _comm_start(descs):
    local, sends, _ = descs
    for cp in local + sends:
        cp.start()


def _comm_wait(descs):
    local, sends, recvs = descs
    for cp in recvs:
        cp.wait_recv()
    for cp in sends:
        cp.wait_send()
    for cp in local:
        cp.wait()


def _comm_hosted(comm, in_refs, tail_refs, grid):
    if comm is None:
        return
    ids = [pl.program_id(a) for a in range(len(grid))]
    first = functools.reduce(jnp.logical_and, [i == 0 for i in ids])
    last = functools.reduce(jnp.logical_and, [i == g - 1 for i, g in zip(ids, grid)])

    @pl.when(first)
    def _():
        _comm_start(_comm_descs(comm, in_refs, tail_refs, with_recvs=False))

    @pl.when(last)
    def _():
        _comm_wait(_comm_descs(comm, in_refs, tail_refs))


def _gather_call_two_level(comm, name):
    n_in, n_out = len(comm["inputs"]), len(comm["out_shapes"])

    def body(*refs):
        in_refs, out_refs = refs[:n_in], refs[n_in:n_in + n_out]
        send_sems, recv_sems, local_sems = refs[n_in + n_out:]
        xfers = comm["xfers"](in_refs, out_refs)
        x, y, c = lax.axis_index("x"), lax.axis_index("y"), lax.axis_index("c")
        pos = lambda px, py, pc: 4 * px + 2 * py + pc
        me, sibling = (x, y, c), (x, y, 1 - c)
        chips = [(1 - x, y), (x, 1 - y), (1 - x, 1 - y)]

        def copy(t, k, block, to, own):
            src_for, dst_for = xfers[t]
            return pltpu.make_async_remote_copy(
                src_ref=src_for(pos(*me)) if own else dst_for(pos(*block)), dst_ref=dst_for(pos(*block)),
                send_sem=send_sems.at[7 * t + k], recv_sem=recv_sems.at[7 * t + k], device_id=to, device_id_type=MESH_ID)

        nt = len(xfers)
        local = [pltpu.make_async_copy(xfers[t][0](pos(*me)), xfers[t][1](pos(*me)), local_sems.at[t]) for t in range(nt)]
        first = [copy(t, 0, me, sibling, True) for t in range(nt)]
        first += [copy(t, 1 + j, me, (*chip, c), True) for t in range(nt) for j, chip in enumerate(chips)]
        for cp in local + first:
            cp.start()
        passed = []
        for j, chip in enumerate(chips):
            for t in range(nt):
                copy(t, 1 + j, (*chip, c), me, False).wait_recv()
                fwd = copy(t, 4 + j, (*chip, c), sibling, False)
                fwd.start()
                passed.append(fwd)
        for t in range(nt):
            copy(t, 0, sibling, me, False).wait_recv()
            for j, chip in enumerate(chips):
                copy(t, 4 + j, (*chip, 1 - c), me, False).wait_recv()
        for cp in first + passed:
            cp.wait_send()
        for cp in local:
            cp.wait()

    n = comm["n_xfers"]
    return pl.pallas_call(
        body, name=name, in_specs=[_ANY] * n_in, out_specs=[_ANY] * n_out, out_shape=list(comm["out_shapes"]),
        scratch_shapes=[pltpu.SemaphoreType.DMA((7 * n,)), pltpu.SemaphoreType.DMA((7 * n,)), pltpu.SemaphoreType.DMA((n,))],
        compiler_params=pltpu.CompilerParams(has_side_effects=True))(*comm["inputs"])


def _rows(ref, j, n):
    return ref.at[pl.ds(pl.multiple_of(j * n, SUBLANES), n), :]


def _gather_job(sh, conv=None):
    conv = list(conv or [])
    sds = jax.ShapeDtypeStruct
    out_shapes = [sds((D_MODEL, H_COLS), MXU_DTYPE), sds((D_MODEL, D_MODEL), MXU_DTYPE), sds((N_DEV, D_MODEL, UP_SHARD), MXU_DTYPE),
                  sds((D_FF, D_MODEL), MXU_DTYPE), sds((D_MODEL, D_MODEL), MXU_DTYPE), sds((PLE_DIM, D_MODEL), MXU_DTYPE)]
    out_shapes += [sds((N_DEV,) + a.shape, a.dtype) for a in conv]

    def xfers(ins, outs):
        whole = lambda a: (lambda j: a)
        r = [(whole(ins[0]), lambda j: _rows(outs[0], j, 128)),
             (whole(ins[1]), lambda j: _rows(outs[1], lax.rem(j + 6, N_DEV), 128)),
             (whole(ins[2]), lambda j: outs[2].at[j]),
             (whole(ins[3]), lambda j: _rows(outs[3], j, DOWN_SHARD)),
             (whole(ins[4]), lambda j: _rows(outs[4], j, 128)),
             (whole(ins[5]), lambda j: outs[5].at[:, pl.ds(pl.multiple_of(j * LANES, LANES), LANES)])]
        for t in range(len(conv)):
            r.append((whole(ins[6 + t]), lambda j, o=outs[6 + t]: o.at[j]))
        return r

    return dict(inputs=[sh[n] for n in BIG] + conv, out_shapes=out_shapes, xfers=xfers, n_xfers=6 + len(conv))


_SHARD_SHAPES = {"w_in": (128, H_COLS), "w_out": (128, D_MODEL), "ffn_w_up": (D_MODEL, UP_SHARD), "ffn_w_down": (DOWN_SHARD, D_MODEL),
                 "ple_w_gate": (128, D_MODEL), "ple_w_proj": (PLE_DIM, LANES)}


def _exchange_job(items):
    def source(name, ref):
        if name in ("w_in", "ple_w_gate"):
            return lambda j: _rows(ref, j, 128)
        if name == "w_out":
            return lambda j: _rows(ref, lax.rem(j + 6, N_DEV), 128)
        if name == "ffn_w_up":
            return lambda j: ref.at[j]
        if name == "ffn_w_down":
            return lambda j: _rows(ref, j, DOWN_SHARD)
        assert name == "ple_w_proj"
        return lambda j: ref.at[:, pl.ds(pl.multiple_of(j * LANES, LANES), LANES)]

    def xfers(ins, outs):
        return [(source(name, i), lambda j, o=o: o.at[j]) for (name, _), i, o in zip(items, ins, outs)]

    return dict(inputs=[g for _, g in items], xfers=xfers, n_xfers=len(items),
                out_shapes=[jax.ShapeDtypeStruct((N_DEV,) + _SHARD_SHAPES[name], F32) for name, _ in items])


def _pick(n, pref):
    if n <= pref:
        return n
    for t in range(pref - pref % LANES, 0, -LANES):
        if n % t == 0:
            return t
    raise ValueError((n, pref))


def _mm(a, b, mode, out_dtype, name, tm=512, tn=512, tk=1024, add=None, add_coef=1.0, comm=None):
    n_in, n_out = _comm_counts(comm)
    if mode == "nn":
        (m, k), (k2, n) = a.shape, b.shape
    elif mode == "nt":
        (m, k), (n, k2) = a.shape, b.shape
    else:
        (k, m), (k2, n) = a.shape, b.shape
    assert k == k2, (a.shape, b.shape, mode)
    tm, tn, tk = _pick(m, tm), _pick(n, tn), _pick(k, tk)
    nk = k // tk
    dims = {"nn": NN, "nt": NT, "tn": TN}[mode]

    def body(*refs):
        a_ref, b_ref = refs[:2]
        n_add = int(add is not None)
        add_ref = refs[2] if n_add else None
        o_ref = refs[2 + n_add + n_in]
        tail = refs[3 + n_add + n_in:]
        if comm is not None:
            _comm_hosted(comm, refs[2 + n_add:2 + n_add + n_in], tail[:n_out] + tail[n_out + int(nk > 1):],
                         (m // tm, n // tn, nk))

        def finish(r):
            if add_ref is not None:
                r = r + add_coef * add_ref[...]
            o_ref[...] = r.astype(out_dtype)

        if nk == 1:
            finish(_dot(a_ref[...], b_ref[...], dims))
            return
        acc_ref = tail[n_out]
        kk = pl.program_id(2)

        @pl.when(kk == 0)
        def _():
            acc_ref[...] = jnp.zeros_like(acc_ref)

        acc_ref[...] += _dot(a_ref[...], b_ref[...], dims)

        @pl.when(kk == nk - 1)
        def _():
            finish(acc_ref[...])

    if mode == "tn":
        a_spec = pl.BlockSpec((tk, tm), lambda i, j, kk: (kk, i))
    else:
        a_spec = pl.BlockSpec((tm, tk), lambda i, j, kk: (i, kk))
    if mode == "nt":
        b_spec = pl.BlockSpec((tn, tk), lambda i, j, kk: (j, kk))
    else:
        b_spec = pl.BlockSpec((tk, tn), lambda i, j, kk: (kk, j))
    o_spec = pl.BlockSpec((tm, tn), lambda i, j, kk: (i, j))
    in_specs = [a_spec, b_spec] + ([o_spec] if add is not None else [])
    args = (a, b) + ((add,) if add is not None else ())
    c_in, c_specs, c_shapes, c_scratch = _comm_call_args(comm)
    res = pl.pallas_call(
        body, name=name, grid=(m // tm, n // tn, nk), in_specs=in_specs + [_ANY] * n_in, out_specs=[o_spec] + c_specs,
        out_shape=[jax.ShapeDtypeStruct((m, n), out_dtype)] + c_shapes,
        scratch_shapes=([pltpu.VMEM((tm, tn), F32)] if nk > 1 else []) + c_scratch,
        compiler_params=_params(3, comm is not None),
    )(*args, *c_in)
    return res[0] if comm is None else (res[0], res[1:])


def _mm_ln(a, b, x, gb, name, gp=None, pp=None, target=None, tm=512, tk=1024):
    (m, k), (k2, d) = a.shape, b.shape
    assert k == k2 and x.shape == (m, d)
    tm, tk = _pick(m, tm), _pick(k, tk)
    nk = k // tk
    with_ple = gp is not None
    with_loss = target is not None

    def body(*refs):
        a_ref, b_ref, x_ref = refs[:3]
        gp_ref, pp_ref = refs[3:5] if with_ple else (None, None)
        t_ref = refs[3 + 2 * with_ple] if with_loss else None
        gb_ref, y_ref, r_ref = refs[3 + 2 * with_ple + with_loss:6 + 2 * with_ple + with_loss]
        l_ref = refs[-2] if with_loss else None
        acc_ref = refs[-1]
        kk = pl.program_id(1)

        if with_loss:
            @pl.when(jnp.logical_and(pl.program_id(0) == 0, kk == 0))
            def _():
                l_ref[...] = jnp.zeros_like(l_ref)

        @pl.when(kk == 0)
        def _():
            acc_ref[...] = jnp.zeros_like(acc_ref)

        acc_ref[...] += _dot(a_ref[...], b_ref[...])

        @pl.when(kk == nk - 1)
        def _():
            r = ALPHA * x_ref[...] + acc_ref[...]
            if with_ple:
                r = r + _sigmoid(gp_ref[...]) * pp_ref[...]
            mu = jnp.mean(r, axis=1, keepdims=True)
            xc = r - mu
            var = jnp.mean(xc * xc, axis=1, keepdims=True)
            y = xc * lax.rsqrt(var + LN_EPS) * gb_ref[0:1, :] + gb_ref[1:2, :]
            r_ref[...] = r
            if with_loss:
                e = y - t_ref[...]
                y_ref[...] = e * (1.0 / d)
                l_ref[...] += 0.5 * jnp.sum(jnp.mean(e * e, axis=1, keepdims=True), axis=0, keepdims=True)
            else:
                y_ref[...] = y

    row = pl.BlockSpec((tm, d), lambda i, kk: (i, 0))
    in_specs = [pl.BlockSpec((tm, tk), lambda i, kk: (i, kk)), pl.BlockSpec((tk, d), lambda i, kk: (kk, 0)), row]
    in_specs += ([row, row] if with_ple else []) + ([row] if with_loss else []) + [pl.BlockSpec((2, d), lambda i, kk: (0, 0))]
    args = (a, b, x) + ((gp, pp) if with_ple else ()) + ((target,) if with_loss else ()) + (gb,)
    loss_spec = [pl.BlockSpec((SUBLANES, LANES), lambda i, kk: (0, 0))] if with_loss else []
    loss_shape = [jax.ShapeDtypeStruct((SUBLANES, LANES), F32)] if with_loss else []
    return pl.pallas_call(
        body, name=name, grid=(m // tm, nk), in_specs=in_specs, out_specs=[row, row] + loss_spec,
        out_shape=[jax.ShapeDtypeStruct((m, d), F32)] * 2 + loss_shape, scratch_shapes=[pltpu.VMEM((tm, d), F32)],
        compiler_params=_params(2),
    )(*args)


def _ln_bwd(r, gb, dy, name, gp=None, pp=None, tr=512):
    t, d = r.shape
    tr = _pick(t, tr)
    with_ple = gp is not None

    def body(*refs):
        if with_ple:
            r_ref, dy_ref, gp_ref, pp_ref, gb_ref, dr_ref, dgp_ref, dpp_ref, st_ref = refs
        else:
            r_ref, dy_ref, gb_ref, dr_ref, st_ref = refs
        i = pl.program_id(0)

        @pl.when(i == 0)
        def _():
            st_ref[...] = jnp.zeros_like(st_ref)

        rv = r_ref[...]
        dy_v = dy_ref[...]
        mu = jnp.mean(rv, axis=1, keepdims=True)
        xc = rv - mu
        var = jnp.mean(xc * xc, axis=1, keepdims=True)
        rstd = lax.rsqrt(var + LN_EPS)
        xhat = xc * rstd
        dxh = dy_v * gb_ref[0:1, :]
        m1 = jnp.mean(dxh, axis=1, keepdims=True)
        m2 = jnp.mean(dxh * xhat, axis=1, keepdims=True)
        dr = rstd * (dxh - m1 - xhat * m2)
        dr_ref[...] = dr
        rid = lax.broadcasted_iota(jnp.int32, (2, d), 0)
        dg = jnp.sum(dy_v * xhat, axis=0, keepdims=True)
        db = jnp.sum(dy_v, axis=0, keepdims=True)
        st_ref[...] += jnp.where(rid == 0, dg, db)
        if with_ple:
            sg = _sigmoid(gp_ref[...])
            ppv = pp_ref[...]
            dgp_ref[...] = (dr * ppv * sg * (1.0 - sg)).astype(dgp_ref.dtype)
            dpp_ref[...] = (dr * sg).astype(dpp_ref.dtype)

    row = pl.BlockSpec((tr, d), lambda i: (i, 0))
    vec = pl.BlockSpec((2, d), lambda i: (0, 0))
    if with_ple:
        in_specs, args = [row] * 4 + [vec], (r, dy, gp, pp, gb)
        out_specs = [row, row, row, vec]
        out_shape = [jax.ShapeDtypeStruct((t, d), F32), jax.ShapeDtypeStruct((t, d), MXU_DTYPE),
                     jax.ShapeDtypeStruct((t, d), MXU_DTYPE), jax.ShapeDtypeStruct((2, d), F32)]
    else:
        in_specs, args = [row] * 2 + [vec], (r, dy, gb)
        out_specs = [row, vec]
        out_shape = [jax.ShapeDtypeStruct((t, d), F32), jax.ShapeDtypeStruct((2, d), F32)]
    return pl.pallas_call(body, name=name, grid=(t // tr,), in_specs=in_specs, out_specs=out_specs,
                          out_shape=out_shape, compiler_params=_params(1))(*args)


def _shift_down(v, k, row):
    return jnp.where(row >= k, pltpu.roll(v, k, 0), 0.0)


def _shift_up(v, k, row):
    n = v.shape[0]
    return jnp.where(row < n - k, pltpu.roll(v, n - k, 0), 0.0)


def _pool_window(lane):
    grp = lane // HEAD_DIM
    return jnp.where(grp == 0, POOL_WINDOWS[0], jnp.where(grp == 1, POOL_WINDOWS[1],
                     jnp.where(grp == 2, POOL_WINDOWS[2], POOL_WINDOWS[3])))


def _pool_select(lane, s2, s4, s8, s16):
    grp = lane // HEAD_DIM
    return jnp.where(grp == 0, s2, jnp.where(grp == 1, s4, jnp.where(grp == 2, s8, s16)))


def _pooled(u, row, lane):
    s2 = u + _shift_down(u, 1, row)
    s4 = s2 + _shift_down(s2, 2, row)
    s8 = s4 + _shift_down(s4, 4, row)
    s16 = s8 + _shift_down(s8, 8, row)
    cnt = jnp.minimum(row + 1, _pool_window(lane)).astype(F32)
    return _pool_select(lane, s2, s4, s8, s16) / cnt - u, cnt


def _pool_fwd(h, wbd, scale, nb, s, name):
    def body(u_ref, w_ref, sc_ref, o_ref):
        u = u_ref[...]
        row = lax.broadcasted_iota(jnp.int32, u.shape, 0)
        lane = lax.broadcasted_iota(jnp.int32, u.shape, 1)
        pooled, _ = _pooled(u, row, lane)
        o_ref[...] = (_dot(pooled, w_ref[...]) * sc_ref[...]).astype(o_ref.dtype)

    wb = POOL_WIDTH
    return pl.pallas_call(
        body, name=name, grid=(nb,),
        in_specs=[pl.BlockSpec((s, wb), lambda b: (b, H_POOL // wb)), pl.BlockSpec((wb, wb), lambda b: (0, 0)),
                  pl.BlockSpec((1, wb), lambda b: (0, 0))],
        out_specs=pl.BlockSpec((s, wb), lambda b: (b, 0)),
        out_shape=jax.ShapeDtypeStruct((nb * s, wb), MXU_DTYPE), compiler_params=_params(1),
    )(h, wbd, scale)


def _pool_bwd(h, dmix, wbd, scale, nb, s, name):
    wb = POOL_WIDTH

    def body(u_ref, do_ref, w_ref, sc_ref, du_ref, dw_ref, ds_ref):
        b = pl.program_id(0)

        @pl.when(b == 0)
        def _():
            dw_ref[...] = jnp.zeros_like(dw_ref)
            ds_ref[...] = jnp.zeros_like(ds_ref)

        u = u_ref[...]
        row = lax.broadcasted_iota(jnp.int32, u.shape, 0)
        lane = lax.broadcasted_iota(jnp.int32, u.shape, 1)
        pooled, cnt = _pooled(u, row, lane)
        mixed = _dot(pooled, w_ref[...])
        do = do_ref[...]
        ds_ref[...] += jnp.sum(do * mixed, axis=0, keepdims=True)
        dm = do * sc_ref[...]
        dw_ref[...] += _dot(pooled, dm, TN)
        dpool = _dot(dm, w_ref[...], NT)
        qv = dpool / cnt
        f2 = qv + _shift_up(qv, 1, row)
        f4 = f2 + _shift_up(f2, 2, row)
        f8 = f4 + _shift_up(f4, 4, row)
        f16 = f8 + _shift_up(f8, 8, row)
        du_ref[...] = (_pool_select(lane, f2, f4, f8, f16) - dpool).astype(du_ref.dtype)

    return pl.pallas_call(
        body, name=name, grid=(nb,),
        in_specs=[pl.BlockSpec((s, wb), lambda b: (b, H_POOL // wb)), pl.BlockSpec((s, wb), lambda b: (b, 3)),
                  pl.BlockSpec((wb, wb), lambda b: (0, 0)), pl.BlockSpec((1, wb), lambda b: (0, 0))],
        out_specs=[pl.BlockSpec((s, wb), lambda b: (b, 0)), pl.BlockSpec((wb, wb), lambda b: (0, 0)),
                   pl.BlockSpec((1, wb), lambda b: (0, 0))],
        out_shape=[jax.ShapeDtypeStruct((nb * s, wb), MXU_DTYPE), jax.ShapeDtypeStruct((wb, wb), F32),
                   jax.ShapeDtypeStruct((1, wb), F32)],
        compiler_params=_params(1),
    )(h, dmix, wbd, scale)


def _glu_conv(x, w_ref, b_ref, row):
    return (b_ref[...] + w_ref[2:3, :] * x + w_ref[1:2, :] * _shift_down(x, 1, row)
            + w_ref[0:1, :] * _shift_down(x, 2, row))


def _glu_fwd(up, cw, cb, nb, s, name):
    wt = 2 * GLU_TILE
    nt = up.shape[1] // wt

    def body(u_ref, w_ref, b_ref, o_ref):
        x = u_ref[...]
        row = lax.broadcasted_iota(jnp.int32, x.shape, 0)
        c = _glu_conv(x, w_ref, b_ref, row)
        o_ref[...] = (_silu(c[:, :GLU_TILE]) * c[:, GLU_TILE:]).astype(o_ref.dtype)

    return pl.pallas_call(
        body, name=name, grid=(nt, nb),
        in_specs=[pl.BlockSpec((s, wt), lambda j, b: (b, j)), pl.BlockSpec((3, wt), lambda j, b: (0, j)),
                  pl.BlockSpec((1, wt), lambda j, b: (0, j))],
        out_specs=pl.BlockSpec((s, GLU_TILE), lambda j, b: (b, j)),
        out_shape=jax.ShapeDtypeStruct((nb * s, nt * GLU_TILE), MXU_DTYPE), compiler_params=_params(2),
    )(up, cw, cb)


def _glu_bwd(up, dact, cw, cb, nb, s, name):
    wt = 2 * GLU_TILE
    nt = up.shape[1] // wt

    def body(u_ref, da_ref, w_ref, b_ref, du_ref, acc_ref):
        b = pl.program_id(1)

        @pl.when(b == 0)
        def _():
            acc_ref[...] = jnp.zeros_like(acc_ref)

        x = u_ref[...]
        row = lax.broadcasted_iota(jnp.int32, x.shape, 0)
        x1 = _shift_down(x, 1, row)
        x2 = _shift_down(x, 2, row)
        c = b_ref[...] + w_ref[2:3, :] * x + w_ref[1:2, :] * x1 + w_ref[0:1, :] * x2
        gate, val = c[:, :GLU_TILE], c[:, GLU_TILE:]
        da = da_ref[...]
        dc = jnp.concatenate([da * val * _dsilu(gate), da * _silu(gate)], axis=1)
        dx = (w_ref[2:3, :] * dc + w_ref[1:2, :] * _shift_up(dc, 1, row) + w_ref[0:1, :] * _shift_up(dc, 2, row))
        du_ref[...] = dx.astype(du_ref.dtype)
        rid = lax.broadcasted_iota(jnp.int32, (SUBLANES, wt), 0)
        dw0 = jnp.sum(dc * x2, axis=0, keepdims=True)
        dw1 = jnp.sum(dc * x1, axis=0, keepdims=True)
        dw2 = jnp.sum(dc * x, axis=0, keepdims=True)
        db = jnp.sum(dc, axis=0, keepdims=True)
        acc_ref[...] += (jnp.where(rid == 0, dw0, 0.0) + jnp.where(rid == 1, dw1, 0.0)
                         + jnp.where(rid == 2, dw2, 0.0) + jnp.where(rid == 3, db, 0.0))

    return pl.pallas_call(
        body, name=name, grid=(nt, nb),
        in_specs=[pl.BlockSpec((s, wt), lambda j, b: (b, j)), pl.BlockSpec((s, GLU_TILE), lambda j, b: (b, j)),
                  pl.BlockSpec((3, wt), lambda j, b: (0, j)), pl.BlockSpec((1, wt), lambda j, b: (0, j))],
        out_specs=[pl.BlockSpec((s, wt), lambda j, b: (b, j)), pl.BlockSpec((SUBLANES, wt), lambda j, b: (0, j))],
        out_shape=[jax.ShapeDtypeStruct((nb * s, nt * wt), MXU_DTYPE), jax.ShapeDtypeStruct((SUBLANES, nt * wt), F32)],
        compiler_params=_params(2),
    )(up, dact, cw, cb)


def _sb_constants():
    row = lax.broadcasted_iota(jnp.int32, (QB, QB), 0)
    col = lax.broadcasted_iota(jnp.int32, (QB, QB), 1)
    return jnp.stack([row > col, row < col, col < row]).astype(BF16)


_SB_CONST_SPEC = pl.BlockSpec((3, QB, QB), lambda b, p, i: (0, 0, 0))


def _sb_fwd(h, nb, s, name, comm=None):
    nq = s // QB
    scale = HEAD_DIM ** -0.5
    n_in, n_out = _comm_counts(comm)

    def body(q_ref, k_ref, v_ref, tri_ref, *rest):
        o_ref = rest[n_in]
        i = pl.program_id(2)
        _comm_hosted(comm, rest[:n_in], rest[n_in + 1:], (nb, 3, nq))
        sls = [slice(hd * HEAD_DIM, (hd + 1) * HEAD_DIM) for hd in range(2)]
        qs = [(q_ref[:, sl] * scale).astype(MXU_DTYPE) for sl in sls]

        def scores(hd, j, diagonal=False):
            r0 = pl.multiple_of(j * QB, QB)
            z = _dot(qs[hd], k_ref[pl.ds(r0, QB), sls[hd]], NT)
            ln = -_softplus(z)
            ls = z + ln
            if diagonal:
                low = tri_ref[2] > 0
                ln = jnp.where(low, ln, 0.0)
                ls = jnp.where(low, ls, MASKED_LOG)
            return ls, _dot_exact01(ln, tri_ref[0], terms=2), jnp.sum(ln, axis=1, keepdims=True)

        def output(hd, j, ls, tl, ct):
            r0 = pl.multiple_of(j * QB, QB)
            return _dot(jnp.exp(ls + tl + ct), v_ref[pl.ds(r0, QB), sls[hd]])

        def group(blocks, carry, diagonal_first=False):
            sc = [[scores(hd, j, diagonal_first and n == 0) for n, j in enumerate(blocks)] for hd in range(2)]
            out = []
            for hd in range(2):
                a, c = carry[hd]
                for (ls, tl, sm), j in zip(sc[hd], blocks):
                    a = a + output(hd, j, ls, tl, c)
                    c = c + sm
                out.append((a, c))
            return tuple(out)

        start = (jnp.zeros((QB, HEAD_DIM), F32), jnp.zeros((QB, 1), F32))
        below = jnp.minimum(i, 1)
        left = i - below
        carry = lax.fori_loop(0, below, lambda t, c: group([i, i - 1], c, True), (start, start))
        carry = lax.fori_loop(0, 1 - below, lambda t, c: group([i], c, True), carry)
        carry = lax.fori_loop(0, left // 2, lambda t, c: group([left - 1 - 2 * t, left - 2 - 2 * t], c), carry)
        carry = lax.fori_loop(0, left % 2, lambda t, c: group([0], c), carry)
        o_ref[:, sls[0]] = carry[0][0].astype(o_ref.dtype)
        o_ref[:, sls[1]] = carry[1][0].astype(o_ref.dtype)

    qspec = lambda off: pl.BlockSpec((QB, LANES), lambda b, p, i: (b * nq + i, off // LANES + p))
    kvspec = lambda off: pl.BlockSpec((s, LANES), lambda b, p, i: (b, off // LANES + p))
    c_in, c_specs, c_shapes, c_scratch = _comm_call_args(comm)
    res = pl.pallas_call(
        body, name=name, grid=(nb, 3, nq), in_specs=[qspec(H_Q), kvspec(H_K), kvspec(H_V), _SB_CONST_SPEC] + [_ANY] * n_in,
        out_specs=[pl.BlockSpec((QB, LANES), lambda b, p, i: (b * nq + i, p))] + c_specs,
        out_shape=[jax.ShapeDtypeStruct((nb * s, SB_WIDTH), MXU_DTYPE)] + c_shapes, scratch_shapes=c_scratch,
        compiler_params=_params(3, comm is not None),
    )(h, h, h, _sb_constants(), *c_in)
    return res[0], res[1:]


def _sb_bwd(h, dmix, nb, s, name, comm=None):
    nq = s // QB
    scale = HEAD_DIM ** -0.5
    n_in, n_out = _comm_counts(comm)

    def body(q_ref, k_ref, v_ref, do_ref, tri_ref, *rest):
        dq_ref, dk_out, dv_out = rest[n_in:n_in + 3]
        p_buf, ls_buf, dk_ref, dv_ref = rest[n_in + 3 + n_out:n_in + 7 + n_out]
        i = pl.program_id(2)
        _comm_hosted(comm, rest[:n_in], rest[n_in + 3:n_in + 3 + n_out] + rest[n_in + 7 + n_out:], (nb, 3, nq))

        @pl.when(i == 0)
        def _():
            dk_ref[...] = jnp.zeros_like(dk_ref)
            dv_ref[...] = jnp.zeros_like(dv_ref)

        sls = [slice(hd * HEAD_DIM, (hd + 1) * HEAD_DIM) for hd in range(2)]
        q_raw = [q_ref[:, sl].astype(MXU_DTYPE) for sl in sls]
        qs = [(q_ref[:, sl] * scale).astype(MXU_DTYPE) for sl in sls]
        do = [do_ref[:, sl].astype(MXU_DTYPE) for sl in sls]

        def down_scores(hd, j, diagonal):
            r0 = pl.multiple_of(j * QB, QB)
            z = _dot(qs[hd], k_ref[pl.ds(r0, QB), sls[hd]], NT)
            ln = -_softplus(z)
            ls = z + ln
            if diagonal:
                low = tri_ref[2] > 0
                ln = jnp.where(low, ln, 0.0)
                ls = jnp.where(low, ls, MASKED_LOG)
            da = _dot(do[hd], v_ref[pl.ds(r0, QB), sls[hd]], NT)
            return ls, _dot_exact01(ln, tri_ref[0], terms=2), jnp.sum(ln, axis=1, keepdims=True), da

        def down_group(blocks, carry, diagonal_first=False):
            sc = [[down_scores(hd, j, diagonal_first and n == 0) for n, j in enumerate(blocks)] for hd in range(2)]
            out = []
            for hd in range(2):
                ct = carry[hd]
                for (ls, tl, sm, da), j in zip(sc[hd], blocks):
                    r0 = pl.multiple_of(j * QB, QB)
                    a = jnp.exp(ls + tl + ct)
                    p_buf[hd, j] = da * a
                    ls_buf[hd, j] = ls
                    dv_ref[pl.ds(r0, QB), sls[hd]] += _dot(a, do[hd], TN)
                    ct = ct + sm
                out.append(ct)
            return tuple(out)

        zero = jnp.zeros((QB, 1), F32)
        below = jnp.minimum(i, 1)
        left = i - below
        carry = lax.fori_loop(0, below, lambda t, c: down_group([i, i - 1], c, True), (zero, zero))
        carry = lax.fori_loop(0, 1 - below, lambda t, c: down_group([i], c, True), carry)
        carry = lax.fori_loop(0, left // 2, lambda t, c: down_group([left - 1 - 2 * t, left - 2 - 2 * t], c), carry)
        lax.fori_loop(0, left % 2, lambda t, c: down_group([0], c), carry)

        def up_group(blocks, carry):
            ld = []
            for hd in range(2):
                ld.append([])
                for j in blocks:
                    pj = p_buf[hd, j]
                    ld[hd].append((pj, jnp.exp(ls_buf[hd, j]), _dot_exact01(pj, tri_ref[1]), jnp.sum(pj, axis=1, keepdims=True)))
            out = []
            for hd in range(2):
                dq, cp = carry[hd]
                for (pj, sg, cm, sm), j in zip(ld[hd], blocks):
                    r0 = pl.multiple_of(j * QB, QB)
                    dz = (pj * (1.0 - sg) - (cp + cm) * sg) * scale
                    dk_ref[pl.ds(r0, QB), sls[hd]] += _dot(dz, q_raw[hd], TN)
                    dq = dq + _dot(dz, k_ref[pl.ds(r0, QB), sls[hd]])
                    cp = cp + sm
                out.append((dq, cp))
            return tuple(out)

        start = (jnp.zeros((QB, HEAD_DIM), F32), zero)
        odd = (i + 1) % 2
        carry = lax.fori_loop(0, odd, lambda t, c: up_group([0], c), (start, start))
        carry = lax.fori_loop(0, (i + 1) // 2, lambda t, c: up_group([odd + 2 * t, odd + 2 * t + 1], c), carry)
        dq_ref[:, sls[0]] = carry[0][0].astype(dq_ref.dtype)
        dq_ref[:, sls[1]] = carry[1][0].astype(dq_ref.dtype)

        @pl.when(i == nq - 1)
        def _():
            dk_out[...] = dk_ref[...].astype(dk_out.dtype)
            dv_out[...] = dv_ref[...].astype(dv_out.dtype)

    qspec = lambda off: pl.BlockSpec((QB, LANES), lambda b, p, i: (b * nq + i, off // LANES + p))
    kvspec = lambda off: pl.BlockSpec((s, LANES), lambda b, p, i: (b, off // LANES + p))
    blk_out = pl.BlockSpec((QB, LANES), lambda b, p, i: (b * nq + i, p))
    seq_out = pl.BlockSpec((s, LANES), lambda b, p, i: (b, p))
    shp = jax.ShapeDtypeStruct((nb * s, SB_WIDTH), MXU_DTYPE)
    c_in, c_specs, c_shapes, c_scratch = _comm_call_args(comm)
    res = pl.pallas_call(
        body, name=name, grid=(nb, 3, nq),
        in_specs=[qspec(H_Q), kvspec(H_K), kvspec(H_V), pl.BlockSpec((QB, LANES), lambda b, p, i: (b * nq + i, 3 + p)),
                  _SB_CONST_SPEC] + [_ANY] * n_in,
        out_specs=[blk_out, seq_out, seq_out] + c_specs, out_shape=[shp, shp, shp] + c_shapes,
        scratch_shapes=[pltpu.VMEM((2, nq, QB, QB), F32), pltpu.VMEM((2, nq, QB, QB), F32),
                        pltpu.VMEM((s, LANES), F32), pltpu.VMEM((s, LANES), F32)] + c_scratch,
        compiler_params=_params(3, comm is not None),
    )(h, h, h, dmix, _sb_constants(), *c_in)
    return res[0], res[1], res[2], res[3:]


def _ssd_conv(cur_ref, halo_ref, w_ref, b_ref, ext_ref, first):
    n = SSD_CHUNK
    cur = cur_ref[...]
    ext_ref[0:SUBLANES, :] = jnp.where(first, 0.0, halo_ref[...])
    ext_ref[SUBLANES:SUBLANES + n, :] = cur
    return (b_ref[...] + w_ref[3:4, :] * cur + w_ref[2:3, :] * ext_ref[pl.ds(SUBLANES - 1, n), :]
            + w_ref[1:2, :] * ext_ref[pl.ds(SUBLANES - 2, n), :] + w_ref[0:1, :] * ext_ref[pl.ds(SUBLANES - 3, n), :])


def _ssd_tri():
    row = lax.broadcasted_iota(jnp.int32, (SSD_CHUNK, SSD_CHUNK), 0)
    col = lax.broadcasted_iota(jnp.int32, (SSD_CHUNK, SSD_CHUNK), 1)
    return row, col


def _ssd_specs(nc, rev):
    n = SSD_CHUNK
    hb = n // SUBLANES

    def cidx(c):
        return (nc - 1 - c) if rev else c

    def blk(width, off):
        return pl.BlockSpec((n, width), lambda b, c: (b * nc + cidx(c), off // width))

    def halo(width, off):
        return pl.BlockSpec((SUBLANES, width), lambda b, c: (jnp.maximum((b * nc + cidx(c)) * hb - 1, 0), off // width))

    def full(shape):
        return pl.BlockSpec(shape, lambda b, c: (0,) * len(shape))

    return cidx, blk, halo, full


def _ssd_core_fwd(x, bc, dt, acum, acum_t, a_row, d_row, h_prev_ref, tri):
    n = SSD_CHUNK
    heads = []
    for g in range(2):
        bm = bc[:, g * SSD_STATE:(g + 1) * SSD_STATE]
        cm = bc[:, 2 * SSD_STATE + g * SSD_STATE: 2 * SSD_STATE + (g + 1) * SSD_STATE]
        gmat = _dot(cm, bm, NT)
        for r in range(3):
            hh = g * 3 + r
            hp = h_prev_ref[hh * HEAD_DIM:(hh + 1) * HEAD_DIM, :]
            heads.append(dict(g=g, hh=hh, bm=bm, cm=cm, gmat=gmat, hp=hp, cmh=_dot(cm, hp, NT)))
    for hd in heads:
        hh = hd["hh"]
        ac = acum[:, hh:hh + 1]
        ar = acum_t[hh:hh + 1, :]
        hd["dec"] = jnp.where(tri, jnp.exp(jnp.minimum(ac - ar, 0.0)), 0.0)
        hd["xh"] = x[:, hh * HEAD_DIM:(hh + 1) * HEAD_DIM]
        hd["dth"] = dt[:, hh:hh + 1]
        hd["xdt"] = hd["xh"] * hd["dth"]
        hd["ea"] = jnp.exp(ac)
        hd["m"] = hd["gmat"] * hd["dec"]
        hd["al"] = acum[n - 1:n, hh:hh + 1]
        hd["w"] = jnp.exp(hd["al"] - ac)
    for hd in heads:
        hd["yd"] = _dot(hd["m"], hd["xdt"])
    for hd in heads:
        hd["yo"] = hd["ea"] * hd["cmh"]
        hd["y"] = hd["yd"] + hd["yo"] + d_row[:, hd["hh"]:hd["hh"] + 1] * hd["xh"]
    return heads


def _ssd_prep(xs_ref, xsh_ref, bc_ref, bch_ref, dt_ref, cwx_ref, cbx_ref, cwb_ref, cbb_ref, vec_ref, xe_ref, be_ref, first):
    pre_x = _ssd_conv(xs_ref, xsh_ref, cwx_ref, cbx_ref, xe_ref, first)
    pre_bc = _ssd_conv(bc_ref, bch_ref, cwb_ref, cbb_ref, be_ref, first)
    x = _silu(pre_x)
    bc = _silu(pre_bc)
    dt_pre = dt_ref[...] + vec_ref[0:1, :]
    dt = _softplus(dt_pre)
    a_row = vec_ref[1:2, :]
    amat = dt * a_row
    row, col = _ssd_tri()
    upper = (row <= col).astype(BF16)
    lower = (col <= row).astype(BF16)
    acum = _dot_exact01(amat, lower, NN, x_left=False)
    acum_t = _dot_exact01(amat, upper, TN, x_left=True)
    return pre_x, pre_bc, x, bc, dt_pre, dt, a_row, acum, acum_t, row, col, upper


def _ssd_gate_norm(y, z, nw):
    lane = lax.broadcasted_iota(jnp.int32, y.shape, 1)
    g0 = lane < SSD_WIDTH // 2
    hg = y * _silu(z)
    sq = hg * hg
    ms0 = jnp.sum(jnp.where(g0, sq, 0.0), axis=1, keepdims=True) * (2.0 / SSD_WIDTH)
    ms1 = jnp.sum(jnp.where(g0, 0.0, sq), axis=1, keepdims=True) * (2.0 / SSD_WIDTH)
    rs = jnp.where(g0, lax.rsqrt(ms0 + RMS_EPS), lax.rsqrt(ms1 + RMS_EPS))
    return hg, rs, g0


def _ssd_fwd(h, cwx, cbx, cwb, cbb, vec, nw, nb, s, name):
    n = SSD_CHUNK
    nc = s // n
    _, blk, halo, full = _ssd_specs(nc, False)

    def body(bc_ref, bch_ref, z_ref, xs_ref, xsh_ref, dt_ref, cwx_ref, cbx_ref, cwb_ref, cbb_ref, vec_ref, nw_ref,
             o_ref, hs_ref, h_scr, xe_ref, be_ref, y_scr):
        c = pl.program_id(1)

        @pl.when(c == 0)
        def _():
            h_scr[...] = jnp.zeros_like(h_scr)

        (_, _, x, bc, _, dt, a_row, acum, acum_t, row, col, _) = _ssd_prep(
            xs_ref, xsh_ref, bc_ref, bch_ref, dt_ref, cwx_ref, cbx_ref, cwb_ref, cbb_ref, vec_ref, xe_ref, be_ref, c == 0)
        hs_ref[...] = h_scr[...]
        heads = _ssd_core_fwd(x, bc, dt, acum, acum_t, a_row, vec_ref[2:3, :], hs_ref, col <= row)
        for hd in heads:
            sl = slice(hd["hh"] * HEAD_DIM, (hd["hh"] + 1) * HEAD_DIM)
            y_scr[:, sl] = hd["y"]
            h_scr[sl, :] = jnp.exp(hd["al"]) * hd["hp"] + _dot(hd["xdt"] * hd["w"], hd["bm"], TN)
        hg, rs, _ = _ssd_gate_norm(y_scr[...], z_ref[...], nw_ref[...])
        o_ref[...] = (hg * rs * nw_ref[...]).astype(o_ref.dtype)

    t = nb * s
    return pl.pallas_call(
        body, name=name, grid=(nb, nc),
        in_specs=[blk(512, H_BC), halo(512, H_BC), blk(384, H_Z), blk(384, H_XS), halo(384, H_XS), blk(128, H_DT),
                  full((4, 384)), full((1, 384)), full((4, 512)), full((1, 512)), full((SUBLANES, LANES)), full((1, 384))],
        out_specs=[pl.BlockSpec((n, SSD_WIDTH), lambda b, c: (b * nc + c, 0)),
                   pl.BlockSpec((None, SSD_WIDTH, SSD_STATE), lambda b, c: (b * nc + c, 0, 0))],
        out_shape=[jax.ShapeDtypeStruct((t, SSD_WIDTH), MXU_DTYPE),
                   jax.ShapeDtypeStruct((nb * nc, SSD_WIDTH, SSD_STATE), F32)],
        scratch_shapes=[pltpu.VMEM((SSD_WIDTH, SSD_STATE), F32), pltpu.VMEM((n + SUBLANES, 384), F32),
                        pltpu.VMEM((n + SUBLANES, 512), F32), pltpu.VMEM((n, SSD_WIDTH), F32)],
        compiler_params=_params(2),
    )(h, h, h, h, h, h, cwx, cbx, cwb, cbb, vec, nw)


def _ssd_bwd(h, hstate, dmix, cwx, cbx, cwb, cbb, vec, nw, nb, s, name):
    n = SSD_CHUNK
    nc = s // n
    cidx, blk, halo, full = _ssd_specs(nc, True)

    def body(bc_ref, bch_ref, z_ref, xs_ref, xsh_ref, dt_ref, hs_ref, do_ref, cwx_ref, cbx_ref, cwb_ref, cbb_ref,
             vec_ref, nw_ref, dz_ref, dxs_ref, dbc_ref, ddt_ref, gx_ref, gb_ref, gv_ref, gn_ref,
             dh_scr, xe_ref, be_ref, y_scr, dx_scr, dbc_scr, dxe_ref, dbe_ref, cx_ref, cb_ref):
        b = pl.program_id(0)
        c = pl.program_id(1)
        cc = nc - 1 - c

        @pl.when(jnp.logical_and(b == 0, c == 0))
        def _():
            gx_ref[...] = jnp.zeros_like(gx_ref)
            gb_ref[...] = jnp.zeros_like(gb_ref)
            gv_ref[...] = jnp.zeros_like(gv_ref)
            gn_ref[...] = jnp.zeros_like(gn_ref)

        @pl.when(c == 0)
        def _():
            dh_scr[...] = jnp.zeros_like(dh_scr)
            cx_ref[...] = jnp.zeros_like(cx_ref)
            cb_ref[...] = jnp.zeros_like(cb_ref)

        (pre_x, pre_bc, x, bc, dt_pre, dt, a_row, acum, acum_t, row, col, upper) = _ssd_prep(
            xs_ref, xsh_ref, bc_ref, bch_ref, dt_ref, cwx_ref, cbx_ref, cwb_ref, cbb_ref, vec_ref, xe_ref, be_ref, cc == 0)
        tri = col <= row
        d_row = vec_ref[2:3, :]
        heads = _ssd_core_fwd(x, bc, dt, acum, acum_t, a_row, d_row, hs_ref, tri)
        for hd in heads:
            y_scr[:, hd["hh"] * HEAD_DIM:(hd["hh"] + 1) * HEAD_DIM] = hd["y"]
        y = y_scr[...]
        z = z_ref[...]
        nwv = nw_ref[...]
        hg, rs, g0 = _ssd_gate_norm(y, z, nwv)
        do = do_ref[...]
        nrm = hg * rs
        gn_ref[...] += jnp.sum(do * nrm, axis=0, keepdims=True)
        dn = do * nwv
        dnn = dn * nrm
        mean0 = jnp.sum(jnp.where(g0, dnn, 0.0), axis=1, keepdims=True) * (2.0 / SSD_WIDTH)
        mean1 = jnp.sum(jnp.where(g0, 0.0, dnn), axis=1, keepdims=True) * (2.0 / SSD_WIDTH)
        dhg = rs * (dn - nrm * jnp.where(g0, mean0, mean1))
        dz_ref[...] = (dhg * y * _dsilu(z)).astype(dz_ref.dtype)
        dy = dhg * _silu(z)

        lane = lax.broadcasted_iota(jnp.int32, (n, LANES), 1)
        lane1 = lax.broadcasted_iota(jnp.int32, (1, LANES), 1)
        last_row = lax.broadcasted_iota(jnp.int32, (n, 1), 0) == n - 1
        dacum_col = jnp.zeros((n, LANES), F32)
        da_rowpart = jnp.zeros((n, LANES), F32)
        ddt = jnp.zeros((n, LANES), F32)
        dd_vec = jnp.zeros((1, LANES), F32)
        for hd in heads:
            sl = slice(hd["hh"] * HEAD_DIM, (hd["hh"] + 1) * HEAD_DIM)
            dyh = dy[:, sl]
            dhn = dh_scr[sl, :]
            hd.update(sl=sl, dyh=dyh, dhn=dhn, t1=_dot(dyh, hd["hp"]), dm=_dot(dyh, hd["xdt"], NT),
                      t2=_dot(hd["bm"], dhn, NT), mtdy=_dot(hd["m"], dyh, TN), xdhn=_dot(hd["xdt"], dhn),
                      dhp=_dot(dyh * hd["ea"], hd["cm"], TN))
        dgs, dbms, dcms = [], [], []
        for g in range(2):
            dg = jnp.zeros((n, n), F32)
            dbm = jnp.zeros((n, SSD_STATE), F32)
            dcm = jnp.zeros((n, SSD_STATE), F32)
            for hd in heads[3 * g:3 * g + 3]:
                hh, sl, dyh, dhn, t2 = hd["hh"], hd["sl"], hd["dyh"], hd["dhn"], hd["t2"]
                el = jnp.exp(hd["al"])
                dd_vec = dd_vec + jnp.where(lane1 == hh, jnp.sum(dyh * hd["xh"]), 0.0)
                dcm = dcm + hd["ea"] * hd["t1"]
                dg = dg + hd["dm"] * hd["dec"]
                e = hd["dm"] * hd["m"]
                dxdt = hd["mtdy"] + hd["w"] * t2
                dbm = dbm + hd["w"] * hd["xdhn"]
                dw_w = jnp.sum(hd["xdt"] * t2, axis=1, keepdims=True) * hd["w"]
                d_el = jnp.sum(dhn * hd["hp"])
                col_part = (jnp.sum(dyh * hd["yo"], axis=1, keepdims=True) + jnp.sum(e, axis=1, keepdims=True) - dw_w
                            + jnp.where(last_row, d_el * el + jnp.sum(dw_w), 0.0))
                dacum_col = dacum_col + jnp.where(lane == hh, col_part, 0.0)
                neg_colsum = -jnp.sum(e, axis=0, keepdims=True)
                rev = jnp.sum(jnp.where(row <= col, neg_colsum, 0.0), axis=1, keepdims=True)
                da_rowpart = da_rowpart + jnp.where(lane == hh, rev, 0.0)
                dh_scr[sl, :] = el * dhn + hd["dhp"]
                dx_scr[:, sl] = d_row[:, hh:hh + 1] * dyh + dxdt * hd["dth"]
                ddt = ddt + jnp.where(lane == hh, jnp.sum(dxdt * hd["xh"], axis=1, keepdims=True), 0.0)
            dgs.append(dg)
            dbms.append(dbm)
            dcms.append(dcm)
        for g in range(2):
            bm, cm = heads[3 * g]["bm"], heads[3 * g]["cm"]
            dbc_scr[:, g * SSD_STATE:(g + 1) * SSD_STATE] = dbms[g] + _dot(dgs[g], cm, TN)
            dbc_scr[:, 2 * SSD_STATE + g * SSD_STATE:2 * SSD_STATE + (g + 1) * SSD_STATE] = dcms[g] + _dot(dgs[g], bm)
        da_mat = _dot_exact01(dacum_col, upper, NN, x_left=False) + da_rowpart
        ddt = ddt + da_mat * a_row
        da_vec = jnp.sum(da_mat * dt, axis=0, keepdims=True)
        ddt_pre = jnp.where(lane < SSD_HEADS, ddt * _sigmoid(dt_pre), 0.0)
        ddt_ref[...] = ddt_pre.astype(ddt_ref.dtype)
        rid = lax.broadcasted_iota(jnp.int32, (SUBLANES, LANES), 0)
        gv_ref[...] += (jnp.where(rid == 0, jnp.sum(ddt_pre, axis=0, keepdims=True), 0.0)
                        + jnp.where(rid == 1, da_vec, 0.0) + jnp.where(rid == 2, dd_vec, 0.0))

        def conv_bwd(dpost, pre, w_ref, ext_ref, dext_ref, carry_ref, cur_ref, out_ref, g_ref, width):
            dco = dpost * _dsilu(pre)
            dext_ref[0:n, :] = dco
            dext_ref[n:n + SUBLANES, :] = carry_ref[...]
            out_ref[...] = (w_ref[3:4, :] * dco + w_ref[2:3, :] * dext_ref[pl.ds(1, n), :]
                            + w_ref[1:2, :] * dext_ref[pl.ds(2, n), :] + w_ref[0:1, :] * dext_ref[pl.ds(3, n), :]
                            ).astype(out_ref.dtype)
            carry_ref[...] = dco[0:SUBLANES, :]
            rid8 = lax.broadcasted_iota(jnp.int32, (SUBLANES, width), 0)
            acc = jnp.where(rid8 == 3, jnp.sum(dco * cur_ref[...], axis=0, keepdims=True), 0.0)
            for j in range(3):
                sh = ext_ref[pl.ds(SUBLANES - 3 + j, n), :]
                acc = acc + jnp.where(rid8 == j, jnp.sum(dco * sh, axis=0, keepdims=True), 0.0)
            acc = acc + jnp.where(rid8 == 4, jnp.sum(dco, axis=0, keepdims=True), 0.0)
            g_ref[...] += acc

        conv_bwd(dx_scr[...], pre_x, cwx_ref, xe_ref, dxe_ref, cx_ref, xs_ref, dxs_ref, gx_ref, 384)
        conv_bwd(dbc_scr[...], pre_bc, cwb_ref, be_ref, dbe_ref, cb_ref, bc_ref, dbc_ref, gb_ref, 512)

    t = nb * s
    rowblk = lambda width: pl.BlockSpec((n, width), lambda b, c: (b * nc + cidx(c), 0))
    return pl.pallas_call(
        body, name=name, grid=(nb, nc),
        in_specs=[blk(512, H_BC), halo(512, H_BC), blk(384, H_Z), blk(384, H_XS), halo(384, H_XS), blk(128, H_DT),
                  pl.BlockSpec((None, SSD_WIDTH, SSD_STATE), lambda b, c: (b * nc + cidx(c), 0, 0)),
                  pl.BlockSpec((n, SSD_WIDTH), lambda b, c: (b * nc + cidx(c), 0)),
                  full((4, 384)), full((1, 384)), full((4, 512)), full((1, 512)), full((SUBLANES, LANES)), full((1, 384))],
        out_specs=[rowblk(384), rowblk(384), rowblk(512), rowblk(128),
                   full((SUBLANES, 384)), full((SUBLANES, 512)), full((SUBLANES, LANES)), full((1, 384))],
        out_shape=[jax.ShapeDtypeStruct((t, 384), MXU_DTYPE), jax.ShapeDtypeStruct((t, 384), MXU_DTYPE),
                   jax.ShapeDtypeStruct((t, 512), MXU_DTYPE), jax.ShapeDtypeStruct((t, 128), MXU_DTYPE),
                   jax.ShapeDtypeStruct((SUBLANES, 384), F32), jax.ShapeDtypeStruct((SUBLANES, 512), F32),
                   jax.ShapeDtypeStruct((SUBLANES, LANES), F32), jax.ShapeDtypeStruct((1, 384), F32)],
        scratch_shapes=[pltpu.VMEM((SSD_WIDTH, SSD_STATE), F32), pltpu.VMEM((n + SUBLANES, 384), F32),
                        pltpu.VMEM((n + SUBLANES, 512), F32), pltpu.VMEM((n, SSD_WIDTH), F32),
                        pltpu.VMEM((n, 384), F32), pltpu.VMEM((n, 512), F32),
                        pltpu.VMEM((n + SUBLANES, 384), F32), pltpu.VMEM((n + SUBLANES, 512), F32),
                        pltpu.VMEM((SUBLANES, 384), F32), pltpu.VMEM((SUBLANES, 512), F32)],
        compiler_params=_params(2),
    )(h, h, h, h, h, h, hstate, dmix, cwx, cbx, cwb, cbb, vec, nw)


def _adamw_math(w, g, m, v):
    m = ADAM_B1 * m + (1.0 - ADAM_B1) * g
    v = ADAM_B2 * v + (1.0 - ADAM_B2) * (g * g)
    m_hat = m / (1.0 - ADAM_B1 ** ADAM_STEP)
    v_hat = v / (1.0 - ADAM_B2 ** ADAM_STEP)
    delta = -ADAM_LR * (m_hat / (jnp.sqrt(v_hat) + ADAM_EPS) + ADAM_WD * w)
    return delta, m, v


def _adamw(w, g, m, v, name, tr=256):
    rows, cols = w.shape
    tr = rows if rows <= tr else tr
    assert rows % tr == 0, (rows, tr)

    def body(w_ref, g_ref, m_ref, v_ref, d_ref, nm_ref, nv_ref):
        d, nm, nv = _adamw_math(w_ref[...], g_ref[...], m_ref[...], v_ref[...])
        d_ref[...] = d
        nm_ref[...] = nm
        nv_ref[...] = nv

    spec = pl.BlockSpec((tr, cols), lambda i: (i, 0))
    shp = jax.ShapeDtypeStruct((rows, cols), F32)
    return pl.pallas_call(body, name=name, grid=(rows // tr,), in_specs=[spec] * 4, out_specs=[spec] * 3,
                          out_shape=[shp] * 3, compiler_params=_params(1))(w, g, m, v)


def _sum8_layers(parts, name, tr):
    _, rows, cols = parts[0].shape
    assert rows % tr == 0
    nt = rows // tr

    def body(*refs):
        o_ref = refs[DEPTH]
        layer = pl.program_id(0)
        for l in range(DEPTH):
            @pl.when(layer == l)
            def _(l=l):
                acc = refs[l][0]
                for k in range(1, N_DEV):
                    acc = acc + refs[l][k]
                o_ref[...] = acc

    in_specs = [pl.BlockSpec((N_DEV, tr, cols), lambda a, i, l=l: (0, jnp.clip(i + (a - l) * nt, 0, nt - 1), 0))
                for l in range(DEPTH)]
    return pl.pallas_call(body, name=name, grid=(DEPTH, nt), in_specs=in_specs,
                          out_specs=pl.BlockSpec((None, tr, cols), lambda a, i: (a, i, 0)),
                          out_shape=jax.ShapeDtypeStruct((DEPTH, rows, cols), F32), compiler_params=_params(2))(*parts)


def _all_reduce_small(vec, name):
    rows, cols = vec.shape

    def body(x_ref, out_ref, gbuf, send_sems, recv_sems):
        x, y, c = lax.axis_index("x"), lax.axis_index("y"), lax.axis_index("c")
        me, sibling = (x, y, c), (x, y, 1 - c)
        chips = [(1 - x, y), (x, 1 - y), (1 - x, 1 - y)]

        def slot(px, py, pc):
            return gbuf.at[4 * px + 2 * py + pc]

        def copy(k, block, to, src=None):
            return pltpu.make_async_remote_copy(
                src_ref=slot(*block) if src is None else src, dst_ref=slot(*block),
                send_sem=send_sems.at[k], recv_sem=recv_sems.at[k], device_id=to, device_id_type=MESH_ID)

        first = [copy(0, me, sibling, src=x_ref)]
        first += [copy(1 + j, me, (*chip, c), src=x_ref) for j, chip in enumerate(chips)]
        for cp in first:
            cp.start()
        gbuf[4 * x + 2 * y + c] = x_ref[...]
        passed = [copy(4 + j, (*chip, c), sibling) for j, chip in enumerate(chips)]
        for j, chip in enumerate(chips):
            copy(1 + j, (*chip, c), me).wait_recv()
            passed[j].start()
        copy(0, sibling, me).wait_recv()
        for j, chip in enumerate(chips):
            copy(4 + j, (*chip, 1 - c), me).wait_recv()
        for cp in first + passed:
            cp.wait_send()
        acc = gbuf[0]
        for k in range(1, N_DEV):
            acc = acc + gbuf[k]
        out_ref[...] = acc

    return pl.pallas_call(
        body, name=name, out_shape=jax.ShapeDtypeStruct((rows, cols), F32),
        in_specs=[pl.BlockSpec(memory_space=pltpu.VMEM)], out_specs=pl.BlockSpec(memory_space=pltpu.VMEM),
        scratch_shapes=[pltpu.VMEM((N_DEV, rows, cols), F32), pltpu.SemaphoreType.DMA((7,)), pltpu.SemaphoreType.DMA((7,))],
        compiler_params=pltpu.CompilerParams(has_side_effects=True, vmem_limit_bytes=VMEM_LIMIT_BYTES),
    )(vec)


_COL_POOL, _COL_Z, _COL_XBC, _COL_DT, _COL_Q, _COL_K, _COL_V = 0, 256, 640, 1536, 1542, 1926, 2310
_H_SEGMENTS = ((_COL_XBC + SSD_WIDTH, 512), (_COL_POOL, 256), (_COL_Q, 384), (_COL_K, 384), (_COL_V, 384),
               (_COL_Z, 384), (_COL_XBC, 384), (_COL_DT, 6))


def _h_from_orig(w):
    parts = [w[..., o:o + n] for o, n in _H_SEGMENTS]
    pad = jnp.zeros(w.shape[:-1] + (H_COLS - IN_COLS,), w.dtype)
    return jnp.concatenate(parts + [pad], axis=-1)


def _h_to_orig(w):
    offs, o = {}, 0
    for orig, n in _H_SEGMENTS:
        offs[orig] = (o, n)
        o += n
    order = sorted(offs)
    return jnp.concatenate([w[..., offs[k][0]:offs[k][0] + offs[k][1]] for k in order], axis=-1)


def _interleave(w):
    lead = w.shape[:-1]
    nt = D_FF // GLU_TILE
    return jnp.swapaxes(w.reshape(lead + (2, nt, GLU_TILE)), -3, -2).reshape(lead + (2 * D_FF,))


def _deinterleave(w):
    lead = w.shape[:-1]
    nt = D_FF // GLU_TILE
    return jnp.swapaxes(w.reshape(lead + (nt, 2, GLU_TILE)), -3, -2).reshape(lead + (2 * D_FF,))


def _up_segments():
    segs = []
    for j in range(N_DEV):
        half, base = j // 4, UP_SHARD * (j % 4)
        c = base
        while c < base + UP_SHARD:
            t, r = divmod(c, GLU_TILE)
            n = min(GLU_TILE - r, base + UP_SHARD - c)
            segs.append((j, c - base, 2 * GLU_TILE * t + GLU_TILE * half + r, n))
            c += n
    return segs


def _up_to_interleaved(w, name, tr=256):
    def body(i_ref, o_ref):
        for j, src, dst, n in _up_segments():
            o_ref[:, dst:dst + n] = i_ref[j, :, src:src + n]

    return pl.pallas_call(
        body, name=name, grid=(D_MODEL // tr,), in_specs=[pl.BlockSpec((N_DEV, tr, UP_SHARD), lambda r: (0, r, 0))],
        out_specs=pl.BlockSpec((tr, 2 * D_FF), lambda r: (r, 0)),
        out_shape=jax.ShapeDtypeStruct((D_MODEL, 2 * D_FF), w.dtype), compiler_params=_params(1))(w)


def _up_from_interleaved(g, name, tr=128):
    def body(i_ref, o_ref):
        for j, src, dst, n in _up_segments():
            o_ref[j, :, src:src + n] = i_ref[:, dst:dst + n]

    return pl.pallas_call(
        body, name=name, grid=(D_MODEL // tr,), in_specs=[pl.BlockSpec((tr, 2 * D_FF), lambda r: (r, 0))],
        out_specs=pl.BlockSpec((N_DEV, tr, UP_SHARD), lambda r: (0, r, 0)),
        out_shape=jax.ShapeDtypeStruct((N_DEV, D_MODEL, UP_SHARD), g.dtype), compiler_params=_params(1))(g)


def _mix_rows_from_orig(w):
    return jnp.concatenate([w[256:640], w[640:1024], w[0:256]], axis=0)


def _mix_rows_to_orig(w):
    return jnp.concatenate([w[768:1024], w[0:384], w[384:768]], axis=0)


def _xbc_split(w):
    return w[..., :SSD_WIDTH], w[..., SSD_WIDTH:]


def _layer_fwd(x, p_l, wt, sp, nb, s, comm=None, target=None):
    h = _mm(x, wt["w_in"], "nn", F32, "mm_in", tm=1024, tn=1408)
    pool_out = _pool_fwd(h, wt["pool_bd"], sp["pool_scale"], nb, s, "pool_fwd")
    ssd_out, hstate = _ssd_fwd(h, sp["cwx"], sp["cbx"], sp["cwb"], sp["cbb"], sp["ssd_vec"], sp["ssd_norm_w"], nb, s, "ssd_fwd")
    sb_out, comm_out = _sb_fwd(h, nb, s, "sb_fwd" if comm is None else "sb_fwd_gather", comm)
    mixcat = jnp.concatenate([ssd_out, sb_out, pool_out], axis=1)
    x1, r1 = _mm_ln(mixcat, wt["w_out"], x, sp["ln1"], "mm_out_ln1", tm=1024, tk=1024)
    up = _mm(x1, wt["w_up"], "nn", F32, "mm_up", tm=1024, tn=1408)
    act = _glu_fwd(up, sp["ffn_cw"], sp["ffn_cb"], nb, s, "glu_fwd")
    gp = _mm(x1, wt["w_gate"], "nn", F32, "mm_gate", tm=1024, tn=1024)
    pp = _mm(p_l, wt["w_proj"], "nn", F32, "mm_proj", tm=2048, tn=1024)
    x2, r2, *loss = _mm_ln(act, wt["w_down"], x1, sp["ln2"], "mm_down_ln2" if target is None else "mm_down_ln2_loss",
                           gp=gp, pp=pp, target=target, tm=512, tk=1408)
    sv = dict(x=x, h=h, hstate=hstate, mixcat=mixcat, r1=r1, x1=x1, up=up, act=act, gp=gp, pp=pp, r2=r2)
    return x2, sv, comm_out, (loss[0] if loss else None)


def _layer_bwd(dx2, p_l, sv, wt, sp, nb, s, comm=None, tail_comm=None):
    dr2, dgp, dpp, st2 = _ln_bwd(sv["r2"], sp["ln2"], dx2, "ln2_bwd", gp=sv["gp"], pp=sv["pp"])
    g_down = _mm(sv["act"], dr2, "tn", F32, "wg_down", tm=1408, tn=1024, tk=512)
    dact = _mm(dr2, wt["w_down"], "nt", F32, "dg_down", tm=1024, tn=1408)
    dup, ffn_acc = _glu_bwd(sv["up"], dact, sp["ffn_cw"], sp["ffn_cb"], nb, s, "glu_bwd")
    g_up = _mm(sv["x1"], dup, "tn", F32, "wg_up", tm=1024, tn=2816, tk=512)
    g_gate = _mm(sv["x1"], dgp, "tn", F32, "wg_gate", tm=1024, tn=1024, tk=512)
    g_proj = _mm(p_l, dpp, "tn", F32, "wg_proj", tm=256, tn=1024, tk=512)
    t1 = _mm(dgp, wt["w_gate"], "nt", F32, "dg_gate", tm=1024, tn=1024, add=dr2, add_coef=ALPHA)
    dx1 = _mm(dup, wt["w_up"], "nt", F32, "dg_up", tm=1024, tn=1024, tk=1408, add=t1)
    dr1, st1 = _ln_bwd(sv["r1"], sp["ln1"], dx1, "ln1_bwd")
    g_out = _mm(sv["mixcat"], dr1, "tn", F32, "wg_out", tm=1024, tn=1024, tk=512)
    dmix = _mm(dr1, wt["w_out"], "nt", F32, "dg_out", tm=1024, tn=1024)
    du, g_pool_bd, g_pool_scale = _pool_bwd(sv["h"], dmix, wt["pool_bd"], sp["pool_scale"], nb, s, "pool_bwd")
    dz, dxs, dbc, ddt, gx, gb, gv, gn = _ssd_bwd(sv["h"], sv["hstate"], dmix, sp["cwx"], sp["cbx"], sp["cwb"], sp["cbb"],
                                                  sp["ssd_vec"], sp["ssd_norm_w"], nb, s, "ssd_bwd")
    ready = dict(w_out=g_out, ffn_w_up=g_up, ffn_w_down=g_down, ple_w_gate=g_gate, ple_w_proj=g_proj)
    job = comm(ready) if comm is not None else None
    dq, dk, dv, comm_out = _sb_bwd(sv["h"], dmix, nb, s, "sb_bwd" if job is None else "sb_bwd_x%d" % job["n_xfers"], job)
    dh = jnp.concatenate([dbc, du, dq, dk, dv, dz, dxs, ddt], axis=1)
    g_in = _mm(sv["x"], dh, "tn", F32, "wg_in", tm=1024, tn=2816, tk=512)
    tail_job = tail_comm(g_in) if tail_comm is not None else None
    dx = _mm(dh, wt["w_in"], "nt", F32, "dg_in" if tail_job is None else "dg_in_x", tm=1024, tn=1024, tk=1408, add=dr1,
             add_coef=ALPHA, comm=tail_job)
    dx, tail_out = dx if tail_job is not None else (dx, None)
    small = dict(
        pool_w=jnp.stack([g_pool_bd[HEAD_DIM * g:HEAD_DIM * (g + 1), HEAD_DIM * g:HEAD_DIM * (g + 1)] for g in range(4)]),
        pool_scale=g_pool_scale[0],
        ssd_conv_w=jnp.concatenate([gx[0:4], gb[0:4]], axis=1),
        ssd_conv_b=jnp.concatenate([gx[4], gb[4]], axis=0),
        ssd_dt_bias=gv[0, :SSD_HEADS],
        ssd_a_log=gv[1, :SSD_HEADS] * sp["ssd_vec"][1, :SSD_HEADS],
        ssd_d=gv[2, :SSD_HEADS],
        ssd_norm_w=gn[0],
        ln1_g=st1[0], ln1_b=st1[1], ln2_g=st2[0], ln2_b=st2[1],
        ffn_conv_w=_deinterleave(ffn_acc[0:3]),
        ffn_conv_b=_deinterleave(ffn_acc[3]),
    )
    return dx, dict(ready, w_in=g_in), small, comm_out, tail_out


def _layer_params(i, big, rep):
    pool_bd = jnp.zeros((POOL_WIDTH, POOL_WIDTH), F32)
    for g in range(4):
        pool_bd = lax.dynamic_update_slice(pool_bd, rep["pool_w"][i, g], (HEAD_DIM * g, HEAD_DIM * g))
    wt = dict(w_in=big["w_in"], w_out=big["w_out"], w_up=big["ffn_w_up"], w_down=big["ffn_w_down"],
              w_gate=big["ple_w_gate"], w_proj=big["ple_w_proj"], pool_bd=pool_bd.astype(MXU_DTYPE))
    cwx, cwb = _xbc_split(rep["ssd_conv_w"][i])
    cbx, cbb = _xbc_split(rep["ssd_conv_b"][i][None, :])
    vec = jnp.zeros((SUBLANES, LANES), F32)
    vec = vec.at[0, :SSD_HEADS].set(rep["ssd_dt_bias"][i])
    vec = vec.at[1, :SSD_HEADS].set(-jnp.exp(rep["ssd_a_log"][i]))
    vec = vec.at[2, :SSD_HEADS].set(rep["ssd_d"][i])
    sp = dict(pool_scale=rep["pool_scale"][i][None, :], cwx=cwx, cbx=cbx, cwb=cwb, cbb=cbb, ssd_vec=vec,
              ssd_norm_w=rep["ssd_norm_w"][i][None, :],
              ln1=jnp.stack([rep["ln1_g"][i], rep["ln1_b"][i]]), ln2=jnp.stack([rep["ln2_g"][i], rep["ln2_b"][i]]),
              ffn_cw=_interleave(rep["ffn_conv_w"][i]), ffn_cb=_interleave(rep["ffn_conv_b"][i][None, :]))
    return wt, sp


def _run_layers(x, p, target, big_w, rep, fwd_job=None, fwd_done=None, bwd_job=None, bwd_done=None, tail_job=None,
                tail_done=None):
    nb, s, d = x.shape
    t = nb * s
    xf = x.reshape(t, d)
    saved, params = [], []
    for i in range(DEPTH):
        wt, sp = _layer_params(i, big_w[i], rep)
        params.append((wt, sp))
        job = fwd_job(i) if fwd_job is not None else None
        xf, sv, res, loss = _layer_fwd(xf, p[i].reshape(t, PLE_DIM), wt, sp, nb, s, job,
                                       target.reshape(t, d) if i == DEPTH - 1 else None)
        if job is not None:
            fwd_done(i, res)
        saved.append(sv)
    dy = xf
    bigs, smalls = [None] * DEPTH, [None] * DEPTH
    for i in reversed(range(DEPTH)):
        wt, sp = params[i]
        job = (lambda ready, i=i: bwd_job(i, bigs, ready)) if bwd_job is not None else None
        dy, bigs[i], smalls[i], res, tail = _layer_bwd(dy, p[i].reshape(t, PLE_DIM), saved[i], wt, sp, nb, s, job,
                                                         tail_job if i == 0 else None)
        if job is not None:
            bwd_done(i, res)
        if tail is not None:
            tail_done(tail)
    return loss, dy.reshape(nb, s, d), bigs, smalls


def _local_step(x, p, target, full, rep):
    return _run_layers(x, p, target, [{n: full[n][i] for n in full} for i in range(DEPTH)], rep)


BIG = ("w_in", "w_out", "ffn_w_up", "ffn_w_down", "ple_w_gate", "ple_w_proj")
SMALL_REPLICATED = ("pool_w", "pool_scale", "ssd_conv_b", "ssd_dt_bias", "ssd_a_log", "ssd_d", "ssd_norm_w",
                    "ln1_g", "ln1_b", "ffn_conv_b", "ln2_g", "ln2_b")
SMALL_SHARDED = ("ssd_conv_w", "ffn_conv_w")
WEIGHTS = ("w_in", "pool_w", "pool_scale", "ssd_conv_w", "ssd_conv_b", "ssd_dt_bias", "ssd_a_log", "ssd_d", "ssd_norm_w",
           "w_out", "ln1_g", "ln1_b", "ffn_w_up", "ffn_conv_w", "ffn_conv_b", "ffn_w_down", "ln2_g", "ln2_b",
           "ple_w_gate", "ple_w_proj")
SUM_BLOCK_BYTES = 3 * 1024 * 1024


def _to_rows(a, cols):
    f = a.reshape(-1)
    pad = (-f.shape[0]) % cols
    if pad:
        f = jnp.concatenate([f, jnp.zeros((pad,), f.dtype)])
    return f.reshape(-1, cols)


def _pack_rows(arrs, cols, row_mult):
    rows = [_to_rows(a, cols) for a in arrs]
    flat = jnp.concatenate(rows, axis=0)
    pad = (-flat.shape[0]) % row_mult
    if pad:
        flat = jnp.concatenate([flat, jnp.zeros((pad, cols), flat.dtype)], axis=0)
    return flat


def _unpack_rows(flat, shapes, cols):
    out, r = [], 0
    for shp in shapes:
        n = 1
        for v in shp:
            n *= v
        nr = -(-n // cols)
        out.append(flat[r:r + nr].reshape(-1)[:n].reshape(shp))
        r += nr
    return out


def kernel(x, p, w_in, pool_w, pool_scale, ssd_conv_w, ssd_conv_b, ssd_dt_bias, ssd_a_log, ssd_d, ssd_norm_w, w_out, ln1_g, ln1_b, ffn_w_up, ffn_conv_w, ffn_conv_b, ffn_w_down, ln2_g, ln2_b, ple_w_gate, ple_w_proj, loss_target, m_w_in, m_pool_w, m_pool_scale, m_ssd_conv_w, m_ssd_conv_b, m_ssd_dt_bias, m_ssd_a_log, m_ssd_d, m_ssd_norm_w, m_w_out, m_ln1_g, m_ln1_b, m_ffn_w_up, m_ffn_conv_w, m_ffn_conv_b, m_ffn_w_down, m_ln2_g, m_ln2_b, m_ple_w_gate, m_ple_w_proj, v_w_in, v_pool_w, v_pool_scale, v_ssd_conv_w, v_ssd_conv_b, v_ssd_dt_bias, v_ssd_a_log, v_ssd_d, v_ssd_norm_w, v_w_out, v_ln1_g, v_ln1_b, v_ffn_w_up, v_ffn_conv_w, v_ffn_conv_b, v_ffn_w_down, v_ln2_g, v_ln2_b, v_ple_w_gate, v_ple_w_proj):
    wts = dict(w_in=w_in, pool_w=pool_w, pool_scale=pool_scale, ssd_conv_w=ssd_conv_w, ssd_conv_b=ssd_conv_b,
               ssd_dt_bias=ssd_dt_bias, ssd_a_log=ssd_a_log, ssd_d=ssd_d, ssd_norm_w=ssd_norm_w, w_out=w_out, ln1_g=ln1_g,
               ln1_b=ln1_b, ffn_w_up=ffn_w_up, ffn_conv_w=ffn_conv_w, ffn_conv_b=ffn_conv_b, ffn_w_down=ffn_w_down,
               ln2_g=ln2_g, ln2_b=ln2_b, ple_w_gate=ple_w_gate, ple_w_proj=ple_w_proj)
    mom_m = dict(w_in=m_w_in, pool_w=m_pool_w, pool_scale=m_pool_scale, ssd_conv_w=m_ssd_conv_w, ssd_conv_b=m_ssd_conv_b,
                 ssd_dt_bias=m_ssd_dt_bias, ssd_a_log=m_ssd_a_log, ssd_d=m_ssd_d, ssd_norm_w=m_ssd_norm_w, w_out=m_w_out,
                 ln1_g=m_ln1_g, ln1_b=m_ln1_b, ffn_w_up=m_ffn_w_up, ffn_conv_w=m_ffn_conv_w, ffn_conv_b=m_ffn_conv_b,
                 ffn_w_down=m_ffn_w_down, ln2_g=m_ln2_g, ln2_b=m_ln2_b, ple_w_gate=m_ple_w_gate, ple_w_proj=m_ple_w_proj)
    mom_v = dict(w_in=v_w_in, pool_w=v_pool_w, pool_scale=v_pool_scale, ssd_conv_w=v_ssd_conv_w, ssd_conv_b=v_ssd_conv_b,
                 ssd_dt_bias=v_ssd_dt_bias, ssd_a_log=v_ssd_a_log, ssd_d=v_ssd_d, ssd_norm_w=v_ssd_norm_w, w_out=v_w_out,
                 ln1_g=v_ln1_g, ln1_b=v_ln1_b, ffn_w_up=v_ffn_w_up, ffn_conv_w=v_ffn_conv_w, ffn_conv_b=v_ffn_conv_b,
                 ffn_w_down=v_ffn_w_down, ln2_g=v_ln2_g, ln2_b=v_ln2_b, ple_w_gate=v_ple_w_gate, ple_w_proj=v_ple_w_proj)
    me = 4 * lax.axis_index("x") + 2 * lax.axis_index("y") + lax.axis_index("c")

    def layer_shards(i):
        sh = {n: wts[n][i].astype(MXU_DTYPE) for n in BIG}
        sh["w_in"] = _h_from_orig(wts["w_in"][i]).astype(MXU_DTYPE)
        return sh

    def gathered_weights(res):
        big = dict(zip(BIG, res[:len(BIG)]))
        big["ffn_w_up"] = _up_to_interleaved(big["ffn_w_up"], "up_to_interleaved")
        return big

    res0 = _gather_call_two_level(_gather_job(layer_shards(0), [wts[n] for n in SMALL_SHARDED]), "gather_layer0")
    big_w = [gathered_weights(res0)] + [None] * (DEPTH - 1)
    rep = {n: wts[n] for n in SMALL_REPLICATED}
    for n, g in zip(SMALL_SHARDED, res0[len(BIG):]):
        rep[n] = jnp.transpose(g, (1, 2, 0, 3)).reshape(g.shape[1], g.shape[2], N_DEV * g.shape[3])

    def fwd_job(i):
        return _gather_job(layer_shards(i + 1)) if i + 1 < DEPTH else None

    def fwd_done(i, res):
        big_w[i + 1] = gathered_weights(res)

    received = [dict() for _ in range(DEPTH)]
    carried = ("w_out", "ffn_w_up", "ffn_w_down", "ple_w_gate", "ple_w_proj")

    def bwd_items(i, bigs, ready):
        items = [(i, n, ready[n]) for n in carried] + ([(i + 1, "w_in", bigs[i + 1]["w_in"])] if i + 1 < DEPTH else [])
        return [(l, n, _up_from_interleaved(g, "up_from_interleaved") if n == "ffn_w_up" else g) for l, n, g in items]

    pending = {}

    def bwd_job(i, bigs, ready):
        pending[i] = bwd_items(i, bigs, ready)
        return _exchange_job([(n, g) for _, n, g in pending[i]])

    def bwd_done(i, res):
        for (l, n, _), r in zip(pending[i], res):
            received[l][n] = r

    def tail_done(res):
        received[0]["w_in"] = res[0]

    loss_loc, grad_x, bigs, smalls = _run_layers(x, p, loss_target, big_w, rep, fwd_job, fwd_done, bwd_job, bwd_done,
                                                 lambda g_in: _exchange_job([("w_in", g_in)]), tail_done)

    grads = {}
    for n in BIG:
        parts = [received[i][n] for i in range(DEPTH)]
        _, rows, cols = parts[0].shape
        tr = next(t for t in (256, 128, 64, 32, 16, 8) if rows % t == 0 and N_DEV * t * cols * 4 <= SUM_BLOCK_BYTES)
        g = _sum8_layers(parts, "sum_" + n, tr)
        grads[n] = _h_to_orig(g) if n == "w_in" else g
    small_names = SMALL_REPLICATED + SMALL_SHARDED
    small_full_shapes = [rep[n].shape for n in small_names]
    small_vec = _pack_rows([jnp.stack([smalls[i][n] for i in range(DEPTH)]) for n in small_names] + [loss_loc[0, :1]],
                           LANES, SUBLANES)
    small_sum = _all_reduce_small(small_vec, "allreduce_small")
    small_out = _unpack_rows(small_sum, small_full_shapes + [(1,)], LANES)
    loss = small_out[-1][0]
    for n, g in zip(small_names, small_out[:-1]):
        if n in SMALL_SHARDED:
            width = wts[n].shape[-1]
            g = lax.dynamic_slice_in_dim(g, me * width, width, axis=g.ndim - 1)
        grads[n] = g

    delta, new_m, new_v = {}, {}, {}
    for n in BIG:
        shp = wts[n].shape
        two_d = lambda a: a.reshape(-1, shp[-1])
        tr = {"w_in": 128, "ffn_w_down": DOWN_SHARD}.get(n, 256)
        d_, m_, v_ = _adamw(two_d(wts[n]), two_d(grads[n]), two_d(mom_m[n]), two_d(mom_v[n]), "adamw_" + n, tr=tr)
        delta[n], new_m[n], new_v[n] = d_.reshape(shp), m_.reshape(shp), v_.reshape(shp)
    packs = [_pack_rows([src[n] for n in small_names], LANES, SUBLANES) for src in (wts, grads, mom_m, mom_v)]
    outs = _adamw(*packs, "adamw_small", tr=packs[0].shape[0])
    shapes = [wts[n].shape for n in small_names]
    for dst, flat in zip((delta, new_m, new_v), outs):
        for n, a in zip(small_names, _unpack_rows(flat, shapes, LANES)):
            dst[n] = a
    return (loss, grad_x, *[grads[n] for n in WEIGHTS], *[delta[n] for n in WEIGHTS],
            *[new_m[n] for n in WEIGHTS], *[new_v[n] for n in WEIGHTS])
```

```python
import functools

import jax
import jax.numpy as jnp
from jax import lax
from jax.experimental import pallas as pl
from jax.experimental.pallas import tpu as pltpu

F32 = jnp.float32
BF16 = jnp.bfloat16
MXU_DTYPE = jnp.bfloat16

D_MODEL = 1024
DEPTH = 4
PLE_DIM = 256
ALPHA = (2 * DEPTH) ** 0.25
LN_EPS = 1e-5
RMS_EPS = 1e-6
HEAD_DIM = 64
POOL_WIDTH = 256
POOL_WINDOWS = (2, 4, 8, 16)
SSD_WIDTH = 384
SSD_HEADS = 6
SSD_STATE = 128
SSD_XBC = 896
SB_WIDTH = 384
IN_COLS = 2694
D_FF = 2816
N_DEV = 8
UP_SHARD = 2 * D_FF // N_DEV
DOWN_SHARD = D_FF // N_DEV

ADAM_LR = 0.001
ADAM_B1 = 0.9
ADAM_B2 = 0.999
ADAM_EPS = 1e-08
ADAM_WD = 0.01
ADAM_STEP = 10

LANES = 128
SUBLANES = 8
VMEM_LIMIT_BYTES = 56 * 1024 * 1024

H_COLS = 2816
H_BC = 0
H_POOL = 512
H_Q = 768
H_K = 1152
H_V = 1536
H_Z = 1920
H_XS = 2304
H_DT = 2688
SSD_CHUNK = 512
QB = 256
GLU_TILE = 256
MASKED_LOG = -1e30

NN = ((1,), (0,))
NT = ((1,), (1,))
TN = ((0,), (0,))


def _dot(a, b, dims=NN):
    return lax.dot_general(a.astype(MXU_DTYPE), b.astype(MXU_DTYPE), (dims, ((), ())), preferred_element_type=F32)


def _dot_exact01(x, m01, dims=NN, x_left=True, terms=3):
    acc = None
    r = x
    for _ in range(terms):
        hi = r.astype(BF16)
        ops = (hi, m01) if x_left else (m01, hi)
        part = lax.dot_general(ops[0], ops[1], (dims, ((), ())), preferred_element_type=F32)
        acc = part if acc is None else acc + part
        r = r - hi.astype(F32)
    return acc


def _sigmoid(v):
    return 1.0 / (1.0 + jnp.exp(-v))


def _silu(v):
    return v * _sigmoid(v)


def _dsilu(v):
    s = _sigmoid(v)
    return s * (1.0 + v * (1.0 - s))


def _softplus(v):
    return jnp.maximum(v, 0.0) + jnp.log(1.0 + jnp.exp(-jnp.abs(v)))


def _params(n_axes, side_effects=False):
    return pltpu.CompilerParams(dimension_semantics=("arbitrary",) * n_axes, vmem_limit_bytes=VMEM_LIMIT_BYTES,
                                has_side_effects=side_effects)


MESH_ID = pl.DeviceIdType.MESH
_ANY = pl.BlockSpec(memory_space=pl.ANY)


def _flip(v, bit):
    return 1 - v if bit else v


def _comm_counts(comm):
    return (0, 0) if comm is None else (len(comm["inputs"]), len(comm["out_shapes"]))


def _comm_call_args(comm):
    if comm is None:
        return [], [], [], []
    n = comm["n_xfers"]
    sems = [pltpu.SemaphoreType.DMA(((N_DEV - 1) * n,)), pltpu.SemaphoreType.DMA(((N_DEV - 1) * n,)),
            pltpu.SemaphoreType.DMA((n,))]
    return list(comm["inputs"]), [_ANY] * len(comm["out_shapes"]), list(comm["out_shapes"]), sems


def _comm_descs(comm, in_refs, tail_refs, with_recvs=True):
    n_out = len(comm["out_shapes"])
    out_refs, (send_sems, recv_sems, local_sems) = tail_refs[:n_out], tail_refs[n_out:n_out + 3]
    xfers = comm["xfers"](in_refs, out_refs)
    n = len(xfers)
    assert n == comm["n_xfers"]
    x, y, c = lax.axis_index("x"), lax.axis_index("y"), lax.axis_index("c")
    me = 4 * x + 2 * y + c
    local = [pltpu.make_async_copy(src_for(me), dst_for(me), local_sems.at[t]) for t, (src_for, dst_for) in enumerate(xfers)]
    sends, recvs = [], []
    for k in range(1, N_DEV):
        pid = (_flip(x, k & 4), _flip(y, k & 2), _flip(c, k & 1))
        peer = 4 * pid[0] + 2 * pid[1] + pid[2]
        for t, (src_for, dst_for) in enumerate(xfers):
            idx = (k - 1) * n + t
            sends.append(pltpu.make_async_remote_copy(
                src_ref=src_for(peer), dst_ref=dst_for(me), send_sem=send_sems.at[idx], recv_sem=recv_sems.at[idx],
                device_id=pid, device_id_type=MESH_ID))
            if with_recvs:
                recvs.append(pltpu.make_async_remote_copy(
                    src_ref=src_for(peer), dst_ref=dst_for(peer), send_sem=send_sems.at[idx], recv_sem=recv_sems.at[idx],
                    device_id=pid, device_id_type=MESH_ID))
    return local, sends, recvs


def _comm_start(descs):
    local, sends, _ = descs
    for cp in local + sends:
        cp.start()


def _comm_wait(descs):
    local, sends, recvs = descs
    for cp in recvs:
        cp.wait_recv()
    for cp in sends:
        cp.wait_send()
    for cp in local:
        cp.wait()


def _comm_hosted(comm, in_refs, tail_refs, grid):
    if comm is None:
        return
    ids = [pl.program_id(a) for a in range(len(grid))]
    first = functools.reduce(jnp.logical_and, [i == 0 for i in ids])
    last = functools.reduce(jnp.logical_and, [i == g - 1 for i, g in zip(ids, grid)])

    @pl.when(first)
    def _():
        _comm_start(_comm_descs(comm, in_refs, tail_refs, with_recvs=False))

    @pl.when(last)
    def _():
        _comm_wait(_comm_descs(comm, in_refs, tail_refs))


def _gather_call_two_level(comm, name):
    n_in, n_out = len(comm["inputs"]), len(comm["out_shapes"])

    def body(*refs):
        in_refs, out_refs = refs[:n_in], refs[n_in:n_in + n_out]
        send_sems, recv_sems, local_sems = refs[n_in + n_out:]
        xfers = comm["xfers"](in_refs, out_refs)
        x, y, c = lax.axis_index("x"), lax.axis_index("y"), lax.axis_index("c")
        pos = lambda px, py, pc: 4 * px + 2 * py + pc
        me, sibling = (x, y, c), (x, y, 1 - c)
        chips = [(1 - x, y), (x, 1 - y), (1 - x, 1 - y)]

        def copy(t, k, block, to, own):
            src_for, dst_for = xfers[t]
            return pltpu.make_async_remote_copy(
                src_ref=src_for(pos(*me)) if own else dst_for(pos(*block)), dst_ref=dst_for(pos(*block)),
                send_sem=send_sems.at[7 * t + k], recv_sem=recv_sems.at[7 * t + k], device_id=to, device_id_type=MESH_ID)

        nt = len(xfers)
        local = [pltpu.make_async_copy(xfers[t][0](pos(*me)), xfers[t][1](pos(*me)), local_sems.at[t]) for t in range(nt)]
        first = [copy(t, 0, me, sibling, True) for t in range(nt)]
        first += [copy(t, 1 + j, me, (*chip, c), True) for t in range(nt) for j, chip in enumerate(chips)]
        for cp in local + first:
            cp.start()
        passed = []
        for j, chip in enumerate(chips):
            for t in range(nt):
                copy(t, 1 + j, (*chip, c), me, False).wait_recv()
                fwd = copy(t, 4 + j, (*chip, c), sibling, False)
                fwd.start()
                passed.append(fwd)
        for t in range(nt):
            copy(t, 0, sibling, me, False).wait_recv()
            for j, chip in enumerate(chips):
                copy(t, 4 + j, (*chip, 1 - c), me, False).wait_recv()
        for cp in first + passed:
            cp.wait_send()
        for cp in local:
            cp.wait()

    n = comm["n_xfers"]
    return pl.pallas_call(
        body, name=name, in_specs=[_ANY] * n_in, out_specs=[_ANY] * n_out, out_shape=list(comm["out_shapes"]),
        scratch_shapes=[pltpu.SemaphoreType.DMA((7 * n,)), pltpu.SemaphoreType.DMA((7 * n,)), pltpu.SemaphoreType.DMA((n,))],
        compiler_params=pltpu.CompilerParams(has_side_effects=True))(*comm["inputs"])


def _rows(ref, j, n):
    return ref.at[pl.ds(pl.multiple_of(j * n, SUBLANES), n), :]


def _gather_job(sh, conv=None):
    conv = list(conv or [])
    sds = jax.ShapeDtypeStruct
    out_shapes = [sds((D_MODEL, H_COLS), MXU_DTYPE), sds((D_MODEL, D_MODEL), MXU_DTYPE), sds((N_DEV, D_MODEL, UP_SHARD), MXU_DTYPE),
                  sds((D_FF, D_MODEL), MXU_DTYPE), sds((D_MODEL, D_MODEL), MXU_DTYPE), sds((PLE_DIM, D_MODEL), MXU_DTYPE)]
    out_shapes += [sds((N_DEV,) + a.shape, a.dtype) for a in conv]

    def xfers(ins, outs):
        whole = lambda a: (lambda j: a)
        r = [(whole(ins[0]), lambda j: _rows(outs[0], j, 128)),
             (whole(ins[1]), lambda j: _rows(outs[1], lax.rem(j + 6, N_DEV), 128)),
             (whole(ins[2]), lambda j: outs[2].at[j]),
             (whole(ins[3]), lambda j: _rows(outs[3], j, DOWN_SHARD)),
             (whole(ins[4]), lambda j: _rows(outs[4], j, 128)),
             (whole(ins[5]), lambda j: outs[5].at[:, pl.ds(pl.multiple_of(j * LANES, LANES), LANES)])]
        for t in range(len(conv)):
            r.append((whole(ins[6 + t]), lambda j, o=outs[6 + t]: o.at[j]))
        return r

    return dict(inputs=[sh[n] for n in BIG] + conv, out_shapes=out_shapes, xfers=xfers, n_xfers=6 + len(conv))


_SHARD_SHAPES = {"w_in": (128, H_COLS), "w_out": (128, D_MODEL), "ffn_w_up": (D_MODEL, UP_SHARD), "ffn_w_down": (DOWN_SHARD, D_MODEL),
                 "ple_w_gate": (128, D_MODEL), "ple_w_proj": (PLE_DIM, LANES)}


def _exchange_job(items):
    def source(name, ref):
        if name in ("w_in", "ple_w_gate"):
            return lambda j: _rows(ref, j, 128)
        if name == "w_out":
            return lambda j: _rows(ref, lax.rem(j + 6, N_DEV), 128)
        if name == "ffn_w_up":
            return lambda j: ref.at[j]
        if name == "ffn_w_down":
            return lambda j: _rows(ref, j, DOWN_SHARD)
        assert name == "ple_w_proj"
        return lambda j: ref.at[:, pl.ds(pl.multiple_of(j * LANES, LANES), LANES)]

    def xfers(ins, outs):
        return [(source(name, i), lambda j, o=o: o.at[j]) for (name, _), i, o in zip(items, ins, outs)]

    return dict(inputs=[g for _, g in items], xfers=xfers, n_xfers=len(items),
                out_shapes=[jax.ShapeDtypeStruct((N_DEV,) + _SHARD_SHAPES[name], F32) for name, _ in items])


def _pick(n, pref):
    if n <= pref:
        return n
    for t in range(pref - pref % LANES, 0, -LANES):
        if n % t == 0:
            return t
    raise ValueError((n, pref))


def _mm(a, b, mode, out_dtype, name, tm=512, tn=512, tk=1024, add=None, add_coef=1.0, comm=None):
    n_in, n_out = _comm_counts(comm)
    if mode == "nn":
        (m, k), (k2, n) = a.shape, b.shape
    elif mode == "nt":
        (m, k), (n, k2) = a.shape, b.shape
    else:
        (k, m), (k2, n) = a.shape, b.shape
    assert k == k2, (a.shape, b.shape, mode)
    tm, tn, tk = _pick(m, tm), _pick(n, tn), _pick(k, tk)
    nk = k // tk
    dims = {"nn": NN, "nt": NT, "tn": TN}[mode]

    def body(*refs):
        a_ref, b_ref = refs[:2]
        n_add = int(add is not None)
        add_ref = refs[2] if n_add else None
        o_ref = refs[2 + n_add + n_in]
        tail = refs[3 + n_add + n_in:]
        if comm is not None:
            _comm_hosted(comm, refs[2 + n_add:2 + n_add + n_in], tail[:n_out] + tail[n_out + int(nk > 1):],
                         (m // tm, n // tn, nk))

        def finish(r):
            if add_ref is not None:
                r = r + add_coef * add_ref[...]
            o_ref[...] = r.astype(out_dtype)

        if nk == 1:
            finish(_dot(a_ref[...], b_ref[...], dims))
            return
        acc_ref = tail[n_out]
        kk = pl.program_id(2)

        @pl.when(kk == 0)
        def _():
            acc_ref[...] = jnp.zeros_like(acc_ref)

        acc_ref[...] += _dot(a_ref[...], b_ref[...], dims)

        @pl.when(kk == nk - 1)
        def _():
            finish(acc_ref[...])

    if mode == "tn":
        a_spec = pl.BlockSpec((tk, tm), lambda i, j, kk: (kk, i))
    else:
        a_spec = pl.BlockSpec((tm, tk), lambda i, j, kk: (i, kk))
    if mode == "nt":
        b_spec = pl.BlockSpec((tn, tk), lambda i, j, kk: (j, kk))
    else:
        b_spec = pl.BlockSpec((tk, tn), lambda i, j, kk: (kk, j))
    o_spec = pl.BlockSpec((tm, tn), lambda i, j, kk: (i, j))
    in_specs = [a_spec, b_spec] + ([o_spec] if add is not None else [])
    args = (a, b) + ((add,) if add is not None else ())
    c_in, c_specs, c_shapes, c_scratch = _comm_call_args(comm)
    res = pl.pallas_call(
        body, name=name, grid=(m // tm, n // tn, nk), in_specs=in_specs + [_ANY] * n_in, out_specs=[o_spec] + c_specs,
        out_shape=[jax.ShapeDtypeStruct((m, n), out_dtype)] + c_shapes,
        scratch_shapes=([pltpu.VMEM((tm, tn), F32)] if nk > 1 else []) + c_scratch,
        compiler_params=_params(3, comm is not None),
    )(*args, *c_in)
    return res[0] if comm is None else (res[0], res[1:])


def _mm_ln(a, b, x, gb, name, gp=None, pp=None, target=None, tm=512, tk=1024):
    (m, k), (k2, d) = a.shape, b.shape
    assert k == k2 and x.shape == (m, d)
    tm, tk = _pick(m, tm), _pick(k, tk)
    nk = k // tk
    with_ple = gp is not None
    with_loss = target is not None

    def body(*refs):
        a_ref, b_ref, x_ref = refs[:3]
        gp_ref, pp_ref = refs[3:5] if with_ple else (None, None)
        t_ref = refs[3 + 2 * with_ple] if with_loss else None
        gb_ref, y_ref, r_ref = refs[3 + 2 * with_ple + with_loss:6 + 2 * with_ple + with_loss]
        l_ref = refs[-2] if with_loss else None
        acc_ref = refs[-1]
        kk = pl.program_id(1)

        if with_loss:
            @pl.when(jnp.logical_and(pl.program_id(0) == 0, kk == 0))
            def _():
                l_ref[...] = jnp.zeros_like(l_ref)

        @pl.when(kk == 0)
        def _():
            acc_ref[...] = jnp.zeros_like(acc_ref)

        acc_ref[...] += _dot(a_ref[...], b_ref[...])

        @pl.when(kk == nk - 1)
        def _():
            r = ALPHA * x_ref[...] + acc_ref[...]
            if with_ple:
                r = r + _sigmoid(gp_ref[...]) * pp_ref[...]
            mu = jnp.mean(r, axis=1, keepdims=True)
            xc = r - mu
            var = jnp.mean(xc * xc, axis=1, keepdims=True)
            y = xc * lax.rsqrt(var + LN_EPS) * gb_ref[0:1, :] + gb_ref[1:2, :]
            r_ref[...] = r
            if with_loss:
                e = y - t_ref[...]
                y_ref[...] = e * (1.0 / d)
                l_ref[...] += 0.5 * jnp.sum(jnp.mean(e * e, axis=1, keepdims=True), axis=0, keepdims=True)
            else:
                y_ref[...] = y

    row = pl.BlockSpec((tm, d), lambda i, kk: (i, 0))
    in_specs = [pl.BlockSpec((tm, tk), lambda i, kk: (i, kk)), pl.BlockSpec((tk, d), lambda i, kk: (kk, 0)), row]
    in_specs += ([row, row] if with_ple else []) + ([row] if with_loss else []) + [pl.BlockSpec((2, d), lambda i, kk: (0, 0))]
    args = (a, b, x) + ((gp, pp) if with_ple else ()) + ((target,) if with_loss else ()) + (gb,)
    loss_spec = [pl.BlockSpec((SUBLANES, LANES), lambda i, kk: (0, 0))] if with_loss else []
    loss_shape = [jax.ShapeDtypeStruct((SUBLANES, LANES), F32)] if with_loss else []
    return pl.pallas_call(
        body, name=name, grid=(m // tm, nk), in_specs=in_specs, out_specs=[row, row] + loss_spec,
        out_shape=[jax.ShapeDtypeStruct((m, d), F32)] * 2 + loss_shape, scratch_shapes=[pltpu.VMEM((tm, d), F32)],
        compiler_params=_params(2),
    )(*args)


def _ln_bwd(r, gb, dy, name, gp=None, pp=None, tr=512):
    t, d = r.shape
    tr = _pick(t, tr)
    with_ple = gp is not None

    def body(*refs):
        if with_ple:
            r_ref, dy_ref, gp_ref, pp_ref, gb_ref, dr_ref, dgp_ref, dpp_ref, st_ref = refs
        else:
            r_ref, dy_ref, gb_ref, dr_ref, st_ref = refs
        i = pl.program_id(0)

        @pl.when(i == 0)
        def _():
            st_ref[...] = jnp.zeros_like(st_ref)

        rv = r_ref[...]
        dy_v = dy_ref[...]
        mu = jnp.mean(rv, axis=1, keepdims=True)
        xc = rv - mu
        var = jnp.mean(xc * xc, axis=1, keepdims=True)
        rstd = lax.rsqrt(var + LN_EPS)
        xhat = xc * rstd
        dxh = dy_v * gb_ref[0:1, :]
        m1 = jnp.mean(dxh, axis=1, keepdims=True)
        m2 = jnp.mean(dxh * xhat, axis=1, keepdims=True)
        dr = rstd * (dxh - m1 - xhat * m2)
        dr_ref[...] = dr
        rid = lax.broadcasted_iota(jnp.int32, (2, d), 0)
        dg = jnp.sum(dy_v * xhat, axis=0, keepdims=True)
        db = jnp.sum(dy_v, axis=0, keepdims=True)
        st_ref[...] += jnp.where(rid == 0, dg, db)
        if with_ple:
            sg = _sigmoid(gp_ref[...])
            ppv = pp_ref[...]
            dgp_ref[...] = (dr * ppv * sg * (1.0 - sg)).astype(dgp_ref.dtype)
            dpp_ref[...] = (dr * sg).astype(dpp_ref.dtype)

    row = pl.BlockSpec((tr, d), lambda i: (i, 0))
    vec = pl.BlockSpec((2, d), lambda i: (0, 0))
    if with_ple:
        in_specs, args = [row] * 4 + [vec], (r, dy, gp, pp, gb)
        out_specs = [row, row, row, vec]
        out_shape = [jax.ShapeDtypeStruct((t, d), F32), jax.ShapeDtypeStruct((t, d), MXU_DTYPE),
                     jax.ShapeDtypeStruct((t, d), MXU_DTYPE), jax.ShapeDtypeStruct((2, d), F32)]
    else:
        in_specs, args = [row] * 2 + [vec], (r, dy, gb)
        out_specs = [row, vec]
        out_shape = [jax.ShapeDtypeStruct((t, d), F32), jax.ShapeDtypeStruct((2, d), F32)]
    return pl.pallas_call(body, name=name, grid=(t // tr,), in_specs=in_specs, out_specs=out_specs,
                          out_shape=out_shape, compiler_params=_params(1))(*args)


def _shift_down(v, k, row):
    return jnp.where(row >= k, pltpu.roll(v, k, 0), 0.0)


def _shift_up(v, k, row):
    n = v.shape[0]
    return jnp.where(row < n - k, pltpu.roll(v, n - k, 0), 0.0)


def _pool_window(lane):
    grp = lane // HEAD_DIM
    return jnp.where(grp == 0, POOL_WINDOWS[0], jnp.where(grp == 1, POOL_WINDOWS[1],
                     jnp.where(grp == 2, POOL_WINDOWS[2], POOL_WINDOWS[3])))


def _pool_select(lane, s2, s4, s8, s16):
    grp = lane // HEAD_DIM
    return jnp.where(grp == 0, s2, jnp.where(grp == 1, s4, jnp.where(grp == 2, s8, s16)))


def _pooled(u, row, lane):
    s2 = u + _shift_down(u, 1, row)
    s4 = s2 + _shift_down(s2, 2, row)
    s8 = s4 + _shift_down(s4, 4, row)
    s16 = s8 + _shift_down(s8, 8, row)
    cnt = jnp.minimum(row + 1, _pool_window(lane)).astype(F32)
    return _pool_select(lane, s2, s4, s8, s16) / cnt - u, cnt


def _pool_fwd(h, wbd, scale, nb, s, name):
    def body(u_ref, w_ref, sc_ref, o_ref):
        u = u_ref[...]
        row = lax.broadcasted_iota(jnp.int32, u.shape, 0)
        lane = lax.broadcasted_iota(jnp.int32, u.shape, 1)
        pooled, _ = _pooled(u, row, lane)
        o_ref[...] = (_dot(pooled, w_ref[...]) * sc_ref[...]).astype(o_ref.dtype)

    wb = POOL_WIDTH
    return pl.pallas_call(
        body, name=name, grid=(nb,),
        in_specs=[pl.BlockSpec((s, wb), lambda b: (b, H_POOL // wb)), pl.BlockSpec((wb, wb), lambda b: (0, 0)),
                  pl.BlockSpec((1, wb), lambda b: (0, 0))],
        out_specs=pl.BlockSpec((s, wb), lambda b: (b, 0)),
        out_shape=jax.ShapeDtypeStruct((nb * s, wb), MXU_DTYPE), compiler_params=_params(1),
    )(h, wbd, scale)


def _pool_bwd(h, dmix, wbd, scale, nb, s, name):
    wb = POOL_WIDTH

    def body(u_ref, do_ref, w_ref, sc_ref, du_ref, dw_ref, ds_ref):
        b = pl.program_id(0)

        @pl.when(b == 0)
        def _():
            dw_ref[...] = jnp.zeros_like(dw_ref)
            ds_ref[...] = jnp.zeros_like(ds_ref)

        u = u_ref[...]
        row = lax.broadcasted_iota(jnp.int32, u.shape, 0)
        lane = lax.broadcasted_iota(jnp.int32, u.shape, 1)
        pooled, cnt = _pooled(u, row, lane)
        mixed = _dot(pooled, w_ref[...])
        do = do_ref[...]
        ds_ref[...] += jnp.sum(do * mixed, axis=0, keepdims=True)
        dm = do * sc_ref[...]
        dw_ref[...] += _dot(pooled, dm, TN)
        dpool = _dot(dm, w_ref[...], NT)
        qv = dpool / cnt
        f2 = qv + _shift_up(qv, 1, row)
        f4 = f2 + _shift_up(f2, 2, row)
        f8 = f4 + _shift_up(f4, 4, row)
        f16 = f8 + _shift_up(f8, 8, row)
        du_ref[...] = (_pool_select(lane, f2, f4, f8, f16) - dpool).astype(du_ref.dtype)

    return pl.pallas_call(
        body, name=name, grid=(nb,),
        in_specs=[pl.BlockSpec((s, wb), lambda b: (b, H_POOL // wb)), pl.BlockSpec((s, wb), lambda b: (b, 3)),
                  pl.BlockSpec((wb, wb), lambda b: (0, 0)), pl.BlockSpec((1, wb), lambda b: (0, 0))],
        out_specs=[pl.BlockSpec((s, wb), lambda b: (b, 0)), pl.BlockSpec((wb, wb), lambda b: (0, 0)),
                   pl.BlockSpec((1, wb), lambda b: (0, 0))],
        out_shape=[jax.ShapeDtypeStruct((nb * s, wb), MXU_DTYPE), jax.ShapeDtypeStruct((wb, wb), F32),
                   jax.ShapeDtypeStruct((1, wb), F32)],
        compiler_params=_params(1),
    )(h, dmix, wbd, scale)


def _glu_conv(x, w_ref, b_ref, row):
    return (b_ref[...] + w_ref[2:3, :] * x + w_ref[1:2, :] * _shift_down(x, 1, row)
            + w_ref[0:1, :] * _shift_down(x, 2, row))


def _glu_fwd(up, cw, cb, nb, s, name):
    wt = 2 * GLU_TILE
    nt = up.shape[1] // wt

    def body(u_ref, w_ref, b_ref, o_ref):
        x = u_ref[...]
        row = lax.broadcasted_iota(jnp.int32, x.shape, 0)
        c = _glu_conv(x, w_ref, b_ref, row)
        o_ref[...] = (_silu(c[:, :GLU_TILE]) * c[:, GLU_TILE:]).astype(o_ref.dtype)

    return pl.pallas_call(
        body, name=name, grid=(nt, nb),
        in_specs=[pl.BlockSpec((s, wt), lambda j, b: (b, j)), pl.BlockSpec((3, wt), lambda j, b: (0, j)),
                  pl.BlockSpec((1, wt), lambda j, b: (0, j))],
        out_specs=pl.BlockSpec((s, GLU_TILE), lambda j, b: (b, j)),
        out_shape=jax.ShapeDtypeStruct((nb * s, nt * GLU_TILE), MXU_DTYPE), compiler_params=_params(2),
    )(up, cw, cb)


def _glu_bwd(up, dact, cw, cb, nb, s, name):
    wt = 2 * GLU_TILE
    nt = up.shape[1] // wt

    def body(u_ref, da_ref, w_ref, b_ref, du_ref, acc_ref):
        b = pl.program_id(1)

        @pl.when(b == 0)
        def _():
            acc_ref[...] = jnp.zeros_like(acc_ref)

        x = u_ref[...]
        row = lax.broadcasted_iota(jnp.int32, x.shape, 0)
        x1 = _shift_down(x, 1, row)
        x2 = _shift_down(x, 2, row)
        c = b_ref[...] + w_ref[2:3, :] * x + w_ref[1:2, :] * x1 + w_ref[0:1, :] * x2
        gate, val = c[:, :GLU_TILE], c[:, GLU_TILE:]
        da = da_ref[...]
        dc = jnp.concatenate([da * val * _dsilu(gate), da * _silu(gate)], axis=1)
        dx = (w_ref[2:3, :] * dc + w_ref[1:2, :] * _shift_up(dc, 1, row) + w_ref[0:1, :] * _shift_up(dc, 2, row))
        du_ref[...] = dx.astype(du_ref.dtype)
        rid = lax.broadcasted_iota(jnp.int32, (SUBLANES, wt), 0)
        dw0 = jnp.sum(dc * x2, axis=0, keepdims=True)
        dw1 = jnp.sum(dc * x1, axis=0, keepdims=True)
        dw2 = jnp.sum(dc * x, axis=0, keepdims=True)
        db = jnp.sum(dc, axis=0, keepdims=True)
        acc_ref[...] += (jnp.where(rid == 0, dw0, 0.0) + jnp.where(rid == 1, dw1, 0.0)
                         + jnp.where(rid == 2, dw2, 0.0) + jnp.where(rid == 3, db, 0.0))

    return pl.pallas_call(
        body, name=name, grid=(nt, nb),
        in_specs=[pl.BlockSpec((s, wt), lambda j, b: (b, j)), pl.BlockSpec((s, GLU_TILE), lambda j, b: (b, j)),
                  pl.BlockSpec((3, wt), lambda j, b: (0, j)), pl.BlockSpec((1, wt), lambda j, b: (0, j))],
        out_specs=[pl.BlockSpec((s, wt), lambda j, b: (b, j)), pl.BlockSpec((SUBLANES, wt), lambda j, b: (0, j))],
        out_shape=[jax.ShapeDtypeStruct((nb * s, nt * wt), MXU_DTYPE), jax.ShapeDtypeStruct((SUBLANES, nt * wt), F32)],
        compiler_params=_params(2),
    )(up, dact, cw, cb)


def _sb_constants():
    row = lax.broadcasted_iota(jnp.int32, (QB, QB), 0)
    col = lax.broadcasted_iota(jnp.int32, (QB, QB), 1)
    return jnp.stack([row > col, row < col, col < row]).astype(BF16)


_SB_CONST_SPEC = pl.BlockSpec((3, QB, QB), lambda b, p, i: (0, 0, 0))


def _sb_fwd(h, nb, s, name, comm=None):
    nq = s // QB
    scale = HEAD_DIM ** -0.5
    n_in, n_out = _comm_counts(comm)

    def body(q_ref, k_ref, v_ref, tri_ref, *rest):
        o_ref = rest[n_in]
        i = pl.program_id(2)
        _comm_hosted(comm, rest[:n_in], rest[n_in + 1:], (nb, 3, nq))
        sls = [slice(hd * HEAD_DIM, (hd + 1) * HEAD_DIM) for hd in range(2)]
        qs = [(q_ref[:, sl] * scale).astype(MXU_DTYPE) for sl in sls]

        def group(blocks, carry, diagonal_first=False):
            items = [(hd, n, j) for hd in range(2) for n, j in enumerate(blocks)]
            zs = [_dot(qs[hd], k_ref[pl.ds(pl.multiple_of(j * QB, QB), QB), sls[hd]], NT) for hd, n, j in items]
            lns, lss = [], []
            for (hd, n, j), z in zip(items, zs):
                ln = -_softplus(z)
                ls = z + ln
                if diagonal_first and n == 0:
                    low = tri_ref[2] > 0
                    ln = jnp.where(low, ln, 0.0)
                    ls = jnp.where(low, ls, MASKED_LOG)
                lns.append(ln)
                lss.append(ls)
            tls = [_dot_exact01(ln, tri_ref[0], terms=2) for ln in lns]
            sms = [jnp.sum(ln, axis=1, keepdims=True) for ln in lns]
            acc, cts, ws = [carry[0][0], carry[1][0]], [carry[0][1], carry[1][1]], []
            for k, (hd, n, j) in enumerate(items):
                ws.append(jnp.exp(lss[k] + tls[k] + cts[hd]))
                cts[hd] = cts[hd] + sms[k]
            for k, (hd, n, j) in enumerate(items):
                acc[hd] = acc[hd] + _dot(ws[k], v_ref[pl.ds(pl.multiple_of(j * QB, QB), QB), sls[hd]])
            return (acc[0], cts[0]), (acc[1], cts[1])

        start = (jnp.zeros((QB, HEAD_DIM), F32), jnp.zeros((QB, 1), F32))
        below = jnp.minimum(i, 1)
        left = i - below
        carry = lax.fori_loop(0, below, lambda t, c: group([i, i - 1], c, True), (start, start))
        carry = lax.fori_loop(0, 1 - below, lambda t, c: group([i], c, True), carry)
        carry = lax.fori_loop(0, left // 2, lambda t, c: group([left - 1 - 2 * t, left - 2 - 2 * t], c), carry)
        carry = lax.fori_loop(0, left % 2, lambda t, c: group([0], c), carry)
        o_ref[:, sls[0]] = carry[0][0].astype(o_ref.dtype)
        o_ref[:, sls[1]] = carry[1][0].astype(o_ref.dtype)

    qspec = lambda off: pl.BlockSpec((QB, LANES), lambda b, p, i: (b * nq + i, off // LANES + p))
    kvspec = lambda off: pl.BlockSpec((s, LANES), lambda b, p, i: (b, off // LANES + p))
    c_in, c_specs, c_shapes, c_scratch = _comm_call_args(comm)
    res = pl.pallas_call(
        body, name=name, grid=(nb, 3, nq), in_specs=[qspec(H_Q), kvspec(H_K), kvspec(H_V), _SB_CONST_SPEC] + [_ANY] * n_in,
        out_specs=[pl.BlockSpec((QB, LANES), lambda b, p, i: (b * nq + i, p))] + c_specs,
        out_shape=[jax.ShapeDtypeStruct((nb * s, SB_WIDTH), MXU_DTYPE)] + c_shapes, scratch_shapes=c_scratch,
        compiler_params=_params(3, comm is not None),
    )(h, h, h, _sb_constants(), *c_in)
    return res[0], res[1:]


def _sb_bwd(h, dmix, nb, s, name, comm=None):
    nq = s // QB
    scale = HEAD_DIM ** -0.5
    n_in, n_out = _comm_counts(comm)

    def body(q_ref, k_ref, v_ref, do_ref, tri_ref, *rest):
        dq_ref, dk_out, dv_out = rest[n_in:n_in + 3]
        p_buf, ls_buf, dk_ref, dv_ref = rest[n_in + 3 + n_out:n_in + 7 + n_out]
        i = pl.program_id(2)
        _comm_hosted(comm, rest[:n_in], rest[n_in + 3:n_in + 3 + n_out] + rest[n_in + 7 + n_out:], (nb, 3, nq))

        @pl.when(i == 0)
        def _():
            dk_ref[...] = jnp.zeros_like(dk_ref)
            dv_ref[...] = jnp.zeros_like(dv_ref)

        sls = [slice(hd * HEAD_DIM, (hd + 1) * HEAD_DIM) for hd in range(2)]
        q_raw = [q_ref[:, sl].astype(MXU_DTYPE) for sl in sls]
        qs = [(q_ref[:, sl] * scale).astype(MXU_DTYPE) for sl in sls]
        do = [do_ref[:, sl].astype(MXU_DTYPE) for sl in sls]

        def down_group(blocks, carry, diagonal_first=False):
            items = [(hd, n, j) for hd in range(2) for n, j in enumerate(blocks)]
            rows = [pl.ds(pl.multiple_of(j * QB, QB), QB) for hd, n, j in items]
            zs = [_dot(qs[hd], k_ref[rows[k], sls[hd]], NT) for k, (hd, n, j) in enumerate(items)]
            das = [_dot(do[hd], v_ref[rows[k], sls[hd]], NT) for k, (hd, n, j) in enumerate(items)]
            lns, lss = [], []
            for (hd, n, j), z in zip(items, zs):
                ln = -_softplus(z)
                ls = z + ln
                if diagonal_first and n == 0:
                    low = tri_ref[2] > 0
                    ln = jnp.where(low, ln, 0.0)
                    ls = jnp.where(low, ls, MASKED_LOG)
                lns.append(ln)
                lss.append(ls)
            tls = [_dot_exact01(ln, tri_ref[0], terms=2) for ln in lns]
            sms = [jnp.sum(ln, axis=1, keepdims=True) for ln in lns]
            cts, ws = [carry[0], carry[1]], []
            for k, (hd, n, j) in enumerate(items):
                ws.append(jnp.exp(lss[k] + tls[k] + cts[hd]))
                cts[hd] = cts[hd] + sms[k]
            for k, (hd, n, j) in enumerate(items):
                p_buf[hd, j] = das[k] * ws[k]
                ls_buf[hd, j] = lss[k]
                dv_ref[rows[k], sls[hd]] += _dot(ws[k], do[hd], TN)
            return cts[0], cts[1]

        zero = jnp.zeros((QB, 1), F32)
        below = jnp.minimum(i, 1)
        left = i - below
        carry = lax.fori_loop(0, below, lambda t, c: down_group([i, i - 1], c, True), (zero, zero))
        carry = lax.fori_loop(0, 1 - below, lambda t, c: down_group([i], c, True), carry)
        carry = lax.fori_loop(0, left // 2, lambda t, c: down_group([left - 1 - 2 * t, left - 2 - 2 * t], c), carry)
        lax.fori_loop(0, left % 2, lambda t, c: down_group([0], c), carry)

        def up_group(blocks, carry):
            ld = []
            for hd in range(2):
                ld.append([])
                for j in blocks:
                    pj = p_buf[hd, j]
                    ld[hd].append((pj, jnp.exp(ls_buf[hd, j]), _dot_exact01(pj, tri_ref[1]), jnp.sum(pj, axis=1, keepdims=True)))
            out = []
            for hd in range(2):
                dq, cp = carry[hd]
                for (pj, sg, cm, sm), j in zip(ld[hd], blocks):
                    r0 = pl.multiple_of(j * QB, QB)
                    dz = (pj * (1.0 - sg) - (cp + cm) * sg) * scale
                    dk_ref[pl.ds(r0, QB), sls[hd]] += _dot(dz, q_raw[hd], TN)
                    dq = dq + _dot(dz, k_ref[pl.ds(r0, QB), sls[hd]])
                    cp = cp + sm
                out.append((dq, cp))
            return tuple(out)

        start = (jnp.zeros((QB, HEAD_DIM), F32), zero)
        odd = (i + 1) % 2
        carry = lax.fori_loop(0, odd, lambda t, c: up_group([0], c), (start, start))
        carry = lax.fori_loop(0, (i + 1) // 2, lambda t, c: up_group([odd + 2 * t, odd + 2 * t + 1], c), carry)
        dq_ref[:, sls[0]] = carry[0][0].astype(dq_ref.dtype)
        dq_ref[:, sls[1]] = carry[1][0].astype(dq_ref.dtype)

        @pl.when(i == nq - 1)
        def _():
            dk_out[...] = dk_ref[...].astype(dk_out.dtype)
            dv_out[...] = dv_ref[...].astype(dv_out.dtype)

    qspec = lambda off: pl.BlockSpec((QB, LANES), lambda b, p, i: (b * nq + i, off // LANES + p))
    kvspec = lambda off: pl.BlockSpec((s, LANES), lambda b, p, i: (b, off // LANES + p))
    blk_out = pl.BlockSpec((QB, LANES), lambda b, p, i: (b * nq + i, p))
    seq_out = pl.BlockSpec((s, LANES), lambda b, p, i: (b, p))
    shp = jax.ShapeDtypeStruct((nb * s, SB_WIDTH), MXU_DTYPE)
    c_in, c_specs, c_shapes, c_scratch = _comm_call_args(comm)
    res = pl.pallas_call(
        body, name=name, grid=(nb, 3, nq),
        in_specs=[qspec(H_Q), kvspec(H_K), kvspec(H_V), pl.BlockSpec((QB, LANES), lambda b, p, i: (b * nq + i, 3 + p)),
                  _SB_CONST_SPEC] + [_ANY] * n_in,
        out_specs=[blk_out, seq_out, seq_out] + c_specs, out_shape=[shp, shp, shp] + c_shapes,
        scratch_shapes=[pltpu.VMEM((2, nq, QB, QB), F32), pltpu.VMEM((2, nq, QB, QB), F32),
                        pltpu.VMEM((s, LANES), F32), pltpu.VMEM((s, LANES), F32)] + c_scratch,
        compiler_params=_params(3, comm is not None),
    )(h, h, h, dmix, _sb_constants(), *c_in)
    return res[0], res[1], res[2], res[3:]


def _ssd_conv(cur_ref, halo_ref, w_ref, b_ref, ext_ref, first):
    n = SSD_CHUNK
    cur = cur_ref[...]
    ext_ref[0:SUBLANES, :] = jnp.where(first, 0.0, halo_ref[...])
    ext_ref[SUBLANES:SUBLANES + n, :] = cur
    return (b_ref[...] + w_ref[3:4, :] * cur + w_ref[2:3, :] * ext_ref[pl.ds(SUBLANES - 1, n), :]
            + w_ref[1:2, :] * ext_ref[pl.ds(SUBLANES - 2, n), :] + w_ref[0:1, :] * ext_ref[pl.ds(SUBLANES - 3, n), :])


def _ssd_tri():
    row = lax.broadcasted_iota(jnp.int32, (SSD_CHUNK, SSD_CHUNK), 0)
    col = lax.broadcasted_iota(jnp.int32, (SSD_CHUNK, SSD_CHUNK), 1)
    return row, col


def _ssd_specs(nc, rev):
    n = SSD_CHUNK
    hb = n // SUBLANES

    def cidx(c):
        return (nc - 1 - c) if rev else c

    def blk(width, off):
        return pl.BlockSpec((n, width), lambda b, c: (b * nc + cidx(c), off // width))

    def halo(width, off):
        return pl.BlockSpec((SUBLANES, width), lambda b, c: (jnp.maximum((b * nc + cidx(c)) * hb - 1, 0), off // width))

    def full(shape):
        return pl.BlockSpec(shape, lambda b, c: (0,) * len(shape))

    return cidx, blk, halo, full


def _ssd_core_fwd(x, bc, dt, acum, acum_t, a_row, d_row, h_prev_ref, tri):
    n = SSD_CHUNK
    heads = []
    for g in range(2):
        bm = bc[:, g * SSD_STATE:(g + 1) * SSD_STATE]
        cm = bc[:, 2 * SSD_STATE + g * SSD_STATE: 2 * SSD_STATE + (g + 1) * SSD_STATE]
        gmat = _dot(cm, bm, NT)
        for r in range(3):
            hh = g * 3 + r
            hp = h_prev_ref[hh * HEAD_DIM:(hh + 1) * HEAD_DIM, :]
            heads.append(dict(g=g, hh=hh, bm=bm, cm=cm, gmat=gmat, hp=hp, cmh=_dot(cm, hp, NT)))
    for hd in heads:
        hh = hd["hh"]
        ac = acum[:, hh:hh + 1]
        ar = acum_t[hh:hh + 1, :]
        hd["dec"] = jnp.where(tri, jnp.exp(jnp.minimum(ac - ar, 0.0)), 0.0)
        hd["xh"] = x[:, hh * HEAD_DIM:(hh + 1) * HEAD_DIM]
        hd["dth"] = dt[:, hh:hh + 1]
        hd["xdt"] = hd["xh"] * hd["dth"]
        hd["ea"] = jnp.exp(ac)
        hd["m"] = hd["gmat"] * hd["dec"]
        hd["al"] = acum[n - 1:n, hh:hh + 1]
        hd["w"] = jnp.exp(hd["al"] - ac)
    for hd in heads:
        hd["yd"] = _dot(hd["m"], hd["xdt"])
    for hd in heads:
        hd["yo"] = hd["ea"] * hd["cmh"]
        hd["y"] = hd["yd"] + hd["yo"] + d_row[:, hd["hh"]:hd["hh"] + 1] * hd["xh"]
    return heads


def _ssd_prep(xs_ref, xsh_ref, bc_ref, bch_ref, dt_ref, cwx_ref, cbx_ref, cwb_ref, cbb_ref, vec_ref, xe_ref, be_ref, first):
    pre_x = _ssd_conv(xs_ref, xsh_ref, cwx_ref, cbx_ref, xe_ref, first)
    pre_bc = _ssd_conv(bc_ref, bch_ref, cwb_ref, cbb_ref, be_ref, first)
    x = _silu(pre_x)
    bc = _silu(pre_bc)
    dt_pre = dt_ref[...] + vec_ref[0:1, :]
    dt = _softplus(dt_pre)
    a_row = vec_ref[1:2, :]
    amat = dt * a_row
    row, col = _ssd_tri()
    upper = (row <= col).astype(BF16)
    lower = (col <= row).astype(BF16)
    acum = _dot_exact01(amat, lower, NN, x_left=False)
    acum_t = _dot_exact01(amat, upper, TN, x_left=True)
    return pre_x, pre_bc, x, bc, dt_pre, dt, a_row, acum, acum_t, row, col, upper


def _ssd_gate_norm(y, z, nw):
    lane = lax.broadcasted_iota(jnp.int32, y.shape, 1)
    g0 = lane < SSD_WIDTH // 2
    hg = y * _silu(z)
    sq = hg * hg
    ms0 = jnp.sum(jnp.where(g0, sq, 0.0), axis=1, keepdims=True) * (2.0 / SSD_WIDTH)
    ms1 = jnp.sum(jnp.where(g0, 0.0, sq), axis=1, keepdims=True) * (2.0 / SSD_WIDTH)
    rs = jnp.where(g0, lax.rsqrt(ms0 + RMS_EPS), lax.rsqrt(ms1 + RMS_EPS))
    return hg, rs, g0


def _ssd_fwd(h, cwx, cbx, cwb, cbb, vec, nw, nb, s, name):
    n = SSD_CHUNK
    nc = s // n
    _, blk, halo, full = _ssd_specs(nc, False)

    def body(bc_ref, bch_ref, z_ref, xs_ref, xsh_ref, dt_ref, cwx_ref, cbx_ref, cwb_ref, cbb_ref, vec_ref, nw_ref,
             o_ref, hs_ref, h_scr, xe_ref, be_ref, y_scr):
        c = pl.program_id(1)

        @pl.when(c == 0)
        def _():
            h_scr[...] = jnp.zeros_like(h_scr)

        (_, _, x, bc, _, dt, a_row, acum, acum_t, row, col, _) = _ssd_prep(
            xs_ref, xsh_ref, bc_ref, bch_ref, dt_ref, cwx_ref, cbx_ref, cwb_ref, cbb_ref, vec_ref, xe_ref, be_ref, c == 0)
        hs_ref[...] = h_scr[...]
        heads = _ssd_core_fwd(x, bc, dt, acum, acum_t, a_row, vec_ref[2:3, :], hs_ref, col <= row)
        for hd in heads:
            sl = slice(hd["hh"] * HEAD_DIM, (hd["hh"] + 1) * HEAD_DIM)
            y_scr[:, sl] = hd["y"]
            h_scr[sl, :] = jnp.exp(hd["al"]) * hd["hp"] + _dot(hd["xdt"] * hd["w"], hd["bm"], TN)
        hg, rs, _ = _ssd_gate_norm(y_scr[...], z_ref[...], nw_ref[...])
        o_ref[...] = (hg * rs * nw_ref[...]).astype(o_ref.dtype)

    t = nb * s
    return pl.pallas_call(
        body, name=name, grid=(nb, nc),
        in_specs=[blk(512, H_BC), halo(512, H_BC), blk(384, H_Z), blk(384, H_XS), halo(384, H_XS), blk(128, H_DT),
                  full((4, 384)), full((1, 384)), full((4, 512)), full((1, 512)), full((SUBLANES, LANES)), full((1, 384))],
        out_specs=[pl.BlockSpec((n, SSD_WIDTH), lambda b, c: (b * nc + c, 0)),
                   pl.BlockSpec((None, SSD_WIDTH, SSD_STATE), lambda b, c: (b * nc + c, 0, 0))],
        out_shape=[jax.ShapeDtypeStruct((t, SSD_WIDTH), MXU_DTYPE),
                   jax.ShapeDtypeStruct((nb * nc, SSD_WIDTH, SSD_STATE), F32)],
        scratch_shapes=[pltpu.VMEM((SSD_WIDTH, SSD_STATE), F32), pltpu.VMEM((n + SUBLANES, 384), F32),
                        pltpu.VMEM((n + SUBLANES, 512), F32), pltpu.VMEM((n, SSD_WIDTH), F32)],
        compiler_params=_params(2),
    )(h, h, h, h, h, h, cwx, cbx, cwb, cbb, vec, nw)


def _ssd_bwd(h, hstate, dmix, cwx, cbx, cwb, cbb, vec, nw, nb, s, name):
    n = SSD_CHUNK
    nc = s // n
    cidx, blk, halo, full = _ssd_specs(nc, True)

    def body(bc_ref, bch_ref, z_ref, xs_ref, xsh_ref, dt_ref, hs_ref, do_ref, cwx_ref, cbx_ref, cwb_ref, cbb_ref,
             vec_ref, nw_ref, dz_ref, dxs_ref, dbc_ref, ddt_ref, gx_ref, gb_ref, gv_ref, gn_ref,
             dh_scr, xe_ref, be_ref, y_scr, dx_scr, dbc_scr, dxe_ref, dbe_ref, cx_ref, cb_ref):
        b = pl.program_id(0)
        c = pl.program_id(1)
        cc = nc - 1 - c

        @pl.when(jnp.logical_and(b == 0, c == 0))
        def _():
            gx_ref[...] = jnp.zeros_like(gx_ref)
            gb_ref[...] = jnp.zeros_like(gb_ref)
            gv_ref[...] = jnp.zeros_like(gv_ref)
            gn_ref[...] = jnp.zeros_like(gn_ref)

        @pl.when(c == 0)
        def _():
            dh_scr[...] = jnp.zeros_like(dh_scr)
            cx_ref[...] = jnp.zeros_like(cx_ref)
            cb_ref[...] = jnp.zeros_like(cb_ref)

        (pre_x, pre_bc, x, bc, dt_pre, dt, a_row, acum, acum_t, row, col, upper) = _ssd_prep(
            xs_ref, xsh_ref, bc_ref, bch_ref, dt_ref, cwx_ref, cbx_ref, cwb_ref, cbb_ref, vec_ref, xe_ref, be_ref, cc == 0)
        tri = col <= row
        d_row = vec_ref[2:3, :]
        heads = _ssd_core_fwd(x, bc, dt, acum, acum_t, a_row, d_row, hs_ref, tri)
        for hd in heads:
            y_scr[:, hd["hh"] * HEAD_DIM:(hd["hh"] + 1) * HEAD_DIM] = hd["y"]
        y = y_scr[...]
        z = z_ref[...]
        nwv = nw_ref[...]
        hg, rs, g0 = _ssd_gate_norm(y, z, nwv)
        do = do_ref[...]
        nrm = hg * rs
        gn_ref[...] += jnp.sum(do * nrm, axis=0, keepdims=True)
        dn = do * nwv
        dnn = dn * nrm
        mean0 = jnp.sum(jnp.where(g0, dnn, 0.0), axis=1, keepdims=True) * (2.0 / SSD_WIDTH)
        mean1 = jnp.sum(jnp.where(g0, 0.0, dnn), axis=1, keepdims=True) * (2.0 / SSD_WIDTH)
        dhg = rs * (dn - nrm * jnp.where(g0, mean0, mean1))
        dz_ref[...] = (dhg * y * _dsilu(z)).astype(dz_ref.dtype)
        dy = dhg * _silu(z)

        lane = lax.broadcasted_iota(jnp.int32, (n, LANES), 1)
        lane1 = lax.broadcasted_iota(jnp.int32, (1, LANES), 1)
        last_row = lax.broadcasted_iota(jnp.int32, (n, 1), 0) == n - 1
        dacum_col = jnp.zeros((n, LANES), F32)
        da_rowpart = jnp.zeros((n, LANES), F32)
        ddt = jnp.zeros((n, LANES), F32)
        dd_vec = jnp.zeros((1, LANES), F32)
        for hd in heads:
            sl = slice(hd["hh"] * HEAD_DIM, (hd["hh"] + 1) * HEAD_DIM)
            dyh = dy[:, sl]
            dhn = dh_scr[sl, :]
            hd.update(sl=sl, dyh=dyh, dhn=dhn, t1=_dot(dyh, hd["hp"]), dm=_dot(dyh, hd["xdt"], NT),
                      t2=_dot(hd["bm"], dhn, NT), mtdy=_dot(hd["m"], dyh, TN), xdhn=_dot(hd["xdt"], dhn),
                      dhp=_dot(dyh * hd["ea"], hd["cm"], TN))
        dgs, dbms, dcms = [], [], []
        for g in range(2):
            dg = jnp.zeros((n, n), F32)
            dbm = jnp.zeros((n, SSD_STATE), F32)
            dcm = jnp.zeros((n, SSD_STATE), F32)
            for hd in heads[3 * g:3 * g + 3]:
                hh, sl, dyh, dhn, t2 = hd["hh"], hd["sl"], hd["dyh"], hd["dhn"], hd["t2"]
                el = jnp.exp(hd["al"])
                dd_vec = dd_vec + jnp.where(lane1 == hh, jnp.sum(dyh * hd["xh"]), 0.0)
                dcm = dcm + hd["ea"] * hd["t1"]
                dg = dg + hd["dm"] * hd["dec"]
                e = hd["dm"] * hd["m"]
                dxdt = hd["mtdy"] + hd["w"] * t2
                dbm = dbm + hd["w"] * hd["xdhn"]
                dw_w = jnp.sum(hd["xdt"] * t2, axis=1, keepdims=True) * hd["w"]
                d_el = jnp.sum(dhn * hd["hp"])
                col_part = (jnp.sum(dyh * hd["yo"], axis=1, keepdims=True) + jnp.sum(e, axis=1, keepdims=True) - dw_w
                            + jnp.where(last_row, d_el * el + jnp.sum(dw_w), 0.0))
                dacum_col = dacum_col + jnp.where(lane == hh, col_part, 0.0)
                neg_colsum = -jnp.sum(e, axis=0, keepdims=True)
                rev = jnp.sum(jnp.where(row <= col, neg_colsum, 0.0), axis=1, keepdims=True)
                da_rowpart = da_rowpart + jnp.where(lane == hh, rev, 0.0)
                dh_scr[sl, :] = el * dhn + hd["dhp"]
                dx_scr[:, sl] = d_row[:, hh:hh + 1] * dyh + dxdt * hd["dth"]
                ddt = ddt + jnp.where(lane == hh, jnp.sum(dxdt * hd["xh"], axis=1, keepdims=True), 0.0)
            dgs.append(dg)
            dbms.append(dbm)
            dcms.append(dcm)
        for g in range(2):
            bm, cm = heads[3 * g]["bm"], heads[3 * g]["cm"]
            dbc_scr[:, g * SSD_STATE:(g + 1) * SSD_STATE] = dbms[g] + _dot(dgs[g], cm, TN)
            dbc_scr[:, 2 * SSD_STATE + g * SSD_STATE:2 * SSD_STATE + (g + 1) * SSD_STATE] = dcms[g] + _dot(dgs[g], bm)
        da_mat = _dot_exact01(dacum_col, upper, NN, x_left=False) + da_rowpart
        ddt = ddt + da_mat * a_row
        da_vec = jnp.sum(da_mat * dt, axis=0, keepdims=True)
        ddt_pre = jnp.where(lane < SSD_HEADS, ddt * _sigmoid(dt_pre), 0.0)
        ddt_ref[...] = ddt_pre.astype(ddt_ref.dtype)
        rid = lax.broadcasted_iota(jnp.int32, (SUBLANES, LANES), 0)
        gv_ref[...] += (jnp.where(rid == 0, jnp.sum(ddt_pre, axis=0, keepdims=True), 0.0)
                        + jnp.where(rid == 1, da_vec, 0.0) + jnp.where(rid == 2, dd_vec, 0.0))

        def conv_bwd(dpost, pre, w_ref, ext_ref, dext_ref, carry_ref, cur_ref, out_ref, g_ref, width):
            dco = dpost * _dsilu(pre)
            dext_ref[0:n, :] = dco
            dext_ref[n:n + SUBLANES, :] = carry_ref[...]
            out_ref[...] = (w_ref[3:4, :] * dco + w_ref[2:3, :] * dext_ref[pl.ds(1, n), :]
                            + w_ref[1:2, :] * dext_ref[pl.ds(2, n), :] + w_ref[0:1, :] * dext_ref[pl.ds(3, n), :]
                            ).astype(out_ref.dtype)
            carry_ref[...] = dco[0:SUBLANES, :]
            rid8 = lax.broadcasted_iota(jnp.int32, (SUBLANES, width), 0)
            acc = jnp.where(rid8 == 3, jnp.sum(dco * cur_ref[...], axis=0, keepdims=True), 0.0)
            for j in range(3):
                sh = ext_ref[pl.ds(SUBLANES - 3 + j, n), :]
                acc = acc + jnp.where(rid8 == j, jnp.sum(dco * sh, axis=0, keepdims=True), 0.0)
            acc = acc + jnp.where(rid8 == 4, jnp.sum(dco, axis=0, keepdims=True), 0.0)
            g_ref[...] += acc

        conv_bwd(dx_scr[...], pre_x, cwx_ref, xe_ref, dxe_ref, cx_ref, xs_ref, dxs_ref, gx_ref, 384)
        conv_bwd(dbc_scr[...], pre_bc, cwb_ref, be_ref, dbe_ref, cb_ref, bc_ref, dbc_ref, gb_ref, 512)

    t = nb * s
    rowblk = lambda width: pl.BlockSpec((n, width), lambda b, c: (b * nc + cidx(c), 0))
    return pl.pallas_call(
        body, name=name, grid=(nb, nc),
        in_specs=[blk(512, H_BC), halo(512, H_BC), blk(384, H_Z), blk(384, H_XS), halo(384, H_XS), blk(128, H_DT),
                  pl.BlockSpec((None, SSD_WIDTH, SSD_STATE), lambda b, c: (b * nc + cidx(c), 0, 0)),
                  pl.BlockSpec((n, SSD_WIDTH), lambda b, c: (b * nc + cidx(c), 0)),
                  full((4, 384)), full((1, 384)), full((4, 512)), full((1, 512)), full((SUBLANES, LANES)), full((1, 384))],
        out_specs=[rowblk(384), rowblk(384), rowblk(512), rowblk(128),
                   full((SUBLANES, 384)), full((SUBLANES, 512)), full((SUBLANES, LANES)), full((1, 384))],
        out_shape=[jax.ShapeDtypeStruct((t, 384), MXU_DTYPE), jax.ShapeDtypeStruct((t, 384), MXU_DTYPE),
                   jax.ShapeDtypeStruct((t, 512), MXU_DTYPE), jax.ShapeDtypeStruct((t, 128), MXU_DTYPE),
                   jax.ShapeDtypeStruct((SUBLANES, 384), F32), jax.ShapeDtypeStruct((SUBLANES, 512), F32),
                   jax.ShapeDtypeStruct((SUBLANES, LANES), F32), jax.ShapeDtypeStruct((1, 384), F32)],
        scratch_shapes=[pltpu.VMEM((SSD_WIDTH, SSD_STATE), F32), pltpu.VMEM((n + SUBLANES, 384), F32),
                        pltpu.VMEM((n + SUBLANES, 512), F32), pltpu.VMEM((n, SSD_WIDTH), F32),
                        pltpu.VMEM((n, 384), F32), pltpu.VMEM((n, 512), F32),
                        pltpu.VMEM((n + SUBLANES, 384), F32), pltpu.VMEM((n + SUBLANES, 512), F32),
                        pltpu.VMEM((SUBLANES, 384), F32), pltpu.VMEM((SUBLANES, 512), F32)],
        compiler_params=_params(2),
    )(h, h, h, h, h, h, hstate, dmix, cwx, cbx, cwb, cbb, vec, nw)


def _adamw_math(w, g, m, v):
    m = ADAM_B1 * m + (1.0 - ADAM_B1) * g
    v = ADAM_B2 * v + (1.0 - ADAM_B2) * (g * g)
    m_hat = m / (1.0 - ADAM_B1 ** ADAM_STEP)
    v_hat = v / (1.0 - ADAM_B2 ** ADAM_STEP)
    delta = -ADAM_LR * (m_hat / (jnp.sqrt(v_hat) + ADAM_EPS) + ADAM_WD * w)
    return delta, m, v


def _adamw(w, g, m, v, name, tr=256):
    rows, cols = w.shape
    tr = rows if rows <= tr else tr
    assert rows % tr == 0, (rows, tr)

    def body(w_ref, g_ref, m_ref, v_ref, d_ref, nm_ref, nv_ref):
        d, nm, nv = _adamw_math(w_ref[...], g_ref[...], m_ref[...], v_ref[...])
        d_ref[...] = d
        nm_ref[...] = nm
        nv_ref[...] = nv

    spec = pl.BlockSpec((tr, cols), lambda i: (i, 0))
    shp = jax.ShapeDtypeStruct((rows, cols), F32)
    return pl.pallas_call(body, name=name, grid=(rows // tr,), in_specs=[spec] * 4, out_specs=[spec] * 3,
                          out_shape=[shp] * 3, compiler_params=_params(1))(w, g, m, v)


def _sum8_layers(parts, name, tr):
    _, rows, cols = parts[0].shape
    assert rows % tr == 0
    nt = rows // tr

    def body(*refs):
        o_ref = refs[DEPTH]
        layer = pl.program_id(0)
        for l in range(DEPTH):
            @pl.when(layer == l)
            def _(l=l):
                acc = refs[l][0]
                for k in range(1, N_DEV):
                    acc = acc + refs[l][k]
                o_ref[...] = acc

    in_specs = [pl.BlockSpec((N_DEV, tr, cols), lambda a, i, l=l: (0, jnp.clip(i + (a - l) * nt, 0, nt - 1), 0))
                for l in range(DEPTH)]
    return pl.pallas_call(body, name=name, grid=(DEPTH, nt), in_specs=in_specs,
                          out_specs=pl.BlockSpec((None, tr, cols), lambda a, i: (a, i, 0)),
                          out_shape=jax.ShapeDtypeStruct((DEPTH, rows, cols), F32), compiler_params=_params(2))(*parts)


def _all_reduce_small(vec, name):
    rows, cols = vec.shape

    def body(x_ref, out_ref, gbuf, send_sems, recv_sems):
        x, y, c = lax.axis_index("x"), lax.axis_index("y"), lax.axis_index("c")
        me, sibling = (x, y, c), (x, y, 1 - c)
        chips = [(1 - x, y), (x, 1 - y), (1 - x, 1 - y)]

        def slot(px, py, pc):
            return gbuf.at[4 * px + 2 * py + pc]

        def copy(k, block, to, src=None):
            return pltpu.make_async_remote_copy(
                src_ref=slot(*block) if src is None else src, dst_ref=slot(*block),
                send_sem=send_sems.at[k], recv_sem=recv_sems.at[k], device_id=to, device_id_type=MESH_ID)

        first = [copy(0, me, sibling, src=x_ref)]
        first += [copy(1 + j, me, (*chip, c), src=x_ref) for j, chip in enumerate(chips)]
        for cp in first:
            cp.start()
        gbuf[4 * x + 2 * y + c] = x_ref[...]
        passed = [copy(4 + j, (*chip, c), sibling) for j, chip in enumerate(chips)]
        for j, chip in enumerate(chips):
            copy(1 + j, (*chip, c), me).wait_recv()
            passed[j].start()
        copy(0, sibling, me).wait_recv()
        for j, chip in enumerate(chips):
            copy(4 + j, (*chip, 1 - c), me).wait_recv()
        for cp in first + passed:
            cp.wait_send()
        acc = gbuf[0]
        for k in range(1, N_DEV):
            acc = acc + gbuf[k]
        out_ref[...] = acc

    return pl.pallas_call(
        body, name=name, out_shape=jax.ShapeDtypeStruct((rows, cols), F32),
        in_specs=[pl.BlockSpec(memory_space=pltpu.VMEM)], out_specs=pl.BlockSpec(memory_space=pltpu.VMEM),
        scratch_shapes=[pltpu.VMEM((N_DEV, rows, cols), F32), pltpu.SemaphoreType.DMA((7,)), pltpu.SemaphoreType.DMA((7,))],
        compiler_params=pltpu.CompilerParams(has_side_effects=True, vmem_limit_bytes=VMEM_LIMIT_BYTES),
    )(vec)


_COL_POOL, _COL_Z, _COL_XBC, _COL_DT, _COL_Q, _COL_K, _COL_V = 0, 256, 640, 1536, 1542, 1926, 2310
_H_SEGMENTS = ((_COL_XBC + SSD_WIDTH, 512), (_COL_POOL, 256), (_COL_Q, 384), (_COL_K, 384), (_COL_V, 384),
               (_COL_Z, 384), (_COL_XBC, 384), (_COL_DT, 6))


def _h_from_orig(w):
    parts = [w[..., o:o + n] for o, n in _H_SEGMENTS]
    pad = jnp.zeros(w.shape[:-1] + (H_COLS - IN_COLS,), w.dtype)
    return jnp.concatenate(parts + [pad], axis=-1)


def _h_to_orig(w):
    offs, o = {}, 0
    for orig, n in _H_SEGMENTS:
        offs[orig] = (o, n)
        o += n
    order = sorted(offs)
    return jnp.concatenate([w[..., offs[k][0]:offs[k][0] + offs[k][1]] for k in order], axis=-1)


def _interleave(w):
    lead = w.shape[:-1]
    nt = D_FF // GLU_TILE
    return jnp.swapaxes(w.reshape(lead + (2, nt, GLU_TILE)), -3, -2).reshape(lead + (2 * D_FF,))


def _deinterleave(w):
    lead = w.shape[:-1]
    nt = D_FF // GLU_TILE
    return jnp.swapaxes(w.reshape(lead + (nt, 2, GLU_TILE)), -3, -2).reshape(lead + (2 * D_FF,))


def _up_segments():
    segs = []
    for j in range(N_DEV):
        half, base = j // 4, UP_SHARD * (j % 4)
        c = base
        while c < base + UP_SHARD:
            t, r = divmod(c, GLU_TILE)
            n = min(GLU_TILE - r, base + UP_SHARD - c)
            segs.append((j, c - base, 2 * GLU_TILE * t + GLU_TILE * half + r, n))
            c += n
    return segs


def _up_to_interleaved(w, name, tr=256):
    def body(i_ref, o_ref):
        for j, src, dst, n in _up_segments():
            o_ref[:, dst:dst + n] = i_ref[j, :, src:src + n]

    return pl.pallas_call(
        body, name=name, grid=(D_MODEL // tr,), in_specs=[pl.BlockSpec((N_DEV, tr, UP_SHARD), lambda r: (0, r, 0))],
        out_specs=pl.BlockSpec((tr, 2 * D_FF), lambda r: (r, 0)),
        out_shape=jax.ShapeDtypeStruct((D_MODEL, 2 * D_FF), w.dtype), compiler_params=_params(1))(w)


def _up_from_interleaved(g, name, tr=128):
    def body(i_ref, o_ref):
        for j, src, dst, n in _up_segments():
            o_ref[j, :, src:src + n] = i_ref[:, dst:dst + n]

    return pl.pallas_call(
        body, name=name, grid=(D_MODEL // tr,), in_specs=[pl.BlockSpec((tr, 2 * D_FF), lambda r: (r, 0))],
        out_specs=pl.BlockSpec((N_DEV, tr, UP_SHARD), lambda r: (0, r, 0)),
        out_shape=jax.ShapeDtypeStruct((N_DEV, D_MODEL, UP_SHARD), g.dtype), compiler_params=_params(1))(g)


def _mix_rows_from_orig(w):
    return jnp.concatenate([w[256:640], w[640:1024], w[0:256]], axis=0)


def _mix_rows_to_orig(w):
    return jnp.concatenate([w[768:1024], w[0:384], w[384:768]], axis=0)


def _xbc_split(w):
    return w[..., :SSD_WIDTH], w[..., SSD_WIDTH:]


def _layer_fwd(x, p_l, wt, sp, nb, s, comm=None, target=None):
    h = _mm(x, wt["w_in"], "nn", F32, "mm_in", tm=1024, tn=1408)
    pool_out = _pool_fwd(h, wt["pool_bd"], sp["pool_scale"], nb, s, "pool_fwd")
    ssd_out, hstate = _ssd_fwd(h, sp["cwx"], sp["cbx"], sp["cwb"], sp["cbb"], sp["ssd_vec"], sp["ssd_norm_w"], nb, s, "ssd_fwd")
    sb_out, comm_out = _sb_fwd(h, nb, s, "sb_fwd" if comm is None else "sb_fwd_gather", comm)
    mixcat = jnp.concatenate([ssd_out, sb_out, pool_out], axis=1)
    x1, r1 = _mm_ln(mixcat, wt["w_out"], x, sp["ln1"], "mm_out_ln1", tm=1024, tk=1024)
    up = _mm(x1, wt["w_up"], "nn", F32, "mm_up", tm=1024, tn=1408)
    act = _glu_fwd(up, sp["ffn_cw"], sp["ffn_cb"], nb, s, "glu_fwd")
    gp = _mm(x1, wt["w_gate"], "nn", F32, "mm_gate", tm=1024, tn=1024)
    pp = _mm(p_l, wt["w_proj"], "nn", F32, "mm_proj", tm=2048, tn=1024)
    x2, r2, *loss = _mm_ln(act, wt["w_down"], x1, sp["ln2"], "mm_down_ln2" if target is None else "mm_down_ln2_loss",
                           gp=gp, pp=pp, target=target, tm=512, tk=1408)
    sv = dict(x=x, h=h, hstate=hstate, mixcat=mixcat, r1=r1, x1=x1, up=up, act=act, gp=gp, pp=pp, r2=r2)
    return x2, sv, comm_out, (loss[0] if loss else None)


def _layer_bwd(dx2, p_l, sv, wt, sp, nb, s, comm=None, tail_comm=None):
    dr2, dgp, dpp, st2 = _ln_bwd(sv["r2"], sp["ln2"], dx2, "ln2_bwd", gp=sv["gp"], pp=sv["pp"])
    g_down = _mm(sv["act"], dr2, "tn", F32, "wg_down", tm=1408, tn=1024, tk=512)
    dact = _mm(dr2, wt["w_down"], "nt", F32, "dg_down", tm=1024, tn=1408)
    dup, ffn_acc = _glu_bwd(sv["up"], dact, sp["ffn_cw"], sp["ffn_cb"], nb, s, "glu_bwd")
    g_up = _mm(sv["x1"], dup, "tn", F32, "wg_up", tm=1024, tn=2816, tk=512)
    g_gate = _mm(sv["x1"], dgp, "tn", F32, "wg_gate", tm=1024, tn=1024, tk=512)
    g_proj = _mm(p_l, dpp, "tn", F32, "wg_proj", tm=256, tn=1024, tk=512)
    t1 = _mm(dgp, wt["w_gate"], "nt", F32, "dg_gate", tm=1024, tn=1024, add=dr2, add_coef=ALPHA)
    dx1 = _mm(dup, wt["w_up"], "nt", F32, "dg_up", tm=1024, tn=1024, tk=1408, add=t1)
    dr1, st1 = _ln_bwd(sv["r1"], sp["ln1"], dx1, "ln1_bwd")
    g_out = _mm(sv["mixcat"], dr1, "tn", F32, "wg_out", tm=1024, tn=1024, tk=512)
    dmix = _mm(dr1, wt["w_out"], "nt", F32, "dg_out", tm=1024, tn=1024)
    du, g_pool_bd, g_pool_scale = _pool_bwd(sv["h"], dmix, wt["pool_bd"], sp["pool_scale"], nb, s, "pool_bwd")
    dz, dxs, dbc, ddt, gx, gb, gv, gn = _ssd_bwd(sv["h"], sv["hstate"], dmix, sp["cwx"], sp["cbx"], sp["cwb"], sp["cbb"],
                                                  sp["ssd_vec"], sp["ssd_norm_w"], nb, s, "ssd_bwd")
    ready = dict(w_out=g_out, ffn_w_up=g_up, ffn_w_down=g_down, ple_w_gate=g_gate, ple_w_proj=g_proj)
    job = comm(ready) if comm is not None else None
    dq, dk, dv, comm_out = _sb_bwd(sv["h"], dmix, nb, s, "sb_bwd" if job is None else "sb_bwd_x%d" % job["n_xfers"], job)
    dh = jnp.concatenate([dbc, du, dq, dk, dv, dz, dxs, ddt], axis=1)
    g_in = _mm(sv["x"], dh, "tn", F32, "wg_in", tm=1024, tn=2816, tk=512)
    tail_job = tail_comm(g_in) if tail_comm is not None else None
    dx = _mm(dh, wt["w_in"], "nt", F32, "dg_in" if tail_job is None else "dg_in_x", tm=1024, tn=1024, tk=1408, add=dr1,
             add_coef=ALPHA, comm=tail_job)
    dx, tail_out = dx if tail_job is not None else (dx, None)
    small = dict(
        pool_w=jnp.stack([g_pool_bd[HEAD_DIM * g:HEAD_DIM * (g + 1), HEAD_DIM * g:HEAD_DIM * (g + 1)] for g in range(4)]),
        pool_scale=g_pool_scale[0],
        ssd_conv_w=jnp.concatenate([gx[0:4], gb[0:4]], axis=1),
        ssd_conv_b=jnp.concatenate([gx[4], gb[4]], axis=0),
        ssd_dt_bias=gv[0, :SSD_HEADS],
        ssd_a_log=gv[1, :SSD_HEADS] * sp["ssd_vec"][1, :SSD_HEADS],
        ssd_d=gv[2, :SSD_HEADS],
        ssd_norm_w=gn[0],
        ln1_g=st1[0], ln1_b=st1[1], ln2_g=st2[0], ln2_b=st2[1],
        ffn_conv_w=_deinterleave(ffn_acc[0:3]),
        ffn_conv_b=_deinterleave(ffn_acc[3]),
    )
    return dx, dict(ready, w_in=g_in), small, comm_out, tail_out


def _layer_params(i, big, rep):
    pool_bd = jnp.zeros((POOL_WIDTH, POOL_WIDTH), F32)
    for g in range(4):
        pool_bd = lax.dynamic_update_slice(pool_bd, rep["pool_w"][i, g], (HEAD_DIM * g, HEAD_DIM * g))
    wt = dict(w_in=big["w_in"], w_out=big["w_out"], w_up=big["ffn_w_up"], w_down=big["ffn_w_down"],
              w_gate=big["ple_w_gate"], w_proj=big["ple_w_proj"], pool_bd=pool_bd.astype(MXU_DTYPE))
    cwx, cwb = _xbc_split(rep["ssd_conv_w"][i])
    cbx, cbb = _xbc_split(rep["ssd_conv_b"][i][None, :])
    vec = jnp.zeros((SUBLANES, LANES), F32)
    vec = vec.at[0, :SSD_HEADS].set(rep["ssd_dt_bias"][i])
    vec = vec.at[1, :SSD_HEADS].set(-jnp.exp(rep["ssd_a_log"][i]))
    vec = vec.at[2, :SSD_HEADS].set(rep["ssd_d"][i])
    sp = dict(pool_scale=rep["pool_scale"][i][None, :], cwx=cwx, cbx=cbx, cwb=cwb, cbb=cbb, ssd_vec=vec,
              ssd_norm_w=rep["ssd_norm_w"][i][None, :],
              ln1=jnp.stack([rep["ln1_g"][i], rep["ln1_b"][i]]), ln2=jnp.stack([rep["ln2_g"][i], rep["ln2_b"][i]]),
              ffn_cw=_interleave(rep["ffn_conv_w"][i]), ffn_cb=_interleave(rep["ffn_conv_b"][i][None, :]))
    return wt, sp


def _run_layers(x, p, target, big_w, rep, fwd_job=None, fwd_done=None, bwd_job=None, bwd_done=None, tail_job=None,
                tail_done=None):
    nb, s, d = x.shape
    t = nb * s
    xf = x.reshape(t, d)
    saved, params = [], []
    for i in range(DEPTH):
        wt, sp = _layer_params(i, big_w[i], rep)
        params.append((wt, sp))
        job = fwd_job(i) if fwd_job is not None else None
        xf, sv, res, loss = _layer_fwd(xf, p[i].reshape(t, PLE_DIM), wt, sp, nb, s, job,
                                       target.reshape(t, d) if i == DEPTH - 1 else None)
        if job is not None:
            fwd_done(i, res)
        saved.append(sv)
    dy = xf
    bigs, smalls = [None] * DEPTH, [None] * DEPTH
    for i in reversed(range(DEPTH)):
        wt, sp = params[i]
        job = (lambda ready, i=i: bwd_job(i, bigs, ready)) if bwd_job is not None else None
        dy, bigs[i], smalls[i], res, tail = _layer_bwd(dy, p[i].reshape(t, PLE_DIM), saved[i], wt, sp, nb, s, job,
                                                         tail_job if i == 0 else None)
        if job is not None:
            bwd_done(i, res)
        if tail is not None:
            tail_done(tail)
    return loss, dy.reshape(nb, s, d), bigs, smalls


def _local_step(x, p, target, full, rep):
    return _run_layers(x, p, target, [{n: full[n][i] for n in full} for i in range(DEPTH)], rep)


BIG = ("w_in", "w_out", "ffn_w_up", "ffn_w_down", "ple_w_gate", "ple_w_proj")
SMALL_REPLICATED = ("pool_w", "pool_scale", "ssd_conv_b", "ssd_dt_bias", "ssd_a_log", "ssd_d", "ssd_norm_w",
                    "ln1_g", "ln1_b", "ffn_conv_b", "ln2_g", "ln2_b")
SMALL_SHARDED = ("ssd_conv_w", "ffn_conv_w")
WEIGHTS = ("w_in", "pool_w", "pool_scale", "ssd_conv_w", "ssd_conv_b", "ssd_dt_bias", "ssd_a_log", "ssd_d", "ssd_norm_w",
           "w_out", "ln1_g", "ln1_b", "ffn_w_up", "ffn_conv_w", "ffn_conv_b", "ffn_w_down", "ln2_g", "ln2_b",
           "ple_w_gate", "ple_w_proj")
SUM_BLOCK_BYTES = 3 * 1024 * 1024


def _to_rows(a, cols):
    f = a.reshape(-1)
    pad = (-f.shape[0]) % cols
    if pad:
        f = jnp.concatenate([f, jnp.zeros((pad,), f.dtype)])
    return f.reshape(-1, cols)


def _pack_rows(arrs, cols, row_mult):
    rows = [_to_rows(a, cols) for a in arrs]
    flat = jnp.concatenate(rows, axis=0)
    pad = (-flat.shape[0]) % row_mult
    if pad:
        flat = jnp.concatenate([flat, jnp.zeros((pad, cols), flat.dtype)], axis=0)
    return flat


def _unpack_rows(flat, shapes, cols):
    out, r = [], 0
    for shp in shapes:
        n = 1
        for v in shp:
            n *= v
        nr = -(-n // cols)
        out.append(flat[r:r + nr].reshape(-1)[:n].reshape(shp))
        r += nr
    return out


def kernel(x, p, w_in, pool_w, pool_scale, ssd_conv_w, ssd_conv_b, ssd_dt_bias, ssd_a_log, ssd_d, ssd_norm_w, w_out, ln1_g, ln1_b, ffn_w_up, ffn_conv_w, ffn_conv_b, ffn_w_down, ln2_g, ln2_b, ple_w_gate, ple_w_proj, loss_target, m_w_in, m_pool_w, m_pool_scale, m_ssd_conv_w, m_ssd_conv_b, m_ssd_dt_bias, m_ssd_a_log, m_ssd_d, m_ssd_norm_w, m_w_out, m_ln1_g, m_ln1_b, m_ffn_w_up, m_ffn_conv_w, m_ffn_conv_b, m_ffn_w_down, m_ln2_g, m_ln2_b, m_ple_w_gate, m_ple_w_proj, v_w_in, v_pool_w, v_pool_scale, v_ssd_conv_w, v_ssd_conv_b, v_ssd_dt_bias, v_ssd_a_log, v_ssd_d, v_ssd_norm_w, v_w_out, v_ln1_g, v_ln1_b, v_ffn_w_up, v_ffn_conv_w, v_ffn_conv_b, v_ffn_w_down, v_ln2_g, v_ln2_b, v_ple_w_gate, v_ple_w_proj):
    wts = dict(w_in=w_in, pool_w=pool_w, pool_scale=pool_scale, ssd_conv_w=ssd_conv_w, ssd_conv_b=ssd_conv_b,
               ssd_dt_bias=ssd_dt_bias, ssd_a_log=ssd_a_log, ssd_d=ssd_d, ssd_norm_w=ssd_norm_w, w_out=w_out, ln1_g=ln1_g,
               ln1_b=ln1_b, ffn_w_up=ffn_w_up, ffn_conv_w=ffn_conv_w, ffn_conv_b=ffn_conv_b, ffn_w_down=ffn_w_down,
               ln2_g=ln2_g, ln2_b=ln2_b, ple_w_gate=ple_w_gate, ple_w_proj=ple_w_proj)
    mom_m = dict(w_in=m_w_in, pool_w=m_pool_w, pool_scale=m_pool_scale, ssd_conv_w=m_ssd_conv_w, ssd_conv_b=m_ssd_conv_b,
                 ssd_dt_bias=m_ssd_dt_bias, ssd_a_log=m_ssd_a_log, ssd_d=m_ssd_d, ssd_norm_w=m_ssd_norm_w, w_out=m_w_out,
                 ln1_g=m_ln1_g, ln1_b=m_ln1_b, ffn_w_up=m_ffn_w_up, ffn_conv_w=m_ffn_conv_w, ffn_conv_b=m_ffn_conv_b,
                 ffn_w_down=m_ffn_w_down, ln2_g=m_ln2_g, ln2_b=m_ln2_b, ple_w_gate=m_ple_w_gate, ple_w_proj=m_ple_w_proj)
    mom_v = dict(w_in=v_w_in, pool_w=v_pool_w, pool_scale=v_pool_scale, ssd_conv_w=v_ssd_conv_w, ssd_conv_b=v_ssd_conv_b,
                 ssd_dt_bias=v_ssd_dt_bias, ssd_a_log=v_ssd_a_log, ssd_d=v_ssd_d, ssd_norm_w=v_ssd_norm_w, w_out=v_w_out,
                 ln1_g=v_ln1_g, ln1_b=v_ln1_b, ffn_w_up=v_ffn_w_up, ffn_conv_w=v_ffn_conv_w, ffn_conv_b=v_ffn_conv_b,
                 ffn_w_down=v_ffn_w_down, ln2_g=v_ln2_g, ln2_b=v_ln2_b, ple_w_gate=v_ple_w_gate, ple_w_proj=v_ple_w_proj)
    me = 4 * lax.axis_index("x") + 2 * lax.axis_index("y") + lax.axis_index("c")

    def layer_shards(i):
        sh = {n: wts[n][i].astype(MXU_DTYPE) for n in BIG}
        sh["w_in"] = _h_from_orig(wts["w_in"][i]).astype(MXU_DTYPE)
        return sh

    def gathered_weights(res):
        big = dict(zip(BIG, res[:len(BIG)]))
        big["ffn_w_up"] = _up_to_interleaved(big["ffn_w_up"], "up_to_interleaved")
        return big

    res0 = _gather_call_two_level(_gather_job(layer_shards(0), [wts[n] for n in SMALL_SHARDED]), "gather_layer0")
    big_w = [gathered_weights(res0)] + [None] * (DEPTH - 1)
    rep = {n: wts[n] for n in SMALL_REPLICATED}
    for n, g in zip(SMALL_SHARDED, res0[len(BIG):]):
        rep[n] = jnp.transpose(g, (1, 2, 0, 3)).reshape(g.shape[1], g.shape[2], N_DEV * g.shape[3])

    def fwd_job(i):
        return _gather_job(layer_shards(i + 1)) if i + 1 < DEPTH else None

    def fwd_done(i, res):
        big_w[i + 1] = gathered_weights(res)

    received = [dict() for _ in range(DEPTH)]
    carried = ("w_out", "ffn_w_up", "ffn_w_down", "ple_w_gate", "ple_w_proj")

    def bwd_items(i, bigs, ready):
        items = [(i, n, ready[n]) for n in carried] + ([(i + 1, "w_in", bigs[i + 1]["w_in"])] if i + 1 < DEPTH else [])
        return [(l, n, _up_from_interleaved(g, "up_from_interleaved") if n == "ffn_w_up" else g) for l, n, g in items]

    pending = {}

    def bwd_job(i, bigs, ready):
        pending[i] = bwd_items(i, bigs, ready)
        return _exchange_job([(n, g) for _, n, g in pending[i]])

    def bwd_done(i, res):
        for (l, n, _), r in zip(pending[i], res):
            received[l][n] = r

    def tail_done(res):
        received[0]["w_in"] = res[0]

    loss_loc, grad_x, bigs, smalls = _run_layers(x, p, loss_target, big_w, rep, fwd_job, fwd_done, bwd_job, bwd_done,
                                                 lambda g_in: _exchange_job([("w_in", g_in)]), tail_done)

    grads = {}
    for n in BIG:
        parts = [received[i][n] for i in range(DEPTH)]
        _, rows, cols = parts[0].shape
        tr = next(t for t in (256, 128, 64, 32, 16, 8) if rows % t == 0 and N_DEV * t * cols * 4 <= SUM_BLOCK_BYTES)
        g = _sum8_layers(parts, "sum_" + n, tr)
        grads[n] = _h_to_orig(g) if n == "w_in" else g
    small_names = SMALL_REPLICATED + SMALL_SHARDED
    small_full_shapes = [rep[n].shape for n in small_names]
    small_vec = _pack_rows([jnp.stack([smalls[i][n] for i in range(DEPTH)]) for n in small_names] + [loss_loc[0, :1]],
                           LANES, SUBLANES)
    small_sum = _all_reduce_small(small_vec, "allreduce_small")
    small_out = _unpack_rows(small_sum, small_full_shapes + [(1,)], LANES)
    loss = small_out[-1][0]
    for n, g in zip(small_names, small_out[:-1]):
        if n in SMALL_SHARDED:
            width = wts[n].shape[-1]
            g = lax.dynamic_slice_in_dim(g, me * width, width, axis=g.ndim - 1)
        grads[n] = g

    delta, new_m, new_v = {}, {}, {}
    for n in BIG:
        shp = wts[n].shape
        two_d = lambda a: a.reshape(-1, shp[-1])
        tr = {"w_in": 128, "ffn_w_down": DOWN_SHARD}.get(n, 256)
        d_, m_, v_ = _adamw(two_d(wts[n]), two_d(grads[n]), two_d(mom_m[n]), two_d(mom_v[n]), "adamw_" + n, tr=tr)
        delta[n], new_m[n], new_v[n] = d_.reshape(shp), m_.reshape(shp), v_.reshape(shp)
    packs = [_pack_rows([src[n] for n in small_names], LANES, SUBLANES) for src in (wts, grads, mom_m, mom_v)]
    outs = _adamw(*packs, "adamw_small", tr=packs[0].shape[0])
    shapes = [wts[n].shape for n in small_names]
    for dst, flat in zip((delta, new_m, new_v), outs):
        for n, a in zip(small_names, _unpack_rows(flat, shapes, LANES)):
            dst[n] = a
    return (loss, grad_x, *[grads[n] for n in WEIGHTS], *[delta[n] for n in WEIGHTS],
            *[new_m[n] for n in WEIGHTS], *[new_v[n] for n in WEIGHTS])
```
